```python
import math
import jax, jax.numpy as jnp
from jax import lax
import numpy as np

D_MODEL = 2048
BATCH = 8
SEQ = 2048
DEPTH = 2

EPS = 1e-6
N_BRANCH = 3
POOL_WINDOWS = (2, 4, 8, 16)
POOL_WIDTH = D_MODEL // 4
POOL_GROUP = POOL_WIDTH // len(POOL_WINDOWS)
SB_HEAD_DIM = 128
SB_WIDTH = 3 * D_MODEL // 8
SB_HEADS = SB_WIDTH // SB_HEAD_DIM
SB_BLOCK = 128
GDN_HEAD_DIM = 128
GDN_WIDTH = 3 * D_MODEL // 8
GDN_HEADS = GDN_WIDTH // GDN_HEAD_DIM
GDN_CONV = 4
GDN_CHUNK = 64
D_FF = 4 * D_MODEL
IN_SIZES = (POOL_WIDTH, 3 * SB_WIDTH, 3 * GDN_WIDTH, GDN_WIDTH, GDN_HEADS, GDN_HEADS, N_BRANCH * D_MODEL)
N_IN = sum(IN_SIZES)

kernel_name = 'hybrid_pool_stickbreak_gdn_block'


def rms_norm(x, gain):
    x32 = x.astype(jnp.float32)
    y = x32 * lax.rsqrt(jnp.mean(x32 * x32, axis=-1, keepdims=True) + EPS)
    return (y * gain.astype(jnp.float32)).astype(x.dtype)


def l2_normalize(x):
    return x * lax.rsqrt(jnp.sum(x * x, axis=-1, keepdims=True) + EPS)


def pool_mixer(p, w_group, scale):
    b, s, _ = p.shape
    p32 = p.astype(jnp.float32)
    csum = jnp.cumsum(p32, axis=1)
    n_seen = jnp.arange(1, s + 1, dtype=jnp.float32)[None, :, None]
    groups = []
    for g, w in enumerate(POOL_WINDOWS):
        sl = slice(g * POOL_GROUP, (g + 1) * POOL_GROUP)
        cg = csum[..., sl]
        lagged = jnp.pad(cg[:, :s - w], ((0, 0), (w, 0), (0, 0)))
        groups.append((cg - lagged) / jnp.minimum(n_seen, w) - p32[..., sl])
    d = jnp.stack(groups, axis=2)
    y = jnp.einsum('bsgc,gcd->bsgd', d, w_group.astype(jnp.float32)).reshape(b, s, POOL_WIDTH)
    return (y * scale.astype(jnp.float32)).astype(p.dtype)


def stick_breaking_attention(q, k, v):
    b, s, h, dh = q.shape
    q32 = q.astype(jnp.float32) * (dh ** -0.5)
    k32 = k.astype(jnp.float32)
    v32 = v.astype(jnp.float32)
    outs = []
    for start in range(0, s, SB_BLOCK):
        end = start + SB_BLOCK
        z = jnp.einsum('bqhd,bkhd->bhqk', q32[:, start:end], k32[:, :end])
        mask = jnp.arange(end)[None, :] < jnp.arange(start, end)[:, None]
        log_stay = jnp.where(mask, jax.nn.log_sigmoid(-z), 0.0)
        log_later = lax.cumsum(log_stay, axis=3, reverse=True) - log_stay
        a = jnp.where(mask, jnp.exp(jax.nn.log_sigmoid(z) + log_later), 0.0)
        outs.append(jnp.einsum('bhqk,bkhd->bqhd', a, v32[:, :end]))
    return jnp.concatenate(outs, axis=1).astype(q.dtype)


def short_causal_conv(x, w):
    k, c = w.shape
    y = lax.conv_general_dilated(x, w[:, None, :], window_strides=(1,), padding=[(k - 1, 0)],
                                 dimension_numbers=('NWC', 'WIO', 'NWC'), feature_group_count=c)
    return jax.nn.silu(y)


def to_chunks(t):
    b, s, h = t.shape[:3]
    t = t.reshape((b, s // GDN_CHUNK, GDN_CHUNK, h) + t.shape[3:])
    return jnp.moveaxis(t, 3, 1)


def gated_delta_rule(q, k, v, log_alpha, beta):
    b, s, h, dk = q.shape
    dv = v.shape[-1]
    qc, kc, vc = to_chunks(q), to_chunks(k), to_chunks(v)
    g = jnp.cumsum(to_chunks(log_alpha), axis=-1)
    bc = to_chunks(beta)
    incl = jnp.tril(jnp.ones((GDN_CHUNK, GDN_CHUNK), dtype=bool))
    strict = jnp.tril(jnp.ones((GDN_CHUNK, GDN_CHUNK), dtype=bool), k=-1)
    diff = g[..., :, None] - g[..., None, :]
    gamma = jnp.where(incl, jnp.exp(jnp.where(incl, diff, 0.0)), 0.0)
    kk = jnp.einsum('bhncd,bhnmd->bhncm', kc, kc)
    lower = jnp.where(strict, bc[..., :, None] * kk * gamma, 0.0)
    unit_lower = lower + jnp.eye(GDN_CHUNK, dtype=lower.dtype)
    rhs = jnp.concatenate([vc * bc[..., None], kc * (bc * jnp.exp(g))[..., None]], axis=-1)
    sol = lax.linalg.triangular_solve(unit_lower, rhs, left_side=True, lower=True, unit_diagonal=True)
    u, w = sol[..., :dv], sol[..., dv:]
    qk = jnp.einsum('bhncd,bhnmd->bhncm', qc, kc) * gamma
    q_dec = qc * jnp.exp(g)[..., None]
    k_dec = kc * jnp.exp(g[..., -1:] - g)[..., None]
    chunk_decay = jnp.exp(g[..., -1])

    def step(state, inp):
        u_n, w_n, qk_n, qd_n, kd_n, dec_n = inp
        v_new = u_n - jnp.einsum('bhck,bhkv->bhcv', w_n, state)
        o_n = jnp.einsum('bhck,bhkv->bhcv', qd_n, state) + jnp.einsum('bhcm,bhmv->bhcv', qk_n, v_new)
        state = state * dec_n[..., None, None] + jnp.einsum('bhck,bhcv->bhkv', kd_n, v_new)
        return state, o_n

    xs = tuple(jnp.moveaxis(t, 2, 0) for t in (u, w, qk, q_dec, k_dec, chunk_decay))
    state0 = jnp.zeros((b, h, dk, dv), jnp.float32)
    _, o = lax.scan(step, state0, xs)
    return o.transpose(1, 0, 3, 2, 4).reshape(b, s, h, dv)


def gdn_mixer(qkv, z, a, bg, conv_w, a_log, dt_bias, norm_gain):
    b, s, _ = qkv.shape
    qkv = short_causal_conv(qkv, conv_w).astype(jnp.float32)
    q, k, v = jnp.split(qkv, 3, axis=-1)
    q = l2_normalize(q.reshape(b, s, GDN_HEADS, GDN_HEAD_DIM)) * (GDN_HEAD_DIM ** -0.5)
    k = l2_normalize(k.reshape(b, s, GDN_HEADS, GDN_HEAD_DIM))
    v = v.reshape(b, s, GDN_HEADS, GDN_HEAD_DIM)
    log_alpha = -jnp.exp(a_log.astype(jnp.float32)) * jax.nn.softplus(a.astype(jnp.float32) + dt_bias.astype(jnp.float32))
    beta = jax.nn.sigmoid(bg.astype(jnp.float32))
    o = gated_delta_rule(q, k, v, log_alpha, beta)
    zh = z.astype(jnp.float32).reshape(b, s, GDN_HEADS, GDN_HEAD_DIM)
    o = rms_norm(o, norm_gain) * jax.nn.silu(zh)
    return o.reshape(b, s, GDN_WIDTH).astype(z.dtype)


def hybrid_mixer(u, w_in, pool_w, pool_scale, gdn_conv, gdn_a_log, gdn_dt_bias, gdn_norm,
                 w_pool_up, w_sb_up, w_gdn_up, w_out):
    b, s, _ = u.shape
    proj = u @ w_in
    offsets = [int(o) for o in np.cumsum(IN_SIZES)[:-1]]
    p, sb_qkv, gdn_qkv, gdn_z, gdn_a, gdn_b, gates = jnp.split(proj, offsets, axis=-1)
    y_pool = pool_mixer(p, pool_w, pool_scale)
    sq, sk, sv = (t.reshape(b, s, SB_HEADS, SB_HEAD_DIM) for t in jnp.split(sb_qkv, 3, axis=-1))
    y_sb = stick_breaking_attention(sq, sk, sv).reshape(b, s, SB_WIDTH)
    y_gdn = gdn_mixer(gdn_qkv, gdn_z, gdn_a, gdn_b, gdn_conv, gdn_a_log, gdn_dt_bias, gdn_norm)
    g_pool, g_sb, g_gdn = jnp.split(jax.nn.sigmoid(gates), N_BRANCH, axis=-1)
    merged = g_pool * (y_pool @ w_pool_up) + g_sb * (y_sb @ w_sb_up) + g_gdn * (y_gdn @ w_gdn_up)
    return merged @ w_out


def squared_relu_mlp(u, w_ff1, w_ff2):
    h = jax.nn.relu(u @ w_ff1)
    return (h * h) @ w_ff2


def _fwd_setup_inputs(seed: int = 0) -> dict:
    key = jax.random.key(seed)
    ks = jax.random.split(key, 20)
    f32 = jnp.float32

    def nrm(k, shape, fan_in):
        return jax.random.normal(k, shape, f32) * (fan_in ** -0.5)

    def gain(k, shape):
        return 1.0 + 0.02 * jax.random.normal(k, shape, f32)

    dt = jnp.exp(jax.random.uniform(ks[7], (DEPTH, GDN_HEADS), f32, math.log(1e-3), math.log(1e-1)))
    return {
        'x': jax.random.normal(ks[0], (BATCH, SEQ, D_MODEL), f32),
        'attn_norm': gain(ks[1], (DEPTH, D_MODEL)),
        'w_in': nrm(ks[2], (DEPTH, D_MODEL, N_IN), D_MODEL),
        'pool_w': nrm(ks[3], (DEPTH, len(POOL_WINDOWS), POOL_GROUP, POOL_GROUP), POOL_GROUP),
        'pool_scale': gain(ks[4], (DEPTH, POOL_WIDTH)),
        'gdn_conv': nrm(ks[5], (DEPTH, GDN_CONV, 3 * GDN_WIDTH), GDN_CONV),
        'gdn_a_log': jnp.log(jax.random.uniform(ks[6], (DEPTH, GDN_HEADS), f32, 1.0, 16.0)),
        'gdn_dt_bias': dt + jnp.log(-jnp.expm1(-dt)),
        'gdn_norm': gain(ks[8], (DEPTH, GDN_HEAD_DIM)),
        'w_pool_up': nrm(ks[9], (DEPTH, POOL_WIDTH, D_MODEL), POOL_WIDTH),
        'w_sb_up': nrm(ks[10], (DEPTH, SB_WIDTH, D_MODEL), SB_WIDTH),
        'w_gdn_up': nrm(ks[11], (DEPTH, GDN_WIDTH, D_MODEL), GDN_WIDTH),
        'w_out': nrm(ks[12], (DEPTH, D_MODEL, D_MODEL), D_MODEL),
        'mlp_norm': gain(ks[13], (DEPTH, D_MODEL)),
        'w_ff1': nrm(ks[14], (DEPTH, D_MODEL, D_FF), D_MODEL),
        'w_ff2': nrm(ks[15], (DEPTH, D_FF, D_MODEL), D_FF),
        'final_norm': gain(ks[16], (D_MODEL,)),
    }


def _fwd_reference(x, attn_norm, w_in, pool_w, pool_scale, gdn_conv, gdn_a_log, gdn_dt_bias, gdn_norm,
              w_pool_up, w_sb_up, w_gdn_up, w_out, mlp_norm, w_ff1, w_ff2, final_norm):
    for l in range(DEPTH):
        u = rms_norm(x, attn_norm[l])
        x = x + hybrid_mixer(u, w_in[l], pool_w[l], pool_scale[l], gdn_conv[l], gdn_a_log[l],
                             gdn_dt_bias[l], gdn_norm[l], w_pool_up[l], w_sb_up[l], w_gdn_up[l], w_out[l])
        x = x + squared_relu_mlp(rms_norm(x, mlp_norm[l]), w_ff1[l], w_ff2[l])
    return rms_norm(x, final_norm)


import jax as _jax
import jax.numpy as _jnp

TWIN_FORMAT = 'train_step'
FWD_PARAMS = ['x', 'attn_norm', 'w_in', 'pool_w', 'pool_scale', 'gdn_conv', 'gdn_a_log', 'gdn_dt_bias', 'gdn_norm', 'w_pool_up', 'w_sb_up', 'w_gdn_up', 'w_out', 'mlp_norm', 'w_ff1', 'w_ff2', 'final_norm']
TWIN_WEIGHTS = ['attn_norm', 'w_in', 'pool_w', 'pool_scale', 'gdn_conv', 'gdn_a_log', 'gdn_dt_bias', 'gdn_norm', 'w_pool_up', 'w_sb_up', 'w_gdn_up', 'w_out', 'mlp_norm', 'w_ff1', 'w_ff2', 'final_norm']
TWIN_DIFF_INPUT = 'x'
TWIN_INPUTS = ['x', 'attn_norm', 'w_in', 'pool_w', 'pool_scale', 'gdn_conv', 'gdn_a_log', 'gdn_dt_bias', 'gdn_norm', 'w_pool_up', 'w_sb_up', 'w_gdn_up', 'w_out', 'mlp_norm', 'w_ff1', 'w_ff2', 'final_norm', 'loss_target', 'm_attn_norm', 'm_w_in', 'm_pool_w', 'm_pool_scale', 'm_gdn_conv', 'm_gdn_a_log', 'm_gdn_dt_bias', 'm_gdn_norm', 'm_w_pool_up', 'm_w_sb_up', 'm_w_gdn_up', 'm_w_out', 'm_mlp_norm', 'm_w_ff1', 'm_w_ff2', 'm_final_norm', 'v_attn_norm', 'v_w_in', 'v_pool_w', 'v_pool_scale', 'v_gdn_conv', 'v_gdn_a_log', 'v_gdn_dt_bias', 'v_gdn_norm', 'v_w_pool_up', 'v_w_sb_up', 'v_w_gdn_up', 'v_w_out', 'v_mlp_norm', 'v_w_ff1', 'v_w_ff2', 'v_final_norm']
TWIN_OUTPUTS = ['loss', 'grad_x', 'grad_attn_norm', 'grad_w_in', 'grad_pool_w', 'grad_pool_scale', 'grad_gdn_conv', 'grad_gdn_a_log', 'grad_gdn_dt_bias', 'grad_gdn_norm', 'grad_w_pool_up', 'grad_w_sb_up', 'grad_w_gdn_up', 'grad_w_out', 'grad_mlp_norm', 'grad_w_ff1', 'grad_w_ff2', 'grad_final_norm', 'delta_attn_norm', 'delta_w_in', 'delta_pool_w', 'delta_pool_scale', 'delta_gdn_conv', 'delta_gdn_a_log', 'delta_gdn_dt_bias', 'delta_gdn_norm', 'delta_w_pool_up', 'delta_w_sb_up', 'delta_w_gdn_up', 'delta_w_out', 'delta_mlp_norm', 'delta_w_ff1', 'delta_w_ff2', 'delta_final_norm', 'new_m_attn_norm', 'new_m_w_in', 'new_m_pool_w', 'new_m_pool_scale', 'new_m_gdn_conv', 'new_m_gdn_a_log', 'new_m_gdn_dt_bias', 'new_m_gdn_norm', 'new_m_w_pool_up', 'new_m_w_sb_up', 'new_m_w_gdn_up', 'new_m_w_out', 'new_m_mlp_norm', 'new_m_w_ff1', 'new_m_w_ff2', 'new_m_final_norm', 'new_v_attn_norm', 'new_v_w_in', 'new_v_pool_w', 'new_v_pool_scale', 'new_v_gdn_conv', 'new_v_gdn_a_log', 'new_v_gdn_dt_bias', 'new_v_gdn_norm', 'new_v_w_pool_up', 'new_v_w_sb_up', 'new_v_w_gdn_up', 'new_v_w_out', 'new_v_mlp_norm', 'new_v_w_ff1', 'new_v_w_ff2', 'new_v_final_norm']
TWIN_LEAF_KINDS = {'loss': 'loss', 'grad_x': 'grad_x', 'grad_attn_norm': 'grad_w', 'grad_w_in': 'grad_w', 'grad_pool_w': 'grad_w', 'grad_pool_scale': 'grad_w', 'grad_gdn_conv': 'grad_w', 'grad_gdn_a_log': 'grad_w', 'grad_gdn_dt_bias': 'grad_w', 'grad_gdn_norm': 'grad_w', 'grad_w_pool_up': 'grad_w', 'grad_w_sb_up': 'grad_w', 'grad_w_gdn_up': 'grad_w', 'grad_w_out': 'grad_w', 'grad_mlp_norm': 'grad_w', 'grad_w_ff1': 'grad_w', 'grad_w_ff2': 'grad_w', 'grad_final_norm': 'grad_w', 'delta_attn_norm': 'delta_w', 'delta_w_in': 'delta_w', 'delta_pool_w': 'delta_w', 'delta_pool_scale': 'delta_w', 'delta_gdn_conv': 'delta_w', 'delta_gdn_a_log': 'delta_w', 'delta_gdn_dt_bias': 'delta_w', 'delta_gdn_norm': 'delta_w', 'delta_w_pool_up': 'delta_w', 'delta_w_sb_up': 'delta_w', 'delta_w_gdn_up': 'delta_w', 'delta_w_out': 'delta_w', 'delta_mlp_norm': 'delta_w', 'delta_w_ff1': 'delta_w', 'delta_w_ff2': 'delta_w', 'delta_final_norm': 'delta_w', 'new_m_attn_norm': 'new_m', 'new_m_w_in': 'new_m', 'new_m_pool_w': 'new_m', 'new_m_pool_scale': 'new_m', 'new_m_gdn_conv': 'new_m', 'new_m_gdn_a_log': 'new_m', 'new_m_gdn_dt_bias': 'new_m', 'new_m_gdn_norm': 'new_m', 'new_m_w_pool_up': 'new_m', 'new_m_w_sb_up': 'new_m', 'new_m_w_gdn_up': 'new_m', 'new_m_w_out': 'new_m', 'new_m_mlp_norm': 'new_m', 'new_m_w_ff1': 'new_m', 'new_m_w_ff2': 'new_m', 'new_m_final_norm': 'new_m', 'new_v_attn_norm': 'new_v', 'new_v_w_in': 'new_v', 'new_v_pool_w': 'new_v', 'new_v_pool_scale': 'new_v', 'new_v_gdn_conv': 'new_v', 'new_v_gdn_a_log': 'new_v', 'new_v_gdn_dt_bias': 'new_v', 'new_v_gdn_norm': 'new_v', 'new_v_w_pool_up': 'new_v', 'new_v_w_sb_up': 'new_v', 'new_v_w_gdn_up': 'new_v', 'new_v_w_out': 'new_v', 'new_v_mlp_norm': 'new_v', 'new_v_w_ff1': 'new_v', 'new_v_w_ff2': 'new_v', 'new_v_final_norm': 'new_v'}


def _forward(args):
    return _fwd_reference(*[args[k] for k in FWD_PARAMS])


def _output_shape():
    out = _jax.eval_shape(lambda: _forward(_fwd_setup_inputs(0)))
    return out.shape, out.dtype

N_MICROBATCH = 1
ADAM_LR = 0.001
ADAM_B1 = 0.9
ADAM_B2 = 0.999
ADAM_EPS = 1e-08
ADAM_WD = 0.01
ADAM_STEP = 10
PER_EXAMPLE_BATCH_AXIS = {'x': 0, 'loss_target': 0}
SHARED_INPUTS = []
_WEIGHT_DTYPES = {'attn_norm': _jnp.float32, 'w_in': _jnp.float32, 'pool_w': _jnp.float32, 'pool_scale': _jnp.float32, 'gdn_conv': _jnp.float32, 'gdn_a_log': _jnp.float32, 'gdn_dt_bias': _jnp.float32, 'gdn_norm': _jnp.float32, 'w_pool_up': _jnp.float32, 'w_sb_up': _jnp.float32, 'w_gdn_up': _jnp.float32, 'w_out': _jnp.float32, 'mlp_norm': _jnp.float32, 'w_ff1': _jnp.float32, 'w_ff2': _jnp.float32, 'final_norm': _jnp.float32}
MOMENT_SCALE = {'attn_norm': 4.468592e-02, 'w_in': 1.817930e-02, 'pool_w': 4.725559e-02, 'pool_scale': 4.859260e-02, 'gdn_conv': 2.068776e-02, 'gdn_a_log': 1.605215e-01, 'gdn_dt_bias': 1.601903e-01, 'gdn_norm': 7.854549e-02, 'w_pool_up': 2.359943e-02, 'w_sb_up': 1.831758e-02, 'w_gdn_up': 1.626801e-02, 'w_out': 3.407532e-02, 'mlp_norm': 4.992775e-02, 'w_ff1': 2.468161e-02, 'w_ff2': 4.706445e-02, 'final_norm': 8.127525e+00}


def _to_microbatches(a, axis):
    t = _jnp.moveaxis(a, axis, 0)
    t = t.reshape((N_MICROBATCH, t.shape[0] // N_MICROBATCH) + t.shape[1:])
    return _jnp.moveaxis(t, 1, axis + 1)


def setup_inputs(seed: int = 0) -> dict:
    inp = _fwd_setup_inputs(seed)
    key = _jax.random.fold_in(_jax.random.key(seed), 7919)
    shape, _ = _output_shape()
    out = dict(inp)
    out["loss_target"] = _jax.random.normal(_jax.random.fold_in(key, 0), shape, _jnp.float32)
    for i, name in enumerate(TWIN_WEIGHTS):
        w = inp[name].astype(_jnp.float32)
        if MOMENT_SCALE is None:
            s = _jnp.sqrt(_jnp.mean(_jnp.square(w)) + 1e-30)
        else:
            s = MOMENT_SCALE[name]
        km, kv = _jax.random.split(_jax.random.fold_in(key, i + 1))
        out[name] = w
        out["m_" + name] = s * _jax.random.normal(km, w.shape, _jnp.float32)
        out["v_" + name] = (s * s) * _jax.random.uniform(kv, w.shape, _jnp.float32, 0.5, 1.5)
    if N_MICROBATCH > 1:
        for name, axis in PER_EXAMPLE_BATCH_AXIS.items():
            out[name] = _to_microbatches(out[name], axis)
    return {'x': out['x'], 'attn_norm': out['attn_norm'], 'w_in': out['w_in'], 'pool_w': out['pool_w'], 'pool_scale': out['pool_scale'], 'gdn_conv': out['gdn_conv'], 'gdn_a_log': out['gdn_a_log'], 'gdn_dt_bias': out['gdn_dt_bias'], 'gdn_norm': out['gdn_norm'], 'w_pool_up': out['w_pool_up'], 'w_sb_up': out['w_sb_up'], 'w_gdn_up': out['w_gdn_up'], 'w_out': out['w_out'], 'mlp_norm': out['mlp_norm'], 'w_ff1': out['w_ff1'], 'w_ff2': out['w_ff2'], 'final_norm': out['final_norm'], 'loss_target': out['loss_target'], 'm_attn_norm': out['m_attn_norm'], 'm_w_in': out['m_w_in'], 'm_pool_w': out['m_pool_w'], 'm_pool_scale': out['m_pool_scale'], 'm_gdn_conv': out['m_gdn_conv'], 'm_gdn_a_log': out['m_gdn_a_log'], 'm_gdn_dt_bias': out['m_gdn_dt_bias'], 'm_gdn_norm': out['m_gdn_norm'], 'm_w_pool_up': out['m_w_pool_up'], 'm_w_sb_up': out['m_w_sb_up'], 'm_w_gdn_up': out['m_w_gdn_up'], 'm_w_out': out['m_w_out'], 'm_mlp_norm': out['m_mlp_norm'], 'm_w_ff1': out['m_w_ff1'], 'm_w_ff2': out['m_w_ff2'], 'm_final_norm': out['m_final_norm'], 'v_attn_norm': out['v_attn_norm'], 'v_w_in': out['v_w_in'], 'v_pool_w': out['v_pool_w'], 'v_pool_scale': out['v_pool_scale'], 'v_gdn_conv': out['v_gdn_conv'], 'v_gdn_a_log': out['v_gdn_a_log'], 'v_gdn_dt_bias': out['v_gdn_dt_bias'], 'v_gdn_norm': out['v_gdn_norm'], 'v_w_pool_up': out['v_w_pool_up'], 'v_w_sb_up': out['v_w_sb_up'], 'v_w_gdn_up': out['v_w_gdn_up'], 'v_w_out': out['v_w_out'], 'v_mlp_norm': out['v_mlp_norm'], 'v_w_ff1': out['v_w_ff1'], 'v_w_ff2': out['v_w_ff2'], 'v_final_norm': out['v_final_norm']}


def _loss(weights, diff, rest, loss_target):
    with _jax.named_scope("forward"):
        args = {**rest, TWIN_DIFF_INPUT: diff, **{k: w.astype(_WEIGHT_DTYPES[k]) for k, w in weights.items()}}
        y = _forward(args)
    with _jax.named_scope("loss_head"):
        err = _jnp.square(y.astype(_jnp.float32) - loss_target)
        return 0.5 * _jnp.sum(_jnp.mean(err, axis=-1)) if err.ndim else 0.5 * err


def _adamw(w, g, m, v):
    m = ADAM_B1 * m + (1.0 - ADAM_B1) * g
    v = ADAM_B2 * v + (1.0 - ADAM_B2) * _jnp.square(g)
    m_hat = m / (1.0 - ADAM_B1 ** ADAM_STEP)
    v_hat = v / (1.0 - ADAM_B2 ** ADAM_STEP)
    delta = -ADAM_LR * (m_hat / (_jnp.sqrt(v_hat) + ADAM_EPS) + ADAM_WD * w)
    return delta, m, v


def reference(x, attn_norm, w_in, pool_w, pool_scale, gdn_conv, gdn_a_log, gdn_dt_bias, gdn_norm, w_pool_up, w_sb_up, w_gdn_up, w_out, mlp_norm, w_ff1, w_ff2, final_norm, loss_target, m_attn_norm, m_w_in, m_pool_w, m_pool_scale, m_gdn_conv, m_gdn_a_log, m_gdn_dt_bias, m_gdn_norm, m_w_pool_up, m_w_sb_up, m_w_gdn_up, m_w_out, m_mlp_norm, m_w_ff1, m_w_ff2, m_final_norm, v_attn_norm, v_w_in, v_pool_w, v_pool_scale, v_gdn_conv, v_gdn_a_log, v_gdn_dt_bias, v_gdn_norm, v_w_pool_up, v_w_sb_up, v_w_gdn_up, v_w_out, v_mlp_norm, v_w_ff1, v_w_ff2, v_final_norm):
    given = dict(x=x, attn_norm=attn_norm, w_in=w_in, pool_w=pool_w, pool_scale=pool_scale, gdn_conv=gdn_conv, gdn_a_log=gdn_a_log, gdn_dt_bias=gdn_dt_bias, gdn_norm=gdn_norm, w_pool_up=w_pool_up, w_sb_up=w_sb_up, w_gdn_up=w_gdn_up, w_out=w_out, mlp_norm=mlp_norm, w_ff1=w_ff1, w_ff2=w_ff2, final_norm=final_norm, loss_target=loss_target, m_attn_norm=m_attn_norm, m_w_in=m_w_in, m_pool_w=m_pool_w, m_pool_scale=m_pool_scale, m_gdn_conv=m_gdn_conv, m_gdn_a_log=m_gdn_a_log, m_gdn_dt_bias=m_gdn_dt_bias, m_gdn_norm=m_gdn_norm, m_w_pool_up=m_w_pool_up, m_w_sb_up=m_w_sb_up, m_w_gdn_up=m_w_gdn_up, m_w_out=m_w_out, m_mlp_norm=m_mlp_norm, m_w_ff1=m_w_ff1, m_w_ff2=m_w_ff2, m_final_norm=m_final_norm, v_attn_norm=v_attn_norm, v_w_in=v_w_in, v_pool_w=v_pool_w, v_pool_scale=v_pool_scale, v_gdn_conv=v_gdn_conv, v_gdn_a_log=v_gdn_a_log, v_gdn_dt_bias=v_gdn_dt_bias, v_gdn_norm=v_gdn_norm, v_w_pool_up=v_w_pool_up, v_w_sb_up=v_w_sb_up, v_w_gdn_up=v_w_gdn_up, v_w_out=v_w_out, v_mlp_norm=v_mlp_norm, v_w_ff1=v_w_ff1, v_w_ff2=v_w_ff2, v_final_norm=v_final_norm)
    weights = {n: given[n] for n in TWIN_WEIGHTS}
    shared = {n: given[n] for n in SHARED_INPUTS}
    per_example = {n: given[n] for n in ['x']}
    grad_fn = _jax.value_and_grad(_loss, argnums=(0, 1))

    def one_microbatch(ex, loss_target):
        ex = dict(ex)
        diff = ex.pop(TWIN_DIFF_INPUT)
        return grad_fn(weights, diff, {**shared, **ex}, loss_target)

    if N_MICROBATCH == 1:
        loss, (grad_w, grad_x) = one_microbatch(per_example, given["loss_target"])
    else:
        def body(carry, xs):
            loss_sum, grad_sum = carry
            l_k, (gw_k, gx_k) = one_microbatch(xs[0], xs[1])
            with _jax.named_scope("update"):
                return (loss_sum + l_k, _jax.tree.map(_jnp.add, grad_sum, gw_k)), gx_k

        init = (_jnp.zeros((), _jnp.float32), _jax.tree.map(_jnp.zeros_like, weights))
        (loss, grad_w), grad_x = _jax.lax.scan(body, init, (per_example, given["loss_target"]))
    with _jax.named_scope("update"):
        delta_w, new_m, new_v = {}, {}, {}
        for n in TWIN_WEIGHTS:
            delta_w[n], new_m[n], new_v[n] = _adamw(weights[n], grad_w[n], given["m_" + n], given["v_" + n])
    return (loss, grad_x, *[grad_w[n] for n in TWIN_WEIGHTS], *[delta_w[n] for n in TWIN_WEIGHTS],
            *[new_m[n] for n in TWIN_WEIGHTS], *[new_v[n] for n in TWIN_WEIGHTS])
```

```python
import functools

import jax
import jax.numpy as jnp
from jax import lax
from jax.experimental import pallas as pl
from jax.experimental.pallas import tpu as pltpu

F32, BF16 = jnp.float32, jnp.bfloat16
HIGHEST = lax.Precision.HIGHEST
MESH = pl.DeviceIdType.MESH

D = 2048
EPS = 1e-6
POOL_WINDOWS = (2, 4, 8, 16)
POOL_W, SB_W, GDN_W = 512, 768, 768
HEADS, HD = 6, 128
SB_BLOCK = 128
GDN_CHUNK = 64
D_FF = 4 * D
N_IN = 12044
N_CHIPS = 4
OFF_P, OFF_SB, OFF_GQKV, OFF_Z, OFF_AB, OFF_GATE = 0, 512, 2816, 5120, 5888, 6144
AB_END = 5900
N_AL = 12288
VMEM_LIMIT = 48 * 1024 * 1024

ADAM_LR, ADAM_B1, ADAM_B2, ADAM_EPS, ADAM_WD, ADAM_STEP = 0.001, 0.9, 0.999, 1e-08, 0.01, 10

NT = (((1,), (1,)), ((), ()))
TN = (((0,), (0,)), ((), ()))


def _cp(*sem):
    return pltpu.CompilerParams(dimension_semantics=sem, vmem_limit_bytes=VMEM_LIMIT)


def _dot(a, b, dims=None, precision=None):
    if dims is None:
        dims = (((a.ndim - 1,), (0,)), ((), ()))
    return lax.dot_general(a, b, dims, precision=precision, preferred_element_type=F32)


def _hdot(a, b, dims=None):
    return _dot(a, b, dims, precision=HIGHEST)


def _bdot(a, b, dims=None):
    return _dot(a.astype(BF16), b.astype(BF16), dims)


def _mm(name, a, b, *, m, n, k, tm, tn, tk, a_spec, b_spec, dims, out_shapes, out_specs,
        extras=(), extra_specs=(), epilogue=None):
    nk = k // tk
    ne, no = len(extras), len(out_shapes)

    def body(*refs):
        a_ref, b_ref = refs[0], refs[1]
        ex = refs[2:2 + ne]
        outs = refs[2 + ne:2 + ne + no]
        acc = refs[-1]
        kk = pl.program_id(2)

        @pl.when(kk == 0)
        def _():
            acc[...] = jnp.zeros_like(acc)

        acc[...] += _dot(a_ref[...].astype(BF16), b_ref[...].astype(BF16), dims)

        @pl.when(kk == nk - 1)
        def _():
            r = acc[...]
            res = epilogue(r, *[e[...] for e in ex]) if epilogue is not None else (r,)
            for o, v in zip(outs, res):
                o[...] = v.astype(o.dtype)

    return pl.pallas_call(
        body, name=name, grid=(m // tm, n // tn, nk),
        in_specs=[a_spec, b_spec, *extra_specs], out_specs=out_specs, out_shape=out_shapes,
        scratch_shapes=[pltpu.VMEM((tm, tn), F32)],
        compiler_params=_cp("parallel", "parallel", "arbitrary"),
    )(a, b, *extras)


def _a_plain(tm, tk):
    return pl.BlockSpec((tm, tk), lambda i, j, kk: (i, kk))


def _a_trans(tm, tk):
    return pl.BlockSpec((tk, tm), lambda i, j, kk: (kk, i))


def _b_plain(tk, tn):
    return pl.BlockSpec((tk, tn), lambda i, j, kk: (kk, j))


def _b_trans(tk, tn):
    return pl.BlockSpec((tn, tk), lambda i, j, kk: (j, kk))


def _o_plain(tm, tn):
    return pl.BlockSpec((tm, tn), lambda i, j, kk: (i, j))


def _o_colshard(tm, tn, ns_cols):
    per = ns_cols // tn
    return pl.BlockSpec((None, tm, tn), lambda i, j, kk: (j // per, i, j % per))


def _w_cols(l, tk, tn, ns):
    per = ns // tn
    return pl.BlockSpec((None, None, tk, tn), lambda i, j, kk: (j // per, l, kk, j % per))


def _w_cols_t(l, tk, tn, ns):
    per = ns // tk
    return pl.BlockSpec((None, None, tn, tk), lambda i, j, kk: (kk // per, l, j, kk % per))


def _w_rows(l, tk, tn, ks):
    per = ks // tk
    return pl.BlockSpec((None, None, tk, tn), lambda i, j, kk: (kk // per, l, kk % per, j))


def _w_rows_t(l, tk, tn, ks):
    per = ks // tn
    return pl.BlockSpec((None, None, tn, tk), lambda i, j, kk: (j // per, l, j % per, kk))


def _sds(shape, dtype):
    return jax.ShapeDtypeStruct(shape, dtype)


def _rms_fwd(name, x, gain):
    t = x.shape[0]
    tt = min(256, t)

    def body(x_ref, g_ref, u_ref):
        xv = x_ref[...]
        r = lax.rsqrt(jnp.mean(xv * xv, axis=-1, keepdims=True) + EPS)
        u_ref[...] = (xv * r * g_ref[...]).astype(u_ref.dtype)

    return pl.pallas_call(
        body, name=name, grid=(t // tt,),
        in_specs=[pl.BlockSpec((tt, D), lambda i: (i, 0)), pl.BlockSpec((1, D), lambda i: (0, 0))],
        out_specs=pl.BlockSpec((tt, D), lambda i: (i, 0)), out_shape=_sds((t, D), BF16),
        compiler_params=_cp("parallel"),
    )(x, gain)


def _rms_bwd(name, du, x, gain, dres):
    t = x.shape[0]
    tt = min(256, t)

    def body(du_ref, x_ref, g_ref, dres_ref, dx_ref, dg_ref):
        @pl.when(pl.program_id(0) == 0)
        def _():
            dg_ref[...] = jnp.zeros_like(dg_ref)

        xv, duv = x_ref[...], du_ref[...]
        r = lax.rsqrt(jnp.mean(xv * xv, axis=-1, keepdims=True) + EPS)
        nx = xv * r
        dn = duv * g_ref[...]
        dg_ref[...] += jnp.sum(duv * nx, axis=0, keepdims=True)
        dx_ref[...] = dres_ref[...] + r * (dn - nx * jnp.mean(dn * nx, axis=-1, keepdims=True))

    row = pl.BlockSpec((tt, D), lambda i: (i, 0))
    vec = pl.BlockSpec((1, D), lambda i: (0, 0))
    return pl.pallas_call(
        body, name=name, grid=(t // tt,), in_specs=[row, row, vec, row], out_specs=[row, vec],
        out_shape=[_sds((t, D), F32), _sds((1, D), F32)], compiler_params=_cp("arbitrary"),
    )(du, x, gain, dres)


def _loss_head(x, gain, target):
    t = x.shape[0]
    tt = min(256, t)

    def body(x_ref, g_ref, t_ref, loss_ref, dx_ref, dg_ref):
        @pl.when(pl.program_id(0) == 0)
        def _():
            dg_ref[...] = jnp.zeros_like(dg_ref)
            loss_ref[...] = jnp.zeros_like(loss_ref)

        xv = x_ref[...]
        r = lax.rsqrt(jnp.mean(xv * xv, axis=-1, keepdims=True) + EPS)
        nx = xv * r
        err = nx * g_ref[...] - t_ref[...]
        loss_ref[...] += 0.5 * jnp.sum(jnp.mean(err * err, axis=-1, keepdims=True), axis=0, keepdims=True)
        dy = err * (1.0 / D)
        dn = dy * g_ref[...]
        dg_ref[...] += jnp.sum(dy * nx, axis=0, keepdims=True)
        dx_ref[...] = r * (dn - nx * jnp.mean(dn * nx, axis=-1, keepdims=True))

    row = pl.BlockSpec((tt, D), lambda i: (i, 0))
    vec = pl.BlockSpec((1, D), lambda i: (0, 0))
    one = pl.BlockSpec((1, 1), lambda i: (0, 0))
    return pl.pallas_call(
        body, name="loss_head", grid=(t // tt,), in_specs=[row, vec, row], out_specs=[one, row, vec],
        out_shape=[_sds((1, 1), F32), _sds((t, D), F32), _sds((1, D), F32)], compiler_params=_cp("arbitrary"),
    )(x, gain, target)


def _shift_down(v, s, t_idx):
    return jnp.where(t_idx >= s, pltpu.roll(v, s, 0), 0.0)


def _shift_up(v, s, t_idx, t):
    return jnp.where(t_idx < t - s, pltpu.roll(v, t - s, 0), 0.0)


def _pool_d(p, g, t_idx):
    s = p
    for step in range(g + 1):
        s = s + _shift_down(s, 1 << step, t_idx)
    cnt = jnp.minimum(t_idx + 1, POOL_WINDOWS[g]).astype(F32)
    return s / cnt - p, cnt


def _pool_fwd(proj, pool_w, pool_scale):
    t = proj.shape[0]
    g128 = POOL_W // len(POOL_WINDOWS)

    def body(p_ref, w_ref, s_ref, y_ref):
        t_idx = lax.broadcasted_iota(jnp.int32, (t, g128), 0)
        for g in range(len(POOL_WINDOWS)):
            sl = slice(g * g128, (g + 1) * g128)
            d, _ = _pool_d(p_ref[:, sl], g, t_idx)
            y_ref[:, sl] = (_bdot(d, w_ref[g]) * s_ref[:, sl]).astype(y_ref.dtype)

    return pl.pallas_call(
        body, name="pool_fwd", grid=(1,),
        in_specs=[pl.BlockSpec((t, POOL_W), lambda i: (0, OFF_P // POOL_W)),
                  pl.BlockSpec((4, g128, g128), lambda i: (0, 0, 0)), pl.BlockSpec((1, POOL_W), lambda i: (0, 0))],
        out_specs=pl.BlockSpec((t, POOL_W), lambda i: (0, 0)), out_shape=_sds((t, POOL_W), BF16),
        compiler_params=_cp("arbitrary"),
    )(proj, pool_w, pool_scale)


def _pool_bwd(proj, pool_w, pool_scale, dy):
    t = proj.shape[0]
    g128 = POOL_W // len(POOL_WINDOWS)

    def body(p_ref, w_ref, s_ref, dy_ref, dp_ref, dw_ref, ds_ref):
        t_idx = lax.broadcasted_iota(jnp.int32, (t, g128), 0)
        for g in range(len(POOL_WINDOWS)):
            sl = slice(g * g128, (g + 1) * g128)
            d, cnt = _pool_d(p_ref[:, sl], g, t_idx)
            dyv = dy_ref[:, sl].astype(F32)
            ds_ref[:, sl] = jnp.sum(dyv * _bdot(d, w_ref[g]), axis=0, keepdims=True)
            dys = dyv * s_ref[:, sl]
            dw_ref[g] = _bdot(d, dys, TN)
            dd = _bdot(dys, w_ref[g], NT)
            s = dd / cnt
            for step in range(g + 1):
                s = s + _shift_up(s, 1 << step, t_idx, t)
            dp_ref[:, sl] = (s - dd).astype(dp_ref.dtype)

    return pl.pallas_call(
        body, name="pool_bwd", grid=(1,),
        in_specs=[pl.BlockSpec((t, POOL_W), lambda i: (0, OFF_P // POOL_W)),
                  pl.BlockSpec((4, g128, g128), lambda i: (0, 0, 0)), pl.BlockSpec((1, POOL_W), lambda i: (0, 0)),
                  pl.BlockSpec((t, POOL_W), lambda i: (0, 0))],
        out_specs=[pl.BlockSpec((t, POOL_W), lambda i: (0, 0)), pl.BlockSpec((4, g128, g128), lambda i: (0, 0, 0)),
                   pl.BlockSpec((1, POOL_W), lambda i: (0, 0))],
        out_shape=[_sds((t, POOL_W), BF16), _sds((4, g128, g128), F32), _sds((1, POOL_W), F32)],
        compiler_params=_cp("arbitrary"),
    )(proj, pool_w, pool_scale, dy)


def _sb_block(q, kb, i, j):
    row = lax.broadcasted_iota(jnp.int32, (SB_BLOCK, SB_BLOCK), 0)
    col = lax.broadcasted_iota(jnp.int32, (SB_BLOCK, SB_BLOCK), 1)
    z = _dot(q, kb, NT)
    mask = (col + j * SB_BLOCK) < (row + i * SB_BLOCK)
    return z, mask, jax.nn.log_sigmoid(z)


def _sb_specs(t):
    qb = SB_BLOCK
    q_spec = pl.BlockSpec((qb, HD), lambda h, i: (i, OFF_SB // HD + h))
    k_spec = pl.BlockSpec((t, HD), lambda h, i: (0, OFF_SB // HD + HEADS + h))
    v_spec = pl.BlockSpec((t, HD), lambda h, i: (0, OFF_SB // HD + 2 * HEADS + h))
    return q_spec, k_spec, v_spec


def _sb_fwd(proj):
    t = proj.shape[0]
    scale = HD ** -0.5

    def body(q_ref, k_ref, v_ref, o_ref):
        i = pl.program_id(1)
        q = (q_ref[...] * scale).astype(BF16)
        row = lax.broadcasted_iota(jnp.int32, (SB_BLOCK, SB_BLOCK), 0)
        col = lax.broadcasted_iota(jnp.int32, (SB_BLOCK, SB_BLOCK), 1)
        later = (row > col).astype(F32)

        def step(jj, carry):
            acc, run = carry
            j = i - jj
            off = pl.multiple_of(j * SB_BLOCK, SB_BLOCK)
            kb = k_ref[pl.ds(off, SB_BLOCK), :].astype(BF16)
            vb = v_ref[pl.ds(off, SB_BLOCK), :].astype(BF16)
            z, mask, lsz = _sb_block(q, kb, i, j)
            ls = jnp.where(mask, lsz - z, 0.0)
            log_later = _hdot(ls, later) + run
            a = jnp.where(mask, jnp.exp(lsz + log_later), 0.0)
            return acc + _dot(a.astype(BF16), vb), run + jnp.sum(ls, axis=1, keepdims=True)

        acc, _ = lax.fori_loop(0, i + 1, step, (jnp.zeros((SB_BLOCK, HD), F32), jnp.zeros((SB_BLOCK, 1), F32)))
        o_ref[...] = acc.astype(o_ref.dtype)

    return pl.pallas_call(
        body, name="sb_fwd", grid=(HEADS, t // SB_BLOCK), in_specs=list(_sb_specs(t)),
        out_specs=pl.BlockSpec((SB_BLOCK, HD), lambda h, i: (i, h)), out_shape=_sds((t, SB_W), BF16),
        compiler_params=_cp("parallel", "arbitrary"),
    )(proj, proj, proj)


def _sb_bwd(proj, dy):
    t = proj.shape[0]
    nq = t // SB_BLOCK
    scale = HD ** -0.5

    def body(q_ref, k_ref, v_ref, do_ref, dq_ref, dk_ref, dv_ref, z_scr, e_scr):
        i = pl.program_id(1)

        @pl.when(i == 0)
        def _():
            dk_ref[...] = jnp.zeros_like(dk_ref)
            dv_ref[...] = jnp.zeros_like(dv_ref)

        q = (q_ref[...] * scale).astype(BF16)
        do = do_ref[...].astype(BF16)
        row = lax.broadcasted_iota(jnp.int32, (SB_BLOCK, SB_BLOCK), 0)
        col = lax.broadcasted_iota(jnp.int32, (SB_BLOCK, SB_BLOCK), 1)
        later = (row > col).astype(F32)
        earlier = (row < col).astype(F32)

        def down(jj, run):
            j = i - jj
            off = pl.multiple_of(j * SB_BLOCK, SB_BLOCK)
            kb = k_ref[pl.ds(off, SB_BLOCK), :].astype(BF16)
            vb = v_ref[pl.ds(off, SB_BLOCK), :].astype(BF16)
            z, mask, lsz = _sb_block(q, kb, i, j)
            ls = jnp.where(mask, lsz - z, 0.0)
            log_later = _hdot(ls, later) + run
            a = jnp.where(mask, jnp.exp(lsz + log_later), 0.0)
            z_scr[j] = z
            e_scr[j] = a * _dot(do, vb, NT)
            dv_ref[pl.ds(off, SB_BLOCK), :] += _dot(a.astype(BF16), do, TN)
            return run + jnp.sum(ls, axis=1, keepdims=True)

        lax.fori_loop(0, i + 1, down, jnp.zeros((SB_BLOCK, 1), F32))

        def up(j, carry):
            dq, run = carry
            off = pl.multiple_of(j * SB_BLOCK, SB_BLOCK)
            kb = k_ref[pl.ds(off, SB_BLOCK), :].astype(BF16)
            z, e = z_scr[j], e_scr[j]
            mask = (col + j * SB_BLOCK) < (row + i * SB_BLOCK)
            before = _hdot(e, earlier) + run
            dz = jnp.where(mask, e * jax.nn.sigmoid(-z) - before * jax.nn.sigmoid(z), 0.0).astype(BF16)
            dk_ref[pl.ds(off, SB_BLOCK), :] += _dot(dz, q, TN)
            return dq + _dot(dz, kb), run + jnp.sum(e, axis=1, keepdims=True)

        dq, _ = lax.fori_loop(0, i + 1, up, (jnp.zeros((SB_BLOCK, HD), F32), jnp.zeros((SB_BLOCK, 1), F32)))
        dq_ref[...] = (dq * scale).astype(dq_ref.dtype)

    blk = pl.BlockSpec((SB_BLOCK, HD), lambda h, i: (i, h))
    seq = pl.BlockSpec((t, HD), lambda h, i: (0, h))
    return pl.pallas_call(
        body, name="sb_bwd", grid=(HEADS, nq), in_specs=[*_sb_specs(t), blk], out_specs=[blk, seq, seq],
        out_shape=[_sds((t, SB_W), BF16), _sds((t, SB_W), F32), _sds((t, SB_W), F32)],
        scratch_shapes=[pltpu.VMEM((nq, SB_BLOCK, SB_BLOCK), F32), pltpu.VMEM((nq, SB_BLOCK, SB_BLOCK), F32)],
        compiler_params=_cp("parallel", "arbitrary"),
    )(proj, proj, proj, dy)


CONV_TILE = 256
GDN_CONV = 4


def _conv_pre(x, w_ref, t_idx):
    pre = w_ref[GDN_CONV - 1:GDN_CONV, :] * x
    for s in range(1, GDN_CONV):
        pre = pre + w_ref[GDN_CONV - 1 - s:GDN_CONV - s, :] * _shift_down(x, s, t_idx)
    return pre


def _conv_fwd(proj, conv_w):
    t = proj.shape[0]
    width = conv_w.shape[1]

    def body(x_ref, w_ref, y_ref):
        t_idx = lax.broadcasted_iota(jnp.int32, (t, CONV_TILE), 0)
        pre = _conv_pre(x_ref[...], w_ref, t_idx)
        y_ref[...] = pre * jax.nn.sigmoid(pre)

    return pl.pallas_call(
        body, name="conv_fwd", grid=(width // CONV_TILE,),
        in_specs=[pl.BlockSpec((t, CONV_TILE), lambda c: (0, OFF_GQKV // CONV_TILE + c)),
                  pl.BlockSpec((GDN_CONV, CONV_TILE), lambda c: (0, c))],
        out_specs=pl.BlockSpec((t, CONV_TILE), lambda c: (0, c)), out_shape=_sds((t, width), F32),
        compiler_params=_cp("parallel"),
    )(proj, conv_w)


def _conv_bwd(proj, conv_w, dc, part):
    t = proj.shape[0]
    width = dc.shape[1]
    per = width // CONV_TILE

    def body(x_ref, w_ref, dc_ref, dx_ref, dw_ref):
        t_idx = lax.broadcasted_iota(jnp.int32, (t, CONV_TILE), 0)
        x = x_ref[...]
        pre = _conv_pre(x, w_ref, t_idx)
        sg = jax.nn.sigmoid(pre)
        dpre = dc_ref[...] * (sg * (1.0 + pre * (1.0 - sg)))
        dx = w_ref[GDN_CONV - 1:GDN_CONV, :] * dpre
        dw_ref[GDN_CONV - 1:GDN_CONV, :] = jnp.sum(dpre * x, axis=0, keepdims=True)
        for s in range(1, GDN_CONV):
            dx = dx + w_ref[GDN_CONV - 1 - s:GDN_CONV - s, :] * _shift_up(dpre, s, t_idx, t)
            dw_ref[GDN_CONV - 1 - s:GDN_CONV - s, :] = jnp.sum(dpre * _shift_down(x, s, t_idx), axis=0, keepdims=True)
        dx_ref[...] = dx.astype(dx_ref.dtype)

    return pl.pallas_call(
        body, name="conv_bwd", grid=(per,),
        in_specs=[pl.BlockSpec((t, CONV_TILE), lambda c: (0, OFF_GQKV // CONV_TILE + part * per + c)),
                  pl.BlockSpec((GDN_CONV, CONV_TILE), lambda c: (0, part * per + c)),
                  pl.BlockSpec((t, CONV_TILE), lambda c: (0, c))],
        out_specs=[pl.BlockSpec((t, CONV_TILE), lambda c: (0, c)), pl.BlockSpec((GDN_CONV, CONV_TILE), lambda c: (0, c))],
        out_shape=[_sds((t, width), BF16), _sds((GDN_CONV, width), F32)],
        compiler_params=_cp("parallel"),
    )(proj, conv_w, dc)


def _lane_pick(v, lane):
    idx = lax.broadcasted_iota(jnp.int32, v.shape, 1)
    return jnp.sum(jnp.where(idx == lane, v, 0.0), axis=1, keepdims=True)


def _gdn_prep(cq, ck, cv, ab, alog_row, dtb_row, h):
    c = GDN_CHUNK
    row = lax.broadcasted_iota(jnp.int32, (c, c), 0)
    col = lax.broadcasted_iota(jnp.int32, (c, c), 1)
    incl, strict, eye = row >= col, row > col, row == col
    a_col, b_col = _lane_pick(ab, h), _lane_pick(ab, HEADS + h)
    a_log, dt_bias = _lane_pick(alog_row, h), _lane_pick(dtb_row, h)
    qn = cq * lax.rsqrt(jnp.sum(cq * cq, axis=-1, keepdims=True) + EPS) * (HD ** -0.5)
    kn = ck * lax.rsqrt(jnp.sum(ck * ck, axis=-1, keepdims=True) + EPS)
    la_col = -jnp.exp(a_log) * jax.nn.softplus(a_col + dt_bias)
    beta = jax.nn.sigmoid(b_col)
    la_row = jnp.sum(jnp.where(eye, la_col, 0.0), axis=0, keepdims=True)
    g_col = jnp.sum(jnp.where(incl, la_row, 0.0), axis=1, keepdims=True)
    g_row = jnp.sum(jnp.where(row <= col, la_col, 0.0), axis=0, keepdims=True)
    g_last = jnp.sum(la_col, axis=0, keepdims=True)
    gamma = jnp.where(incl, jnp.exp(jnp.where(incl, g_col - g_row, 0.0)), 0.0)
    lower = jnp.where(strict, beta * _hdot(kn, kn, NT) * gamma, 0.0)
    inv = jnp.where(eye, 1.0, 0.0) - lower
    pw = _hdot(lower, lower)
    for step in range(5):
        inv = inv + _hdot(inv, pw)
        if step < 4:
            pw = _hdot(pw, pw)
    u = _hdot(inv, cv * beta)
    w = _hdot(inv, kn * (beta * jnp.exp(g_col)))
    qk = _hdot(qn, kn, NT) * gamma
    return u, w, qk, qn * jnp.exp(g_col), kn * jnp.exp(g_last - g_col), jnp.exp(g_last)


def _gdn_post(o, z, gain):
    y = o * lax.rsqrt(jnp.mean(o * o, axis=-1, keepdims=True) + EPS) * gain
    return y * (z * jax.nn.sigmoid(z))


def _gdn_specs(nrev):
    c = GDN_CHUNK

    def ch(n):
        return nrev - 1 - n if nrev else n

    def cblk(off):
        return pl.BlockSpec((c, HD), lambda n, h: (ch(n), off + h))

    row = pl.BlockSpec((1, HD), lambda n, h: (0, 0))
    return cblk, row, ch


def _gdn_fwd(cqkv, proj, a_log, dt_bias, gain):
    t = proj.shape[0]
    c, nc = GDN_CHUNK, t // GDN_CHUNK
    cblk, row, _ = _gdn_specs(0)

    def body(cq_ref, ck_ref, cv_ref, ab_ref, z_ref, al_ref, dt_ref, g_ref, y_ref, sprev_ref, s_scr):
        n, h = pl.program_id(0), pl.program_id(1)

        @pl.when(n == 0)
        def _():
            s_scr[h] = jnp.zeros((HD, HD), F32)

        u, w, qk, qd, kd, dec = _gdn_prep(cq_ref[...], ck_ref[...], cv_ref[...], ab_ref[...], al_ref[...], dt_ref[...], h)
        s = s_scr[h]
        sprev_ref[...] = s
        v_new = u - _hdot(w, s)
        o = _hdot(qd, s) + _hdot(qk, v_new)
        s_scr[h] = s * dec + _hdot(kd, v_new, TN)
        y_ref[...] = _gdn_post(o, z_ref[...], g_ref[...]).astype(y_ref.dtype)

    return pl.pallas_call(
        body, name="gdn_fwd", grid=(nc, HEADS),
        in_specs=[cblk(0), cblk(HEADS), cblk(2 * HEADS),
                  pl.BlockSpec((c, HD), lambda n, h: (n, OFF_AB // HD)), cblk(OFF_Z // HD), row, row, row],
        out_specs=[cblk(0), pl.BlockSpec((None, None, HD, HD), lambda n, h: (n, h, 0, 0))],
        out_shape=[_sds((t, GDN_W), BF16), _sds((nc, HEADS, HD, HD), F32)],
        scratch_shapes=[pltpu.VMEM((HEADS, HD, HD), F32)],
        compiler_params=_cp("arbitrary", "arbitrary"),
    )(cqkv, cqkv, cqkv, proj, proj, a_log, dt_bias, gain)


def _gdn_bwd(cqkv, proj, a_log, dt_bias, gain, sprev, dy):
    t = proj.shape[0]
    c, nc = GDN_CHUNK, t // GDN_CHUNK
    cblk, row, ch = _gdn_specs(nc)

    def body(cq_ref, ck_ref, cv_ref, ab_ref, z_ref, al_ref, dt_ref, g_ref, sp_ref, dy_ref,
             dcq_ref, dck_ref, dcv_ref, dab_ref, dz_ref, dal_ref, ddt_ref, dg_ref, ds_scr):
        n, h = pl.program_id(0), pl.program_id(1)

        @pl.when(n == 0)
        def _():
            ds_scr[h] = jnp.zeros((HD, HD), F32)

        @pl.when((n == 0) & (h == 0))
        def _():
            dal_ref[...] = jnp.zeros_like(dal_ref)
            ddt_ref[...] = jnp.zeros_like(ddt_ref)
            dg_ref[...] = jnp.zeros_like(dg_ref)

        @pl.when(h == 0)
        def _():
            dab_ref[...] = jnp.zeros_like(dab_ref)

        (u, w, qk, qd, kd, dec), prep_vjp = jax.vjp(
            lambda cq, ck, cv, ab, al, dt: _gdn_prep(cq, ck, cv, ab, al, dt, h),
            cq_ref[...], ck_ref[...], cv_ref[...], ab_ref[...], al_ref[...], dt_ref[...])
        s = sp_ref[...]
        v_new = u - _hdot(w, s)
        o = _hdot(qd, s) + _hdot(qk, v_new)
        _, post_vjp = jax.vjp(_gdn_post, o, z_ref[...], g_ref[...])
        do, dz, dgain = post_vjp(dy_ref[...].astype(F32))
        ds_next = ds_scr[h]
        d_vnew = _hdot(qk, do, TN) + _hdot(kd, ds_next)
        d_qk = _hdot(do, v_new, NT)
        d_qd = _hdot(do, s, NT)
        d_kd = _hdot(v_new, ds_next, NT)
        d_dec = jnp.sum(jnp.sum(s * ds_next, axis=1, keepdims=True), axis=0, keepdims=True)
        ds_scr[h] = dec * ds_next + _hdot(qd, do, TN) - _hdot(w, d_vnew, TN)
        d_w = -_hdot(d_vnew, s, NT)
        dcq, dck, dcv, dab, dal, ddt = prep_vjp((d_vnew, d_w, d_qk, d_qd, d_kd, d_dec))
        dcq_ref[...] = dcq
        dck_ref[...] = dck
        dcv_ref[...] = dcv
        dz_ref[...] = dz.astype(dz_ref.dtype)
        dab_ref[...] += dab
        dal_ref[...] += dal
        ddt_ref[...] += ddt
        dg_ref[...] += dgain

    hblk = pl.BlockSpec((c, HD), lambda n, h: (ch(n), h))
    return pl.pallas_call(
        body, name="gdn_bwd", grid=(nc, HEADS),
        in_specs=[cblk(0), cblk(HEADS), cblk(2 * HEADS),
                  pl.BlockSpec((c, HD), lambda n, h: (ch(n), OFF_AB // HD)), cblk(OFF_Z // HD), row, row, row,
                  pl.BlockSpec((None, None, HD, HD), lambda n, h: (ch(n), h, 0, 0)), hblk],
        out_specs=[hblk, hblk, hblk, pl.BlockSpec((c, HD), lambda n, h: (ch(n), 0)), hblk, row, row, row],
        out_shape=[_sds((t, GDN_W), F32), _sds((t, GDN_W), F32), _sds((t, GDN_W), F32), _sds((t, HD), F32),
                   _sds((t, GDN_W), BF16), _sds((1, HD), F32), _sds((1, HD), F32), _sds((1, HD), F32)],
        scratch_shapes=[pltpu.VMEM((HEADS, HD, HD), F32)],
        compiler_params=_cp("arbitrary", "arbitrary"),
    )(cqkv, cqkv, cqkv, proj, proj, a_log, dt_bias, gain, sprev, dy)


MERGE_TN = 512


def _merge_specs(t, tm, l):
    tn = MERGE_TN
    ys = [pl.BlockSpec((tm, wd), lambda i, j: (i, 0)) for wd in (POOL_W, SB_W, GDN_W)]
    ws = [pl.BlockSpec((None, None, wd, tn), lambda i, j: (j, l, 0, 0)) for wd in (POOL_W, SB_W, GDN_W)]
    gs = [pl.BlockSpec((tm, tn), functools.partial(lambda i, j, b: (i, OFF_GATE // tn + b * (D // tn) + j), b=b))
          for b in range(3)]
    out = pl.BlockSpec((tm, tn), lambda i, j: (i, j))
    return ys, ws, gs, out


def _merge_fwd(ys, wups, proj, l):
    t = proj.shape[0]
    tm = min(512, t)
    y_specs, w_specs, g_specs, out = _merge_specs(t, tm, l)

    def body(y0, y1, y2, w0, w1, w2, g0, g1, g2, o_ref):
        acc = jnp.zeros(o_ref.shape, F32)
        for y, w, g in ((y0, w0, g0), (y1, w1, g1), (y2, w2, g2)):
            acc = acc + jax.nn.sigmoid(g[...]) * _dot(y[...], w[...])
        o_ref[...] = acc.astype(o_ref.dtype)

    return pl.pallas_call(
        body, name="merge_fwd", grid=(t // tm, D // MERGE_TN), in_specs=[*y_specs, *w_specs, *g_specs],
        out_specs=out, out_shape=_sds((t, D), BF16), compiler_params=_cp("parallel", "parallel"),
    )(*ys, *wups, proj, proj, proj)


def _merge_bwd(ys, wups, proj, dmerged, l):
    t = proj.shape[0]
    tm = min(512, t)
    y_specs, w_specs, g_specs, out = _merge_specs(t, tm, l)

    def body(y0, y1, y2, w0, w1, w2, g0, g1, g2, dm_ref, dg0, dg1, dg2, dm0, dm1, dm2):
        dm = dm_ref[...].astype(F32)
        for y, w, g, dg, dmb in ((y0, w0, g0, dg0, dm0), (y1, w1, g1, dg1, dm1), (y2, w2, g2, dg2, dm2)):
            sg = jax.nn.sigmoid(g[...])
            dg[...] = (dm * _dot(y[...], w[...]) * sg * (1.0 - sg)).astype(dg.dtype)
            dmb[...] = (dm * sg).astype(dmb.dtype)

    return pl.pallas_call(
        body, name="merge_bwd", grid=(t // tm, D // MERGE_TN), in_specs=[*y_specs, *w_specs, *g_specs, out],
        out_specs=[out] * 6, out_shape=[_sds((t, D), BF16)] * 6, compiler_params=_cp("parallel", "parallel"),
    )(*ys, *wups, proj, proj, proj, dmerged)


def _tile(t, want):
    return min(t, want)


def _layer_fwd(x, l, gw, w_al, sp):
    t = x.shape[0]
    tm = _tile(t, 1024)
    u = _rms_fwd("rms_attn", x, sp["attn_norm"][l])
    proj = _mm("proj", u, w_al, m=t, n=N_AL, k=D, tm=tm, tn=1024, tk=512, a_spec=_a_plain(tm, 512),
               b_spec=_b_plain(512, 1024), dims=None, out_shapes=[_sds((t, N_AL), F32)], out_specs=[_o_plain(tm, 1024)])[0]
    y_pool = _pool_fwd(proj, sp["pool_w"][l], sp["pool_scale"][l])
    y_sb = _sb_fwd(proj)
    cqkv = _conv_fwd(proj, sp["conv"][l])
    y_gdn, sprev = _gdn_fwd(cqkv, proj, sp["a_log"][l], sp["dt_bias"][l], sp["gdn_norm"][l])
    ys = (y_pool, y_sb, y_gdn)
    wups = (gw["w_pool_up"], gw["w_sb_up"], gw["w_gdn_up"])
    merged = _merge_fwd(ys, wups, proj, l)
    x1 = _mm("out_proj", merged, gw["w_out"], m=t, n=D, k=D, tm=tm, tn=1024, tk=512, a_spec=_a_plain(tm, 512),
             b_spec=_w_rows(l, 512, 1024, 512), dims=None, out_shapes=[_sds((t, D), F32)], out_specs=[_o_plain(tm, 1024)],
             extras=[x], extra_specs=[_o_plain(tm, 1024)], epilogue=lambda r, xr: (r + xr,))[0]
    u2 = _rms_fwd("rms_mlp", x1, sp["mlp_norm"][l])

    def relu2(r):
        hv = jnp.maximum(r, 0.0)
        return hv, hv * hv

    hid, hid2 = _mm("ff1", u2, gw["w_ff1"], m=t, n=D_FF, k=D, tm=tm, tn=1024, tk=512, a_spec=_a_plain(tm, 512),
                    b_spec=_w_cols(l, 512, 1024, 2048), dims=None, out_shapes=[_sds((t, D_FF), BF16)] * 2,
                    out_specs=[_o_plain(tm, 1024)] * 2, epilogue=relu2)
    x2 = _mm("ff2", hid2, gw["w_ff2"], m=t, n=D, k=D_FF, tm=tm, tn=1024, tk=512, a_spec=_a_plain(tm, 512),
             b_spec=_w_rows(l, 512, 1024, 2048), dims=None, out_shapes=[_sds((t, D), F32)], out_specs=[_o_plain(tm, 1024)],
             extras=[x1], extra_specs=[_o_plain(tm, 1024)], epilogue=lambda r, xr: (r + xr,))[0]
    saved = dict(x=x, u=u, proj=proj, cqkv=cqkv, sprev=sprev, ys=ys, merged=merged, x1=x1, u2=u2, hid=hid, hid2=hid2)
    return x2, saved


def _layer_bwd(dx2, l, gw, w_al, sp, sv):
    t = dx2.shape[0]
    tm = _tile(t, 1024)
    tk = _tile(t, 512)
    g = {}
    dpre = _mm("ff2_dx", dx2, gw["w_ff2"], m=t, n=D_FF, k=D, tm=tm, tn=1024, tk=512, a_spec=_a_plain(tm, 512),
               b_spec=_w_rows_t(l, 512, 1024, 2048), dims=NT, out_shapes=[_sds((t, D_FF), BF16)],
               out_specs=[_o_plain(tm, 1024)], extras=[sv["hid"]], extra_specs=[_o_plain(tm, 1024)],
               epilogue=lambda r, hv: (r * (2.0 * hv.astype(F32)),))[0]
    g["w_ff2"] = _mm("ff2_dw", sv["hid2"], dx2, m=D_FF, n=D, k=t, tm=1024, tn=1024, tk=tk, a_spec=_a_trans(1024, tk),
                     b_spec=_b_plain(tk, 1024), dims=TN, out_shapes=[_sds((D_FF, D), BF16)],
                     out_specs=[_o_plain(1024, 1024)])[0].reshape(N_CHIPS, D_FF // N_CHIPS, D)
    du2 = _mm("ff1_dx", dpre, gw["w_ff1"], m=t, n=D, k=D_FF, tm=tm, tn=1024, tk=512, a_spec=_a_plain(tm, 512),
              b_spec=_w_cols_t(l, 512, 1024, 2048), dims=NT, out_shapes=[_sds((t, D), F32)], out_specs=[_o_plain(tm, 1024)])[0]
    g["w_ff1"] = _mm("ff1_dw", sv["u2"], dpre, m=D, n=D_FF, k=t, tm=1024, tn=1024, tk=tk, a_spec=_a_trans(1024, tk),
                     b_spec=_b_plain(tk, 1024), dims=TN, out_shapes=[_sds((N_CHIPS, D, D_FF // N_CHIPS), BF16)],
                     out_specs=[_o_colshard(1024, 1024, D_FF // N_CHIPS)])[0]
    dx1, g["mlp_norm"] = _rms_bwd("rms_mlp_bwd", du2, sv["x1"], sp["mlp_norm"][l], dx2)
    dmerged = _mm("out_dx", dx1, gw["w_out"], m=t, n=D, k=D, tm=tm, tn=512, tk=1024, a_spec=_a_plain(tm, 1024),
                  b_spec=_w_rows_t(l, 1024, 512, 512), dims=NT, out_shapes=[_sds((t, D), BF16)], out_specs=[_o_plain(tm, 512)])[0]
    g["w_out"] = _mm("out_dw", sv["merged"], dx1, m=D, n=D, k=t, tm=1024, tn=1024, tk=tk, a_spec=_a_trans(1024, tk),
                     b_spec=_b_plain(tk, 1024), dims=TN, out_shapes=[_sds((D, D), BF16)],
                     out_specs=[_o_plain(1024, 1024)])[0].reshape(N_CHIPS, D // N_CHIPS, D)
    wups = (gw["w_pool_up"], gw["w_sb_up"], gw["w_gdn_up"])
    dg0, dg1, dg2, dm0, dm1, dm2 = _merge_bwd(sv["ys"], wups, sv["proj"], dmerged, l)
    dys = []
    for nm, yb, dmb, wd in zip(("w_pool_up", "w_sb_up", "w_gdn_up"), sv["ys"], (dm0, dm1, dm2), (POOL_W, SB_W, GDN_W)):
        dys.append(_mm(nm + "_dx", dmb, gw[nm], m=t, n=wd, k=D, tm=tm, tn=256, tk=512, a_spec=_a_plain(tm, 512),
                       b_spec=_w_cols_t(l, 512, 256, 512), dims=NT, out_shapes=[_sds((t, wd), F32)],
                       out_specs=[_o_plain(tm, 256)])[0])
        g[nm] = _mm(nm + "_dw", yb, dmb, m=wd, n=D, k=t, tm=256, tn=512, tk=tk, a_spec=_a_trans(256, tk),
                    b_spec=_b_plain(tk, 512), dims=TN, out_shapes=[_sds((N_CHIPS, wd, D // N_CHIPS), BF16)],
                    out_specs=[_o_colshard(256, 512, D // N_CHIPS)])[0]
    proj = sv["proj"]
    dp, g["pool_w"], g["pool_scale"] = _pool_bwd(proj, sp["pool_w"][l], sp["pool_scale"][l], dys[0])
    dsq, dsk, dsv = _sb_bwd(proj, dys[1])
    dcq, dck, dcv, dab, dz, g["a_log"], g["dt_bias"], g["gdn_norm"] = _gdn_bwd(
        sv["cqkv"], proj, sp["a_log"][l], sp["dt_bias"][l], sp["gdn_norm"][l], sv["sprev"], dys[2])
    dgx, dgw = zip(*[_conv_bwd(proj, sp["conv"][l], dc, part) for part, dc in enumerate((dcq, dck, dcv))])
    g["conv"] = jnp.concatenate(dgw, axis=1)
    dproj = jnp.concatenate(
        [dp, dsq, dsk.astype(BF16), dsv.astype(BF16), *dgx, dz, dab.astype(BF16),
         jnp.zeros((t, OFF_GATE - OFF_AB - HD), BF16), dg0, dg1, dg2], axis=1)
    du = _mm("proj_dx", dproj, w_al, m=t, n=D, k=N_AL, tm=tm, tn=1024, tk=512, a_spec=_a_plain(tm, 512),
             b_spec=_b_trans(512, 1024), dims=NT, out_shapes=[_sds((t, D), F32)], out_specs=[_o_plain(tm, 1024)])[0]
    g["w_al"] = _mm("proj_dw", sv["u"], dproj, m=D, n=N_AL, k=t, tm=1024, tn=1024, tk=tk, a_spec=_a_trans(1024, tk),
                    b_spec=_b_plain(tk, 1024), dims=TN, out_shapes=[_sds((D, N_AL), BF16)], out_specs=[_o_plain(1024, 1024)])[0]
    dx, g["attn_norm"] = _rms_bwd("rms_attn_bwd", du, sv["x"], sp["attn_norm"][l], dx1)
    return dx, g


def _align_w_in(w):
    return jnp.concatenate([w[:, :AB_END], jnp.zeros((D, OFF_GATE - AB_END), w.dtype), w[:, AB_END:]], axis=1)


def _unalign_w_in(w):
    return jnp.concatenate([w[:, :AB_END], w[:, OFF_GATE:]], axis=1)


def _row128(v):
    return jnp.pad(v.reshape(1, -1), ((0, 0), (0, HD - v.shape[-1])))


def _local_step(x, target, gw, w_in_al, sp):
    saved = []
    h = x
    for l in range(2):
        h, sv = _layer_fwd(h, l, gw, w_in_al[l], sp)
        saved.append(sv)
    loss, dh, g_final = _loss_head(h, sp["final_norm"], target)
    grads = [None, None]
    for l in (1, 0):
        dh, grads[l] = _layer_bwd(dh, l, gw, w_in_al[l], sp, saved[l])
    return loss, dh, grads, g_final


ANY = pl.BlockSpec(memory_space=pl.ANY)


def _me():
    return lax.axis_index("x"), lax.axis_index("y"), lax.axis_index("c")


def _other_chips(x, y):
    return [(1 - x, y), (x, 1 - y), (1 - x, 1 - y)]


def _half(ref, axis, c, rows):
    half = rows // 2
    idx = [slice(None)] * axis + [pl.ds(pl.multiple_of(c * half, 16), half)]
    return ref.at[tuple(idx)]


def _gather_weights(shards):
    n = len(shards)

    def body(*refs):
        w, out = refs[:n], refs[n:2 * n]
        send, recv, local = refs[2 * n:]
        x, y, c = _me()
        mine = 2 * x + y
        sibling = (x, y, 1 - c)
        chips = _other_chips(x, y)
        own, sends = [], []
        for t in range(n):
            rows = w[t].shape[1]
            cp = pltpu.make_async_copy(w[t], out[t].at[mine], local.at[t])
            cp.start()
            own.append(cp)
            for k, (px, py) in enumerate(chips):
                cp = pltpu.make_async_remote_copy(
                    src_ref=_half(w[t], 1, c, rows), dst_ref=_half(out[t].at[mine], 1, c, rows),
                    send_sem=send.at[6 * t + k], recv_sem=recv.at[6 * t + k], device_id=(px, py, c), device_id_type=MESH)
                cp.start()
                sends.append(cp)
        for t in range(n):
            rows = w[t].shape[1]
            for k, (px, py) in enumerate(chips):
                landed = _half(out[t].at[2 * px + py], 1, c, rows)
                pltpu.make_async_remote_copy(
                    src_ref=landed, dst_ref=landed, send_sem=send.at[6 * t + k], recv_sem=recv.at[6 * t + k],
                    device_id=(px, py, c), device_id_type=MESH).wait_recv()
                cp = pltpu.make_async_remote_copy(
                    src_ref=landed, dst_ref=landed, send_sem=send.at[6 * t + 3 + k], recv_sem=recv.at[6 * t + 3 + k],
                    device_id=sibling, device_id_type=MESH)
                cp.start()
                sends.append(cp)
        for t in range(n):
            rows = w[t].shape[1]
            for k, (px, py) in enumerate(chips):
                other =_half(out[t].at[2 * px + py], 1, 1 - c, rows)
                pltpu.make_async_remote_copy(
                    src_ref=other, dst_ref=other, send_sem=send.at[6 * t + 3 + k], recv_sem=recv.at[6 * t + 3 + k],
                    device_id=sibling, device_id_type=MESH).wait_recv()
        for cp in sends:
            cp.wait_send()
        for cp in own:
            cp.wait()

    return pl.pallas_call(
        body, name="gather_weights", in_specs=[ANY] * n, out_specs=[ANY] * n,
        out_shape=[_sds((N_CHIPS, *s.shape), s.dtype) for s in shards],
        scratch_shapes=[pltpu.SemaphoreType.DMA((6 * n,)), pltpu.SemaphoreType.DMA((6 * n,)), pltpu.SemaphoreType.DMA((n,))],
    )(*shards)


def _rs_pair(grads):
    n = len(grads)

    def body(*refs):
        g, out = refs[:n], refs[n:2 * n]
        send, recv = refs[2 * n:]
        x, y, c = _me()
        copies = []
        for t in range(n):
            cp = pltpu.make_async_remote_copy(
                src_ref=_half(g[t], 1, 1 - c, g[t].shape[1]), dst_ref=out[t], send_sem=send.at[t], recv_sem=recv.at[t],
                device_id=(x, y, 1 - c), device_id_type=MESH)
            cp.start()
            copies.append(cp)
        for cp in copies:
            cp.wait()

    return pl.pallas_call(
        body, name="rs_pair", in_specs=[ANY] * n, out_specs=[ANY] * n,
        out_shape=[_sds((N_CHIPS, s.shape[1] // 2, s.shape[2]), s.dtype) for s in grads],
        scratch_shapes=[pltpu.SemaphoreType.DMA((n,)), pltpu.SemaphoreType.DMA((n,))],
    )(*grads)


def _rs_chips(parts):
    n = len(parts)

    def body(*refs):
        p, out = refs[:n], refs[n:2 * n]
        send, recv = refs[2 * n:]
        x, y, c = _me()
        copies = []
        for t in range(n):
            for k, (px, py) in enumerate(_other_chips(x, y)):
                cp = pltpu.make_async_remote_copy(
                    src_ref=p[t].at[2 * px + py], dst_ref=out[t].at[k], send_sem=send.at[3 * t + k],
                    recv_sem=recv.at[3 * t + k], device_id=(px, py, c), device_id_type=MESH)
                cp.start()
                copies.append(cp)
        for cp in copies:
            cp.wait()

    return pl.pallas_call(
        body, name="rs_chips", in_specs=[ANY] * n, out_specs=[ANY] * n,
        out_shape=[_sds((3, *s.shape[1:]), s.dtype) for s in parts],
        scratch_shapes=[pltpu.SemaphoreType.DMA((3 * n,)), pltpu.SemaphoreType.DMA((3 * n,))],
    )(*parts)


def _pair_exchange(halves):
    n = len(halves)

    def body(*refs):
        h, out = refs[:n], refs[n:2 * n]
        send, recv, local = refs[2 * n:]
        x, y, c = _me()
        copies = []
        for t in range(n):
            cp = pltpu.make_async_copy(h[t], out[t].at[c], local.at[t])
            cp.start()
            copies.append(cp)
            cp = pltpu.make_async_remote_copy(
                src_ref=h[t], dst_ref=out[t].at[c], send_sem=send.at[t], recv_sem=recv.at[t],
                device_id=(x, y, 1 - c), device_id_type=MESH)
            cp.start()
            copies.append(cp)
        for cp in copies:
            cp.wait()

    return pl.pallas_call(
        body, name="pair_exchange", in_specs=[ANY] * n, out_specs=[ANY] * n,
        out_shape=[_sds((2, *s.shape), s.dtype) for s in halves],
        scratch_shapes=[pltpu.SemaphoreType.DMA((n,)), pltpu.SemaphoreType.DMA((n,)), pltpu.SemaphoreType.DMA((n,))],
    )(*halves)


def _row_tile(rows, cols, itemsize, budget=2 * 1024 * 1024):
    tr = rows
    while tr * cols * itemsize > budget and tr % 32 == 0:
        tr //= 2
    return tr


def _sum_pair(name, g, got, c_arr):
    nchip, rows, cols = g.shape
    half = rows // 2
    tr = _row_tile(half, cols, 4)
    per = half // tr

    def body(c_ref, g_ref, r_ref, o_ref):
        o_ref[...] = (g_ref[...].astype(F32) + r_ref[...].astype(F32)).astype(o_ref.dtype)

    blk = pl.BlockSpec((None, tr, cols), lambda j, i, c_ref: (j, i, 0))
    return pl.pallas_call(
        body, name=name,
        grid_spec=pltpu.PrefetchScalarGridSpec(
            num_scalar_prefetch=1, grid=(nchip, per),
            in_specs=[pl.BlockSpec((None, tr, cols), lambda j, i, c_ref: (j, c_ref[0] * per + i, 0)), blk], out_specs=blk),
        out_shape=_sds((nchip, half, cols), BF16), compiler_params=_cp("parallel", "parallel"),
    )(c_arr, g, got)


def _sum_chips(name, p, got, chip_arr):
    _, rows, cols = p.shape
    tr = _row_tile(rows, cols, 4)

    def body(j_ref, p_ref, r0, r1, r2, o_ref):
        o_ref[...] = ((p_ref[...].astype(F32) + r0[...].astype(F32)) + r1[...].astype(F32)) + r2[...].astype(F32)

    def got_k(k):
        return pl.BlockSpec((None, tr, cols), lambda i, j_ref: (k, i, 0))

    return pl.pallas_call(
        body, name=name,
        grid_spec=pltpu.PrefetchScalarGridSpec(
            num_scalar_prefetch=1, grid=(rows // tr,),
            in_specs=[pl.BlockSpec((None, tr, cols), lambda i, j_ref: (j_ref[0], i, 0)), got_k(0), got_k(1), got_k(2)],
            out_specs=pl.BlockSpec((tr, cols), lambda i, j_ref: (i, 0))),
        out_shape=_sds((rows, cols), F32), compiler_params=_cp("parallel"),
    )(chip_arr, p, got, got, got)


def _reduce_scatter(grads, c_arr, chip_arr):
    got = _rs_pair(grads)
    parts = [_sum_pair(f"sum_pair_{t}", g, r, c_arr) for t, (g, r) in enumerate(zip(grads, got))]
    got = _rs_chips(parts)
    halves = [_sum_chips(f"sum_chips_{t}", p, r, chip_arr) for t, (p, r) in enumerate(zip(parts, got))]
    return [o.reshape(-1, o.shape[-1]) for o in _pair_exchange(halves)]


def _all_reduce_small(name, v):
    rows = v.shape[0]

    def body(v_ref, o_ref, land, send, recv):
        x, y, c = _me()
        mine = 4 * x + 2 * y + c
        copies = []
        for k in range(1, 8):
            kx, ky, kc = k >> 2, (k >> 1) & 1, k & 1
            peer = (x ^ kx, y ^ ky, c ^ kc)
            cp = pltpu.make_async_remote_copy(
                src_ref=v_ref, dst_ref=land.at[mine], send_sem=send.at[k - 1], recv_sem=recv.at[k - 1],
                device_id=peer, device_id_type=MESH)
            cp.start()
            copies.append(cp)
        land[mine] = v_ref[...]
        for k in range(1, 8):
            kx, ky, kc = k >> 2, (k >> 1) & 1, k & 1
            src = 4 * (x ^ kx) + 2 * (y ^ ky) + (c ^ kc)
            pltpu.make_async_remote_copy(
                src_ref=v_ref, dst_ref=land.at[src], send_sem=send.at[k - 1], recv_sem=recv.at[k - 1],
                device_id=(x ^ kx, y ^ ky, c ^ kc), device_id_type=MESH).wait_recv()
        acc = land[0]
        for d in range(1, 8):
            acc = acc + land[d]
        o_ref[...] = acc
        for cp in copies:
            cp.wait_send()

    vm = pl.BlockSpec(memory_space=pltpu.VMEM)
    return pl.pallas_call(
        body, name=name, in_specs=[vm], out_specs=vm, out_shape=_sds((rows, 128), F32),
        scratch_shapes=[pltpu.VMEM((8, rows, 128), F32), pltpu.SemaphoreType.DMA((7,)), pltpu.SemaphoreType.DMA((7,))],
    )(v)


def _adamw(name, w, g, m, v):
    rows, cols = w.shape
    tr = _row_tile(rows, cols, 4, budget=1024 * 1024)
    c1 = 1.0 / (1.0 - ADAM_B1 ** ADAM_STEP)
    c2 = 1.0 / (1.0 - ADAM_B2 ** ADAM_STEP)

    def body(w_ref, g_ref, m_ref, v_ref, d_ref, nm_ref, nv_ref):
        gv = g_ref[...]
        nm = ADAM_B1 * m_ref[...] + (1.0 - ADAM_B1) * gv
        nv = ADAM_B2 * v_ref[...] + (1.0 - ADAM_B2) * (gv * gv)
        d_ref[...] = -ADAM_LR * ((nm * c1) / (jnp.sqrt(nv * c2) + ADAM_EPS) + ADAM_WD * w_ref[...])
        nm_ref[...] = nm
        nv_ref[...] = nv

    blk = pl.BlockSpec((tr, cols), lambda i: (i, 0))
    return pl.pallas_call(
        body, name=name, grid=(rows // tr,), in_specs=[blk] * 4, out_specs=[blk] * 3,
        out_shape=[_sds((rows, cols), F32)] * 3, compiler_params=_cp("parallel"),
    )(w, g, m, v)


def _to_bf16(name, w):
    rows, cols = w.shape
    tr = _row_tile(rows, cols, 4)

    def body(w_ref, o_ref):
        o_ref[...] = w_ref[...].astype(BF16)

    blk = pl.BlockSpec((tr, cols), lambda i: (i, 0))
    return pl.pallas_call(body, name=name, grid=(rows // tr,), in_specs=[blk], out_specs=blk,
                          out_shape=_sds((rows, cols), BF16), compiler_params=_cp("parallel"))(w)


BIG = ("w_in", "w_pool_up", "w_sb_up", "w_gdn_up", "w_out", "w_ff1", "w_ff2")
SMALL = (("attn_norm", (D,)), ("pool_w", (4, 128, 128)), ("pool_scale", (POOL_W,)), ("gdn_a_log", (HEADS,)),
         ("gdn_dt_bias", (HEADS,)), ("gdn_norm", (HD,)), ("mlp_norm", (D,)))


def _rows128(a):
    flat = a.reshape(-1)
    pad = (-flat.shape[0]) % 128
    return jnp.pad(flat, (0, pad)).reshape(-1, 128)


def _pack(parts):
    packed = jnp.concatenate([_rows128(p) for p in parts], axis=0)
    return jnp.pad(packed, ((0, (-packed.shape[0]) % 8), (0, 0)))


def _unpack(packed, shapes):
    out, r = [], 0
    for shp in shapes:
        size = 1
        for s in shp:
            size *= s
        nr = -(-size // 128)
        out.append(packed[r:r + nr].reshape(-1)[:size].reshape(shp))
        r += nr
    return out


def kernel(x, attn_norm, w_in, pool_w, pool_scale, gdn_conv, gdn_a_log, gdn_dt_bias, gdn_norm, w_pool_up, w_sb_up, w_gdn_up, w_out, mlp_norm, w_ff1, w_ff2, final_norm, loss_target, m_attn_norm, m_w_in, m_pool_w, m_pool_scale, m_gdn_conv, m_gdn_a_log, m_gdn_dt_bias, m_gdn_norm, m_w_pool_up, m_w_sb_up, m_w_gdn_up, m_w_out, m_mlp_norm, m_w_ff1, m_w_ff2, m_final_norm, v_attn_norm, v_w_in, v_pool_w, v_pool_scale, v_gdn_conv, v_gdn_a_log, v_gdn_dt_bias, v_gdn_norm, v_w_pool_up, v_w_sb_up, v_w_gdn_up, v_w_out, v_mlp_norm, v_w_ff1, v_w_ff2, v_final_norm):
    weights = dict(attn_norm=attn_norm, w_in=w_in, pool_w=pool_w, pool_scale=pool_scale, gdn_conv=gdn_conv,
                   gdn_a_log=gdn_a_log, gdn_dt_bias=gdn_dt_bias, gdn_norm=gdn_norm, w_pool_up=w_pool_up, w_sb_up=w_sb_up,
                   w_gdn_up=w_gdn_up, w_out=w_out, mlp_norm=mlp_norm, w_ff1=w_ff1, w_ff2=w_ff2, final_norm=final_norm)
    mom1 = dict(attn_norm=m_attn_norm, w_in=m_w_in, pool_w=m_pool_w, pool_scale=m_pool_scale, gdn_conv=m_gdn_conv,
                gdn_a_log=m_gdn_a_log, gdn_dt_bias=m_gdn_dt_bias, gdn_norm=m_gdn_norm, w_pool_up=m_w_pool_up,
                w_sb_up=m_w_sb_up, w_gdn_up=m_w_gdn_up, w_out=m_w_out, mlp_norm=m_mlp_norm, w_ff1=m_w_ff1, w_ff2=m_w_ff2,
                final_norm=m_final_norm)
    mom2 = dict(attn_norm=v_attn_norm, w_in=v_w_in, pool_w=v_pool_w, pool_scale=v_pool_scale, gdn_conv=v_gdn_conv,
                gdn_a_log=v_gdn_a_log, gdn_dt_bias=v_gdn_dt_bias, gdn_norm=v_gdn_norm, w_pool_up=v_w_pool_up,
                w_sb_up=v_w_sb_up, w_gdn_up=v_w_gdn_up, w_out=v_w_out, mlp_norm=v_mlp_norm, w_ff1=v_w_ff1, w_ff2=v_w_ff2,
                final_norm=v_final_norm)
    xi, yi, ci = lax.axis_index("x"), lax.axis_index("y"), lax.axis_index("c")
    chip = 2 * xi + yi
    c_arr = jnp.reshape(ci, (1,)).astype(jnp.int32)
    chip_arr = jnp.reshape(chip, (1,)).astype(jnp.int32)

    shards = [_to_bf16("cast_" + nm, weights[nm].reshape(-1, weights[nm].shape[-1])).reshape(weights[nm].shape) for nm in BIG]
    gw = dict(zip(BIG, _gather_weights(shards)))
    conv_cols = gdn_conv.shape[-1]
    conv_place = lax.dynamic_update_slice(jnp.zeros((2, GDN_CONV, N_CHIPS * conv_cols), F32),
                                          jnp.where(ci == 0, gdn_conv, 0.0), (0, 0, chip * conv_cols))
    conv_full = _all_reduce_small("gather_conv", _rows128(conv_place)).reshape(2, GDN_CONV, N_CHIPS * conv_cols)
    w_in_full = jnp.transpose(gw["w_in"], (1, 2, 0, 3)).reshape(2, D, N_IN)
    w_in_al = [_align_w_in(w_in_full[l]) for l in range(2)]
    sp = dict(attn_norm=attn_norm.reshape(2, 1, D), pool_w=pool_w, pool_scale=pool_scale.reshape(2, 1, POOL_W),
              conv=conv_full, a_log=jnp.stack([_row128(gdn_a_log[l]) for l in range(2)]),
              dt_bias=jnp.stack([_row128(gdn_dt_bias[l]) for l in range(2)]), gdn_norm=gdn_norm.reshape(2, 1, HD),
              mlp_norm=mlp_norm.reshape(2, 1, D), final_norm=final_norm.reshape(1, D))

    loss, grad_x, grads, g_final = _local_step(x[0], loss_target[0], gw, w_in_al, sp)
    loss = lax.psum(loss[0, 0], ("x", "y", "c"))

    big_grads = {nm: [] for nm in BIG}
    for l in range(2):
        g = grads[l]
        w_in_g = _unalign_w_in(g["w_al"]).reshape(D, N_CHIPS, N_IN // N_CHIPS)
        per_layer = [jnp.transpose(w_in_g, (1, 0, 2))] + [g[nm] for nm in BIG[1:]]
        for nm, red in zip(BIG, _reduce_scatter(per_layer, c_arr, chip_arr)):
            big_grads[nm].append(red)
    small_parts, small_shapes = [], []
    for l in range(2):
        g = grads[l]
        for nm, shp in SMALL:
            key = {"gdn_a_log": "a_log", "gdn_dt_bias": "dt_bias"}.get(nm, nm)
            val = g[key]
            small_parts.append(val[0, :HEADS] if nm in ("gdn_a_log", "gdn_dt_bias") else val)
            small_shapes.append(shp)
        small_parts.append(g["conv"])
        small_shapes.append((GDN_CONV, N_CHIPS * conv_cols))
    small_parts.append(g_final)
    small_shapes.append((D,))
    reduced = _unpack(_all_reduce_small("reduce_small", _pack(small_parts)), small_shapes)
    per = len(SMALL) + 1
    grad = {}
    for i, (nm, _) in enumerate(SMALL):
        grad[nm] = jnp.stack([reduced[i], reduced[per + i]])
    conv_g = jnp.stack([reduced[per - 1], reduced[2 * per - 1]])
    grad["gdn_conv"] = lax.dynamic_slice(conv_g, (0, 0, chip * conv_cols), (2, GDN_CONV, conv_cols))
    grad["final_norm"] = reduced[-1]
    for nm in BIG:
        grad[nm] = jnp.stack(big_grads[nm]).reshape(weights[nm].shape)

    delta, new_m, new_v = {}, {}, {}
    for nm in BIG:
        shp = weights[nm].shape
        flat = lambda a: a.reshape(-1, shp[-1])
        d, nm1, nv1 = _adamw("adamw_" + nm, flat(weights[nm]), flat(grad[nm]), flat(mom1[nm]), flat(mom2[nm]))
        delta[nm], new_m[nm], new_v[nm] = d.reshape(shp), nm1.reshape(shp), nv1.reshape(shp)
    small_names = [nm for nm, _ in SMALL] + ["gdn_conv", "final_norm"]
    packs = [_pack([src[nm] for nm in small_names]) for src in (weights, grad, mom1, mom2)]
    outs = _adamw("adamw_small", *packs)
    shapes = [weights[nm].shape for nm in small_names]
    for dst, packed in zip((delta, new_m, new_v), outs):
        for nm, val in zip(small_names, _unpack(packed, shapes)):
            dst[nm] = val

    order = ("attn_norm", "w_in", "pool_w", "pool_scale", "gdn_conv", "gdn_a_log", "gdn_dt_bias", "gdn_norm", "w_pool_up",
             "w_sb_up", "w_gdn_up", "w_out", "mlp_norm", "w_ff1", "w_ff2", "final_norm")
    return (loss, grad_x[None], *[grad[n] for n in order], *[delta[n] for n in order], *[new_m[n] for n in order],
            *[new_v[n] for n in order])
```

```python
import functools

import jax
import jax.numpy as jnp
from jax import lax
from jax.experimental import pallas as pl
from jax.experimental.pallas import tpu as pltpu

F32, BF16 = jnp.float32, jnp.bfloat16
HIGHEST = lax.Precision.HIGHEST
MESH = pl.DeviceIdType.MESH

D = 2048
EPS = 1e-6
POOL_WINDOWS = (2, 4, 8, 16)
POOL_W, SB_W, GDN_W = 512, 768, 768
HEADS, HD = 6, 128
SB_BLOCK = 128
GDN_CHUNK = 64
D_FF = 4 * D
N_IN = 12044
N_CHIPS = 4
OFF_P, OFF_SB, OFF_GQKV, OFF_Z, OFF_AB, OFF_GATE = 0, 512, 2816, 5120, 5888, 6144
AB_END = 5900
N_AL = 12288
VMEM_LIMIT = 48 * 1024 * 1024

ADAM_LR, ADAM_B1, ADAM_B2, ADAM_EPS, ADAM_WD, ADAM_STEP = 0.001, 0.9, 0.999, 1e-08, 0.01, 10

NT = (((1,), (1,)), ((), ()))
TN = (((0,), (0,)), ((), ()))


def _cp(*sem):
    return pltpu.CompilerParams(dimension_semantics=sem, vmem_limit_bytes=VMEM_LIMIT)


def _dot(a, b, dims=None, precision=None):
    if dims is None:
        dims = (((a.ndim - 1,), (0,)), ((), ()))
    return lax.dot_general(a, b, dims, precision=precision, preferred_element_type=F32)


def _hdot(a, b, dims=None):
    return _dot(a, b, dims, precision=HIGHEST)


def _bdot(a, b, dims=None):
    return _dot(a.astype(BF16), b.astype(BF16), dims)


def _mm(name, a, b, *, m, n, k, tm, tn, tk, a_spec, b_spec, dims, out_shapes, out_specs,
        extras=(), extra_specs=(), epilogue=None):
    nk = k // tk
    ne, no = len(extras), len(out_shapes)

    def body(*refs):
        a_ref, b_ref = refs[0], refs[1]
        ex = refs[2:2 + ne]
        outs = refs[2 + ne:2 + ne + no]
        acc = refs[-1]
        kk = pl.program_id(2)

        @pl.when(kk == 0)
        def _():
            acc[...] = jnp.zeros_like(acc)

        acc[...] += _dot(a_ref[...].astype(BF16), b_ref[...].astype(BF16), dims)

        @pl.when(kk == nk - 1)
        def _():
            r = acc[...]
            res = epilogue(r, *[e[...] for e in ex]) if epilogue is not None else (r,)
            for o, v in zip(outs, res):
                o[...] = v.astype(o.dtype)

    return pl.pallas_call(
        body, name=name, grid=(m // tm, n // tn, nk),
        in_specs=[a_spec, b_spec, *extra_specs], out_specs=out_specs, out_shape=out_shapes,
        scratch_shapes=[pltpu.VMEM((tm, tn), F32)],
        compiler_params=_cp("parallel", "parallel", "arbitrary"),
    )(a, b, *extras)


def _a_plain(tm, tk):
    return pl.BlockSpec((tm, tk), lambda i, j, kk: (i, kk))


def _a_trans(tm, tk):
    return pl.BlockSpec((tk, tm), lambda i, j, kk: (kk, i))


def _b_plain(tk, tn):
    return pl.BlockSpec((tk, tn), lambda i, j, kk: (kk, j))


def _b_trans(tk, tn):
    return pl.BlockSpec((tn, tk), lambda i, j, kk: (j, kk))


def _o_plain(tm, tn):
    return pl.BlockSpec((tm, tn), lambda i, j, kk: (i, j))


def _o_colshard(tm, tn, ns_cols):
    per = ns_cols // tn
    return pl.BlockSpec((None, tm, tn), lambda i, j, kk: (j // per, i, j % per))


def _w_cols(l, tk, tn, ns):
    per = ns // tn
    return pl.BlockSpec((None, None, tk, tn), lambda i, j, kk: (j // per, l, kk, j % per))


def _w_cols_t(l, tk, tn, ns):
    per = ns // tk
    return pl.BlockSpec((None, None, tn, tk), lambda i, j, kk: (kk // per, l, j, kk % per))


def _w_rows(l, tk, tn, ks):
    per = ks // tk
    return pl.BlockSpec((None, None, tk, tn), lambda i, j, kk: (kk // per, l, kk % per, j))


def _w_rows_t(l, tk, tn, ks):
    per = ks // tn
    return pl.BlockSpec((None, None, tn, tk), lambda i, j, kk: (j // per, l, j % per, kk))


def _sds(shape, dtype):
    return jax.ShapeDtypeStruct(shape, dtype)


def _rms_fwd(name, x, gain):
    t = x.shape[0]
    tt = min(256, t)

    def body(x_ref, g_ref, u_ref):
        xv = x_ref[...]
        r = lax.rsqrt(jnp.mean(xv * xv, axis=-1, keepdims=True) + EPS)
        u_ref[...] = (xv * r * g_ref[...]).astype(u_ref.dtype)

    return pl.pallas_call(
        body, name=name, grid=(t // tt,),
        in_specs=[pl.BlockSpec((tt, D), lambda i: (i, 0)), pl.BlockSpec((1, D), lambda i: (0, 0))],
        out_specs=pl.BlockSpec((tt, D), lambda i: (i, 0)), out_shape=_sds((t, D), BF16),
        compiler_params=_cp("parallel"),
    )(x, gain)


def _rms_bwd(name, du, x, gain, dres):
    t = x.shape[0]
    tt = min(256, t)

    def body(du_ref, x_ref, g_ref, dres_ref, dx_ref, dg_ref):
        @pl.when(pl.program_id(0) == 0)
        def _():
            dg_ref[...] = jnp.zeros_like(dg_ref)

        xv, duv = x_ref[...], du_ref[...]
        r = lax.rsqrt(jnp.mean(xv * xv, axis=-1, keepdims=True) + EPS)
        nx = xv * r
        dn = duv * g_ref[...]
        dg_ref[...] += jnp.sum(duv * nx, axis=0, keepdims=True)
        dx_ref[...] = dres_ref[...] + r * (dn - nx * jnp.mean(dn * nx, axis=-1, keepdims=True))

    row = pl.BlockSpec((tt, D), lambda i: (i, 0))
    vec = pl.BlockSpec((1, D), lambda i: (0, 0))
    return pl.pallas_call(
        body, name=name, grid=(t // tt,), in_specs=[row, row, vec, row], out_specs=[row, vec],
        out_shape=[_sds((t, D), F32), _sds((1, D), F32)], compiler_params=_cp("arbitrary"),
    )(du, x, gain, dres)


def _loss_head(x, gain, target):
    t = x.shape[0]
    tt = min(256, t)

    def body(x_ref, g_ref, t_ref, loss_ref, dx_ref, dg_ref):
        @pl.when(pl.program_id(0) == 0)
        def _():
            dg_ref[...] = jnp.zeros_like(dg_ref)
            loss_ref[...] = jnp.zeros_like(loss_ref)

        xv = x_ref[...]
        r = lax.rsqrt(jnp.mean(xv * xv, axis=-1, keepdims=True) + EPS)
        nx = xv * r
        err = nx * g_ref[...] - t_ref[...]
        loss_ref[...] += 0.5 * jnp.sum(jnp.mean(err * err, axis=-1, keepdims=True), axis=0, keepdims=True)
        dy = err * (1.0 / D)
        dn = dy * g_ref[...]
        dg_ref[...] += jnp.sum(dy * nx, axis=0, keepdims=True)
        dx_ref[...] = r * (dn - nx * jnp.mean(dn * nx, axis=-1, keepdims=True))

    row = pl.BlockSpec((tt, D), lambda i: (i, 0))
    vec = pl.BlockSpec((1, D), lambda i: (0, 0))
    one = pl.BlockSpec((1, 1), lambda i: (0, 0))
    return pl.pallas_call(
        body, name="loss_head", grid=(t // tt,), in_specs=[row, vec, row], out_specs=[one, row, vec],
        out_shape=[_sds((1, 1), F32), _sds((t, D), F32), _sds((1, D), F32)], compiler_params=_cp("arbitrary"),
    )(x, gain, target)


def _shift_down(v, s, t_idx):
    return jnp.where(t_idx >= s, pltpu.roll(v, s, 0), 0.0)


def _shift_up(v, s, t_idx, t):
    return jnp.where(t_idx < t - s, pltpu.roll(v, t - s, 0), 0.0)


def _pool_d(p, g, t_idx):
    s = p
    for step in range(g + 1):
        s = s + _shift_down(s, 1 << step, t_idx)
    cnt = jnp.minimum(t_idx + 1, POOL_WINDOWS[g]).astype(F32)
    return s / cnt - p, cnt


def _pool_fwd(proj, pool_w, pool_scale):
    t = proj.shape[0]
    g128 = POOL_W // len(POOL_WINDOWS)

    def body(p_ref, w_ref, s_ref, y_ref):
        t_idx = lax.broadcasted_iota(jnp.int32, (t, g128), 0)
        for g in range(len(POOL_WINDOWS)):
            sl = slice(g * g128, (g + 1) * g128)
            d, _ = _pool_d(p_ref[:, sl], g, t_idx)
            y_ref[:, sl] = (_bdot(d, w_ref[g]) * s_ref[:, sl]).astype(y_ref.dtype)

    return pl.pallas_call(
        body, name="pool_fwd", grid=(1,),
        in_specs=[pl.BlockSpec((t, POOL_W), lambda i: (0, OFF_P // POOL_W)),
                  pl.BlockSpec((4, g128, g128), lambda i: (0, 0, 0)), pl.BlockSpec((1, POOL_W), lambda i: (0, 0))],
        out_specs=pl.BlockSpec((t, POOL_W), lambda i: (0, 0)), out_shape=_sds((t, POOL_W), BF16),
        compiler_params=_cp("arbitrary"),
    )(proj, pool_w, pool_scale)


def _pool_bwd(proj, pool_w, pool_scale, dy):
    t = proj.shape[0]
    g128 = POOL_W // len(POOL_WINDOWS)

    def body(p_ref, w_ref, s_ref, dy_ref, dp_ref, dw_ref, ds_ref):
        t_idx = lax.broadcasted_iota(jnp.int32, (t, g128), 0)
        for g in range(len(POOL_WINDOWS)):
            sl = slice(g * g128, (g + 1) * g128)
            d, cnt = _pool_d(p_ref[:, sl], g, t_idx)
            dyv = dy_ref[:, sl].astype(F32)
            ds_ref[:, sl] = jnp.sum(dyv * _bdot(d, w_ref[g]), axis=0, keepdims=True)
            dys = dyv * s_ref[:, sl]
            dw_ref[g] = _bdot(d, dys, TN)
            dd = _bdot(dys, w_ref[g], NT)
            s = dd / cnt
            for step in range(g + 1):
                s = s + _shift_up(s, 1 << step, t_idx, t)
            dp_ref[:, sl] = (s - dd).astype(dp_ref.dtype)

    return pl.pallas_call(
        body, name="pool_bwd", grid=(1,),
        in_specs=[pl.BlockSpec((t, POOL_W), lambda i: (0, OFF_P // POOL_W)),
                  pl.BlockSpec((4, g128, g128), lambda i: (0, 0, 0)), pl.BlockSpec((1, POOL_W), lambda i: (0, 0)),
                  pl.BlockSpec((t, POOL_W), lambda i: (0, 0))],
        out_specs=[pl.BlockSpec((t, POOL_W), lambda i: (0, 0)), pl.BlockSpec((4, g128, g128), lambda i: (0, 0, 0)),
                   pl.BlockSpec((1, POOL_W), lambda i: (0, 0))],
        out_shape=[_sds((t, POOL_W), BF16), _sds((4, g128, g128), F32), _sds((1, POOL_W), F32)],
        compiler_params=_cp("arbitrary"),
    )(proj, pool_w, pool_scale, dy)


def _sb_block(q, kb, i, j):
    row = lax.broadcasted_iota(jnp.int32, (SB_BLOCK, SB_BLOCK), 0)
    col = lax.broadcasted_iota(jnp.int32, (SB_BLOCK, SB_BLOCK), 1)
    z = _dot(q, kb, NT)
    mask = (col + j * SB_BLOCK) < (row + i * SB_BLOCK)
    return z, mask, jax.nn.log_sigmoid(z)


def _sb_specs(t):
    qb = SB_BLOCK
    q_spec = pl.BlockSpec((qb, HD), lambda h, i: (i, OFF_SB // HD + h))
    k_spec = pl.BlockSpec((t, HD), lambda h, i: (0, OFF_SB // HD + HEADS + h))
    v_spec = pl.BlockSpec((t, HD), lambda h, i: (0, OFF_SB // HD + 2 * HEADS + h))
    return q_spec, k_spec, v_spec


def _sb_fwd(proj):
    t = proj.shape[0]
    scale = HD ** -0.5

    def body(q_ref, k_ref, v_ref, o_ref):
        i = pl.program_id(1)
        q = (q_ref[...] * scale).astype(BF16)
        row = lax.broadcasted_iota(jnp.int32, (SB_BLOCK, SB_BLOCK), 0)
        col = lax.broadcasted_iota(jnp.int32, (SB_BLOCK, SB_BLOCK), 1)
        later = (row > col).astype(F32)

        def step(jj, carry):
            acc, run = carry
            j = i - jj
            off = pl.multiple_of(j * SB_BLOCK, SB_BLOCK)
            kb = k_ref[pl.ds(off, SB_BLOCK), :].astype(BF16)
            vb = v_ref[pl.ds(off, SB_BLOCK), :].astype(BF16)
            z, mask, lsz = _sb_block(q, kb, i, j)
            ls = jnp.where(mask, lsz - z, 0.0)
            log_later = _hdot(ls, later) + run
            a = jnp.where(mask, jnp.exp(lsz + log_later), 0.0)
            return acc + _dot(a.astype(BF16), vb), run + jnp.sum(ls, axis=1, keepdims=True)

        acc, _ = lax.fori_loop(0, i + 1, step, (jnp.zeros((SB_BLOCK, HD), F32), jnp.zeros((SB_BLOCK, 1), F32)))
        o_ref[...] = acc.astype(o_ref.dtype)

    return pl.pallas_call(
        body, name="sb_fwd", grid=(HEADS, t // SB_BLOCK), in_specs=list(_sb_specs(t)),
        out_specs=pl.BlockSpec((SB_BLOCK, HD), lambda h, i: (i, h)), out_shape=_sds((t, SB_W), BF16),
        compiler_params=_cp("parallel", "arbitrary"),
    )(proj, proj, proj)


def _sb_bwd(proj, dy):
    t = proj.shape[0]
    nq = t // SB_BLOCK
    scale = HD ** -0.5

    def body(q_ref, k_ref, v_ref, do_ref, dq_ref, dk_ref, dv_ref, z_scr, e_scr):
        i = pl.program_id(1)

        @pl.when(i == 0)
        def _():
            dk_ref[...] = jnp.zeros_like(dk_ref)
            dv_ref[...] = jnp.zeros_like(dv_ref)

        q = (q_ref[...] * scale).astype(BF16)
        do = do_ref[...].astype(BF16)
        row = lax.broadcasted_iota(jnp.int32, (SB_BLOCK, SB_BLOCK), 0)
        col = lax.broadcasted_iota(jnp.int32, (SB_BLOCK, SB_BLOCK), 1)
        later = (row > col).astype(F32)
        earlier = (row < col).astype(F32)

        def down(jj, run):
            j = i - jj
            off = pl.multiple_of(j * SB_BLOCK, SB_BLOCK)
            kb = k_ref[pl.ds(off, SB_BLOCK), :].astype(BF16)
            vb = v_ref[pl.ds(off, SB_BLOCK), :].astype(BF16)
            z, mask, lsz = _sb_block(q, kb, i, j)
            ls = jnp.where(mask, lsz - z, 0.0)
            log_later = _hdot(ls, later) + run
            a = jnp.where(mask, jnp.exp(lsz + log_later), 0.0)
            z_scr[j] = z
            e_scr[j] = a * _dot(do, vb, NT)
            dv_ref[pl.ds(off, SB_BLOCK), :] += _dot(a.astype(BF16), do, TN)
            return run + jnp.sum(ls, axis=1, keepdims=True)

        lax.fori_loop(0, i + 1, down, jnp.zeros((SB_BLOCK, 1), F32))

        def up(j, carry):
            dq, run = carry
            off = pl.multiple_of(j * SB_BLOCK, SB_BLOCK)
            kb = k_ref[pl.ds(off, SB_BLOCK), :].astype(BF16)
            z, e = z_scr[j], e_scr[j]
            mask = (col + j * SB_BLOCK) < (row + i * SB_BLOCK)
            before = _hdot(e, earlier) + run
            dz = jnp.where(mask, e * jax.nn.sigmoid(-z) - before * jax.nn.sigmoid(z), 0.0).astype(BF16)
            dk_ref[pl.ds(off, SB_BLOCK), :] += _dot(dz, q, TN)
            return dq + _dot(dz, kb), run + jnp.sum(e, axis=1, keepdims=True)

        dq, _ = lax.fori_loop(0, i + 1, up, (jnp.zeros((SB_BLOCK, HD), F32), jnp.zeros((SB_BLOCK, 1), F32)))
        dq_ref[...] = (dq * scale).astype(dq_ref.dtype)

    blk = pl.BlockSpec((SB_BLOCK, HD), lambda h, i: (i, h))
    seq = pl.BlockSpec((t, HD), lambda h, i: (0, h))
    return pl.pallas_call(
        body, name="sb_bwd", grid=(HEADS, nq), in_specs=[*_sb_specs(t), blk], out_specs=[blk, seq, seq],
        out_shape=[_sds((t, SB_W), BF16), _sds((t, SB_W), F32), _sds((t, SB_W), F32)],
        scratch_shapes=[pltpu.VMEM((nq, SB_BLOCK, SB_BLOCK), F32), pltpu.VMEM((nq, SB_BLOCK, SB_BLOCK), F32)],
        compiler_params=_cp("parallel", "arbitrary"),
    )(proj, proj, proj, dy)


CONV_TILE = 256
GDN_CONV = 4


def _conv_pre(x, w_ref, t_idx):
    pre = w_ref[GDN_CONV - 1:GDN_CONV, :] * x
    for s in range(1, GDN_CONV):
        pre = pre + w_ref[GDN_CONV - 1 - s:GDN_CONV - s, :] * _shift_down(x, s, t_idx)
    return pre


def _conv_fwd(proj, conv_w):
    t = proj.shape[0]
    width = conv_w.shape[1]

    def body(x_ref, w_ref, y_ref):
        t_idx = lax.broadcasted_iota(jnp.int32, (t, CONV_TILE), 0)
        pre = _conv_pre(x_ref[...], w_ref, t_idx)
        y_ref[...] = pre * jax.nn.sigmoid(pre)

    return pl.pallas_call(
        body, name="conv_fwd", grid=(width // CONV_TILE,),
        in_specs=[pl.BlockSpec((t, CONV_TILE), lambda c: (0, OFF_GQKV // CONV_TILE + c)),
                  pl.BlockSpec((GDN_CONV, CONV_TILE), lambda c: (0, c))],
        out_specs=pl.BlockSpec((t, CONV_TILE), lambda c: (0, c)), out_shape=_sds((t, width), F32),
        compiler_params=_cp("parallel"),
    )(proj, conv_w)


def _conv_bwd(proj, conv_w, dc, part):
    t = proj.shape[0]
    width = dc.shape[1]
    per = width // CONV_TILE

    def body(x_ref, w_ref, dc_ref, dx_ref, dw_ref):
        t_idx = lax.broadcasted_iota(jnp.int32, (t, CONV_TILE), 0)
        x = x_ref[...]
        pre = _conv_pre(x, w_ref, t_idx)
        sg = jax.nn.sigmoid(pre)
        dpre = dc_ref[...] * (sg * (1.0 + pre * (1.0 - sg)))
        dx = w_ref[GDN_CONV - 1:GDN_CONV, :] * dpre
        dw_ref[GDN_CONV - 1:GDN_CONV, :] = jnp.sum(dpre * x, axis=0, keepdims=True)
        for s in range(1, GDN_CONV):
            dx = dx + w_ref[GDN_CONV - 1 - s:GDN_CONV - s, :] * _shift_up(dpre, s, t_idx, t)
            dw_ref[GDN_CONV - 1 - s:GDN_CONV - s, :] = jnp.sum(dpre * _shift_down(x, s, t_idx), axis=0, keepdims=True)
        dx_ref[...] = dx.astype(dx_ref.dtype)

    return pl.pallas_call(
        body, name="conv_bwd", grid=(per,),
        in_specs=[pl.BlockSpec((t, CONV_TILE), lambda c: (0, OFF_GQKV // CONV_TILE + part * per + c)),
                  pl.BlockSpec((GDN_CONV, CONV_TILE), lambda c: (0, part * per + c)),
                  pl.BlockSpec((t, CONV_TILE), lambda c: (0, c))],
        out_specs=[pl.BlockSpec((t, CONV_TILE), lambda c: (0, c)), pl.BlockSpec((GDN_CONV, CONV_TILE), lambda c: (0, c))],
        out_shape=[_sds((t, width), BF16), _sds((GDN_CONV, width), F32)],
        compiler_params=_cp("parallel"),
    )(proj, conv_w, dc)


def _lane_pick(v, lane):
    idx = lax.broadcasted_iota(jnp.int32, v.shape, 1)
    return jnp.sum(jnp.where(idx == lane, v, 0.0), axis=1, keepdims=True)


def _gdn_prep(cq, ck, cv, ab, alog_row, dtb_row, h):
    c = GDN_CHUNK
    row = lax.broadcasted_iota(jnp.int32, (c, c), 0)
    col = lax.broadcasted_iota(jnp.int32, (c, c), 1)
    incl, strict, eye = row >= col, row > col, row == col
    a_col, b_col = _lane_pick(ab, h), _lane_pick(ab, HEADS + h)
    a_log, dt_bias = _lane_pick(alog_row, h), _lane_pick(dtb_row, h)
    qn = cq * lax.rsqrt(jnp.sum(cq * cq, axis=-1, keepdims=True) + EPS) * (HD ** -0.5)
    kn = ck * lax.rsqrt(jnp.sum(ck * ck, axis=-1, keepdims=True) + EPS)
    la_col = -jnp.exp(a_log) * jax.nn.softplus(a_col + dt_bias)
    beta = jax.nn.sigmoid(b_col)
    la_row = jnp.sum(jnp.where(eye, la_col, 0.0), axis=0, keepdims=True)
    g_col = jnp.sum(jnp.where(incl, la_row, 0.0), axis=1, keepdims=True)
    g_row = jnp.sum(jnp.where(row <= col, la_col, 0.0), axis=0, keepdims=True)
    g_last = jnp.sum(la_col, axis=0, keepdims=True)
    gamma = jnp.where(incl, jnp.exp(jnp.where(incl, g_col - g_row, 0.0)), 0.0)
    lower = jnp.where(strict, beta * _hdot(kn, kn, NT) * gamma, 0.0)
    inv = jnp.where(eye, 1.0, 0.0) - lower
    pw = _hdot(lower, lower)
    for step in range(5):
        inv = inv + _hdot(inv, pw)
        if step < 4:
            pw = _hdot(pw, pw)
    u = _hdot(inv, cv * beta)
    w = _hdot(inv, kn * (beta * jnp.exp(g_col)))
    qk = _hdot(qn, kn, NT) * gamma
    return u, w, qk, qn * jnp.exp(g_col), kn * jnp.exp(g_last - g_col), jnp.exp(g_last)


def _gdn_post(o, z, gain):
    y = o * lax.rsqrt(jnp.mean(o * o, axis=-1, keepdims=True) + EPS) * gain
    return y * (z * jax.nn.sigmoid(z))


def _gdn_specs(nrev):
    c = GDN_CHUNK

    def ch(n):
        return nrev - 1 - n if nrev else n

    def cblk(off):
        return pl.BlockSpec((c, HD), lambda n, h: (ch(n), off + h))

    row = pl.BlockSpec((1, HD), lambda n, h: (0, 0))
    return cblk, row, ch


def _gdn_fwd(cqkv, proj, a_log, dt_bias, gain):
    t = proj.shape[0]
    c, nc = GDN_CHUNK, t // GDN_CHUNK
    cblk, row, _ = _gdn_specs(0)

    def body(cq_ref, ck_ref, cv_ref, ab_ref, z_ref, al_ref, dt_ref, g_ref, y_ref, sprev_ref, s_scr):
        n, h = pl.program_id(0), pl.program_id(1)

        @pl.when(n == 0)
        def _():
            s_scr[h] = jnp.zeros((HD, HD), F32)

        u, w, qk, qd, kd, dec = _gdn_prep(cq_ref[...], ck_ref[...], cv_ref[...], ab_ref[...], al_ref[...], dt_ref[...], h)
        s = s_scr[h]
        sprev_ref[...] = s
        v_new = u - _hdot(w, s)
        o = _hdot(qd, s) + _hdot(qk, v_new)
        s_scr[h] = s * dec + _hdot(kd, v_new, TN)
        y_ref[...] = _gdn_post(o, z_ref[...], g_ref[...]).astype(y_ref.dtype)

    return pl.pallas_call(
        body, name="gdn_fwd", grid=(nc, HEADS),
        in_specs=[cblk(0), cblk(HEADS), cblk(2 * HEADS),
                  pl.BlockSpec((c, HD), lambda n, h: (n, OFF_AB // HD)), cblk(OFF_Z // HD), row, row, row],
        out_specs=[cblk(0), pl.BlockSpec((None, None, HD, HD), lambda n, h: (n, h, 0, 0))],
        out_shape=[_sds((t, GDN_W), BF16), _sds((nc, HEADS, HD, HD), F32)],
        scratch_shapes=[pltpu.VMEM((HEADS, HD, HD), F32)],
        compiler_params=_cp("arbitrary", "arbitrary"),
    )(cqkv, cqkv, cqkv, proj, proj, a_log, dt_bias, gain)


def _gdn_bwd(cqkv, proj, a_log, dt_bias, gain, sprev, dy):
    t = proj.shape[0]
    c, nc = GDN_CHUNK, t // GDN_CHUNK
    cblk, row, ch = _gdn_specs(nc)

    def body(cq_ref, ck_ref, cv_ref, ab_ref, z_ref, al_ref, dt_ref, g_ref, sp_ref, dy_ref,
             dcq_ref, dck_ref, dcv_ref, dab_ref, dz_ref, dal_ref, ddt_ref, dg_ref, ds_scr):
        n, h = pl.program_id(0), pl.program_id(1)

        @pl.when(n == 0)
        def _():
            ds_scr[h] = jnp.zeros((HD, HD), F32)

        @pl.when((n == 0) & (h == 0))
        def _():
            dal_ref[...] = jnp.zeros_like(dal_ref)
            ddt_ref[...] = jnp.zeros_like(ddt_ref)
            dg_ref[...] = jnp.zeros_like(dg_ref)

        @pl.when(h == 0)
        def _():
            dab_ref[...] = jnp.zeros_like(dab_ref)

        (u, w, qk, qd, kd, dec), prep_vjp = jax.vjp(
            lambda cq, ck, cv, ab, al, dt: _gdn_prep(cq, ck, cv, ab, al, dt, h),
            cq_ref[...], ck_ref[...], cv_ref[...], ab_ref[...], al_ref[...], dt_ref[...])
        s = sp_ref[...]
        v_new = u - _hdot(w, s)
        o = _hdot(qd, s) + _hdot(qk, v_new)
        _, post_vjp = jax.vjp(_gdn_post, o, z_ref[...], g_ref[...])
        do, dz, dgain = post_vjp(dy_ref[...].astype(F32))
        ds_next = ds_scr[h]
        d_vnew = _hdot(qk, do, TN) + _hdot(kd, ds_next)
        d_qk = _hdot(do, v_new, NT)
        d_qd = _hdot(do, s, NT)
        d_kd = _hdot(v_new, ds_next, NT)
        d_dec = jnp.sum(jnp.sum(s * ds_next, axis=1, keepdims=True), axis=0, keepdims=True)
        ds_scr[h] = dec * ds_next + _hdot(qd, do, TN) - _hdot(w, d_vnew, TN)
        d_w = -_hdot(d_vnew, s, NT)
        dcq, dck, dcv, dab, dal, ddt = prep_vjp((d_vnew, d_w, d_qk, d_qd, d_kd, d_dec))
        dcq_ref[...] = dcq
        dck_ref[...] = dck
        dcv_ref[...] = dcv
        dz_ref[...] = dz.astype(dz_ref.dtype)
        dab_ref[...] += dab
        dal_ref[...] += dal
        ddt_ref[...] += ddt
        dg_ref[...] += dgain

    hblk = pl.BlockSpec((c, HD), lambda n, h: (ch(n), h))
    return pl.pallas_call(
        body, name="gdn_bwd", grid=(nc, HEADS),
        in_specs=[cblk(0), cblk(HEADS), cblk(2 * HEADS),
                  pl.BlockSpec((c, HD), lambda n, h: (ch(n), OFF_AB // HD)), cblk(OFF_Z // HD), row, row, row,
                  pl.BlockSpec((None, None, HD, HD), lambda n, h: (ch(n), h, 0, 0)), hblk],
        out_specs=[hblk, hblk, hblk, pl.BlockSpec((c, HD), lambda n, h: (ch(n), 0)), hblk, row, row, row],
        out_shape=[_sds((t, GDN_W), F32), _sds((t, GDN_W), F32), _sds((t, GDN_W), F32), _sds((t, HD), F32),
                   _sds((t, GDN_W), BF16), _sds((1, HD), F32), _sds((1, HD), F32), _sds((1, HD), F32)],
        scratch_shapes=[pltpu.VMEM((HEADS, HD, HD), F32)],
        compiler_params=_cp("arbitrary", "arbitrary"),
    )(cqkv, cqkv, cqkv, proj, proj, a_log, dt_bias, gain, sprev, dy)


MERGE_TN = 512


def _merge_specs(t, tm, l):
    tn = MERGE_TN
    ys = [pl.BlockSpec((tm, wd), lambda i, j: (i, 0)) for wd in (POOL_W, SB_W, GDN_W)]
    ws = [pl.BlockSpec((None, None, wd, tn), lambda i, j: (j, l, 0, 0)) for wd in (POOL_W, SB_W, GDN_W)]
    gs = [pl.BlockSpec((tm, tn), functools.partial(lambda i, j, b: (i, OFF_GATE // tn + b * (D // tn) + j), b=b))
          for b in range(3)]
    out = pl.BlockSpec((tm, tn), lambda i, j: (i, j))
    return ys, ws, gs, out


def _merge_fwd(ys, wups, proj, l):
    t = proj.shape[0]
    tm = min(512, t)
    y_specs, w_specs, g_specs, out = _merge_specs(t, tm, l)

    def body(y0, y1, y2, w0, w1, w2, g0, g1, g2, o_ref):
        acc = jnp.zeros(o_ref.shape, F32)
        for y, w, g in ((y0, w0, g0), (y1, w1, g1), (y2, w2, g2)):
            acc = acc + jax.nn.sigmoid(g[...]) * _dot(y[...], w[...])
        o_ref[...] = acc.astype(o_ref.dtype)

    return pl.pallas_call(
        body, name="merge_fwd", grid=(t // tm, D // MERGE_TN), in_specs=[*y_specs, *w_specs, *g_specs],
        out_specs=out, out_shape=_sds((t, D), BF16), compiler_params=_cp("parallel", "parallel"),
    )(*ys, *wups, proj, proj, proj)


def _merge_bwd(ys, wups, proj, dmerged, l):
    t = proj.shape[0]
    tm = min(512, t)
    y_specs, w_specs, g_specs, out = _merge_specs(t, tm, l)

    def body(y0, y1, y2, w0, w1, w2, g0, g1, g2, dm_ref, dg0, dg1, dg2, dm0, dm1, dm2):
        dm = dm_ref[...].astype(F32)
        for y, w, g, dg, dmb in ((y0, w0, g0, dg0, dm0), (y1, w1, g1, dg1, dm1), (y2, w2, g2, dg2, dm2)):
            sg = jax.nn.sigmoid(g[...])
            dg[...] = (dm * _dot(y[...], w[...]) * sg * (1.0 - sg)).astype(dg.dtype)
            dmb[...] = (dm * sg).astype(dmb.dtype)

    return pl.pallas_call(
        body, name="merge_bwd", grid=(t // tm, D // MERGE_TN), in_specs=[*y_specs, *w_specs, *g_specs, out],
        out_specs=[out] * 6, out_shape=[_sds((t, D), BF16)] * 6, compiler_params=_cp("parallel", "parallel"),
    )(*ys, *wups, proj, proj, proj, dmerged)


def _tile(t, want):
    return min(t, want)


def _layer_fwd(x, l, gw, w_al, sp):
    t = x.shape[0]
    tm = _tile(t, 1024)
    u = _rms_fwd("rms_attn", x, sp["attn_norm"][l])
    proj = _mm("proj", u, w_al, m=t, n=N_AL, k=D, tm=tm, tn=1024, tk=512, a_spec=_a_plain(tm, 512),
               b_spec=_b_plain(512, 1024), dims=None, out_shapes=[_sds((t, N_AL), F32)], out_specs=[_o_plain(tm, 1024)])[0]
    y_pool = _pool_fwd(proj, sp["pool_w"][l], sp["pool_scale"][l])
    y_sb = _sb_fwd(proj)
    cqkv = _conv_fwd(proj, sp["conv"][l])
    y_gdn, sprev = _gdn_fwd(cqkv, proj, sp["a_log"][l], sp["dt_bias"][l], sp["gdn_norm"][l])
    ys = (y_pool, y_sb, y_gdn)
    wups = (gw["w_pool_up"], gw["w_sb_up"], gw["w_gdn_up"])
    merged = _merge_fwd(ys, wups, proj, l)
    x1 = _mm("out_proj", merged, gw["w_out"], m=t, n=D, k=D, tm=tm, tn=1024, tk=512, a_spec=_a_plain(tm, 512),
             b_spec=_w_rows(l, 512, 1024, 512), dims=None, out_shapes=[_sds((t, D), F32)], out_specs=[_o_plain(tm, 1024)],
             extras=[x], extra_specs=[_o_plain(tm, 1024)], epilogue=lambda r, xr: (r + xr,))[0]
    u2 = _rms_fwd("rms_mlp", x1, sp["mlp_norm"][l])

    def relu2(r):
        hv = jnp.maximum(r, 0.0)
        return hv, hv * hv

    hid, hid2 = _mm("ff1", u2, gw["w_ff1"], m=t, n=D_FF, k=D, tm=tm, tn=1024, tk=512, a_spec=_a_plain(tm, 512),
                    b_spec=_w_cols(l, 512, 1024, 2048), dims=None, out_shapes=[_sds((t, D_FF), BF16)] * 2,
                    out_specs=[_o_plain(tm, 1024)] * 2, epilogue=relu2)
    x2 = _mm("ff2", hid2, gw["w_ff2"], m=t, n=D, k=D_FF, tm=tm, tn=1024, tk=512, a_spec=_a_plain(tm, 512),
             b_spec=_w_rows(l, 512, 1024, 2048), dims=None, out_shapes=[_sds((t, D), F32)], out_specs=[_o_plain(tm, 1024)],
             extras=[x1], extra_specs=[_o_plain(tm, 1024)], epilogue=lambda r, xr: (r + xr,))[0]
    saved = dict(x=x, u=u, proj=proj, cqkv=cqkv, sprev=sprev, ys=ys, merged=merged, x1=x1, u2=u2, hid=hid, hid2=hid2)
    return x2, saved


def _layer_bwd(dx2, l, gw, w_al, sp, sv):
    t = dx2.shape[0]
    tm = _tile(t, 1024)
    tk = _tile(t, 512)
    g = {}
    dpre = _mm("ff2_dx", dx2, gw["w_ff2"], m=t, n=D_FF, k=D, tm=tm, tn=1024, tk=512, a_spec=_a_plain(tm, 512),
               b_spec=_w_rows_t(l, 512, 1024, 2048), dims=NT, out_shapes=[_sds((t, D_FF), BF16)],
               out_specs=[_o_plain(tm, 1024)], extras=[sv["hid"]], extra_specs=[_o_plain(tm, 1024)],
               epilogue=lambda r, hv: (r * (2.0 * hv.astype(F32)),))[0]
    g["w_ff2"] = _mm("ff2_dw", sv["hid2"], dx2, m=D_FF, n=D, k=t, tm=1024, tn=1024, tk=tk, a_spec=_a_trans(1024, tk),
                     b_spec=_b_plain(tk, 1024), dims=TN, out_shapes=[_sds((D_FF, D), BF16)],
                     out_specs=[_o_plain(1024, 1024)])[0].reshape(N_CHIPS, D_FF // N_CHIPS, D)
    du2 = _mm("ff1_dx", dpre, gw["w_ff1"], m=t, n=D, k=D_FF, tm=tm, tn=1024, tk=512, a_spec=_a_plain(tm, 512),
              b_spec=_w_cols_t(l, 512, 1024, 2048), dims=NT, out_shapes=[_sds((t, D), F32)], out_specs=[_o_plain(tm, 1024)])[0]
    g["w_ff1"] = _mm("ff1_dw", sv["u2"], dpre, m=D, n=D_FF, k=t, tm=1024, tn=1024, tk=tk, a_spec=_a_trans(1024, tk),
                     b_spec=_b_plain(tk, 1024), dims=TN, out_shapes=[_sds((N_CHIPS, D, D_FF // N_CHIPS), BF16)],
                     out_specs=[_o_colshard(1024, 1024, D_FF // N_CHIPS)])[0]
    dx1, g["mlp_norm"] = _rms_bwd("rms_mlp_bwd", du2, sv["x1"], sp["mlp_norm"][l], dx2)
    dmerged = _mm("out_dx", dx1, gw["w_out"], m=t, n=D, k=D, tm=tm, tn=512, tk=1024, a_spec=_a_plain(tm, 1024),
                  b_spec=_w_rows_t(l, 1024, 512, 512), dims=NT, out_shapes=[_sds((t, D), BF16)], out_specs=[_o_plain(tm, 512)])[0]
    g["w_out"] = _mm("out_dw", sv["merged"], dx1, m=D, n=D, k=t, tm=1024, tn=1024, tk=tk, a_spec=_a_trans(1024, tk),
                     b_spec=_b_plain(tk, 1024), dims=TN, out_shapes=[_sds((D, D), BF16)],
                     out_specs=[_o_plain(1024, 1024)])[0].reshape(N_CHIPS, D // N_CHIPS, D)
    wups = (gw["w_pool_up"], gw["w_sb_up"], gw["w_gdn_up"])
    dg0, dg1, dg2, dm0, dm1, dm2 = _merge_bwd(sv["ys"], wups, sv["proj"], dmerged, l)
    dys = []
    for nm, yb, dmb, wd in zip(("w_pool_up", "w_sb_up", "w_gdn_up"), sv["ys"], (dm0, dm1, dm2), (POOL_W, SB_W, GDN_W)):
        dys.append(_mm(nm + "_dx", dmb, gw[nm], m=t, n=wd, k=D, tm=tm, tn=256, tk=512, a_spec=_a_plain(tm, 512),
                       b_spec=_w_cols_t(l, 512, 256, 512), dims=NT, out_shapes=[_sds((t, wd), F32)],
                       out_specs=[_o_plain(tm, 256)])[0])
        g[nm] = _mm(nm + "_dw", yb, dmb, m=wd, n=D, k=t, tm=256, tn=512, tk=tk, a_spec=_a_trans(256, tk),
                    b_spec=_b_plain(tk, 512), dims=TN, out_shapes=[_sds((N_CHIPS, wd, D // N_CHIPS), BF16)],
                    out_specs=[_o_colshard(256, 512, D // N_CHIPS)])[0]
    proj = sv["proj"]
    dp, g["pool_w"], g["pool_scale"] = _pool_bwd(proj, sp["pool_w"][l], sp["pool_scale"][l], dys[0])
    dsq, dsk, dsv = _sb_bwd(proj, dys[1])
    dcq, dck, dcv, dab, dz, g["a_log"], g["dt_bias"], g["gdn_norm"] = _gdn_bwd(
        sv["cqkv"], proj, sp["a_log"][l], sp["dt_bias"][l], sp["gdn_norm"][l], sv["sprev"], dys[2])
    dgx, dgw = zip(*[_conv_bwd(proj, sp["conv"][l], dc, part) for part, dc in enumerate((dcq, dck, dcv))])
    g["conv"] = jnp.concatenate(dgw, axis=1)
    dproj = jnp.concatenate(
        [dp, dsq, dsk.astype(BF16), dsv.astype(BF16), *dgx, dz, dab.astype(BF16),
         jnp.zeros((t, OFF_GATE - OFF_AB - HD), BF16), dg0, dg1, dg2], axis=1)
    du = _mm("proj_dx", dproj, w_al, m=t, n=D, k=N_AL, tm=tm, tn=1024, tk=512, a_spec=_a_plain(tm, 512),
             b_spec=_b_trans(512, 1024), dims=NT, out_shapes=[_sds((t, D), F32)], out_specs=[_o_plain(tm, 1024)])[0]
    g["w_al"] = _mm("proj_dw", sv["u"], dproj, m=D, n=N_AL, k=t, tm=1024, tn=1024, tk=tk, a_spec=_a_trans(1024, tk),
                    b_spec=_b_plain(tk, 1024), dims=TN, out_shapes=[_sds((D, N_AL), BF16)], out_specs=[_o_plain(1024, 1024)])[0]
    dx, g["attn_norm"] = _rms_bwd("rms_attn_bwd", du, sv["x"], sp["attn_norm"][l], dx1)
    return dx, g


def _align_w_in(w):
    return jnp.concatenate([w[:, :AB_END], jnp.zeros((D, OFF_GATE - AB_END), w.dtype), w[:, AB_END:]], axis=1)


def _unalign_w_in(w):
    return jnp.concatenate([w[:, :AB_END], w[:, OFF_GATE:]], axis=1)


def _row128(v):
    return jnp.pad(v.reshape(1, -1), ((0, 0), (0, HD - v.shape[-1])))


def _local_step(x, target, gw, w_in_al, sp):
    saved = []
    h = x
    for l in range(2):
        h, sv = _layer_fwd(h, l, gw, w_in_al[l], sp)
        saved.append(sv)
    loss, dh, g_final = _loss_head(h, sp["final_norm"], target)
    grads = [None, None]
    for l in (1, 0):
        dh, grads[l] = _layer_bwd(dh, l, gw, w_in_al[l], sp, saved[l])
    return loss, dh, grads, g_final


ANY = pl.BlockSpec(memory_space=pl.ANY)


def _me():
    return lax.axis_index("x"), lax.axis_index("y"), lax.axis_index("c")


def _other_chips(x, y):
    return [(1 - x, y), (x, 1 - y), (1 - x, 1 - y)]


def _half(ref, axis, c, rows):
    half = rows // 2
    idx = [slice(None)] * axis + [pl.ds(pl.multiple_of(c * half, 16), half)]
    return ref.at[tuple(idx)]


def _gather_weights(bufs):
    n = len(bufs)

    def body(*refs):
        out = refs[n:2 * n]
        send, recv = refs[2 * n:]
        x, y, c = _me()
        mine = 2 * x + y
        sibling = (x, y, 1 - c)
        chips = _other_chips(x, y)
        sends = []
        for t in range(n):
            rows = out[t].shape[2]
            for k, (px, py) in enumerate(chips):
                own_half = _half(out[t].at[mine], 1, c, rows)
                cp = pltpu.make_async_remote_copy(
                    src_ref=own_half, dst_ref=own_half,
                    send_sem=send.at[6 * t + k], recv_sem=recv.at[6 * t + k], device_id=(px, py, c), device_id_type=MESH)
                cp.start()
                sends.append(cp)
        for t in range(n):
            rows = out[t].shape[2]
            for k, (px, py) in enumerate(chips):
                landed = _half(out[t].at[2 * px + py], 1, c, rows)
                pltpu.make_async_remote_copy(
                    src_ref=landed, dst_ref=landed, send_sem=send.at[6 * t + k], recv_sem=recv.at[6 * t + k],
                    device_id=(px, py, c), device_id_type=MESH).wait_recv()
                cp = pltpu.make_async_remote_copy(
                    src_ref=landed, dst_ref=landed, send_sem=send.at[6 * t + 3 + k], recv_sem=recv.at[6 * t + 3 + k],
                    device_id=sibling, device_id_type=MESH)
                cp.start()
                sends.append(cp)
        for t in range(n):
            rows = out[t].shape[2]
            for k, (px, py) in enumerate(chips):
                other = _half(out[t].at[2 * px + py], 1, 1 - c, rows)
                pltpu.make_async_remote_copy(
                    src_ref=other, dst_ref=other, send_sem=send.at[6 * t + 3 + k], recv_sem=recv.at[6 * t + 3 + k],
                    device_id=sibling, device_id_type=MESH).wait_recv()
        for cp in sends:
            cp.wait_send()

    return pl.pallas_call(
        body, name="gather_weights", in_specs=[ANY] * n, out_specs=[ANY] * n,
        out_shape=[_sds(s.shape, s.dtype) for s in bufs], input_output_aliases={t: t for t in range(n)},
        scratch_shapes=[pltpu.SemaphoreType.DMA((6 * n,)), pltpu.SemaphoreType.DMA((6 * n,))],
    )(*bufs)


def _rs_pair(grads):
    n = len(grads)

    def body(*refs):
        g, out = refs[:n], refs[n:2 * n]
        send, recv = refs[2 * n:]
        x, y, c = _me()
        copies = []
        for t in range(n):
            cp = pltpu.make_async_remote_copy(
                src_ref=_half(g[t], 1, 1 - c, g[t].shape[1]), dst_ref=out[t], send_sem=send.at[t], recv_sem=recv.at[t],
                device_id=(x, y, 1 - c), device_id_type=MESH)
            cp.start()
            copies.append(cp)
        for cp in copies:
            cp.wait()

    return pl.pallas_call(
        body, name="rs_pair", in_specs=[ANY] * n, out_specs=[ANY] * n,
        out_shape=[_sds((N_CHIPS, s.shape[1] // 2, s.shape[2]), s.dtype) for s in grads],
        scratch_shapes=[pltpu.SemaphoreType.DMA((n,)), pltpu.SemaphoreType.DMA((n,))],
    )(*grads)


def _rs_chips(parts):
    n = len(parts)

    def body(*refs):
        p, out = refs[:n], refs[n:2 * n]
        send, recv = refs[2 * n:]
        x, y, c = _me()
        copies = []
        for t in range(n):
            for k, (px, py) in enumerate(_other_chips(x, y)):
                cp = pltpu.make_async_remote_copy(
                    src_ref=p[t].at[2 * px + py], dst_ref=out[t].at[k], send_sem=send.at[3 * t + k],
                    recv_sem=recv.at[3 * t + k], device_id=(px, py, c), device_id_type=MESH)
                cp.start()
                copies.append(cp)
        for cp in copies:
            cp.wait()

    return pl.pallas_call(
        body, name="rs_chips", in_specs=[ANY] * n, out_specs=[ANY] * n,
        out_shape=[_sds((3, *s.shape[1:]), s.dtype) for s in parts],
        scratch_shapes=[pltpu.SemaphoreType.DMA((3 * n,)), pltpu.SemaphoreType.DMA((3 * n,))],
    )(*parts)


def _pair_exchange(bufs):
    n = len(bufs)

    def body(*refs):
        out = refs[n:2 * n]
        send, recv = refs[2 * n:]
        x, y, c = _me()
        copies = []
        for t in range(n):
            cp = pltpu.make_async_remote_copy(
                src_ref=out[t].at[c], dst_ref=out[t].at[c], send_sem=send.at[t], recv_sem=recv.at[t],
                device_id=(x, y, 1 - c), device_id_type=MESH)
            cp.start()
            copies.append(cp)
        for t, cp in enumerate(copies):
            cp.wait_send()
            pltpu.make_async_remote_copy(
                src_ref=out[t].at[1 - c], dst_ref=out[t].at[1 - c], send_sem=send.at[t], recv_sem=recv.at[t],
                device_id=(x, y, 1 - c), device_id_type=MESH).wait_recv()

    return pl.pallas_call(
        body, name="pair_exchange", in_specs=[ANY] * n, out_specs=[ANY] * n,
        out_shape=[_sds(s.shape, s.dtype) for s in bufs], input_output_aliases={t: t for t in range(n)},
        scratch_shapes=[pltpu.SemaphoreType.DMA((n,)), pltpu.SemaphoreType.DMA((n,))],
    )(*bufs)


def _row_tile(rows, cols, itemsize, budget=2 * 1024 * 1024):
    tr = rows
    while tr * cols * itemsize > budget and tr % 32 == 0:
        tr //= 2
    return tr


def _sum_pair(name, g, got, where):
    nchip, rows, cols = g.shape
    half = rows // 2
    tr = _row_tile(half, cols, 4)
    per = half // tr

    def body(w_ref, g_ref, r_ref, o_ref):
        o_ref[...] = (g_ref[...].astype(F32) + r_ref[...].astype(F32)).astype(o_ref.dtype)

    blk = pl.BlockSpec((None, tr, cols), lambda j, i, w_ref: (j, i, 0))
    return pl.pallas_call(
        body, name=name,
        grid_spec=pltpu.PrefetchScalarGridSpec(
            num_scalar_prefetch=1, grid=(nchip, per),
            in_specs=[pl.BlockSpec((None, tr, cols), lambda j, i, w_ref: (j, w_ref[1] * per + i, 0)), blk], out_specs=blk),
        out_shape=_sds((nchip, half, cols), BF16), compiler_params=_cp("parallel", "parallel"),
    )(where, g, got)


def _sum_chips(name, p, got, where):
    _, rows, cols = p.shape
    tr = _row_tile(rows, cols, 4)

    def body(w_ref, p_ref, r0, r1, r2, o_ref):
        o_ref[...] = ((p_ref[...].astype(F32) + r0[...].astype(F32)) + r1[...].astype(F32)) + r2[...].astype(F32)

    def got_k(k):
        return pl.BlockSpec((None, tr, cols), lambda i, w_ref: (k, i, 0))

    return pl.pallas_call(
        body, name=name,
        grid_spec=pltpu.PrefetchScalarGridSpec(
            num_scalar_prefetch=1, grid=(rows // tr,),
            in_specs=[pl.BlockSpec((None, tr, cols), lambda i, w_ref: (w_ref[0], i, 0)), got_k(0), got_k(1), got_k(2)],
            out_specs=pl.BlockSpec((None, tr, cols), lambda i, w_ref: (w_ref[1], i, 0))),
        out_shape=_sds((2, rows, cols), F32), compiler_params=_cp("parallel"),
    )(where, p, got, got, got)


def _reduce_scatter(grads, where):
    got = _rs_pair(grads)
    parts = [_sum_pair(f"sum_pair_{t}", g, r, where) for t, (g, r) in enumerate(zip(grads, got))]
    got = _rs_chips(parts)
    halves = [_sum_chips(f"sum_chips_{t}", p, r, where) for t, (p, r) in enumerate(zip(parts, got))]
    return [o.reshape(-1, o.shape[-1]) for o in _pair_exchange(halves)]


def _all_reduce_small(name, v):
    rows = v.shape[0]

    def body(v_ref, o_ref, land, send, recv):
        x, y, c = _me()
        mine = 4 * x + 2 * y + c
        copies = []
        for k in range(1, 8):
            kx, ky, kc = k >> 2, (k >> 1) & 1, k & 1
            peer = (x ^ kx, y ^ ky, c ^ kc)
            cp = pltpu.make_async_remote_copy(
                src_ref=v_ref, dst_ref=land.at[mine], send_sem=send.at[k - 1], recv_sem=recv.at[k - 1],
                device_id=peer, device_id_type=MESH)
            cp.start()
            copies.append(cp)
        land[mine] = v_ref[...]
        for k in range(1, 8):
            kx, ky, kc = k >> 2, (k >> 1) & 1, k & 1
            src = 4 * (x ^ kx) + 2 * (y ^ ky) + (c ^ kc)
            pltpu.make_async_remote_copy(
                src_ref=v_ref, dst_ref=land.at[src], send_sem=send.at[k - 1], recv_sem=recv.at[k - 1],
                device_id=(x ^ kx, y ^ ky, c ^ kc), device_id_type=MESH).wait_recv()
        acc = land[0]
        for d in range(1, 8):
            acc = acc + land[d]
        o_ref[...] = acc
        for cp in copies:
            cp.wait_send()

    vm = pl.BlockSpec(memory_space=pltpu.VMEM)
    return pl.pallas_call(
        body, name=name, in_specs=[vm], out_specs=vm, out_shape=_sds((rows, 128), F32),
        scratch_shapes=[pltpu.VMEM((8, rows, 128), F32), pltpu.SemaphoreType.DMA((7,)), pltpu.SemaphoreType.DMA((7,))],
    )(v)


def _adamw(name, w, g, m, v):
    rows, cols = w.shape
    tr = _row_tile(rows, cols, 4, budget=1024 * 1024)
    c1 = 1.0 / (1.0 - ADAM_B1 ** ADAM_STEP)
    c2 = 1.0 / (1.0 - ADAM_B2 ** ADAM_STEP)

    def body(w_ref, g_ref, m_ref, v_ref, d_ref, nm_ref, nv_ref):
        gv = g_ref[...]
        nm = ADAM_B1 * m_ref[...] + (1.0 - ADAM_B1) * gv
        nv = ADAM_B2 * v_ref[...] + (1.0 - ADAM_B2) * (gv * gv)
        d_ref[...] = -ADAM_LR * ((nm * c1) / (jnp.sqrt(nv * c2) + ADAM_EPS) + ADAM_WD * w_ref[...])
        nm_ref[...] = nm
        nv_ref[...] = nv

    blk = pl.BlockSpec((tr, cols), lambda i: (i, 0))
    return pl.pallas_call(
        body, name=name, grid=(rows // tr,), in_specs=[blk] * 4, out_specs=[blk] * 3,
        out_shape=[_sds((rows, cols), F32)] * 3, compiler_params=_cp("parallel"),
    )(w, g, m, v)


def _to_bf16_slot(name, w, where):
    rows, cols = w.shape
    tr = _row_tile(rows, cols, 4)

    def body(w_ref, x_ref, o_ref):
        o_ref[...] = x_ref[...].astype(BF16)

    return pl.pallas_call(
        body, name=name,
        grid_spec=pltpu.PrefetchScalarGridSpec(
            num_scalar_prefetch=1, grid=(rows // tr,), in_specs=[pl.BlockSpec((tr, cols), lambda i, w_ref: (i, 0))],
            out_specs=pl.BlockSpec((None, tr, cols), lambda i, w_ref: (w_ref[0], i, 0))),
        out_shape=_sds((N_CHIPS, rows, cols), BF16), compiler_params=_cp("parallel"))(where, w)


BIG = ("w_in", "w_pool_up", "w_sb_up", "w_gdn_up", "w_out", "w_ff1", "w_ff2")
SMALL = (("attn_norm", (D,)), ("pool_w", (4, 128, 128)), ("pool_scale", (POOL_W,)), ("gdn_a_log", (HEADS,)),
         ("gdn_dt_bias", (HEADS,)), ("gdn_norm", (HD,)), ("mlp_norm", (D,)))


def _rows128(a):
    flat = a.reshape(-1)
    pad = (-flat.shape[0]) % 128
    return jnp.pad(flat, (0, pad)).reshape(-1, 128)


def _pack(parts):
    packed = jnp.concatenate([_rows128(p) for p in parts], axis=0)
    return jnp.pad(packed, ((0, (-packed.shape[0]) % 8), (0, 0)))


def _unpack(packed, shapes):
    out, r = [], 0
    for shp in shapes:
        size = 1
        for s in shp:
            size *= s
        nr = -(-size // 128)
        out.append(packed[r:r + nr].reshape(-1)[:size].reshape(shp))
        r += nr
    return out


def kernel(x, attn_norm, w_in, pool_w, pool_scale, gdn_conv, gdn_a_log, gdn_dt_bias, gdn_norm, w_pool_up, w_sb_up, w_gdn_up, w_out, mlp_norm, w_ff1, w_ff2, final_norm, loss_target, m_attn_norm, m_w_in, m_pool_w, m_pool_scale, m_gdn_conv, m_gdn_a_log, m_gdn_dt_bias, m_gdn_norm, m_w_pool_up, m_w_sb_up, m_w_gdn_up, m_w_out, m_mlp_norm, m_w_ff1, m_w_ff2, m_final_norm, v_attn_norm, v_w_in, v_pool_w, v_pool_scale, v_gdn_conv, v_gdn_a_log, v_gdn_dt_bias, v_gdn_norm, v_w_pool_up, v_w_sb_up, v_w_gdn_up, v_w_out, v_mlp_norm, v_w_ff1, v_w_ff2, v_final_norm):
    weights = dict(attn_norm=attn_norm, w_in=w_in, pool_w=pool_w, pool_scale=pool_scale, gdn_conv=gdn_conv,
                   gdn_a_log=gdn_a_log, gdn_dt_bias=gdn_dt_bias, gdn_norm=gdn_norm, w_pool_up=w_pool_up, w_sb_up=w_sb_up,
                   w_gdn_up=w_gdn_up, w_out=w_out, mlp_norm=mlp_norm, w_ff1=w_ff1, w_ff2=w_ff2, final_norm=final_norm)
    mom1 = dict(attn_norm=m_attn_norm, w_in=m_w_in, pool_w=m_pool_w, pool_scale=m_pool_scale, gdn_conv=m_gdn_conv,
                gdn_a_log=m_gdn_a_log, gdn_dt_bias=m_gdn_dt_bias, gdn_norm=m_gdn_norm, w_pool_up=m_w_pool_up,
                w_sb_up=m_w_sb_up, w_gdn_up=m_w_gdn_up, w_out=m_w_out, mlp_norm=m_mlp_norm, w_ff1=m_w_ff1, w_ff2=m_w_ff2,
                final_norm=m_final_norm)
    mom2 = dict(attn_norm=v_attn_norm, w_in=v_w_in, pool_w=v_pool_w, pool_scale=v_pool_scale, gdn_conv=v_gdn_conv,
                gdn_a_log=v_gdn_a_log, gdn_dt_bias=v_gdn_dt_bias, gdn_norm=v_gdn_norm, w_pool_up=v_w_pool_up,
                w_sb_up=v_w_sb_up, w_gdn_up=v_w_gdn_up, w_out=v_w_out, mlp_norm=v_mlp_norm, w_ff1=v_w_ff1, w_ff2=v_w_ff2,
                final_norm=v_final_norm)
    xi, yi, ci = lax.axis_index("x"), lax.axis_index("y"), lax.axis_index("c")
    chip = 2 * xi + yi
    where = jnp.stack([chip, ci]).astype(jnp.int32)

    bufs = [_to_bf16_slot("cast_" + nm, weights[nm].reshape(-1, weights[nm].shape[-1]), where).reshape(N_CHIPS, *weights[nm].shape)
            for nm in BIG]
    gw = dict(zip(BIG, _gather_weights(bufs)))
    conv_cols = gdn_conv.shape[-1]
    conv_place = lax.dynamic_update_slice(jnp.zeros((2, GDN_CONV, N_CHIPS * conv_cols), F32),
                                          jnp.where(ci == 0, gdn_conv, 0.0), (0, 0, chip * conv_cols))
    conv_full = _all_reduce_small("gather_conv", _rows128(conv_place)).reshape(2, GDN_CONV, N_CHIPS * conv_cols)
    w_in_full = jnp.transpose(gw["w_in"], (1, 2, 0, 3)).reshape(2, D, N_IN)
    w_in_al = [_align_w_in(w_in_full[l]) for l in range(2)]
    sp = dict(attn_norm=attn_norm.reshape(2, 1, D), pool_w=pool_w, pool_scale=pool_scale.reshape(2, 1, POOL_W),
              conv=conv_full, a_log=jnp.stack([_row128(gdn_a_log[l]) for l in range(2)]),
              dt_bias=jnp.stack([_row128(gdn_dt_bias[l]) for l in range(2)]), gdn_norm=gdn_norm.reshape(2, 1, HD),
              mlp_norm=mlp_norm.reshape(2, 1, D), final_norm=final_norm.reshape(1, D))

    loss, grad_x, grads, g_final = _local_step(x[0], loss_target[0], gw, w_in_al, sp)
    loss = lax.psum(loss[0, 0], ("x", "y", "c"))

    big_grads = {nm: [] for nm in BIG}
    for l in range(2):
        g = grads[l]
        w_in_g = _unalign_w_in(g["w_al"]).reshape(D, N_CHIPS, N_IN // N_CHIPS)
        per_layer = [jnp.transpose(w_in_g, (1, 0, 2))] + [g[nm] for nm in BIG[1:]]
        for nm, red in zip(BIG, _reduce_scatter(per_layer, where)):
            big_grads[nm].append(red)
    small_parts, small_shapes = [], []
    for l in range(2):
        g = grads[l]
        for nm, shp in SMALL:
            key = {"gdn_a_log": "a_log", "gdn_dt_bias": "dt_bias"}.get(nm, nm)
            val = g[key]
            small_parts.append(val[0, :HEADS] if nm in ("gdn_a_log", "gdn_dt_bias") else val)
            small_shapes.append(shp)
        small_parts.append(g["conv"])
        small_shapes.append((GDN_CONV, N_CHIPS * conv_cols))
    small_parts.append(g_final)
    small_shapes.append((D,))
    reduced = _unpack(_all_reduce_small("reduce_small", _pack(small_parts)), small_shapes)
    per = len(SMALL) + 1
    grad = {}
    for i, (nm, _) in enumerate(SMALL):
        grad[nm] = jnp.stack([reduced[i], reduced[per + i]])
    conv_g = jnp.stack([reduced[per - 1], reduced[2 * per - 1]])
    grad["gdn_conv"] = lax.dynamic_slice(conv_g, (0, 0, chip * conv_cols), (2, GDN_CONV, conv_cols))
    grad["final_norm"] = reduced[-1]
    for nm in BIG:
        grad[nm] = jnp.stack(big_grads[nm]).reshape(weights[nm].shape)

    delta, new_m, new_v = {}, {}, {}
    for nm in BIG:
        shp = weights[nm].shape
        flat = lambda a: a.reshape(-1, shp[-1])
        d, nm1, nv1 = _adamw("adamw_" + nm, flat(weights[nm]), flat(grad[nm]), flat(mom1[nm]), flat(mom2[nm]))
        delta[nm], new_m[nm], new_v[nm] = d.reshape(shp), nm1.reshape(shp), nv1.reshape(shp)
    small_names = [nm for nm, _ in SMALL] + ["gdn_conv", "final_norm"]
    packs = [_pack([src[nm] for nm in small_names]) for src in (weights, grad, mom1, mom2)]
    outs = _adamw("adamw_small", *packs)
    shapes = [weights[nm].shape for nm in small_names]
    for dst, packed in zip((delta, new_m, new_v), outs):
        for nm, val in zip(small_names, _unpack(packed, shapes)):
            dst[nm] = val

    order = ("attn_norm", "w_in", "pool_w", "pool_scale", "gdn_conv", "gdn_a_log", "gdn_dt_bias", "gdn_norm", "w_pool_up",
             "w_sb_up", "w_gdn_up", "w_out", "mlp_norm", "w_ff1", "w_ff2", "final_norm")
    return (loss, grad_x[None], *[grad[n] for n in order], *[delta[n] for n in order], *[new_m[n] for n in order],
            *[new_v[n] for n in order])
```

```python
import functools

import jax
import jax.numpy as jnp
from jax import lax
from jax.experimental import pallas as pl
from jax.experimental.pallas import tpu as pltpu

F32, BF16 = jnp.float32, jnp.bfloat16
HIGH = lax.Precision.HIGH
MESH = pl.DeviceIdType.MESH

D = 2048
EPS = 1e-6
POOL_WINDOWS = (2, 4, 8, 16)
POOL_W, SB_W, GDN_W = 512, 768, 768
HEADS, HD = 6, 128
SB_BLOCK = 128
GDN_CHUNK = 64
D_FF = 4 * D
N_IN = 12044
N_CHIPS = 4
OFF_SB, OFF_GQKV, OFF_Z, OFF_AB, OFF_P, OFF_GATE = 0, 2304, 4608, 5376, 5632, 6144
AB_W = 256
ORIG_SB, ORIG_AB, ORIG_GATE = 512, 5888, 5900
N_AL = 12288
VMEM_LIMIT = 48 * 1024 * 1024

ADAM_LR, ADAM_B1, ADAM_B2, ADAM_EPS, ADAM_WD, ADAM_STEP = 0.001, 0.9, 0.999, 1e-08, 0.01, 10

NT = (((1,), (1,)), ((), ()))
TN = (((0,), (0,)), ((), ()))


def _cp(*sem):
    return pltpu.CompilerParams(dimension_semantics=sem, vmem_limit_bytes=VMEM_LIMIT)


def _dot(a, b, dims=None, precision=None):
    if dims is None:
        dims = (((a.ndim - 1,), (0,)), ((), ()))
    return lax.dot_general(a, b, dims, precision=precision, preferred_element_type=F32)


def _hdot(a, b, dims=None):
    return _dot(a, b, dims, precision=HIGH)


def _bdot(a, b, dims=None):
    return _dot(a.astype(BF16), b.astype(BF16), dims)


def _mm(name, a, b, *, m, n, k, tm, tn, tk, a_spec, b_spec, dims, out_shapes, out_specs,
        extras=(), extra_specs=(), epilogue=None):
    nk = k // tk
    ne, no = len(extras), len(out_shapes)

    def body(*refs):
        a_ref, b_ref = refs[0], refs[1]
        ex = refs[2:2 + ne]
        outs = refs[2 + ne:2 + ne + no]
        acc = refs[-1]
        kk = pl.program_id(2)

        @pl.when(kk == 0)
        def _():
            acc[...] = jnp.zeros_like(acc)

        acc[...] += _dot(a_ref[...].astype(BF16), b_ref[...].astype(BF16), dims)

        @pl.when(kk == nk - 1)
        def _():
            r = acc[...]
            res = epilogue(r, *[e[...] for e in ex]) if epilogue is not None else (r,)
            for o, v in zip(outs, res):
                o[...] = v.astype(o.dtype)

    return pl.pallas_call(
        body, name=name, grid=(m // tm, n // tn, nk),
        in_specs=[a_spec, b_spec, *extra_specs], out_specs=out_specs, out_shape=out_shapes,
        scratch_shapes=[pltpu.VMEM((tm, tn), F32)],
        compiler_params=_cp("parallel", "parallel", "arbitrary"),
    )(a, b, *extras)


def _a_plain(tm, tk):
    return pl.BlockSpec((tm, tk), lambda i, j, kk: (i, kk))


def _a_trans(tm, tk):
    return pl.BlockSpec((tk, tm), lambda i, j, kk: (kk, i))


def _b_plain(tk, tn):
    return pl.BlockSpec((tk, tn), lambda i, j, kk: (kk, j))


def _b_trans(tk, tn):
    return pl.BlockSpec((tn, tk), lambda i, j, kk: (j, kk))


def _o_plain(tm, tn):
    return pl.BlockSpec((tm, tn), lambda i, j, kk: (i, j))


def _o_colshard(tm, tn, ns_cols):
    per = ns_cols // tn
    return pl.BlockSpec((None, tm, tn), lambda i, j, kk: (j // per, i, j % per))


def _w_cols(l, tk, tn, ns):
    per = ns // tn
    return pl.BlockSpec((None, None, tk, tn), lambda i, j, kk: (j // per, l, kk, j % per))


def _w_cols_t(l, tk, tn, ns):
    per = ns // tk
    return pl.BlockSpec((None, None, tn, tk), lambda i, j, kk: (kk // per, l, j, kk % per))


def _w_rows(l, tk, tn, ks):
    per = ks // tk
    return pl.BlockSpec((None, None, tk, tn), lambda i, j, kk: (kk // per, l, kk % per, j))


def _w_rows_t(l, tk, tn, ks):
    per = ks // tn
    return pl.BlockSpec((None, None, tn, tk), lambda i, j, kk: (j // per, l, j % per, kk))


def _sds(shape, dtype):
    return jax.ShapeDtypeStruct(shape, dtype)


def _rms_fwd(name, x, gain):
    t = x.shape[0]
    tt = min(256, t)

    def body(x_ref, g_ref, u_ref):
        xv = x_ref[...]
        r = lax.rsqrt(jnp.mean(xv * xv, axis=-1, keepdims=True) + EPS)
        u_ref[...] = (xv * r * g_ref[...]).astype(u_ref.dtype)

    return pl.pallas_call(
        body, name=name, grid=(t // tt,),
        in_specs=[pl.BlockSpec((tt, D), lambda i: (i, 0)), pl.BlockSpec((1, D), lambda i: (0, 0))],
        out_specs=pl.BlockSpec((tt, D), lambda i: (i, 0)), out_shape=_sds((t, D), BF16),
        compiler_params=_cp("parallel"),
    )(x, gain)


def _rms_bwd(name, du, x, gain, dres):
    t = x.shape[0]
    tt = min(256, t)

    def body(du_ref, x_ref, g_ref, dres_ref, dx_ref, dg_ref):
        @pl.when(pl.program_id(0) == 0)
        def _():
            dg_ref[...] = jnp.zeros_like(dg_ref)

        xv, duv = x_ref[...], du_ref[...]
        r = lax.rsqrt(jnp.mean(xv * xv, axis=-1, keepdims=True) + EPS)
        nx = xv * r
        dn = duv * g_ref[...]
        dg_ref[...] += jnp.sum(duv * nx, axis=0, keepdims=True)
        dx_ref[...] = dres_ref[...] + r * (dn - nx * jnp.mean(dn * nx, axis=-1, keepdims=True))

    row = pl.BlockSpec((tt, D), lambda i: (i, 0))
    vec = pl.BlockSpec((1, D), lambda i: (0, 0))
    return pl.pallas_call(
        body, name=name, grid=(t // tt,), in_specs=[row, row, vec, row], out_specs=[row, vec],
        out_shape=[_sds((t, D), F32), _sds((1, D), F32)], compiler_params=_cp("arbitrary"),
    )(du, x, gain, dres)


def _loss_head(x, gain, target):
    t = x.shape[0]
    tt = min(256, t)

    def body(x_ref, g_ref, t_ref, loss_ref, dx_ref, dg_ref):
        @pl.when(pl.program_id(0) == 0)
        def _():
            dg_ref[...] = jnp.zeros_like(dg_ref)
            loss_ref[...] = jnp.zeros_like(loss_ref)

        xv = x_ref[...]
        r = lax.rsqrt(jnp.mean(xv * xv, axis=-1, keepdims=True) + EPS)
        nx = xv * r
        err = nx * g_ref[...] - t_ref[...]
        loss_ref[...] += 0.5 * jnp.sum(jnp.mean(err * err, axis=-1, keepdims=True), axis=0, keepdims=True)
        dy = err * (1.0 / D)
        dn = dy * g_ref[...]
        dg_ref[...] += jnp.sum(dy * nx, axis=0, keepdims=True)
        dx_ref[...] = r * (dn - nx * jnp.mean(dn * nx, axis=-1, keepdims=True))

    row = pl.BlockSpec((tt, D), lambda i: (i, 0))
    vec = pl.BlockSpec((1, D), lambda i: (0, 0))
    one = pl.BlockSpec((1, 1), lambda i: (0, 0))
    return pl.pallas_call(
        body, name="loss_head", grid=(t // tt,), in_specs=[row, vec, row], out_specs=[one, row, vec],
        out_shape=[_sds((1, 1), F32), _sds((t, D), F32), _sds((1, D), F32)], compiler_params=_cp("arbitrary"),
    )(x, gain, target)


def _shift_down(v, s, t_idx):
    return jnp.where(t_idx >= s, pltpu.roll(v, s, 0), 0.0)


def _shift_up(v, s, t_idx, t):
    return jnp.where(t_idx < t - s, pltpu.roll(v, t - s, 0), 0.0)


def _pool_d(p, g, t_idx):
    s = p
    for step in range(g + 1):
        s = s + _shift_down(s, 1 << step, t_idx)
    cnt = jnp.minimum(t_idx + 1, POOL_WINDOWS[g]).astype(F32)
    return s / cnt - p, cnt


def _pool_fwd(proj, pool_w, pool_scale):
    t = proj.shape[0]
    g128 = POOL_W // len(POOL_WINDOWS)

    def body(p_ref, w_ref, s_ref, y_ref):
        t_idx = lax.broadcasted_iota(jnp.int32, (t, g128), 0)
        for g in range(len(POOL_WINDOWS)):
            sl = slice(g * g128, (g + 1) * g128)
            d, _ = _pool_d(p_ref[:, sl], g, t_idx)
            y_ref[:, sl] = (_bdot(d, w_ref[g]) * s_ref[:, sl]).astype(y_ref.dtype)

    return pl.pallas_call(
        body, name="pool_fwd", grid=(1,),
        in_specs=[pl.BlockSpec((t, POOL_W), lambda i: (0, OFF_P // POOL_W)),
                  pl.BlockSpec((4, g128, g128), lambda i: (0, 0, 0)), pl.BlockSpec((1, POOL_W), lambda i: (0, 0))],
        out_specs=pl.BlockSpec((t, POOL_W), lambda i: (0, 0)), out_shape=_sds((t, POOL_W), BF16),
        compiler_params=_cp("arbitrary"),
    )(proj, pool_w, pool_scale)


def _pool_bwd(proj, pool_w, pool_scale, dy):
    t = proj.shape[0]
    g128 = POOL_W // len(POOL_WINDOWS)

    def body(p_ref, w_ref, s_ref, dy_ref, dp_ref, dw_ref, ds_ref):
        t_idx = lax.broadcasted_iota(jnp.int32, (t, g128), 0)
        for g in range(len(POOL_WINDOWS)):
            sl = slice(g * g128, (g + 1) * g128)
            d, cnt = _pool_d(p_ref[:, sl], g, t_idx)
            dyv = dy_ref[:, sl].astype(F32)
            ds_ref[:, sl] = jnp.sum(dyv * _bdot(d, w_ref[g]), axis=0, keepdims=True)
            dys = dyv * s_ref[:, sl]
            dw_ref[g] = _bdot(d, dys, TN)
            dd = _bdot(dys, w_ref[g], NT)
            s = dd / cnt
            for step in range(g + 1):
                s = s + _shift_up(s, 1 << step, t_idx, t)
            dp_ref[:, sl] = (s - dd).astype(dp_ref.dtype)

    return pl.pallas_call(
        body, name="pool_bwd", grid=(1,),
        in_specs=[pl.BlockSpec((t, POOL_W), lambda i: (0, OFF_P // POOL_W)),
                  pl.BlockSpec((4, g128, g128), lambda i: (0, 0, 0)), pl.BlockSpec((1, POOL_W), lambda i: (0, 0)),
                  pl.BlockSpec((t, POOL_W), lambda i: (0, 0))],
        out_specs=[pl.BlockSpec((t, POOL_W), lambda i: (0, 0)), pl.BlockSpec((4, g128, g128), lambda i: (0, 0, 0)),
                   pl.BlockSpec((1, POOL_W), lambda i: (0, 0))],
        out_shape=[_sds((t, POOL_W), BF16), _sds((4, g128, g128), F32), _sds((1, POOL_W), F32)],
        compiler_params=_cp("arbitrary"),
    )(proj, pool_w, pool_scale, dy)


SB_GROUP = 3
SB_GW = SB_GROUP * HD


def _sb_cast_kv(proj):
    t = proj.shape[0]
    tt = min(512, t)

    def body(x_ref, o_ref):
        o_ref[...] = x_ref[...].astype(BF16)

    return pl.pallas_call(
        body, name="sb_cast_kv", grid=(t // tt, 2),
        in_specs=[pl.BlockSpec((tt, SB_W), lambda i, j: (i, OFF_SB // SB_W + 1 + j))],
        out_specs=pl.BlockSpec((tt, SB_W), lambda i, j: (i, j)), out_shape=_sds((t, 2 * SB_W), BF16),
        compiler_params=_cp("parallel", "parallel"),
    )(proj)


def _sb_specs(t):
    q_spec = pl.BlockSpec((SB_BLOCK, SB_GW), lambda g, i: (i, OFF_SB // SB_GW + g))
    k_spec = pl.BlockSpec((t, SB_GW), lambda g, i: (0, g))
    v_spec = pl.BlockSpec((t, SB_GW), lambda g, i: (0, SB_W // SB_GW + g))
    return q_spec, k_spec, v_spec


def _head(ref, h, rows=None):
    cols = slice(h * HD, (h + 1) * HD)
    return ref[:, cols] if rows is None else ref[rows, cols]


def _sb_pair(q, kb, diagonal):
    z = _dot(q, kb, NT)
    lsz = jax.nn.log_sigmoid(z)
    ls = lsz - z
    if diagonal:
        row = lax.broadcasted_iota(jnp.int32, (SB_BLOCK, SB_BLOCK), 0)
        col = lax.broadcasted_iota(jnp.int32, (SB_BLOCK, SB_BLOCK), 1)
        ls = jnp.where(col < row, ls, 0.0)
    return z, lsz, ls


def _sb_weights(lsz, ls, later, run, diagonal):
    a = jnp.exp(lsz + _hdot(ls, later) + run)
    if diagonal:
        row = lax.broadcasted_iota(jnp.int32, (SB_BLOCK, SB_BLOCK), 0)
        col = lax.broadcasted_iota(jnp.int32, (SB_BLOCK, SB_BLOCK), 1)
        a = jnp.where(col < row, a, 0.0)
    return a


def _sb_fwd(proj, kv):
    t = proj.shape[0]
    scale = HD ** -0.5

    def body(q_ref, k_ref, v_ref, o_ref):
        i = pl.program_id(1)
        qs = [(_head(q_ref, h) * scale).astype(BF16) for h in range(SB_GROUP)]
        row = lax.broadcasted_iota(jnp.int32, (SB_BLOCK, SB_BLOCK), 0)
        col = lax.broadcasted_iota(jnp.int32, (SB_BLOCK, SB_BLOCK), 1)
        later = (row > col).astype(F32)

        def pair(j, carry, diagonal):
            rows = pl.ds(pl.multiple_of(j * SB_BLOCK, SB_BLOCK), SB_BLOCK)
            out = []
            for h in range(SB_GROUP):
                acc, run = carry[h]
                z, lsz, ls = _sb_pair(qs[h], _head(k_ref, h, rows), diagonal)
                a = _sb_weights(lsz, ls, later, run, diagonal)
                out.append((acc + _dot(a.astype(BF16), _head(v_ref, h, rows)), run + jnp.sum(ls, axis=1, keepdims=True)))
            return tuple(out)

        zero = tuple((jnp.zeros((SB_BLOCK, HD), F32), jnp.zeros((SB_BLOCK, 1), F32)) for _ in range(SB_GROUP))
        carry = pair(i, zero, True)
        carry = lax.fori_loop(0, i, lambda jj, c: pair(i - 1 - jj, c, False), carry)
        for h in range(SB_GROUP):
            o_ref[:, h * HD:(h + 1) * HD] = carry[h][0].astype(o_ref.dtype)

    return pl.pallas_call(
        body, name="sb_fwd", grid=(HEADS // SB_GROUP, t // SB_BLOCK), in_specs=list(_sb_specs(t)),
        out_specs=pl.BlockSpec((SB_BLOCK, SB_GW), lambda g, i: (i, g)), out_shape=_sds((t, SB_W), BF16),
        compiler_params=_cp("parallel", "arbitrary"),
    )(proj, kv, kv)


def _sb_bwd(proj, kv, dy):
    t = proj.shape[0]
    nq = t // SB_BLOCK
    scale = HD ** -0.5

    def body(q_ref, k_ref, v_ref, do_ref, dq_ref, dk_ref, dv_ref, z_scr, e_scr):
        i = pl.program_id(1)

        @pl.when(i == 0)
        def _():
            dk_ref[...] = jnp.zeros_like(dk_ref)
            dv_ref[...] = jnp.zeros_like(dv_ref)

        qs = [(_head(q_ref, h) * scale).astype(BF16) for h in range(SB_GROUP)]
        dos = [_head(do_ref, h).astype(BF16) for h in range(SB_GROUP)]
        row = lax.broadcasted_iota(jnp.int32, (SB_BLOCK, SB_BLOCK), 0)
        col = lax.broadcasted_iota(jnp.int32, (SB_BLOCK, SB_BLOCK), 1)
        later = (row > col).astype(F32)
        earlier = (row < col).astype(F32)

        def down(j, runs, diagonal):
            rows = pl.ds(pl.multiple_of(j * SB_BLOCK, SB_BLOCK), SB_BLOCK)
            out = []
            for h in range(SB_GROUP):
                z, lsz, ls = _sb_pair(qs[h], _head(k_ref, h, rows), diagonal)
                a = _sb_weights(lsz, ls, later, runs[h], diagonal)
                z_scr[h, j] = z
                e_scr[h, j] = a * _dot(dos[h], _head(v_ref, h, rows), NT)
                dv_ref[rows, h * HD:(h + 1) * HD] += _dot(a.astype(BF16), dos[h], TN)
                out.append(runs[h] + jnp.sum(ls, axis=1, keepdims=True))
            return tuple(out)

        zero = tuple(jnp.zeros((SB_BLOCK, 1), F32) for _ in range(SB_GROUP))
        runs = down(i, zero, True)
        lax.fori_loop(0, i, lambda jj, r: down(i - 1 - jj, r, False), runs)

        def up(j, carry, diagonal):
            rows = pl.ds(pl.multiple_of(j * SB_BLOCK, SB_BLOCK), SB_BLOCK)
            out = []
            for h in range(SB_GROUP):
                dq, run = carry[h]
                z, e = z_scr[h, j], e_scr[h, j]
                sz = jax.nn.sigmoid(z)
                dz = e * (1.0 - sz) - (_hdot(e, earlier) + run) * sz
                if diagonal:
                    dz = jnp.where(col < row, dz, 0.0)
                dz = dz.astype(BF16)
                dk_ref[rows, h * HD:(h + 1) * HD] += _dot(dz, qs[h], TN)
                out.append((dq + _dot(dz, _head(k_ref, h, rows)), run + jnp.sum(e, axis=1, keepdims=True)))
            return tuple(out)

        zero = tuple((jnp.zeros((SB_BLOCK, HD), F32), jnp.zeros((SB_BLOCK, 1), F32)) for _ in range(SB_GROUP))
        carry = lax.fori_loop(0, i, lambda j, c: up(j, c, False), zero)
        carry = up(i, carry, True)
        for h in range(SB_GROUP):
            dq_ref[:, h * HD:(h + 1) * HD] = (carry[h][0] * scale).astype(dq_ref.dtype)

    blk = pl.BlockSpec((SB_BLOCK, SB_GW), lambda g, i: (i, g))
    seq = pl.BlockSpec((t, SB_GW), lambda g, i: (0, g))
    return pl.pallas_call(
        body, name="sb_bwd", grid=(HEADS // SB_GROUP, nq), in_specs=[*_sb_specs(t), blk], out_specs=[blk, seq, seq],
        out_shape=[_sds((t, SB_W), BF16), _sds((t, SB_W), F32), _sds((t, SB_W), F32)],
        scratch_shapes=[pltpu.VMEM((SB_GROUP, nq, SB_BLOCK, SB_BLOCK), F32), pltpu.VMEM((SB_GROUP, nq, SB_BLOCK, SB_BLOCK), F32)],
        compiler_params=_cp("parallel", "arbitrary"),
    )(proj, kv, kv, dy)


CONV_TILE = 256
GDN_CONV = 4


def _conv_pre(x, w_ref, t_idx):
    pre = w_ref[GDN_CONV - 1:GDN_CONV, :] * x
    for s in range(1, GDN_CONV):
        pre = pre + w_ref[GDN_CONV - 1 - s:GDN_CONV - s, :] * _shift_down(x, s, t_idx)
    return pre


def _conv_fwd(proj, conv_w):
    t = proj.shape[0]
    width = conv_w.shape[1]

    def body(x_ref, w_ref, y_ref):
        t_idx = lax.broadcasted_iota(jnp.int32, (t, CONV_TILE), 0)
        pre = _conv_pre(x_ref[...], w_ref, t_idx)
        y_ref[...] = pre * jax.nn.sigmoid(pre)

    return pl.pallas_call(
        body, name="conv_fwd", grid=(width // CONV_TILE,),
        in_specs=[pl.BlockSpec((t, CONV_TILE), lambda c: (0, OFF_GQKV // CONV_TILE + c)),
                  pl.BlockSpec((GDN_CONV, CONV_TILE), lambda c: (0, c))],
        out_specs=pl.BlockSpec((t, CONV_TILE), lambda c: (0, c)), out_shape=_sds((t, width), F32),
        compiler_params=_cp("parallel"),
    )(proj, conv_w)


def _conv_bwd(proj, conv_w, dc):
    t = proj.shape[0]
    width = dc.shape[1]
    per = width // CONV_TILE
    part = 0

    def body(x_ref, w_ref, dc_ref, dx_ref, dw_ref):
        t_idx = lax.broadcasted_iota(jnp.int32, (t, CONV_TILE), 0)
        x = x_ref[...]
        pre = _conv_pre(x, w_ref, t_idx)
        sg = jax.nn.sigmoid(pre)
        dpre = dc_ref[...] * (sg * (1.0 + pre * (1.0 - sg)))
        dx = w_ref[GDN_CONV - 1:GDN_CONV, :] * dpre
        dw_ref[GDN_CONV - 1:GDN_CONV, :] = jnp.sum(dpre * x, axis=0, keepdims=True)
        for s in range(1, GDN_CONV):
            dx = dx + w_ref[GDN_CONV - 1 - s:GDN_CONV - s, :] * _shift_up(dpre, s, t_idx, t)
            dw_ref[GDN_CONV - 1 - s:GDN_CONV - s, :] = jnp.sum(dpre * _shift_down(x, s, t_idx), axis=0, keepdims=True)
        dx_ref[...] = dx.astype(dx_ref.dtype)

    return pl.pallas_call(
        body, name="conv_bwd", grid=(per,),
        in_specs=[pl.BlockSpec((t, CONV_TILE), lambda c: (0, OFF_GQKV // CONV_TILE + part * per + c)),
                  pl.BlockSpec((GDN_CONV, CONV_TILE), lambda c: (0, part * per + c)),
                  pl.BlockSpec((t, CONV_TILE), lambda c: (0, c))],
        out_specs=[pl.BlockSpec((t, CONV_TILE), lambda c: (0, c)), pl.BlockSpec((GDN_CONV, CONV_TILE), lambda c: (0, c))],
        out_shape=[_sds((t, width), BF16), _sds((GDN_CONV, width), F32)],
        compiler_params=_cp("parallel"),
    )(proj, conv_w, dc)


def _gdn_prep(cq, ck, cv, ab, alog_row, dtb_row, h):
    c = GDN_CHUNK
    row = lax.broadcasted_iota(jnp.int32, (c, c), 0)
    col = lax.broadcasted_iota(jnp.int32, (c, c), 1)
    incl, strict, eye = row >= col, row > col, row == col
    a_col, b_col = ab[:, h:h + 1], ab[:, HEADS + h:HEADS + h + 1]
    a_log, dt_bias = alog_row[:, h:h + 1], dtb_row[:, h:h + 1]
    qn = cq * lax.rsqrt(jnp.sum(cq * cq, axis=-1, keepdims=True) + EPS) * (HD ** -0.5)
    kn = ck * lax.rsqrt(jnp.sum(ck * ck, axis=-1, keepdims=True) + EPS)
    la_col = -jnp.exp(a_log) * jax.nn.softplus(a_col + dt_bias)
    beta = jax.nn.sigmoid(b_col)
    la_row = jnp.sum(jnp.where(eye, la_col, 0.0), axis=0, keepdims=True)
    g_col = jnp.sum(jnp.where(incl, la_row, 0.0), axis=1, keepdims=True)
    g_row = jnp.sum(jnp.where(row <= col, la_col, 0.0), axis=0, keepdims=True)
    g_last = jnp.sum(la_col, axis=0, keepdims=True)
    gamma = jnp.where(incl, jnp.exp(jnp.where(incl, g_col - g_row, 0.0)), 0.0)
    lower = jnp.where(strict, beta * _hdot(kn, kn, NT) * gamma, 0.0)
    inv = jnp.where(eye, 1.0, 0.0) - lower
    pw = _hdot(lower, lower)
    for step in range(5):
        inv = inv + _hdot(inv, pw)
        if step < 4:
            pw = _hdot(pw, pw)
    u = _hdot(inv, cv * beta)
    w = _hdot(inv, kn * (beta * jnp.exp(g_col)))
    qk = _hdot(qn, kn, NT) * gamma
    return u, w, qk, qn * jnp.exp(g_col), kn * jnp.exp(g_last - g_col), jnp.exp(g_last)


def _gdn_post(o, z, gain):
    y = o * lax.rsqrt(jnp.mean(o * o, axis=-1, keepdims=True) + EPS) * gain
    return y * (z * jax.nn.sigmoid(z))


def _gdn_specs(nc, reverse):
    c = GDN_CHUNK

    def ch(n):
        return nc - 1 - n if reverse else n

    def wide(array_off):
        return pl.BlockSpec((c, GDN_W), lambda n: (ch(n), array_off // GDN_W))

    ab = pl.BlockSpec((c, HD), lambda n: (ch(n), OFF_AB // HD))
    row = pl.BlockSpec((1, HD), lambda n: (0, 0))
    state = pl.BlockSpec((None, HEADS, HD, HD), lambda n: (ch(n), 0, 0, 0))
    return wide, ab, row, state


def _gdn_fwd(cqkv, proj, a_log, dt_bias, gain):
    t = proj.shape[0]
    nc = t // GDN_CHUNK
    wide, ab, row, state = _gdn_specs(nc, False)

    def body(cq_ref, ck_ref, cv_ref, ab_ref, z_ref, al_ref, dt_ref, g_ref, y_ref, sprev_ref, s_scr):
        @pl.when(pl.program_id(0) == 0)
        def _():
            s_scr[...] = jnp.zeros_like(s_scr)

        for h in range(HEADS):
            u, w, qk, qd, kd, dec = _gdn_prep(_head(cq_ref, h), _head(ck_ref, h), _head(cv_ref, h), ab_ref[...],
                                              al_ref[...], dt_ref[...], h)
            s = s_scr[h]
            sprev_ref[h] = s
            v_new = u - _hdot(w, s)
            o = _hdot(qd, s) + _hdot(qk, v_new)
            s_scr[h] = s * dec + _hdot(kd, v_new, TN)
            y_ref[:, h * HD:(h + 1) * HD] = _gdn_post(o, _head(z_ref, h), g_ref[...]).astype(y_ref.dtype)

    return pl.pallas_call(
        body, name="gdn_fwd", grid=(nc,),
        in_specs=[wide(0), wide(GDN_W), wide(2 * GDN_W), ab, wide(OFF_Z), row, row, row],
        out_specs=[wide(0), state], out_shape=[_sds((t, GDN_W), BF16), _sds((nc, HEADS, HD, HD), F32)],
        scratch_shapes=[pltpu.VMEM((HEADS, HD, HD), F32)], compiler_params=_cp("arbitrary"),
    )(cqkv, cqkv, cqkv, proj, proj, a_log, dt_bias, gain)


def _gdn_bwd(cqkv, proj, a_log, dt_bias, gain, sprev, dy):
    t = proj.shape[0]
    nc = t // GDN_CHUNK
    wide, ab, row, state = _gdn_specs(nc, True)

    def body(cq_ref, ck_ref, cv_ref, ab_ref, z_ref, al_ref, dt_ref, g_ref, sp_ref, dy_ref,
             dc_ref, dab_ref, dz_ref, dal_ref, ddt_ref, dg_ref, ds_scr):
        @pl.when(pl.program_id(0) == 0)
        def _():
            ds_scr[...] = jnp.zeros_like(ds_scr)
            dal_ref[...] = jnp.zeros_like(dal_ref)
            ddt_ref[...] = jnp.zeros_like(ddt_ref)
            dg_ref[...] = jnp.zeros_like(dg_ref)

        dab_sum = jnp.zeros(dab_ref.shape, F32)
        for h in range(HEADS):
            (u, w, qk, qd, kd, dec), prep_vjp = jax.vjp(
                functools.partial(_gdn_prep, h=h),
                _head(cq_ref, h), _head(ck_ref, h), _head(cv_ref, h), ab_ref[...], al_ref[...], dt_ref[...])
            s = sp_ref[h]
            v_new = u - _hdot(w, s)
            o = _hdot(qd, s) + _hdot(qk, v_new)
            _, post_vjp = jax.vjp(_gdn_post, o, _head(z_ref, h), g_ref[...])
            do, dz, dgain = post_vjp(_head(dy_ref, h).astype(F32))
            ds_next = ds_scr[h]
            d_vnew = _hdot(qk, do, TN) + _hdot(kd, ds_next)
            d_qk = _hdot(do, v_new, NT)
            d_qd = _hdot(do, s, NT)
            d_kd = _hdot(v_new, ds_next, NT)
            d_dec = jnp.sum(jnp.sum(s * ds_next, axis=1, keepdims=True), axis=0, keepdims=True)
            ds_scr[h] = dec * ds_next + _hdot(qd, do, TN) - _hdot(w, d_vnew, TN)
            d_w = -_hdot(d_vnew, s, NT)
            dcq, dck, dcv, dab, dal, ddt = prep_vjp((d_vnew, d_w, d_qk, d_qd, d_kd, d_dec))
            dc_ref[:, h * HD:(h + 1) * HD] = dcq
            dc_ref[:, GDN_W + h * HD:GDN_W + (h + 1) * HD] = dck
            dc_ref[:, 2 * GDN_W + h * HD:2 * GDN_W + (h + 1) * HD] = dcv
            dz_ref[:, h * HD:(h + 1) * HD] = dz.astype(dz_ref.dtype)
            dab_sum = dab_sum + dab
            dal_ref[...] += dal
            ddt_ref[...] += ddt
            dg_ref[...] += dgain
        dab_ref[...] = dab_sum

    c = GDN_CHUNK
    return pl.pallas_call(
        body, name="gdn_bwd", grid=(nc,),
        in_specs=[wide(0), wide(GDN_W), wide(2 * GDN_W), ab, wide(OFF_Z), row, row, row, state, wide(0)],
        out_specs=[pl.BlockSpec((c, 3 * GDN_W), lambda n: (nc - 1 - n, 0)), pl.BlockSpec((c, HD), lambda n: (nc - 1 - n, 0)),
                   wide(0), row, row, row],
        out_shape=[_sds((t, 3 * GDN_W), F32), _sds((t, HD), F32), _sds((t, GDN_W), BF16),
                   _sds((1, HD), F32), _sds((1, HD), F32), _sds((1, HD), F32)],
        scratch_shapes=[pltpu.VMEM((HEADS, HD, HD), F32)], compiler_params=_cp("arbitrary"),
    )(cqkv, cqkv, cqkv, proj, proj, a_log, dt_bias, gain, sprev, dy)


MERGE_TN = 512


def _merge_specs(t, tm, l):
    tn = MERGE_TN
    ys = [pl.BlockSpec((tm, wd), lambda i, j: (i, 0)) for wd in (POOL_W, SB_W, GDN_W)]
    ws = [pl.BlockSpec((None, None, wd, tn), lambda i, j: (j, l, 0, 0)) for wd in (POOL_W, SB_W, GDN_W)]
    gs = [pl.BlockSpec((tm, tn), functools.partial(lambda i, j, b: (i, OFF_GATE // tn + b * (D // tn) + j), b=b))
          for b in range(3)]
    out = pl.BlockSpec((tm, tn), lambda i, j: (i, j))
    return ys, ws, gs, out


def _merge_fwd(ys, wups, proj, l):
    t = proj.shape[0]
    tm = min(512, t)
    y_specs, w_specs, g_specs, out = _merge_specs(t, tm, l)

    def body(y0, y1, y2, w0, w1, w2, g0, g1, g2, o_ref):
        acc = jnp.zeros(o_ref.shape, F32)
        for y, w, g in ((y0, w0, g0), (y1, w1, g1), (y2, w2, g2)):
            acc = acc + jax.nn.sigmoid(g[...]) * _dot(y[...], w[...])
        o_ref[...] = acc.astype(o_ref.dtype)

    return pl.pallas_call(
        body, name="merge_fwd", grid=(t // tm, D // MERGE_TN), in_specs=[*y_specs, *w_specs, *g_specs],
        out_specs=out, out_shape=_sds((t, D), BF16), compiler_params=_cp("parallel", "parallel"),
    )(*ys, *wups, proj, proj, proj)


def _merge_bwd(ys, wups, proj, dmerged, l):
    t = proj.shape[0]
    tm = min(512, t)
    y_specs, w_specs, g_specs, out = _merge_specs(t, tm, l)

    def body(y0, y1, y2, w0, w1, w2, g0, g1, g2, dm_ref, dg0, dg1, dg2, dm0, dm1, dm2):
        dm = dm_ref[...].astype(F32)
        for y, w, g, dg, dmb in ((y0, w0, g0, dg0, dm0), (y1, w1, g1, dg1, dm1), (y2, w2, g2, dg2, dm2)):
            sg = jax.nn.sigmoid(g[...])
            dg[...] = (dm * _dot(y[...], w[...]) * sg * (1.0 - sg)).astype(dg.dtype)
            dmb[...] = (dm * sg).astype(dmb.dtype)

    return pl.pallas_call(
        body, name="merge_bwd", grid=(t // tm, D // MERGE_TN), in_specs=[*y_specs, *w_specs, *g_specs, out],
        out_specs=[out] * 6, out_shape=[_sds((t, D), BF16)] * 6, compiler_params=_cp("parallel", "parallel"),
    )(*ys, *wups, proj, proj, proj, dmerged)


def _tile(t, want):
    return min(t, want)


def _layer_fwd(x, l, gw, w_al, sp):
    t = x.shape[0]
    tm = _tile(t, 1024)
    u = _rms_fwd("rms_attn", x, sp["attn_norm"][l])
    proj = _mm("proj", u, w_al, m=t, n=N_AL, k=D, tm=tm, tn=1024, tk=512, a_spec=_a_plain(tm, 512),
               b_spec=_b_plain(512, 1024), dims=None, out_shapes=[_sds((t, N_AL), F32)], out_specs=[_o_plain(tm, 1024)])[0]
    y_pool = _pool_fwd(proj, sp["pool_w"][l], sp["pool_scale"][l])
    kv = _sb_cast_kv(proj)
    y_sb = _sb_fwd(proj, kv)
    cqkv = _conv_fwd(proj, sp["conv"][l])
    y_gdn, sprev = _gdn_fwd(cqkv, proj, sp["a_log"][l], sp["dt_bias"][l], sp["gdn_norm"][l])
    ys = (y_pool, y_sb, y_gdn)
    wups = (gw["w_pool_up"], gw["w_sb_up"], gw["w_gdn_up"])
    merged = _merge_fwd(ys, wups, proj, l)
    x1 = _mm("out_proj", merged, gw["w_out"], m=t, n=D, k=D, tm=tm, tn=1024, tk=512, a_spec=_a_plain(tm, 512),
             b_spec=_w_rows(l, 512, 1024, 512), dims=None, out_shapes=[_sds((t, D), F32)], out_specs=[_o_plain(tm, 1024)],
             extras=[x], extra_specs=[_o_plain(tm, 1024)], epilogue=lambda r, xr: (r + xr,))[0]
    u2 = _rms_fwd("rms_mlp", x1, sp["mlp_norm"][l])

    def relu2(r):
        hv = jnp.maximum(r, 0.0)
        return hv, hv * hv

    hid, hid2 = _mm("ff1", u2, gw["w_ff1"], m=t, n=D_FF, k=D, tm=tm, tn=1024, tk=512, a_spec=_a_plain(tm, 512),
                    b_spec=_w_cols(l, 512, 1024, 2048), dims=None, out_shapes=[_sds((t, D_FF), BF16)] * 2,
                    out_specs=[_o_plain(tm, 1024)] * 2, epilogue=relu2)
    x2 = _mm("ff2", hid2, gw["w_ff2"], m=t, n=D, k=D_FF, tm=tm, tn=1024, tk=512, a_spec=_a_plain(tm, 512),
             b_spec=_w_rows(l, 512, 1024, 2048), dims=None, out_shapes=[_sds((t, D), F32)], out_specs=[_o_plain(tm, 1024)],
             extras=[x1], extra_specs=[_o_plain(tm, 1024)], epilogue=lambda r, xr: (r + xr,))[0]
    saved = dict(x=x, u=u, proj=proj, kv=kv, cqkv=cqkv, sprev=sprev, ys=ys, merged=merged, x1=x1, u2=u2, hid=hid, hid2=hid2)
    return x2, saved


def _layer_bwd(dx2, l, gw, w_al, sp, sv):
    t = dx2.shape[0]
    tm = _tile(t, 1024)
    tk = _tile(t, 512)
    g = {}
    dpre = _mm("ff2_dx", dx2, gw["w_ff2"], m=t, n=D_FF, k=D, tm=tm, tn=1024, tk=512, a_spec=_a_plain(tm, 512),
               b_spec=_w_rows_t(l, 512, 1024, 2048), dims=NT, out_shapes=[_sds((t, D_FF), BF16)],
               out_specs=[_o_plain(tm, 1024)], extras=[sv["hid"]], extra_specs=[_o_plain(tm, 1024)],
               epilogue=lambda r, hv: (r * (2.0 * hv.astype(F32)),))[0]
    g["w_ff2"] = _mm("ff2_dw", sv["hid2"], dx2, m=D_FF, n=D, k=t, tm=1024, tn=1024, tk=tk, a_spec=_a_trans(1024, tk),
                     b_spec=_b_plain(tk, 1024), dims=TN, out_shapes=[_sds((D_FF, D), BF16)],
                     out_specs=[_o_plain(1024, 1024)])[0].reshape(N_CHIPS, D_FF // N_CHIPS, D)
    du2 = _mm("ff1_dx", dpre, gw["w_ff1"], m=t, n=D, k=D_FF, tm=tm, tn=1024, tk=512, a_spec=_a_plain(tm, 512),
              b_spec=_w_cols_t(l, 512, 1024, 2048), dims=NT, out_shapes=[_sds((t, D), F32)], out_specs=[_o_plain(tm, 1024)])[0]
    g["w_ff1"] = _mm("ff1_dw", sv["u2"], dpre, m=D, n=D_FF, k=t, tm=1024, tn=1024, tk=tk, a_spec=_a_trans(1024, tk),
                     b_spec=_b_plain(tk, 1024), dims=TN, out_shapes=[_sds((N_CHIPS, D, D_FF // N_CHIPS), BF16)],
                     out_specs=[_o_colshard(1024, 1024, D_FF // N_CHIPS)])[0]
    dx1, g["mlp_norm"] = _rms_bwd("rms_mlp_bwd", du2, sv["x1"], sp["mlp_norm"][l], dx2)
    dmerged = _mm("out_dx", dx1, gw["w_out"], m=t, n=D, k=D, tm=tm, tn=512, tk=1024, a_spec=_a_plain(tm, 1024),
                  b_spec=_w_rows_t(l, 1024, 512, 512), dims=NT, out_shapes=[_sds((t, D), BF16)], out_specs=[_o_plain(tm, 512)])[0]
    g["w_out"] = _mm("out_dw", sv["merged"], dx1, m=D, n=D, k=t, tm=1024, tn=1024, tk=tk, a_spec=_a_trans(1024, tk),
                     b_spec=_b_plain(tk, 1024), dims=TN, out_shapes=[_sds((D, D), BF16)],
                     out_specs=[_o_plain(1024, 1024)])[0].reshape(N_CHIPS, D // N_CHIPS, D)
    wups = (gw["w_pool_up"], gw["w_sb_up"], gw["w_gdn_up"])
    dg0, dg1, dg2, dm0, dm1, dm2 = _merge_bwd(sv["ys"], wups, sv["proj"], dmerged, l)
    dys = []
    for nm, yb, dmb, wd in zip(("w_pool_up", "w_sb_up", "w_gdn_up"), sv["ys"], (dm0, dm1, dm2), (POOL_W, SB_W, GDN_W)):
        dys.append(_mm(nm + "_dx", dmb, gw[nm], m=t, n=wd, k=D, tm=tm, tn=256, tk=512, a_spec=_a_plain(tm, 512),
                       b_spec=_w_cols_t(l, 512, 256, 512), dims=NT, out_shapes=[_sds((t, wd), F32)],
                       out_specs=[_o_plain(tm, 256)])[0])
        g[nm] = _mm(nm + "_dw", yb, dmb, m=wd, n=D, k=t, tm=256, tn=512, tk=tk, a_spec=_a_trans(256, tk),
                    b_spec=_b_plain(tk, 512), dims=TN, out_shapes=[_sds((N_CHIPS, wd, D // N_CHIPS), BF16)],
                    out_specs=[_o_colshard(256, 512, D // N_CHIPS)])[0]
    proj = sv["proj"]
    dp, g["pool_w"], g["pool_scale"] = _pool_bwd(proj, sp["pool_w"][l], sp["pool_scale"][l], dys[0])
    dsq, dsk, dsv = _sb_bwd(proj, sv["kv"], dys[1])
    dc, dab, dz, g["a_log"], g["dt_bias"], g["gdn_norm"] = _gdn_bwd(
        sv["cqkv"], proj, sp["a_log"][l], sp["dt_bias"][l], sp["gdn_norm"][l], sv["sprev"], dys[2])
    dgx, g["conv"] = _conv_bwd(proj, sp["conv"][l], dc)
    dproj = jnp.concatenate(
        [dsq, dsk.astype(BF16), dsv.astype(BF16), dgx, dz, dab.astype(BF16), jnp.zeros((t, AB_W - HD), BF16),
         dp, dg0, dg1, dg2], axis=1)
    du = _mm("proj_dx", dproj, w_al, m=t, n=D, k=N_AL, tm=tm, tn=1024, tk=512, a_spec=_a_plain(tm, 512),
             b_spec=_b_trans(512, 1024), dims=NT, out_shapes=[_sds((t, D), F32)], out_specs=[_o_plain(tm, 1024)])[0]
    g["w_al"] = _mm("proj_dw", sv["u"], dproj, m=D, n=N_AL, k=t, tm=1024, tn=1024, tk=tk, a_spec=_a_trans(1024, tk),
                    b_spec=_b_plain(tk, 1024), dims=TN, out_shapes=[_sds((D, N_AL), BF16)], out_specs=[_o_plain(1024, 1024)])[0]
    dx, g["attn_norm"] = _rms_bwd("rms_attn_bwd", du, sv["x"], sp["attn_norm"][l], dx1)
    return dx, g


def _align_w_in(w):
    n_ab = ORIG_GATE - ORIG_AB
    return jnp.concatenate([w[:, ORIG_SB:ORIG_GATE], jnp.zeros((D, AB_W - n_ab), w.dtype), w[:, :ORIG_SB], w[:, ORIG_GATE:]],
                           axis=1)


def _unalign_w_in(w):
    n_ab = ORIG_GATE - ORIG_AB
    return jnp.concatenate([w[:, OFF_P:OFF_GATE], w[:, :OFF_AB + n_ab], w[:, OFF_GATE:]], axis=1)


def _row128(v):
    return jnp.pad(v.reshape(1, -1), ((0, 0), (0, HD - v.shape[-1])))


def _local_step(x, target, gw, w_in_al, sp):
    saved = []
    h = x
    for l in range(2):
        h, sv = _layer_fwd(h, l, gw, w_in_al[l], sp)
        saved.append(sv)
    loss, dh, g_final = _loss_head(h, sp["final_norm"], target)
    grads = [None, None]
    for l in (1, 0):
        dh, grads[l] = _layer_bwd(dh, l, gw, w_in_al[l], sp, saved[l])
    return loss, dh, grads, g_final


ANY = pl.BlockSpec(memory_space=pl.ANY)


def _me():
    return lax.axis_index("x"), lax.axis_index("y"), lax.axis_index("c")


def _other_chips(x, y):
    return [(1 - x, y), (x, 1 - y), (1 - x, 1 - y)]


def _half(ref, axis, c, rows):
    half = rows // 2
    idx = [slice(None)] * axis + [pl.ds(pl.multiple_of(c * half, 16), half)]
    return ref.at[tuple(idx)]


def _gather_weights(bufs):
    n = len(bufs)

    def body(*refs):
        out = refs[n:2 * n]
        send, recv = refs[2 * n:]
        x, y, c = _me()
        mine = 2 * x + y
        sibling = (x, y, 1 - c)
        chips = _other_chips(x, y)
        sends = []
        for t in range(n):
            rows = out[t].shape[2]
            for k, (px, py) in enumerate(chips):
                own_half = _half(out[t].at[mine], 1, c, rows)
                cp = pltpu.make_async_remote_copy(
                    src_ref=own_half, dst_ref=own_half,
                    send_sem=send.at[6 * t + k], recv_sem=recv.at[6 * t + k], device_id=(px, py, c), device_id_type=MESH)
                cp.start()
                sends.append(cp)
        for t in range(n):
            rows = out[t].shape[2]
            for k, (px, py) in enumerate(chips):
                landed = _half(out[t].at[2 * px + py], 1, c, rows)
                pltpu.make_async_remote_copy(
                    src_ref=landed, dst_ref=landed, send_sem=send.at[6 * t + k], recv_sem=recv.at[6 * t + k],
                    device_id=(px, py, c), device_id_type=MESH).wait_recv()
                cp = pltpu.make_async_remote_copy(
                    src_ref=landed, dst_ref=landed, send_sem=send.at[6 * t + 3 + k], recv_sem=recv.at[6 * t + 3 + k],
                    device_id=sibling, device_id_type=MESH)
                cp.start()
                sends.append(cp)
        for t in range(n):
            rows = out[t].shape[2]
            for k, (px, py) in enumerate(chips):
                other = _half(out[t].at[2 * px + py], 1, 1 - c, rows)
                pltpu.make_async_remote_copy(
                    src_ref=other, dst_ref=other, send_sem=send.at[6 * t + 3 + k], recv_sem=recv.at[6 * t + 3 + k],
                    device_id=sibling, device_id_type=MESH).wait_recv()
        for cp in sends:
            cp.wait_send()

    return pl.pallas_call(
        body, name="gather_weights", in_specs=[ANY] * n, out_specs=[ANY] * n,
        out_shape=[_sds(s.shape, s.dtype) for s in bufs], input_output_aliases={t: t for t in range(n)},
        scratch_shapes=[pltpu.SemaphoreType.DMA((6 * n,)), pltpu.SemaphoreType.DMA((6 * n,))],
    )(*bufs)


def _rs_pair(grads):
    n = len(grads)

    def body(*refs):
        g, out = refs[:n], refs[n:2 * n]
        send, recv = refs[2 * n:]
        x, y, c = _me()
        copies = []
        for t in range(n):
            cp = pltpu.make_async_remote_copy(
                src_ref=_half(g[t], 1, 1 - c, g[t].shape[1]), dst_ref=out[t], send_sem=send.at[t], recv_sem=recv.at[t],
                device_id=(x, y, 1 - c), device_id_type=MESH)
            cp.start()
            copies.append(cp)
        for cp in copies:
            cp.wait()

    return pl.pallas_call(
        body, name="rs_pair", in_specs=[ANY] * n, out_specs=[ANY] * n,
        out_shape=[_sds((N_CHIPS, s.shape[1] // 2, s.shape[2]), s.dtype) for s in grads],
        scratch_shapes=[pltpu.SemaphoreType.DMA((n,)), pltpu.SemaphoreType.DMA((n,))],
    )(*grads)


def _rs_chips(parts):
    n = len(parts)

    def body(*refs):
        p, out = refs[:n], refs[n:2 * n]
        send, recv = refs[2 * n:]
        x, y, c = _me()
        copies = []
        for t in range(n):
            for k, (px, py) in enumerate(_other_chips(x, y)):
                cp = pltpu.make_async_remote_copy(
                    src_ref=p[t].at[2 * px + py], dst_ref=out[t].at[k], send_sem=send.at[3 * t + k],
                    recv_sem=recv.at[3 * t + k], device_id=(px, py, c), device_id_type=MESH)
                cp.start()
                copies.append(cp)
        for cp in copies:
            cp.wait()

    return pl.pallas_call(
        body, name="rs_chips", in_specs=[ANY] * n, out_specs=[ANY] * n,
        out_shape=[_sds((3, *s.shape[1:]), s.dtype) for s in parts],
        scratch_shapes=[pltpu.SemaphoreType.DMA((3 * n,)), pltpu.SemaphoreType.DMA((3 * n,))],
    )(*parts)


def _pair_exchange(bufs):
    n = len(bufs)

    def body(*refs):
        out = refs[n:2 * n]
        send, recv = refs[2 * n:]
        x, y, c = _me()
        copies = []
        for t in range(n):
            cp = pltpu.make_async_remote_copy(
                src_ref=out[t].at[c], dst_ref=out[t].at[c], send_sem=send.at[t], recv_sem=recv.at[t],
                device_id=(x, y, 1 - c), device_id_type=MESH)
            cp.start()
            copies.append(cp)
        for t, cp in enumerate(copies):
            cp.wait_send()
            pltpu.make_async_remote_copy(
                src_ref=out[t].at[1 - c], dst_ref=out[t].at[1 - c], send_sem=send.at[t], recv_sem=recv.at[t],
                device_id=(x, y, 1 - c), device_id_type=MESH).wait_recv()

    return pl.pallas_call(
        body, name="pair_exchange", in_specs=[ANY] * n, out_specs=[ANY] * n,
        out_shape=[_sds(s.shape, s.dtype) for s in bufs], input_output_aliases={t: t for t in range(n)},
        scratch_shapes=[pltpu.SemaphoreType.DMA((n,)), pltpu.SemaphoreType.DMA((n,))],
    )(*bufs)


def _row_tile(rows, cols, itemsize, budget=2 * 1024 * 1024):
    tr = rows
    while tr * cols * itemsize > budget and tr % 32 == 0:
        tr //= 2
    return tr


def _sum_pair(name, g, got, where):
    nchip, rows, cols = g.shape
    half = rows // 2
    tr = _row_tile(half, cols, 4)
    per = half // tr

    def body(w_ref, g_ref, r_ref, o_ref):
        o_ref[...] = (g_ref[...].astype(F32) + r_ref[...].astype(F32)).astype(o_ref.dtype)

    blk = pl.BlockSpec((None, tr, cols), lambda j, i, w_ref: (j, i, 0))
    return pl.pallas_call(
        body, name=name,
        grid_spec=pltpu.PrefetchScalarGridSpec(
            num_scalar_prefetch=1, grid=(nchip, per),
            in_specs=[pl.BlockSpec((None, tr, cols), lambda j, i, w_ref: (j, w_ref[1] * per + i, 0)), blk], out_specs=blk),
        out_shape=_sds((nchip, half, cols), BF16), compiler_params=_cp("parallel", "parallel"),
    )(where, g, got)


def _sum_chips(name, p, got, where):
    _, rows, cols = p.shape
    tr = _row_tile(rows, cols, 4)

    def body(w_ref, p_ref, r0, r1, r2, o_ref):
        o_ref[...] = ((p_ref[...].astype(F32) + r0[...].astype(F32)) + r1[...].astype(F32)) + r2[...].astype(F32)

    def got_k(k):
        return pl.BlockSpec((None, tr, cols), lambda i, w_ref: (k, i, 0))

    return pl.pallas_call(
        body, name=name,
        grid_spec=pltpu.PrefetchScalarGridSpec(
            num_scalar_prefetch=1, grid=(rows // tr,),
            in_specs=[pl.BlockSpec((None, tr, cols), lambda i, w_ref: (w_ref[0], i, 0)), got_k(0), got_k(1), got_k(2)],
            out_specs=pl.BlockSpec((None, tr, cols), lambda i, w_ref: (w_ref[1], i, 0))),
        out_shape=_sds((2, rows, cols), F32), compiler_params=_cp("parallel"),
    )(where, p, got, got, got)


def _reduce_scatter(grads, where):
    got = _rs_pair(grads)
    parts = [_sum_pair(f"sum_pair_{t}", g, r, where) for t, (g, r) in enumerate(zip(grads, got))]
    got = _rs_chips(parts)
    halves = [_sum_chips(f"sum_chips_{t}", p, r, where) for t, (p, r) in enumerate(zip(parts, got))]
    return [o.reshape(-1, o.shape[-1]) for o in _pair_exchange(halves)]


def _all_reduce_small(name, v):
    rows = v.shape[0]

    def body(v_ref, o_ref, land, send, recv):
        x, y, c = _me()
        mine = 4 * x + 2 * y + c
        copies = []
        for k in range(1, 8):
            kx, ky, kc = k >> 2, (k >> 1) & 1, k & 1
            peer = (x ^ kx, y ^ ky, c ^ kc)
            cp = pltpu.make_async_remote_copy(
                src_ref=v_ref, dst_ref=land.at[mine], send_sem=send.at[k - 1], recv_sem=recv.at[k - 1],
                device_id=peer, device_id_type=MESH)
            cp.start()
            copies.append(cp)
        land[mine] = v_ref[...]
        for k in range(1, 8):
            kx, ky, kc = k >> 2, (k >> 1) & 1, k & 1
            src = 4 * (x ^ kx) + 2 * (y ^ ky) + (c ^ kc)
            pltpu.make_async_remote_copy(
                src_ref=v_ref, dst_ref=land.at[src], send_sem=send.at[k - 1], recv_sem=recv.at[k - 1],
                device_id=(x ^ kx, y ^ ky, c ^ kc), device_id_type=MESH).wait_recv()
        acc = land[0]
        for d in range(1, 8):
            acc = acc + land[d]
        o_ref[...] = acc
        for cp in copies:
            cp.wait_send()

    vm = pl.BlockSpec(memory_space=pltpu.VMEM)
    return pl.pallas_call(
        body, name=name, in_specs=[vm], out_specs=vm, out_shape=_sds((rows, 128), F32),
        scratch_shapes=[pltpu.VMEM((8, rows, 128), F32), pltpu.SemaphoreType.DMA((7,)), pltpu.SemaphoreType.DMA((7,))],
    )(v)


def _adamw(name, w, g, m, v):
    rows, cols = w.shape
    tr = _row_tile(rows, cols, 4, budget=1024 * 1024)
    c1 = 1.0 / (1.0 - ADAM_B1 ** ADAM_STEP)
    c2 = 1.0 / (1.0 - ADAM_B2 ** ADAM_STEP)

    def body(w_ref, g_ref, m_ref, v_ref, d_ref, nm_ref, nv_ref):
        gv = g_ref[...]
        nm = ADAM_B1 * m_ref[...] + (1.0 - ADAM_B1) * gv
        nv = ADAM_B2 * v_ref[...] + (1.0 - ADAM_B2) * (gv * gv)
        d_ref[...] = -ADAM_LR * ((nm * c1) / (jnp.sqrt(nv * c2) + ADAM_EPS) + ADAM_WD * w_ref[...])
        nm_ref[...] = nm
        nv_ref[...] = nv

    blk = pl.BlockSpec((tr, cols), lambda i: (i, 0))
    return pl.pallas_call(
        body, name=name, grid=(rows // tr,), in_specs=[blk] * 4, out_specs=[blk] * 3,
        out_shape=[_sds((rows, cols), F32)] * 3, compiler_params=_cp("parallel"),
    )(w, g, m, v)


def _to_bf16_slot(name, w, where):
    rows, cols = w.shape
    tr = _row_tile(rows, cols, 4)

    def body(w_ref, x_ref, o_ref):
        o_ref[...] = x_ref[...].astype(BF16)

    return pl.pallas_call(
        body, name=name,
        grid_spec=pltpu.PrefetchScalarGridSpec(
            num_scalar_prefetch=1, grid=(rows // tr,), in_specs=[pl.BlockSpec((tr, cols), lambda i, w_ref: (i, 0))],
            out_specs=pl.BlockSpec((None, tr, cols), lambda i, w_ref: (w_ref[0], i, 0))),
        out_shape=_sds((N_CHIPS, rows, cols), BF16), compiler_params=_cp("parallel"))(where, w)


BIG = ("w_in", "w_pool_up", "w_sb_up", "w_gdn_up", "w_out", "w_ff1", "w_ff2")
SMALL = (("attn_norm", (D,)), ("pool_w", (4, 128, 128)), ("pool_scale", (POOL_W,)), ("gdn_a_log", (HEADS,)),
         ("gdn_dt_bias", (HEADS,)), ("gdn_norm", (HD,)), ("mlp_norm", (D,)))


def _rows128(a):
    flat = a.reshape(-1)
    pad = (-flat.shape[0]) % 128
    return jnp.pad(flat, (0, pad)).reshape(-1, 128)


def _pack(parts):
    packed = jnp.concatenate([_rows128(p) for p in parts], axis=0)
    return jnp.pad(packed, ((0, (-packed.shape[0]) % 8), (0, 0)))


def _unpack(packed, shapes):
    out, r = [], 0
    for shp in shapes:
        size = 1
        for s in shp:
            size *= s
        nr = -(-size // 128)
        out.append(packed[r:r + nr].reshape(-1)[:size].reshape(shp))
        r += nr
    return out


def kernel(x, attn_norm, w_in, pool_w, pool_scale, gdn_conv, gdn_a_log, gdn_dt_bias, gdn_norm, w_pool_up, w_sb_up, w_gdn_up, w_out, mlp_norm, w_ff1, w_ff2, final_norm, loss_target, m_attn_norm, m_w_in, m_pool_w, m_pool_scale, m_gdn_conv, m_gdn_a_log, m_gdn_dt_bias, m_gdn_norm, m_w_pool_up, m_w_sb_up, m_w_gdn_up, m_w_out, m_mlp_norm, m_w_ff1, m_w_ff2, m_final_norm, v_attn_norm, v_w_in, v_pool_w, v_pool_scale, v_gdn_conv, v_gdn_a_log, v_gdn_dt_bias, v_gdn_norm, v_w_pool_up, v_w_sb_up, v_w_gdn_up, v_w_out, v_mlp_norm, v_w_ff1, v_w_ff2, v_final_norm):
    weights = dict(attn_norm=attn_norm, w_in=w_in, pool_w=pool_w, pool_scale=pool_scale, gdn_conv=gdn_conv,
                   gdn_a_log=gdn_a_log, gdn_dt_bias=gdn_dt_bias, gdn_norm=gdn_norm, w_pool_up=w_pool_up, w_sb_up=w_sb_up,
                   w_gdn_up=w_gdn_up, w_out=w_out, mlp_norm=mlp_norm, w_ff1=w_ff1, w_ff2=w_ff2, final_norm=final_norm)
    mom1 = dict(attn_norm=m_attn_norm, w_in=m_w_in, pool_w=m_pool_w, pool_scale=m_pool_scale, gdn_conv=m_gdn_conv,
                gdn_a_log=m_gdn_a_log, gdn_dt_bias=m_gdn_dt_bias, gdn_norm=m_gdn_norm, w_pool_up=m_w_pool_up,
                w_sb_up=m_w_sb_up, w_gdn_up=m_w_gdn_up, w_out=m_w_out, mlp_norm=m_mlp_norm, w_ff1=m_w_ff1, w_ff2=m_w_ff2,
                final_norm=m_final_norm)
    mom2 = dict(attn_norm=v_attn_norm, w_in=v_w_in, pool_w=v_pool_w, pool_scale=v_pool_scale, gdn_conv=v_gdn_conv,
                gdn_a_log=v_gdn_a_log, gdn_dt_bias=v_gdn_dt_bias, gdn_norm=v_gdn_norm, w_pool_up=v_w_pool_up,
                w_sb_up=v_w_sb_up, w_gdn_up=v_w_gdn_up, w_out=v_w_out, mlp_norm=v_mlp_norm, w_ff1=v_w_ff1, w_ff2=v_w_ff2,
                final_norm=v_final_norm)
    xi, yi, ci = lax.axis_index("x"), lax.axis_index("y"), lax.axis_index("c")
    chip = 2 * xi + yi
    where = jnp.stack([chip, ci]).astype(jnp.int32)

    bufs = [_to_bf16_slot("cast_" + nm, weights[nm].reshape(-1, weights[nm].shape[-1]), where).reshape(N_CHIPS, *weights[nm].shape)
            for nm in BIG]
    gw = dict(zip(BIG, _gather_weights(bufs)))
    conv_cols = gdn_conv.shape[-1]
    conv_place = lax.dynamic_update_slice(jnp.zeros((2, GDN_CONV, N_CHIPS * conv_cols), F32),
                                          jnp.where(ci == 0, gdn_conv, 0.0), (0, 0, chip * conv_cols))
    conv_full = _all_reduce_small("gather_conv", _rows128(conv_place)).reshape(2, GDN_CONV, N_CHIPS * conv_cols)
    w_in_full = jnp.transpose(gw["w_in"], (1, 2, 0, 3)).reshape(2, D, N_IN)
    w_in_al = [_align_w_in(w_in_full[l]) for l in range(2)]
    sp = dict(attn_norm=attn_norm.reshape(2, 1, D), pool_w=pool_w, pool_scale=pool_scale.reshape(2, 1, POOL_W),
              conv=conv_full, a_log=jnp.stack([_row128(gdn_a_log[l]) for l in range(2)]),
              dt_bias=jnp.stack([_row128(gdn_dt_bias[l]) for l in range(2)]), gdn_norm=gdn_norm.reshape(2, 1, HD),
              mlp_norm=mlp_norm.reshape(2, 1, D), final_norm=final_norm.reshape(1, D))

    loss, grad_x, grads, g_final = _local_step(x[0], loss_target[0], gw, w_in_al, sp)
    loss = lax.psum(loss[0, 0], ("x", "y", "c"))

    big_grads = {nm: [] for nm in BIG}
    for l in range(2):
        g = grads[l]
        w_in_g = _unalign_w_in(g["w_al"]).reshape(D, N_CHIPS, N_IN // N_CHIPS)
        per_layer = [jnp.transpose(w_in_g, (1, 0, 2))] + [g[nm] for nm in BIG[1:]]
        for nm, red in zip(BIG, _reduce_scatter(per_layer, where)):
            big_grads[nm].append(red)
    small_parts, small_shapes = [], []
    for l in range(2):
        g = grads[l]
        for nm, shp in SMALL:
            key = {"gdn_a_log": "a_log", "gdn_dt_bias": "dt_bias"}.get(nm, nm)
            val = g[key]
            small_parts.append(val[0, :HEADS] if nm in ("gdn_a_log", "gdn_dt_bias") else val)
            small_shapes.append(shp)
        small_parts.append(g["conv"])
        small_shapes.append((GDN_CONV, N_CHIPS * conv_cols))
    small_parts.append(g_final)
    small_shapes.append((D,))
    reduced = _unpack(_all_reduce_small("reduce_small", _pack(small_parts)), small_shapes)
    per = len(SMALL) + 1
    grad = {}
    for i, (nm, _) in enumerate(SMALL):
        grad[nm] = jnp.stack([reduced[i], reduced[per + i]])
    conv_g = jnp.stack([reduced[per - 1], reduced[2 * per - 1]])
    grad["gdn_conv"] = lax.dynamic_slice(conv_g, (0, 0, chip * conv_cols), (2, GDN_CONV, conv_cols))
    grad["final_norm"] = reduced[-1]
    for nm in BIG:
        grad[nm] = jnp.stack(big_grads[nm]).reshape(weights[nm].shape)

    delta, new_m, new_v = {}, {}, {}
    for nm in BIG:
        shp = weights[nm].shape
        flat = lambda a: a.reshape(-1, shp[-1])
        d, nm1, nv1 = _adamw("adamw_" + nm, flat(weights[nm]), flat(grad[nm]), flat(mom1[nm]), flat(mom2[nm]))
        delta[nm], new_m[nm], new_v[nm] = d.reshape(shp), nm1.reshape(shp), nv1.reshape(shp)
    small_names = [nm for nm, _ in SMALL] + ["gdn_conv", "final_norm"]
    packs = [_pack([src[nm] for nm in small_names]) for src in (weights, grad, mom1, mom2)]
    outs = _adamw("adamw_small", *packs)
    shapes = [weights[nm].shape for nm in small_names]
    for dst, packed in zip((delta, new_m, new_v), outs):
        for nm, val in zip(small_names, _unpack(packed, shapes)):
            dst[nm] = val

    order = ("attn_norm", "w_in", "pool_w", "pool_scale", "gdn_conv", "gdn_a_log", "gdn_dt_bias", "gdn_norm", "w_pool_up",
             "w_sb_up", "w_gdn_up", "w_out", "mlp_norm", "w_ff1", "w_ff2", "final_norm")
    return (loss, grad_x[None], *[grad[n] for n in order], *[delta[n] for n in order], *[new_m[n] for n in order],
            *[new_v[n] for n in order])
```

```python
import functools

import jax
import jax.numpy as jnp
from jax import lax
from jax.experimental import pallas as pl
from jax.experimental.pallas import tpu as pltpu

F32, BF16 = jnp.float32, jnp.bfloat16
HIGH = lax.Precision.HIGH
MESH = pl.DeviceIdType.MESH

D = 2048
EPS = 1e-6
POOL_WINDOWS = (2, 4, 8, 16)
POOL_W, SB_W, GDN_W = 512, 768, 768
HEADS, HD = 6, 128
SB_BLOCK = 128
GDN_CHUNK = 64
D_FF = 4 * D
N_IN = 12044
N_CHIPS = 4
OFF_SB, OFF_GQKV, OFF_Z, OFF_AB, OFF_P, OFF_GATE = 0, 2304, 4608, 5376, 5632, 6144
AB_W = 256
ORIG_SB, ORIG_AB, ORIG_GATE = 512, 5888, 5900
N_AL = 12288
VMEM_LIMIT = 48 * 1024 * 1024

ADAM_LR, ADAM_B1, ADAM_B2, ADAM_EPS, ADAM_WD, ADAM_STEP = 0.001, 0.9, 0.999, 1e-08, 0.01, 10

NT = (((1,), (1,)), ((), ()))
TN = (((0,), (0,)), ((), ()))


def _cp(*sem):
    return pltpu.CompilerParams(dimension_semantics=sem, vmem_limit_bytes=VMEM_LIMIT)


def _dot(a, b, dims=None, precision=None):
    if dims is None:
        dims = (((a.ndim - 1,), (0,)), ((), ()))
    return lax.dot_general(a, b, dims, precision=precision, preferred_element_type=F32)


def _hdot(a, b, dims=None):
    return _dot(a, b, dims, precision=HIGH)


def _bdot(a, b, dims=None):
    return _dot(a.astype(BF16), b.astype(BF16), dims)


def _mm(name, a, b, *, m, n, k, tm, tn, tk, a_spec, b_spec, dims, out_shapes, out_specs,
        extras=(), extra_specs=(), epilogue=None):
    nk = k // tk
    ne, no = len(extras), len(out_shapes)

    def body(*refs):
        a_ref, b_ref = refs[0], refs[1]
        ex = refs[2:2 + ne]
        outs = refs[2 + ne:2 + ne + no]
        acc = refs[-1]
        kk = pl.program_id(2)

        @pl.when(kk == 0)
        def _():
            acc[...] = jnp.zeros_like(acc)

        acc[...] += _dot(a_ref[...].astype(BF16), b_ref[...].astype(BF16), dims)

        @pl.when(kk == nk - 1)
        def _():
            r = acc[...]
            res = epilogue(r, *[e[...] for e in ex]) if epilogue is not None else (r,)
            for o, v in zip(outs, res):
                o[...] = v.astype(o.dtype)

    return pl.pallas_call(
        body, name=name, grid=(m // tm, n // tn, nk),
        in_specs=[a_spec, b_spec, *extra_specs], out_specs=out_specs, out_shape=out_shapes,
        scratch_shapes=[pltpu.VMEM((tm, tn), F32)],
        compiler_params=_cp("parallel", "parallel", "arbitrary"),
    )(a, b, *extras)


def _a_plain(tm, tk):
    return pl.BlockSpec((tm, tk), lambda i, j, kk: (i, kk))


def _a_trans(tm, tk):
    return pl.BlockSpec((tk, tm), lambda i, j, kk: (kk, i))


def _b_plain(tk, tn):
    return pl.BlockSpec((tk, tn), lambda i, j, kk: (kk, j))


def _b_trans(tk, tn):
    return pl.BlockSpec((tn, tk), lambda i, j, kk: (j, kk))


def _o_plain(tm, tn):
    return pl.BlockSpec((tm, tn), lambda i, j, kk: (i, j))


def _o_colshard(tm, tn, ns_cols):
    per = ns_cols // tn
    return pl.BlockSpec((None, tm, tn), lambda i, j, kk: (j // per, i, j % per))


def _w_cols(l, tk, tn, ns):
    per = ns // tn
    return pl.BlockSpec((None, None, tk, tn), lambda i, j, kk: (j // per, l, kk, j % per))


def _w_cols_t(l, tk, tn, ns):
    per = ns // tk
    return pl.BlockSpec((None, None, tn, tk), lambda i, j, kk: (kk // per, l, j, kk % per))


def _w_rows(l, tk, tn, ks):
    per = ks // tk
    return pl.BlockSpec((None, None, tk, tn), lambda i, j, kk: (kk // per, l, kk % per, j))


def _w_rows_t(l, tk, tn, ks):
    per = ks // tn
    return pl.BlockSpec((None, None, tn, tk), lambda i, j, kk: (j // per, l, j % per, kk))


def _sds(shape, dtype):
    return jax.ShapeDtypeStruct(shape, dtype)


def _rms_fwd(name, x, gain):
    t = x.shape[0]
    tt = min(256, t)

    def body(x_ref, g_ref, u_ref):
        xv = x_ref[...]
        r = lax.rsqrt(jnp.mean(xv * xv, axis=-1, keepdims=True) + EPS)
        u_ref[...] = (xv * r * g_ref[...]).astype(u_ref.dtype)

    return pl.pallas_call(
        body, name=name, grid=(t // tt,),
        in_specs=[pl.BlockSpec((tt, D), lambda i: (i, 0)), pl.BlockSpec((1, D), lambda i: (0, 0))],
        out_specs=pl.BlockSpec((tt, D), lambda i: (i, 0)), out_shape=_sds((t, D), BF16),
        compiler_params=_cp("parallel"),
    )(x, gain)


def _rms_bwd(name, du, x, gain, dres):
    t = x.shape[0]
    tt = min(256, t)

    def body(du_ref, x_ref, g_ref, dres_ref, dx_ref, dg_ref):
        @pl.when(pl.program_id(0) == 0)
        def _():
            dg_ref[...] = jnp.zeros_like(dg_ref)

        xv, duv = x_ref[...], du_ref[...]
        r = lax.rsqrt(jnp.mean(xv * xv, axis=-1, keepdims=True) + EPS)
        nx = xv * r
        dn = duv * g_ref[...]
        dg_ref[...] += jnp.sum(duv * nx, axis=0, keepdims=True)
        dx_ref[...] = dres_ref[...] + r * (dn - nx * jnp.mean(dn * nx, axis=-1, keepdims=True))

    row = pl.BlockSpec((tt, D), lambda i: (i, 0))
    vec = pl.BlockSpec((1, D), lambda i: (0, 0))
    return pl.pallas_call(
        body, name=name, grid=(t // tt,), in_specs=[row, row, vec, row], out_specs=[row, vec],
        out_shape=[_sds((t, D), F32), _sds((1, D), F32)], compiler_params=_cp("arbitrary"),
    )(du, x, gain, dres)


def _loss_head(x, gain, target):
    t = x.shape[0]
    tt = min(256, t)

    def body(x_ref, g_ref, t_ref, loss_ref, dx_ref, dg_ref):
        @pl.when(pl.program_id(0) == 0)
        def _():
            dg_ref[...] = jnp.zeros_like(dg_ref)
            loss_ref[...] = jnp.zeros_like(loss_ref)

        xv = x_ref[...]
        r = lax.rsqrt(jnp.mean(xv * xv, axis=-1, keepdims=True) + EPS)
        nx = xv * r
        err = nx * g_ref[...] - t_ref[...]
        loss_ref[...] += 0.5 * jnp.sum(jnp.mean(err * err, axis=-1, keepdims=True), axis=0, keepdims=True)
        dy = err * (1.0 / D)
        dn = dy * g_ref[...]
        dg_ref[...] += jnp.sum(dy * nx, axis=0, keepdims=True)
        dx_ref[...] = r * (dn - nx * jnp.mean(dn * nx, axis=-1, keepdims=True))

    row = pl.BlockSpec((tt, D), lambda i: (i, 0))
    vec = pl.BlockSpec((1, D), lambda i: (0, 0))
    one = pl.BlockSpec((1, 1), lambda i: (0, 0))
    return pl.pallas_call(
        body, name="loss_head", grid=(t // tt,), in_specs=[row, vec, row], out_specs=[one, row, vec],
        out_shape=[_sds((1, 1), F32), _sds((t, D), F32), _sds((1, D), F32)], compiler_params=_cp("arbitrary"),
    )(x, gain, target)


def _shift_down(v, s, t_idx):
    return jnp.where(t_idx >= s, pltpu.roll(v, s, 0), 0.0)


def _shift_up(v, s, t_idx, t):
    return jnp.where(t_idx < t - s, pltpu.roll(v, t - s, 0), 0.0)


def _pool_d(p, g, t_idx):
    s = p
    for step in range(g + 1):
        s = s + _shift_down(s, 1 << step, t_idx)
    cnt = jnp.minimum(t_idx + 1, POOL_WINDOWS[g]).astype(F32)
    return s / cnt - p, cnt


def _pool_fwd(proj, pool_w, pool_scale):
    t = proj.shape[0]
    g128 = POOL_W // len(POOL_WINDOWS)

    def body(p_ref, w_ref, s_ref, y_ref):
        t_idx = lax.broadcasted_iota(jnp.int32, (t, g128), 0)
        for g in range(len(POOL_WINDOWS)):
            sl = slice(g * g128, (g + 1) * g128)
            d, _ = _pool_d(p_ref[:, sl], g, t_idx)
            y_ref[:, sl] = (_bdot(d, w_ref[g]) * s_ref[:, sl]).astype(y_ref.dtype)

    return pl.pallas_call(
        body, name="pool_fwd", grid=(1,),
        in_specs=[pl.BlockSpec((t, POOL_W), lambda i: (0, OFF_P // POOL_W)),
                  pl.BlockSpec((4, g128, g128), lambda i: (0, 0, 0)), pl.BlockSpec((1, POOL_W), lambda i: (0, 0))],
        out_specs=pl.BlockSpec((t, POOL_W), lambda i: (0, 0)), out_shape=_sds((t, POOL_W), BF16),
        compiler_params=_cp("arbitrary"),
    )(proj, pool_w, pool_scale)


def _pool_bwd(proj, pool_w, pool_scale, dy):
    t = proj.shape[0]
    g128 = POOL_W // len(POOL_WINDOWS)

    def body(p_ref, w_ref, s_ref, dy_ref, dp_ref, dw_ref, ds_ref):
        t_idx = lax.broadcasted_iota(jnp.int32, (t, g128), 0)
        for g in range(len(POOL_WINDOWS)):
            sl = slice(g * g128, (g + 1) * g128)
            d, cnt = _pool_d(p_ref[:, sl], g, t_idx)
            dyv = dy_ref[:, sl].astype(F32)
            ds_ref[:, sl] = jnp.sum(dyv * _bdot(d, w_ref[g]), axis=0, keepdims=True)
            dys = dyv * s_ref[:, sl]
            dw_ref[g] = _bdot(d, dys, TN)
            dd = _bdot(dys, w_ref[g], NT)
            s = dd / cnt
            for step in range(g + 1):
                s = s + _shift_up(s, 1 << step, t_idx, t)
            dp_ref[:, sl] = (s - dd).astype(dp_ref.dtype)

    return pl.pallas_call(
        body, name="pool_bwd", grid=(1,),
        in_specs=[pl.BlockSpec((t, POOL_W), lambda i: (0, OFF_P // POOL_W)),
                  pl.BlockSpec((4, g128, g128), lambda i: (0, 0, 0)), pl.BlockSpec((1, POOL_W), lambda i: (0, 0)),
                  pl.BlockSpec((t, POOL_W), lambda i: (0, 0))],
        out_specs=[pl.BlockSpec((t, POOL_W), lambda i: (0, 0)), pl.BlockSpec((4, g128, g128), lambda i: (0, 0, 0)),
                   pl.BlockSpec((1, POOL_W), lambda i: (0, 0))],
        out_shape=[_sds((t, POOL_W), BF16), _sds((4, g128, g128), F32), _sds((1, POOL_W), F32)],
        compiler_params=_cp("arbitrary"),
    )(proj, pool_w, pool_scale, dy)


SB_GROUP = 3
SB_GW = SB_GROUP * HD


def _sb_cast_kv(proj):
    t = proj.shape[0]
    tt = min(512, t)

    def body(x_ref, o_ref):
        o_ref[...] = x_ref[...].astype(BF16)

    return pl.pallas_call(
        body, name="sb_cast_kv", grid=(t // tt, 2),
        in_specs=[pl.BlockSpec((tt, SB_W), lambda i, j: (i, OFF_SB // SB_W + 1 + j))],
        out_specs=pl.BlockSpec((tt, SB_W), lambda i, j: (i, j)), out_shape=_sds((t, 2 * SB_W), BF16),
        compiler_params=_cp("parallel", "parallel"),
    )(proj)


def _sb_specs(t):
    q_spec = pl.BlockSpec((SB_BLOCK, SB_GW), lambda g, i: (i, OFF_SB // SB_GW + g))
    k_spec = pl.BlockSpec((t, SB_GW), lambda g, i: (0, g))
    v_spec = pl.BlockSpec((t, SB_GW), lambda g, i: (0, SB_W // SB_GW + g))
    return q_spec, k_spec, v_spec


def _head(ref, h, rows=None):
    cols = slice(h * HD, (h + 1) * HD)
    return ref[:, cols] if rows is None else ref[rows, cols]


SB_KEYS = 512


def _sub(v, b):
    return v[:, b * SB_BLOCK:(b + 1) * SB_BLOCK]


def _sb_keep(kc, limit):
    row = lax.broadcasted_iota(jnp.int32, (SB_BLOCK, kc), 0)
    col = lax.broadcasted_iota(jnp.int32, (SB_BLOCK, kc), 1)
    return col < row + limit


def _sb_chunk(q, keys, run, later, limit):
    kc = keys.shape[0]
    z = _dot(q, keys, NT)
    lsz = jax.nn.log_sigmoid(z)
    ls = lsz - z
    if limit is not None:
        keep = _sb_keep(kc, limit)
        ls = jnp.where(keep, ls, 0.0)
    parts = [None] * (kc // SB_BLOCK)
    for b in reversed(range(kc // SB_BLOCK)):
        parts[b] = _hdot(_sub(ls, b), later) + run
        run = run + jnp.sum(_sub(ls, b), axis=1, keepdims=True)
    a = jnp.exp(lsz + jnp.concatenate(parts, axis=1))
    if limit is not None:
        a = jnp.where(keep, a, 0.0)
    return z, a, run


def _sb_fwd(proj, kv):
    t = proj.shape[0]
    kc = min(SB_KEYS, t)
    scale = HD ** -0.5

    def body(q_ref, k_ref, v_ref, o_ref):
        i = pl.program_id(1)
        top = (i * SB_BLOCK) // kc
        qs = [(_head(q_ref, h) * scale).astype(BF16) for h in range(SB_GROUP)]
        row = lax.broadcasted_iota(jnp.int32, (SB_BLOCK, SB_BLOCK), 0)
        col = lax.broadcasted_iota(jnp.int32, (SB_BLOCK, SB_BLOCK), 1)
        later = (row > col).astype(F32)

        def chunk(jc, carry, masked):
            rows = pl.ds(pl.multiple_of(jc * kc, kc), kc)
            limit = i * SB_BLOCK - jc * kc if masked else None
            out = []
            for h in range(SB_GROUP):
                acc, run = carry[h]
                _, a, run = _sb_chunk(qs[h], _head(k_ref, h, rows), run, later, limit)
                out.append((acc + _dot(a.astype(BF16), _head(v_ref, h, rows)), run))
            return tuple(out)

        zero = tuple((jnp.zeros((SB_BLOCK, HD), F32), jnp.zeros((SB_BLOCK, 1), F32)) for _ in range(SB_GROUP))
        carry = chunk(top, zero, True)
        carry = lax.fori_loop(0, top, lambda jj, c: chunk(top - 1 - jj, c, False), carry)
        for h in range(SB_GROUP):
            o_ref[:, h * HD:(h + 1) * HD] = carry[h][0].astype(o_ref.dtype)

    return pl.pallas_call(
        body, name="sb_fwd", grid=(HEADS // SB_GROUP, t // SB_BLOCK), in_specs=list(_sb_specs(t)),
        out_specs=pl.BlockSpec((SB_BLOCK, SB_GW), lambda g, i: (i, g)), out_shape=_sds((t, SB_W), BF16),
        compiler_params=_cp("parallel", "arbitrary"),
    )(proj, kv, kv)


def _sb_bwd(proj, kv, dy):
    t = proj.shape[0]
    nq = t // SB_BLOCK
    kc = min(SB_KEYS, t)
    scale = HD ** -0.5

    def body(q_ref, k_ref, v_ref, do_ref, dq_ref, dk_ref, dv_ref, z_scr, e_scr):
        i = pl.program_id(1)
        top = (i * SB_BLOCK) // kc

        @pl.when(i == 0)
        def _():
            dk_ref[...] = jnp.zeros_like(dk_ref)
            dv_ref[...] = jnp.zeros_like(dv_ref)

        qs = [(_head(q_ref, h) * scale).astype(BF16) for h in range(SB_GROUP)]
        dos = [_head(do_ref, h).astype(BF16) for h in range(SB_GROUP)]
        row = lax.broadcasted_iota(jnp.int32, (SB_BLOCK, SB_BLOCK), 0)
        col = lax.broadcasted_iota(jnp.int32, (SB_BLOCK, SB_BLOCK), 1)
        later = (row > col).astype(F32)
        earlier = (row < col).astype(F32)

        def down(jc, runs, masked):
            rows = pl.ds(pl.multiple_of(jc * kc, kc), kc)
            limit = i * SB_BLOCK - jc * kc if masked else None
            out = []
            for h in range(SB_GROUP):
                z, a, run = _sb_chunk(qs[h], _head(k_ref, h, rows), runs[h], later, limit)
                z_scr[h, jc] = z
                e_scr[h, jc] = a * _dot(dos[h], _head(v_ref, h, rows), NT)
                dv_ref[rows, h * HD:(h + 1) * HD] += _dot(a.astype(BF16), dos[h], TN)
                out.append(run)
            return tuple(out)

        zero = tuple(jnp.zeros((SB_BLOCK, 1), F32) for _ in range(SB_GROUP))
        runs = down(top, zero, True)
        lax.fori_loop(0, top, lambda jj, r: down(top - 1 - jj, r, False), runs)

        def up(jc, carry, masked):
            rows = pl.ds(pl.multiple_of(jc * kc, kc), kc)
            out = []
            for h in range(SB_GROUP):
                dq, run = carry[h]
                z, e = z_scr[h, jc], e_scr[h, jc]
                parts = []
                for b in range(kc // SB_BLOCK):
                    parts.append(_hdot(_sub(e, b), earlier) + run)
                    run = run + jnp.sum(_sub(e, b), axis=1, keepdims=True)
                sz = jax.nn.sigmoid(z)
                dz = e * (1.0 - sz) - jnp.concatenate(parts, axis=1) * sz
                if masked:
                    dz = jnp.where(_sb_keep(kc, i * SB_BLOCK - jc * kc), dz, 0.0)
                dz = dz.astype(BF16)
                dk_ref[rows, h * HD:(h + 1) * HD] += _dot(dz, qs[h], TN)
                out.append((dq + _dot(dz, _head(k_ref, h, rows)), run))
            return tuple(out)

        zero = tuple((jnp.zeros((SB_BLOCK, HD), F32), jnp.zeros((SB_BLOCK, 1), F32)) for _ in range(SB_GROUP))
        carry = lax.fori_loop(0, top, lambda jc, c: up(jc, c, False), zero)
        carry = up(top, carry, True)
        for h in range(SB_GROUP):
            dq_ref[:, h * HD:(h + 1) * HD] = (carry[h][0] * scale).astype(dq_ref.dtype)

    blk = pl.BlockSpec((SB_BLOCK, SB_GW), lambda g, i: (i, g))
    seq = pl.BlockSpec((t, SB_GW), lambda g, i: (0, g))
    scratch = pltpu.VMEM((SB_GROUP, t // kc, SB_BLOCK, kc), F32)
    return pl.pallas_call(
        body, name="sb_bwd", grid=(HEADS // SB_GROUP, nq), in_specs=[*_sb_specs(t), blk], out_specs=[blk, seq, seq],
        out_shape=[_sds((t, SB_W), BF16), _sds((t, SB_W), F32), _sds((t, SB_W), F32)],
        scratch_shapes=[scratch, scratch], compiler_params=_cp("parallel", "arbitrary"),
    )(proj, kv, kv, dy)


CONV_TILE = 256
GDN_CONV = 4


def _conv_pre(x, w_ref, t_idx):
    pre = w_ref[GDN_CONV - 1:GDN_CONV, :] * x
    for s in range(1, GDN_CONV):
        pre = pre + w_ref[GDN_CONV - 1 - s:GDN_CONV - s, :] * _shift_down(x, s, t_idx)
    return pre


def _conv_fwd(proj, conv_w):
    t = proj.shape[0]
    width = conv_w.shape[1]

    def body(x_ref, w_ref, y_ref):
        t_idx = lax.broadcasted_iota(jnp.int32, (t, CONV_TILE), 0)
        pre = _conv_pre(x_ref[...], w_ref, t_idx)
        y_ref[...] = pre * jax.nn.sigmoid(pre)

    return pl.pallas_call(
        body, name="conv_fwd", grid=(width // CONV_TILE,),
        in_specs=[pl.BlockSpec((t, CONV_TILE), lambda c: (0, OFF_GQKV // CONV_TILE + c)),
                  pl.BlockSpec((GDN_CONV, CONV_TILE), lambda c: (0, c))],
        out_specs=pl.BlockSpec((t, CONV_TILE), lambda c: (0, c)), out_shape=_sds((t, width), F32),
        compiler_params=_cp("parallel"),
    )(proj, conv_w)


def _conv_bwd(proj, conv_w, dc):
    t = proj.shape[0]
    width = dc.shape[1]
    per = width // CONV_TILE
    part = 0

    def body(x_ref, w_ref, dc_ref, dx_ref, dw_ref):
        t_idx = lax.broadcasted_iota(jnp.int32, (t, CONV_TILE), 0)
        x = x_ref[...]
        pre = _conv_pre(x, w_ref, t_idx)
        sg = jax.nn.sigmoid(pre)
        dpre = dc_ref[...] * (sg * (1.0 + pre * (1.0 - sg)))
        dx = w_ref[GDN_CONV - 1:GDN_CONV, :] * dpre
        dw_ref[GDN_CONV - 1:GDN_CONV, :] = jnp.sum(dpre * x, axis=0, keepdims=True)
        for s in range(1, GDN_CONV):
            dx = dx + w_ref[GDN_CONV - 1 - s:GDN_CONV - s, :] * _shift_up(dpre, s, t_idx, t)
            dw_ref[GDN_CONV - 1 - s:GDN_CONV - s, :] = jnp.sum(dpre * _shift_down(x, s, t_idx), axis=0, keepdims=True)
        dx_ref[...] = dx.astype(dx_ref.dtype)

    return pl.pallas_call(
        body, name="conv_bwd", grid=(per,),
        in_specs=[pl.BlockSpec((t, CONV_TILE), lambda c: (0, OFF_GQKV // CONV_TILE + part * per + c)),
                  pl.BlockSpec((GDN_CONV, CONV_TILE), lambda c: (0, part * per + c)),
                  pl.BlockSpec((t, CONV_TILE), lambda c: (0, c))],
        out_specs=[pl.BlockSpec((t, CONV_TILE), lambda c: (0, c)), pl.BlockSpec((GDN_CONV, CONV_TILE), lambda c: (0, c))],
        out_shape=[_sds((t, width), BF16), _sds((GDN_CONV, width), F32)],
        compiler_params=_cp("parallel"),
    )(proj, conv_w, dc)


def _gdn_prep(cq, ck, cv, ab, alog_row, dtb_row, h):
    c = GDN_CHUNK
    row = lax.broadcasted_iota(jnp.int32, (c, c), 0)
    col = lax.broadcasted_iota(jnp.int32, (c, c), 1)
    incl, strict, eye = row >= col, row > col, row == col
    a_col, b_col = ab[:, h:h + 1], ab[:, HEADS + h:HEADS + h + 1]
    a_log, dt_bias = alog_row[:, h:h + 1], dtb_row[:, h:h + 1]
    qn = cq * lax.rsqrt(jnp.sum(cq * cq, axis=-1, keepdims=True) + EPS) * (HD ** -0.5)
    kn = ck * lax.rsqrt(jnp.sum(ck * ck, axis=-1, keepdims=True) + EPS)
    la_col = -jnp.exp(a_log) * jax.nn.softplus(a_col + dt_bias)
    beta = jax.nn.sigmoid(b_col)
    la_row = jnp.sum(jnp.where(eye, la_col, 0.0), axis=0, keepdims=True)
    g_col = jnp.sum(jnp.where(incl, la_row, 0.0), axis=1, keepdims=True)
    g_row = jnp.sum(jnp.where(row <= col, la_col, 0.0), axis=0, keepdims=True)
    g_last = jnp.sum(la_col, axis=0, keepdims=True)
    gamma = jnp.where(incl, jnp.exp(jnp.where(incl, g_col - g_row, 0.0)), 0.0)
    lower = jnp.where(strict, beta * _hdot(kn, kn, NT) * gamma, 0.0)
    inv = jnp.where(eye, 1.0, 0.0) - lower
    pw = _hdot(lower, lower)
    for step in range(5):
        inv = inv + _hdot(inv, pw)
        if step < 4:
            pw = _hdot(pw, pw)
    u = _hdot(inv, cv * beta)
    w = _hdot(inv, kn * (beta * jnp.exp(g_col)))
    qk = _hdot(qn, kn, NT) * gamma
    return u, w, qk, qn * jnp.exp(g_col), kn * jnp.exp(g_last - g_col), jnp.exp(g_last)


def _gdn_post(o, z, gain):
    y = o * lax.rsqrt(jnp.mean(o * o, axis=-1, keepdims=True) + EPS) * gain
    return y * (z * jax.nn.sigmoid(z))


def _gdn_specs(nc, reverse):
    c = GDN_CHUNK

    def ch(n):
        return nc - 1 - n if reverse else n

    def wide(array_off):
        return pl.BlockSpec((c, GDN_W), lambda n: (ch(n), array_off // GDN_W))

    ab = pl.BlockSpec((c, HD), lambda n: (ch(n), OFF_AB // HD))
    row = pl.BlockSpec((1, HD), lambda n: (0, 0))
    state = pl.BlockSpec((None, HEADS, HD, HD), lambda n: (ch(n), 0, 0, 0))
    return wide, ab, row, state


def _gdn_fwd(cqkv, proj, a_log, dt_bias, gain):
    t = proj.shape[0]
    nc = t // GDN_CHUNK
    wide, ab, row, state = _gdn_specs(nc, False)

    def body(cq_ref, ck_ref, cv_ref, ab_ref, z_ref, al_ref, dt_ref, g_ref, y_ref, sprev_ref, s_scr):
        @pl.when(pl.program_id(0) == 0)
        def _():
            s_scr[...] = jnp.zeros_like(s_scr)

        for h in range(HEADS):
            u, w, qk, qd, kd, dec = _gdn_prep(_head(cq_ref, h), _head(ck_ref, h), _head(cv_ref, h), ab_ref[...],
                                              al_ref[...], dt_ref[...], h)
            s = s_scr[h]
            sprev_ref[h] = s
            v_new = u - _hdot(w, s)
            o = _hdot(qd, s) + _hdot(qk, v_new)
            s_scr[h] = s * dec + _hdot(kd, v_new, TN)
            y_ref[:, h * HD:(h + 1) * HD] = _gdn_post(o, _head(z_ref, h), g_ref[...]).astype(y_ref.dtype)

    return pl.pallas_call(
        body, name="gdn_fwd", grid=(nc,),
        in_specs=[wide(0), wide(GDN_W), wide(2 * GDN_W), ab, wide(OFF_Z), row, row, row],
        out_specs=[wide(0), state], out_shape=[_sds((t, GDN_W), BF16), _sds((nc, HEADS, HD, HD), F32)],
        scratch_shapes=[pltpu.VMEM((HEADS, HD, HD), F32)], compiler_params=_cp("arbitrary"),
    )(cqkv, cqkv, cqkv, proj, proj, a_log, dt_bias, gain)


def _gdn_bwd(cqkv, proj, a_log, dt_bias, gain, sprev, dy):
    t = proj.shape[0]
    nc = t // GDN_CHUNK
    wide, ab, row, state = _gdn_specs(nc, True)

    def body(cq_ref, ck_ref, cv_ref, ab_ref, z_ref, al_ref, dt_ref, g_ref, sp_ref, dy_ref,
             dc_ref, dab_ref, dz_ref, dal_ref, ddt_ref, dg_ref, ds_scr):
        @pl.when(pl.program_id(0) == 0)
        def _():
            ds_scr[...] = jnp.zeros_like(ds_scr)
            dal_ref[...] = jnp.zeros_like(dal_ref)
            ddt_ref[...] = jnp.zeros_like(ddt_ref)
            dg_ref[...] = jnp.zeros_like(dg_ref)

        dab_sum = jnp.zeros(dab_ref.shape, F32)
        for h in range(HEADS):
            (u, w, qk, qd, kd, dec), prep_vjp = jax.vjp(
                functools.partial(_gdn_prep, h=h),
                _head(cq_ref, h), _head(ck_ref, h), _head(cv_ref, h), ab_ref[...], al_ref[...], dt_ref[...])
            s = sp_ref[h]
            v_new = u - _hdot(w, s)
            o = _hdot(qd, s) + _hdot(qk, v_new)
            _, post_vjp = jax.vjp(_gdn_post, o, _head(z_ref, h), g_ref[...])
            do, dz, dgain = post_vjp(_head(dy_ref, h).astype(F32))
            ds_next = ds_scr[h]
            d_vnew = _hdot(qk, do, TN) + _hdot(kd, ds_next)
            d_qk = _hdot(do, v_new, NT)
            d_qd = _hdot(do, s, NT)
            d_kd = _hdot(v_new, ds_next, NT)
            d_dec = jnp.sum(jnp.sum(s * ds_next, axis=1, keepdims=True), axis=0, keepdims=True)
            ds_scr[h] = dec * ds_next + _hdot(qd, do, TN) - _hdot(w, d_vnew, TN)
            d_w = -_hdot(d_vnew, s, NT)
            dcq, dck, dcv, dab, dal, ddt = prep_vjp((d_vnew, d_w, d_qk, d_qd, d_kd, d_dec))
            dc_ref[:, h * HD:(h + 1) * HD] = dcq
            dc_ref[:, GDN_W + h * HD:GDN_W + (h + 1) * HD] = dck
            dc_ref[:, 2 * GDN_W + h * HD:2 * GDN_W + (h + 1) * HD] = dcv
            dz_ref[:, h * HD:(h + 1) * HD] = dz.astype(dz_ref.dtype)
            dab_sum = dab_sum + dab
            dal_ref[...] += dal
            ddt_ref[...] += ddt
            dg_ref[...] += dgain
        dab_ref[...] = dab_sum

    c = GDN_CHUNK
    return pl.pallas_call(
        body, name="gdn_bwd", grid=(nc,),
        in_specs=[wide(0), wide(GDN_W), wide(2 * GDN_W), ab, wide(OFF_Z), row, row, row, state, wide(0)],
        out_specs=[pl.BlockSpec((c, 3 * GDN_W), lambda n: (nc - 1 - n, 0)), pl.BlockSpec((c, HD), lambda n: (nc - 1 - n, 0)),
                   wide(0), row, row, row],
        out_shape=[_sds((t, 3 * GDN_W), F32), _sds((t, HD), F32), _sds((t, GDN_W), BF16),
                   _sds((1, HD), F32), _sds((1, HD), F32), _sds((1, HD), F32)],
        scratch_shapes=[pltpu.VMEM((HEADS, HD, HD), F32)], compiler_params=_cp("arbitrary"),
    )(cqkv, cqkv, cqkv, proj, proj, a_log, dt_bias, gain, sprev, dy)


MERGE_TN = 512


def _merge_specs(t, tm, l):
    tn = MERGE_TN
    ys = [pl.BlockSpec((tm, wd), lambda i, j: (i, 0)) for wd in (POOL_W, SB_W, GDN_W)]
    ws = [pl.BlockSpec((None, None, wd, tn), lambda i, j: (j, l, 0, 0)) for wd in (POOL_W, SB_W, GDN_W)]
    gs = [pl.BlockSpec((tm, tn), functools.partial(lambda i, j, b: (i, OFF_GATE // tn + b * (D // tn) + j), b=b))
          for b in range(3)]
    out = pl.BlockSpec((tm, tn), lambda i, j: (i, j))
    return ys, ws, gs, out


def _merge_fwd(ys, wups, proj, l):
    t = proj.shape[0]
    tm = min(512, t)
    y_specs, w_specs, g_specs, out = _merge_specs(t, tm, l)

    def body(y0, y1, y2, w0, w1, w2, g0, g1, g2, o_ref):
        acc = jnp.zeros(o_ref.shape, F32)
        for y, w, g in ((y0, w0, g0), (y1, w1, g1), (y2, w2, g2)):
            acc = acc + jax.nn.sigmoid(g[...]) * _dot(y[...], w[...])
        o_ref[...] = acc.astype(o_ref.dtype)

    return pl.pallas_call(
        body, name="merge_fwd", grid=(t // tm, D // MERGE_TN), in_specs=[*y_specs, *w_specs, *g_specs],
        out_specs=out, out_shape=_sds((t, D), BF16), compiler_params=_cp("parallel", "parallel"),
    )(*ys, *wups, proj, proj, proj)


def _merge_bwd(ys, wups, proj, dmerged, l):
    t = proj.shape[0]
    tm = min(512, t)
    y_specs, w_specs, g_specs, out = _merge_specs(t, tm, l)

    def body(y0, y1, y2, w0, w1, w2, g0, g1, g2, dm_ref, dg0, dg1, dg2, dm0, dm1, dm2):
        dm = dm_ref[...].astype(F32)
        for y, w, g, dg, dmb in ((y0, w0, g0, dg0, dm0), (y1, w1, g1, dg1, dm1), (y2, w2, g2, dg2, dm2)):
            sg = jax.nn.sigmoid(g[...])
            dg[...] = (dm * _dot(y[...], w[...]) * sg * (1.0 - sg)).astype(dg.dtype)
            dmb[...] = (dm * sg).astype(dmb.dtype)

    return pl.pallas_call(
        body, name="merge_bwd", grid=(t // tm, D // MERGE_TN), in_specs=[*y_specs, *w_specs, *g_specs, out],
        out_specs=[out] * 6, out_shape=[_sds((t, D), BF16)] * 6, compiler_params=_cp("parallel", "parallel"),
    )(*ys, *wups, proj, proj, proj, dmerged)


def _tile(t, want):
    return min(t, want)


def _layer_fwd(x, l, gw, w_al, sp):
    t = x.shape[0]
    tm = _tile(t, 1024)
    u = _rms_fwd("rms_attn", x, sp["attn_norm"][l])
    proj = _mm("proj", u, w_al, m=t, n=N_AL, k=D, tm=tm, tn=1024, tk=512, a_spec=_a_plain(tm, 512),
               b_spec=_b_plain(512, 1024), dims=None, out_shapes=[_sds((t, N_AL), F32)], out_specs=[_o_plain(tm, 1024)])[0]
    y_pool = _pool_fwd(proj, sp["pool_w"][l], sp["pool_scale"][l])
    kv = _sb_cast_kv(proj)
    y_sb = _sb_fwd(proj, kv)
    cqkv = _conv_fwd(proj, sp["conv"][l])
    y_gdn, sprev = _gdn_fwd(cqkv, proj, sp["a_log"][l], sp["dt_bias"][l], sp["gdn_norm"][l])
    ys = (y_pool, y_sb, y_gdn)
    wups = (gw["w_pool_up"], gw["w_sb_up"], gw["w_gdn_up"])
    merged = _merge_fwd(ys, wups, proj, l)
    x1 = _mm("out_proj", merged, gw["w_out"], m=t, n=D, k=D, tm=tm, tn=1024, tk=512, a_spec=_a_plain(tm, 512),
             b_spec=_w_rows(l, 512, 1024, 512), dims=None, out_shapes=[_sds((t, D), F32)], out_specs=[_o_plain(tm, 1024)],
             extras=[x], extra_specs=[_o_plain(tm, 1024)], epilogue=lambda r, xr: (r + xr,))[0]
    u2 = _rms_fwd("rms_mlp", x1, sp["mlp_norm"][l])

    def relu2(r):
        hv = jnp.maximum(r, 0.0)
        return hv, hv * hv

    hid, hid2 = _mm("ff1", u2, gw["w_ff1"], m=t, n=D_FF, k=D, tm=tm, tn=1024, tk=512, a_spec=_a_plain(tm, 512),
                    b_spec=_w_cols(l, 512, 1024, 2048), dims=None, out_shapes=[_sds((t, D_FF), BF16)] * 2,
                    out_specs=[_o_plain(tm, 1024)] * 2, epilogue=relu2)
    x2 = _mm("ff2", hid2, gw["w_ff2"], m=t, n=D, k=D_FF, tm=tm, tn=1024, tk=512, a_spec=_a_plain(tm, 512),
             b_spec=_w_rows(l, 512, 1024, 2048), dims=None, out_shapes=[_sds((t, D), F32)], out_specs=[_o_plain(tm, 1024)],
             extras=[x1], extra_specs=[_o_plain(tm, 1024)], epilogue=lambda r, xr: (r + xr,))[0]
    saved = dict(x=x, u=u, proj=proj, kv=kv, cqkv=cqkv, sprev=sprev, ys=ys, merged=merged, x1=x1, u2=u2, hid=hid, hid2=hid2)
    return x2, saved


def _layer_bwd(dx2, l, gw, w_al, sp, sv):
    t = dx2.shape[0]
    tm = _tile(t, 1024)
    tk = _tile(t, 512)
    g = {}
    dpre = _mm("ff2_dx", dx2, gw["w_ff2"], m=t, n=D_FF, k=D, tm=tm, tn=1024, tk=512, a_spec=_a_plain(tm, 512),
               b_spec=_w_rows_t(l, 512, 1024, 2048), dims=NT, out_shapes=[_sds((t, D_FF), BF16)],
               out_specs=[_o_plain(tm, 1024)], extras=[sv["hid"]], extra_specs=[_o_plain(tm, 1024)],
               epilogue=lambda r, hv: (r * (2.0 * hv.astype(F32)),))[0]
    g["w_ff2"] = _mm("ff2_dw", sv["hid2"], dx2, m=D_FF, n=D, k=t, tm=1024, tn=1024, tk=tk, a_spec=_a_trans(1024, tk),
                     b_spec=_b_plain(tk, 1024), dims=TN, out_shapes=[_sds((D_FF, D), BF16)],
                     out_specs=[_o_plain(1024, 1024)])[0].reshape(N_CHIPS, D_FF // N_CHIPS, D)
    du2 = _mm("ff1_dx", dpre, gw["w_ff1"], m=t, n=D, k=D_FF, tm=tm, tn=1024, tk=512, a_spec=_a_plain(tm, 512),
              b_spec=_w_cols_t(l, 512, 1024, 2048), dims=NT, out_shapes=[_sds((t, D), F32)], out_specs=[_o_plain(tm, 1024)])[0]
    g["w_ff1"] = _mm("ff1_dw", sv["u2"], dpre, m=D, n=D_FF, k=t, tm=1024, tn=1024, tk=tk, a_spec=_a_trans(1024, tk),
                     b_spec=_b_plain(tk, 1024), dims=TN, out_shapes=[_sds((N_CHIPS, D, D_FF // N_CHIPS), BF16)],
                     out_specs=[_o_colshard(1024, 1024, D_FF // N_CHIPS)])[0]
    dx1, g["mlp_norm"] = _rms_bwd("rms_mlp_bwd", du2, sv["x1"], sp["mlp_norm"][l], dx2)
    dmerged = _mm("out_dx", dx1, gw["w_out"], m=t, n=D, k=D, tm=tm, tn=512, tk=1024, a_spec=_a_plain(tm, 1024),
                  b_spec=_w_rows_t(l, 1024, 512, 512), dims=NT, out_shapes=[_sds((t, D), BF16)], out_specs=[_o_plain(tm, 512)])[0]
    g["w_out"] = _mm("out_dw", sv["merged"], dx1, m=D, n=D, k=t, tm=1024, tn=1024, tk=tk, a_spec=_a_trans(1024, tk),
                     b_spec=_b_plain(tk, 1024), dims=TN, out_shapes=[_sds((D, D), BF16)],
                     out_specs=[_o_plain(1024, 1024)])[0].reshape(N_CHIPS, D // N_CHIPS, D)
    wups = (gw["w_pool_up"], gw["w_sb_up"], gw["w_gdn_up"])
    dg0, dg1, dg2, dm0, dm1, dm2 = _merge_bwd(sv["ys"], wups, sv["proj"], dmerged, l)
    dys = []
    for nm, yb, dmb, wd in zip(("w_pool_up", "w_sb_up", "w_gdn_up"), sv["ys"], (dm0, dm1, dm2), (POOL_W, SB_W, GDN_W)):
        dys.append(_mm(nm + "_dx", dmb, gw[nm], m=t, n=wd, k=D, tm=tm, tn=256, tk=512, a_spec=_a_plain(tm, 512),
                       b_spec=_w_cols_t(l, 512, 256, 512), dims=NT, out_shapes=[_sds((t, wd), F32)],
                       out_specs=[_o_plain(tm, 256)])[0])
        g[nm] = _mm(nm + "_dw", yb, dmb, m=wd, n=D, k=t, tm=256, tn=512, tk=tk, a_spec=_a_trans(256, tk),
                    b_spec=_b_plain(tk, 512), dims=TN, out_shapes=[_sds((N_CHIPS, wd, D // N_CHIPS), BF16)],
                    out_specs=[_o_colshard(256, 512, D // N_CHIPS)])[0]
    proj = sv["proj"]
    dp, g["pool_w"], g["pool_scale"] = _pool_bwd(proj, sp["pool_w"][l], sp["pool_scale"][l], dys[0])
    dsq, dsk, dsv = _sb_bwd(proj, sv["kv"], dys[1])
    dc, dab, dz, g["a_log"], g["dt_bias"], g["gdn_norm"] = _gdn_bwd(
        sv["cqkv"], proj, sp["a_log"][l], sp["dt_bias"][l], sp["gdn_norm"][l], sv["sprev"], dys[2])
    dgx, g["conv"] = _conv_bwd(proj, sp["conv"][l], dc)
    dproj = jnp.concatenate(
        [dsq, dsk.astype(BF16), dsv.astype(BF16), dgx, dz, dab.astype(BF16), jnp.zeros((t, AB_W - HD), BF16),
         dp, dg0, dg1, dg2], axis=1)
    du = _mm("proj_dx", dproj, w_al, m=t, n=D, k=N_AL, tm=tm, tn=1024, tk=512, a_spec=_a_plain(tm, 512),
             b_spec=_b_trans(512, 1024), dims=NT, out_shapes=[_sds((t, D), F32)], out_specs=[_o_plain(tm, 1024)])[0]
    g["w_al"] = _mm("proj_dw", sv["u"], dproj, m=D, n=N_AL, k=t, tm=1024, tn=1024, tk=tk, a_spec=_a_trans(1024, tk),
                    b_spec=_b_plain(tk, 1024), dims=TN, out_shapes=[_sds((D, N_AL), BF16)], out_specs=[_o_plain(1024, 1024)])[0]
    dx, g["attn_norm"] = _rms_bwd("rms_attn_bwd", du, sv["x"], sp["attn_norm"][l], dx1)
    return dx, g


def _align_w_in(w):
    n_ab = ORIG_GATE - ORIG_AB
    return jnp.concatenate([w[:, ORIG_SB:ORIG_GATE], jnp.zeros((D, AB_W - n_ab), w.dtype), w[:, :ORIG_SB], w[:, ORIG_GATE:]],
                           axis=1)


def _unalign_w_in(w):
    n_ab = ORIG_GATE - ORIG_AB
    return jnp.concatenate([w[:, OFF_P:OFF_GATE], w[:, :OFF_AB + n_ab], w[:, OFF_GATE:]], axis=1)


W_IN_RUNS = ((0, ORIG_SB, OFF_P), (ORIG_SB, ORIG_GATE, OFF_SB), (ORIG_GATE, N_IN, OFF_GATE))
W_IN_SHARD = N_IN // N_CHIPS


def _w_in_from_shards(gathered, l):
    parts = []
    for lo, hi, al in sorted(W_IN_RUNS, key=lambda r: r[2]):
        if al == OFF_P:
            parts.append(jnp.zeros((D, OFF_P - (OFF_AB + ORIG_GATE - ORIG_AB)), gathered.dtype))
        while lo < hi:
            chip = lo // W_IN_SHARD
            end = min(hi, (chip + 1) * W_IN_SHARD)
            parts.append(gathered[chip, l, :, lo - chip * W_IN_SHARD:end - chip * W_IN_SHARD])
            lo = end
    return jnp.concatenate(parts, axis=1)


def _w_in_to_shards(g_al):
    shards = []
    for chip in range(N_CHIPS):
        a, b = chip * W_IN_SHARD, (chip + 1) * W_IN_SHARD
        parts = [g_al[:, al + max(a, lo) - lo:al + min(b, hi) - lo] for lo, hi, al in W_IN_RUNS if max(a, lo) < min(b, hi)]
        shards.append(jnp.concatenate(parts, axis=1))
    return jnp.stack(shards)


def _row128(v):
    return jnp.pad(v.reshape(1, -1), ((0, 0), (0, HD - v.shape[-1])))


def _local_step(x, target, gw, w_in_al, sp):
    saved = []
    h = x
    for l in range(2):
        h, sv = _layer_fwd(h, l, gw, w_in_al[l], sp)
        saved.append(sv)
    loss, dh, g_final = _loss_head(h, sp["final_norm"], target)
    grads = [None, None]
    for l in (1, 0):
        dh, grads[l] = _layer_bwd(dh, l, gw, w_in_al[l], sp, saved[l])
    return loss, dh, grads, g_final


ANY = pl.BlockSpec(memory_space=pl.ANY)


def _me():
    return lax.axis_index("x"), lax.axis_index("y"), lax.axis_index("c")


def _other_chips(x, y):
    return [(1 - x, y), (x, 1 - y), (1 - x, 1 - y)]


def _half(ref, axis, c, rows):
    half = rows // 2
    idx = [slice(None)] * axis + [pl.ds(pl.multiple_of(c * half, 16), half)]
    return ref.at[tuple(idx)]


def _gather_weights(bufs):
    n = len(bufs)

    def body(*refs):
        out = refs[n:2 * n]
        send, recv = refs[2 * n:]
        x, y, c = _me()
        mine = 2 * x + y
        sibling = (x, y, 1 - c)
        chips = _other_chips(x, y)
        sends = []
        for t in range(n):
            rows = out[t].shape[2]
            for k, (px, py) in enumerate(chips):
                own_half = _half(out[t].at[mine], 1, c, rows)
                cp = pltpu.make_async_remote_copy(
                    src_ref=own_half, dst_ref=own_half,
                    send_sem=send.at[6 * t + k], recv_sem=recv.at[6 * t + k], device_id=(px, py, c), device_id_type=MESH)
                cp.start()
                sends.append(cp)
        for t in range(n):
            rows = out[t].shape[2]
            for k, (px, py) in enumerate(chips):
                landed = _half(out[t].at[2 * px + py], 1, c, rows)
                pltpu.make_async_remote_copy(
                    src_ref=landed, dst_ref=landed, send_sem=send.at[6 * t + k], recv_sem=recv.at[6 * t + k],
                    device_id=(px, py, c), device_id_type=MESH).wait_recv()
                cp = pltpu.make_async_remote_copy(
                    src_ref=landed, dst_ref=landed, send_sem=send.at[6 * t + 3 + k], recv_sem=recv.at[6 * t + 3 + k],
                    device_id=sibling, device_id_type=MESH)
                cp.start()
                sends.append(cp)
        for t in range(n):
            rows = out[t].shape[2]
            for k, (px, py) in enumerate(chips):
                other = _half(out[t].at[2 * px + py], 1, 1 - c, rows)
                pltpu.make_async_remote_copy(
                    src_ref=other, dst_ref=other, send_sem=send.at[6 * t + 3 + k], recv_sem=recv.at[6 * t + 3 + k],
                    device_id=sibling, device_id_type=MESH).wait_recv()
        for cp in sends:
            cp.wait_send()

    return pl.pallas_call(
        body, name="gather_weights", in_specs=[ANY] * n, out_specs=[ANY] * n,
        out_shape=[_sds(s.shape, s.dtype) for s in bufs], input_output_aliases={t: t for t in range(n)},
        scratch_shapes=[pltpu.SemaphoreType.DMA((6 * n,)), pltpu.SemaphoreType.DMA((6 * n,))],
    )(*bufs)


def _rs_pair(grads):
    n = len(grads)

    def body(*refs):
        g, out = refs[:n], refs[n:2 * n]
        send, recv = refs[2 * n:]
        x, y, c = _me()
        copies = []
        for t in range(n):
            cp = pltpu.make_async_remote_copy(
                src_ref=_half(g[t], 1, 1 - c, g[t].shape[1]), dst_ref=out[t], send_sem=send.at[t], recv_sem=recv.at[t],
                device_id=(x, y, 1 - c), device_id_type=MESH)
            cp.start()
            copies.append(cp)
        for cp in copies:
            cp.wait()

    return pl.pallas_call(
        body, name="rs_pair", in_specs=[ANY] * n, out_specs=[ANY] * n,
        out_shape=[_sds((N_CHIPS, s.shape[1] // 2, s.shape[2]), s.dtype) for s in grads],
        scratch_shapes=[pltpu.SemaphoreType.DMA((n,)), pltpu.SemaphoreType.DMA((n,))],
    )(*grads)


def _rs_chips(parts):
    n = len(parts)

    def body(*refs):
        p, out = refs[:n], refs[n:2 * n]
        send, recv = refs[2 * n:]
        x, y, c = _me()
        copies = []
        for t in range(n):
            for k, (px, py) in enumerate(_other_chips(x, y)):
                cp = pltpu.make_async_remote_copy(
                    src_ref=p[t].at[2 * px + py], dst_ref=out[t].at[k], send_sem=send.at[3 * t + k],
                    recv_sem=recv.at[3 * t + k], device_id=(px, py, c), device_id_type=MESH)
                cp.start()
                copies.append(cp)
        for cp in copies:
            cp.wait()

    return pl.pallas_call(
        body, name="rs_chips", in_specs=[ANY] * n, out_specs=[ANY] * n,
        out_shape=[_sds((3, *s.shape[1:]), s.dtype) for s in parts],
        scratch_shapes=[pltpu.SemaphoreType.DMA((3 * n,)), pltpu.SemaphoreType.DMA((3 * n,))],
    )(*parts)


def _pair_exchange(bufs):
    n = len(bufs)

    def body(*refs):
        out = refs[n:2 * n]
        send, recv = refs[2 * n:]
        x, y, c = _me()
        copies = []
        for t in range(n):
            cp = pltpu.make_async_remote_copy(
                src_ref=out[t].at[c], dst_ref=out[t].at[c], send_sem=send.at[t], recv_sem=recv.at[t],
                device_id=(x, y, 1 - c), device_id_type=MESH)
            cp.start()
            copies.append(cp)
        for t, cp in enumerate(copies):
            cp.wait_send()
            pltpu.make_async_remote_copy(
                src_ref=out[t].at[1 - c], dst_ref=out[t].at[1 - c], send_sem=send.at[t], recv_sem=recv.at[t],
                device_id=(x, y, 1 - c), device_id_type=MESH).wait_recv()

    return pl.pallas_call(
        body, name="pair_exchange", in_specs=[ANY] * n, out_specs=[ANY] * n,
        out_shape=[_sds(s.shape, s.dtype) for s in bufs], input_output_aliases={t: t for t in range(n)},
        scratch_shapes=[pltpu.SemaphoreType.DMA((n,)), pltpu.SemaphoreType.DMA((n,))],
    )(*bufs)


def _row_tile(rows, cols, itemsize, budget=2 * 1024 * 1024):
    tr = rows
    while tr * cols * itemsize > budget and tr % 32 == 0:
        tr //= 2
    return tr


def _sum_pair(name, g, got, where):
    nchip, rows, cols = g.shape
    half = rows // 2
    tr = _row_tile(half, cols, 4)
    per = half // tr

    def body(w_ref, g_ref, r_ref, o_ref):
        o_ref[...] = (g_ref[...].astype(F32) + r_ref[...].astype(F32)).astype(o_ref.dtype)

    blk = pl.BlockSpec((None, tr, cols), lambda j, i, w_ref: (j, i, 0))
    return pl.pallas_call(
        body, name=name,
        grid_spec=pltpu.PrefetchScalarGridSpec(
            num_scalar_prefetch=1, grid=(nchip, per),
            in_specs=[pl.BlockSpec((None, tr, cols), lambda j, i, w_ref: (j, w_ref[1] * per + i, 0)), blk], out_specs=blk),
        out_shape=_sds((nchip, half, cols), BF16), compiler_params=_cp("parallel", "parallel"),
    )(where, g, got)


def _sum_chips(name, p, got, where):
    _, rows, cols = p.shape
    tr = _row_tile(rows, cols, 4)

    def body(w_ref, p_ref, r0, r1, r2, o_ref):
        o_ref[...] = ((p_ref[...].astype(F32) + r0[...].astype(F32)) + r1[...].astype(F32)) + r2[...].astype(F32)

    def got_k(k):
        return pl.BlockSpec((None, tr, cols), lambda i, w_ref: (k, i, 0))

    return pl.pallas_call(
        body, name=name,
        grid_spec=pltpu.PrefetchScalarGridSpec(
            num_scalar_prefetch=1, grid=(rows // tr,),
            in_specs=[pl.BlockSpec((None, tr, cols), lambda i, w_ref: (w_ref[0], i, 0)), got_k(0), got_k(1), got_k(2)],
            out_specs=pl.BlockSpec((None, tr, cols), lambda i, w_ref: (w_ref[1], i, 0))),
        out_shape=_sds((2, rows, cols), F32), compiler_params=_cp("parallel"),
    )(where, p, got, got, got)


def _reduce_scatter(grads, where):
    got = _rs_pair(grads)
    parts = [_sum_pair(f"sum_pair_{t}", g, r, where) for t, (g, r) in enumerate(zip(grads, got))]
    got = _rs_chips(parts)
    halves = [_sum_chips(f"sum_chips_{t}", p, r, where) for t, (p, r) in enumerate(zip(parts, got))]
    return [o.reshape(-1, o.shape[-1]) for o in _pair_exchange(halves)]


def _all_reduce_small(name, v):
    rows = v.shape[0]

    def body(v_ref, o_ref, land, send, recv):
        x, y, c = _me()
        mine = 4 * x + 2 * y + c
        copies = []
        for k in range(1, 8):
            kx, ky, kc = k >> 2, (k >> 1) & 1, k & 1
            peer = (x ^ kx, y ^ ky, c ^ kc)
            cp = pltpu.make_async_remote_copy(
                src_ref=v_ref, dst_ref=land.at[mine], send_sem=send.at[k - 1], recv_sem=recv.at[k - 1],
                device_id=peer, device_id_type=MESH)
            cp.start()
            copies.append(cp)
        land[mine] = v_ref[...]
        for k in range(1, 8):
            kx, ky, kc = k >> 2, (k >> 1) & 1, k & 1
            src = 4 * (x ^ kx) + 2 * (y ^ ky) + (c ^ kc)
            pltpu.make_async_remote_copy(
                src_ref=v_ref, dst_ref=land.at[src], send_sem=send.at[k - 1], recv_sem=recv.at[k - 1],
                device_id=(x ^ kx, y ^ ky, c ^ kc), device_id_type=MESH).wait_recv()
        acc = land[0]
        for d in range(1, 8):
            acc = acc + land[d]
        o_ref[...] = acc
        for cp in copies:
            cp.wait_send()

    vm = pl.BlockSpec(memory_space=pltpu.VMEM)
    return pl.pallas_call(
        body, name=name, in_specs=[vm], out_specs=vm, out_shape=_sds((rows, 128), F32),
        scratch_shapes=[pltpu.VMEM((8, rows, 128), F32), pltpu.SemaphoreType.DMA((7,)), pltpu.SemaphoreType.DMA((7,))],
    )(v)


def _adamw(name, w, g, m, v):
    rows, cols = w.shape
    tr = _row_tile(rows, cols, 4, budget=1024 * 1024)
    c1 = 1.0 / (1.0 - ADAM_B1 ** ADAM_STEP)
    c2 = 1.0 / (1.0 - ADAM_B2 ** ADAM_STEP)

    def body(w_ref, g_ref, m_ref, v_ref, d_ref, nm_ref, nv_ref):
        gv = g_ref[...]
        nm = ADAM_B1 * m_ref[...] + (1.0 - ADAM_B1) * gv
        nv = ADAM_B2 * v_ref[...] + (1.0 - ADAM_B2) * (gv * gv)
        d_ref[...] = -ADAM_LR * ((nm * c1) / (jnp.sqrt(nv * c2) + ADAM_EPS) + ADAM_WD * w_ref[...])
        nm_ref[...] = nm
        nv_ref[...] = nv

    blk = pl.BlockSpec((tr, cols), lambda i: (i, 0))
    return pl.pallas_call(
        body, name=name, grid=(rows // tr,), in_specs=[blk] * 4, out_specs=[blk] * 3,
        out_shape=[_sds((rows, cols), F32)] * 3, compiler_params=_cp("parallel"),
    )(w, g, m, v)


def _adamw_layers(name, w, g0, g1, m, v):
    rows, cols = g0.shape
    tr = _row_tile(rows, cols, 4, budget=1024 * 1024)
    per = rows // tr
    c1 = 1.0 / (1.0 - ADAM_B1 ** ADAM_STEP)
    c2 = 1.0 / (1.0 - ADAM_B2 ** ADAM_STEP)

    def body(w_ref, g0_ref, g1_ref, m_ref, v_ref, g_ref, d_ref, nm_ref, nv_ref):
        gv = jnp.where(pl.program_id(0) == 0, g0_ref[...], g1_ref[...])
        nm = ADAM_B1 * m_ref[...] + (1.0 - ADAM_B1) * gv
        nv = ADAM_B2 * v_ref[...] + (1.0 - ADAM_B2) * (gv * gv)
        g_ref[...] = gv
        d_ref[...] = -ADAM_LR * ((nm * c1) / (jnp.sqrt(nv * c2) + ADAM_EPS) + ADAM_WD * w_ref[...])
        nm_ref[...] = nm
        nv_ref[...] = nv

    both = pl.BlockSpec((tr, cols), lambda l, i: (l * per + i, 0))
    first = pl.BlockSpec((tr, cols), lambda l, i: (i * (1 - l) + (per - 1) * l, 0))
    second = pl.BlockSpec((tr, cols), lambda l, i: (i * l, 0))
    return pl.pallas_call(
        body, name=name, grid=(2, per), in_specs=[both, first, second, both, both], out_specs=[both] * 4,
        out_shape=[_sds((2 * rows, cols), F32)] * 4, compiler_params=_cp("arbitrary", "arbitrary"),
    )(w, g0, g1, m, v)


def _to_bf16_slot(name, w, where):
    rows, cols = w.shape
    tr = _row_tile(rows, cols, 4)

    def body(w_ref, x_ref, o_ref):
        o_ref[...] = x_ref[...].astype(BF16)

    return pl.pallas_call(
        body, name=name,
        grid_spec=pltpu.PrefetchScalarGridSpec(
            num_scalar_prefetch=1, grid=(rows // tr,), in_specs=[pl.BlockSpec((tr, cols), lambda i, w_ref: (i, 0))],
            out_specs=pl.BlockSpec((None, tr, cols), lambda i, w_ref: (w_ref[0], i, 0))),
        out_shape=_sds((N_CHIPS, rows, cols), BF16), compiler_params=_cp("parallel"))(where, w)


BIG = ("w_in", "w_pool_up", "w_sb_up", "w_gdn_up", "w_out", "w_ff1", "w_ff2")
SMALL = (("attn_norm", (D,)), ("pool_w", (4, 128, 128)), ("pool_scale", (POOL_W,)), ("gdn_a_log", (HEADS,)),
         ("gdn_dt_bias", (HEADS,)), ("gdn_norm", (HD,)), ("mlp_norm", (D,)))


def _rows128(a):
    flat = a.reshape(-1)
    pad = (-flat.shape[0]) % 128
    return jnp.pad(flat, (0, pad)).reshape(-1, 128)


def _pack(parts):
    packed = jnp.concatenate([_rows128(p) for p in parts], axis=0)
    return jnp.pad(packed, ((0, (-packed.shape[0]) % 8), (0, 0)))


def _unpack(packed, shapes):
    out, r = [], 0
    for shp in shapes:
        size = 1
        for s in shp:
            size *= s
        nr = -(-size // 128)
        out.append(packed[r:r + nr].reshape(-1)[:size].reshape(shp))
        r += nr
    return out


def kernel(x, attn_norm, w_in, pool_w, pool_scale, gdn_conv, gdn_a_log, gdn_dt_bias, gdn_norm, w_pool_up, w_sb_up, w_gdn_up, w_out, mlp_norm, w_ff1, w_ff2, final_norm, loss_target, m_attn_norm, m_w_in, m_pool_w, m_pool_scale, m_gdn_conv, m_gdn_a_log, m_gdn_dt_bias, m_gdn_norm, m_w_pool_up, m_w_sb_up, m_w_gdn_up, m_w_out, m_mlp_norm, m_w_ff1, m_w_ff2, m_final_norm, v_attn_norm, v_w_in, v_pool_w, v_pool_scale, v_gdn_conv, v_gdn_a_log, v_gdn_dt_bias, v_gdn_norm, v_w_pool_up, v_w_sb_up, v_w_gdn_up, v_w_out, v_mlp_norm, v_w_ff1, v_w_ff2, v_final_norm):
    weights = dict(attn_norm=attn_norm, w_in=w_in, pool_w=pool_w, pool_scale=pool_scale, gdn_conv=gdn_conv,
                   gdn_a_log=gdn_a_log, gdn_dt_bias=gdn_dt_bias, gdn_norm=gdn_norm, w_pool_up=w_pool_up, w_sb_up=w_sb_up,
                   w_gdn_up=w_gdn_up, w_out=w_out, mlp_norm=mlp_norm, w_ff1=w_ff1, w_ff2=w_ff2, final_norm=final_norm)
    mom1 = dict(attn_norm=m_attn_norm, w_in=m_w_in, pool_w=m_pool_w, pool_scale=m_pool_scale, gdn_conv=m_gdn_conv,
                gdn_a_log=m_gdn_a_log, gdn_dt_bias=m_gdn_dt_bias, gdn_norm=m_gdn_norm, w_pool_up=m_w_pool_up,
                w_sb_up=m_w_sb_up, w_gdn_up=m_w_gdn_up, w_out=m_w_out, mlp_norm=m_mlp_norm, w_ff1=m_w_ff1, w_ff2=m_w_ff2,
                final_norm=m_final_norm)
    mom2 = dict(attn_norm=v_attn_norm, w_in=v_w_in, pool_w=v_pool_w, pool_scale=v_pool_scale, gdn_conv=v_gdn_conv,
                gdn_a_log=v_gdn_a_log, gdn_dt_bias=v_gdn_dt_bias, gdn_norm=v_gdn_norm, w_pool_up=v_w_pool_up,
                w_sb_up=v_w_sb_up, w_gdn_up=v_w_gdn_up, w_out=v_w_out, mlp_norm=v_mlp_norm, w_ff1=v_w_ff1, w_ff2=v_w_ff2,
                final_norm=v_final_norm)
    xi, yi, ci = lax.axis_index("x"), lax.axis_index("y"), lax.axis_index("c")
    chip = 2 * xi + yi
    where = jnp.stack([chip, ci]).astype(jnp.int32)

    bufs = [_to_bf16_slot("cast_" + nm, weights[nm].reshape(-1, weights[nm].shape[-1]), where).reshape(N_CHIPS, *weights[nm].shape)
            for nm in BIG]
    gw = dict(zip(BIG, _gather_weights(bufs)))
    conv_cols = gdn_conv.shape[-1]
    conv_place = lax.dynamic_update_slice(jnp.zeros((2, GDN_CONV, N_CHIPS * conv_cols), F32),
                                          jnp.where(ci == 0, gdn_conv, 0.0), (0, 0, chip * conv_cols))
    conv_full = _all_reduce_small("gather_conv", _rows128(conv_place)).reshape(2, GDN_CONV, N_CHIPS * conv_cols)
    w_in_al = [_w_in_from_shards(gw["w_in"], l) for l in range(2)]
    sp = dict(attn_norm=attn_norm.reshape(2, 1, D), pool_w=pool_w, pool_scale=pool_scale.reshape(2, 1, POOL_W),
              conv=conv_full, a_log=jnp.stack([_row128(gdn_a_log[l]) for l in range(2)]),
              dt_bias=jnp.stack([_row128(gdn_dt_bias[l]) for l in range(2)]), gdn_norm=gdn_norm.reshape(2, 1, HD),
              mlp_norm=mlp_norm.reshape(2, 1, D), final_norm=final_norm.reshape(1, D))

    loss, grad_x, grads, g_final = _local_step(x[0], loss_target[0], gw, w_in_al, sp)
    loss = lax.psum(loss[0, 0], ("x", "y", "c"))

    big_grads = {nm: [] for nm in BIG}
    for l in range(2):
        g = grads[l]
        per_layer = [_w_in_to_shards(g["w_al"])] + [g[nm] for nm in BIG[1:]]
        for nm, red in zip(BIG, _reduce_scatter(per_layer, where)):
            big_grads[nm].append(red)
    small_parts, small_shapes = [], []
    for l in range(2):
        g = grads[l]
        for nm, shp in SMALL:
            key = {"gdn_a_log": "a_log", "gdn_dt_bias": "dt_bias"}.get(nm, nm)
            val = g[key]
            small_parts.append(val[0, :HEADS] if nm in ("gdn_a_log", "gdn_dt_bias") else val)
            small_shapes.append(shp)
        small_parts.append(g["conv"])
        small_shapes.append((GDN_CONV, N_CHIPS * conv_cols))
    small_parts.append(g_final)
    small_shapes.append((D,))
    reduced = _unpack(_all_reduce_small("reduce_small", _pack(small_parts)), small_shapes)
    per = len(SMALL) + 1
    grad = {}
    for i, (nm, _) in enumerate(SMALL):
        grad[nm] = jnp.stack([reduced[i], reduced[per + i]])
    conv_g = jnp.stack([reduced[per - 1], reduced[2 * per - 1]])
    grad["gdn_conv"] = lax.dynamic_slice(conv_g, (0, 0, chip * conv_cols), (2, GDN_CONV, conv_cols))
    grad["final_norm"] = reduced[-1]

    delta, new_m, new_v = {}, {}, {}
    for nm in BIG:
        shp = weights[nm].shape
        flat = lambda a: a.reshape(-1, shp[-1])
        outs = _adamw_layers("adamw_" + nm, flat(weights[nm]), *big_grads[nm], flat(mom1[nm]), flat(mom2[nm]))
        grad[nm], delta[nm], new_m[nm], new_v[nm] = [o.reshape(shp) for o in outs]
    small_names = [nm for nm, _ in SMALL] + ["gdn_conv", "final_norm"]
    packs = [_pack([src[nm] for nm in small_names]) for src in (weights, grad, mom1, mom2)]
    outs = _adamw("adamw_small", *packs)
    shapes = [weights[nm].shape for nm in small_names]
    for dst, packed in zip((delta, new_m, new_v), outs):
        for nm, val in zip(small_names, _unpack(packed, shapes)):
            dst[nm] = val

    order = ("attn_norm", "w_in", "pool_w", "pool_scale", "gdn_conv", "gdn_a_log", "gdn_dt_bias", "gdn_norm", "w_pool_up",
             "w_sb_up", "w_gdn_up", "w_out", "mlp_norm", "w_ff1", "w_ff2", "final_norm")
    return (loss, grad_x[None], *[grad[n] for n in order], *[delta[n] for n in order], *[new_m[n] for n in order],
            *[new_v[n] for n in order])
```

```python
import functools

import jax
import jax.numpy as jnp
from jax import lax
from jax.experimental import pallas as pl
from jax.experimental.pallas import tpu as pltpu
from jax.experimental.pallas import tpu_sc as plsc

F32, BF16 = jnp.float32, jnp.bfloat16
HIGH = lax.Precision.HIGH
MESH = pl.DeviceIdType.MESH

D = 2048
EPS = 1e-6
POOL_WINDOWS = (2, 4, 8, 16)
POOL_W, SB_W, GDN_W = 512, 768, 768
HEADS, HD = 6, 128
SB_BLOCK = 128
GDN_CHUNK = 64
D_FF = 4 * D
N_IN = 12044
N_CHIPS = 4
OFF_SB, OFF_GQKV, OFF_Z, OFF_AB, OFF_P, OFF_GATE = 0, 2304, 4608, 5376, 5632, 6144
AB_W = 256
ORIG_SB, ORIG_AB, ORIG_GATE = 512, 5888, 5900
N_AL = 12288
VMEM_LIMIT = 48 * 1024 * 1024

ADAM_LR, ADAM_B1, ADAM_B2, ADAM_EPS, ADAM_WD, ADAM_STEP = 0.001, 0.9, 0.999, 1e-08, 0.01, 10

NT = (((1,), (1,)), ((), ()))
TN = (((0,), (0,)), ((), ()))


def _cp(*sem):
    return pltpu.CompilerParams(dimension_semantics=sem, vmem_limit_bytes=VMEM_LIMIT)


def _dot(a, b, dims=None, precision=None):
    if dims is None:
        dims = (((a.ndim - 1,), (0,)), ((), ()))
    return lax.dot_general(a, b, dims, precision=precision, preferred_element_type=F32)


def _hdot(a, b, dims=None):
    return _dot(a, b, dims, precision=HIGH)


def _bdot(a, b, dims=None):
    return _dot(a.astype(BF16), b.astype(BF16), dims)


def _mm(name, a, b, *, m, n, k, tm, tn, tk, a_spec, b_spec, dims, out_shapes, out_specs,
        extras=(), extra_specs=(), epilogue=None):
    nk = k // tk
    ne, no = len(extras), len(out_shapes)

    def body(*refs):
        a_ref, b_ref = refs[0], refs[1]
        ex = refs[2:2 + ne]
        outs = refs[2 + ne:2 + ne + no]
        acc = refs[-1]
        kk = pl.program_id(2)

        @pl.when(kk == 0)
        def _():
            acc[...] = jnp.zeros_like(acc)

        acc[...] += _dot(a_ref[...].astype(BF16), b_ref[...].astype(BF16), dims)

        @pl.when(kk == nk - 1)
        def _():
            r = acc[...]
            res = epilogue(r, *[e[...] for e in ex]) if epilogue is not None else (r,)
            for o, v in zip(outs, res):
                o[...] = v.astype(o.dtype)

    return pl.pallas_call(
        body, name=name, grid=(m // tm, n // tn, nk),
        in_specs=[a_spec, b_spec, *extra_specs], out_specs=out_specs, out_shape=out_shapes,
        scratch_shapes=[pltpu.VMEM((tm, tn), F32)],
        compiler_params=_cp("parallel", "parallel", "arbitrary"),
    )(a, b, *extras)


def _a_plain(tm, tk):
    return pl.BlockSpec((tm, tk), lambda i, j, kk: (i, kk))


def _a_trans(tm, tk):
    return pl.BlockSpec((tk, tm), lambda i, j, kk: (kk, i))


def _b_plain(tk, tn):
    return pl.BlockSpec((tk, tn), lambda i, j, kk: (kk, j))


def _b_trans(tk, tn):
    return pl.BlockSpec((tn, tk), lambda i, j, kk: (j, kk))


def _o_plain(tm, tn):
    return pl.BlockSpec((tm, tn), lambda i, j, kk: (i, j))


def _o_colshard(tm, tn, ns_cols):
    per = ns_cols // tn
    return pl.BlockSpec((None, tm, tn), lambda i, j, kk: (j // per, i, j % per))


def _w_cols(tk, tn, ns):
    per = ns // tn
    return pl.BlockSpec((None, tk, tn), lambda i, j, kk: (j // per, kk, j % per))


def _w_cols_t(tk, tn, ns):
    per = ns // tk
    return pl.BlockSpec((None, tn, tk), lambda i, j, kk: (kk // per, j, kk % per))


def _w_rows(tk, tn, ks):
    per = ks // tk
    return pl.BlockSpec((None, tk, tn), lambda i, j, kk: (kk // per, kk % per, j))


def _w_rows_t(tk, tn, ks):
    per = ks // tn
    return pl.BlockSpec((None, tn, tk), lambda i, j, kk: (j // per, j % per, kk))


def _sds(shape, dtype):
    return jax.ShapeDtypeStruct(shape, dtype)


def _rms_fwd(name, x, gain):
    t = x.shape[0]
    tt = min(256, t)

    def body(x_ref, g_ref, u_ref):
        xv = x_ref[...]
        r = lax.rsqrt(jnp.mean(xv * xv, axis=-1, keepdims=True) + EPS)
        u_ref[...] = (xv * r * g_ref[...]).astype(u_ref.dtype)

    return pl.pallas_call(
        body, name=name, grid=(t // tt,),
        in_specs=[pl.BlockSpec((tt, D), lambda i: (i, 0)), pl.BlockSpec((1, D), lambda i: (0, 0))],
        out_specs=pl.BlockSpec((tt, D), lambda i: (i, 0)), out_shape=_sds((t, D), BF16),
        compiler_params=_cp("parallel"),
    )(x, gain)


def _rms_bwd(name, du, x, gain, dres):
    t = x.shape[0]
    tt = min(256, t)

    def body(du_ref, x_ref, g_ref, dres_ref, dx_ref, dg_ref):
        @pl.when(pl.program_id(0) == 0)
        def _():
            dg_ref[...] = jnp.zeros_like(dg_ref)

        xv, duv = x_ref[...], du_ref[...]
        r = lax.rsqrt(jnp.mean(xv * xv, axis=-1, keepdims=True) + EPS)
        nx = xv * r
        dn = duv * g_ref[...]
        dg_ref[...] += jnp.sum(duv * nx, axis=0, keepdims=True)
        dx_ref[...] = dres_ref[...] + r * (dn - nx * jnp.mean(dn * nx, axis=-1, keepdims=True))

    row = pl.BlockSpec((tt, D), lambda i: (i, 0))
    vec = pl.BlockSpec((1, D), lambda i: (0, 0))
    return pl.pallas_call(
        body, name=name, grid=(t // tt,), in_specs=[row, row, vec, row], out_specs=[row, vec],
        out_shape=[_sds((t, D), F32), _sds((1, D), F32)], compiler_params=_cp("arbitrary"),
    )(du, x, gain, dres)


def _loss_head(x, gain, target):
    t = x.shape[0]
    tt = min(256, t)

    def body(x_ref, g_ref, t_ref, loss_ref, dx_ref, dg_ref):
        @pl.when(pl.program_id(0) == 0)
        def _():
            dg_ref[...] = jnp.zeros_like(dg_ref)
            loss_ref[...] = jnp.zeros_like(loss_ref)

        xv = x_ref[...]
        r = lax.rsqrt(jnp.mean(xv * xv, axis=-1, keepdims=True) + EPS)
        nx = xv * r
        err = nx * g_ref[...] - t_ref[...]
        loss_ref[...] += 0.5 * jnp.sum(jnp.mean(err * err, axis=-1, keepdims=True), axis=0, keepdims=True)
        dy = err * (1.0 / D)
        dn = dy * g_ref[...]
        dg_ref[...] += jnp.sum(dy * nx, axis=0, keepdims=True)
        dx_ref[...] = r * (dn - nx * jnp.mean(dn * nx, axis=-1, keepdims=True))

    row = pl.BlockSpec((tt, D), lambda i: (i, 0))
    vec = pl.BlockSpec((1, D), lambda i: (0, 0))
    one = pl.BlockSpec((1, 1), lambda i: (0, 0))
    return pl.pallas_call(
        body, name="loss_head", grid=(t // tt,), in_specs=[row, vec, row], out_specs=[one, row, vec],
        out_shape=[_sds((1, 1), F32), _sds((t, D), F32), _sds((1, D), F32)], compiler_params=_cp("arbitrary"),
    )(x, gain, target)


def _shift_down(v, s, t_idx):
    return jnp.where(t_idx >= s, pltpu.roll(v, s, 0), 0.0)


def _shift_up(v, s, t_idx, t):
    return jnp.where(t_idx < t - s, pltpu.roll(v, t - s, 0), 0.0)


def _pool_d(p, g, t_idx):
    s = p
    for step in range(g + 1):
        s = s + _shift_down(s, 1 << step, t_idx)
    cnt = jnp.minimum(t_idx + 1, POOL_WINDOWS[g]).astype(F32)
    return s / cnt - p, cnt


def _pool_fwd(proj, pool_w, pool_scale):
    t = proj.shape[0]
    g128 = POOL_W // len(POOL_WINDOWS)

    def body(p_ref, w_ref, s_ref, y_ref):
        t_idx = lax.broadcasted_iota(jnp.int32, (t, g128), 0)
        for g in range(len(POOL_WINDOWS)):
            sl = slice(g * g128, (g + 1) * g128)
            d, _ = _pool_d(p_ref[:, sl], g, t_idx)
            y_ref[:, sl] = (_bdot(d, w_ref[g]) * s_ref[:, sl]).astype(y_ref.dtype)

    return pl.pallas_call(
        body, name="pool_fwd", grid=(1,),
        in_specs=[pl.BlockSpec((t, POOL_W), lambda i: (0, OFF_P // POOL_W)),
                  pl.BlockSpec((4, g128, g128), lambda i: (0, 0, 0)), pl.BlockSpec((1, POOL_W), lambda i: (0, 0))],
        out_specs=pl.BlockSpec((t, POOL_W), lambda i: (0, 0)), out_shape=_sds((t, POOL_W), BF16),
        compiler_params=_cp("arbitrary"),
    )(proj, pool_w, pool_scale)


def _pool_bwd(proj, pool_w, pool_scale, dy):
    t = proj.shape[0]
    g128 = POOL_W // len(POOL_WINDOWS)

    def body(p_ref, w_ref, s_ref, dy_ref, dp_ref, dw_ref, ds_ref):
        t_idx = lax.broadcasted_iota(jnp.int32, (t, g128), 0)
        for g in range(len(POOL_WINDOWS)):
            sl = slice(g * g128, (g + 1) * g128)
            d, cnt = _pool_d(p_ref[:, sl], g, t_idx)
            dyv = dy_ref[:, sl].astype(F32)
            ds_ref[:, sl] = jnp.sum(dyv * _bdot(d, w_ref[g]), axis=0, keepdims=True)
            dys = dyv * s_ref[:, sl]
            dw_ref[g] = _bdot(d, dys, TN)
            dd = _bdot(dys, w_ref[g], NT)
            s = dd / cnt
            for step in range(g + 1):
                s = s + _shift_up(s, 1 << step, t_idx, t)
            dp_ref[:, sl] = (s - dd).astype(dp_ref.dtype)

    return pl.pallas_call(
        body, name="pool_bwd", grid=(1,),
        in_specs=[pl.BlockSpec((t, POOL_W), lambda i: (0, OFF_P // POOL_W)),
                  pl.BlockSpec((4, g128, g128), lambda i: (0, 0, 0)), pl.BlockSpec((1, POOL_W), lambda i: (0, 0)),
                  pl.BlockSpec((t, POOL_W), lambda i: (0, 0))],
        out_specs=[pl.BlockSpec((t, POOL_W), lambda i: (0, 0)), pl.BlockSpec((4, g128, g128), lambda i: (0, 0, 0)),
                   pl.BlockSpec((1, POOL_W), lambda i: (0, 0))],
        out_shape=[_sds((t, POOL_W), BF16), _sds((4, g128, g128), F32), _sds((1, POOL_W), F32)],
        compiler_params=_cp("arbitrary"),
    )(proj, pool_w, pool_scale, dy)


SB_GROUP = 3
SB_GW = SB_GROUP * HD


def _sb_cast_kv(proj):
    t = proj.shape[0]
    tt = min(512, t)

    def body(x_ref, o_ref):
        o_ref[...] = x_ref[...].astype(BF16)

    return pl.pallas_call(
        body, name="sb_cast_kv", grid=(t // tt, 2),
        in_specs=[pl.BlockSpec((tt, SB_W), lambda i, j: (i, OFF_SB // SB_W + 1 + j))],
        out_specs=pl.BlockSpec((tt, SB_W), lambda i, j: (i, j)), out_shape=_sds((t, 2 * SB_W), BF16),
        compiler_params=_cp("parallel", "parallel"),
    )(proj)


def _sb_specs(t):
    q_spec = pl.BlockSpec((SB_BLOCK, SB_GW), lambda g, i: (i, OFF_SB // SB_GW + g))
    k_spec = pl.BlockSpec((t, SB_GW), lambda g, i: (0, g))
    v_spec = pl.BlockSpec((t, SB_GW), lambda g, i: (0, SB_W // SB_GW + g))
    return q_spec, k_spec, v_spec


def _head(ref, h, rows=None):
    cols = slice(h * HD, (h + 1) * HD)
    return ref[:, cols] if rows is None else ref[rows, cols]


SB_KEYS = 512


def _sub(v, b):
    return v[:, b * SB_BLOCK:(b + 1) * SB_BLOCK]


def _sb_keep(kc, limit):
    row = lax.broadcasted_iota(jnp.int32, (SB_BLOCK, kc), 0)
    col = lax.broadcasted_iota(jnp.int32, (SB_BLOCK, kc), 1)
    return col < row + limit


def _sb_chunk(q, keys, run, later, limit):
    kc = keys.shape[0]
    z = _dot(q, keys, NT)
    lsz = jax.nn.log_sigmoid(z)
    ls = lsz - z
    if limit is not None:
        keep = _sb_keep(kc, limit)
        ls = jnp.where(keep, ls, 0.0)
    parts = [None] * (kc // SB_BLOCK)
    for b in reversed(range(kc // SB_BLOCK)):
        parts[b] = _hdot(_sub(ls, b), later) + run
        run = run + jnp.sum(_sub(ls, b), axis=1, keepdims=True)
    a = jnp.exp(lsz + jnp.concatenate(parts, axis=1))
    if limit is not None:
        a = jnp.where(keep, a, 0.0)
    return z, a, run


def _sb_fwd(proj, kv):
    t = proj.shape[0]
    kc = min(SB_KEYS, t)
    scale = HD ** -0.5

    def body(q_ref, k_ref, v_ref, o_ref):
        i = pl.program_id(1)
        top = (i * SB_BLOCK) // kc
        qs = [(_head(q_ref, h) * scale).astype(BF16) for h in range(SB_GROUP)]
        row = lax.broadcasted_iota(jnp.int32, (SB_BLOCK, SB_BLOCK), 0)
        col = lax.broadcasted_iota(jnp.int32, (SB_BLOCK, SB_BLOCK), 1)
        later = (row > col).astype(F32)

        def chunk(jc, carry, masked):
            rows = pl.ds(pl.multiple_of(jc * kc, kc), kc)
            limit = i * SB_BLOCK - jc * kc if masked else None
            out = []
            for h in range(SB_GROUP):
                acc, run = carry[h]
                _, a, run = _sb_chunk(qs[h], _head(k_ref, h, rows), run, later, limit)
                out.append((acc + _dot(a.astype(BF16), _head(v_ref, h, rows)), run))
            return tuple(out)

        zero = tuple((jnp.zeros((SB_BLOCK, HD), F32), jnp.zeros((SB_BLOCK, 1), F32)) for _ in range(SB_GROUP))
        carry = chunk(top, zero, True)
        carry = lax.fori_loop(0, top, lambda jj, c: chunk(top - 1 - jj, c, False), carry)
        for h in range(SB_GROUP):
            o_ref[:, h * HD:(h + 1) * HD] = carry[h][0].astype(o_ref.dtype)

    return pl.pallas_call(
        body, name="sb_fwd", grid=(HEADS // SB_GROUP, t // SB_BLOCK), in_specs=list(_sb_specs(t)),
        out_specs=pl.BlockSpec((SB_BLOCK, SB_GW), lambda g, i: (i, g)), out_shape=_sds((t, SB_W), BF16),
        compiler_params=_cp("parallel", "arbitrary"),
    )(proj, kv, kv)


def _sb_bwd(proj, kv, dy):
    t = proj.shape[0]
    nq = t // SB_BLOCK
    kc = min(SB_KEYS, t)
    scale = HD ** -0.5

    def body(q_ref, k_ref, v_ref, do_ref, dq_ref, dk_ref, dv_ref, z_scr, e_scr):
        i = pl.program_id(1)
        top = (i * SB_BLOCK) // kc

        @pl.when(i == 0)
        def _():
            dk_ref[...] = jnp.zeros_like(dk_ref)
            dv_ref[...] = jnp.zeros_like(dv_ref)

        qs = [(_head(q_ref, h) * scale).astype(BF16) for h in range(SB_GROUP)]
        dos = [_head(do_ref, h).astype(BF16) for h in range(SB_GROUP)]
        row = lax.broadcasted_iota(jnp.int32, (SB_BLOCK, SB_BLOCK), 0)
        col = lax.broadcasted_iota(jnp.int32, (SB_BLOCK, SB_BLOCK), 1)
        later = (row > col).astype(F32)
        earlier = (row < col).astype(F32)

        def down(jc, runs, masked):
            rows = pl.ds(pl.multiple_of(jc * kc, kc), kc)
            limit = i * SB_BLOCK - jc * kc if masked else None
            out = []
            for h in range(SB_GROUP):
                z, a, run = _sb_chunk(qs[h], _head(k_ref, h, rows), runs[h], later, limit)
                z_scr[h, jc] = z
                e_scr[h, jc] = a * _dot(dos[h], _head(v_ref, h, rows), NT)
                dv_ref[rows, h * HD:(h + 1) * HD] += _dot(a.astype(BF16), dos[h], TN)
                out.append(run)
            return tuple(out)

        zero = tuple(jnp.zeros((SB_BLOCK, 1), F32) for _ in range(SB_GROUP))
        runs = down(top, zero, True)
        lax.fori_loop(0, top, lambda jj, r: down(top - 1 - jj, r, False), runs)

        def up(jc, carry, masked):
            rows = pl.ds(pl.multiple_of(jc * kc, kc), kc)
            out = []
            for h in range(SB_GROUP):
                dq, run = carry[h]
                z, e = z_scr[h, jc], e_scr[h, jc]
                parts = []
                for b in range(kc // SB_BLOCK):
                    parts.append(_hdot(_sub(e, b), earlier) + run)
                    run = run + jnp.sum(_sub(e, b), axis=1, keepdims=True)
                sz = jax.nn.sigmoid(z)
                dz = e * (1.0 - sz) - jnp.concatenate(parts, axis=1) * sz
                if masked:
                    dz = jnp.where(_sb_keep(kc, i * SB_BLOCK - jc * kc), dz, 0.0)
                dz = dz.astype(BF16)
                dk_ref[rows, h * HD:(h + 1) * HD] += _dot(dz, qs[h], TN)
                out.append((dq + _dot(dz, _head(k_ref, h, rows)), run))
            return tuple(out)

        zero = tuple((jnp.zeros((SB_BLOCK, HD), F32), jnp.zeros((SB_BLOCK, 1), F32)) for _ in range(SB_GROUP))
        carry = lax.fori_loop(0, top, lambda jc, c: up(jc, c, False), zero)
        carry = up(top, carry, True)
        for h in range(SB_GROUP):
            dq_ref[:, h * HD:(h + 1) * HD] = (carry[h][0] * scale).astype(dq_ref.dtype)

    blk = pl.BlockSpec((SB_BLOCK, SB_GW), lambda g, i: (i, g))
    seq = pl.BlockSpec((t, SB_GW), lambda g, i: (0, g))
    scratch = pltpu.VMEM((SB_GROUP, t // kc, SB_BLOCK, kc), F32)
    return pl.pallas_call(
        body, name="sb_bwd", grid=(HEADS // SB_GROUP, nq), in_specs=[*_sb_specs(t), blk], out_specs=[blk, seq, seq],
        out_shape=[_sds((t, SB_W), BF16), _sds((t, SB_W), F32), _sds((t, SB_W), F32)],
        scratch_shapes=[scratch, scratch], compiler_params=_cp("parallel", "arbitrary"),
    )(proj, kv, kv, dy)


CONV_TILE = 256
GDN_CONV = 4


def _conv_pre(x, w_ref, t_idx):
    pre = w_ref[GDN_CONV - 1:GDN_CONV, :] * x
    for s in range(1, GDN_CONV):
        pre = pre + w_ref[GDN_CONV - 1 - s:GDN_CONV - s, :] * _shift_down(x, s, t_idx)
    return pre


def _conv_fwd(proj, conv_w):
    t = proj.shape[0]
    width = conv_w.shape[1]

    def body(x_ref, w_ref, y_ref):
        t_idx = lax.broadcasted_iota(jnp.int32, (t, CONV_TILE), 0)
        pre = _conv_pre(x_ref[...], w_ref, t_idx)
        y_ref[...] = pre * jax.nn.sigmoid(pre)

    return pl.pallas_call(
        body, name="conv_fwd", grid=(width // CONV_TILE,),
        in_specs=[pl.BlockSpec((t, CONV_TILE), lambda c: (0, OFF_GQKV // CONV_TILE + c)),
                  pl.BlockSpec((GDN_CONV, CONV_TILE), lambda c: (0, c))],
        out_specs=pl.BlockSpec((t, CONV_TILE), lambda c: (0, c)), out_shape=_sds((t, width), F32),
        compiler_params=_cp("parallel"),
    )(proj, conv_w)


def _conv_bwd(proj, conv_w, dc):
    t = proj.shape[0]
    width = dc.shape[1]
    per = width // CONV_TILE
    part = 0

    def body(x_ref, w_ref, dc_ref, dx_ref, dw_ref):
        t_idx = lax.broadcasted_iota(jnp.int32, (t, CONV_TILE), 0)
        x = x_ref[...]
        pre = _conv_pre(x, w_ref, t_idx)
        sg = jax.nn.sigmoid(pre)
        dpre = dc_ref[...] * (sg * (1.0 + pre * (1.0 - sg)))
        dx = w_ref[GDN_CONV - 1:GDN_CONV, :] * dpre
        dw_ref[GDN_CONV - 1:GDN_CONV, :] = jnp.sum(dpre * x, axis=0, keepdims=True)
        for s in range(1, GDN_CONV):
            dx = dx + w_ref[GDN_CONV - 1 - s:GDN_CONV - s, :] * _shift_up(dpre, s, t_idx, t)
            dw_ref[GDN_CONV - 1 - s:GDN_CONV - s, :] = jnp.sum(dpre * _shift_down(x, s, t_idx), axis=0, keepdims=True)
        dx_ref[...] = dx.astype(dx_ref.dtype)

    return pl.pallas_call(
        body, name="conv_bwd", grid=(per,),
        in_specs=[pl.BlockSpec((t, CONV_TILE), lambda c: (0, OFF_GQKV // CONV_TILE + part * per + c)),
                  pl.BlockSpec((GDN_CONV, CONV_TILE), lambda c: (0, part * per + c)),
                  pl.BlockSpec((t, CONV_TILE), lambda c: (0, c))],
        out_specs=[pl.BlockSpec((t, CONV_TILE), lambda c: (0, c)), pl.BlockSpec((GDN_CONV, CONV_TILE), lambda c: (0, c))],
        out_shape=[_sds((t, width), BF16), _sds((GDN_CONV, width), F32)],
        compiler_params=_cp("parallel"),
    )(proj, conv_w, dc)


def _gdn_prep(cq, ck, cv, ab, alog_row, dtb_row, h):
    c = GDN_CHUNK
    row = lax.broadcasted_iota(jnp.int32, (c, c), 0)
    col = lax.broadcasted_iota(jnp.int32, (c, c), 1)
    incl, strict, eye = row >= col, row > col, row == col
    a_col, b_col = ab[:, h:h + 1], ab[:, HEADS + h:HEADS + h + 1]
    a_log, dt_bias = alog_row[:, h:h + 1], dtb_row[:, h:h + 1]
    qn = cq * lax.rsqrt(jnp.sum(cq * cq, axis=-1, keepdims=True) + EPS) * (HD ** -0.5)
    kn = ck * lax.rsqrt(jnp.sum(ck * ck, axis=-1, keepdims=True) + EPS)
    la_col = -jnp.exp(a_log) * jax.nn.softplus(a_col + dt_bias)
    beta = jax.nn.sigmoid(b_col)
    la_row = jnp.sum(jnp.where(eye, la_col, 0.0), axis=0, keepdims=True)
    g_col = jnp.sum(jnp.where(incl, la_row, 0.0), axis=1, keepdims=True)
    g_row = jnp.sum(jnp.where(row <= col, la_col, 0.0), axis=0, keepdims=True)
    g_last = jnp.sum(la_col, axis=0, keepdims=True)
    gamma = jnp.where(incl, jnp.exp(jnp.where(incl, g_col - g_row, 0.0)), 0.0)
    lower = jnp.where(strict, beta * _hdot(kn, kn, NT) * gamma, 0.0)
    inv = jnp.where(eye, 1.0, 0.0) - lower
    pw = _hdot(lower, lower)
    for step in range(5):
        inv = inv + _hdot(inv, pw)
        if step < 4:
            pw = _hdot(pw, pw)
    u = _hdot(inv, cv * beta)
    w = _hdot(inv, kn * (beta * jnp.exp(g_col)))
    qk = _hdot(qn, kn, NT) * gamma
    return u, w, qk, qn * jnp.exp(g_col), kn * jnp.exp(g_last - g_col), jnp.exp(g_last)


def _gdn_post(o, z, gain):
    y = o * lax.rsqrt(jnp.mean(o * o, axis=-1, keepdims=True) + EPS) * gain
    return y * (z * jax.nn.sigmoid(z))


def _gdn_specs(nc, reverse):
    c = GDN_CHUNK

    def ch(n):
        return nc - 1 - n if reverse else n

    def wide(array_off):
        return pl.BlockSpec((c, GDN_W), lambda n: (ch(n), array_off // GDN_W))

    ab = pl.BlockSpec((c, HD), lambda n: (ch(n), OFF_AB // HD))
    row = pl.BlockSpec((1, HD), lambda n: (0, 0))
    state = pl.BlockSpec((None, HEADS, HD, HD), lambda n: (ch(n), 0, 0, 0))
    return wide, ab, row, state


def _gdn_fwd(cqkv, proj, a_log, dt_bias, gain):
    t = proj.shape[0]
    nc = t // GDN_CHUNK
    wide, ab, row, state = _gdn_specs(nc, False)

    def body(cq_ref, ck_ref, cv_ref, ab_ref, z_ref, al_ref, dt_ref, g_ref, y_ref, sprev_ref, s_scr):
        @pl.when(pl.program_id(0) == 0)
        def _():
            s_scr[...] = jnp.zeros_like(s_scr)

        for h in range(HEADS):
            u, w, qk, qd, kd, dec = _gdn_prep(_head(cq_ref, h), _head(ck_ref, h), _head(cv_ref, h), ab_ref[...],
                                              al_ref[...], dt_ref[...], h)
            s = s_scr[h]
            sprev_ref[h] = s
            v_new = u - _hdot(w, s)
            o = _hdot(qd, s) + _hdot(qk, v_new)
            s_scr[h] = s * dec + _hdot(kd, v_new, TN)
            y_ref[:, h * HD:(h + 1) * HD] = _gdn_post(o, _head(z_ref, h), g_ref[...]).astype(y_ref.dtype)

    return pl.pallas_call(
        body, name="gdn_fwd", grid=(nc,),
        in_specs=[wide(0), wide(GDN_W), wide(2 * GDN_W), ab, wide(OFF_Z), row, row, row],
        out_specs=[wide(0), state], out_shape=[_sds((t, GDN_W), BF16), _sds((nc, HEADS, HD, HD), F32)],
        scratch_shapes=[pltpu.VMEM((HEADS, HD, HD), F32)], compiler_params=_cp("arbitrary"),
    )(cqkv, cqkv, cqkv, proj, proj, a_log, dt_bias, gain)


def _gdn_bwd(cqkv, proj, a_log, dt_bias, gain, sprev, dy):
    t = proj.shape[0]
    nc = t // GDN_CHUNK
    wide, ab, row, state = _gdn_specs(nc, True)

    def body(cq_ref, ck_ref, cv_ref, ab_ref, z_ref, al_ref, dt_ref, g_ref, sp_ref, dy_ref,
             dc_ref, dab_ref, dz_ref, dal_ref, ddt_ref, dg_ref, ds_scr):
        @pl.when(pl.program_id(0) == 0)
        def _():
            ds_scr[...] = jnp.zeros_like(ds_scr)
            dal_ref[...] = jnp.zeros_like(dal_ref)
            ddt_ref[...] = jnp.zeros_like(ddt_ref)
            dg_ref[...] = jnp.zeros_like(dg_ref)

        dab_sum = jnp.zeros(dab_ref.shape, F32)
        for h in range(HEADS):
            (u, w, qk, qd, kd, dec), prep_vjp = jax.vjp(
                functools.partial(_gdn_prep, h=h),
                _head(cq_ref, h), _head(ck_ref, h), _head(cv_ref, h), ab_ref[...], al_ref[...], dt_ref[...])
            s = sp_ref[h]
            v_new = u - _hdot(w, s)
            o = _hdot(qd, s) + _hdot(qk, v_new)
            _, post_vjp = jax.vjp(_gdn_post, o, _head(z_ref, h), g_ref[...])
            do, dz, dgain = post_vjp(_head(dy_ref, h).astype(F32))
            ds_next = ds_scr[h]
            d_vnew = _hdot(qk, do, TN) + _hdot(kd, ds_next)
            d_qk = _hdot(do, v_new, NT)
            d_qd = _hdot(do, s, NT)
            d_kd = _hdot(v_new, ds_next, NT)
            d_dec = jnp.sum(jnp.sum(s * ds_next, axis=1, keepdims=True), axis=0, keepdims=True)
            ds_scr[h] = dec * ds_next + _hdot(qd, do, TN) - _hdot(w, d_vnew, TN)
            d_w = -_hdot(d_vnew, s, NT)
            dcq, dck, dcv, dab, dal, ddt = prep_vjp((d_vnew, d_w, d_qk, d_qd, d_kd, d_dec))
            dc_ref[:, h * HD:(h + 1) * HD] = dcq
            dc_ref[:, GDN_W + h * HD:GDN_W + (h + 1) * HD] = dck
            dc_ref[:, 2 * GDN_W + h * HD:2 * GDN_W + (h + 1) * HD] = dcv
            dz_ref[:, h * HD:(h + 1) * HD] = dz.astype(dz_ref.dtype)
            dab_sum = dab_sum + dab
            dal_ref[...] += dal
            ddt_ref[...] += ddt
            dg_ref[...] += dgain
        dab_ref[...] = dab_sum

    c = GDN_CHUNK
    return pl.pallas_call(
        body, name="gdn_bwd", grid=(nc,),
        in_specs=[wide(0), wide(GDN_W), wide(2 * GDN_W), ab, wide(OFF_Z), row, row, row, state, wide(0)],
        out_specs=[pl.BlockSpec((c, 3 * GDN_W), lambda n: (nc - 1 - n, 0)), pl.BlockSpec((c, HD), lambda n: (nc - 1 - n, 0)),
                   wide(0), row, row, row],
        out_shape=[_sds((t, 3 * GDN_W), F32), _sds((t, HD), F32), _sds((t, GDN_W), BF16),
                   _sds((1, HD), F32), _sds((1, HD), F32), _sds((1, HD), F32)],
        scratch_shapes=[pltpu.VMEM((HEADS, HD, HD), F32)], compiler_params=_cp("arbitrary"),
    )(cqkv, cqkv, cqkv, proj, proj, a_log, dt_bias, gain, sprev, dy)


MERGE_TN = 512


def _merge_specs(t, tm):
    tn = MERGE_TN
    ys = [pl.BlockSpec((tm, wd), lambda i, j: (i, 0)) for wd in (POOL_W, SB_W, GDN_W)]
    ws = [pl.BlockSpec((None, wd, tn), lambda i, j: (j, 0, 0)) for wd in (POOL_W, SB_W, GDN_W)]
    gs = [pl.BlockSpec((tm, tn), functools.partial(lambda i, j, b: (i, OFF_GATE // tn + b * (D // tn) + j), b=b))
          for b in range(3)]
    out = pl.BlockSpec((tm, tn), lambda i, j: (i, j))
    return ys, ws, gs, out


def _merge_fwd(ys, wups, proj):
    t = proj.shape[0]
    tm = min(512, t)
    y_specs, w_specs, g_specs, out = _merge_specs(t, tm)

    def body(y0, y1, y2, w0, w1, w2, g0, g1, g2, o_ref):
        acc = jnp.zeros(o_ref.shape, F32)
        for y, w, g in ((y0, w0, g0), (y1, w1, g1), (y2, w2, g2)):
            acc = acc + jax.nn.sigmoid(g[...]) * _dot(y[...], w[...])
        o_ref[...] = acc.astype(o_ref.dtype)

    return pl.pallas_call(
        body, name="merge_fwd", grid=(t // tm, D // MERGE_TN), in_specs=[*y_specs, *w_specs, *g_specs],
        out_specs=out, out_shape=_sds((t, D), BF16), compiler_params=_cp("parallel", "parallel"),
    )(*ys, *wups, proj, proj, proj)


def _merge_bwd(ys, wups, proj, dmerged):
    t = proj.shape[0]
    tm = min(512, t)
    y_specs, w_specs, g_specs, out = _merge_specs(t, tm)

    def body(y0, y1, y2, w0, w1, w2, g0, g1, g2, dm_ref, dg0, dg1, dg2, dm0, dm1, dm2):
        dm = dm_ref[...].astype(F32)
        for y, w, g, dg, dmb in ((y0, w0, g0, dg0, dm0), (y1, w1, g1, dg1, dm1), (y2, w2, g2, dg2, dm2)):
            sg = jax.nn.sigmoid(g[...])
            dg[...] = (dm * _dot(y[...], w[...]) * sg * (1.0 - sg)).astype(dg.dtype)
            dmb[...] = (dm * sg).astype(dmb.dtype)

    return pl.pallas_call(
        body, name="merge_bwd", grid=(t // tm, D // MERGE_TN), in_specs=[*y_specs, *w_specs, *g_specs, out],
        out_specs=[out] * 6, out_shape=[_sds((t, D), BF16)] * 6, compiler_params=_cp("parallel", "parallel"),
    )(*ys, *wups, proj, proj, proj, dmerged)


def _tile(t, want):
    return min(t, want)


def _layer_fwd(x, l, gw, w_al, sp):
    t = x.shape[0]
    tm = _tile(t, 1024)
    u = _rms_fwd("rms_attn", x, sp["attn_norm"][l])
    proj = _mm("proj", u, w_al, m=t, n=N_AL, k=D, tm=tm, tn=1024, tk=512, a_spec=_a_plain(tm, 512),
               b_spec=_b_plain(512, 1024), dims=None, out_shapes=[_sds((t, N_AL), F32)], out_specs=[_o_plain(tm, 1024)])[0]
    y_pool = _pool_fwd(proj, sp["pool_w"][l], sp["pool_scale"][l])
    kv = _sb_cast_kv(proj)
    y_sb = _sb_fwd(proj, kv)
    cqkv = _conv_fwd(proj, sp["conv"][l])
    y_gdn, sprev = _gdn_fwd(cqkv, proj, sp["a_log"][l], sp["dt_bias"][l], sp["gdn_norm"][l])
    ys = (y_pool, y_sb, y_gdn)
    wups = (gw["w_pool_up"], gw["w_sb_up"], gw["w_gdn_up"])
    merged = _merge_fwd(ys, wups, proj)
    x1 = _mm("out_proj", merged, gw["w_out"], m=t, n=D, k=D, tm=tm, tn=1024, tk=512, a_spec=_a_plain(tm, 512),
             b_spec=_w_rows(512, 1024, 512), dims=None, out_shapes=[_sds((t, D), F32)], out_specs=[_o_plain(tm, 1024)],
             extras=[x], extra_specs=[_o_plain(tm, 1024)], epilogue=lambda r, xr: (r + xr,))[0]
    u2 = _rms_fwd("rms_mlp", x1, sp["mlp_norm"][l])

    def relu2(r):
        hv = jnp.maximum(r, 0.0)
        return hv, hv * hv

    hid, hid2 = _mm("ff1", u2, gw["w_ff1"], m=t, n=D_FF, k=D, tm=tm, tn=1024, tk=512, a_spec=_a_plain(tm, 512),
                    b_spec=_w_cols(512, 1024, 2048), dims=None, out_shapes=[_sds((t, D_FF), BF16)] * 2,
                    out_specs=[_o_plain(tm, 1024)] * 2, epilogue=relu2)
    x2 = _mm("ff2", hid2, gw["w_ff2"], m=t, n=D, k=D_FF, tm=tm, tn=1024, tk=512, a_spec=_a_plain(tm, 512),
             b_spec=_w_rows(512, 1024, 2048), dims=None, out_shapes=[_sds((t, D), F32)], out_specs=[_o_plain(tm, 1024)],
             extras=[x1], extra_specs=[_o_plain(tm, 1024)], epilogue=lambda r, xr: (r + xr,))[0]
    saved = dict(x=x, u=u, proj=proj, kv=kv, cqkv=cqkv, sprev=sprev, ys=ys, merged=merged, x1=x1, u2=u2, hid=hid, hid2=hid2)
    return x2, saved


def _layer_bwd(dx2, l, gw, w_al, sp, sv):
    t = dx2.shape[0]
    tm = _tile(t, 1024)
    tk = _tile(t, 512)
    g = {}
    dpre = _mm("ff2_dx", dx2, gw["w_ff2"], m=t, n=D_FF, k=D, tm=tm, tn=1024, tk=512, a_spec=_a_plain(tm, 512),
               b_spec=_w_rows_t(512, 1024, 2048), dims=NT, out_shapes=[_sds((t, D_FF), BF16)],
               out_specs=[_o_plain(tm, 1024)], extras=[sv["hid"]], extra_specs=[_o_plain(tm, 1024)],
               epilogue=lambda r, hv: (r * (2.0 * hv.astype(F32)),))[0]
    g["w_ff2"] = _mm("ff2_dw", sv["hid2"], dx2, m=D_FF, n=D, k=t, tm=1024, tn=1024, tk=tk, a_spec=_a_trans(1024, tk),
                     b_spec=_b_plain(tk, 1024), dims=TN, out_shapes=[_sds((D_FF, D), BF16)],
                     out_specs=[_o_plain(1024, 1024)])[0].reshape(N_CHIPS, D_FF // N_CHIPS, D)
    du2 = _mm("ff1_dx", dpre, gw["w_ff1"], m=t, n=D, k=D_FF, tm=tm, tn=1024, tk=512, a_spec=_a_plain(tm, 512),
              b_spec=_w_cols_t(512, 1024, 2048), dims=NT, out_shapes=[_sds((t, D), F32)], out_specs=[_o_plain(tm, 1024)])[0]
    g["w_ff1"] = _mm("ff1_dw", sv["u2"], dpre, m=D, n=D_FF, k=t, tm=1024, tn=1024, tk=tk, a_spec=_a_trans(1024, tk),
                     b_spec=_b_plain(tk, 1024), dims=TN, out_shapes=[_sds((N_CHIPS, D, D_FF // N_CHIPS), BF16)],
                     out_specs=[_o_colshard(1024, 1024, D_FF // N_CHIPS)])[0]
    dx1, g["mlp_norm"] = _rms_bwd("rms_mlp_bwd", du2, sv["x1"], sp["mlp_norm"][l], dx2)
    dmerged = _mm("out_dx", dx1, gw["w_out"], m=t, n=D, k=D, tm=tm, tn=512, tk=1024, a_spec=_a_plain(tm, 1024),
                  b_spec=_w_rows_t(1024, 512, 512), dims=NT, out_shapes=[_sds((t, D), BF16)], out_specs=[_o_plain(tm, 512)])[0]
    g["w_out"] = _mm("out_dw", sv["merged"], dx1, m=D, n=D, k=t, tm=1024, tn=1024, tk=tk, a_spec=_a_trans(1024, tk),
                     b_spec=_b_plain(tk, 1024), dims=TN, out_shapes=[_sds((D, D), BF16)],
                     out_specs=[_o_plain(1024, 1024)])[0].reshape(N_CHIPS, D // N_CHIPS, D)
    wups = (gw["w_pool_up"], gw["w_sb_up"], gw["w_gdn_up"])
    dg0, dg1, dg2, dm0, dm1, dm2 = _merge_bwd(sv["ys"], wups, sv["proj"], dmerged)
    dys = []
    for nm, yb, dmb, wd in zip(("w_pool_up", "w_sb_up", "w_gdn_up"), sv["ys"], (dm0, dm1, dm2), (POOL_W, SB_W, GDN_W)):
        dys.append(_mm(nm + "_dx", dmb, gw[nm], m=t, n=wd, k=D, tm=tm, tn=256, tk=512, a_spec=_a_plain(tm, 512),
                       b_spec=_w_cols_t(512, 256, 512), dims=NT, out_shapes=[_sds((t, wd), F32)],
                       out_specs=[_o_plain(tm, 256)])[0])
        g[nm] = _mm(nm + "_dw", yb, dmb, m=wd, n=D, k=t, tm=256, tn=512, tk=tk, a_spec=_a_trans(256, tk),
                    b_spec=_b_plain(tk, 512), dims=TN, out_shapes=[_sds((N_CHIPS, wd, D // N_CHIPS), BF16)],
                    out_specs=[_o_colshard(256, 512, D // N_CHIPS)])[0]
    proj = sv["proj"]
    dp, g["pool_w"], g["pool_scale"] = _pool_bwd(proj, sp["pool_w"][l], sp["pool_scale"][l], dys[0])
    dsq, dsk, dsv = _sb_bwd(proj, sv["kv"], dys[1])
    dc, dab, dz, g["a_log"], g["dt_bias"], g["gdn_norm"] = _gdn_bwd(
        sv["cqkv"], proj, sp["a_log"][l], sp["dt_bias"][l], sp["gdn_norm"][l], sv["sprev"], dys[2])
    dgx, g["conv"] = _conv_bwd(proj, sp["conv"][l], dc)
    dproj = jnp.concatenate(
        [dsq, dsk.astype(BF16), dsv.astype(BF16), dgx, dz, dab.astype(BF16), jnp.zeros((t, AB_W - HD), BF16),
         dp, dg0, dg1, dg2], axis=1)
    du = _mm("proj_dx", dproj, w_al, m=t, n=D, k=N_AL, tm=tm, tn=1024, tk=512, a_spec=_a_plain(tm, 512),
             b_spec=_b_trans(512, 1024), dims=NT, out_shapes=[_sds((t, D), F32)], out_specs=[_o_plain(tm, 1024)])[0]
    g["w_al"] = _mm("proj_dw", sv["u"], dproj, m=D, n=N_AL, k=t, tm=1024, tn=1024, tk=tk, a_spec=_a_trans(1024, tk),
                    b_spec=_b_plain(tk, 1024), dims=TN, out_shapes=[_sds((D, N_AL), BF16)], out_specs=[_o_plain(1024, 1024)])[0]
    dx, g["attn_norm"] = _rms_bwd("rms_attn_bwd", du, sv["x"], sp["attn_norm"][l], dx1)
    return dx, g


def _align_w_in(w):
    n_ab = ORIG_GATE - ORIG_AB
    return jnp.concatenate([w[:, ORIG_SB:ORIG_GATE], jnp.zeros((D, AB_W - n_ab), w.dtype), w[:, :ORIG_SB], w[:, ORIG_GATE:]],
                           axis=1)


def _unalign_w_in(w):
    n_ab = ORIG_GATE - ORIG_AB
    return jnp.concatenate([w[:, OFF_P:OFF_GATE], w[:, :OFF_AB + n_ab], w[:, OFF_GATE:]], axis=1)


W_IN_RUNS = ((0, ORIG_SB, OFF_P), (ORIG_SB, ORIG_GATE, OFF_SB), (ORIG_GATE, N_IN, OFF_GATE))
W_IN_SHARD = N_IN // N_CHIPS


def _w_in_from_shards(gathered):
    parts = []
    for lo, hi, al in sorted(W_IN_RUNS, key=lambda r: r[2]):
        if al == OFF_P:
            parts.append(jnp.zeros((D, OFF_P - (OFF_AB + ORIG_GATE - ORIG_AB)), gathered.dtype))
        while lo < hi:
            chip = lo // W_IN_SHARD
            end = min(hi, (chip + 1) * W_IN_SHARD)
            parts.append(gathered[chip, :, lo - chip * W_IN_SHARD:end - chip * W_IN_SHARD])
            lo = end
    return jnp.concatenate(parts, axis=1)


def _w_in_to_shards(g_al):
    shards = []
    for chip in range(N_CHIPS):
        a, b = chip * W_IN_SHARD, (chip + 1) * W_IN_SHARD
        parts = [g_al[:, al + max(a, lo) - lo:al + min(b, hi) - lo] for lo, hi, al in W_IN_RUNS if max(a, lo) < min(b, hi)]
        shards.append(jnp.concatenate(parts, axis=1))
    return jnp.stack(shards)


def _row128(v):
    return jnp.pad(v.reshape(1, -1), ((0, 0), (0, HD - v.shape[-1])))


def _local_step(x, target, gw, w_in_al, sp):
    saved = []
    h = x
    for l in range(2):
        h, sv = _layer_fwd(h, l, gw[l], w_in_al[l], sp)
        saved.append(sv)
    loss, dh, g_final = _loss_head(h, sp["final_norm"], target)
    grads = [None, None]
    for l in (1, 0):
        dh, grads[l] = _layer_bwd(dh, l, gw[l], w_in_al[l], sp, saved[l])
    return loss, dh, grads, g_final


ANY = pl.BlockSpec(memory_space=pl.ANY)


def _me():
    return lax.axis_index("x"), lax.axis_index("y"), lax.axis_index("c")


def _other_chips(x, y):
    return [(1 - x, y), (x, 1 - y), (1 - x, 1 - y)]


def _half(ref, axis, c, rows):
    half = rows // 2
    idx = [slice(None)] * axis + [pl.ds(pl.multiple_of(c * half, 16), half)]
    return ref.at[tuple(idx)]


def _gather_steps(out, send, recv):
    n = len(out)
    x, y, c = _me()
    mine = 2 * x + y
    sibling = (x, y, 1 - c)
    chips = _other_chips(x, y)
    sends = []
    for t in range(n):
        rows = out[t].shape[1]
        for k, (px, py) in enumerate(chips):
            own_half = _half(out[t].at[mine], 0, c, rows)
            cp = pltpu.make_async_remote_copy(
                src_ref=own_half, dst_ref=own_half,
                send_sem=send.at[6 * t + k], recv_sem=recv.at[6 * t + k], device_id=(px, py, c), device_id_type=MESH)
            cp.start()
            sends.append(cp)
    for t in range(n):
        rows = out[t].shape[1]
        for k, (px, py) in enumerate(chips):
            landed = _half(out[t].at[2 * px + py], 0, c, rows)
            pltpu.make_async_remote_copy(
                src_ref=landed, dst_ref=landed, send_sem=send.at[6 * t + k], recv_sem=recv.at[6 * t + k],
                device_id=(px, py, c), device_id_type=MESH).wait_recv()
            cp = pltpu.make_async_remote_copy(
                src_ref=landed, dst_ref=landed, send_sem=send.at[6 * t + 3 + k], recv_sem=recv.at[6 * t + 3 + k],
                device_id=sibling, device_id_type=MESH)
            cp.start()
            sends.append(cp)
    for t in range(n):
        rows = out[t].shape[1]
        for k, (px, py) in enumerate(chips):
            other = _half(out[t].at[2 * px + py], 0, 1 - c, rows)
            pltpu.make_async_remote_copy(
                src_ref=other, dst_ref=other, send_sem=send.at[6 * t + 3 + k], recv_sem=recv.at[6 * t + 3 + k],
                device_id=sibling, device_id_type=MESH).wait_recv()
    for cp in sends:
        cp.wait_send()


def _gather_weights(bufs):
    n = len(bufs)

    def body(*refs):
        _gather_steps(refs[n:2 * n], *refs[2 * n:])

    return pl.pallas_call(
        body, name="gather_weights", in_specs=[ANY] * n, out_specs=[ANY] * n,
        out_shape=[_sds(s.shape, s.dtype) for s in bufs], input_output_aliases={t: t for t in range(n)},
        scratch_shapes=[pltpu.SemaphoreType.DMA((6 * n,)), pltpu.SemaphoreType.DMA((6 * n,))],
    )(*bufs)


GATHER_ASYNC_ID = 1


def _gather_weights_async(bufs):
    n = len(bufs)
    refs = [jax.new_ref(b, memory_space=pltpu.MemorySpace.HBM) for b in bufs]

    @pl.kernel(mesh=plsc.ScalarSubcoreMesh(axis_name="sequencer", num_cores=1), name="gather_weights_async",
               scratch_types=(pltpu.SemaphoreType.DMA((6 * n,)), pltpu.SemaphoreType.DMA((6 * n,))),
               compiler_params=pltpu.CompilerParams(collective_id=GATHER_ASYNC_ID))
    def launch(send, recv):
        x, y, c = _me()
        barrier = pltpu.get_barrier_semaphore()
        peers = [(x, y, 1 - c)] + [(px, py, c) for px, py in _other_chips(x, y)]
        for peer in peers:
            pl.semaphore_signal(barrier, inc=1, device_id=peer, device_id_type=MESH)
        pl.semaphore_wait(barrier, len(peers))
        _gather_steps(refs, send, recv)

    launch()
    return [r[...] for r in refs]


def _rs_pair(grads):
    n = len(grads)

    def body(*refs):
        g, out = refs[:n], refs[n:2 * n]
        send, recv = refs[2 * n:]
        x, y, c = _me()
        copies = []
        for t in range(n):
            cp = pltpu.make_async_remote_copy(
                src_ref=_half(g[t], 1, 1 - c, g[t].shape[1]), dst_ref=out[t], send_sem=send.at[t], recv_sem=recv.at[t],
                device_id=(x, y, 1 - c), device_id_type=MESH)
            cp.start()
            copies.append(cp)
        for cp in copies:
            cp.wait()

    return pl.pallas_call(
        body, name="rs_pair", in_specs=[ANY] * n, out_specs=[ANY] * n,
        out_shape=[_sds((N_CHIPS, s.shape[1] // 2, s.shape[2]), s.dtype) for s in grads],
        scratch_shapes=[pltpu.SemaphoreType.DMA((n,)), pltpu.SemaphoreType.DMA((n,))],
    )(*grads)


def _rs_chips(parts):
    n = len(parts)

    def body(*refs):
        p, out = refs[:n], refs[n:2 * n]
        send, recv = refs[2 * n:]
        x, y, c = _me()
        copies = []
        for t in range(n):
            for k, (px, py) in enumerate(_other_chips(x, y)):
                cp = pltpu.make_async_remote_copy(
                    src_ref=p[t].at[2 * px + py], dst_ref=out[t].at[k], send_sem=send.at[3 * t + k],
                    recv_sem=recv.at[3 * t + k], device_id=(px, py, c), device_id_type=MESH)
                cp.start()
                copies.append(cp)
        for cp in copies:
            cp.wait()

    return pl.pallas_call(
        body, name="rs_chips", in_specs=[ANY] * n, out_specs=[ANY] * n,
        out_shape=[_sds((3, *s.shape[1:]), s.dtype) for s in parts],
        scratch_shapes=[pltpu.SemaphoreType.DMA((3 * n,)), pltpu.SemaphoreType.DMA((3 * n,))],
    )(*parts)


def _pair_exchange(bufs):
    n = len(bufs)

    def body(*refs):
        out = refs[n:2 * n]
        send, recv = refs[2 * n:]
        x, y, c = _me()
        copies = []
        for t in range(n):
            cp = pltpu.make_async_remote_copy(
                src_ref=out[t].at[c], dst_ref=out[t].at[c], send_sem=send.at[t], recv_sem=recv.at[t],
                device_id=(x, y, 1 - c), device_id_type=MESH)
            cp.start()
            copies.append(cp)
        for t, cp in enumerate(copies):
            cp.wait_send()
            pltpu.make_async_remote_copy(
                src_ref=out[t].at[1 - c], dst_ref=out[t].at[1 - c], send_sem=send.at[t], recv_sem=recv.at[t],
                device_id=(x, y, 1 - c), device_id_type=MESH).wait_recv()

    return pl.pallas_call(
        body, name="pair_exchange", in_specs=[ANY] * n, out_specs=[ANY] * n,
        out_shape=[_sds(s.shape, s.dtype) for s in bufs], input_output_aliases={t: t for t in range(n)},
        scratch_shapes=[pltpu.SemaphoreType.DMA((n,)), pltpu.SemaphoreType.DMA((n,))],
    )(*bufs)


def _row_tile(rows, cols, itemsize, budget=2 * 1024 * 1024):
    tr = rows
    while tr * cols * itemsize > budget and tr % 32 == 0:
        tr //= 2
    return tr


def _sum_pair(name, g, got, where):
    nchip, rows, cols = g.shape
    half = rows // 2
    tr = _row_tile(half, cols, 4)
    per = half // tr

    def body(w_ref, g_ref, r_ref, o_ref):
        o_ref[...] = (g_ref[...].astype(F32) + r_ref[...].astype(F32)).astype(o_ref.dtype)

    blk = pl.BlockSpec((None, tr, cols), lambda j, i, w_ref: (j, i, 0))
    return pl.pallas_call(
        body, name=name,
        grid_spec=pltpu.PrefetchScalarGridSpec(
            num_scalar_prefetch=1, grid=(nchip, per),
            in_specs=[pl.BlockSpec((None, tr, cols), lambda j, i, w_ref: (j, w_ref[1] * per + i, 0)), blk], out_specs=blk),
        out_shape=_sds((nchip, half, cols), BF16), compiler_params=_cp("parallel", "parallel"),
    )(where, g, got)


def _sum_chips(name, p, got, where):
    _, rows, cols = p.shape
    tr = _row_tile(rows, cols, 4)

    def body(w_ref, p_ref, r0, r1, r2, o_ref):
        o_ref[...] = ((p_ref[...].astype(F32) + r0[...].astype(F32)) + r1[...].astype(F32)) + r2[...].astype(F32)

    def got_k(k):
        return pl.BlockSpec((None, tr, cols), lambda i, w_ref: (k, i, 0))

    return pl.pallas_call(
        body, name=name,
        grid_spec=pltpu.PrefetchScalarGridSpec(
            num_scalar_prefetch=1, grid=(rows // tr,),
            in_specs=[pl.BlockSpec((None, tr, cols), lambda i, w_ref: (w_ref[0], i, 0)), got_k(0), got_k(1), got_k(2)],
            out_specs=pl.BlockSpec((None, tr, cols), lambda i, w_ref: (w_ref[1], i, 0))),
        out_shape=_sds((2, rows, cols), F32), compiler_params=_cp("parallel"),
    )(where, p, got, got, got)


def _reduce_scatter(grads, where):
    got = _rs_pair(grads)
    parts = [_sum_pair(f"sum_pair_{t}", g, r, where) for t, (g, r) in enumerate(zip(grads, got))]
    got = _rs_chips(parts)
    halves = [_sum_chips(f"sum_chips_{t}", p, r, where) for t, (p, r) in enumerate(zip(parts, got))]
    return _pair_exchange(halves)


def _all_reduce_small(name, v):
    rows = v.shape[0]

    def body(v_ref, o_ref, land, send, recv):
        x, y, c = _me()
        mine = 4 * x + 2 * y + c
        copies = []
        for k in range(1, 8):
            kx, ky, kc = k >> 2, (k >> 1) & 1, k & 1
            peer = (x ^ kx, y ^ ky, c ^ kc)
            cp = pltpu.make_async_remote_copy(
                src_ref=v_ref, dst_ref=land.at[mine], send_sem=send.at[k - 1], recv_sem=recv.at[k - 1],
                device_id=peer, device_id_type=MESH)
            cp.start()
            copies.append(cp)
        land[mine] = v_ref[...]
        for k in range(1, 8):
            kx, ky, kc = k >> 2, (k >> 1) & 1, k & 1
            src = 4 * (x ^ kx) + 2 * (y ^ ky) + (c ^ kc)
            pltpu.make_async_remote_copy(
                src_ref=v_ref, dst_ref=land.at[src], send_sem=send.at[k - 1], recv_sem=recv.at[k - 1],
                device_id=(x ^ kx, y ^ ky, c ^ kc), device_id_type=MESH).wait_recv()
        acc = land[0]
        for d in range(1, 8):
            acc = acc + land[d]
        o_ref[...] = acc
        for cp in copies:
            cp.wait_send()

    vm = pl.BlockSpec(memory_space=pltpu.VMEM)
    return pl.pallas_call(
        body, name=name, in_specs=[vm], out_specs=vm, out_shape=_sds((rows, 128), F32),
        scratch_shapes=[pltpu.VMEM((8, rows, 128), F32), pltpu.SemaphoreType.DMA((7,)), pltpu.SemaphoreType.DMA((7,))],
    )(v)


def _adamw(name, w, g, m, v):
    rows, cols = w.shape
    tr = _row_tile(rows, cols, 4, budget=1024 * 1024)
    c1 = 1.0 / (1.0 - ADAM_B1 ** ADAM_STEP)
    c2 = 1.0 / (1.0 - ADAM_B2 ** ADAM_STEP)

    def body(w_ref, g_ref, m_ref, v_ref, d_ref, nm_ref, nv_ref):
        gv = g_ref[...]
        nm = ADAM_B1 * m_ref[...] + (1.0 - ADAM_B1) * gv
        nv = ADAM_B2 * v_ref[...] + (1.0 - ADAM_B2) * (gv * gv)
        d_ref[...] = -ADAM_LR * ((nm * c1) / (jnp.sqrt(nv * c2) + ADAM_EPS) + ADAM_WD * w_ref[...])
        nm_ref[...] = nm
        nv_ref[...] = nv

    blk = pl.BlockSpec((tr, cols), lambda i: (i, 0))
    return pl.pallas_call(
        body, name=name, grid=(rows // tr,), in_specs=[blk] * 4, out_specs=[blk] * 3,
        out_shape=[_sds((rows, cols), F32)] * 3, compiler_params=_cp("parallel"),
    )(w, g, m, v)


def _adamw_layers(name, w, g0, g1, m, v):
    _, half, cols = g0.shape
    tr = _row_tile(half, cols, 4, budget=1024 * 1024)
    per_half = half // tr
    per = 2 * per_half
    c1 = 1.0 / (1.0 - ADAM_B1 ** ADAM_STEP)
    c2 = 1.0 / (1.0 - ADAM_B2 ** ADAM_STEP)

    def body(w_ref, g0_ref, g1_ref, m_ref, v_ref, g_ref, d_ref, nm_ref, nv_ref):
        gv = jnp.where(pl.program_id(0) == 0, g0_ref[...], g1_ref[...])
        nm = ADAM_B1 * m_ref[...] + (1.0 - ADAM_B1) * gv
        nv = ADAM_B2 * v_ref[...] + (1.0 - ADAM_B2) * (gv * gv)
        g_ref[...] = gv
        d_ref[...] = -ADAM_LR * ((nm * c1) / (jnp.sqrt(nv * c2) + ADAM_EPS) + ADAM_WD * w_ref[...])
        nm_ref[...] = nm
        nv_ref[...] = nv

    both = pl.BlockSpec((None, tr, cols), lambda l, i: (l, i, 0))

    def halves(i):
        return i // per_half, i % per_half, 0

    first = pl.BlockSpec((None, tr, cols), lambda l, i: halves(i * (1 - l) + (per - 1) * l))
    second = pl.BlockSpec((None, tr, cols), lambda l, i: halves(i * l))
    return pl.pallas_call(
        body, name=name, grid=(2, per), in_specs=[both, first, second, both, both], out_specs=[both] * 4,
        out_shape=[_sds(w.shape, F32)] * 4, compiler_params=_cp("arbitrary", "arbitrary"),
    )(w, g0, g1, m, v)


def _to_bf16_slot(name, w, l, where):
    _, rows, cols = w.shape
    tr = _row_tile(rows, cols, 4)

    def body(w_ref, x_ref, o_ref):
        o_ref[...] = x_ref[...].astype(BF16)

    return pl.pallas_call(
        body, name=name,
        grid_spec=pltpu.PrefetchScalarGridSpec(
            num_scalar_prefetch=1, grid=(rows // tr,), in_specs=[pl.BlockSpec((None, tr, cols), lambda i, w_ref: (l, i, 0))],
            out_specs=pl.BlockSpec((None, tr, cols), lambda i, w_ref: (w_ref[0], i, 0))),
        out_shape=_sds((N_CHIPS, rows, cols), BF16), compiler_params=_cp("parallel"))(where, w)


BIG = ("w_in", "w_pool_up", "w_sb_up", "w_gdn_up", "w_out", "w_ff1", "w_ff2")
SMALL = (("attn_norm", (D,)), ("pool_w", (4, 128, 128)), ("pool_scale", (POOL_W,)), ("gdn_a_log", (HEADS,)),
         ("gdn_dt_bias", (HEADS,)), ("gdn_norm", (HD,)), ("mlp_norm", (D,)))


PACK_TILE = 8 * 128


def _rows128(a):
    flat = a.reshape(-1)
    pad = (-flat.shape[0]) % PACK_TILE
    return jnp.pad(flat, (0, pad)).reshape(-1, 128)


def _pack(parts):
    packed = jnp.concatenate([_rows128(p) for p in parts], axis=0)
    return jnp.pad(packed, ((0, (-packed.shape[0]) % 8), (0, 0)))


def _unpack(packed, shapes):
    out, r = [], 0
    for shp in shapes:
        size = 1
        for s in shp:
            size *= s
        nr = -(-size // PACK_TILE) * 8
        out.append(packed[r:r + nr].reshape(-1)[:size].reshape(shp))
        r += nr
    return out


def kernel(x, attn_norm, w_in, pool_w, pool_scale, gdn_conv, gdn_a_log, gdn_dt_bias, gdn_norm, w_pool_up, w_sb_up, w_gdn_up, w_out, mlp_norm, w_ff1, w_ff2, final_norm, loss_target, m_attn_norm, m_w_in, m_pool_w, m_pool_scale, m_gdn_conv, m_gdn_a_log, m_gdn_dt_bias, m_gdn_norm, m_w_pool_up, m_w_sb_up, m_w_gdn_up, m_w_out, m_mlp_norm, m_w_ff1, m_w_ff2, m_final_norm, v_attn_norm, v_w_in, v_pool_w, v_pool_scale, v_gdn_conv, v_gdn_a_log, v_gdn_dt_bias, v_gdn_norm, v_w_pool_up, v_w_sb_up, v_w_gdn_up, v_w_out, v_mlp_norm, v_w_ff1, v_w_ff2, v_final_norm):
    weights = dict(attn_norm=attn_norm, w_in=w_in, pool_w=pool_w, pool_scale=pool_scale, gdn_conv=gdn_conv,
                   gdn_a_log=gdn_a_log, gdn_dt_bias=gdn_dt_bias, gdn_norm=gdn_norm, w_pool_up=w_pool_up, w_sb_up=w_sb_up,
                   w_gdn_up=w_gdn_up, w_out=w_out, mlp_norm=mlp_norm, w_ff1=w_ff1, w_ff2=w_ff2, final_norm=final_norm)
    mom1 = dict(attn_norm=m_attn_norm, w_in=m_w_in, pool_w=m_pool_w, pool_scale=m_pool_scale, gdn_conv=m_gdn_conv,
                gdn_a_log=m_gdn_a_log, gdn_dt_bias=m_gdn_dt_bias, gdn_norm=m_gdn_norm, w_pool_up=m_w_pool_up,
                w_sb_up=m_w_sb_up, w_gdn_up=m_w_gdn_up, w_out=m_w_out, mlp_norm=m_mlp_norm, w_ff1=m_w_ff1, w_ff2=m_w_ff2,
                final_norm=m_final_norm)
    mom2 = dict(attn_norm=v_attn_norm, w_in=v_w_in, pool_w=v_pool_w, pool_scale=v_pool_scale, gdn_conv=v_gdn_conv,
                gdn_a_log=v_gdn_a_log, gdn_dt_bias=v_gdn_dt_bias, gdn_norm=v_gdn_norm, w_pool_up=v_w_pool_up,
                w_sb_up=v_w_sb_up, w_gdn_up=v_w_gdn_up, w_out=v_w_out, mlp_norm=v_mlp_norm, w_ff1=v_w_ff1, w_ff2=v_w_ff2,
                final_norm=v_final_norm)
    xi, yi, ci = lax.axis_index("x"), lax.axis_index("y"), lax.axis_index("c")
    chip = 2 * xi + yi
    where = jnp.stack([chip, ci]).astype(jnp.int32)

    bufs = [[_to_bf16_slot(f"cast_{nm}_{l}", weights[nm], l, where) for nm in BIG] for l in range(2)]
    gw = [dict(zip(BIG, _gather_weights(bufs[0]))), dict(zip(BIG, _gather_weights_async(bufs[1])))]
    conv_cols = gdn_conv.shape[-1]
    conv_place = lax.dynamic_update_slice(jnp.zeros((2, GDN_CONV, N_CHIPS * conv_cols), F32),
                                          jnp.where(ci == 0, gdn_conv, 0.0), (0, 0, chip * conv_cols))
    conv_full = _all_reduce_small("gather_conv", _rows128(conv_place)).reshape(2, GDN_CONV, N_CHIPS * conv_cols)
    w_in_al = [_w_in_from_shards(gw[l]["w_in"]) for l in range(2)]
    sp = dict(attn_norm=attn_norm.reshape(2, 1, D), pool_w=pool_w, pool_scale=pool_scale.reshape(2, 1, POOL_W),
              conv=conv_full, a_log=jnp.stack([_row128(gdn_a_log[l]) for l in range(2)]),
              dt_bias=jnp.stack([_row128(gdn_dt_bias[l]) for l in range(2)]), gdn_norm=gdn_norm.reshape(2, 1, HD),
              mlp_norm=mlp_norm.reshape(2, 1, D), final_norm=final_norm.reshape(1, D))

    loss, grad_x, grads, g_final = _local_step(x[0], loss_target[0], gw, w_in_al, sp)
    loss = lax.psum(loss[0, 0], ("x", "y", "c"))

    big_grads = {nm: [] for nm in BIG}
    for l in range(2):
        g = grads[l]
        per_layer = [_w_in_to_shards(g["w_al"])] + [g[nm] for nm in BIG[1:]]
        for nm, red in zip(BIG, _reduce_scatter(per_layer, where)):
            big_grads[nm].append(red)
    small_parts, small_shapes = [], []
    for l in range(2):
        g = grads[l]
        for nm, shp in SMALL:
            key = {"gdn_a_log": "a_log", "gdn_dt_bias": "dt_bias"}.get(nm, nm)
            val = g[key]
            small_parts.append(val[0, :HEADS] if nm in ("gdn_a_log", "gdn_dt_bias") else val)
            small_shapes.append(shp)
        small_parts.append(g["conv"])
        small_shapes.append((GDN_CONV, N_CHIPS * conv_cols))
    small_parts.append(g_final)
    small_shapes.append((D,))
    reduced = _unpack(_all_reduce_small("reduce_small", _pack(small_parts)), small_shapes)
    per = len(SMALL) + 1
    grad = {}
    for i, (nm, _) in enumerate(SMALL):
        grad[nm] = jnp.stack([reduced[i], reduced[per + i]])
    conv_g = jnp.stack([reduced[per - 1], reduced[2 * per - 1]])
    grad["gdn_conv"] = lax.dynamic_slice(conv_g, (0, 0, chip * conv_cols), (2, GDN_CONV, conv_cols))
    grad["final_norm"] = reduced[-1]

    delta, new_m, new_v = {}, {}, {}
    for nm in BIG:
        grad[nm], delta[nm], new_m[nm], new_v[nm] = _adamw_layers("adamw_" + nm, weights[nm], *big_grads[nm], mom1[nm], mom2[nm])
    small_names = [nm for nm, _ in SMALL] + ["gdn_conv", "final_norm"]
    packs = [_pack([src[nm] for nm in small_names]) for src in (weights, grad, mom1, mom2)]
    outs = _adamw("adamw_small", *packs)
    shapes = [weights[nm].shape for nm in small_names]
    for dst, packed in zip((delta, new_m, new_v), outs):
        for nm, val in zip(small_names, _unpack(packed, shapes)):
            dst[nm] = val

    order = ("attn_norm", "w_in", "pool_w", "pool_scale", "gdn_conv", "gdn_a_log", "gdn_dt_bias", "gdn_norm", "w_pool_up",
             "w_sb_up", "w_gdn_up", "w_out", "mlp_norm", "w_ff1", "w_ff2", "final_norm")
    return (loss, grad_x[None], *[grad[n] for n in order], *[delta[n] for n in order], *[new_m[n] for n in order],
            *[new_v[n] for n in order])
```

```python
import functools

import jax
import jax.numpy as jnp
from jax import lax
from jax.experimental import pallas as pl
from jax.experimental.pallas import tpu as pltpu
from jax.experimental.pallas import tpu_sc as plsc

F32, BF16 = jnp.float32, jnp.bfloat16
HIGH = lax.Precision.HIGH
MESH = pl.DeviceIdType.MESH

D = 2048
EPS = 1e-6
POOL_WINDOWS = (2, 4, 8, 16)
POOL_W, SB_W, GDN_W = 512, 768, 768
HEADS, HD = 6, 128
SB_BLOCK = 128
GDN_CHUNK = 64
D_FF = 4 * D
N_IN = 12044
N_CHIPS = 4
OFF_SB, OFF_GQKV, OFF_Z, OFF_AB, OFF_P, OFF_GATE = 0, 2304, 4608, 5376, 5632, 6144
AB_W = 256
ORIG_SB, ORIG_AB, ORIG_GATE = 512, 5888, 5900
N_AL = 12288
VMEM_LIMIT = 48 * 1024 * 1024

ADAM_LR, ADAM_B1, ADAM_B2, ADAM_EPS, ADAM_WD, ADAM_STEP = 0.001, 0.9, 0.999, 1e-08, 0.01, 10

NT = (((1,), (1,)), ((), ()))
TN = (((0,), (0,)), ((), ()))


def _cp(*sem):
    return pltpu.CompilerParams(dimension_semantics=sem, vmem_limit_bytes=VMEM_LIMIT)


def _dot(a, b, dims=None, precision=None):
    if dims is None:
        dims = (((a.ndim - 1,), (0,)), ((), ()))
    return lax.dot_general(a, b, dims, precision=precision, preferred_element_type=F32)


def _hdot(a, b, dims=None):
    return _dot(a, b, dims, precision=HIGH)


def _bdot(a, b, dims=None):
    return _dot(a.astype(BF16), b.astype(BF16), dims)


def _mm(name, a, b, *, m, n, k, tm, tn, tk, a_spec, b_spec, dims, out_shapes, out_specs,
        extras=(), extra_specs=(), epilogue=None):
    nk = k // tk
    ne, no = len(extras), len(out_shapes)

    def body(*refs):
        a_ref, b_ref = refs[0], refs[1]
        ex = refs[2:2 + ne]
        outs = refs[2 + ne:2 + ne + no]
        acc = refs[-1]
        kk = pl.program_id(2)

        @pl.when(kk == 0)
        def _():
            acc[...] = jnp.zeros_like(acc)

        acc[...] += _dot(a_ref[...].astype(BF16), b_ref[...].astype(BF16), dims)

        @pl.when(kk == nk - 1)
        def _():
            r = acc[...]
            res = epilogue(r, *[e[...] for e in ex]) if epilogue is not None else (r,)
            for o, v in zip(outs, res):
                o[...] = v.astype(o.dtype)

    return pl.pallas_call(
        body, name=name, grid=(m // tm, n // tn, nk),
        in_specs=[a_spec, b_spec, *extra_specs], out_specs=out_specs, out_shape=out_shapes,
        scratch_shapes=[pltpu.VMEM((tm, tn), F32)],
        compiler_params=_cp("parallel", "parallel", "arbitrary"),
    )(a, b, *extras)


def _a_plain(tm, tk):
    return pl.BlockSpec((tm, tk), lambda i, j, kk: (i, kk))


def _a_trans(tm, tk):
    return pl.BlockSpec((tk, tm), lambda i, j, kk: (kk, i))


def _b_plain(tk, tn):
    return pl.BlockSpec((tk, tn), lambda i, j, kk: (kk, j))


def _b_trans(tk, tn):
    return pl.BlockSpec((tn, tk), lambda i, j, kk: (j, kk))


def _o_plain(tm, tn):
    return pl.BlockSpec((tm, tn), lambda i, j, kk: (i, j))


def _o_colshard(tm, tn, ns_cols):
    per = ns_cols // tn
    return pl.BlockSpec((None, tm, tn), lambda i, j, kk: (j // per, i, j % per))


def _w_cols(tk, tn, ns):
    per = ns // tn
    return pl.BlockSpec((None, tk, tn), lambda i, j, kk: (j // per, kk, j % per))


def _w_cols_t(tk, tn, ns):
    per = ns // tk
    return pl.BlockSpec((None, tn, tk), lambda i, j, kk: (kk // per, j, kk % per))


def _w_rows(tk, tn, ks):
    per = ks // tk
    return pl.BlockSpec((None, tk, tn), lambda i, j, kk: (kk // per, kk % per, j))


def _w_rows_t(tk, tn, ks):
    per = ks // tn
    return pl.BlockSpec((None, tn, tk), lambda i, j, kk: (j // per, j % per, kk))


def _sds(shape, dtype):
    return jax.ShapeDtypeStruct(shape, dtype)


def _rms_fwd(name, x, gain):
    t = x.shape[0]
    tt = min(256, t)

    def body(x_ref, g_ref, u_ref):
        xv = x_ref[...]
        r = lax.rsqrt(jnp.mean(xv * xv, axis=-1, keepdims=True) + EPS)
        u_ref[...] = (xv * r * g_ref[...]).astype(u_ref.dtype)

    return pl.pallas_call(
        body, name=name, grid=(t // tt,),
        in_specs=[pl.BlockSpec((tt, D), lambda i: (i, 0)), pl.BlockSpec((1, D), lambda i: (0, 0))],
        out_specs=pl.BlockSpec((tt, D), lambda i: (i, 0)), out_shape=_sds((t, D), BF16),
        compiler_params=_cp("parallel"),
    )(x, gain)


def _rms_bwd(name, du, x, gain, dres):
    t = x.shape[0]
    tt = min(256, t)

    def body(du_ref, x_ref, g_ref, dres_ref, dx_ref, dg_ref):
        @pl.when(pl.program_id(0) == 0)
        def _():
            dg_ref[...] = jnp.zeros_like(dg_ref)

        xv, duv = x_ref[...], du_ref[...]
        r = lax.rsqrt(jnp.mean(xv * xv, axis=-1, keepdims=True) + EPS)
        nx = xv * r
        dn = duv * g_ref[...]
        dg_ref[...] += jnp.sum(duv * nx, axis=0, keepdims=True)
        dx_ref[...] = dres_ref[...] + r * (dn - nx * jnp.mean(dn * nx, axis=-1, keepdims=True))

    row = pl.BlockSpec((tt, D), lambda i: (i, 0))
    vec = pl.BlockSpec((1, D), lambda i: (0, 0))
    return pl.pallas_call(
        body, name=name, grid=(t // tt,), in_specs=[row, row, vec, row], out_specs=[row, vec],
        out_shape=[_sds((t, D), F32), _sds((1, D), F32)], compiler_params=_cp("arbitrary"),
    )(du, x, gain, dres)


def _loss_head(x, gain, target):
    t = x.shape[0]
    tt = min(256, t)

    def body(x_ref, g_ref, t_ref, loss_ref, dx_ref, dg_ref):
        @pl.when(pl.program_id(0) == 0)
        def _():
            dg_ref[...] = jnp.zeros_like(dg_ref)
            loss_ref[...] = jnp.zeros_like(loss_ref)

        xv = x_ref[...]
        r = lax.rsqrt(jnp.mean(xv * xv, axis=-1, keepdims=True) + EPS)
        nx = xv * r
        err = nx * g_ref[...] - t_ref[...]
        loss_ref[...] += 0.5 * jnp.sum(jnp.mean(err * err, axis=-1, keepdims=True), axis=0, keepdims=True)
        dy = err * (1.0 / D)
        dn = dy * g_ref[...]
        dg_ref[...] += jnp.sum(dy * nx, axis=0, keepdims=True)
        dx_ref[...] = r * (dn - nx * jnp.mean(dn * nx, axis=-1, keepdims=True))

    row = pl.BlockSpec((tt, D), lambda i: (i, 0))
    vec = pl.BlockSpec((1, D), lambda i: (0, 0))
    one = pl.BlockSpec((1, 1), lambda i: (0, 0))
    return pl.pallas_call(
        body, name="loss_head", grid=(t // tt,), in_specs=[row, vec, row], out_specs=[one, row, vec],
        out_shape=[_sds((1, 1), F32), _sds((t, D), F32), _sds((1, D), F32)], compiler_params=_cp("arbitrary"),
    )(x, gain, target)


def _shift_down(v, s, t_idx):
    return jnp.where(t_idx >= s, pltpu.roll(v, s, 0), 0.0)


def _shift_up(v, s, t_idx, t):
    return jnp.where(t_idx < t - s, pltpu.roll(v, t - s, 0), 0.0)


def _pool_d(p, g, t_idx):
    s = p
    for step in range(g + 1):
        s = s + _shift_down(s, 1 << step, t_idx)
    cnt = jnp.minimum(t_idx + 1, POOL_WINDOWS[g]).astype(F32)
    return s / cnt - p, cnt


def _pool_fwd(proj, pool_w, pool_scale):
    t = proj.shape[0]
    g128 = POOL_W // len(POOL_WINDOWS)

    def body(p_ref, w_ref, s_ref, y_ref):
        t_idx = lax.broadcasted_iota(jnp.int32, (t, g128), 0)
        for g in range(len(POOL_WINDOWS)):
            sl = slice(g * g128, (g + 1) * g128)
            d, _ = _pool_d(p_ref[:, sl], g, t_idx)
            y_ref[:, sl] = (_bdot(d, w_ref[g]) * s_ref[:, sl]).astype(y_ref.dtype)

    return pl.pallas_call(
        body, name="pool_fwd", grid=(1,),
        in_specs=[pl.BlockSpec((t, POOL_W), lambda i: (0, OFF_P // POOL_W)),
                  pl.BlockSpec((4, g128, g128), lambda i: (0, 0, 0)), pl.BlockSpec((1, POOL_W), lambda i: (0, 0))],
        out_specs=pl.BlockSpec((t, POOL_W), lambda i: (0, 0)), out_shape=_sds((t, POOL_W), BF16),
        compiler_params=_cp("arbitrary"),
    )(proj, pool_w, pool_scale)


def _pool_bwd(proj, pool_w, pool_scale, dy):
    t = proj.shape[0]
    g128 = POOL_W // len(POOL_WINDOWS)

    def body(p_ref, w_ref, s_ref, dy_ref, dp_ref, dw_ref, ds_ref):
        t_idx = lax.broadcasted_iota(jnp.int32, (t, g128), 0)
        for g in range(len(POOL_WINDOWS)):
            sl = slice(g * g128, (g + 1) * g128)
            d, cnt = _pool_d(p_ref[:, sl], g, t_idx)
            dyv = dy_ref[:, sl].astype(F32)
            ds_ref[:, sl] = jnp.sum(dyv * _bdot(d, w_ref[g]), axis=0, keepdims=True)
            dys = dyv * s_ref[:, sl]
            dw_ref[g] = _bdot(d, dys, TN)
            dd = _bdot(dys, w_ref[g], NT)
            s = dd / cnt
            for step in range(g + 1):
                s = s + _shift_up(s, 1 << step, t_idx, t)
            dp_ref[:, sl] = (s - dd).astype(dp_ref.dtype)

    return pl.pallas_call(
        body, name="pool_bwd", grid=(1,),
        in_specs=[pl.BlockSpec((t, POOL_W), lambda i: (0, OFF_P // POOL_W)),
                  pl.BlockSpec((4, g128, g128), lambda i: (0, 0, 0)), pl.BlockSpec((1, POOL_W), lambda i: (0, 0)),
                  pl.BlockSpec((t, POOL_W), lambda i: (0, 0))],
        out_specs=[pl.BlockSpec((t, POOL_W), lambda i: (0, 0)), pl.BlockSpec((4, g128, g128), lambda i: (0, 0, 0)),
                   pl.BlockSpec((1, POOL_W), lambda i: (0, 0))],
        out_shape=[_sds((t, POOL_W), BF16), _sds((4, g128, g128), F32), _sds((1, POOL_W), F32)],
        compiler_params=_cp("arbitrary"),
    )(proj, pool_w, pool_scale, dy)


SB_GROUP = 3
SB_GW = SB_GROUP * HD


def _sb_cast_kv(proj):
    t = proj.shape[0]
    tt = min(512, t)

    def body(x_ref, o_ref):
        o_ref[...] = x_ref[...].astype(BF16)

    return pl.pallas_call(
        body, name="sb_cast_kv", grid=(t // tt, 2),
        in_specs=[pl.BlockSpec((tt, SB_W), lambda i, j: (i, OFF_SB // SB_W + 1 + j))],
        out_specs=pl.BlockSpec((tt, SB_W), lambda i, j: (i, j)), out_shape=_sds((t, 2 * SB_W), BF16),
        compiler_params=_cp("parallel", "parallel"),
    )(proj)


def _sb_specs(t):
    q_spec = pl.BlockSpec((SB_BLOCK, SB_GW), lambda g, i: (i, OFF_SB // SB_GW + g))
    k_spec = pl.BlockSpec((t, SB_GW), lambda g, i: (0, g))
    v_spec = pl.BlockSpec((t, SB_GW), lambda g, i: (0, SB_W // SB_GW + g))
    return q_spec, k_spec, v_spec


def _head(ref, h, rows=None):
    cols = slice(h * HD, (h + 1) * HD)
    return ref[:, cols] if rows is None else ref[rows, cols]


SB_KEYS = 512


def _sub(v, b):
    return v[:, b * SB_BLOCK:(b + 1) * SB_BLOCK]


def _sb_keep(kc, limit):
    row = lax.broadcasted_iota(jnp.int32, (SB_BLOCK, kc), 0)
    col = lax.broadcasted_iota(jnp.int32, (SB_BLOCK, kc), 1)
    return col < row + limit


def _sb_chunk(q, keys, run, later, limit):
    kc = keys.shape[0]
    z = _dot(q, keys, NT)
    lsz = jax.nn.log_sigmoid(z)
    ls = lsz - z
    if limit is not None:
        keep = _sb_keep(kc, limit)
        ls = jnp.where(keep, ls, 0.0)
    parts = [None] * (kc // SB_BLOCK)
    for b in reversed(range(kc // SB_BLOCK)):
        parts[b] = _hdot(_sub(ls, b), later) + run
        run = run + jnp.sum(_sub(ls, b), axis=1, keepdims=True)
    a = jnp.exp(lsz + jnp.concatenate(parts, axis=1))
    if limit is not None:
        a = jnp.where(keep, a, 0.0)
    return z, a, run


def _sb_fwd(proj, kv):
    t = proj.shape[0]
    kc = min(SB_KEYS, t)
    scale = HD ** -0.5

    def body(q_ref, k_ref, v_ref, o_ref):
        i = pl.program_id(1)
        top = (i * SB_BLOCK) // kc
        qs = [(_head(q_ref, h) * scale).astype(BF16) for h in range(SB_GROUP)]
        row = lax.broadcasted_iota(jnp.int32, (SB_BLOCK, SB_BLOCK), 0)
        col = lax.broadcasted_iota(jnp.int32, (SB_BLOCK, SB_BLOCK), 1)
        later = (row > col).astype(F32)

        def chunk(jc, carry, masked):
            rows = pl.ds(pl.multiple_of(jc * kc, kc), kc)
            limit = i * SB_BLOCK - jc * kc if masked else None
            out = []
            for h in range(SB_GROUP):
                acc, run = carry[h]
                _, a, run = _sb_chunk(qs[h], _head(k_ref, h, rows), run, later, limit)
                out.append((acc + _dot(a.astype(BF16), _head(v_ref, h, rows)), run))
            return tuple(out)

        zero = tuple((jnp.zeros((SB_BLOCK, HD), F32), jnp.zeros((SB_BLOCK, 1), F32)) for _ in range(SB_GROUP))
        carry = chunk(top, zero, True)
        carry = lax.fori_loop(0, top, lambda jj, c: chunk(top - 1 - jj, c, False), carry)
        for h in range(SB_GROUP):
            o_ref[:, h * HD:(h + 1) * HD] = carry[h][0].astype(o_ref.dtype)

    return pl.pallas_call(
        body, name="sb_fwd", grid=(HEADS // SB_GROUP, t // SB_BLOCK), in_specs=list(_sb_specs(t)),
        out_specs=pl.BlockSpec((SB_BLOCK, SB_GW), lambda g, i: (i, g)), out_shape=_sds((t, SB_W), BF16),
        compiler_params=_cp("parallel", "arbitrary"),
    )(proj, kv, kv)


def _sb_bwd(proj, kv, dy):
    t = proj.shape[0]
    nq = t // SB_BLOCK
    kc = min(SB_KEYS, t)
    scale = HD ** -0.5

    def body(q_ref, k_ref, v_ref, do_ref, dq_ref, dk_ref, dv_ref, z_scr, e_scr):
        i = pl.program_id(1)
        top = (i * SB_BLOCK) // kc

        @pl.when(i == 0)
        def _():
            dk_ref[...] = jnp.zeros_like(dk_ref)
            dv_ref[...] = jnp.zeros_like(dv_ref)

        qs = [(_head(q_ref, h) * scale).astype(BF16) for h in range(SB_GROUP)]
        dos = [_head(do_ref, h).astype(BF16) for h in range(SB_GROUP)]
        row = lax.broadcasted_iota(jnp.int32, (SB_BLOCK, SB_BLOCK), 0)
        col = lax.broadcasted_iota(jnp.int32, (SB_BLOCK, SB_BLOCK), 1)
        later = (row > col).astype(F32)
        earlier = (row < col).astype(F32)

        def down(jc, runs, masked):
            rows = pl.ds(pl.multiple_of(jc * kc, kc), kc)
            limit = i * SB_BLOCK - jc * kc if masked else None
            out = []
            for h in range(SB_GROUP):
                z, a, run = _sb_chunk(qs[h], _head(k_ref, h, rows), runs[h], later, limit)
                z_scr[h, jc] = z
                e_scr[h, jc] = a * _dot(dos[h], _head(v_ref, h, rows), NT)
                dv_ref[rows, h * HD:(h + 1) * HD] += _dot(a.astype(BF16), dos[h], TN)
                out.append(run)
            return tuple(out)

        zero = tuple(jnp.zeros((SB_BLOCK, 1), F32) for _ in range(SB_GROUP))
        runs = down(top, zero, True)
        lax.fori_loop(0, top, lambda jj, r: down(top - 1 - jj, r, False), runs)

        def up(jc, carry, masked):
            rows = pl.ds(pl.multiple_of(jc * kc, kc), kc)
            out = []
            for h in range(SB_GROUP):
                dq, run = carry[h]
                z, e = z_scr[h, jc], e_scr[h, jc]
                parts = []
                for b in range(kc // SB_BLOCK):
                    parts.append(_hdot(_sub(e, b), earlier) + run)
                    run = run + jnp.sum(_sub(e, b), axis=1, keepdims=True)
                sz = jax.nn.sigmoid(z)
                dz = e * (1.0 - sz) - jnp.concatenate(parts, axis=1) * sz
                if masked:
                    dz = jnp.where(_sb_keep(kc, i * SB_BLOCK - jc * kc), dz, 0.0)
                dz = dz.astype(BF16)
                dk_ref[rows, h * HD:(h + 1) * HD] += _dot(dz, qs[h], TN)
                out.append((dq + _dot(dz, _head(k_ref, h, rows)), run))
            return tuple(out)

        zero = tuple((jnp.zeros((SB_BLOCK, HD), F32), jnp.zeros((SB_BLOCK, 1), F32)) for _ in range(SB_GROUP))
        carry = lax.fori_loop(0, top, lambda jc, c: up(jc, c, False), zero)
        carry = up(top, carry, True)
        for h in range(SB_GROUP):
            dq_ref[:, h * HD:(h + 1) * HD] = (carry[h][0] * scale).astype(dq_ref.dtype)

    blk = pl.BlockSpec((SB_BLOCK, SB_GW), lambda g, i: (i, g))
    seq = pl.BlockSpec((t, SB_GW), lambda g, i: (0, g))
    scratch = pltpu.VMEM((SB_GROUP, t // kc, SB_BLOCK, kc), F32)
    return pl.pallas_call(
        body, name="sb_bwd", grid=(HEADS // SB_GROUP, nq), in_specs=[*_sb_specs(t), blk], out_specs=[blk, seq, seq],
        out_shape=[_sds((t, SB_W), BF16), _sds((t, SB_W), F32), _sds((t, SB_W), F32)],
        scratch_shapes=[scratch, scratch], compiler_params=_cp("parallel", "arbitrary"),
    )(proj, kv, kv, dy)


CONV_TILE = 256
GDN_CONV = 4


def _conv_pre(x, w_ref, t_idx):
    pre = w_ref[GDN_CONV - 1:GDN_CONV, :] * x
    for s in range(1, GDN_CONV):
        pre = pre + w_ref[GDN_CONV - 1 - s:GDN_CONV - s, :] * _shift_down(x, s, t_idx)
    return pre


def _conv_fwd(proj, conv_w):
    t = proj.shape[0]
    width = conv_w.shape[1]

    def body(x_ref, w_ref, y_ref):
        t_idx = lax.broadcasted_iota(jnp.int32, (t, CONV_TILE), 0)
        pre = _conv_pre(x_ref[...], w_ref, t_idx)
        y_ref[...] = pre * jax.nn.sigmoid(pre)

    return pl.pallas_call(
        body, name="conv_fwd", grid=(width // CONV_TILE,),
        in_specs=[pl.BlockSpec((t, CONV_TILE), lambda c: (0, OFF_GQKV // CONV_TILE + c)),
                  pl.BlockSpec((GDN_CONV, CONV_TILE), lambda c: (0, c))],
        out_specs=pl.BlockSpec((t, CONV_TILE), lambda c: (0, c)), out_shape=_sds((t, width), F32),
        compiler_params=_cp("parallel"),
    )(proj, conv_w)


def _conv_bwd(proj, conv_w, dc):
    t = proj.shape[0]
    width = dc.shape[1]
    per = width // CONV_TILE
    part = 0

    def body(x_ref, w_ref, dc_ref, dx_ref, dw_ref):
        t_idx = lax.broadcasted_iota(jnp.int32, (t, CONV_TILE), 0)
        x = x_ref[...]
        pre = _conv_pre(x, w_ref, t_idx)
        sg = jax.nn.sigmoid(pre)
        dpre = dc_ref[...] * (sg * (1.0 + pre * (1.0 - sg)))
        dx = w_ref[GDN_CONV - 1:GDN_CONV, :] * dpre
        dw_ref[GDN_CONV - 1:GDN_CONV, :] = jnp.sum(dpre * x, axis=0, keepdims=True)
        for s in range(1, GDN_CONV):
            dx = dx + w_ref[GDN_CONV - 1 - s:GDN_CONV - s, :] * _shift_up(dpre, s, t_idx, t)
            dw_ref[GDN_CONV - 1 - s:GDN_CONV - s, :] = jnp.sum(dpre * _shift_down(x, s, t_idx), axis=0, keepdims=True)
        dx_ref[...] = dx.astype(dx_ref.dtype)

    return pl.pallas_call(
        body, name="conv_bwd", grid=(per,),
        in_specs=[pl.BlockSpec((t, CONV_TILE), lambda c: (0, OFF_GQKV // CONV_TILE + part * per + c)),
                  pl.BlockSpec((GDN_CONV, CONV_TILE), lambda c: (0, part * per + c)),
                  pl.BlockSpec((t, CONV_TILE), lambda c: (0, c))],
        out_specs=[pl.BlockSpec((t, CONV_TILE), lambda c: (0, c)), pl.BlockSpec((GDN_CONV, CONV_TILE), lambda c: (0, c))],
        out_shape=[_sds((t, width), BF16), _sds((GDN_CONV, width), F32)],
        compiler_params=_cp("parallel"),
    )(proj, conv_w, dc)


def _gdn_prep(cq, ck, cv, ab, alog_row, dtb_row, h):
    c = GDN_CHUNK
    row = lax.broadcasted_iota(jnp.int32, (c, c), 0)
    col = lax.broadcasted_iota(jnp.int32, (c, c), 1)
    incl, strict, eye = row >= col, row > col, row == col
    a_col, b_col = ab[:, h:h + 1], ab[:, HEADS + h:HEADS + h + 1]
    a_log, dt_bias = alog_row[:, h:h + 1], dtb_row[:, h:h + 1]
    qn = cq * lax.rsqrt(jnp.sum(cq * cq, axis=-1, keepdims=True) + EPS) * (HD ** -0.5)
    kn = ck * lax.rsqrt(jnp.sum(ck * ck, axis=-1, keepdims=True) + EPS)
    la_col = -jnp.exp(a_log) * jax.nn.softplus(a_col + dt_bias)
    beta = jax.nn.sigmoid(b_col)
    la_row = jnp.sum(jnp.where(eye, la_col, 0.0), axis=0, keepdims=True)
    g_col = jnp.sum(jnp.where(incl, la_row, 0.0), axis=1, keepdims=True)
    g_row = jnp.sum(jnp.where(row <= col, la_col, 0.0), axis=0, keepdims=True)
    g_last = jnp.sum(la_col, axis=0, keepdims=True)
    gamma = jnp.where(incl, jnp.exp(jnp.where(incl, g_col - g_row, 0.0)), 0.0)
    lower = jnp.where(strict, beta * _hdot(kn, kn, NT) * gamma, 0.0)
    inv = jnp.where(eye, 1.0, 0.0) - lower
    pw = _hdot(lower, lower)
    for step in range(5):
        inv = inv + _hdot(inv, pw)
        if step < 4:
            pw = _hdot(pw, pw)
    u = _hdot(inv, cv * beta)
    w = _hdot(inv, kn * (beta * jnp.exp(g_col)))
    qk = _hdot(qn, kn, NT) * gamma
    return u, w, qk, qn * jnp.exp(g_col), kn * jnp.exp(g_last - g_col), jnp.exp(g_last)


def _gdn_post(o, z, gain):
    y = o * lax.rsqrt(jnp.mean(o * o, axis=-1, keepdims=True) + EPS) * gain
    return y * (z * jax.nn.sigmoid(z))


def _gdn_specs(nc, reverse):
    c = GDN_CHUNK

    def ch(n):
        return nc - 1 - n if reverse else n

    def wide(array_off):
        return pl.BlockSpec((c, GDN_W), lambda n: (ch(n), array_off // GDN_W))

    ab = pl.BlockSpec((c, HD), lambda n: (ch(n), OFF_AB // HD))
    row = pl.BlockSpec((1, HD), lambda n: (0, 0))
    state = pl.BlockSpec((None, HEADS, HD, HD), lambda n: (ch(n), 0, 0, 0))
    return wide, ab, row, state


def _gdn_fwd(cqkv, proj, a_log, dt_bias, gain):
    t = proj.shape[0]
    nc = t // GDN_CHUNK
    wide, ab, row, state = _gdn_specs(nc, False)

    def body(cq_ref, ck_ref, cv_ref, ab_ref, z_ref, al_ref, dt_ref, g_ref, y_ref, sprev_ref, s_scr):
        @pl.when(pl.program_id(0) == 0)
        def _():
            s_scr[...] = jnp.zeros_like(s_scr)

        for h in range(HEADS):
            u, w, qk, qd, kd, dec = _gdn_prep(_head(cq_ref, h), _head(ck_ref, h), _head(cv_ref, h), ab_ref[...],
                                              al_ref[...], dt_ref[...], h)
            s = s_scr[h]
            sprev_ref[h] = s
            v_new = u - _hdot(w, s)
            o = _hdot(qd, s) + _hdot(qk, v_new)
            s_scr[h] = s * dec + _hdot(kd, v_new, TN)
            y_ref[:, h * HD:(h + 1) * HD] = _gdn_post(o, _head(z_ref, h), g_ref[...]).astype(y_ref.dtype)

    return pl.pallas_call(
        body, name="gdn_fwd", grid=(nc,),
        in_specs=[wide(0), wide(GDN_W), wide(2 * GDN_W), ab, wide(OFF_Z), row, row, row],
        out_specs=[wide(0), state], out_shape=[_sds((t, GDN_W), BF16), _sds((nc, HEADS, HD, HD), F32)],
        scratch_shapes=[pltpu.VMEM((HEADS, HD, HD), F32)], compiler_params=_cp("arbitrary"),
    )(cqkv, cqkv, cqkv, proj, proj, a_log, dt_bias, gain)


def _gdn_bwd(cqkv, proj, a_log, dt_bias, gain, sprev, dy):
    t = proj.shape[0]
    nc = t // GDN_CHUNK
    wide, ab, row, state = _gdn_specs(nc, True)

    def body(cq_ref, ck_ref, cv_ref, ab_ref, z_ref, al_ref, dt_ref, g_ref, sp_ref, dy_ref,
             dc_ref, dab_ref, dz_ref, dal_ref, ddt_ref, dg_ref, ds_scr):
        @pl.when(pl.program_id(0) == 0)
        def _():
            ds_scr[...] = jnp.zeros_like(ds_scr)
            dal_ref[...] = jnp.zeros_like(dal_ref)
            ddt_ref[...] = jnp.zeros_like(ddt_ref)
            dg_ref[...] = jnp.zeros_like(dg_ref)

        dab_sum = jnp.zeros(dab_ref.shape, F32)
        for h in range(HEADS):
            (u, w, qk, qd, kd, dec), prep_vjp = jax.vjp(
                functools.partial(_gdn_prep, h=h),
                _head(cq_ref, h), _head(ck_ref, h), _head(cv_ref, h), ab_ref[...], al_ref[...], dt_ref[...])
            s = sp_ref[h]
            v_new = u - _hdot(w, s)
            o = _hdot(qd, s) + _hdot(qk, v_new)
            _, post_vjp = jax.vjp(_gdn_post, o, _head(z_ref, h), g_ref[...])
            do, dz, dgain = post_vjp(_head(dy_ref, h).astype(F32))
            ds_next = ds_scr[h]
            d_vnew = _hdot(qk, do, TN) + _hdot(kd, ds_next)
            d_qk = _hdot(do, v_new, NT)
            d_qd = _hdot(do, s, NT)
            d_kd = _hdot(v_new, ds_next, NT)
            d_dec = jnp.sum(jnp.sum(s * ds_next, axis=1, keepdims=True), axis=0, keepdims=True)
            ds_scr[h] = dec * ds_next + _hdot(qd, do, TN) - _hdot(w, d_vnew, TN)
            d_w = -_hdot(d_vnew, s, NT)
            dcq, dck, dcv, dab, dal, ddt = prep_vjp((d_vnew, d_w, d_qk, d_qd, d_kd, d_dec))
            dc_ref[:, h * HD:(h + 1) * HD] = dcq
            dc_ref[:, GDN_W + h * HD:GDN_W + (h + 1) * HD] = dck
            dc_ref[:, 2 * GDN_W + h * HD:2 * GDN_W + (h + 1) * HD] = dcv
            dz_ref[:, h * HD:(h + 1) * HD] = dz.astype(dz_ref.dtype)
            dab_sum = dab_sum + dab
            dal_ref[...] += dal
            ddt_ref[...] += ddt
            dg_ref[...] += dgain
        dab_ref[...] = dab_sum

    c = GDN_CHUNK
    return pl.pallas_call(
        body, name="gdn_bwd", grid=(nc,),
        in_specs=[wide(0), wide(GDN_W), wide(2 * GDN_W), ab, wide(OFF_Z), row, row, row, state, wide(0)],
        out_specs=[pl.BlockSpec((c, 3 * GDN_W), lambda n: (nc - 1 - n, 0)), pl.BlockSpec((c, HD), lambda n: (nc - 1 - n, 0)),
                   wide(0), row, row, row],
        out_shape=[_sds((t, 3 * GDN_W), F32), _sds((t, HD), F32), _sds((t, GDN_W), BF16),
                   _sds((1, HD), F32), _sds((1, HD), F32), _sds((1, HD), F32)],
        scratch_shapes=[pltpu.VMEM((HEADS, HD, HD), F32)], compiler_params=_cp("arbitrary"),
    )(cqkv, cqkv, cqkv, proj, proj, a_log, dt_bias, gain, sprev, dy)


MERGE_TN = 512


def _merge_specs(t, tm):
    tn = MERGE_TN
    ys = [pl.BlockSpec((tm, wd), lambda i, j: (i, 0)) for wd in (POOL_W, SB_W, GDN_W)]
    ws = [pl.BlockSpec((None, wd, tn), lambda i, j: (j, 0, 0)) for wd in (POOL_W, SB_W, GDN_W)]
    gs = [pl.BlockSpec((tm, tn), functools.partial(lambda i, j, b: (i, OFF_GATE // tn + b * (D // tn) + j), b=b))
          for b in range(3)]
    out = pl.BlockSpec((tm, tn), lambda i, j: (i, j))
    return ys, ws, gs, out


def _merge_fwd(ys, wups, proj):
    t = proj.shape[0]
    tm = min(512, t)
    y_specs, w_specs, g_specs, out = _merge_specs(t, tm)

    def body(y0, y1, y2, w0, w1, w2, g0, g1, g2, o_ref):
        acc = jnp.zeros(o_ref.shape, F32)
        for y, w, g in ((y0, w0, g0), (y1, w1, g1), (y2, w2, g2)):
            acc = acc + jax.nn.sigmoid(g[...]) * _dot(y[...], w[...])
        o_ref[...] = acc.astype(o_ref.dtype)

    return pl.pallas_call(
        body, name="merge_fwd", grid=(t // tm, D // MERGE_TN), in_specs=[*y_specs, *w_specs, *g_specs],
        out_specs=out, out_shape=_sds((t, D), BF16), compiler_params=_cp("parallel", "parallel"),
    )(*ys, *wups, proj, proj, proj)


def _merge_bwd(ys, wups, proj, dmerged):
    t = proj.shape[0]
    tm = min(512, t)
    y_specs, w_specs, g_specs, out = _merge_specs(t, tm)

    def body(y0, y1, y2, w0, w1, w2, g0, g1, g2, dm_ref, dg0, dg1, dg2, dm0, dm1, dm2):
        dm = dm_ref[...].astype(F32)
        for y, w, g, dg, dmb in ((y0, w0, g0, dg0, dm0), (y1, w1, g1, dg1, dm1), (y2, w2, g2, dg2, dm2)):
            sg = jax.nn.sigmoid(g[...])
            dg[...] = (dm * _dot(y[...], w[...]) * sg * (1.0 - sg)).astype(dg.dtype)
            dmb[...] = (dm * sg).astype(dmb.dtype)

    return pl.pallas_call(
        body, name="merge_bwd", grid=(t // tm, D // MERGE_TN), in_specs=[*y_specs, *w_specs, *g_specs, out],
        out_specs=[out] * 6, out_shape=[_sds((t, D), BF16)] * 6, compiler_params=_cp("parallel", "parallel"),
    )(*ys, *wups, proj, proj, proj, dmerged)


def _tile(t, want):
    return min(t, want)


def _layer_fwd(x, l, gw, w_al, sp):
    t = x.shape[0]
    tm = _tile(t, 1024)
    u = _rms_fwd("rms_attn", x, sp["attn_norm"][l])
    proj = _mm("proj", u, w_al, m=t, n=N_AL, k=D, tm=tm, tn=1024, tk=512, a_spec=_a_plain(tm, 512),
               b_spec=_b_plain(512, 1024), dims=None, out_shapes=[_sds((t, N_AL), F32)], out_specs=[_o_plain(tm, 1024)])[0]
    y_pool = _pool_fwd(proj, sp["pool_w"][l], sp["pool_scale"][l])
    kv = _sb_cast_kv(proj)
    y_sb = _sb_fwd(proj, kv)
    cqkv = _conv_fwd(proj, sp["conv"][l])
    y_gdn, sprev = _gdn_fwd(cqkv, proj, sp["a_log"][l], sp["dt_bias"][l], sp["gdn_norm"][l])
    ys = (y_pool, y_sb, y_gdn)
    wups = (gw["w_pool_up"], gw["w_sb_up"], gw["w_gdn_up"])
    merged = _merge_fwd(ys, wups, proj)
    x1 = _mm("out_proj", merged, gw["w_out"], m=t, n=D, k=D, tm=tm, tn=1024, tk=512, a_spec=_a_plain(tm, 512),
             b_spec=_w_rows(512, 1024, 512), dims=None, out_shapes=[_sds((t, D), F32)], out_specs=[_o_plain(tm, 1024)],
             extras=[x], extra_specs=[_o_plain(tm, 1024)], epilogue=lambda r, xr: (r + xr,))[0]
    u2 = _rms_fwd("rms_mlp", x1, sp["mlp_norm"][l])

    def relu2(r):
        hv = jnp.maximum(r, 0.0)
        return hv, hv * hv

    hid, hid2 = _mm("ff1", u2, gw["w_ff1"], m=t, n=D_FF, k=D, tm=tm, tn=1024, tk=512, a_spec=_a_plain(tm, 512),
                    b_spec=_w_cols(512, 1024, 2048), dims=None, out_shapes=[_sds((t, D_FF), BF16)] * 2,
                    out_specs=[_o_plain(tm, 1024)] * 2, epilogue=relu2)
    x2 = _mm("ff2", hid2, gw["w_ff2"], m=t, n=D, k=D_FF, tm=tm, tn=1024, tk=512, a_spec=_a_plain(tm, 512),
             b_spec=_w_rows(512, 1024, 2048), dims=None, out_shapes=[_sds((t, D), F32)], out_specs=[_o_plain(tm, 1024)],
             extras=[x1], extra_specs=[_o_plain(tm, 1024)], epilogue=lambda r, xr: (r + xr,))[0]
    saved = dict(x=x, u=u, proj=proj, kv=kv, cqkv=cqkv, sprev=sprev, ys=ys, merged=merged, x1=x1, u2=u2, hid=hid, hid2=hid2)
    return x2, saved


def _layer_bwd(dx2, l, gw, w_al, sp, sv):
    t = dx2.shape[0]
    tm = _tile(t, 1024)
    tk = _tile(t, 512)
    g = {}
    dpre = _mm("ff2_dx", dx2, gw["w_ff2"], m=t, n=D_FF, k=D, tm=tm, tn=1024, tk=512, a_spec=_a_plain(tm, 512),
               b_spec=_w_rows_t(512, 1024, 2048), dims=NT, out_shapes=[_sds((t, D_FF), BF16)],
               out_specs=[_o_plain(tm, 1024)], extras=[sv["hid"]], extra_specs=[_o_plain(tm, 1024)],
               epilogue=lambda r, hv: (r * (2.0 * hv.astype(F32)),))[0]
    g["w_ff2"] = _mm("ff2_dw", sv["hid2"], dx2, m=D_FF, n=D, k=t, tm=1024, tn=1024, tk=tk, a_spec=_a_trans(1024, tk),
                     b_spec=_b_plain(tk, 1024), dims=TN, out_shapes=[_sds((D_FF, D), BF16)],
                     out_specs=[_o_plain(1024, 1024)])[0].reshape(N_CHIPS, D_FF // N_CHIPS, D)
    du2 = _mm("ff1_dx", dpre, gw["w_ff1"], m=t, n=D, k=D_FF, tm=tm, tn=1024, tk=512, a_spec=_a_plain(tm, 512),
              b_spec=_w_cols_t(512, 1024, 2048), dims=NT, out_shapes=[_sds((t, D), F32)], out_specs=[_o_plain(tm, 1024)])[0]
    g["w_ff1"] = _mm("ff1_dw", sv["u2"], dpre, m=D, n=D_FF, k=t, tm=1024, tn=1024, tk=tk, a_spec=_a_trans(1024, tk),
                     b_spec=_b_plain(tk, 1024), dims=TN, out_shapes=[_sds((N_CHIPS, D, D_FF // N_CHIPS), BF16)],
                     out_specs=[_o_colshard(1024, 1024, D_FF // N_CHIPS)])[0]
    dx1, g["mlp_norm"] = _rms_bwd("rms_mlp_bwd", du2, sv["x1"], sp["mlp_norm"][l], dx2)
    dmerged = _mm("out_dx", dx1, gw["w_out"], m=t, n=D, k=D, tm=tm, tn=512, tk=1024, a_spec=_a_plain(tm, 1024),
                  b_spec=_w_rows_t(1024, 512, 512), dims=NT, out_shapes=[_sds((t, D), BF16)], out_specs=[_o_plain(tm, 512)])[0]
    g["w_out"] = _mm("out_dw", sv["merged"], dx1, m=D, n=D, k=t, tm=1024, tn=1024, tk=tk, a_spec=_a_trans(1024, tk),
                     b_spec=_b_plain(tk, 1024), dims=TN, out_shapes=[_sds((D, D), BF16)],
                     out_specs=[_o_plain(1024, 1024)])[0].reshape(N_CHIPS, D // N_CHIPS, D)
    wups = (gw["w_pool_up"], gw["w_sb_up"], gw["w_gdn_up"])
    dg0, dg1, dg2, dm0, dm1, dm2 = _merge_bwd(sv["ys"], wups, sv["proj"], dmerged)
    dys = []
    for nm, yb, dmb, wd in zip(("w_pool_up", "w_sb_up", "w_gdn_up"), sv["ys"], (dm0, dm1, dm2), (POOL_W, SB_W, GDN_W)):
        dys.append(_mm(nm + "_dx", dmb, gw[nm], m=t, n=wd, k=D, tm=tm, tn=256, tk=512, a_spec=_a_plain(tm, 512),
                       b_spec=_w_cols_t(512, 256, 512), dims=NT, out_shapes=[_sds((t, wd), F32)],
                       out_specs=[_o_plain(tm, 256)])[0])
        g[nm] = _mm(nm + "_dw", yb, dmb, m=wd, n=D, k=t, tm=256, tn=512, tk=tk, a_spec=_a_trans(256, tk),
                    b_spec=_b_plain(tk, 512), dims=TN, out_shapes=[_sds((N_CHIPS, wd, D // N_CHIPS), BF16)],
                    out_specs=[_o_colshard(256, 512, D // N_CHIPS)])[0]
    proj = sv["proj"]
    dp, g["pool_w"], g["pool_scale"] = _pool_bwd(proj, sp["pool_w"][l], sp["pool_scale"][l], dys[0])
    dsq, dsk, dsv = _sb_bwd(proj, sv["kv"], dys[1])
    dc, dab, dz, g["a_log"], g["dt_bias"], g["gdn_norm"] = _gdn_bwd(
        sv["cqkv"], proj, sp["a_log"][l], sp["dt_bias"][l], sp["gdn_norm"][l], sv["sprev"], dys[2])
    dgx, g["conv"] = _conv_bwd(proj, sp["conv"][l], dc)
    dproj = jnp.concatenate(
        [dsq, dsk.astype(BF16), dsv.astype(BF16), dgx, dz, dab.astype(BF16), jnp.zeros((t, AB_W - HD), BF16),
         dp, dg0, dg1, dg2], axis=1)
    du = _mm("proj_dx", dproj, w_al, m=t, n=D, k=N_AL, tm=tm, tn=1024, tk=512, a_spec=_a_plain(tm, 512),
             b_spec=_b_trans(512, 1024), dims=NT, out_shapes=[_sds((t, D), F32)], out_specs=[_o_plain(tm, 1024)])[0]
    g["w_al"] = _mm("proj_dw", sv["u"], dproj, m=D, n=N_AL, k=t, tm=1024, tn=1024, tk=tk, a_spec=_a_trans(1024, tk),
                    b_spec=_b_plain(tk, 1024), dims=TN, out_shapes=[_sds((D, N_AL), BF16)], out_specs=[_o_plain(1024, 1024)])[0]
    dx, g["attn_norm"] = _rms_bwd("rms_attn_bwd", du, sv["x"], sp["attn_norm"][l], dx1)
    return dx, g


def _align_w_in(w):
    n_ab = ORIG_GATE - ORIG_AB
    return jnp.concatenate([w[:, ORIG_SB:ORIG_GATE], jnp.zeros((D, AB_W - n_ab), w.dtype), w[:, :ORIG_SB], w[:, ORIG_GATE:]],
                           axis=1)


def _unalign_w_in(w):
    n_ab = ORIG_GATE - ORIG_AB
    return jnp.concatenate([w[:, OFF_P:OFF_GATE], w[:, :OFF_AB + n_ab], w[:, OFF_GATE:]], axis=1)


W_IN_RUNS = ((0, ORIG_SB, OFF_P), (ORIG_SB, ORIG_GATE, OFF_SB), (ORIG_GATE, N_IN, OFF_GATE))
W_IN_SHARD = N_IN // N_CHIPS


def _w_in_from_shards(gathered):
    parts = []
    for lo, hi, al in sorted(W_IN_RUNS, key=lambda r: r[2]):
        if al == OFF_P:
            parts.append(jnp.zeros((D, OFF_P - (OFF_AB + ORIG_GATE - ORIG_AB)), gathered.dtype))
        while lo < hi:
            chip = lo // W_IN_SHARD
            end = min(hi, (chip + 1) * W_IN_SHARD)
            parts.append(gathered[chip, :, lo - chip * W_IN_SHARD:end - chip * W_IN_SHARD])
            lo = end
    return jnp.concatenate(parts, axis=1)


def _w_in_to_shards(g_al):
    shards = []
    for chip in range(N_CHIPS):
        a, b = chip * W_IN_SHARD, (chip + 1) * W_IN_SHARD
        parts = [g_al[:, al + max(a, lo) - lo:al + min(b, hi) - lo] for lo, hi, al in W_IN_RUNS if max(a, lo) < min(b, hi)]
        shards.append(jnp.concatenate(parts, axis=1))
    return jnp.stack(shards)


def _row128(v):
    return jnp.pad(v.reshape(1, -1), ((0, 0), (0, HD - v.shape[-1])))


def _local_step(x, target, gw, w_in_al, sp, on_layer_grads=None):
    saved = []
    h = x
    for l in range(2):
        h, sv = _layer_fwd(h, l, gw[l], w_in_al[l], sp)
        saved.append(sv)
    loss, dh, g_final = _loss_head(h, sp["final_norm"], target)
    grads = [None, None]
    for l in (1, 0):
        dh, grads[l] = _layer_bwd(dh, l, gw[l], w_in_al[l], sp, saved[l])
        if on_layer_grads is not None:
            dh = on_layer_grads(l, dh, grads[l])
    return loss, dh, grads, g_final


ANY = pl.BlockSpec(memory_space=pl.ANY)


def _me():
    return lax.axis_index("x"), lax.axis_index("y"), lax.axis_index("c")


def _other_chips(x, y):
    return [(1 - x, y), (x, 1 - y), (1 - x, 1 - y)]


def _half(ref, axis, c, rows):
    half = rows // 2
    idx = [slice(None)] * axis + [pl.ds(pl.multiple_of(c * half, 16), half)]
    return ref.at[tuple(idx)]


def _gather_steps(out, send, recv):
    n = len(out)
    x, y, c = _me()
    mine = 2 * x + y
    sibling = (x, y, 1 - c)
    chips = _other_chips(x, y)
    sends = []
    for t in range(n):
        rows = out[t].shape[1]
        for k, (px, py) in enumerate(chips):
            own_half = _half(out[t].at[mine], 0, c, rows)
            cp = pltpu.make_async_remote_copy(
                src_ref=own_half, dst_ref=own_half,
                send_sem=send.at[6 * t + k], recv_sem=recv.at[6 * t + k], device_id=(px, py, c), device_id_type=MESH)
            cp.start()
            sends.append(cp)
    for t in range(n):
        rows = out[t].shape[1]
        for k, (px, py) in enumerate(chips):
            landed = _half(out[t].at[2 * px + py], 0, c, rows)
            pltpu.make_async_remote_copy(
                src_ref=landed, dst_ref=landed, send_sem=send.at[6 * t + k], recv_sem=recv.at[6 * t + k],
                device_id=(px, py, c), device_id_type=MESH).wait_recv()
            cp = pltpu.make_async_remote_copy(
                src_ref=landed, dst_ref=landed, send_sem=send.at[6 * t + 3 + k], recv_sem=recv.at[6 * t + 3 + k],
                device_id=sibling, device_id_type=MESH)
            cp.start()
            sends.append(cp)
    for t in range(n):
        rows = out[t].shape[1]
        for k, (px, py) in enumerate(chips):
            other = _half(out[t].at[2 * px + py], 0, 1 - c, rows)
            pltpu.make_async_remote_copy(
                src_ref=other, dst_ref=other, send_sem=send.at[6 * t + 3 + k], recv_sem=recv.at[6 * t + 3 + k],
                device_id=sibling, device_id_type=MESH).wait_recv()
    for cp in sends:
        cp.wait_send()


def _gather_weights(bufs):
    n = len(bufs)

    def body(*refs):
        _gather_steps(refs[n:2 * n], *refs[2 * n:])

    return pl.pallas_call(
        body, name="gather_weights", in_specs=[ANY] * n, out_specs=[ANY] * n,
        out_shape=[_sds(s.shape, s.dtype) for s in bufs], input_output_aliases={t: t for t in range(n)},
        scratch_shapes=[pltpu.SemaphoreType.DMA((6 * n,)), pltpu.SemaphoreType.DMA((6 * n,))],
    )(*bufs)


GATHER_ASYNC_ID = 1


def _gather_weights_async(bufs):
    n = len(bufs)
    refs = [jax.new_ref(b, memory_space=pltpu.MemorySpace.HBM) for b in bufs]

    @pl.kernel(mesh=plsc.ScalarSubcoreMesh(axis_name="sequencer", num_cores=1), name="gather_weights_async",
               scratch_types=(pltpu.SemaphoreType.DMA((6 * n,)), pltpu.SemaphoreType.DMA((6 * n,))),
               compiler_params=pltpu.CompilerParams(collective_id=GATHER_ASYNC_ID))
    def launch(send, recv):
        x, y, c = _me()
        barrier = pltpu.get_barrier_semaphore()
        peers = [(x, y, 1 - c)] + [(px, py, c) for px, py in _other_chips(x, y)]
        for peer in peers:
            pl.semaphore_signal(barrier, inc=1, device_id=peer, device_id_type=MESH)
        pl.semaphore_wait(barrier, len(peers))
        _gather_steps(refs, send, recv)

    launch()
    return [r[...] for r in refs]


def _rs_pair(grads):
    n = len(grads)

    def body(*refs):
        g, out = refs[:n], refs[n:2 * n]
        send, recv = refs[2 * n:]
        x, y, c = _me()
        copies = []
        for t in range(n):
            cp = pltpu.make_async_remote_copy(
                src_ref=_half(g[t], 1, 1 - c, g[t].shape[1]), dst_ref=out[t], send_sem=send.at[t], recv_sem=recv.at[t],
                device_id=(x, y, 1 - c), device_id_type=MESH)
            cp.start()
            copies.append(cp)
        for cp in copies:
            cp.wait()

    return pl.pallas_call(
        body, name="rs_pair", in_specs=[ANY] * n, out_specs=[ANY] * n,
        out_shape=[_sds((N_CHIPS, s.shape[1] // 2, s.shape[2]), s.dtype) for s in grads],
        scratch_shapes=[pltpu.SemaphoreType.DMA((n,)), pltpu.SemaphoreType.DMA((n,))],
    )(*grads)


def _rs_chips_steps(p, out, send, recv):
    x, y, c = _me()
    copies = []
    for t in range(len(p)):
        for k, (px, py) in enumerate(_other_chips(x, y)):
            cp = pltpu.make_async_remote_copy(
                src_ref=p[t].at[2 * px + py], dst_ref=out[t].at[k], send_sem=send.at[3 * t + k],
                recv_sem=recv.at[3 * t + k], device_id=(px, py, c), device_id_type=MESH)
            cp.start()
            copies.append(cp)
    for cp in copies:
        cp.wait()


RS_ASYNC_ID = 2


def _rs_chips_async(parts):
    n = len(parts)
    src = [jax.new_ref(p, memory_space=pltpu.MemorySpace.HBM) for p in parts]
    got = [jax.empty_ref(_sds((3, *p.shape[1:]), p.dtype), memory_space=pltpu.MemorySpace.HBM) for p in parts]

    @pl.kernel(mesh=plsc.ScalarSubcoreMesh(axis_name="sequencer", num_cores=1), name="rs_chips_async",
               scratch_types=(pltpu.SemaphoreType.DMA((3 * n,)), pltpu.SemaphoreType.DMA((3 * n,))),
               compiler_params=pltpu.CompilerParams(collective_id=RS_ASYNC_ID))
    def launch(send, recv):
        x, y, c = _me()
        barrier = pltpu.get_barrier_semaphore()
        peers = [(px, py, c) for px, py in _other_chips(x, y)]
        for peer in peers:
            pl.semaphore_signal(barrier, inc=1, device_id=peer, device_id_type=MESH)
        pl.semaphore_wait(barrier, len(peers))
        _rs_chips_steps(src, got, send, recv)

    launch()
    return [g[...] for g in got]


def _rs_chips(parts):
    n = len(parts)

    def body(*refs):
        _rs_chips_steps(refs[:n], refs[n:2 * n], *refs[2 * n:])

    return pl.pallas_call(
        body, name="rs_chips", in_specs=[ANY] * n, out_specs=[ANY] * n,
        out_shape=[_sds((3, *s.shape[1:]), s.dtype) for s in parts],
        scratch_shapes=[pltpu.SemaphoreType.DMA((3 * n,)), pltpu.SemaphoreType.DMA((3 * n,))],
    )(*parts)


def _pair_exchange(bufs):
    n = len(bufs)

    def body(*refs):
        out = refs[n:2 * n]
        send, recv = refs[2 * n:]
        x, y, c = _me()
        copies = []
        for t in range(n):
            cp = pltpu.make_async_remote_copy(
                src_ref=out[t].at[c], dst_ref=out[t].at[c], send_sem=send.at[t], recv_sem=recv.at[t],
                device_id=(x, y, 1 - c), device_id_type=MESH)
            cp.start()
            copies.append(cp)
        for t, cp in enumerate(copies):
            cp.wait_send()
            pltpu.make_async_remote_copy(
                src_ref=out[t].at[1 - c], dst_ref=out[t].at[1 - c], send_sem=send.at[t], recv_sem=recv.at[t],
                device_id=(x, y, 1 - c), device_id_type=MESH).wait_recv()

    return pl.pallas_call(
        body, name="pair_exchange", in_specs=[ANY] * n, out_specs=[ANY] * n,
        out_shape=[_sds(s.shape, s.dtype) for s in bufs], input_output_aliases={t: t for t in range(n)},
        scratch_shapes=[pltpu.SemaphoreType.DMA((n,)), pltpu.SemaphoreType.DMA((n,))],
    )(*bufs)


def _row_tile(rows, cols, itemsize, budget=2 * 1024 * 1024):
    tr = rows
    while tr * cols * itemsize > budget and tr % 32 == 0:
        tr //= 2
    return tr


def _sum_pair(name, g, got, where):
    nchip, rows, cols = g.shape
    half = rows // 2
    tr = _row_tile(half, cols, 4)
    per = half // tr

    def body(w_ref, g_ref, r_ref, o_ref):
        o_ref[...] = (g_ref[...].astype(F32) + r_ref[...].astype(F32)).astype(o_ref.dtype)

    blk = pl.BlockSpec((None, tr, cols), lambda j, i, w_ref: (j, i, 0))
    return pl.pallas_call(
        body, name=name,
        grid_spec=pltpu.PrefetchScalarGridSpec(
            num_scalar_prefetch=1, grid=(nchip, per),
            in_specs=[pl.BlockSpec((None, tr, cols), lambda j, i, w_ref: (j, w_ref[1] * per + i, 0)), blk], out_specs=blk),
        out_shape=_sds((nchip, half, cols), BF16), compiler_params=_cp("parallel", "parallel"),
    )(where, g, got)


def _sum_chips(name, p, got, where):
    _, rows, cols = p.shape
    tr = _row_tile(rows, cols, 4)

    def body(w_ref, p_ref, r0, r1, r2, o_ref):
        o_ref[...] = ((p_ref[...].astype(F32) + r0[...].astype(F32)) + r1[...].astype(F32)) + r2[...].astype(F32)

    def got_k(k):
        return pl.BlockSpec((None, tr, cols), lambda i, w_ref: (k, i, 0))

    return pl.pallas_call(
        body, name=name,
        grid_spec=pltpu.PrefetchScalarGridSpec(
            num_scalar_prefetch=1, grid=(rows // tr,),
            in_specs=[pl.BlockSpec((None, tr, cols), lambda i, w_ref: (w_ref[0], i, 0)), got_k(0), got_k(1), got_k(2)],
            out_specs=pl.BlockSpec((None, tr, cols), lambda i, w_ref: (w_ref[1], i, 0))),
        out_shape=_sds((2, rows, cols), F32), compiler_params=_cp("parallel"),
    )(where, p, got, got, got)


def _reduce_scatter(grads, where):
    return _rs_finish(*_rs_begin(grads, where, False), where)


def _rs_begin(grads, where, asynchronous):
    got = _rs_pair(grads)
    parts = [_sum_pair(f"sum_pair_{t}", g, r, where) for t, (g, r) in enumerate(zip(grads, got))]
    return parts, (_rs_chips_async(parts) if asynchronous else _rs_chips(parts))


def _rs_finish(parts, got, where):
    halves = [_sum_chips(f"sum_chips_{t}", p, r, where) for t, (p, r) in enumerate(zip(parts, got))]
    return _pair_exchange(halves)


def _all_reduce_small(name, v):
    rows = v.shape[0]

    def body(v_ref, o_ref, land, send, recv):
        x, y, c = _me()
        mine = 4 * x + 2 * y + c
        copies = []
        for k in range(1, 8):
            kx, ky, kc = k >> 2, (k >> 1) & 1, k & 1
            peer = (x ^ kx, y ^ ky, c ^ kc)
            cp = pltpu.make_async_remote_copy(
                src_ref=v_ref, dst_ref=land.at[mine], send_sem=send.at[k - 1], recv_sem=recv.at[k - 1],
                device_id=peer, device_id_type=MESH)
            cp.start()
            copies.append(cp)
        land[mine] = v_ref[...]
        for k in range(1, 8):
            kx, ky, kc = k >> 2, (k >> 1) & 1, k & 1
            src = 4 * (x ^ kx) + 2 * (y ^ ky) + (c ^ kc)
            pltpu.make_async_remote_copy(
                src_ref=v_ref, dst_ref=land.at[src], send_sem=send.at[k - 1], recv_sem=recv.at[k - 1],
                device_id=(x ^ kx, y ^ ky, c ^ kc), device_id_type=MESH).wait_recv()
        acc = land[0]
        for d in range(1, 8):
            acc = acc + land[d]
        o_ref[...] = acc
        for cp in copies:
            cp.wait_send()

    vm = pl.BlockSpec(memory_space=pltpu.VMEM)
    return pl.pallas_call(
        body, name=name, in_specs=[vm], out_specs=vm, out_shape=_sds((rows, 128), F32),
        scratch_shapes=[pltpu.VMEM((8, rows, 128), F32), pltpu.SemaphoreType.DMA((7,)), pltpu.SemaphoreType.DMA((7,))],
    )(v)


def _adamw(name, w, g, m, v):
    rows, cols = w.shape
    tr = _row_tile(rows, cols, 4, budget=1024 * 1024)
    c1 = 1.0 / (1.0 - ADAM_B1 ** ADAM_STEP)
    c2 = 1.0 / (1.0 - ADAM_B2 ** ADAM_STEP)

    def body(w_ref, g_ref, m_ref, v_ref, d_ref, nm_ref, nv_ref):
        gv = g_ref[...]
        nm = ADAM_B1 * m_ref[...] + (1.0 - ADAM_B1) * gv
        nv = ADAM_B2 * v_ref[...] + (1.0 - ADAM_B2) * (gv * gv)
        d_ref[...] = -ADAM_LR * ((nm * c1) / (jnp.sqrt(nv * c2) + ADAM_EPS) + ADAM_WD * w_ref[...])
        nm_ref[...] = nm
        nv_ref[...] = nv

    blk = pl.BlockSpec((tr, cols), lambda i: (i, 0))
    return pl.pallas_call(
        body, name=name, grid=(rows // tr,), in_specs=[blk] * 4, out_specs=[blk] * 3,
        out_shape=[_sds((rows, cols), F32)] * 3, compiler_params=_cp("parallel"),
    )(w, g, m, v)


def _adamw_layers(name, w, g0, g1, m, v):
    _, half, cols = g0.shape
    tr = _row_tile(half, cols, 4, budget=1024 * 1024)
    per_half = half // tr
    per = 2 * per_half
    c1 = 1.0 / (1.0 - ADAM_B1 ** ADAM_STEP)
    c2 = 1.0 / (1.0 - ADAM_B2 ** ADAM_STEP)

    def body(w_ref, g0_ref, g1_ref, m_ref, v_ref, g_ref, d_ref, nm_ref, nv_ref):
        gv = jnp.where(pl.program_id(0) == 0, g0_ref[...], g1_ref[...])
        nm = ADAM_B1 * m_ref[...] + (1.0 - ADAM_B1) * gv
        nv = ADAM_B2 * v_ref[...] + (1.0 - ADAM_B2) * (gv * gv)
        g_ref[...] = gv
        d_ref[...] = -ADAM_LR * ((nm * c1) / (jnp.sqrt(nv * c2) + ADAM_EPS) + ADAM_WD * w_ref[...])
        nm_ref[...] = nm
        nv_ref[...] = nv

    both = pl.BlockSpec((None, tr, cols), lambda l, i: (l, i, 0))

    def halves(i):
        return i // per_half, i % per_half, 0

    first = pl.BlockSpec((None, tr, cols), lambda l, i: halves(i * (1 - l) + (per - 1) * l))
    second = pl.BlockSpec((None, tr, cols), lambda l, i: halves(i * l))
    return pl.pallas_call(
        body, name=name, grid=(2, per), in_specs=[both, first, second, both, both], out_specs=[both] * 4,
        out_shape=[_sds(w.shape, F32)] * 4, compiler_params=_cp("arbitrary", "arbitrary"),
    )(w, g0, g1, m, v)


def _to_bf16_slot(name, w, l, where):
    _, rows, cols = w.shape
    tr = _row_tile(rows, cols, 4)

    def body(w_ref, x_ref, o_ref):
        o_ref[...] = x_ref[...].astype(BF16)

    return pl.pallas_call(
        body, name=name,
        grid_spec=pltpu.PrefetchScalarGridSpec(
            num_scalar_prefetch=1, grid=(rows // tr,), in_specs=[pl.BlockSpec((None, tr, cols), lambda i, w_ref: (l, i, 0))],
            out_specs=pl.BlockSpec((None, tr, cols), lambda i, w_ref: (w_ref[0], i, 0))),
        out_shape=_sds((N_CHIPS, rows, cols), BF16), compiler_params=_cp("parallel"))(where, w)


BIG = ("w_in", "w_pool_up", "w_sb_up", "w_gdn_up", "w_out", "w_ff1", "w_ff2")
SMALL = (("attn_norm", (D,)), ("pool_w", (4, 128, 128)), ("pool_scale", (POOL_W,)), ("gdn_a_log", (HEADS,)),
         ("gdn_dt_bias", (HEADS,)), ("gdn_norm", (HD,)), ("mlp_norm", (D,)))


PACK_TILE = 8 * 128


def _rows128(a):
    flat = a.reshape(-1)
    pad = (-flat.shape[0]) % PACK_TILE
    return jnp.pad(flat, (0, pad)).reshape(-1, 128)


def _pack(parts):
    packed = jnp.concatenate([_rows128(p) for p in parts], axis=0)
    return jnp.pad(packed, ((0, (-packed.shape[0]) % 8), (0, 0)))


def _unpack(packed, shapes):
    out, r = [], 0
    for shp in shapes:
        size = 1
        for s in shp:
            size *= s
        nr = -(-size // PACK_TILE) * 8
        out.append(packed[r:r + nr].reshape(-1)[:size].reshape(shp))
        r += nr
    return out


def kernel(x, attn_norm, w_in, pool_w, pool_scale, gdn_conv, gdn_a_log, gdn_dt_bias, gdn_norm, w_pool_up, w_sb_up, w_gdn_up, w_out, mlp_norm, w_ff1, w_ff2, final_norm, loss_target, m_attn_norm, m_w_in, m_pool_w, m_pool_scale, m_gdn_conv, m_gdn_a_log, m_gdn_dt_bias, m_gdn_norm, m_w_pool_up, m_w_sb_up, m_w_gdn_up, m_w_out, m_mlp_norm, m_w_ff1, m_w_ff2, m_final_norm, v_attn_norm, v_w_in, v_pool_w, v_pool_scale, v_gdn_conv, v_gdn_a_log, v_gdn_dt_bias, v_gdn_norm, v_w_pool_up, v_w_sb_up, v_w_gdn_up, v_w_out, v_mlp_norm, v_w_ff1, v_w_ff2, v_final_norm):
    weights = dict(attn_norm=attn_norm, w_in=w_in, pool_w=pool_w, pool_scale=pool_scale, gdn_conv=gdn_conv,
                   gdn_a_log=gdn_a_log, gdn_dt_bias=gdn_dt_bias, gdn_norm=gdn_norm, w_pool_up=w_pool_up, w_sb_up=w_sb_up,
                   w_gdn_up=w_gdn_up, w_out=w_out, mlp_norm=mlp_norm, w_ff1=w_ff1, w_ff2=w_ff2, final_norm=final_norm)
    mom1 = dict(attn_norm=m_attn_norm, w_in=m_w_in, pool_w=m_pool_w, pool_scale=m_pool_scale, gdn_conv=m_gdn_conv,
                gdn_a_log=m_gdn_a_log, gdn_dt_bias=m_gdn_dt_bias, gdn_norm=m_gdn_norm, w_pool_up=m_w_pool_up,
                w_sb_up=m_w_sb_up, w_gdn_up=m_w_gdn_up, w_out=m_w_out, mlp_norm=m_mlp_norm, w_ff1=m_w_ff1, w_ff2=m_w_ff2,
                final_norm=m_final_norm)
    mom2 = dict(attn_norm=v_attn_norm, w_in=v_w_in, pool_w=v_pool_w, pool_scale=v_pool_scale, gdn_conv=v_gdn_conv,
                gdn_a_log=v_gdn_a_log, gdn_dt_bias=v_gdn_dt_bias, gdn_norm=v_gdn_norm, w_pool_up=v_w_pool_up,
                w_sb_up=v_w_sb_up, w_gdn_up=v_w_gdn_up, w_out=v_w_out, mlp_norm=v_mlp_norm, w_ff1=v_w_ff1, w_ff2=v_w_ff2,
                final_norm=v_final_norm)
    xi, yi, ci = lax.axis_index("x"), lax.axis_index("y"), lax.axis_index("c")
    chip = 2 * xi + yi
    where = jnp.stack([chip, ci]).astype(jnp.int32)

    bufs = [[_to_bf16_slot(f"cast_{nm}_{l}", weights[nm], l, where) for nm in BIG] for l in range(2)]
    first = _gather_weights(bufs[0])
    later, _ = lax.optimization_barrier((bufs[1], first))
    gw = [dict(zip(BIG, first)), dict(zip(BIG, _gather_weights_async(later)))]
    conv_cols = gdn_conv.shape[-1]
    conv_place = lax.dynamic_update_slice(jnp.zeros((2, GDN_CONV, N_CHIPS * conv_cols), F32),
                                          jnp.where(ci == 0, gdn_conv, 0.0), (0, 0, chip * conv_cols))
    conv_full = _all_reduce_small("gather_conv", _rows128(conv_place)).reshape(2, GDN_CONV, N_CHIPS * conv_cols)
    w_in_al = [_w_in_from_shards(gw[l]["w_in"]) for l in range(2)]
    sp = dict(attn_norm=attn_norm.reshape(2, 1, D), pool_w=pool_w, pool_scale=pool_scale.reshape(2, 1, POOL_W),
              conv=conv_full, a_log=jnp.stack([_row128(gdn_a_log[l]) for l in range(2)]),
              dt_bias=jnp.stack([_row128(gdn_dt_bias[l]) for l in range(2)]), gdn_norm=gdn_norm.reshape(2, 1, HD),
              mlp_norm=mlp_norm.reshape(2, 1, D), final_norm=final_norm.reshape(1, D))

    started = {}

    def on_layer_grads(l, dh, g):
        per_layer = [_w_in_to_shards(g["w_al"])] + [g[nm] for nm in BIG[1:]]
        started[l] = _rs_begin(per_layer, where, l == 1)
        if l == 1:
            dh, _ = lax.optimization_barrier((dh, started[l][0]))
        return dh

    loss, grad_x, grads, g_final = _local_step(x[0], loss_target[0], gw, w_in_al, sp, on_layer_grads)
    loss = lax.psum(loss[0, 0], ("x", "y", "c"))
    parts, got = started[1]
    got, _ = lax.optimization_barrier((got, grad_x))
    started[1] = (parts, got)
    big_grads = {nm: [] for nm in BIG}
    for l in range(2):
        for nm, red in zip(BIG, _rs_finish(*started[l], where)):
            big_grads[nm].append(red)
    small_parts, small_shapes = [], []
    for l in range(2):
        g = grads[l]
        for nm, shp in SMALL:
            key = {"gdn_a_log": "a_log", "gdn_dt_bias": "dt_bias"}.get(nm, nm)
            val = g[key]
            small_parts.append(val[0, :HEADS] if nm in ("gdn_a_log", "gdn_dt_bias") else val)
            small_shapes.append(shp)
        small_parts.append(g["conv"])
        small_shapes.append((GDN_CONV, N_CHIPS * conv_cols))
    small_parts.append(g_final)
    small_shapes.append((D,))
    reduced = _unpack(_all_reduce_small("reduce_small", _pack(small_parts)), small_shapes)
    per = len(SMALL) + 1
    grad = {}
    for i, (nm, _) in enumerate(SMALL):
        grad[nm] = jnp.stack([reduced[i], reduced[per + i]])
    conv_g = jnp.stack([reduced[per - 1], reduced[2 * per - 1]])
    grad["gdn_conv"] = lax.dynamic_slice(conv_g, (0, 0, chip * conv_cols), (2, GDN_CONV, conv_cols))
    grad["final_norm"] = reduced[-1]

    delta, new_m, new_v = {}, {}, {}
    for nm in BIG:
        grad[nm], delta[nm], new_m[nm], new_v[nm] = _adamw_layers("adamw_" + nm, weights[nm], *big_grads[nm], mom1[nm], mom2[nm])
    small_names = [nm for nm, _ in SMALL] + ["gdn_conv", "final_norm"]
    packs = [_pack([src[nm] for nm in small_names]) for src in (weights, grad, mom1, mom2)]
    outs = _adamw("adamw_small", *packs)
    shapes = [weights[nm].shape for nm in small_names]
    for dst, packed in zip((delta, new_m, new_v), outs):
        for nm, val in zip(small_names, _unpack(packed, shapes)):
            dst[nm] = val

    order = ("attn_norm", "w_in", "pool_w", "pool_scale", "gdn_conv", "gdn_a_log", "gdn_dt_bias", "gdn_norm", "w_pool_up",
             "w_sb_up", "w_gdn_up", "w_out", "mlp_norm", "w_ff1", "w_ff2", "final_norm")
    return (loss, grad_x[None], *[grad[n] for n in order], *[delta[n] for n in order], *[new_m[n] for n in order],
            *[new_v[n] for n in order])
```

```python
import functools

import jax
import jax.numpy as jnp
from jax import lax
from jax.experimental import pallas as pl
from jax.experimental.pallas import tpu as pltpu
from jax.experimental.pallas import tpu_sc as plsc

F32, BF16 = jnp.float32, jnp.bfloat16
HIGH = lax.Precision.HIGH
MESH = pl.DeviceIdType.MESH

D = 2048
EPS = 1e-6
POOL_WINDOWS = (2, 4, 8, 16)
POOL_W, SB_W, GDN_W = 512, 768, 768
HEADS, HD = 6, 128
SB_BLOCK = 128
GDN_CHUNK = 64
D_FF = 4 * D
N_IN = 12044
N_CHIPS = 4
OFF_SB, OFF_GQKV, OFF_Z, OFF_AB, OFF_P, OFF_GATE = 0, 2304, 4608, 5376, 5632, 6144
AB_W = 256
ORIG_SB, ORIG_AB, ORIG_GATE = 512, 5888, 5900
N_AL = 12288
VMEM_LIMIT = 48 * 1024 * 1024

ADAM_LR, ADAM_B1, ADAM_B2, ADAM_EPS, ADAM_WD, ADAM_STEP = 0.001, 0.9, 0.999, 1e-08, 0.01, 10

NT = (((1,), (1,)), ((), ()))
TN = (((0,), (0,)), ((), ()))


def _cp(*sem):
    return pltpu.CompilerParams(dimension_semantics=sem, vmem_limit_bytes=VMEM_LIMIT)


def _dot(a, b, dims=None, precision=None):
    if dims is None:
        dims = (((a.ndim - 1,), (0,)), ((), ()))
    return lax.dot_general(a, b, dims, precision=precision, preferred_element_type=F32)


def _hdot(a, b, dims=None):
    return _dot(a, b, dims, precision=HIGH)


def _bdot(a, b, dims=None):
    return _dot(a.astype(BF16), b.astype(BF16), dims)


def _mm(name, a, b, *, m, n, k, tm, tn, tk, a_spec, b_spec, dims, out_shapes, out_specs,
        extras=(), extra_specs=(), epilogue=None):
    nk = k // tk
    ne, no = len(extras), len(out_shapes)

    def body(*refs):
        a_ref, b_ref = refs[0], refs[1]
        ex = refs[2:2 + ne]
        outs = refs[2 + ne:2 + ne + no]
        acc = refs[-1]
        kk = pl.program_id(2)

        @pl.when(kk == 0)
        def _():
            acc[...] = jnp.zeros_like(acc)

        acc[...] += _dot(a_ref[...].astype(BF16), b_ref[...].astype(BF16), dims)

        @pl.when(kk == nk - 1)
        def _():
            r = acc[...]
            res = epilogue(r, *[e[...] for e in ex]) if epilogue is not None else (r,)
            for o, v in zip(outs, res):
                o[...] = v.astype(o.dtype)

    return pl.pallas_call(
        body, name=name, grid=(m // tm, n // tn, nk),
        in_specs=[a_spec, b_spec, *extra_specs], out_specs=out_specs, out_shape=out_shapes,
        scratch_shapes=[pltpu.VMEM((tm, tn), F32)],
        compiler_params=_cp("parallel", "parallel", "arbitrary"),
    )(a, b, *extras)


def _a_plain(tm, tk):
    return pl.BlockSpec((tm, tk), lambda i, j, kk: (i, kk))


def _a_trans(tm, tk):
    return pl.BlockSpec((tk, tm), lambda i, j, kk: (kk, i))


def _b_plain(tk, tn):
    return pl.BlockSpec((tk, tn), lambda i, j, kk: (kk, j))


def _b_trans(tk, tn):
    return pl.BlockSpec((tn, tk), lambda i, j, kk: (j, kk))


def _o_plain(tm, tn):
    return pl.BlockSpec((tm, tn), lambda i, j, kk: (i, j))


def _o_colshard(tm, tn, ns_cols):
    per = ns_cols // tn
    return pl.BlockSpec((None, tm, tn), lambda i, j, kk: (j // per, i, j % per))


def _w_cols(tk, tn, ns):
    per = ns // tn
    return pl.BlockSpec((None, tk, tn), lambda i, j, kk: (j // per, kk, j % per))


def _w_cols_t(tk, tn, ns):
    per = ns // tk
    return pl.BlockSpec((None, tn, tk), lambda i, j, kk: (kk // per, j, kk % per))


def _w_rows(tk, tn, ks):
    per = ks // tk
    return pl.BlockSpec((None, tk, tn), lambda i, j, kk: (kk // per, kk % per, j))


def _w_rows_t(tk, tn, ks):
    per = ks // tn
    return pl.BlockSpec((None, tn, tk), lambda i, j, kk: (j // per, j % per, kk))


def _sds(shape, dtype):
    return jax.ShapeDtypeStruct(shape, dtype)


def _rms_fwd(name, x, gain):
    t = x.shape[0]
    tt = min(256, t)

    def body(x_ref, g_ref, u_ref):
        xv = x_ref[...]
        r = lax.rsqrt(jnp.mean(xv * xv, axis=-1, keepdims=True) + EPS)
        u_ref[...] = (xv * r * g_ref[...]).astype(u_ref.dtype)

    return pl.pallas_call(
        body, name=name, grid=(t // tt,),
        in_specs=[pl.BlockSpec((tt, D), lambda i: (i, 0)), pl.BlockSpec((1, D), lambda i: (0, 0))],
        out_specs=pl.BlockSpec((tt, D), lambda i: (i, 0)), out_shape=_sds((t, D), BF16),
        compiler_params=_cp("parallel"),
    )(x, gain)


def _rms_bwd(name, du, x, gain, dres):
    t = x.shape[0]
    tt = min(256, t)

    def body(du_ref, x_ref, g_ref, dres_ref, dx_ref, dg_ref):
        @pl.when(pl.program_id(0) == 0)
        def _():
            dg_ref[...] = jnp.zeros_like(dg_ref)

        xv, duv = x_ref[...], du_ref[...]
        r = lax.rsqrt(jnp.mean(xv * xv, axis=-1, keepdims=True) + EPS)
        nx = xv * r
        dn = duv * g_ref[...]
        dg_ref[...] += jnp.sum(duv * nx, axis=0, keepdims=True)
        dx_ref[...] = dres_ref[...] + r * (dn - nx * jnp.mean(dn * nx, axis=-1, keepdims=True))

    row = pl.BlockSpec((tt, D), lambda i: (i, 0))
    vec = pl.BlockSpec((1, D), lambda i: (0, 0))
    return pl.pallas_call(
        body, name=name, grid=(t // tt,), in_specs=[row, row, vec, row], out_specs=[row, vec],
        out_shape=[_sds((t, D), F32), _sds((1, D), F32)], compiler_params=_cp("arbitrary"),
    )(du, x, gain, dres)


def _loss_head(x, gain, target):
    t = x.shape[0]
    tt = min(256, t)

    def body(x_ref, g_ref, t_ref, loss_ref, dx_ref, dg_ref):
        @pl.when(pl.program_id(0) == 0)
        def _():
            dg_ref[...] = jnp.zeros_like(dg_ref)
            loss_ref[...] = jnp.zeros_like(loss_ref)

        xv = x_ref[...]
        r = lax.rsqrt(jnp.mean(xv * xv, axis=-1, keepdims=True) + EPS)
        nx = xv * r
        err = nx * g_ref[...] - t_ref[...]
        loss_ref[...] += 0.5 * jnp.sum(jnp.mean(err * err, axis=-1, keepdims=True), axis=0, keepdims=True)
        dy = err * (1.0 / D)
        dn = dy * g_ref[...]
        dg_ref[...] += jnp.sum(dy * nx, axis=0, keepdims=True)
        dx_ref[...] = r * (dn - nx * jnp.mean(dn * nx, axis=-1, keepdims=True))

    row = pl.BlockSpec((tt, D), lambda i: (i, 0))
    vec = pl.BlockSpec((1, D), lambda i: (0, 0))
    one = pl.BlockSpec((1, 1), lambda i: (0, 0))
    return pl.pallas_call(
        body, name="loss_head", grid=(t // tt,), in_specs=[row, vec, row], out_specs=[one, row, vec],
        out_shape=[_sds((1, 1), F32), _sds((t, D), F32), _sds((1, D), F32)], compiler_params=_cp("arbitrary"),
    )(x, gain, target)


def _shift_down(v, s, t_idx):
    return jnp.where(t_idx >= s, pltpu.roll(v, s, 0), 0.0)


def _shift_up(v, s, t_idx, t):
    return jnp.where(t_idx < t - s, pltpu.roll(v, t - s, 0), 0.0)


def _pool_d(p, g, t_idx):
    s = p
    for step in range(g + 1):
        s = s + _shift_down(s, 1 << step, t_idx)
    cnt = jnp.minimum(t_idx + 1, POOL_WINDOWS[g]).astype(F32)
    return s / cnt - p, cnt


def _pool_fwd(proj, pool_w, pool_scale):
    t = proj.shape[0]
    g128 = POOL_W // len(POOL_WINDOWS)

    def body(p_ref, w_ref, s_ref, y_ref):
        t_idx = lax.broadcasted_iota(jnp.int32, (t, g128), 0)
        for g in range(len(POOL_WINDOWS)):
            sl = slice(g * g128, (g + 1) * g128)
            d, _ = _pool_d(p_ref[:, sl], g, t_idx)
            y_ref[:, sl] = (_bdot(d, w_ref[g]) * s_ref[:, sl]).astype(y_ref.dtype)

    return pl.pallas_call(
        body, name="pool_fwd", grid=(1,),
        in_specs=[pl.BlockSpec((t, POOL_W), lambda i: (0, OFF_P // POOL_W)),
                  pl.BlockSpec((4, g128, g128), lambda i: (0, 0, 0)), pl.BlockSpec((1, POOL_W), lambda i: (0, 0))],
        out_specs=pl.BlockSpec((t, POOL_W), lambda i: (0, 0)), out_shape=_sds((t, POOL_W), BF16),
        compiler_params=_cp("arbitrary"),
    )(proj, pool_w, pool_scale)


def _pool_bwd(proj, pool_w, pool_scale, dy):
    t = proj.shape[0]
    g128 = POOL_W // len(POOL_WINDOWS)

    def body(p_ref, w_ref, s_ref, dy_ref, dp_ref, dw_ref, ds_ref):
        t_idx = lax.broadcasted_iota(jnp.int32, (t, g128), 0)
        for g in range(len(POOL_WINDOWS)):
            sl = slice(g * g128, (g + 1) * g128)
            d, cnt = _pool_d(p_ref[:, sl], g, t_idx)
            dyv = dy_ref[:, sl].astype(F32)
            ds_ref[:, sl] = jnp.sum(dyv * _bdot(d, w_ref[g]), axis=0, keepdims=True)
            dys = dyv * s_ref[:, sl]
            dw_ref[g] = _bdot(d, dys, TN)
            dd = _bdot(dys, w_ref[g], NT)
            s = dd / cnt
            for step in range(g + 1):
                s = s + _shift_up(s, 1 << step, t_idx, t)
            dp_ref[:, sl] = (s - dd).astype(dp_ref.dtype)

    return pl.pallas_call(
        body, name="pool_bwd", grid=(1,),
        in_specs=[pl.BlockSpec((t, POOL_W), lambda i: (0, OFF_P // POOL_W)),
                  pl.BlockSpec((4, g128, g128), lambda i: (0, 0, 0)), pl.BlockSpec((1, POOL_W), lambda i: (0, 0)),
                  pl.BlockSpec((t, POOL_W), lambda i: (0, 0))],
        out_specs=[pl.BlockSpec((t, POOL_W), lambda i: (0, 0)), pl.BlockSpec((4, g128, g128), lambda i: (0, 0, 0)),
                   pl.BlockSpec((1, POOL_W), lambda i: (0, 0))],
        out_shape=[_sds((t, POOL_W), BF16), _sds((4, g128, g128), F32), _sds((1, POOL_W), F32)],
        compiler_params=_cp("arbitrary"),
    )(proj, pool_w, pool_scale, dy)


SB_GROUP = 3
SB_GW = SB_GROUP * HD


def _sb_cast_kv(proj):
    t = proj.shape[0]
    tt = min(512, t)

    def body(x_ref, o_ref):
        o_ref[...] = x_ref[...].astype(BF16)

    return pl.pallas_call(
        body, name="sb_cast_kv", grid=(t // tt, 2),
        in_specs=[pl.BlockSpec((tt, SB_W), lambda i, j: (i, OFF_SB // SB_W + 1 + j))],
        out_specs=pl.BlockSpec((tt, SB_W), lambda i, j: (i, j)), out_shape=_sds((t, 2 * SB_W), BF16),
        compiler_params=_cp("parallel", "parallel"),
    )(proj)


def _sb_specs(t):
    q_spec = pl.BlockSpec((SB_BLOCK, SB_GW), lambda g, i: (i, OFF_SB // SB_GW + g))
    k_spec = pl.BlockSpec((t, SB_GW), lambda g, i: (0, g))
    v_spec = pl.BlockSpec((t, SB_GW), lambda g, i: (0, SB_W // SB_GW + g))
    return q_spec, k_spec, v_spec


def _head(ref, h, rows=None):
    cols = slice(h * HD, (h + 1) * HD)
    return ref[:, cols] if rows is None else ref[rows, cols]


SB_KEYS = 512


def _sub(v, b):
    return v[:, b * SB_BLOCK:(b + 1) * SB_BLOCK]


def _sb_keep(kc, limit):
    row = lax.broadcasted_iota(jnp.int32, (SB_BLOCK, kc), 0)
    col = lax.broadcasted_iota(jnp.int32, (SB_BLOCK, kc), 1)
    return col < row + limit


def _sb_chunk(q, keys, run, later, limit):
    kc = keys.shape[0]
    z = _dot(q, keys, NT)
    lsz = jax.nn.log_sigmoid(z)
    ls = lsz - z
    if limit is not None:
        keep = _sb_keep(kc, limit)
        ls = jnp.where(keep, ls, 0.0)
    parts = [None] * (kc // SB_BLOCK)
    for b in reversed(range(kc // SB_BLOCK)):
        parts[b] = _hdot(_sub(ls, b), later) + run
        run = run + jnp.sum(_sub(ls, b), axis=1, keepdims=True)
    a = jnp.exp(lsz + jnp.concatenate(parts, axis=1))
    if limit is not None:
        a = jnp.where(keep, a, 0.0)
    return z, a, run


def _sb_fwd(proj, kv):
    t = proj.shape[0]
    kc = min(SB_KEYS, t)
    scale = HD ** -0.5

    def body(q_ref, k_ref, v_ref, o_ref):
        i = pl.program_id(1)
        top = (i * SB_BLOCK) // kc
        qs = [(_head(q_ref, h) * scale).astype(BF16) for h in range(SB_GROUP)]
        row = lax.broadcasted_iota(jnp.int32, (SB_BLOCK, SB_BLOCK), 0)
        col = lax.broadcasted_iota(jnp.int32, (SB_BLOCK, SB_BLOCK), 1)
        later = (row > col).astype(F32)

        def chunk(jc, carry, masked):
            rows = pl.ds(pl.multiple_of(jc * kc, kc), kc)
            limit = i * SB_BLOCK - jc * kc if masked else None
            out = []
            for h in range(SB_GROUP):
                acc, run = carry[h]
                _, a, run = _sb_chunk(qs[h], _head(k_ref, h, rows), run, later, limit)
                out.append((acc + _dot(a.astype(BF16), _head(v_ref, h, rows)), run))
            return tuple(out)

        zero = tuple((jnp.zeros((SB_BLOCK, HD), F32), jnp.zeros((SB_BLOCK, 1), F32)) for _ in range(SB_GROUP))
        carry = chunk(top, zero, True)
        carry = lax.fori_loop(0, top, lambda jj, c: chunk(top - 1 - jj, c, False), carry)
        for h in range(SB_GROUP):
            o_ref[:, h * HD:(h + 1) * HD] = carry[h][0].astype(o_ref.dtype)

    return pl.pallas_call(
        body, name="sb_fwd", grid=(HEADS // SB_GROUP, t // SB_BLOCK), in_specs=list(_sb_specs(t)),
        out_specs=pl.BlockSpec((SB_BLOCK, SB_GW), lambda g, i: (i, g)), out_shape=_sds((t, SB_W), BF16),
        compiler_params=_cp("parallel", "arbitrary"),
    )(proj, kv, kv)


def _sb_bwd(proj, kv, dy):
    t = proj.shape[0]
    nq = t // SB_BLOCK
    kc = min(SB_KEYS, t)
    scale = HD ** -0.5

    def body(q_ref, k_ref, v_ref, do_ref, dq_ref, dk_ref, dv_ref, z_scr, e_scr):
        i = pl.program_id(1)
        top = (i * SB_BLOCK) // kc

        @pl.when(i == 0)
        def _():
            dk_ref[...] = jnp.zeros_like(dk_ref)
            dv_ref[...] = jnp.zeros_like(dv_ref)

        qs = [(_head(q_ref, h) * scale).astype(BF16) for h in range(SB_GROUP)]
        dos = [_head(do_ref, h).astype(BF16) for h in range(SB_GROUP)]
        row = lax.broadcasted_iota(jnp.int32, (SB_BLOCK, SB_BLOCK), 0)
        col = lax.broadcasted_iota(jnp.int32, (SB_BLOCK, SB_BLOCK), 1)
        later = (row > col).astype(F32)
        earlier = (row < col).astype(F32)

        def down(jc, runs, masked):
            rows = pl.ds(pl.multiple_of(jc * kc, kc), kc)
            limit = i * SB_BLOCK - jc * kc if masked else None
            out = []
            for h in range(SB_GROUP):
                z, a, run = _sb_chunk(qs[h], _head(k_ref, h, rows), runs[h], later, limit)
                z_scr[h, jc] = z
                e_scr[h, jc] = a * _dot(dos[h], _head(v_ref, h, rows), NT)
                dv_ref[rows, h * HD:(h + 1) * HD] += _dot(a.astype(BF16), dos[h], TN)
                out.append(run)
            return tuple(out)

        zero = tuple(jnp.zeros((SB_BLOCK, 1), F32) for _ in range(SB_GROUP))
        runs = down(top, zero, True)
        lax.fori_loop(0, top, lambda jj, r: down(top - 1 - jj, r, False), runs)

        def up(jc, carry, masked):
            rows = pl.ds(pl.multiple_of(jc * kc, kc), kc)
            out = []
            for h in range(SB_GROUP):
                dq, run = carry[h]
                z, e = z_scr[h, jc], e_scr[h, jc]
                parts = []
                for b in range(kc // SB_BLOCK):
                    parts.append(_hdot(_sub(e, b), earlier) + run)
                    run = run + jnp.sum(_sub(e, b), axis=1, keepdims=True)
                sz = jax.nn.sigmoid(z)
                dz = e * (1.0 - sz) - jnp.concatenate(parts, axis=1) * sz
                if masked:
                    dz = jnp.where(_sb_keep(kc, i * SB_BLOCK - jc * kc), dz, 0.0)
                dz = dz.astype(BF16)
                dk_ref[rows, h * HD:(h + 1) * HD] += _dot(dz, qs[h], TN)
                out.append((dq + _dot(dz, _head(k_ref, h, rows)), run))
            return tuple(out)

        zero = tuple((jnp.zeros((SB_BLOCK, HD), F32), jnp.zeros((SB_BLOCK, 1), F32)) for _ in range(SB_GROUP))
        carry = lax.fori_loop(0, top, lambda jc, c: up(jc, c, False), zero)
        carry = up(top, carry, True)
        for h in range(SB_GROUP):
            dq_ref[:, h * HD:(h + 1) * HD] = (carry[h][0] * scale).astype(dq_ref.dtype)

    blk = pl.BlockSpec((SB_BLOCK, SB_GW), lambda g, i: (i, g))
    seq = pl.BlockSpec((t, SB_GW), lambda g, i: (0, g))
    scratch = pltpu.VMEM((SB_GROUP, t // kc, SB_BLOCK, kc), F32)
    return pl.pallas_call(
        body, name="sb_bwd", grid=(HEADS // SB_GROUP, nq), in_specs=[*_sb_specs(t), blk], out_specs=[blk, seq, seq],
        out_shape=[_sds((t, SB_W), BF16), _sds((t, SB_W), F32), _sds((t, SB_W), F32)],
        scratch_shapes=[scratch, scratch], compiler_params=_cp("parallel", "arbitrary"),
    )(proj, kv, kv, dy)


CONV_TILE = 256
GDN_CONV = 4


def _conv_pre(x, w_ref, t_idx):
    pre = w_ref[GDN_CONV - 1:GDN_CONV, :] * x
    for s in range(1, GDN_CONV):
        pre = pre + w_ref[GDN_CONV - 1 - s:GDN_CONV - s, :] * _shift_down(x, s, t_idx)
    return pre


def _conv_fwd(proj, conv_w):
    t = proj.shape[0]
    width = conv_w.shape[1]

    def body(x_ref, w_ref, y_ref):
        t_idx = lax.broadcasted_iota(jnp.int32, (t, CONV_TILE), 0)
        pre = _conv_pre(x_ref[...], w_ref, t_idx)
        y_ref[...] = pre * jax.nn.sigmoid(pre)

    return pl.pallas_call(
        body, name="conv_fwd", grid=(width // CONV_TILE,),
        in_specs=[pl.BlockSpec((t, CONV_TILE), lambda c: (0, OFF_GQKV // CONV_TILE + c)),
                  pl.BlockSpec((GDN_CONV, CONV_TILE), lambda c: (0, c))],
        out_specs=pl.BlockSpec((t, CONV_TILE), lambda c: (0, c)), out_shape=_sds((t, width), F32),
        compiler_params=_cp("parallel"),
    )(proj, conv_w)


def _conv_bwd(proj, conv_w, dc):
    t = proj.shape[0]
    width = dc.shape[1]
    per = width // CONV_TILE
    part = 0

    def body(x_ref, w_ref, dc_ref, dx_ref, dw_ref):
        t_idx = lax.broadcasted_iota(jnp.int32, (t, CONV_TILE), 0)
        x = x_ref[...]
        pre = _conv_pre(x, w_ref, t_idx)
        sg = jax.nn.sigmoid(pre)
        dpre = dc_ref[...] * (sg * (1.0 + pre * (1.0 - sg)))
        dx = w_ref[GDN_CONV - 1:GDN_CONV, :] * dpre
        dw_ref[GDN_CONV - 1:GDN_CONV, :] = jnp.sum(dpre * x, axis=0, keepdims=True)
        for s in range(1, GDN_CONV):
            dx = dx + w_ref[GDN_CONV - 1 - s:GDN_CONV - s, :] * _shift_up(dpre, s, t_idx, t)
            dw_ref[GDN_CONV - 1 - s:GDN_CONV - s, :] = jnp.sum(dpre * _shift_down(x, s, t_idx), axis=0, keepdims=True)
        dx_ref[...] = dx.astype(dx_ref.dtype)

    return pl.pallas_call(
        body, name="conv_bwd", grid=(per,),
        in_specs=[pl.BlockSpec((t, CONV_TILE), lambda c: (0, OFF_GQKV // CONV_TILE + part * per + c)),
                  pl.BlockSpec((GDN_CONV, CONV_TILE), lambda c: (0, part * per + c)),
                  pl.BlockSpec((t, CONV_TILE), lambda c: (0, c))],
        out_specs=[pl.BlockSpec((t, CONV_TILE), lambda c: (0, c)), pl.BlockSpec((GDN_CONV, CONV_TILE), lambda c: (0, c))],
        out_shape=[_sds((t, width), BF16), _sds((GDN_CONV, width), F32)],
        compiler_params=_cp("parallel"),
    )(proj, conv_w, dc)


def _heads(x):
    return jnp.concatenate([x[:, h * HD:(h + 1) * HD][None] for h in range(HEADS)], axis=0)


def _hb(a, b, ca=2, cb=1):
    return lax.dot_general(a, b, (((ca,), (cb,)), ((0,), (0,))), precision=HIGH, preferred_element_type=F32)


def _gdn_prep(cq, ck, cv, ab, alog_row, dtb_row):
    c = GDN_CHUNK
    row = lax.broadcasted_iota(jnp.int32, (c, c), 0)
    col = lax.broadcasted_iota(jnp.int32, (c, c), 1)
    incl, strict, eye = row >= col, row > col, row == col
    def lanes(v, first):
        return jnp.concatenate([v[:, first + h:first + h + 1][None] for h in range(HEADS)], axis=0)

    a_col, b_col = lanes(ab, 0), lanes(ab, HEADS)
    a_log, dt_bias = lanes(alog_row, 0), lanes(dtb_row, 0)
    qn = cq * lax.rsqrt(jnp.sum(cq * cq, axis=-1, keepdims=True) + EPS) * (HD ** -0.5)
    kn = ck * lax.rsqrt(jnp.sum(ck * ck, axis=-1, keepdims=True) + EPS)
    la_col = -jnp.exp(a_log) * jax.nn.softplus(a_col + dt_bias)
    beta = jax.nn.sigmoid(b_col)
    la_row = jnp.sum(jnp.where(eye, la_col, 0.0), axis=1, keepdims=True)
    g_col = jnp.sum(jnp.where(incl, la_row, 0.0), axis=2, keepdims=True)
    g_row = jnp.sum(jnp.where(row <= col, la_col, 0.0), axis=1, keepdims=True)
    g_last = jnp.sum(la_col, axis=1, keepdims=True)
    gamma = jnp.where(incl, jnp.exp(jnp.where(incl, g_col - g_row, 0.0)), 0.0)
    lower = jnp.where(strict, beta * _hb(kn, kn, 2, 2) * gamma, 0.0)
    inv = jnp.where(eye, 1.0, 0.0) - lower
    pw = _hb(lower, lower)
    for step in range(5):
        inv = inv + _hb(inv, pw)
        if step < 4:
            pw = _hb(pw, pw)
    u = _hb(inv, cv * beta)
    w = _hb(inv, kn * (beta * jnp.exp(g_col)))
    qk = _hb(qn, kn, 2, 2) * gamma
    return u, w, qk, qn * jnp.exp(g_col), kn * jnp.exp(g_last - g_col), jnp.exp(g_last)


def _gdn_post(o, z, gain):
    y = o * lax.rsqrt(jnp.mean(o * o, axis=-1, keepdims=True) + EPS) * gain
    return y * (z * jax.nn.sigmoid(z))


def _gdn_specs(nc, reverse):
    c = GDN_CHUNK

    def ch(n):
        return nc - 1 - n if reverse else n

    def wide(array_off):
        return pl.BlockSpec((c, GDN_W), lambda n: (ch(n), array_off // GDN_W))

    ab = pl.BlockSpec((c, HD), lambda n: (ch(n), OFF_AB // HD))
    row = pl.BlockSpec((1, HD), lambda n: (0, 0))
    state = pl.BlockSpec((None, HEADS, HD, HD), lambda n: (ch(n), 0, 0, 0))
    return wide, ab, row, state


def _gdn_fwd(cqkv, proj, a_log, dt_bias, gain):
    t = proj.shape[0]
    nc = t // GDN_CHUNK
    wide, ab, row, state = _gdn_specs(nc, False)

    def body(cq_ref, ck_ref, cv_ref, ab_ref, z_ref, al_ref, dt_ref, g_ref, y_ref, sprev_ref, s_scr):
        @pl.when(pl.program_id(0) == 0)
        def _():
            s_scr[...] = jnp.zeros_like(s_scr)

        u, w, qk, qd, kd, dec = _gdn_prep(_heads(cq_ref[...]), _heads(ck_ref[...]), _heads(cv_ref[...]), ab_ref[...],
                                          al_ref[...], dt_ref[...])
        s = s_scr[...]
        sprev_ref[...] = s
        v_new = u - _hb(w, s)
        o = _hb(qd, s) + _hb(qk, v_new)
        s_scr[...] = s * dec + _hb(kd, v_new, 1, 1)
        y = _gdn_post(o, _heads(z_ref[...]), g_ref[...])
        for h in range(HEADS):
            y_ref[:, h * HD:(h + 1) * HD] = y[h].astype(y_ref.dtype)

    return pl.pallas_call(
        body, name="gdn_fwd", grid=(nc,),
        in_specs=[wide(0), wide(GDN_W), wide(2 * GDN_W), ab, wide(OFF_Z), row, row, row],
        out_specs=[wide(0), state], out_shape=[_sds((t, GDN_W), BF16), _sds((nc, HEADS, HD, HD), F32)],
        scratch_shapes=[pltpu.VMEM((HEADS, HD, HD), F32)], compiler_params=_cp("arbitrary"),
    )(cqkv, cqkv, cqkv, proj, proj, a_log, dt_bias, gain)


def _gdn_bwd(cqkv, proj, a_log, dt_bias, gain, sprev, dy):
    t = proj.shape[0]
    nc = t // GDN_CHUNK
    wide, ab, row, state = _gdn_specs(nc, True)

    def body(cq_ref, ck_ref, cv_ref, ab_ref, z_ref, al_ref, dt_ref, g_ref, sp_ref, dy_ref,
             dc_ref, dab_ref, dz_ref, dal_ref, ddt_ref, dg_ref, ds_scr):
        @pl.when(pl.program_id(0) == 0)
        def _():
            ds_scr[...] = jnp.zeros_like(ds_scr)
            dal_ref[...] = jnp.zeros_like(dal_ref)
            ddt_ref[...] = jnp.zeros_like(ddt_ref)
            dg_ref[...] = jnp.zeros_like(dg_ref)

        (u, w, qk, qd, kd, dec), prep_vjp = jax.vjp(
            _gdn_prep, _heads(cq_ref[...]), _heads(ck_ref[...]), _heads(cv_ref[...]), ab_ref[...], al_ref[...], dt_ref[...])
        s = sp_ref[...]
        v_new = u - _hb(w, s)
        o = _hb(qd, s) + _hb(qk, v_new)
        _, post_vjp = jax.vjp(_gdn_post, o, _heads(z_ref[...]), g_ref[...])
        do, dz, dgain = post_vjp(_heads(dy_ref[...]).astype(F32))
        ds_next = ds_scr[...]
        d_vnew = _hb(qk, do, 1, 1) + _hb(kd, ds_next)
        d_qk = _hb(do, v_new, 2, 2)
        d_qd = _hb(do, s, 2, 2)
        d_kd = _hb(v_new, ds_next, 2, 2)
        d_dec = jnp.sum(jnp.sum(s * ds_next, axis=2, keepdims=True), axis=1, keepdims=True)
        ds_scr[...] = dec * ds_next + _hb(qd, do, 1, 1) - _hb(w, d_vnew, 1, 1)
        d_w = -_hb(d_vnew, s, 2, 2)
        dcq, dck, dcv, dab, dal, ddt = prep_vjp((d_vnew, d_w, d_qk, d_qd, d_kd, d_dec))
        for h in range(HEADS):
            dc_ref[:, h * HD:(h + 1) * HD] = dcq[h]
            dc_ref[:, GDN_W + h * HD:GDN_W + (h + 1) * HD] = dck[h]
            dc_ref[:, 2 * GDN_W + h * HD:2 * GDN_W + (h + 1) * HD] = dcv[h]
            dz_ref[:, h * HD:(h + 1) * HD] = dz[h].astype(dz_ref.dtype)
        dab_ref[...] = dab
        dal_ref[...] += dal
        ddt_ref[...] += ddt
        dg_ref[...] += dgain

    c = GDN_CHUNK
    return pl.pallas_call(
        body, name="gdn_bwd", grid=(nc,),
        in_specs=[wide(0), wide(GDN_W), wide(2 * GDN_W), ab, wide(OFF_Z), row, row, row, state, wide(0)],
        out_specs=[pl.BlockSpec((c, 3 * GDN_W), lambda n: (nc - 1 - n, 0)), pl.BlockSpec((c, HD), lambda n: (nc - 1 - n, 0)),
                   wide(0), row, row, row],
        out_shape=[_sds((t, 3 * GDN_W), F32), _sds((t, HD), F32), _sds((t, GDN_W), BF16),
                   _sds((1, HD), F32), _sds((1, HD), F32), _sds((1, HD), F32)],
        scratch_shapes=[pltpu.VMEM((HEADS, HD, HD), F32)], compiler_params=_cp("arbitrary"),
    )(cqkv, cqkv, cqkv, proj, proj, a_log, dt_bias, gain, sprev, dy)


MERGE_TN = 512


def _merge_specs(t, tm):
    tn = MERGE_TN
    ys = [pl.BlockSpec((tm, wd), lambda i, j: (i, 0)) for wd in (POOL_W, SB_W, GDN_W)]
    ws = [pl.BlockSpec((None, wd, tn), lambda i, j: (j, 0, 0)) for wd in (POOL_W, SB_W, GDN_W)]
    gs = [pl.BlockSpec((tm, tn), functools.partial(lambda i, j, b: (i, OFF_GATE // tn + b * (D // tn) + j), b=b))
          for b in range(3)]
    out = pl.BlockSpec((tm, tn), lambda i, j: (i, j))
    return ys, ws, gs, out


def _merge_fwd(ys, wups, proj):
    t = proj.shape[0]
    tm = min(512, t)
    y_specs, w_specs, g_specs, out = _merge_specs(t, tm)

    def body(y0, y1, y2, w0, w1, w2, g0, g1, g2, o_ref):
        acc = jnp.zeros(o_ref.shape, F32)
        for y, w, g in ((y0, w0, g0), (y1, w1, g1), (y2, w2, g2)):
            acc = acc + jax.nn.sigmoid(g[...]) * _dot(y[...], w[...])
        o_ref[...] = acc.astype(o_ref.dtype)

    return pl.pallas_call(
        body, name="merge_fwd", grid=(t // tm, D // MERGE_TN), in_specs=[*y_specs, *w_specs, *g_specs],
        out_specs=out, out_shape=_sds((t, D), BF16), compiler_params=_cp("parallel", "parallel"),
    )(*ys, *wups, proj, proj, proj)


def _merge_bwd(ys, wups, proj, dmerged):
    t = proj.shape[0]
    tm = min(512, t)
    y_specs, w_specs, g_specs, out = _merge_specs(t, tm)

    def body(y0, y1, y2, w0, w1, w2, g0, g1, g2, dm_ref, dg0, dg1, dg2, dm0, dm1, dm2):
        dm = dm_ref[...].astype(F32)
        for y, w, g, dg, dmb in ((y0, w0, g0, dg0, dm0), (y1, w1, g1, dg1, dm1), (y2, w2, g2, dg2, dm2)):
            sg = jax.nn.sigmoid(g[...])
            dg[...] = (dm * _dot(y[...], w[...]) * sg * (1.0 - sg)).astype(dg.dtype)
            dmb[...] = (dm * sg).astype(dmb.dtype)

    return pl.pallas_call(
        body, name="merge_bwd", grid=(t // tm, D // MERGE_TN), in_specs=[*y_specs, *w_specs, *g_specs, out],
        out_specs=[out] * 6, out_shape=[_sds((t, D), BF16)] * 6, compiler_params=_cp("parallel", "parallel"),
    )(*ys, *wups, proj, proj, proj, dmerged)


def _tile(t, want):
    return min(t, want)


def _layer_fwd(x, l, gw, w_al, sp):
    t = x.shape[0]
    tm = _tile(t, 1024)
    u = _rms_fwd("rms_attn", x, sp["attn_norm"][l])
    proj = _mm("proj", u, w_al, m=t, n=N_AL, k=D, tm=tm, tn=1024, tk=512, a_spec=_a_plain(tm, 512),
               b_spec=_b_plain(512, 1024), dims=None, out_shapes=[_sds((t, N_AL), F32)], out_specs=[_o_plain(tm, 1024)])[0]
    y_pool = _pool_fwd(proj, sp["pool_w"][l], sp["pool_scale"][l])
    kv = _sb_cast_kv(proj)
    y_sb = _sb_fwd(proj, kv)
    cqkv = _conv_fwd(proj, sp["conv"][l])
    y_gdn, sprev = _gdn_fwd(cqkv, proj, sp["a_log"][l], sp["dt_bias"][l], sp["gdn_norm"][l])
    ys = (y_pool, y_sb, y_gdn)
    wups = (gw["w_pool_up"], gw["w_sb_up"], gw["w_gdn_up"])
    merged = _merge_fwd(ys, wups, proj)
    x1 = _mm("out_proj", merged, gw["w_out"], m=t, n=D, k=D, tm=tm, tn=1024, tk=512, a_spec=_a_plain(tm, 512),
             b_spec=_w_rows(512, 1024, 512), dims=None, out_shapes=[_sds((t, D), F32)], out_specs=[_o_plain(tm, 1024)],
             extras=[x], extra_specs=[_o_plain(tm, 1024)], epilogue=lambda r, xr: (r + xr,))[0]
    u2 = _rms_fwd("rms_mlp", x1, sp["mlp_norm"][l])

    def relu2(r):
        hv = jnp.maximum(r, 0.0)
        return hv, hv * hv

    hid, hid2 = _mm("ff1", u2, gw["w_ff1"], m=t, n=D_FF, k=D, tm=tm, tn=1024, tk=512, a_spec=_a_plain(tm, 512),
                    b_spec=_w_cols(512, 1024, 2048), dims=None, out_shapes=[_sds((t, D_FF), BF16)] * 2,
                    out_specs=[_o_plain(tm, 1024)] * 2, epilogue=relu2)
    x2 = _mm("ff2", hid2, gw["w_ff2"], m=t, n=D, k=D_FF, tm=tm, tn=1024, tk=512, a_spec=_a_plain(tm, 512),
             b_spec=_w_rows(512, 1024, 2048), dims=None, out_shapes=[_sds((t, D), F32)], out_specs=[_o_plain(tm, 1024)],
             extras=[x1], extra_specs=[_o_plain(tm, 1024)], epilogue=lambda r, xr: (r + xr,))[0]
    saved = dict(x=x, u=u, proj=proj, kv=kv, cqkv=cqkv, sprev=sprev, ys=ys, merged=merged, x1=x1, u2=u2, hid=hid, hid2=hid2)
    return x2, saved


def _layer_bwd(dx2, l, gw, w_al, sp, sv, emit=None):
    t = dx2.shape[0]
    tm = _tile(t, 1024)
    tk = _tile(t, 512)
    g = {}
    if emit is None:
        emit = lambda names, grads, v: v
    dpre = _mm("ff2_dx", dx2, gw["w_ff2"], m=t, n=D_FF, k=D, tm=tm, tn=1024, tk=512, a_spec=_a_plain(tm, 512),
               b_spec=_w_rows_t(512, 1024, 2048), dims=NT, out_shapes=[_sds((t, D_FF), BF16)],
               out_specs=[_o_plain(tm, 1024)], extras=[sv["hid"]], extra_specs=[_o_plain(tm, 1024)],
               epilogue=lambda r, hv: (r * (2.0 * hv.astype(F32)),))[0]
    g["w_ff2"] = _mm("ff2_dw", sv["hid2"], dx2, m=D_FF, n=D, k=t, tm=1024, tn=1024, tk=tk, a_spec=_a_trans(1024, tk),
                     b_spec=_b_plain(tk, 1024), dims=TN, out_shapes=[_sds((D_FF, D), BF16)],
                     out_specs=[_o_plain(1024, 1024)])[0].reshape(N_CHIPS, D_FF // N_CHIPS, D)
    du2 = _mm("ff1_dx", dpre, gw["w_ff1"], m=t, n=D, k=D_FF, tm=tm, tn=1024, tk=512, a_spec=_a_plain(tm, 512),
              b_spec=_w_cols_t(512, 1024, 2048), dims=NT, out_shapes=[_sds((t, D), F32)], out_specs=[_o_plain(tm, 1024)])[0]
    g["w_ff1"] = _mm("ff1_dw", sv["u2"], dpre, m=D, n=D_FF, k=t, tm=1024, tn=1024, tk=tk, a_spec=_a_trans(1024, tk),
                     b_spec=_b_plain(tk, 1024), dims=TN, out_shapes=[_sds((N_CHIPS, D, D_FF // N_CHIPS), BF16)],
                     out_specs=[_o_colshard(1024, 1024, D_FF // N_CHIPS)])[0]
    dx1, g["mlp_norm"] = _rms_bwd("rms_mlp_bwd", du2, sv["x1"], sp["mlp_norm"][l], dx2)
    dx1 = emit(("w_ff1", "w_ff2"), g, dx1)
    dmerged = _mm("out_dx", dx1, gw["w_out"], m=t, n=D, k=D, tm=tm, tn=512, tk=1024, a_spec=_a_plain(tm, 1024),
                  b_spec=_w_rows_t(1024, 512, 512), dims=NT, out_shapes=[_sds((t, D), BF16)], out_specs=[_o_plain(tm, 512)])[0]
    g["w_out"] = _mm("out_dw", sv["merged"], dx1, m=D, n=D, k=t, tm=1024, tn=1024, tk=tk, a_spec=_a_trans(1024, tk),
                     b_spec=_b_plain(tk, 1024), dims=TN, out_shapes=[_sds((D, D), BF16)],
                     out_specs=[_o_plain(1024, 1024)])[0].reshape(N_CHIPS, D // N_CHIPS, D)
    wups = (gw["w_pool_up"], gw["w_sb_up"], gw["w_gdn_up"])
    dg0, dg1, dg2, dm0, dm1, dm2 = _merge_bwd(sv["ys"], wups, sv["proj"], dmerged)
    dys = []
    for nm, yb, dmb, wd in zip(("w_pool_up", "w_sb_up", "w_gdn_up"), sv["ys"], (dm0, dm1, dm2), (POOL_W, SB_W, GDN_W)):
        dys.append(_mm(nm + "_dx", dmb, gw[nm], m=t, n=wd, k=D, tm=tm, tn=256, tk=512, a_spec=_a_plain(tm, 512),
                       b_spec=_w_cols_t(512, 256, 512), dims=NT, out_shapes=[_sds((t, wd), F32)],
                       out_specs=[_o_plain(tm, 256)])[0])
        g[nm] = _mm(nm + "_dw", yb, dmb, m=wd, n=D, k=t, tm=256, tn=512, tk=tk, a_spec=_a_trans(256, tk),
                    b_spec=_b_plain(tk, 512), dims=TN, out_shapes=[_sds((N_CHIPS, wd, D // N_CHIPS), BF16)],
                    out_specs=[_o_colshard(256, 512, D // N_CHIPS)])[0]
    dys[2] = emit(("w_pool_up", "w_sb_up", "w_gdn_up", "w_out"), g, dys[2])
    proj = sv["proj"]
    dp, g["pool_w"], g["pool_scale"] = _pool_bwd(proj, sp["pool_w"][l], sp["pool_scale"][l], dys[0])
    dsq, dsk, dsv = _sb_bwd(proj, sv["kv"], dys[1])
    dc, dab, dz, g["a_log"], g["dt_bias"], g["gdn_norm"] = _gdn_bwd(
        sv["cqkv"], proj, sp["a_log"][l], sp["dt_bias"][l], sp["gdn_norm"][l], sv["sprev"], dys[2])
    dgx, g["conv"] = _conv_bwd(proj, sp["conv"][l], dc)
    dproj = jnp.concatenate(
        [dsq, dsk.astype(BF16), dsv.astype(BF16), dgx, dz, dab.astype(BF16), jnp.zeros((t, AB_W - HD), BF16),
         dp, dg0, dg1, dg2], axis=1)
    du = _mm("proj_dx", dproj, w_al, m=t, n=D, k=N_AL, tm=tm, tn=1024, tk=512, a_spec=_a_plain(tm, 512),
             b_spec=_b_trans(512, 1024), dims=NT, out_shapes=[_sds((t, D), F32)], out_specs=[_o_plain(tm, 1024)])[0]
    g["w_al"] = _mm("proj_dw", sv["u"], dproj, m=D, n=N_AL, k=t, tm=1024, tn=1024, tk=tk, a_spec=_a_trans(1024, tk),
                    b_spec=_b_plain(tk, 1024), dims=TN, out_shapes=[_sds((D, N_AL), BF16)], out_specs=[_o_plain(1024, 1024)])[0]
    dx, g["attn_norm"] = _rms_bwd("rms_attn_bwd", du, sv["x"], sp["attn_norm"][l], dx1)
    g["w_in"] = _w_in_to_shards(g["w_al"])
    dx = emit(("w_in",), g, dx)
    return dx, g


def _align_w_in(w):
    n_ab = ORIG_GATE - ORIG_AB
    return jnp.concatenate([w[:, ORIG_SB:ORIG_GATE], jnp.zeros((D, AB_W - n_ab), w.dtype), w[:, :ORIG_SB], w[:, ORIG_GATE:]],
                           axis=1)


def _unalign_w_in(w):
    n_ab = ORIG_GATE - ORIG_AB
    return jnp.concatenate([w[:, OFF_P:OFF_GATE], w[:, :OFF_AB + n_ab], w[:, OFF_GATE:]], axis=1)


W_IN_RUNS = ((0, ORIG_SB, OFF_P), (ORIG_SB, ORIG_GATE, OFF_SB), (ORIG_GATE, N_IN, OFF_GATE))
W_IN_SHARD = N_IN // N_CHIPS


def _w_in_from_shards(gathered):
    parts = []
    for lo, hi, al in sorted(W_IN_RUNS, key=lambda r: r[2]):
        if al == OFF_P:
            parts.append(jnp.zeros((D, OFF_P - (OFF_AB + ORIG_GATE - ORIG_AB)), gathered.dtype))
        while lo < hi:
            chip = lo // W_IN_SHARD
            end = min(hi, (chip + 1) * W_IN_SHARD)
            parts.append(gathered[chip, :, lo - chip * W_IN_SHARD:end - chip * W_IN_SHARD])
            lo = end
    return jnp.concatenate(parts, axis=1)


def _w_in_to_shards(g_al):
    shards = []
    for chip in range(N_CHIPS):
        a, b = chip * W_IN_SHARD, (chip + 1) * W_IN_SHARD
        parts = [g_al[:, al + max(a, lo) - lo:al + min(b, hi) - lo] for lo, hi, al in W_IN_RUNS if max(a, lo) < min(b, hi)]
        shards.append(jnp.concatenate(parts, axis=1))
    return jnp.stack(shards)


def _row128(v):
    return jnp.pad(v.reshape(1, -1), ((0, 0), (0, HD - v.shape[-1])))


def _local_step(x, target, gw, w_in_al, sp, emit=None):
    saved = []
    h = x
    for l in range(2):
        h, sv = _layer_fwd(h, l, gw[l], w_in_al[l], sp)
        saved.append(sv)
    loss, dh, g_final = _loss_head(h, sp["final_norm"], target)
    grads = [None, None]
    for l in (1, 0):
        dh, grads[l] = _layer_bwd(dh, l, gw[l], w_in_al[l], sp, saved[l],
                                  None if emit is None else functools.partial(emit, l))
    return loss, dh, grads, g_final


ANY = pl.BlockSpec(memory_space=pl.ANY)


def _me():
    return lax.axis_index("x"), lax.axis_index("y"), lax.axis_index("c")


def _other_chips(x, y):
    return [(1 - x, y), (x, 1 - y), (1 - x, 1 - y)]


def _half(ref, axis, c, rows):
    half = rows // 2
    idx = [slice(None)] * axis + [pl.ds(pl.multiple_of(c * half, 16), half)]
    return ref.at[tuple(idx)]


def _gather_steps(out, send, recv):
    n = len(out)
    x, y, c = _me()
    mine = 2 * x + y
    sibling = (x, y, 1 - c)
    chips = _other_chips(x, y)
    sends = []
    for t in range(n):
        rows = out[t].shape[1]
        for k, (px, py) in enumerate(chips):
            own_half = _half(out[t].at[mine], 0, c, rows)
            cp = pltpu.make_async_remote_copy(
                src_ref=own_half, dst_ref=own_half,
                send_sem=send.at[6 * t + k], recv_sem=recv.at[6 * t + k], device_id=(px, py, c), device_id_type=MESH)
            cp.start()
            sends.append(cp)
    for t in range(n):
        rows = out[t].shape[1]
        for k, (px, py) in enumerate(chips):
            landed = _half(out[t].at[2 * px + py], 0, c, rows)
            pltpu.make_async_remote_copy(
                src_ref=landed, dst_ref=landed, send_sem=send.at[6 * t + k], recv_sem=recv.at[6 * t + k],
                device_id=(px, py, c), device_id_type=MESH).wait_recv()
            cp = pltpu.make_async_remote_copy(
                src_ref=landed, dst_ref=landed, send_sem=send.at[6 * t + 3 + k], recv_sem=recv.at[6 * t + 3 + k],
                device_id=sibling, device_id_type=MESH)
            cp.start()
            sends.append(cp)
    for t in range(n):
        rows = out[t].shape[1]
        for k, (px, py) in enumerate(chips):
            other = _half(out[t].at[2 * px + py], 0, 1 - c, rows)
            pltpu.make_async_remote_copy(
                src_ref=other, dst_ref=other, send_sem=send.at[6 * t + 3 + k], recv_sem=recv.at[6 * t + 3 + k],
                device_id=sibling, device_id_type=MESH).wait_recv()
    for cp in sends:
        cp.wait_send()


def _gather_weights(bufs):
    n = len(bufs)

    def body(*refs):
        _gather_steps(refs[n:2 * n], *refs[2 * n:])

    return pl.pallas_call(
        body, name="gather_weights", in_specs=[ANY] * n, out_specs=[ANY] * n,
        out_shape=[_sds(s.shape, s.dtype) for s in bufs], input_output_aliases={t: t for t in range(n)},
        scratch_shapes=[pltpu.SemaphoreType.DMA((6 * n,)), pltpu.SemaphoreType.DMA((6 * n,))],
    )(*bufs)


def _gather_weights_async(bufs, tag, collective_id):
    n = len(bufs)
    refs = [jax.new_ref(b, memory_space=pltpu.MemorySpace.HBM) for b in bufs]

    @pl.kernel(mesh=plsc.ScalarSubcoreMesh(axis_name="sequencer", num_cores=1), name=f"gather_async_{tag}",
               scratch_types=(pltpu.SemaphoreType.DMA((6 * n,)), pltpu.SemaphoreType.DMA((6 * n,))),
               compiler_params=pltpu.CompilerParams(collective_id=collective_id))
    def launch(send, recv):
        x, y, c = _me()
        barrier = pltpu.get_barrier_semaphore()
        peers = [(x, y, 1 - c)] + [(px, py, c) for px, py in _other_chips(x, y)]
        for peer in peers:
            pl.semaphore_signal(barrier, inc=1, device_id=peer, device_id_type=MESH)
        pl.semaphore_wait(barrier, len(peers))
        _gather_steps(refs, send, recv)

    launch()
    return [r[...] for r in refs]


def _rs_pair(grads):
    n = len(grads)

    def body(*refs):
        g, out = refs[:n], refs[n:2 * n]
        send, recv = refs[2 * n:]
        x, y, c = _me()
        copies = []
        for t in range(n):
            cp = pltpu.make_async_remote_copy(
                src_ref=_half(g[t], 1, 1 - c, g[t].shape[1]), dst_ref=out[t], send_sem=send.at[t], recv_sem=recv.at[t],
                device_id=(x, y, 1 - c), device_id_type=MESH)
            cp.start()
            copies.append(cp)
        for cp in copies:
            cp.wait()

    return pl.pallas_call(
        body, name="rs_pair", in_specs=[ANY] * n, out_specs=[ANY] * n,
        out_shape=[_sds((N_CHIPS, s.shape[1] // 2, s.shape[2]), s.dtype) for s in grads],
        scratch_shapes=[pltpu.SemaphoreType.DMA((n,)), pltpu.SemaphoreType.DMA((n,))],
    )(*grads)


def _rs_chips_steps(p, out, send, recv):
    x, y, c = _me()
    copies = []
    for t in range(len(p)):
        for k, (px, py) in enumerate(_other_chips(x, y)):
            cp = pltpu.make_async_remote_copy(
                src_ref=p[t].at[2 * px + py], dst_ref=out[t].at[k], send_sem=send.at[3 * t + k],
                recv_sem=recv.at[3 * t + k], device_id=(px, py, c), device_id_type=MESH)
            cp.start()
            copies.append(cp)
    for cp in copies:
        cp.wait()


def _rs_chips_async(parts, tag, collective_id):
    n = len(parts)
    src = [jax.new_ref(p, memory_space=pltpu.MemorySpace.HBM) for p in parts]
    got = [jax.empty_ref(_sds((3, *p.shape[1:]), p.dtype), memory_space=pltpu.MemorySpace.HBM) for p in parts]

    @pl.kernel(mesh=plsc.ScalarSubcoreMesh(axis_name="sequencer", num_cores=1), name=f"rs_chips_async_{tag}",
               scratch_types=(pltpu.SemaphoreType.DMA((3 * n,)), pltpu.SemaphoreType.DMA((3 * n,))),
               compiler_params=pltpu.CompilerParams(collective_id=collective_id))
    def launch(send, recv):
        x, y, c = _me()
        barrier = pltpu.get_barrier_semaphore()
        peers = [(px, py, c) for px, py in _other_chips(x, y)]
        for peer in peers:
            pl.semaphore_signal(barrier, inc=1, device_id=peer, device_id_type=MESH)
        pl.semaphore_wait(barrier, len(peers))
        _rs_chips_steps(src, got, send, recv)

    launch()
    return [g[...] for g in got]


def _rs_chips(parts):
    n = len(parts)

    def body(*refs):
        _rs_chips_steps(refs[:n], refs[n:2 * n], *refs[2 * n:])

    return pl.pallas_call(
        body, name="rs_chips", in_specs=[ANY] * n, out_specs=[ANY] * n,
        out_shape=[_sds((3, *s.shape[1:]), s.dtype) for s in parts],
        scratch_shapes=[pltpu.SemaphoreType.DMA((3 * n,)), pltpu.SemaphoreType.DMA((3 * n,))],
    )(*parts)


def _pair_exchange(bufs):
    n = len(bufs)

    def body(*refs):
        out = refs[n:2 * n]
        send, recv = refs[2 * n:]
        x, y, c = _me()
        copies = []
        for t in range(n):
            cp = pltpu.make_async_remote_copy(
                src_ref=out[t].at[c], dst_ref=out[t].at[c], send_sem=send.at[t], recv_sem=recv.at[t],
                device_id=(x, y, 1 - c), device_id_type=MESH)
            cp.start()
            copies.append(cp)
        for t, cp in enumerate(copies):
            cp.wait_send()
            pltpu.make_async_remote_copy(
                src_ref=out[t].at[1 - c], dst_ref=out[t].at[1 - c], send_sem=send.at[t], recv_sem=recv.at[t],
                device_id=(x, y, 1 - c), device_id_type=MESH).wait_recv()

    return pl.pallas_call(
        body, name="pair_exchange", in_specs=[ANY] * n, out_specs=[ANY] * n,
        out_shape=[_sds(s.shape, s.dtype) for s in bufs], input_output_aliases={t: t for t in range(n)},
        scratch_shapes=[pltpu.SemaphoreType.DMA((n,)), pltpu.SemaphoreType.DMA((n,))],
    )(*bufs)


def _row_tile(rows, cols, itemsize, budget=2 * 1024 * 1024):
    tr = rows
    while tr * cols * itemsize > budget and tr % 32 == 0:
        tr //= 2
    return tr


def _sum_pair(name, g, got, where):
    nchip, rows, cols = g.shape
    half = rows // 2
    tr = _row_tile(half, cols, 4)
    per = half // tr

    def body(w_ref, g_ref, r_ref, o_ref):
        o_ref[...] = (g_ref[...].astype(F32) + r_ref[...].astype(F32)).astype(o_ref.dtype)

    blk = pl.BlockSpec((None, tr, cols), lambda j, i, w_ref: (j, i, 0))
    return pl.pallas_call(
        body, name=name,
        grid_spec=pltpu.PrefetchScalarGridSpec(
            num_scalar_prefetch=1, grid=(nchip, per),
            in_specs=[pl.BlockSpec((None, tr, cols), lambda j, i, w_ref: (j, w_ref[1] * per + i, 0)), blk], out_specs=blk),
        out_shape=_sds((nchip, half, cols), BF16), compiler_params=_cp("parallel", "parallel"),
    )(where, g, got)


def _sum_chips(name, p, got, where):
    _, rows, cols = p.shape
    tr = _row_tile(rows, cols, 4)

    def body(w_ref, p_ref, r0, r1, r2, o_ref):
        o_ref[...] = ((p_ref[...].astype(F32) + r0[...].astype(F32)) + r1[...].astype(F32)) + r2[...].astype(F32)

    def got_k(k):
        return pl.BlockSpec((None, tr, cols), lambda i, w_ref: (k, i, 0))

    return pl.pallas_call(
        body, name=name,
        grid_spec=pltpu.PrefetchScalarGridSpec(
            num_scalar_prefetch=1, grid=(rows // tr,),
            in_specs=[pl.BlockSpec((None, tr, cols), lambda i, w_ref: (w_ref[0], i, 0)), got_k(0), got_k(1), got_k(2)],
            out_specs=pl.BlockSpec((None, tr, cols), lambda i, w_ref: (w_ref[1], i, 0))),
        out_shape=_sds((2, rows, cols), F32), compiler_params=_cp("parallel"),
    )(where, p, got, got, got)


def _reduce_scatter(grads, where):
    return _rs_finish(*_rs_begin(grads, where), where)


def _rs_begin(grads, where, tag=None, collective_id=None):
    got = _rs_pair(grads)
    parts = [_sum_pair(f"sum_pair_{t}", g, r, where) for t, (g, r) in enumerate(zip(grads, got))]
    return parts, (_rs_chips(parts) if tag is None else _rs_chips_async(parts, tag, collective_id))


def _rs_finish(parts, got, where):
    halves = [_sum_chips(f"sum_chips_{t}", p, r, where) for t, (p, r) in enumerate(zip(parts, got))]
    return _pair_exchange(halves)


def _all_reduce_small(name, v):
    rows = v.shape[0]

    def body(v_ref, o_ref, land, send, recv):
        x, y, c = _me()
        mine = 4 * x + 2 * y + c
        copies = []
        for k in range(1, 8):
            kx, ky, kc = k >> 2, (k >> 1) & 1, k & 1
            peer = (x ^ kx, y ^ ky, c ^ kc)
            cp = pltpu.make_async_remote_copy(
                src_ref=v_ref, dst_ref=land.at[mine], send_sem=send.at[k - 1], recv_sem=recv.at[k - 1],
                device_id=peer, device_id_type=MESH)
            cp.start()
            copies.append(cp)
        land[mine] = v_ref[...]
        for k in range(1, 8):
            kx, ky, kc = k >> 2, (k >> 1) & 1, k & 1
            src = 4 * (x ^ kx) + 2 * (y ^ ky) + (c ^ kc)
            pltpu.make_async_remote_copy(
                src_ref=v_ref, dst_ref=land.at[src], send_sem=send.at[k - 1], recv_sem=recv.at[k - 1],
                device_id=(x ^ kx, y ^ ky, c ^ kc), device_id_type=MESH).wait_recv()
        acc = land[0]
        for d in range(1, 8):
            acc = acc + land[d]
        o_ref[...] = acc
        for cp in copies:
            cp.wait_send()

    vm = pl.BlockSpec(memory_space=pltpu.VMEM)
    return pl.pallas_call(
        body, name=name, in_specs=[vm], out_specs=vm, out_shape=_sds((rows, 128), F32),
        scratch_shapes=[pltpu.VMEM((8, rows, 128), F32), pltpu.SemaphoreType.DMA((7,)), pltpu.SemaphoreType.DMA((7,))],
    )(v)


def _adamw(name, w, g, m, v):
    rows, cols = w.shape
    tr = _row_tile(rows, cols, 4, budget=1024 * 1024)
    c1 = 1.0 / (1.0 - ADAM_B1 ** ADAM_STEP)
    c2 = 1.0 / (1.0 - ADAM_B2 ** ADAM_STEP)

    def body(w_ref, g_ref, m_ref, v_ref, d_ref, nm_ref, nv_ref):
        gv = g_ref[...]
        nm = ADAM_B1 * m_ref[...] + (1.0 - ADAM_B1) * gv
        nv = ADAM_B2 * v_ref[...] + (1.0 - ADAM_B2) * (gv * gv)
        d_ref[...] = -ADAM_LR * ((nm * c1) / (jnp.sqrt(nv * c2) + ADAM_EPS) + ADAM_WD * w_ref[...])
        nm_ref[...] = nm
        nv_ref[...] = nv

    blk = pl.BlockSpec((tr, cols), lambda i: (i, 0))
    return pl.pallas_call(
        body, name=name, grid=(rows // tr,), in_specs=[blk] * 4, out_specs=[blk] * 3,
        out_shape=[_sds((rows, cols), F32)] * 3, compiler_params=_cp("parallel"),
    )(w, g, m, v)


def _adamw_layers(name, w, g0, g1, m, v):
    _, half, cols = g0.shape
    tr = _row_tile(half, cols, 4, budget=1024 * 1024)
    per_half = half // tr
    per = 2 * per_half
    c1 = 1.0 / (1.0 - ADAM_B1 ** ADAM_STEP)
    c2 = 1.0 / (1.0 - ADAM_B2 ** ADAM_STEP)

    def body(w_ref, g0_ref, g1_ref, m_ref, v_ref, g_ref, d_ref, nm_ref, nv_ref):
        gv = jnp.where(pl.program_id(0) == 0, g0_ref[...], g1_ref[...])
        nm = ADAM_B1 * m_ref[...] + (1.0 - ADAM_B1) * gv
        nv = ADAM_B2 * v_ref[...] + (1.0 - ADAM_B2) * (gv * gv)
        g_ref[...] = gv
        d_ref[...] = -ADAM_LR * ((nm * c1) / (jnp.sqrt(nv * c2) + ADAM_EPS) + ADAM_WD * w_ref[...])
        nm_ref[...] = nm
        nv_ref[...] = nv

    both = pl.BlockSpec((None, tr, cols), lambda l, i: (l, i, 0))

    def halves(i):
        return i // per_half, i % per_half, 0

    first = pl.BlockSpec((None, tr, cols), lambda l, i: halves(i * (1 - l) + (per - 1) * l))
    second = pl.BlockSpec((None, tr, cols), lambda l, i: halves(i * l))
    return pl.pallas_call(
        body, name=name, grid=(2, per), in_specs=[both, first, second, both, both], out_specs=[both] * 4,
        out_shape=[_sds(w.shape, F32)] * 4, compiler_params=_cp("arbitrary", "arbitrary"),
    )(w, g0, g1, m, v)


def _to_bf16_slot(name, w, l, where):
    _, rows, cols = w.shape
    tr = _row_tile(rows, cols, 4)

    def body(w_ref, x_ref, o_ref):
        o_ref[...] = x_ref[...].astype(BF16)

    return pl.pallas_call(
        body, name=name,
        grid_spec=pltpu.PrefetchScalarGridSpec(
            num_scalar_prefetch=1, grid=(rows // tr,), in_specs=[pl.BlockSpec((None, tr, cols), lambda i, w_ref: (l, i, 0))],
            out_specs=pl.BlockSpec((None, tr, cols), lambda i, w_ref: (w_ref[0], i, 0))),
        out_shape=_sds((N_CHIPS, rows, cols), BF16), compiler_params=_cp("parallel"))(where, w)


BIG = ("w_in", "w_pool_up", "w_sb_up", "w_gdn_up", "w_out", "w_ff1", "w_ff2")
SMALL = (("attn_norm", (D,)), ("pool_w", (4, 128, 128)), ("pool_scale", (POOL_W,)), ("gdn_a_log", (HEADS,)),
         ("gdn_dt_bias", (HEADS,)), ("gdn_norm", (HD,)), ("mlp_norm", (D,)))


PACK_TILE = 8 * 128


def _rows128(a):
    flat = a.reshape(-1)
    pad = (-flat.shape[0]) % PACK_TILE
    return jnp.pad(flat, (0, pad)).reshape(-1, 128)


def _pack(parts):
    packed = jnp.concatenate([_rows128(p) for p in parts], axis=0)
    return jnp.pad(packed, ((0, (-packed.shape[0]) % 8), (0, 0)))


def _unpack(packed, shapes):
    out, r = [], 0
    for shp in shapes:
        size = 1
        for s in shp:
            size *= s
        nr = -(-size // PACK_TILE) * 8
        out.append(packed[r:r + nr].reshape(-1)[:size].reshape(shp))
        r += nr
    return out


def kernel(x, attn_norm, w_in, pool_w, pool_scale, gdn_conv, gdn_a_log, gdn_dt_bias, gdn_norm, w_pool_up, w_sb_up, w_gdn_up, w_out, mlp_norm, w_ff1, w_ff2, final_norm, loss_target, m_attn_norm, m_w_in, m_pool_w, m_pool_scale, m_gdn_conv, m_gdn_a_log, m_gdn_dt_bias, m_gdn_norm, m_w_pool_up, m_w_sb_up, m_w_gdn_up, m_w_out, m_mlp_norm, m_w_ff1, m_w_ff2, m_final_norm, v_attn_norm, v_w_in, v_pool_w, v_pool_scale, v_gdn_conv, v_gdn_a_log, v_gdn_dt_bias, v_gdn_norm, v_w_pool_up, v_w_sb_up, v_w_gdn_up, v_w_out, v_mlp_norm, v_w_ff1, v_w_ff2, v_final_norm):
    weights = dict(attn_norm=attn_norm, w_in=w_in, pool_w=pool_w, pool_scale=pool_scale, gdn_conv=gdn_conv,
                   gdn_a_log=gdn_a_log, gdn_dt_bias=gdn_dt_bias, gdn_norm=gdn_norm, w_pool_up=w_pool_up, w_sb_up=w_sb_up,
                   w_gdn_up=w_gdn_up, w_out=w_out, mlp_norm=mlp_norm, w_ff1=w_ff1, w_ff2=w_ff2, final_norm=final_norm)
    mom1 = dict(attn_norm=m_attn_norm, w_in=m_w_in, pool_w=m_pool_w, pool_scale=m_pool_scale, gdn_conv=m_gdn_conv,
                gdn_a_log=m_gdn_a_log, gdn_dt_bias=m_gdn_dt_bias, gdn_norm=m_gdn_norm, w_pool_up=m_w_pool_up,
                w_sb_up=m_w_sb_up, w_gdn_up=m_w_gdn_up, w_out=m_w_out, mlp_norm=m_mlp_norm, w_ff1=m_w_ff1, w_ff2=m_w_ff2,
                final_norm=m_final_norm)
    mom2 = dict(attn_norm=v_attn_norm, w_in=v_w_in, pool_w=v_pool_w, pool_scale=v_pool_scale, gdn_conv=v_gdn_conv,
                gdn_a_log=v_gdn_a_log, gdn_dt_bias=v_gdn_dt_bias, gdn_norm=v_gdn_norm, w_pool_up=v_w_pool_up,
                w_sb_up=v_w_sb_up, w_gdn_up=v_w_gdn_up, w_out=v_w_out, mlp_norm=v_mlp_norm, w_ff1=v_w_ff1, w_ff2=v_w_ff2,
                final_norm=v_final_norm)
    xi, yi, ci = lax.axis_index("x"), lax.axis_index("y"), lax.axis_index("c")
    chip = 2 * xi + yi
    where = jnp.stack([chip, ci]).astype(jnp.int32)

    bufs = [[_to_bf16_slot(f"cast_{nm}_{l}", weights[nm], l, where) for nm in BIG] for l in range(2)]
    first = _gather_weights(bufs[0][:1])
    (rest, later), _ = lax.optimization_barrier(((bufs[0][1:], bufs[1]), first))
    gw = [dict(zip(BIG, list(first) + _gather_weights_async(rest, "rest", 1))),
          dict(zip(BIG, _gather_weights_async(later, "next", 2)))]
    conv_cols = gdn_conv.shape[-1]
    conv_place = lax.dynamic_update_slice(jnp.zeros((2, GDN_CONV, N_CHIPS * conv_cols), F32),
                                          jnp.where(ci == 0, gdn_conv, 0.0), (0, 0, chip * conv_cols))
    conv_full = _all_reduce_small("gather_conv", _rows128(conv_place)).reshape(2, GDN_CONV, N_CHIPS * conv_cols)
    w_in_al = [_w_in_from_shards(gw[l]["w_in"]) for l in range(2)]
    sp = dict(attn_norm=attn_norm.reshape(2, 1, D), pool_w=pool_w, pool_scale=pool_scale.reshape(2, 1, POOL_W),
              conv=conv_full, a_log=jnp.stack([_row128(gdn_a_log[l]) for l in range(2)]),
              dt_bias=jnp.stack([_row128(gdn_dt_bias[l]) for l in range(2)]), gdn_norm=gdn_norm.reshape(2, 1, HD),
              mlp_norm=mlp_norm.reshape(2, 1, D), final_norm=final_norm.reshape(1, D))

    started = []

    def emit(l, names, g, v):
        last = l == 0 and names == ("w_in",)
        tag = None if last else f"{l}_{names[0]}"
        parts, got = _rs_begin([g[nm] for nm in names], where, tag, 3 + len(started))
        started.append((l, names, parts, got, last))
        if not last:
            v, _ = lax.optimization_barrier((v, parts))
        return v

    loss, grad_x, grads, g_final = _local_step(x[0], loss_target[0], gw, w_in_al, sp, emit)
    loss = lax.psum(loss[0, 0], ("x", "y", "c"))
    big_grads = {nm: [None, None] for nm in BIG}
    for l, names, parts, got, last in started:
        if not last:
            got, _ = lax.optimization_barrier((got, grad_x))
        for nm, red in zip(names, _rs_finish(parts, got, where)):
            big_grads[nm][l] = red
    small_parts, small_shapes = [], []
    for l in range(2):
        g = grads[l]
        for nm, shp in SMALL:
            key = {"gdn_a_log": "a_log", "gdn_dt_bias": "dt_bias"}.get(nm, nm)
            val = g[key]
            small_parts.append(val[0, :HEADS] if nm in ("gdn_a_log", "gdn_dt_bias") else val)
            small_shapes.append(shp)
        small_parts.append(g["conv"])
        small_shapes.append((GDN_CONV, N_CHIPS * conv_cols))
    small_parts.append(g_final)
    small_shapes.append((D,))
    reduced = _unpack(_all_reduce_small("reduce_small", _pack(small_parts)), small_shapes)
    per = len(SMALL) + 1
    grad = {}
    for i, (nm, _) in enumerate(SMALL):
        grad[nm] = jnp.stack([reduced[i], reduced[per + i]])
    conv_g = jnp.stack([reduced[per - 1], reduced[2 * per - 1]])
    grad["gdn_conv"] = lax.dynamic_slice(conv_g, (0, 0, chip * conv_cols), (2, GDN_CONV, conv_cols))
    grad["final_norm"] = reduced[-1]

    delta, new_m, new_v = {}, {}, {}
    for nm in BIG:
        grad[nm], delta[nm], new_m[nm], new_v[nm] = _adamw_layers("adamw_" + nm, weights[nm], *big_grads[nm], mom1[nm], mom2[nm])
    small_names = [nm for nm, _ in SMALL] + ["gdn_conv", "final_norm"]
    packs = [_pack([src[nm] for nm in small_names]) for src in (weights, grad, mom1, mom2)]
    outs = _adamw("adamw_small", *packs)
    shapes = [weights[nm].shape for nm in small_names]
    for dst, packed in zip((delta, new_m, new_v), outs):
        for nm, val in zip(small_names, _unpack(packed, shapes)):
            dst[nm] = val

    order = ("attn_norm", "w_in", "pool_w", "pool_scale", "gdn_conv", "gdn_a_log", "gdn_dt_bias", "gdn_norm", "w_pool_up",
             "w_sb_up", "w_gdn_up", "w_out", "mlp_norm", "w_ff1", "w_ff2", "final_norm")
    return (loss, grad_x[None], *[grad[n] for n in order], *[delta[n] for n in order], *[new_m[n] for n in order],
            *[new_v[n] for n in order])
```

```python
import functools

import jax
import jax.numpy as jnp
from jax import lax
from jax.experimental import pallas as pl
from jax.experimental.pallas import tpu as pltpu
from jax.experimental.pallas import tpu_sc as plsc

F32, BF16 = jnp.float32, jnp.bfloat16
HIGH = lax.Precision.HIGH
MESH = pl.DeviceIdType.MESH

D = 2048
EPS = 1e-6
POOL_WINDOWS = (2, 4, 8, 16)
POOL_W, SB_W, GDN_W = 512, 768, 768
HEADS, HD = 6, 128
SB_BLOCK = 128
GDN_CHUNK = 64
D_FF = 4 * D
N_IN = 12044
N_CHIPS = 4
OFF_SB, OFF_GQKV, OFF_Z, OFF_AB, OFF_P, OFF_GATE = 0, 2304, 4608, 5376, 5632, 6144
AB_W = 256
ORIG_SB, ORIG_AB, ORIG_GATE = 512, 5888, 5900
N_AL = 12288
VMEM_LIMIT = 48 * 1024 * 1024

ADAM_LR, ADAM_B1, ADAM_B2, ADAM_EPS, ADAM_WD, ADAM_STEP = 0.001, 0.9, 0.999, 1e-08, 0.01, 10

NT = (((1,), (1,)), ((), ()))
TN = (((0,), (0,)), ((), ()))


def _cp(*sem):
    return pltpu.CompilerParams(dimension_semantics=sem, vmem_limit_bytes=VMEM_LIMIT)


def _dot(a, b, dims=None, precision=None):
    if dims is None:
        dims = (((a.ndim - 1,), (0,)), ((), ()))
    return lax.dot_general(a, b, dims, precision=precision, preferred_element_type=F32)


def _hdot(a, b, dims=None):
    return _dot(a, b, dims, precision=HIGH)


def _bdot(a, b, dims=None):
    return _dot(a.astype(BF16), b.astype(BF16), dims)


def _mm(name, a, b, *, m, n, k, tm, tn, tk, a_spec, b_spec, dims, out_shapes, out_specs,
        extras=(), extra_specs=(), epilogue=None):
    nk = k // tk
    ne, no = len(extras), len(out_shapes)

    def body(*refs):
        a_ref, b_ref = refs[0], refs[1]
        ex = refs[2:2 + ne]
        outs = refs[2 + ne:2 + ne + no]
        acc = refs[-1]
        kk = pl.program_id(2)

        @pl.when(kk == 0)
        def _():
            acc[...] = jnp.zeros_like(acc)

        acc[...] += _dot(a_ref[...].astype(BF16), b_ref[...].astype(BF16), dims)

        @pl.when(kk == nk - 1)
        def _():
            r = acc[...]
            res = epilogue(r, *[e[...] for e in ex]) if epilogue is not None else (r,)
            for o, v in zip(outs, res):
                o[...] = v.astype(o.dtype)

    return pl.pallas_call(
        body, name=name, grid=(m // tm, n // tn, nk),
        in_specs=[a_spec, b_spec, *extra_specs], out_specs=out_specs, out_shape=out_shapes,
        scratch_shapes=[pltpu.VMEM((tm, tn), F32)],
        compiler_params=_cp("parallel", "parallel", "arbitrary"),
    )(a, b, *extras)


def _a_plain(tm, tk):
    return pl.BlockSpec((tm, tk), lambda i, j, kk: (i, kk))


def _a_trans(tm, tk):
    return pl.BlockSpec((tk, tm), lambda i, j, kk: (kk, i))


def _b_plain(tk, tn):
    return pl.BlockSpec((tk, tn), lambda i, j, kk: (kk, j))


def _b_trans(tk, tn):
    return pl.BlockSpec((tn, tk), lambda i, j, kk: (j, kk))


def _o_plain(tm, tn):
    return pl.BlockSpec((tm, tn), lambda i, j, kk: (i, j))


def _o_colshard(tm, tn, ns_cols):
    per = ns_cols // tn
    return pl.BlockSpec((None, tm, tn), lambda i, j, kk: (j // per, i, j % per))


def _w_cols(tk, tn, ns):
    per = ns // tn
    return pl.BlockSpec((None, tk, tn), lambda i, j, kk: (j // per, kk, j % per))


def _w_cols_t(tk, tn, ns):
    per = ns // tk
    return pl.BlockSpec((None, tn, tk), lambda i, j, kk: (kk // per, j, kk % per))


def _w_rows(tk, tn, ks):
    per = ks // tk
    return pl.BlockSpec((None, tk, tn), lambda i, j, kk: (kk // per, kk % per, j))


def _w_rows_t(tk, tn, ks):
    per = ks // tn
    return pl.BlockSpec((None, tn, tk), lambda i, j, kk: (j // per, j % per, kk))


def _sds(shape, dtype):
    return jax.ShapeDtypeStruct(shape, dtype)


def _rms_fwd(name, x, gain):
    t = x.shape[0]
    tt = min(256, t)

    def body(x_ref, g_ref, u_ref):
        xv = x_ref[...]
        r = lax.rsqrt(jnp.mean(xv * xv, axis=-1, keepdims=True) + EPS)
        u_ref[...] = (xv * r * g_ref[...]).astype(u_ref.dtype)

    return pl.pallas_call(
        body, name=name, grid=(t // tt,),
        in_specs=[pl.BlockSpec((tt, D), lambda i: (i, 0)), pl.BlockSpec((1, D), lambda i: (0, 0))],
        out_specs=pl.BlockSpec((tt, D), lambda i: (i, 0)), out_shape=_sds((t, D), BF16),
        compiler_params=_cp("parallel"),
    )(x, gain)


def _rms_bwd(name, du, x, gain, dres):
    t = x.shape[0]
    tt = min(256, t)

    def body(du_ref, x_ref, g_ref, dres_ref, dx_ref, dg_ref):
        @pl.when(pl.program_id(0) == 0)
        def _():
            dg_ref[...] = jnp.zeros_like(dg_ref)

        xv, duv = x_ref[...], du_ref[...]
        r = lax.rsqrt(jnp.mean(xv * xv, axis=-1, keepdims=True) + EPS)
        nx = xv * r
        dn = duv * g_ref[...]
        dg_ref[...] += jnp.sum(duv * nx, axis=0, keepdims=True)
        dx_ref[...] = dres_ref[...] + r * (dn - nx * jnp.mean(dn * nx, axis=-1, keepdims=True))

    row = pl.BlockSpec((tt, D), lambda i: (i, 0))
    vec = pl.BlockSpec((1, D), lambda i: (0, 0))
    return pl.pallas_call(
        body, name=name, grid=(t // tt,), in_specs=[row, row, vec, row], out_specs=[row, vec],
        out_shape=[_sds((t, D), F32), _sds((1, D), F32)], compiler_params=_cp("arbitrary"),
    )(du, x, gain, dres)


def _loss_head(x, gain, target):
    t = x.shape[0]
    tt = min(256, t)

    def body(x_ref, g_ref, t_ref, loss_ref, dx_ref, dg_ref):
        @pl.when(pl.program_id(0) == 0)
        def _():
            dg_ref[...] = jnp.zeros_like(dg_ref)
            loss_ref[...] = jnp.zeros_like(loss_ref)

        xv = x_ref[...]
        r = lax.rsqrt(jnp.mean(xv * xv, axis=-1, keepdims=True) + EPS)
        nx = xv * r
        err = nx * g_ref[...] - t_ref[...]
        loss_ref[...] += 0.5 * jnp.sum(jnp.mean(err * err, axis=-1, keepdims=True), axis=0, keepdims=True)
        dy = err * (1.0 / D)
        dn = dy * g_ref[...]
        dg_ref[...] += jnp.sum(dy * nx, axis=0, keepdims=True)
        dx_ref[...] = r * (dn - nx * jnp.mean(dn * nx, axis=-1, keepdims=True))

    row = pl.BlockSpec((tt, D), lambda i: (i, 0))
    vec = pl.BlockSpec((1, D), lambda i: (0, 0))
    one = pl.BlockSpec((1, 1), lambda i: (0, 0))
    return pl.pallas_call(
        body, name="loss_head", grid=(t // tt,), in_specs=[row, vec, row], out_specs=[one, row, vec],
        out_shape=[_sds((1, 1), F32), _sds((t, D), F32), _sds((1, D), F32)], compiler_params=_cp("arbitrary"),
    )(x, gain, target)


def _shift_down(v, s, t_idx):
    return jnp.where(t_idx >= s, pltpu.roll(v, s, 0), 0.0)


def _shift_up(v, s, t_idx, t):
    return jnp.where(t_idx < t - s, pltpu.roll(v, t - s, 0), 0.0)


def _pool_d(p, g, t_idx):
    s = p
    for step in range(g + 1):
        s = s + _shift_down(s, 1 << step, t_idx)
    cnt = jnp.minimum(t_idx + 1, POOL_WINDOWS[g]).astype(F32)
    return s / cnt - p, cnt


def _pool_fwd(proj, pool_w, pool_scale):
    t = proj.shape[0]
    g128 = POOL_W // len(POOL_WINDOWS)

    def body(p_ref, w_ref, s_ref, y_ref):
        t_idx = lax.broadcasted_iota(jnp.int32, (t, g128), 0)
        for g in range(len(POOL_WINDOWS)):
            sl = slice(g * g128, (g + 1) * g128)
            d, _ = _pool_d(p_ref[:, sl], g, t_idx)
            y_ref[:, sl] = (_bdot(d, w_ref[g]) * s_ref[:, sl]).astype(y_ref.dtype)

    return pl.pallas_call(
        body, name="pool_fwd", grid=(1,),
        in_specs=[pl.BlockSpec((t, POOL_W), lambda i: (0, OFF_P // POOL_W)),
                  pl.BlockSpec((4, g128, g128), lambda i: (0, 0, 0)), pl.BlockSpec((1, POOL_W), lambda i: (0, 0))],
        out_specs=pl.BlockSpec((t, POOL_W), lambda i: (0, 0)), out_shape=_sds((t, POOL_W), BF16),
        compiler_params=_cp("arbitrary"),
    )(proj, pool_w, pool_scale)


def _pool_bwd(proj, pool_w, pool_scale, dy):
    t = proj.shape[0]
    g128 = POOL_W // len(POOL_WINDOWS)

    def body(p_ref, w_ref, s_ref, dy_ref, dp_ref, dw_ref, ds_ref):
        t_idx = lax.broadcasted_iota(jnp.int32, (t, g128), 0)
        for g in range(len(POOL_WINDOWS)):
            sl = slice(g * g128, (g + 1) * g128)
            d, cnt = _pool_d(p_ref[:, sl], g, t_idx)
            dyv = dy_ref[:, sl].astype(F32)
            ds_ref[:, sl] = jnp.sum(dyv * _bdot(d, w_ref[g]), axis=0, keepdims=True)
            dys = dyv * s_ref[:, sl]
            dw_ref[g] = _bdot(d, dys, TN)
            dd = _bdot(dys, w_ref[g], NT)
            s = dd / cnt
            for step in range(g + 1):
                s = s + _shift_up(s, 1 << step, t_idx, t)
            dp_ref[:, sl] = (s - dd).astype(dp_ref.dtype)

    return pl.pallas_call(
        body, name="pool_bwd", grid=(1,),
        in_specs=[pl.BlockSpec((t, POOL_W), lambda i: (0, OFF_P // POOL_W)),
                  pl.BlockSpec((4, g128, g128), lambda i: (0, 0, 0)), pl.BlockSpec((1, POOL_W), lambda i: (0, 0)),
                  pl.BlockSpec((t, POOL_W), lambda i: (0, 0))],
        out_specs=[pl.BlockSpec((t, POOL_W), lambda i: (0, 0)), pl.BlockSpec((4, g128, g128), lambda i: (0, 0, 0)),
                   pl.BlockSpec((1, POOL_W), lambda i: (0, 0))],
        out_shape=[_sds((t, POOL_W), BF16), _sds((4, g128, g128), F32), _sds((1, POOL_W), F32)],
        compiler_params=_cp("arbitrary"),
    )(proj, pool_w, pool_scale, dy)


SB_GROUP = 3
SB_GW = SB_GROUP * HD


def _sb_cast_kv(proj):
    t = proj.shape[0]
    tt = min(512, t)

    def body(x_ref, o_ref):
        o_ref[...] = x_ref[...].astype(BF16)

    return pl.pallas_call(
        body, name="sb_cast_kv", grid=(t // tt, 2),
        in_specs=[pl.BlockSpec((tt, SB_W), lambda i, j: (i, OFF_SB // SB_W + 1 + j))],
        out_specs=pl.BlockSpec((tt, SB_W), lambda i, j: (i, j)), out_shape=_sds((t, 2 * SB_W), BF16),
        compiler_params=_cp("parallel", "parallel"),
    )(proj)


def _sb_specs(t):
    q_spec = pl.BlockSpec((SB_BLOCK, SB_GW), lambda g, i: (i, OFF_SB // SB_GW + g))
    k_spec = pl.BlockSpec((t, SB_GW), lambda g, i: (0, g))
    v_spec = pl.BlockSpec((t, SB_GW), lambda g, i: (0, SB_W // SB_GW + g))
    return q_spec, k_spec, v_spec


def _head(ref, h, rows=None):
    cols = slice(h * HD, (h + 1) * HD)
    return ref[:, cols] if rows is None else ref[rows, cols]


SB_KEYS = 512


def _sub(v, b):
    return v[:, b * SB_BLOCK:(b + 1) * SB_BLOCK]


def _sb_keep(kc, limit):
    row = lax.broadcasted_iota(jnp.int32, (SB_BLOCK, kc), 0)
    col = lax.broadcasted_iota(jnp.int32, (SB_BLOCK, kc), 1)
    return col < row + limit


def _sb_chunk(q, keys, run, later, limit):
    kc = keys.shape[0]
    z = _dot(q, keys, NT)
    lsz = jax.nn.log_sigmoid(z)
    ls = lsz - z
    if limit is not None:
        keep = _sb_keep(kc, limit)
        ls = jnp.where(keep, ls, 0.0)
    parts = [None] * (kc // SB_BLOCK)
    for b in reversed(range(kc // SB_BLOCK)):
        parts[b] = _hdot(_sub(ls, b), later) + run
        run = run + jnp.sum(_sub(ls, b), axis=1, keepdims=True)
    a = jnp.exp(lsz + jnp.concatenate(parts, axis=1))
    if limit is not None:
        a = jnp.where(keep, a, 0.0)
    return z, a, run


def _sb_fwd(proj, kv):
    t = proj.shape[0]
    kc = min(SB_KEYS, t)
    scale = HD ** -0.5

    def body(q_ref, k_ref, v_ref, o_ref):
        i = pl.program_id(1)
        top = (i * SB_BLOCK) // kc
        qs = [(_head(q_ref, h) * scale).astype(BF16) for h in range(SB_GROUP)]
        row = lax.broadcasted_iota(jnp.int32, (SB_BLOCK, SB_BLOCK), 0)
        col = lax.broadcasted_iota(jnp.int32, (SB_BLOCK, SB_BLOCK), 1)
        later = (row > col).astype(F32)

        def chunk(jc, carry, masked):
            rows = pl.ds(pl.multiple_of(jc * kc, kc), kc)
            limit = i * SB_BLOCK - jc * kc if masked else None
            out = []
            for h in range(SB_GROUP):
                acc, run = carry[h]
                _, a, run = _sb_chunk(qs[h], _head(k_ref, h, rows), run, later, limit)
                out.append((acc + _dot(a.astype(BF16), _head(v_ref, h, rows)), run))
            return tuple(out)

        zero = tuple((jnp.zeros((SB_BLOCK, HD), F32), jnp.zeros((SB_BLOCK, 1), F32)) for _ in range(SB_GROUP))
        carry = chunk(top, zero, True)
        carry = lax.fori_loop(0, top, lambda jj, c: chunk(top - 1 - jj, c, False), carry)
        for h in range(SB_GROUP):
            o_ref[:, h * HD:(h + 1) * HD] = carry[h][0].astype(o_ref.dtype)

    return pl.pallas_call(
        body, name="sb_fwd", grid=(HEADS // SB_GROUP, t // SB_BLOCK), in_specs=list(_sb_specs(t)),
        out_specs=pl.BlockSpec((SB_BLOCK, SB_GW), lambda g, i: (i, g)), out_shape=_sds((t, SB_W), BF16),
        compiler_params=_cp("parallel", "arbitrary"),
    )(proj, kv, kv)


def _sb_bwd(proj, kv, dy):
    t = proj.shape[0]
    nq = t // SB_BLOCK
    kc = min(SB_KEYS, t)
    scale = HD ** -0.5

    def body(q_ref, k_ref, v_ref, do_ref, dq_ref, dk_ref, dv_ref, z_scr, e_scr):
        i = pl.program_id(1)
        top = (i * SB_BLOCK) // kc

        @pl.when(i == 0)
        def _():
            dk_ref[...] = jnp.zeros_like(dk_ref)
            dv_ref[...] = jnp.zeros_like(dv_ref)

        qs = [(_head(q_ref, h) * scale).astype(BF16) for h in range(SB_GROUP)]
        dos = [_head(do_ref, h).astype(BF16) for h in range(SB_GROUP)]
        row = lax.broadcasted_iota(jnp.int32, (SB_BLOCK, SB_BLOCK), 0)
        col = lax.broadcasted_iota(jnp.int32, (SB_BLOCK, SB_BLOCK), 1)
        later = (row > col).astype(F32)
        earlier = (row < col).astype(F32)

        def down(jc, runs, masked):
            rows = pl.ds(pl.multiple_of(jc * kc, kc), kc)
            limit = i * SB_BLOCK - jc * kc if masked else None
            out = []
            for h in range(SB_GROUP):
                z, a, run = _sb_chunk(qs[h], _head(k_ref, h, rows), runs[h], later, limit)
                z_scr[h, jc] = z
                e_scr[h, jc] = a * _dot(dos[h], _head(v_ref, h, rows), NT)
                dv_ref[rows, h * HD:(h + 1) * HD] += _dot(a.astype(BF16), dos[h], TN)
                out.append(run)
            return tuple(out)

        zero = tuple(jnp.zeros((SB_BLOCK, 1), F32) for _ in range(SB_GROUP))
        runs = down(top, zero, True)
        lax.fori_loop(0, top, lambda jj, r: down(top - 1 - jj, r, False), runs)

        def up(jc, carry, masked):
            rows = pl.ds(pl.multiple_of(jc * kc, kc), kc)
            out = []
            for h in range(SB_GROUP):
                dq, run = carry[h]
                z, e = z_scr[h, jc], e_scr[h, jc]
                parts = []
                for b in range(kc // SB_BLOCK):
                    parts.append(_hdot(_sub(e, b), earlier) + run)
                    run = run + jnp.sum(_sub(e, b), axis=1, keepdims=True)
                sz = jax.nn.sigmoid(z)
                dz = e * (1.0 - sz) - jnp.concatenate(parts, axis=1) * sz
                if masked:
                    dz = jnp.where(_sb_keep(kc, i * SB_BLOCK - jc * kc), dz, 0.0)
                dz = dz.astype(BF16)
                dk_ref[rows, h * HD:(h + 1) * HD] += _dot(dz, qs[h], TN)
                out.append((dq + _dot(dz, _head(k_ref, h, rows)), run))
            return tuple(out)

        zero = tuple((jnp.zeros((SB_BLOCK, HD), F32), jnp.zeros((SB_BLOCK, 1), F32)) for _ in range(SB_GROUP))
        carry = lax.fori_loop(0, top, lambda jc, c: up(jc, c, False), zero)
        carry = up(top, carry, True)
        for h in range(SB_GROUP):
            dq_ref[:, h * HD:(h + 1) * HD] = (carry[h][0] * scale).astype(dq_ref.dtype)

    blk = pl.BlockSpec((SB_BLOCK, SB_GW), lambda g, i: (i, g))
    seq = pl.BlockSpec((t, SB_GW), lambda g, i: (0, g))
    scratch = pltpu.VMEM((SB_GROUP, t // kc, SB_BLOCK, kc), F32)
    return pl.pallas_call(
        body, name="sb_bwd", grid=(HEADS // SB_GROUP, nq), in_specs=[*_sb_specs(t), blk], out_specs=[blk, seq, seq],
        out_shape=[_sds((t, SB_W), BF16), _sds((t, SB_W), F32), _sds((t, SB_W), F32)],
        scratch_shapes=[scratch, scratch], compiler_params=_cp("parallel", "arbitrary"),
    )(proj, kv, kv, dy)


CONV_TILE = 256
GDN_CONV = 4


def _conv_pre(x, w_ref, t_idx):
    pre = w_ref[GDN_CONV - 1:GDN_CONV, :] * x
    for s in range(1, GDN_CONV):
        pre = pre + w_ref[GDN_CONV - 1 - s:GDN_CONV - s, :] * _shift_down(x, s, t_idx)
    return pre


def _conv_fwd(proj, conv_w):
    t = proj.shape[0]
    width = conv_w.shape[1]

    def body(x_ref, w_ref, y_ref):
        t_idx = lax.broadcasted_iota(jnp.int32, (t, CONV_TILE), 0)
        pre = _conv_pre(x_ref[...], w_ref, t_idx)
        y_ref[...] = pre * jax.nn.sigmoid(pre)

    return pl.pallas_call(
        body, name="conv_fwd", grid=(width // CONV_TILE,),
        in_specs=[pl.BlockSpec((t, CONV_TILE), lambda c: (0, OFF_GQKV // CONV_TILE + c)),
                  pl.BlockSpec((GDN_CONV, CONV_TILE), lambda c: (0, c))],
        out_specs=pl.BlockSpec((t, CONV_TILE), lambda c: (0, c)), out_shape=_sds((t, width), F32),
        compiler_params=_cp("parallel"),
    )(proj, conv_w)


def _conv_bwd(proj, conv_w, dc):
    t = proj.shape[0]
    width = dc.shape[1]
    per = width // CONV_TILE
    part = 0

    def body(x_ref, w_ref, dc_ref, dx_ref, dw_ref):
        t_idx = lax.broadcasted_iota(jnp.int32, (t, CONV_TILE), 0)
        x = x_ref[...]
        pre = _conv_pre(x, w_ref, t_idx)
        sg = jax.nn.sigmoid(pre)
        dpre = dc_ref[...] * (sg * (1.0 + pre * (1.0 - sg)))
        dx = w_ref[GDN_CONV - 1:GDN_CONV, :] * dpre
        dw_ref[GDN_CONV - 1:GDN_CONV, :] = jnp.sum(dpre * x, axis=0, keepdims=True)
        for s in range(1, GDN_CONV):
            dx = dx + w_ref[GDN_CONV - 1 - s:GDN_CONV - s, :] * _shift_up(dpre, s, t_idx, t)
            dw_ref[GDN_CONV - 1 - s:GDN_CONV - s, :] = jnp.sum(dpre * _shift_down(x, s, t_idx), axis=0, keepdims=True)
        dx_ref[...] = dx.astype(dx_ref.dtype)

    return pl.pallas_call(
        body, name="conv_bwd", grid=(per,),
        in_specs=[pl.BlockSpec((t, CONV_TILE), lambda c: (0, OFF_GQKV // CONV_TILE + part * per + c)),
                  pl.BlockSpec((GDN_CONV, CONV_TILE), lambda c: (0, part * per + c)),
                  pl.BlockSpec((t, CONV_TILE), lambda c: (0, c))],
        out_specs=[pl.BlockSpec((t, CONV_TILE), lambda c: (0, c)), pl.BlockSpec((GDN_CONV, CONV_TILE), lambda c: (0, c))],
        out_shape=[_sds((t, width), BF16), _sds((GDN_CONV, width), F32)],
        compiler_params=_cp("parallel"),
    )(proj, conv_w, dc)


def _heads(x):
    return jnp.concatenate([x[:, h * HD:(h + 1) * HD][None] for h in range(HEADS)], axis=0)


def _hb(a, b, ca=2, cb=1):
    return lax.dot_general(a, b, (((ca,), (cb,)), ((0,), (0,))), precision=HIGH, preferred_element_type=F32)


def _gdn_prep(cq, ck, cv, ab, alog_row, dtb_row):
    c = GDN_CHUNK
    row = lax.broadcasted_iota(jnp.int32, (c, c), 0)
    col = lax.broadcasted_iota(jnp.int32, (c, c), 1)
    incl, strict, eye = row >= col, row > col, row == col
    def lanes(v, first):
        return jnp.concatenate([v[:, first + h:first + h + 1][None] for h in range(HEADS)], axis=0)

    a_col, b_col = lanes(ab, 0), lanes(ab, HEADS)
    a_log, dt_bias = lanes(alog_row, 0), lanes(dtb_row, 0)
    qn = cq * lax.rsqrt(jnp.sum(cq * cq, axis=-1, keepdims=True) + EPS) * (HD ** -0.5)
    kn = ck * lax.rsqrt(jnp.sum(ck * ck, axis=-1, keepdims=True) + EPS)
    la_col = -jnp.exp(a_log) * jax.nn.softplus(a_col + dt_bias)
    beta = jax.nn.sigmoid(b_col)
    la_row = jnp.sum(jnp.where(eye, la_col, 0.0), axis=1, keepdims=True)
    g_col = jnp.sum(jnp.where(incl, la_row, 0.0), axis=2, keepdims=True)
    g_row = jnp.sum(jnp.where(row <= col, la_col, 0.0), axis=1, keepdims=True)
    g_last = jnp.sum(la_col, axis=1, keepdims=True)
    gamma = jnp.where(incl, jnp.exp(jnp.where(incl, g_col - g_row, 0.0)), 0.0)
    lower = jnp.where(strict, beta * _hb(kn, kn, 2, 2) * gamma, 0.0)
    inv = jnp.where(eye, 1.0, 0.0) - lower
    pw = _hb(lower, lower)
    for step in range(5):
        inv = inv + _hb(inv, pw)
        if step < 4:
            pw = _hb(pw, pw)
    u = _hb(inv, cv * beta)
    w = _hb(inv, kn * (beta * jnp.exp(g_col)))
    qk = _hb(qn, kn, 2, 2) * gamma
    return u, w, qk, qn * jnp.exp(g_col), kn * jnp.exp(g_last - g_col), jnp.exp(g_last)


def _gdn_post(o, z, gain):
    y = o * lax.rsqrt(jnp.mean(o * o, axis=-1, keepdims=True) + EPS) * gain
    return y * (z * jax.nn.sigmoid(z))


def _gdn_specs(nc, reverse):
    c = GDN_CHUNK

    def ch(n):
        return nc - 1 - n if reverse else n

    def wide(array_off):
        return pl.BlockSpec((c, GDN_W), lambda n: (ch(n), array_off // GDN_W))

    ab = pl.BlockSpec((c, HD), lambda n: (ch(n), OFF_AB // HD))
    row = pl.BlockSpec((1, HD), lambda n: (0, 0))
    state = pl.BlockSpec((None, HEADS, HD, HD), lambda n: (ch(n), 0, 0, 0))
    return wide, ab, row, state


def _gdn_fwd(cqkv, proj, a_log, dt_bias, gain):
    t = proj.shape[0]
    nc = t // GDN_CHUNK
    wide, ab, row, state = _gdn_specs(nc, False)

    def body(cq_ref, ck_ref, cv_ref, ab_ref, z_ref, al_ref, dt_ref, g_ref, y_ref, sprev_ref, s_scr):
        @pl.when(pl.program_id(0) == 0)
        def _():
            s_scr[...] = jnp.zeros_like(s_scr)

        u, w, qk, qd, kd, dec = _gdn_prep(_heads(cq_ref[...]), _heads(ck_ref[...]), _heads(cv_ref[...]), ab_ref[...],
                                          al_ref[...], dt_ref[...])
        s = s_scr[...]
        sprev_ref[...] = s
        v_new = u - _hb(w, s)
        o = _hb(qd, s) + _hb(qk, v_new)
        s_scr[...] = s * dec + _hb(kd, v_new, 1, 1)
        y = _gdn_post(o, _heads(z_ref[...]), g_ref[...])
        for h in range(HEADS):
            y_ref[:, h * HD:(h + 1) * HD] = y[h].astype(y_ref.dtype)

    return pl.pallas_call(
        body, name="gdn_fwd", grid=(nc,),
        in_specs=[wide(0), wide(GDN_W), wide(2 * GDN_W), ab, wide(OFF_Z), row, row, row],
        out_specs=[wide(0), state], out_shape=[_sds((t, GDN_W), BF16), _sds((nc, HEADS, HD, HD), F32)],
        scratch_shapes=[pltpu.VMEM((HEADS, HD, HD), F32)], compiler_params=_cp("arbitrary"),
    )(cqkv, cqkv, cqkv, proj, proj, a_log, dt_bias, gain)


def _gdn_bwd(cqkv, proj, a_log, dt_bias, gain, sprev, dy):
    t = proj.shape[0]
    nc = t // GDN_CHUNK
    wide, ab, row, state = _gdn_specs(nc, True)

    def body(cq_ref, ck_ref, cv_ref, ab_ref, z_ref, al_ref, dt_ref, g_ref, sp_ref, dy_ref,
             dc_ref, dab_ref, dz_ref, dal_ref, ddt_ref, dg_ref, ds_scr):
        @pl.when(pl.program_id(0) == 0)
        def _():
            ds_scr[...] = jnp.zeros_like(ds_scr)
            dal_ref[...] = jnp.zeros_like(dal_ref)
            ddt_ref[...] = jnp.zeros_like(ddt_ref)
            dg_ref[...] = jnp.zeros_like(dg_ref)

        (u, w, qk, qd, kd, dec), prep_vjp = jax.vjp(
            _gdn_prep, _heads(cq_ref[...]), _heads(ck_ref[...]), _heads(cv_ref[...]), ab_ref[...], al_ref[...], dt_ref[...])
        s = sp_ref[...]
        v_new = u - _hb(w, s)
        o = _hb(qd, s) + _hb(qk, v_new)
        _, post_vjp = jax.vjp(_gdn_post, o, _heads(z_ref[...]), g_ref[...])
        do, dz, dgain = post_vjp(_heads(dy_ref[...]).astype(F32))
        ds_next = ds_scr[...]
        d_vnew = _hb(qk, do, 1, 1) + _hb(kd, ds_next)
        d_qk = _hb(do, v_new, 2, 2)
        d_qd = _hb(do, s, 2, 2)
        d_kd = _hb(v_new, ds_next, 2, 2)
        d_dec = jnp.sum(jnp.sum(s * ds_next, axis=2, keepdims=True), axis=1, keepdims=True)
        ds_scr[...] = dec * ds_next + _hb(qd, do, 1, 1) - _hb(w, d_vnew, 1, 1)
        d_w = -_hb(d_vnew, s, 2, 2)
        dcq, dck, dcv, dab, dal, ddt = prep_vjp((d_vnew, d_w, d_qk, d_qd, d_kd, d_dec))
        for h in range(HEADS):
            dc_ref[:, h * HD:(h + 1) * HD] = dcq[h]
            dc_ref[:, GDN_W + h * HD:GDN_W + (h + 1) * HD] = dck[h]
            dc_ref[:, 2 * GDN_W + h * HD:2 * GDN_W + (h + 1) * HD] = dcv[h]
            dz_ref[:, h * HD:(h + 1) * HD] = dz[h].astype(dz_ref.dtype)
        dab_ref[...] = dab
        dal_ref[...] += dal
        ddt_ref[...] += ddt
        dg_ref[...] += dgain

    c = GDN_CHUNK
    return pl.pallas_call(
        body, name="gdn_bwd", grid=(nc,),
        in_specs=[wide(0), wide(GDN_W), wide(2 * GDN_W), ab, wide(OFF_Z), row, row, row, state, wide(0)],
        out_specs=[pl.BlockSpec((c, 3 * GDN_W), lambda n: (nc - 1 - n, 0)), pl.BlockSpec((c, HD), lambda n: (nc - 1 - n, 0)),
                   wide(0), row, row, row],
        out_shape=[_sds((t, 3 * GDN_W), F32), _sds((t, HD), F32), _sds((t, GDN_W), BF16),
                   _sds((1, HD), F32), _sds((1, HD), F32), _sds((1, HD), F32)],
        scratch_shapes=[pltpu.VMEM((HEADS, HD, HD), F32)], compiler_params=_cp("arbitrary"),
    )(cqkv, cqkv, cqkv, proj, proj, a_log, dt_bias, gain, sprev, dy)


MERGE_TN = 512


def _merge_specs(t, tm):
    tn = MERGE_TN
    ys = [pl.BlockSpec((tm, wd), lambda i, j: (i, 0)) for wd in (POOL_W, SB_W, GDN_W)]
    ws = [pl.BlockSpec((None, wd, tn), lambda i, j: (j, 0, 0)) for wd in (POOL_W, SB_W, GDN_W)]
    gs = [pl.BlockSpec((tm, tn), functools.partial(lambda i, j, b: (i, OFF_GATE // tn + b * (D // tn) + j), b=b))
          for b in range(3)]
    out = pl.BlockSpec((tm, tn), lambda i, j: (i, j))
    return ys, ws, gs, out


def _merge_fwd(ys, wups, proj):
    t = proj.shape[0]
    tm = min(512, t)
    y_specs, w_specs, g_specs, out = _merge_specs(t, tm)

    def body(y0, y1, y2, w0, w1, w2, g0, g1, g2, o_ref):
        acc = jnp.zeros(o_ref.shape, F32)
        for y, w, g in ((y0, w0, g0), (y1, w1, g1), (y2, w2, g2)):
            acc = acc + jax.nn.sigmoid(g[...]) * _dot(y[...], w[...])
        o_ref[...] = acc.astype(o_ref.dtype)

    return pl.pallas_call(
        body, name="merge_fwd", grid=(t // tm, D // MERGE_TN), in_specs=[*y_specs, *w_specs, *g_specs],
        out_specs=out, out_shape=_sds((t, D), BF16), compiler_params=_cp("parallel", "parallel"),
    )(*ys, *wups, proj, proj, proj)


def _merge_bwd(ys, wups, proj, dmerged):
    t = proj.shape[0]
    tm = min(512, t)
    y_specs, w_specs, g_specs, out = _merge_specs(t, tm)

    def body(y0, y1, y2, w0, w1, w2, g0, g1, g2, dm_ref, dg0, dg1, dg2, dm0, dm1, dm2):
        dm = dm_ref[...].astype(F32)
        for y, w, g, dg, dmb in ((y0, w0, g0, dg0, dm0), (y1, w1, g1, dg1, dm1), (y2, w2, g2, dg2, dm2)):
            sg = jax.nn.sigmoid(g[...])
            dg[...] = (dm * _dot(y[...], w[...]) * sg * (1.0 - sg)).astype(dg.dtype)
            dmb[...] = (dm * sg).astype(dmb.dtype)

    return pl.pallas_call(
        body, name="merge_bwd", grid=(t // tm, D // MERGE_TN), in_specs=[*y_specs, *w_specs, *g_specs, out],
        out_specs=[out] * 6, out_shape=[_sds((t, D), BF16)] * 6, compiler_params=_cp("parallel", "parallel"),
    )(*ys, *wups, proj, proj, proj, dmerged)


def _tile(t, want):
    return min(t, want)


def _layer_fwd(x, l, gw, w_al, sp, reached=None):
    t = x.shape[0]
    tm = _tile(t, 1024)
    u = _rms_fwd("rms_attn", x, sp["attn_norm"][l])
    proj = _mm("proj", u, w_al, m=t, n=N_AL, k=D, tm=tm, tn=1024, tk=512, a_spec=_a_plain(tm, 512),
               b_spec=_b_plain(512, 1024), dims=None, out_shapes=[_sds((t, N_AL), F32)], out_specs=[_o_plain(tm, 1024)])[0]
    if reached is not None:
        reached("proj", proj)
    y_pool = _pool_fwd(proj, sp["pool_w"][l], sp["pool_scale"][l])
    kv = _sb_cast_kv(proj)
    y_sb = _sb_fwd(proj, kv)
    cqkv = _conv_fwd(proj, sp["conv"][l])
    y_gdn, sprev = _gdn_fwd(cqkv, proj, sp["a_log"][l], sp["dt_bias"][l], sp["gdn_norm"][l])
    ys = (y_pool, y_sb, y_gdn)
    wups = (gw["w_pool_up"], gw["w_sb_up"], gw["w_gdn_up"])
    merged = _merge_fwd(ys, wups, proj)
    if reached is not None:
        reached("merged", merged)
    x1 = _mm("out_proj", merged, gw["w_out"], m=t, n=D, k=D, tm=tm, tn=1024, tk=512, a_spec=_a_plain(tm, 512),
             b_spec=_w_rows(512, 1024, 512), dims=None, out_shapes=[_sds((t, D), F32)], out_specs=[_o_plain(tm, 1024)],
             extras=[x], extra_specs=[_o_plain(tm, 1024)], epilogue=lambda r, xr: (r + xr,))[0]
    u2 = _rms_fwd("rms_mlp", x1, sp["mlp_norm"][l])

    def relu2(r):
        hv = jnp.maximum(r, 0.0)
        return hv, hv * hv

    hid, hid2 = _mm("ff1", u2, gw["w_ff1"], m=t, n=D_FF, k=D, tm=tm, tn=1024, tk=512, a_spec=_a_plain(tm, 512),
                    b_spec=_w_cols(512, 1024, 2048), dims=None, out_shapes=[_sds((t, D_FF), BF16)] * 2,
                    out_specs=[_o_plain(tm, 1024)] * 2, epilogue=relu2)
    x2 = _mm("ff2", hid2, gw["w_ff2"], m=t, n=D, k=D_FF, tm=tm, tn=1024, tk=512, a_spec=_a_plain(tm, 512),
             b_spec=_w_rows(512, 1024, 2048), dims=None, out_shapes=[_sds((t, D), F32)], out_specs=[_o_plain(tm, 1024)],
             extras=[x1], extra_specs=[_o_plain(tm, 1024)], epilogue=lambda r, xr: (r + xr,))[0]
    saved = dict(x=x, u=u, proj=proj, kv=kv, cqkv=cqkv, sprev=sprev, ys=ys, merged=merged, x1=x1, u2=u2, hid=hid, hid2=hid2)
    return x2, saved


def _layer_bwd(dx2, l, gw, w_al, sp, sv, emit=None):
    t = dx2.shape[0]
    tm = _tile(t, 1024)
    tk = _tile(t, 512)
    g = {}
    if emit is None:
        emit = lambda names, grads, v: v
    dpre = _mm("ff2_dx", dx2, gw["w_ff2"], m=t, n=D_FF, k=D, tm=tm, tn=1024, tk=512, a_spec=_a_plain(tm, 512),
               b_spec=_w_rows_t(512, 1024, 2048), dims=NT, out_shapes=[_sds((t, D_FF), BF16)],
               out_specs=[_o_plain(tm, 1024)], extras=[sv["hid"]], extra_specs=[_o_plain(tm, 1024)],
               epilogue=lambda r, hv: (r * (2.0 * hv.astype(F32)),))[0]
    g["w_ff2"] = _mm("ff2_dw", sv["hid2"], dx2, m=D_FF, n=D, k=t, tm=1024, tn=1024, tk=tk, a_spec=_a_trans(1024, tk),
                     b_spec=_b_plain(tk, 1024), dims=TN, out_shapes=[_sds((D_FF, D), BF16)],
                     out_specs=[_o_plain(1024, 1024)])[0].reshape(N_CHIPS, D_FF // N_CHIPS, D)
    du2 = _mm("ff1_dx", dpre, gw["w_ff1"], m=t, n=D, k=D_FF, tm=tm, tn=1024, tk=512, a_spec=_a_plain(tm, 512),
              b_spec=_w_cols_t(512, 1024, 2048), dims=NT, out_shapes=[_sds((t, D), F32)], out_specs=[_o_plain(tm, 1024)])[0]
    g["w_ff1"] = _mm("ff1_dw", sv["u2"], dpre, m=D, n=D_FF, k=t, tm=1024, tn=1024, tk=tk, a_spec=_a_trans(1024, tk),
                     b_spec=_b_plain(tk, 1024), dims=TN, out_shapes=[_sds((N_CHIPS, D, D_FF // N_CHIPS), BF16)],
                     out_specs=[_o_colshard(1024, 1024, D_FF // N_CHIPS)])[0]
    dx1, g["mlp_norm"] = _rms_bwd("rms_mlp_bwd", du2, sv["x1"], sp["mlp_norm"][l], dx2)
    dx1 = emit(("w_ff1", "w_ff2"), g, dx1)
    dmerged = _mm("out_dx", dx1, gw["w_out"], m=t, n=D, k=D, tm=tm, tn=512, tk=1024, a_spec=_a_plain(tm, 1024),
                  b_spec=_w_rows_t(1024, 512, 512), dims=NT, out_shapes=[_sds((t, D), BF16)], out_specs=[_o_plain(tm, 512)])[0]
    g["w_out"] = _mm("out_dw", sv["merged"], dx1, m=D, n=D, k=t, tm=1024, tn=1024, tk=tk, a_spec=_a_trans(1024, tk),
                     b_spec=_b_plain(tk, 1024), dims=TN, out_shapes=[_sds((D, D), BF16)],
                     out_specs=[_o_plain(1024, 1024)])[0].reshape(N_CHIPS, D // N_CHIPS, D)
    wups = (gw["w_pool_up"], gw["w_sb_up"], gw["w_gdn_up"])
    dg0, dg1, dg2, dm0, dm1, dm2 = _merge_bwd(sv["ys"], wups, sv["proj"], dmerged)
    dys = []
    for nm, yb, dmb, wd in zip(("w_pool_up", "w_sb_up", "w_gdn_up"), sv["ys"], (dm0, dm1, dm2), (POOL_W, SB_W, GDN_W)):
        dys.append(_mm(nm + "_dx", dmb, gw[nm], m=t, n=wd, k=D, tm=tm, tn=256, tk=512, a_spec=_a_plain(tm, 512),
                       b_spec=_w_cols_t(512, 256, 512), dims=NT, out_shapes=[_sds((t, wd), F32)],
                       out_specs=[_o_plain(tm, 256)])[0])
        g[nm] = _mm(nm + "_dw", yb, dmb, m=wd, n=D, k=t, tm=256, tn=512, tk=tk, a_spec=_a_trans(256, tk),
                    b_spec=_b_plain(tk, 512), dims=TN, out_shapes=[_sds((N_CHIPS, wd, D // N_CHIPS), BF16)],
                    out_specs=[_o_colshard(256, 512, D // N_CHIPS)])[0]
    dys[2] = emit(("w_pool_up", "w_sb_up", "w_gdn_up", "w_out"), g, dys[2])
    proj = sv["proj"]
    dp, g["pool_w"], g["pool_scale"] = _pool_bwd(proj, sp["pool_w"][l], sp["pool_scale"][l], dys[0])
    dsq, dsk, dsv = _sb_bwd(proj, sv["kv"], dys[1])
    dc, dab, dz, g["a_log"], g["dt_bias"], g["gdn_norm"] = _gdn_bwd(
        sv["cqkv"], proj, sp["a_log"][l], sp["dt_bias"][l], sp["gdn_norm"][l], sv["sprev"], dys[2])
    dgx, g["conv"] = _conv_bwd(proj, sp["conv"][l], dc)
    dproj = jnp.concatenate(
        [dsq, dsk.astype(BF16), dsv.astype(BF16), dgx, dz, dab.astype(BF16), jnp.zeros((t, AB_W - HD), BF16),
         dp, dg0, dg1, dg2], axis=1)
    du = _mm("proj_dx", dproj, w_al, m=t, n=D, k=N_AL, tm=tm, tn=1024, tk=512, a_spec=_a_plain(tm, 512),
             b_spec=_b_trans(512, 1024), dims=NT, out_shapes=[_sds((t, D), F32)], out_specs=[_o_plain(tm, 1024)])[0]
    g["w_al"] = _mm("proj_dw", sv["u"], dproj, m=D, n=N_AL, k=t, tm=1024, tn=1024, tk=tk, a_spec=_a_trans(1024, tk),
                    b_spec=_b_plain(tk, 1024), dims=TN, out_shapes=[_sds((D, N_AL), BF16)], out_specs=[_o_plain(1024, 1024)])[0]
    dx, g["attn_norm"] = _rms_bwd("rms_attn_bwd", du, sv["x"], sp["attn_norm"][l], dx1)
    g["w_in"] = _w_in_to_shards(g["w_al"])
    dx = emit(("w_in",), g, dx)
    return dx, g


def _align_w_in(w):
    n_ab = ORIG_GATE - ORIG_AB
    return jnp.concatenate([w[:, ORIG_SB:ORIG_GATE], jnp.zeros((D, AB_W - n_ab), w.dtype), w[:, :ORIG_SB], w[:, ORIG_GATE:]],
                           axis=1)


def _unalign_w_in(w):
    n_ab = ORIG_GATE - ORIG_AB
    return jnp.concatenate([w[:, OFF_P:OFF_GATE], w[:, :OFF_AB + n_ab], w[:, OFF_GATE:]], axis=1)


W_IN_RUNS = ((0, ORIG_SB, OFF_P), (ORIG_SB, ORIG_GATE, OFF_SB), (ORIG_GATE, N_IN, OFF_GATE))
W_IN_SHARD = N_IN // N_CHIPS


def _w_in_from_shards(gathered):
    parts = []
    for lo, hi, al in sorted(W_IN_RUNS, key=lambda r: r[2]):
        if al == OFF_P:
            parts.append(jnp.zeros((D, OFF_P - (OFF_AB + ORIG_GATE - ORIG_AB)), gathered.dtype))
        while lo < hi:
            chip = lo // W_IN_SHARD
            end = min(hi, (chip + 1) * W_IN_SHARD)
            parts.append(gathered[chip, :, lo - chip * W_IN_SHARD:end - chip * W_IN_SHARD])
            lo = end
    return jnp.concatenate(parts, axis=1)


def _w_in_to_shards(g_al):
    shards = []
    for chip in range(N_CHIPS):
        a, b = chip * W_IN_SHARD, (chip + 1) * W_IN_SHARD
        parts = [g_al[:, al + max(a, lo) - lo:al + min(b, hi) - lo] for lo, hi, al in W_IN_RUNS if max(a, lo) < min(b, hi)]
        shards.append(jnp.concatenate(parts, axis=1))
    return jnp.stack(shards)


def _row128(v):
    return jnp.pad(v.reshape(1, -1), ((0, 0), (0, HD - v.shape[-1])))


def _local_step(x, target, weights_of, sp, emit=None, reached=None):
    saved, gw, w_in_al = [], [], []
    h = x
    for l in range(2):
        gw_l, w_al_l = weights_of(l)
        gw.append(gw_l)
        w_in_al.append(w_al_l)
        h, sv = _layer_fwd(h, l, gw_l, w_al_l, sp, None if reached is None else functools.partial(reached, l))
        saved.append(sv)
    loss, dh, g_final = _loss_head(h, sp["final_norm"], target)
    grads = [None, None]
    for l in (1, 0):
        dh, grads[l] = _layer_bwd(dh, l, gw[l], w_in_al[l], sp, saved[l],
                                  None if emit is None else functools.partial(emit, l))
    return loss, dh, grads, g_final


ANY = pl.BlockSpec(memory_space=pl.ANY)


def _me():
    return lax.axis_index("x"), lax.axis_index("y"), lax.axis_index("c")


def _other_chips(x, y):
    return [(1 - x, y), (x, 1 - y), (1 - x, 1 - y)]


def _half(ref, axis, c, rows):
    half = rows // 2
    idx = [slice(None)] * axis + [pl.ds(pl.multiple_of(c * half, 16), half)]
    return ref.at[tuple(idx)]


def _gather_steps(out, send, recv):
    n = len(out)
    x, y, c = _me()
    mine = 2 * x + y
    sibling = (x, y, 1 - c)
    chips = _other_chips(x, y)
    sends = []
    for t in range(n):
        rows = out[t].shape[1]
        for k, (px, py) in enumerate(chips):
            own_half = _half(out[t].at[mine], 0, c, rows)
            cp = pltpu.make_async_remote_copy(
                src_ref=own_half, dst_ref=own_half,
                send_sem=send.at[6 * t + k], recv_sem=recv.at[6 * t + k], device_id=(px, py, c), device_id_type=MESH)
            cp.start()
            sends.append(cp)
    for t in range(n):
        rows = out[t].shape[1]
        for k, (px, py) in enumerate(chips):
            landed = _half(out[t].at[2 * px + py], 0, c, rows)
            pltpu.make_async_remote_copy(
                src_ref=landed, dst_ref=landed, send_sem=send.at[6 * t + k], recv_sem=recv.at[6 * t + k],
                device_id=(px, py, c), device_id_type=MESH).wait_recv()
            cp = pltpu.make_async_remote_copy(
                src_ref=landed, dst_ref=landed, send_sem=send.at[6 * t + 3 + k], recv_sem=recv.at[6 * t + 3 + k],
                device_id=sibling, device_id_type=MESH)
            cp.start()
            sends.append(cp)
    for t in range(n):
        rows = out[t].shape[1]
        for k, (px, py) in enumerate(chips):
            other = _half(out[t].at[2 * px + py], 0, 1 - c, rows)
            pltpu.make_async_remote_copy(
                src_ref=other, dst_ref=other, send_sem=send.at[6 * t + 3 + k], recv_sem=recv.at[6 * t + 3 + k],
                device_id=sibling, device_id_type=MESH).wait_recv()
    for cp in sends:
        cp.wait_send()


def _gather_weights(bufs):
    n = len(bufs)

    def body(*refs):
        _gather_steps(refs[n:2 * n], *refs[2 * n:])

    return pl.pallas_call(
        body, name="gather_weights", in_specs=[ANY] * n, out_specs=[ANY] * n,
        out_shape=[_sds(s.shape, s.dtype) for s in bufs], input_output_aliases={t: t for t in range(n)},
        scratch_shapes=[pltpu.SemaphoreType.DMA((6 * n,)), pltpu.SemaphoreType.DMA((6 * n,))],
    )(*bufs)


def _gather_weights_async(bufs, tag, collective_id):
    n = len(bufs)
    refs = [jax.new_ref(b, memory_space=pltpu.MemorySpace.HBM) for b in bufs]

    @pl.kernel(mesh=plsc.ScalarSubcoreMesh(axis_name="sequencer", num_cores=1), name=f"gather_async_{tag}",
               scratch_types=(pltpu.SemaphoreType.DMA((6 * n,)), pltpu.SemaphoreType.DMA((6 * n,))),
               compiler_params=pltpu.CompilerParams(collective_id=collective_id))
    def launch(send, recv):
        x, y, c = _me()
        barrier = pltpu.get_barrier_semaphore()
        peers = [(x, y, 1 - c)] + [(px, py, c) for px, py in _other_chips(x, y)]
        for peer in peers:
            pl.semaphore_signal(barrier, inc=1, device_id=peer, device_id_type=MESH)
        pl.semaphore_wait(barrier, len(peers))
        _gather_steps(refs, send, recv)

    launch()
    return [r[...] for r in refs]


def _rs_pair(grads):
    n = len(grads)

    def body(*refs):
        g, out = refs[:n], refs[n:2 * n]
        send, recv = refs[2 * n:]
        x, y, c = _me()
        copies = []
        for t in range(n):
            cp = pltpu.make_async_remote_copy(
                src_ref=_half(g[t], 1, 1 - c, g[t].shape[1]), dst_ref=out[t], send_sem=send.at[t], recv_sem=recv.at[t],
                device_id=(x, y, 1 - c), device_id_type=MESH)
            cp.start()
            copies.append(cp)
        for cp in copies:
            cp.wait()

    return pl.pallas_call(
        body, name="rs_pair", in_specs=[ANY] * n, out_specs=[ANY] * n,
        out_shape=[_sds((N_CHIPS, s.shape[1] // 2, s.shape[2]), s.dtype) for s in grads],
        scratch_shapes=[pltpu.SemaphoreType.DMA((n,)), pltpu.SemaphoreType.DMA((n,))],
    )(*grads)


def _rs_chips_steps(p, out, send, recv):
    x, y, c = _me()
    copies = []
    for t in range(len(p)):
        for k, (px, py) in enumerate(_other_chips(x, y)):
            cp = pltpu.make_async_remote_copy(
                src_ref=p[t].at[2 * px + py], dst_ref=out[t].at[k], send_sem=send.at[3 * t + k],
                recv_sem=recv.at[3 * t + k], device_id=(px, py, c), device_id_type=MESH)
            cp.start()
            copies.append(cp)
    for cp in copies:
        cp.wait()


def _rs_chips_async(parts, tag, collective_id):
    n = len(parts)
    src = [jax.new_ref(p, memory_space=pltpu.MemorySpace.HBM) for p in parts]
    got = [jax.empty_ref(_sds((3, *p.shape[1:]), p.dtype), memory_space=pltpu.MemorySpace.HBM) for p in parts]

    @pl.kernel(mesh=plsc.ScalarSubcoreMesh(axis_name="sequencer", num_cores=1), name=f"rs_chips_async_{tag}",
               scratch_types=(pltpu.SemaphoreType.DMA((3 * n,)), pltpu.SemaphoreType.DMA((3 * n,))),
               compiler_params=pltpu.CompilerParams(collective_id=collective_id))
    def launch(send, recv):
        x, y, c = _me()
        barrier = pltpu.get_barrier_semaphore()
        peers = [(px, py, c) for px, py in _other_chips(x, y)]
        for peer in peers:
            pl.semaphore_signal(barrier, inc=1, device_id=peer, device_id_type=MESH)
        pl.semaphore_wait(barrier, len(peers))
        _rs_chips_steps(src, got, send, recv)

    launch()
    return [g[...] for g in got]


def _rs_chips(parts):
    n = len(parts)

    def body(*refs):
        _rs_chips_steps(refs[:n], refs[n:2 * n], *refs[2 * n:])

    return pl.pallas_call(
        body, name="rs_chips", in_specs=[ANY] * n, out_specs=[ANY] * n,
        out_shape=[_sds((3, *s.shape[1:]), s.dtype) for s in parts],
        scratch_shapes=[pltpu.SemaphoreType.DMA((3 * n,)), pltpu.SemaphoreType.DMA((3 * n,))],
    )(*parts)


def _pair_exchange(bufs):
    n = len(bufs)

    def body(*refs):
        out = refs[n:2 * n]
        send, recv = refs[2 * n:]
        x, y, c = _me()
        copies = []
        for t in range(n):
            cp = pltpu.make_async_remote_copy(
                src_ref=out[t].at[c], dst_ref=out[t].at[c], send_sem=send.at[t], recv_sem=recv.at[t],
                device_id=(x, y, 1 - c), device_id_type=MESH)
            cp.start()
            copies.append(cp)
        for t, cp in enumerate(copies):
            cp.wait_send()
            pltpu.make_async_remote_copy(
                src_ref=out[t].at[1 - c], dst_ref=out[t].at[1 - c], send_sem=send.at[t], recv_sem=recv.at[t],
                device_id=(x, y, 1 - c), device_id_type=MESH).wait_recv()

    return pl.pallas_call(
        body, name="pair_exchange", in_specs=[ANY] * n, out_specs=[ANY] * n,
        out_shape=[_sds(s.shape, s.dtype) for s in bufs], input_output_aliases={t: t for t in range(n)},
        scratch_shapes=[pltpu.SemaphoreType.DMA((n,)), pltpu.SemaphoreType.DMA((n,))],
    )(*bufs)


def _row_tile(rows, cols, itemsize, budget=2 * 1024 * 1024):
    tr = rows
    while tr * cols * itemsize > budget and tr % 32 == 0:
        tr //= 2
    return tr


def _sum_pair(name, g, got, where):
    nchip, rows, cols = g.shape
    half = rows // 2
    tr = _row_tile(half, cols, 4)
    per = half // tr

    def body(w_ref, g_ref, r_ref, o_ref):
        o_ref[...] = (g_ref[...].astype(F32) + r_ref[...].astype(F32)).astype(o_ref.dtype)

    blk = pl.BlockSpec((None, tr, cols), lambda j, i, w_ref: (j, i, 0))
    return pl.pallas_call(
        body, name=name,
        grid_spec=pltpu.PrefetchScalarGridSpec(
            num_scalar_prefetch=1, grid=(nchip, per),
            in_specs=[pl.BlockSpec((None, tr, cols), lambda j, i, w_ref: (j, w_ref[1] * per + i, 0)), blk], out_specs=blk),
        out_shape=_sds((nchip, half, cols), BF16), compiler_params=_cp("parallel", "parallel"),
    )(where, g, got)


def _sum_chips(name, p, got, where):
    _, rows, cols = p.shape
    tr = _row_tile(rows, cols, 4)

    def body(w_ref, p_ref, r0, r1, r2, o_ref):
        o_ref[...] = ((p_ref[...].astype(F32) + r0[...].astype(F32)) + r1[...].astype(F32)) + r2[...].astype(F32)

    def got_k(k):
        return pl.BlockSpec((None, tr, cols), lambda i, w_ref: (k, i, 0))

    return pl.pallas_call(
        body, name=name,
        grid_spec=pltpu.PrefetchScalarGridSpec(
            num_scalar_prefetch=1, grid=(rows // tr,),
            in_specs=[pl.BlockSpec((None, tr, cols), lambda i, w_ref: (w_ref[0], i, 0)), got_k(0), got_k(1), got_k(2)],
            out_specs=pl.BlockSpec((None, tr, cols), lambda i, w_ref: (w_ref[1], i, 0))),
        out_shape=_sds((2, rows, cols), F32), compiler_params=_cp("parallel"),
    )(where, p, got, got, got)


def _reduce_scatter(grads, where):
    return _rs_finish(*_rs_begin(grads, where), where)


def _rs_begin(grads, where, tag=None, collective_id=None):
    got = _rs_pair(grads)
    parts = [_sum_pair(f"sum_pair_{t}", g, r, where) for t, (g, r) in enumerate(zip(grads, got))]
    return parts, (_rs_chips(parts) if tag is None else _rs_chips_async(parts, tag, collective_id))


def _rs_finish(parts, got, where):
    halves = [_sum_chips(f"sum_chips_{t}", p, r, where) for t, (p, r) in enumerate(zip(parts, got))]
    return _pair_exchange(halves)


def _all_reduce_small(name, v):
    rows = v.shape[0]

    def body(v_ref, o_ref, land, send, recv):
        x, y, c = _me()
        mine = 4 * x + 2 * y + c
        copies = []
        for k in range(1, 8):
            kx, ky, kc = k >> 2, (k >> 1) & 1, k & 1
            peer = (x ^ kx, y ^ ky, c ^ kc)
            cp = pltpu.make_async_remote_copy(
                src_ref=v_ref, dst_ref=land.at[mine], send_sem=send.at[k - 1], recv_sem=recv.at[k - 1],
                device_id=peer, device_id_type=MESH)
            cp.start()
            copies.append(cp)
        land[mine] = v_ref[...]
        for k in range(1, 8):
            kx, ky, kc = k >> 2, (k >> 1) & 1, k & 1
            src = 4 * (x ^ kx) + 2 * (y ^ ky) + (c ^ kc)
            pltpu.make_async_remote_copy(
                src_ref=v_ref, dst_ref=land.at[src], send_sem=send.at[k - 1], recv_sem=recv.at[k - 1],
                device_id=(x ^ kx, y ^ ky, c ^ kc), device_id_type=MESH).wait_recv()
        acc = land[0]
        for d in range(1, 8):
            acc = acc + land[d]
        o_ref[...] = acc
        for cp in copies:
            cp.wait_send()

    vm = pl.BlockSpec(memory_space=pltpu.VMEM)
    return pl.pallas_call(
        body, name=name, in_specs=[vm], out_specs=vm, out_shape=_sds((rows, 128), F32),
        scratch_shapes=[pltpu.VMEM((8, rows, 128), F32), pltpu.SemaphoreType.DMA((7,)), pltpu.SemaphoreType.DMA((7,))],
    )(v)


def _adamw(name, w, g, m, v):
    rows, cols = w.shape
    tr = _row_tile(rows, cols, 4, budget=1024 * 1024)
    c1 = 1.0 / (1.0 - ADAM_B1 ** ADAM_STEP)
    c2 = 1.0 / (1.0 - ADAM_B2 ** ADAM_STEP)

    def body(w_ref, g_ref, m_ref, v_ref, d_ref, nm_ref, nv_ref):
        gv = g_ref[...]
        nm = ADAM_B1 * m_ref[...] + (1.0 - ADAM_B1) * gv
        nv = ADAM_B2 * v_ref[...] + (1.0 - ADAM_B2) * (gv * gv)
        d_ref[...] = -ADAM_LR * ((nm * c1) / (jnp.sqrt(nv * c2) + ADAM_EPS) + ADAM_WD * w_ref[...])
        nm_ref[...] = nm
        nv_ref[...] = nv

    blk = pl.BlockSpec((tr, cols), lambda i: (i, 0))
    return pl.pallas_call(
        body, name=name, grid=(rows // tr,), in_specs=[blk] * 4, out_specs=[blk] * 3,
        out_shape=[_sds((rows, cols), F32)] * 3, compiler_params=_cp("parallel"),
    )(w, g, m, v)


def _adamw_layers(name, w, g0, g1, m, v):
    _, half, cols = g0.shape
    tr = _row_tile(half, cols, 4, budget=1024 * 1024)
    per_half = half // tr
    per = 2 * per_half
    c1 = 1.0 / (1.0 - ADAM_B1 ** ADAM_STEP)
    c2 = 1.0 / (1.0 - ADAM_B2 ** ADAM_STEP)

    def body(w_ref, g0_ref, g1_ref, m_ref, v_ref, g_ref, d_ref, nm_ref, nv_ref):
        gv = jnp.where(pl.program_id(0) == 0, g0_ref[...], g1_ref[...])
        nm = ADAM_B1 * m_ref[...] + (1.0 - ADAM_B1) * gv
        nv = ADAM_B2 * v_ref[...] + (1.0 - ADAM_B2) * (gv * gv)
        g_ref[...] = gv
        d_ref[...] = -ADAM_LR * ((nm * c1) / (jnp.sqrt(nv * c2) + ADAM_EPS) + ADAM_WD * w_ref[...])
        nm_ref[...] = nm
        nv_ref[...] = nv

    both = pl.BlockSpec((None, tr, cols), lambda l, i: (l, i, 0))

    def halves(i):
        return i // per_half, i % per_half, 0

    first = pl.BlockSpec((None, tr, cols), lambda l, i: halves(i * (1 - l) + (per - 1) * l))
    second = pl.BlockSpec((None, tr, cols), lambda l, i: halves(i * l))
    return pl.pallas_call(
        body, name=name, grid=(2, per), in_specs=[both, first, second, both, both], out_specs=[both] * 4,
        out_shape=[_sds(w.shape, F32)] * 4, compiler_params=_cp("arbitrary", "arbitrary"),
    )(w, g0, g1, m, v)


def _to_bf16_slot(name, w, l, where):
    _, rows, cols = w.shape
    tr = _row_tile(rows, cols, 4)

    def body(w_ref, x_ref, o_ref):
        o_ref[...] = x_ref[...].astype(BF16)

    return pl.pallas_call(
        body, name=name,
        grid_spec=pltpu.PrefetchScalarGridSpec(
            num_scalar_prefetch=1, grid=(rows // tr,), in_specs=[pl.BlockSpec((None, tr, cols), lambda i, w_ref: (l, i, 0))],
            out_specs=pl.BlockSpec((None, tr, cols), lambda i, w_ref: (w_ref[0], i, 0))),
        out_shape=_sds((N_CHIPS, rows, cols), BF16), compiler_params=_cp("parallel"))(where, w)


BIG = ("w_in", "w_pool_up", "w_sb_up", "w_gdn_up", "w_out", "w_ff1", "w_ff2")
SMALL = (("attn_norm", (D,)), ("pool_w", (4, 128, 128)), ("pool_scale", (POOL_W,)), ("gdn_a_log", (HEADS,)),
         ("gdn_dt_bias", (HEADS,)), ("gdn_norm", (HD,)), ("mlp_norm", (D,)))


PACK_TILE = 8 * 128


def _rows128(a):
    flat = a.reshape(-1)
    pad = (-flat.shape[0]) % PACK_TILE
    return jnp.pad(flat, (0, pad)).reshape(-1, 128)


def _pack(parts):
    packed = jnp.concatenate([_rows128(p) for p in parts], axis=0)
    return jnp.pad(packed, ((0, (-packed.shape[0]) % 8), (0, 0)))


def _unpack(packed, shapes):
    out, r = [], 0
    for shp in shapes:
        size = 1
        for s in shp:
            size *= s
        nr = -(-size // PACK_TILE) * 8
        out.append(packed[r:r + nr].reshape(-1)[:size].reshape(shp))
        r += nr
    return out


def kernel(x, attn_norm, w_in, pool_w, pool_scale, gdn_conv, gdn_a_log, gdn_dt_bias, gdn_norm, w_pool_up, w_sb_up, w_gdn_up, w_out, mlp_norm, w_ff1, w_ff2, final_norm, loss_target, m_attn_norm, m_w_in, m_pool_w, m_pool_scale, m_gdn_conv, m_gdn_a_log, m_gdn_dt_bias, m_gdn_norm, m_w_pool_up, m_w_sb_up, m_w_gdn_up, m_w_out, m_mlp_norm, m_w_ff1, m_w_ff2, m_final_norm, v_attn_norm, v_w_in, v_pool_w, v_pool_scale, v_gdn_conv, v_gdn_a_log, v_gdn_dt_bias, v_gdn_norm, v_w_pool_up, v_w_sb_up, v_w_gdn_up, v_w_out, v_mlp_norm, v_w_ff1, v_w_ff2, v_final_norm):
    weights = dict(attn_norm=attn_norm, w_in=w_in, pool_w=pool_w, pool_scale=pool_scale, gdn_conv=gdn_conv,
                   gdn_a_log=gdn_a_log, gdn_dt_bias=gdn_dt_bias, gdn_norm=gdn_norm, w_pool_up=w_pool_up, w_sb_up=w_sb_up,
                   w_gdn_up=w_gdn_up, w_out=w_out, mlp_norm=mlp_norm, w_ff1=w_ff1, w_ff2=w_ff2, final_norm=final_norm)
    mom1 = dict(attn_norm=m_attn_norm, w_in=m_w_in, pool_w=m_pool_w, pool_scale=m_pool_scale, gdn_conv=m_gdn_conv,
                gdn_a_log=m_gdn_a_log, gdn_dt_bias=m_gdn_dt_bias, gdn_norm=m_gdn_norm, w_pool_up=m_w_pool_up,
                w_sb_up=m_w_sb_up, w_gdn_up=m_w_gdn_up, w_out=m_w_out, mlp_norm=m_mlp_norm, w_ff1=m_w_ff1, w_ff2=m_w_ff2,
                final_norm=m_final_norm)
    mom2 = dict(attn_norm=v_attn_norm, w_in=v_w_in, pool_w=v_pool_w, pool_scale=v_pool_scale, gdn_conv=v_gdn_conv,
                gdn_a_log=v_gdn_a_log, gdn_dt_bias=v_gdn_dt_bias, gdn_norm=v_gdn_norm, w_pool_up=v_w_pool_up,
                w_sb_up=v_w_sb_up, w_gdn_up=v_w_gdn_up, w_out=v_w_out, mlp_norm=v_mlp_norm, w_ff1=v_w_ff1, w_ff2=v_w_ff2,
                final_norm=v_final_norm)
    xi, yi, ci = lax.axis_index("x"), lax.axis_index("y"), lax.axis_index("c")
    chip = 2 * xi + yi
    where = jnp.stack([chip, ci]).astype(jnp.int32)

    bufs = [[_to_bf16_slot(f"cast_{nm}_{l}", weights[nm], l, where) for nm in BIG] for l in range(2)]
    first = _gather_weights(bufs[0][:1])
    rest, _ = lax.optimization_barrier((bufs[0][1:], first))
    gw = [dict(zip(BIG, list(first) + _gather_weights_async(rest, "0_rest", 1))), {}]

    def reached(l, stage, value):
        if l == 0 and stage == "proj":
            later, _ = lax.optimization_barrier((bufs[1][:1], value))
            gw[1]["w_in"] = _gather_weights_async(later, "1_w_in", 2)[0]
        if l == 0 and stage == "merged":
            later, _ = lax.optimization_barrier((bufs[1][1:], value))
            gw[1].update(zip(BIG[1:], _gather_weights_async(later, "1_rest", 3)))

    def weights_of(l):
        return gw[l], _w_in_from_shards(gw[l]["w_in"])

    conv_cols = gdn_conv.shape[-1]
    conv_place = lax.dynamic_update_slice(jnp.zeros((2, GDN_CONV, N_CHIPS * conv_cols), F32),
                                          jnp.where(ci == 0, gdn_conv, 0.0), (0, 0, chip * conv_cols))
    conv_full = _all_reduce_small("gather_conv", _rows128(conv_place)).reshape(2, GDN_CONV, N_CHIPS * conv_cols)
    sp = dict(attn_norm=attn_norm.reshape(2, 1, D), pool_w=pool_w, pool_scale=pool_scale.reshape(2, 1, POOL_W),
              conv=conv_full, a_log=jnp.stack([_row128(gdn_a_log[l]) for l in range(2)]),
              dt_bias=jnp.stack([_row128(gdn_dt_bias[l]) for l in range(2)]), gdn_norm=gdn_norm.reshape(2, 1, HD),
              mlp_norm=mlp_norm.reshape(2, 1, D), final_norm=final_norm.reshape(1, D))

    started = []

    def emit(l, names, g, v):
        last = l == 0 and names == ("w_in",)
        tag = None if last else f"{l}_{names[0]}"
        parts, got = _rs_begin([g[nm] for nm in names], where, tag, 4 + len(started))
        started.append((l, names, parts, got, last))
        if not last:
            v, _ = lax.optimization_barrier((v, parts))
        return v

    loss, grad_x, grads, g_final = _local_step(x[0], loss_target[0], weights_of, sp, emit, reached)
    loss = lax.psum(loss[0, 0], ("x", "y", "c"))
    big_grads = {nm: [None, None] for nm in BIG}
    for l, names, parts, got, last in started:
        if not last:
            got, _ = lax.optimization_barrier((got, grad_x))
        for nm, red in zip(names, _rs_finish(parts, got, where)):
            big_grads[nm][l] = red
    small_parts, small_shapes = [], []
    for l in range(2):
        g = grads[l]
        for nm, shp in SMALL:
            key = {"gdn_a_log": "a_log", "gdn_dt_bias": "dt_bias"}.get(nm, nm)
            val = g[key]
            small_parts.append(val[0, :HEADS] if nm in ("gdn_a_log", "gdn_dt_bias") else val)
            small_shapes.append(shp)
        small_parts.append(g["conv"])
        small_shapes.append((GDN_CONV, N_CHIPS * conv_cols))
    small_parts.append(g_final)
    small_shapes.append((D,))
    reduced = _unpack(_all_reduce_small("reduce_small", _pack(small_parts)), small_shapes)
    per = len(SMALL) + 1
    grad = {}
    for i, (nm, _) in enumerate(SMALL):
        grad[nm] = jnp.stack([reduced[i], reduced[per + i]])
    conv_g = jnp.stack([reduced[per - 1], reduced[2 * per - 1]])
    grad["gdn_conv"] = lax.dynamic_slice(conv_g, (0, 0, chip * conv_cols), (2, GDN_CONV, conv_cols))
    grad["final_norm"] = reduced[-1]

    delta, new_m, new_v = {}, {}, {}
    for nm in BIG:
        grad[nm], delta[nm], new_m[nm], new_v[nm] = _adamw_layers("adamw_" + nm, weights[nm], *big_grads[nm], mom1[nm], mom2[nm])
    small_names = [nm for nm, _ in SMALL] + ["gdn_conv", "final_norm"]
    packs = [_pack([src[nm] for nm in small_names]) for src in (weights, grad, mom1, mom2)]
    outs = _adamw("adamw_small", *packs)
    shapes = [weights[nm].shape for nm in small_names]
    for dst, packed in zip((delta, new_m, new_v), outs):
        for nm, val in zip(small_names, _unpack(packed, shapes)):
            dst[nm] = val

    order = ("attn_norm", "w_in", "pool_w", "pool_scale", "gdn_conv", "gdn_a_log", "gdn_dt_bias", "gdn_norm", "w_pool_up",
             "w_sb_up", "w_gdn_up", "w_out", "mlp_norm", "w_ff1", "w_ff2", "final_norm")
    return (loss, grad_x[None], *[grad[n] for n in order], *[delta[n] for n in order], *[new_m[n] for n in order],
            *[new_v[n] for n in order])
```

```python
import functools

import jax
import jax.numpy as jnp
from jax import lax
from jax.experimental import pallas as pl
from jax.experimental.pallas import tpu as pltpu
from jax.experimental.pallas import tpu_sc as plsc

F32, BF16 = jnp.float32, jnp.bfloat16
HIGH = lax.Precision.HIGH
MESH = pl.DeviceIdType.MESH

D = 2048
EPS = 1e-6
POOL_WINDOWS = (2, 4, 8, 16)
POOL_W, SB_W, GDN_W = 512, 768, 768
HEADS, HD = 6, 128
SB_BLOCK = 128
GDN_CHUNK = 64
D_FF = 4 * D
N_IN = 12044
N_CHIPS = 4
OFF_SB, OFF_GQKV, OFF_Z, OFF_AB, OFF_P, OFF_GATE = 0, 2304, 4608, 5376, 5632, 6144
AB_W = 256
ORIG_SB, ORIG_AB, ORIG_GATE = 512, 5888, 5900
N_AL = 12288
VMEM_LIMIT = 48 * 1024 * 1024

ADAM_LR, ADAM_B1, ADAM_B2, ADAM_EPS, ADAM_WD, ADAM_STEP = 0.001, 0.9, 0.999, 1e-08, 0.01, 10

NT = (((1,), (1,)), ((), ()))
TN = (((0,), (0,)), ((), ()))


def _cp(*sem):
    return pltpu.CompilerParams(dimension_semantics=sem, vmem_limit_bytes=VMEM_LIMIT)


def _dot(a, b, dims=None, precision=None):
    if dims is None:
        dims = (((a.ndim - 1,), (0,)), ((), ()))
    return lax.dot_general(a, b, dims, precision=precision, preferred_element_type=F32)


def _hdot(a, b, dims=None):
    return _dot(a, b, dims, precision=HIGH)


def _bdot(a, b, dims=None):
    return _dot(a.astype(BF16), b.astype(BF16), dims)


def _mm(name, a, b, *, m, n, k, tm, tn, tk, a_spec, b_spec, dims, out_shapes, out_specs,
        extras=(), extra_specs=(), epilogue=None):
    nk = k // tk
    ne, no = len(extras), len(out_shapes)

    def body(*refs):
        a_ref, b_ref = refs[0], refs[1]
        ex = refs[2:2 + ne]
        outs = refs[2 + ne:2 + ne + no]
        kk = pl.program_id(2)

        def finish(r):
            res = epilogue(r, *[e[...] for e in ex]) if epilogue is not None else (r,)
            for o, v in zip(outs, res):
                o[...] = v.astype(o.dtype)

        part = _dot(a_ref[...].astype(BF16), b_ref[...].astype(BF16), dims)
        if nk == 1:
            finish(part)
            return
        acc = refs[-1]

        @pl.when(kk == 0)
        def _():
            acc[...] = part

        @pl.when((kk > 0) & (kk < nk - 1))
        def _():
            acc[...] += part

        @pl.when(kk == nk - 1)
        def _():
            finish(acc[...] + part)

    return pl.pallas_call(
        body, name=name, grid=(m // tm, n // tn, nk),
        in_specs=[a_spec, b_spec, *extra_specs], out_specs=out_specs, out_shape=out_shapes,
        scratch_shapes=[] if nk == 1 else [pltpu.VMEM((tm, tn), F32)],
        compiler_params=_cp("parallel", "parallel", "arbitrary"),
    )(a, b, *extras)


def _a_plain(tm, tk):
    return pl.BlockSpec((tm, tk), lambda i, j, kk: (i, kk))


def _a_trans(tm, tk):
    return pl.BlockSpec((tk, tm), lambda i, j, kk: (kk, i))


def _b_plain(tk, tn):
    return pl.BlockSpec((tk, tn), lambda i, j, kk: (kk, j))


def _b_trans(tk, tn):
    return pl.BlockSpec((tn, tk), lambda i, j, kk: (j, kk))


def _o_plain(tm, tn):
    return pl.BlockSpec((tm, tn), lambda i, j, kk: (i, j))


def _o_colshard(tm, tn, ns_cols):
    per = ns_cols // tn
    return pl.BlockSpec((None, tm, tn), lambda i, j, kk: (j // per, i, j % per))


def _w_cols(tk, tn, ns):
    per = ns // tn
    return pl.BlockSpec((None, tk, tn), lambda i, j, kk: (j // per, kk, j % per))


def _w_cols_t(tk, tn, ns):
    per = ns // tk
    return pl.BlockSpec((None, tn, tk), lambda i, j, kk: (kk // per, j, kk % per))


def _w_rows(tk, tn, ks):
    per = ks // tk
    return pl.BlockSpec((None, tk, tn), lambda i, j, kk: (kk // per, kk % per, j))


def _w_rows_t(tk, tn, ks):
    per = ks // tn
    return pl.BlockSpec((None, tn, tk), lambda i, j, kk: (j // per, j % per, kk))


def _sds(shape, dtype):
    return jax.ShapeDtypeStruct(shape, dtype)


def _rms_fwd(name, x, gain):
    t = x.shape[0]
    tt = min(256, t)

    def body(x_ref, g_ref, u_ref):
        xv = x_ref[...]
        r = lax.rsqrt(jnp.mean(xv * xv, axis=-1, keepdims=True) + EPS)
        u_ref[...] = (xv * r * g_ref[...]).astype(u_ref.dtype)

    return pl.pallas_call(
        body, name=name, grid=(t // tt,),
        in_specs=[pl.BlockSpec((tt, D), lambda i: (i, 0)), pl.BlockSpec((1, D), lambda i: (0, 0))],
        out_specs=pl.BlockSpec((tt, D), lambda i: (i, 0)), out_shape=_sds((t, D), BF16),
        compiler_params=_cp("parallel"),
    )(x, gain)


def _rms_bwd(name, du, x, gain, dres):
    t = x.shape[0]
    tt = min(256, t)

    def body(du_ref, x_ref, g_ref, dres_ref, dx_ref, dg_ref):
        @pl.when(pl.program_id(0) == 0)
        def _():
            dg_ref[...] = jnp.zeros_like(dg_ref)

        xv, duv = x_ref[...], du_ref[...]
        r = lax.rsqrt(jnp.mean(xv * xv, axis=-1, keepdims=True) + EPS)
        nx = xv * r
        dn = duv * g_ref[...]
        dg_ref[...] += jnp.sum(duv * nx, axis=0, keepdims=True)
        dx_ref[...] = dres_ref[...] + r * (dn - nx * jnp.mean(dn * nx, axis=-1, keepdims=True))

    row = pl.BlockSpec((tt, D), lambda i: (i, 0))
    vec = pl.BlockSpec((1, D), lambda i: (0, 0))
    return pl.pallas_call(
        body, name=name, grid=(t // tt,), in_specs=[row, row, vec, row], out_specs=[row, vec],
        out_shape=[_sds((t, D), F32), _sds((1, D), F32)], compiler_params=_cp("arbitrary"),
    )(du, x, gain, dres)


def _loss_head(x, gain, target):
    t = x.shape[0]
    tt = min(256, t)

    def body(x_ref, g_ref, t_ref, loss_ref, dx_ref, dg_ref):
        @pl.when(pl.program_id(0) == 0)
        def _():
            dg_ref[...] = jnp.zeros_like(dg_ref)
            loss_ref[...] = jnp.zeros_like(loss_ref)

        xv = x_ref[...]
        r = lax.rsqrt(jnp.mean(xv * xv, axis=-1, keepdims=True) + EPS)
        nx = xv * r
        err = nx * g_ref[...] - t_ref[...]
        loss_ref[...] += 0.5 * jnp.sum(jnp.mean(err * err, axis=-1, keepdims=True), axis=0, keepdims=True)
        dy = err * (1.0 / D)
        dn = dy * g_ref[...]
        dg_ref[...] += jnp.sum(dy * nx, axis=0, keepdims=True)
        dx_ref[...] = r * (dn - nx * jnp.mean(dn * nx, axis=-1, keepdims=True))

    row = pl.BlockSpec((tt, D), lambda i: (i, 0))
    vec = pl.BlockSpec((1, D), lambda i: (0, 0))
    one = pl.BlockSpec((1, 1), lambda i: (0, 0))
    return pl.pallas_call(
        body, name="loss_head", grid=(t // tt,), in_specs=[row, vec, row], out_specs=[one, row, vec],
        out_shape=[_sds((1, 1), F32), _sds((t, D), F32), _sds((1, D), F32)], compiler_params=_cp("arbitrary"),
    )(x, gain, target)


def _shift_down(v, s, t_idx):
    return jnp.where(t_idx >= s, pltpu.roll(v, s, 0), 0.0)


def _shift_up(v, s, t_idx, t):
    return jnp.where(t_idx < t - s, pltpu.roll(v, t - s, 0), 0.0)


def _pool_d(p, g, t_idx):
    s = p
    for step in range(g + 1):
        s = s + _shift_down(s, 1 << step, t_idx)
    cnt = jnp.minimum(t_idx + 1, POOL_WINDOWS[g]).astype(F32)
    return s / cnt - p, cnt


def _pool_fwd(proj, pool_w, pool_scale):
    t = proj.shape[0]
    g128 = POOL_W // len(POOL_WINDOWS)

    def body(p_ref, w_ref, s_ref, y_ref):
        t_idx = lax.broadcasted_iota(jnp.int32, (t, g128), 0)
        for g in range(len(POOL_WINDOWS)):
            sl = slice(g * g128, (g + 1) * g128)
            d, _ = _pool_d(p_ref[:, sl], g, t_idx)
            y_ref[:, sl] = (_bdot(d, w_ref[g]) * s_ref[:, sl]).astype(y_ref.dtype)

    return pl.pallas_call(
        body, name="pool_fwd", grid=(1,),
        in_specs=[pl.BlockSpec((t, POOL_W), lambda i: (0, OFF_P // POOL_W)),
                  pl.BlockSpec((4, g128, g128), lambda i: (0, 0, 0)), pl.BlockSpec((1, POOL_W), lambda i: (0, 0))],
        out_specs=pl.BlockSpec((t, POOL_W), lambda i: (0, 0)), out_shape=_sds((t, POOL_W), BF16),
        compiler_params=_cp("arbitrary"),
    )(proj, pool_w, pool_scale)


def _pool_bwd(proj, pool_w, pool_scale, dy):
    t = proj.shape[0]
    g128 = POOL_W // len(POOL_WINDOWS)

    def body(p_ref, w_ref, s_ref, dy_ref, dp_ref, dw_ref, ds_ref):
        t_idx = lax.broadcasted_iota(jnp.int32, (t, g128), 0)
        for g in range(len(POOL_WINDOWS)):
            sl = slice(g * g128, (g + 1) * g128)
            d, cnt = _pool_d(p_ref[:, sl], g, t_idx)
            dyv = dy_ref[:, sl].astype(F32)
            ds_ref[:, sl] = jnp.sum(dyv * _bdot(d, w_ref[g]), axis=0, keepdims=True)
            dys = dyv * s_ref[:, sl]
            dw_ref[g] = _bdot(d, dys, TN)
            dd = _bdot(dys, w_ref[g], NT)
            s = dd / cnt
            for step in range(g + 1):
                s = s + _shift_up(s, 1 << step, t_idx, t)
            dp_ref[:, sl] = (s - dd).astype(dp_ref.dtype)

    return pl.pallas_call(
        body, name="pool_bwd", grid=(1,),
        in_specs=[pl.BlockSpec((t, POOL_W), lambda i: (0, OFF_P // POOL_W)),
                  pl.BlockSpec((4, g128, g128), lambda i: (0, 0, 0)), pl.BlockSpec((1, POOL_W), lambda i: (0, 0)),
                  pl.BlockSpec((t, POOL_W), lambda i: (0, 0))],
        out_specs=[pl.BlockSpec((t, POOL_W), lambda i: (0, 0)), pl.BlockSpec((4, g128, g128), lambda i: (0, 0, 0)),
                   pl.BlockSpec((1, POOL_W), lambda i: (0, 0))],
        out_shape=[_sds((t, POOL_W), BF16), _sds((4, g128, g128), F32), _sds((1, POOL_W), F32)],
        compiler_params=_cp("arbitrary"),
    )(proj, pool_w, pool_scale, dy)


SB_GROUP = 3
SB_GW = SB_GROUP * HD


def _sb_cast_kv(proj):
    t = proj.shape[0]
    tt = min(512, t)

    def body(x_ref, o_ref):
        o_ref[...] = x_ref[...].astype(BF16)

    return pl.pallas_call(
        body, name="sb_cast_kv", grid=(t // tt, 2),
        in_specs=[pl.BlockSpec((tt, SB_W), lambda i, j: (i, OFF_SB // SB_W + 1 + j))],
        out_specs=pl.BlockSpec((tt, SB_W), lambda i, j: (i, j)), out_shape=_sds((t, 2 * SB_W), BF16),
        compiler_params=_cp("parallel", "parallel"),
    )(proj)


def _sb_specs(t):
    q_spec = pl.BlockSpec((SB_BLOCK, SB_GW), lambda g, i: (i, OFF_SB // SB_GW + g))
    k_spec = pl.BlockSpec((t, SB_GW), lambda g, i: (0, g))
    v_spec = pl.BlockSpec((t, SB_GW), lambda g, i: (0, SB_W // SB_GW + g))
    return q_spec, k_spec, v_spec


def _head(ref, h, rows=None):
    cols = slice(h * HD, (h + 1) * HD)
    return ref[:, cols] if rows is None else ref[rows, cols]


SB_KEYS = 512


def _sub(v, b):
    return v[:, b * SB_BLOCK:(b + 1) * SB_BLOCK]


def _sb_keep(kc, limit):
    row = lax.broadcasted_iota(jnp.int32, (SB_BLOCK, kc), 0)
    col = lax.broadcasted_iota(jnp.int32, (SB_BLOCK, kc), 1)
    return col < row + limit


def _sb_chunk(q, keys, run, later, limit):
    kc = keys.shape[0]
    z = _dot(q, keys, NT)
    lsz = jax.nn.log_sigmoid(z)
    ls = lsz - z
    if limit is not None:
        keep = _sb_keep(kc, limit)
        ls = jnp.where(keep, ls, 0.0)
    parts = [None] * (kc // SB_BLOCK)
    for b in reversed(range(kc // SB_BLOCK)):
        parts[b] = _hdot(_sub(ls, b), later) + run
        run = run + jnp.sum(_sub(ls, b), axis=1, keepdims=True)
    a = jnp.exp(lsz + jnp.concatenate(parts, axis=1))
    if limit is not None:
        a = jnp.where(keep, a, 0.0)
    return z, a, run


def _sb_fwd(proj, kv):
    t = proj.shape[0]
    kc = min(SB_KEYS, t)
    scale = HD ** -0.5

    def body(q_ref, k_ref, v_ref, o_ref):
        i = pl.program_id(1)
        top = (i * SB_BLOCK) // kc
        qs = [(_head(q_ref, h) * scale).astype(BF16) for h in range(SB_GROUP)]
        row = lax.broadcasted_iota(jnp.int32, (SB_BLOCK, SB_BLOCK), 0)
        col = lax.broadcasted_iota(jnp.int32, (SB_BLOCK, SB_BLOCK), 1)
        later = (row > col).astype(F32)

        def chunk(jc, carry, masked):
            rows = pl.ds(pl.multiple_of(jc * kc, kc), kc)
            limit = i * SB_BLOCK - jc * kc if masked else None
            out = []
            for h in range(SB_GROUP):
                acc, run = carry[h]
                _, a, run = _sb_chunk(qs[h], _head(k_ref, h, rows), run, later, limit)
                out.append((acc + _dot(a.astype(BF16), _head(v_ref, h, rows)), run))
            return tuple(out)

        zero = tuple((jnp.zeros((SB_BLOCK, HD), F32), jnp.zeros((SB_BLOCK, 1), F32)) for _ in range(SB_GROUP))
        carry = chunk(top, zero, True)
        carry = lax.fori_loop(0, top, lambda jj, c: chunk(top - 1 - jj, c, False), carry)
        for h in range(SB_GROUP):
            o_ref[:, h * HD:(h + 1) * HD] = carry[h][0].astype(o_ref.dtype)

    return pl.pallas_call(
        body, name="sb_fwd", grid=(HEADS // SB_GROUP, t // SB_BLOCK), in_specs=list(_sb_specs(t)),
        out_specs=pl.BlockSpec((SB_BLOCK, SB_GW), lambda g, i: (i, g)), out_shape=_sds((t, SB_W), BF16),
        compiler_params=_cp("parallel", "arbitrary"),
    )(proj, kv, kv)


def _sb_bwd(proj, kv, dy):
    t = proj.shape[0]
    nq = t // SB_BLOCK
    kc = min(SB_KEYS, t)
    scale = HD ** -0.5

    def body(q_ref, k_ref, v_ref, do_ref, dq_ref, dk_ref, dv_ref, z_scr, e_scr):
        i = pl.program_id(1)
        top = (i * SB_BLOCK) // kc

        @pl.when(i == 0)
        def _():
            dk_ref[...] = jnp.zeros_like(dk_ref)
            dv_ref[...] = jnp.zeros_like(dv_ref)

        qs = [(_head(q_ref, h) * scale).astype(BF16) for h in range(SB_GROUP)]
        dos = [_head(do_ref, h).astype(BF16) for h in range(SB_GROUP)]
        row = lax.broadcasted_iota(jnp.int32, (SB_BLOCK, SB_BLOCK), 0)
        col = lax.broadcasted_iota(jnp.int32, (SB_BLOCK, SB_BLOCK), 1)
        later = (row > col).astype(F32)
        earlier = (row < col).astype(F32)

        def down(jc, runs, masked):
            rows = pl.ds(pl.multiple_of(jc * kc, kc), kc)
            limit = i * SB_BLOCK - jc * kc if masked else None
            out = []
            for h in range(SB_GROUP):
                z, a, run = _sb_chunk(qs[h], _head(k_ref, h, rows), runs[h], later, limit)
                z_scr[h, jc] = z
                e_scr[h, jc] = a * _dot(dos[h], _head(v_ref, h, rows), NT)
                dv_ref[rows, h * HD:(h + 1) * HD] += _dot(a.astype(BF16), dos[h], TN)
                out.append(run)
            return tuple(out)

        zero = tuple(jnp.zeros((SB_BLOCK, 1), F32) for _ in range(SB_GROUP))
        runs = down(top, zero, True)
        lax.fori_loop(0, top, lambda jj, r: down(top - 1 - jj, r, False), runs)

        def up(jc, carry, masked):
            rows = pl.ds(pl.multiple_of(jc * kc, kc), kc)
            out = []
            for h in range(SB_GROUP):
                dq, run = carry[h]
                z, e = z_scr[h, jc], e_scr[h, jc]
                parts = []
                for b in range(kc // SB_BLOCK):
                    parts.append(_hdot(_sub(e, b), earlier) + run)
                    run = run + jnp.sum(_sub(e, b), axis=1, keepdims=True)
                sz = jax.nn.sigmoid(z)
                dz = e * (1.0 - sz) - jnp.concatenate(parts, axis=1) * sz
                if masked:
                    dz = jnp.where(_sb_keep(kc, i * SB_BLOCK - jc * kc), dz, 0.0)
                dz = dz.astype(BF16)
                dk_ref[rows, h * HD:(h + 1) * HD] += _dot(dz, qs[h], TN)
                out.append((dq + _dot(dz, _head(k_ref, h, rows)), run))
            return tuple(out)

        zero = tuple((jnp.zeros((SB_BLOCK, HD), F32), jnp.zeros((SB_BLOCK, 1), F32)) for _ in range(SB_GROUP))
        carry = lax.fori_loop(0, top, lambda jc, c: up(jc, c, False), zero)
        carry = up(top, carry, True)
        for h in range(SB_GROUP):
            dq_ref[:, h * HD:(h + 1) * HD] = (carry[h][0] * scale).astype(dq_ref.dtype)

    blk = pl.BlockSpec((SB_BLOCK, SB_GW), lambda g, i: (i, g))
    seq = pl.BlockSpec((t, SB_GW), lambda g, i: (0, g))
    scratch = pltpu.VMEM((SB_GROUP, t // kc, SB_BLOCK, kc), F32)
    return pl.pallas_call(
        body, name="sb_bwd", grid=(HEADS // SB_GROUP, nq), in_specs=[*_sb_specs(t), blk], out_specs=[blk, seq, seq],
        out_shape=[_sds((t, SB_W), BF16), _sds((t, SB_W), F32), _sds((t, SB_W), F32)],
        scratch_shapes=[scratch, scratch], compiler_params=_cp("parallel", "arbitrary"),
    )(proj, kv, kv, dy)


CONV_TILE = 256
GDN_CONV = 4


def _conv_pre(x, w_ref, t_idx):
    pre = w_ref[GDN_CONV - 1:GDN_CONV, :] * x
    for s in range(1, GDN_CONV):
        pre = pre + w_ref[GDN_CONV - 1 - s:GDN_CONV - s, :] * _shift_down(x, s, t_idx)
    return pre


def _conv_fwd(proj, conv_w):
    t = proj.shape[0]
    width = conv_w.shape[1]

    def body(x_ref, w_ref, y_ref):
        t_idx = lax.broadcasted_iota(jnp.int32, (t, CONV_TILE), 0)
        pre = _conv_pre(x_ref[...], w_ref, t_idx)
        y_ref[...] = pre * jax.nn.sigmoid(pre)

    return pl.pallas_call(
        body, name="conv_fwd", grid=(width // CONV_TILE,),
        in_specs=[pl.BlockSpec((t, CONV_TILE), lambda c: (0, OFF_GQKV // CONV_TILE + c)),
                  pl.BlockSpec((GDN_CONV, CONV_TILE), lambda c: (0, c))],
        out_specs=pl.BlockSpec((t, CONV_TILE), lambda c: (0, c)), out_shape=_sds((t, width), F32),
        compiler_params=_cp("parallel"),
    )(proj, conv_w)


def _conv_bwd(proj, conv_w, dc):
    t = proj.shape[0]
    width = dc.shape[1]
    per = width // CONV_TILE
    part = 0

    def body(x_ref, w_ref, dc_ref, dx_ref, dw_ref):
        t_idx = lax.broadcasted_iota(jnp.int32, (t, CONV_TILE), 0)
        x = x_ref[...]
        pre = _conv_pre(x, w_ref, t_idx)
        sg = jax.nn.sigmoid(pre)
        dpre = dc_ref[...] * (sg * (1.0 + pre * (1.0 - sg)))
        dx = w_ref[GDN_CONV - 1:GDN_CONV, :] * dpre
        dw_ref[GDN_CONV - 1:GDN_CONV, :] = jnp.sum(dpre * x, axis=0, keepdims=True)
        for s in range(1, GDN_CONV):
            dx = dx + w_ref[GDN_CONV - 1 - s:GDN_CONV - s, :] * _shift_up(dpre, s, t_idx, t)
            dw_ref[GDN_CONV - 1 - s:GDN_CONV - s, :] = jnp.sum(dpre * _shift_down(x, s, t_idx), axis=0, keepdims=True)
        dx_ref[...] = dx.astype(dx_ref.dtype)

    return pl.pallas_call(
        body, name="conv_bwd", grid=(per,),
        in_specs=[pl.BlockSpec((t, CONV_TILE), lambda c: (0, OFF_GQKV // CONV_TILE + part * per + c)),
                  pl.BlockSpec((GDN_CONV, CONV_TILE), lambda c: (0, part * per + c)),
                  pl.BlockSpec((t, CONV_TILE), lambda c: (0, c))],
        out_specs=[pl.BlockSpec((t, CONV_TILE), lambda c: (0, c)), pl.BlockSpec((GDN_CONV, CONV_TILE), lambda c: (0, c))],
        out_shape=[_sds((t, width), BF16), _sds((GDN_CONV, width), F32)],
        compiler_params=_cp("parallel"),
    )(proj, conv_w, dc)


def _heads(x):
    return jnp.concatenate([x[:, h * HD:(h + 1) * HD][None] for h in range(HEADS)], axis=0)


def _hb(a, b, ca=2, cb=1):
    return lax.dot_general(a, b, (((ca,), (cb,)), ((0,), (0,))), precision=HIGH, preferred_element_type=F32)


def _gdn_prep(cq, ck, cv, ab, alog_row, dtb_row):
    c = GDN_CHUNK
    row = lax.broadcasted_iota(jnp.int32, (c, c), 0)
    col = lax.broadcasted_iota(jnp.int32, (c, c), 1)
    incl, strict, eye = row >= col, row > col, row == col
    def lanes(v, first):
        return jnp.concatenate([v[:, first + h:first + h + 1][None] for h in range(HEADS)], axis=0)

    a_col, b_col = lanes(ab, 0), lanes(ab, HEADS)
    a_log, dt_bias = lanes(alog_row, 0), lanes(dtb_row, 0)
    qn = cq * lax.rsqrt(jnp.sum(cq * cq, axis=-1, keepdims=True) + EPS) * (HD ** -0.5)
    kn = ck * lax.rsqrt(jnp.sum(ck * ck, axis=-1, keepdims=True) + EPS)
    la_col = -jnp.exp(a_log) * jax.nn.softplus(a_col + dt_bias)
    beta = jax.nn.sigmoid(b_col)
    la_row = jnp.sum(jnp.where(eye, la_col, 0.0), axis=1, keepdims=True)
    g_col = jnp.sum(jnp.where(incl, la_row, 0.0), axis=2, keepdims=True)
    g_row = jnp.sum(jnp.where(row <= col, la_col, 0.0), axis=1, keepdims=True)
    g_last = jnp.sum(la_col, axis=1, keepdims=True)
    gamma = jnp.where(incl, jnp.exp(jnp.where(incl, g_col - g_row, 0.0)), 0.0)
    lower = jnp.where(strict, beta * _hb(kn, kn, 2, 2) * gamma, 0.0)
    inv = jnp.where(eye, 1.0, 0.0) - lower
    pw = _hb(lower, lower)
    for step in range(5):
        inv = inv + _hb(inv, pw)
        if step < 4:
            pw = _hb(pw, pw)
    u = _hb(inv, cv * beta)
    w = _hb(inv, kn * (beta * jnp.exp(g_col)))
    qk = _hb(qn, kn, 2, 2) * gamma
    return u, w, qk, qn * jnp.exp(g_col), kn * jnp.exp(g_last - g_col), jnp.exp(g_last)


def _gdn_post(o, z, gain):
    y = o * lax.rsqrt(jnp.mean(o * o, axis=-1, keepdims=True) + EPS) * gain
    return y * (z * jax.nn.sigmoid(z))


def _gdn_specs(nc, reverse):
    c = GDN_CHUNK

    def ch(n):
        return nc - 1 - n if reverse else n

    def wide(array_off):
        return pl.BlockSpec((c, GDN_W), lambda n: (ch(n), array_off // GDN_W))

    ab = pl.BlockSpec((c, HD), lambda n: (ch(n), OFF_AB // HD))
    row = pl.BlockSpec((1, HD), lambda n: (0, 0))
    state = pl.BlockSpec((None, HEADS, HD, HD), lambda n: (ch(n), 0, 0, 0))
    return wide, ab, row, state


def _gdn_fwd(cqkv, proj, a_log, dt_bias, gain):
    t = proj.shape[0]
    nc = t // GDN_CHUNK
    wide, ab, row, state = _gdn_specs(nc, False)

    def body(cq_ref, ck_ref, cv_ref, ab_ref, z_ref, al_ref, dt_ref, g_ref, y_ref, sprev_ref, s_scr):
        @pl.when(pl.program_id(0) == 0)
        def _():
            s_scr[...] = jnp.zeros_like(s_scr)

        u, w, qk, qd, kd, dec = _gdn_prep(_heads(cq_ref[...]), _heads(ck_ref[...]), _heads(cv_ref[...]), ab_ref[...],
                                          al_ref[...], dt_ref[...])
        s = s_scr[...]
        sprev_ref[...] = s
        v_new = u - _hb(w, s)
        o = _hb(qd, s) + _hb(qk, v_new)
        s_scr[...] = s * dec + _hb(kd, v_new, 1, 1)
        y = _gdn_post(o, _heads(z_ref[...]), g_ref[...])
        for h in range(HEADS):
            y_ref[:, h * HD:(h + 1) * HD] = y[h].astype(y_ref.dtype)

    return pl.pallas_call(
        body, name="gdn_fwd", grid=(nc,),
        in_specs=[wide(0), wide(GDN_W), wide(2 * GDN_W), ab, wide(OFF_Z), row, row, row],
        out_specs=[wide(0), state], out_shape=[_sds((t, GDN_W), BF16), _sds((nc, HEADS, HD, HD), F32)],
        scratch_shapes=[pltpu.VMEM((HEADS, HD, HD), F32)], compiler_params=_cp("arbitrary"),
    )(cqkv, cqkv, cqkv, proj, proj, a_log, dt_bias, gain)


def _gdn_bwd(cqkv, proj, a_log, dt_bias, gain, sprev, dy):
    t = proj.shape[0]
    nc = t // GDN_CHUNK
    wide, ab, row, state = _gdn_specs(nc, True)

    def body(cq_ref, ck_ref, cv_ref, ab_ref, z_ref, al_ref, dt_ref, g_ref, sp_ref, dy_ref,
             dc_ref, dab_ref, dz_ref, dal_ref, ddt_ref, dg_ref, ds_scr):
        @pl.when(pl.program_id(0) == 0)
        def _():
            ds_scr[...] = jnp.zeros_like(ds_scr)
            dal_ref[...] = jnp.zeros_like(dal_ref)
            ddt_ref[...] = jnp.zeros_like(ddt_ref)
            dg_ref[...] = jnp.zeros_like(dg_ref)

        (u, w, qk, qd, kd, dec), prep_vjp = jax.vjp(
            _gdn_prep, _heads(cq_ref[...]), _heads(ck_ref[...]), _heads(cv_ref[...]), ab_ref[...], al_ref[...], dt_ref[...])
        s = sp_ref[...]
        v_new = u - _hb(w, s)
        o = _hb(qd, s) + _hb(qk, v_new)
        _, post_vjp = jax.vjp(_gdn_post, o, _heads(z_ref[...]), g_ref[...])
        do, dz, dgain = post_vjp(_heads(dy_ref[...]).astype(F32))
        ds_next = ds_scr[...]
        d_vnew = _hb(qk, do, 1, 1) + _hb(kd, ds_next)
        d_qk = _hb(do, v_new, 2, 2)
        d_qd = _hb(do, s, 2, 2)
        d_kd = _hb(v_new, ds_next, 2, 2)
        d_dec = jnp.sum(jnp.sum(s * ds_next, axis=2, keepdims=True), axis=1, keepdims=True)
        ds_scr[...] = dec * ds_next + _hb(qd, do, 1, 1) - _hb(w, d_vnew, 1, 1)
        d_w = -_hb(d_vnew, s, 2, 2)
        dcq, dck, dcv, dab, dal, ddt = prep_vjp((d_vnew, d_w, d_qk, d_qd, d_kd, d_dec))
        for h in range(HEADS):
            dc_ref[:, h * HD:(h + 1) * HD] = dcq[h]
            dc_ref[:, GDN_W + h * HD:GDN_W + (h + 1) * HD] = dck[h]
            dc_ref[:, 2 * GDN_W + h * HD:2 * GDN_W + (h + 1) * HD] = dcv[h]
            dz_ref[:, h * HD:(h + 1) * HD] = dz[h].astype(dz_ref.dtype)
        dab_ref[...] = dab
        dal_ref[...] += dal
        ddt_ref[...] += ddt
        dg_ref[...] += dgain

    c = GDN_CHUNK
    return pl.pallas_call(
        body, name="gdn_bwd", grid=(nc,),
        in_specs=[wide(0), wide(GDN_W), wide(2 * GDN_W), ab, wide(OFF_Z), row, row, row, state, wide(0)],
        out_specs=[pl.BlockSpec((c, 3 * GDN_W), lambda n: (nc - 1 - n, 0)), pl.BlockSpec((c, HD), lambda n: (nc - 1 - n, 0)),
                   wide(0), row, row, row],
        out_shape=[_sds((t, 3 * GDN_W), F32), _sds((t, HD), F32), _sds((t, GDN_W), BF16),
                   _sds((1, HD), F32), _sds((1, HD), F32), _sds((1, HD), F32)],
        scratch_shapes=[pltpu.VMEM((HEADS, HD, HD), F32)], compiler_params=_cp("arbitrary"),
    )(cqkv, cqkv, cqkv, proj, proj, a_log, dt_bias, gain, sprev, dy)


MERGE_TN = 512


def _merge_specs(t, tm):
    tn = MERGE_TN
    ys = [pl.BlockSpec((tm, wd), lambda i, j: (i, 0)) for wd in (POOL_W, SB_W, GDN_W)]
    ws = [pl.BlockSpec((None, wd, tn), lambda i, j: (j, 0, 0)) for wd in (POOL_W, SB_W, GDN_W)]
    gs = [pl.BlockSpec((tm, tn), functools.partial(lambda i, j, b: (i, OFF_GATE // tn + b * (D // tn) + j), b=b))
          for b in range(3)]
    out = pl.BlockSpec((tm, tn), lambda i, j: (i, j))
    return ys, ws, gs, out


def _merge_fwd(ys, wups, proj):
    t = proj.shape[0]
    tm = min(512, t)
    y_specs, w_specs, g_specs, out = _merge_specs(t, tm)

    def body(y0, y1, y2, w0, w1, w2, g0, g1, g2, o_ref):
        acc = jnp.zeros(o_ref.shape, F32)
        for y, w, g in ((y0, w0, g0), (y1, w1, g1), (y2, w2, g2)):
            acc = acc + jax.nn.sigmoid(g[...]) * _dot(y[...], w[...])
        o_ref[...] = acc.astype(o_ref.dtype)

    return pl.pallas_call(
        body, name="merge_fwd", grid=(t // tm, D // MERGE_TN), in_specs=[*y_specs, *w_specs, *g_specs],
        out_specs=out, out_shape=_sds((t, D), BF16), compiler_params=_cp("parallel", "parallel"),
    )(*ys, *wups, proj, proj, proj)


def _merge_bwd(ys, wups, proj, dmerged):
    t = proj.shape[0]
    tm = min(512, t)
    y_specs, w_specs, g_specs, out = _merge_specs(t, tm)

    def body(y0, y1, y2, w0, w1, w2, g0, g1, g2, dm_ref, dg0, dg1, dg2, dm0, dm1, dm2):
        dm = dm_ref[...].astype(F32)
        for y, w, g, dg, dmb in ((y0, w0, g0, dg0, dm0), (y1, w1, g1, dg1, dm1), (y2, w2, g2, dg2, dm2)):
            sg = jax.nn.sigmoid(g[...])
            dg[...] = (dm * _dot(y[...], w[...]) * sg * (1.0 - sg)).astype(dg.dtype)
            dmb[...] = (dm * sg).astype(dmb.dtype)

    return pl.pallas_call(
        body, name="merge_bwd", grid=(t // tm, D // MERGE_TN), in_specs=[*y_specs, *w_specs, *g_specs, out],
        out_specs=[out] * 6, out_shape=[_sds((t, D), BF16)] * 6, compiler_params=_cp("parallel", "parallel"),
    )(*ys, *wups, proj, proj, proj, dmerged)


def _tile(t, want):
    return min(t, want)


def _layer_fwd(x, l, gw, w_al, sp, reached=None):
    t = x.shape[0]
    tm = _tile(t, 1024)
    u = _rms_fwd("rms_attn", x, sp["attn_norm"][l])
    proj = _mm("proj", u, w_al, m=t, n=N_AL, k=D, tm=tm, tn=1024, tk=D, a_spec=_a_plain(tm, D),
               b_spec=_b_plain(D, 1024), dims=None, out_shapes=[_sds((t, N_AL), F32)], out_specs=[_o_plain(tm, 1024)])[0]
    if reached is not None:
        reached("proj", proj)
    y_pool = _pool_fwd(proj, sp["pool_w"][l], sp["pool_scale"][l])
    kv = _sb_cast_kv(proj)
    y_sb = _sb_fwd(proj, kv)
    cqkv = _conv_fwd(proj, sp["conv"][l])
    y_gdn, sprev = _gdn_fwd(cqkv, proj, sp["a_log"][l], sp["dt_bias"][l], sp["gdn_norm"][l])
    ys = (y_pool, y_sb, y_gdn)
    wups = (gw["w_pool_up"], gw["w_sb_up"], gw["w_gdn_up"])
    merged = _merge_fwd(ys, wups, proj)
    if reached is not None:
        reached("merged", merged)
    x1 = _mm("out_proj", merged, gw["w_out"], m=t, n=D, k=D, tm=tm, tn=1024, tk=512, a_spec=_a_plain(tm, 512),
             b_spec=_w_rows(512, 1024, 512), dims=None, out_shapes=[_sds((t, D), F32)], out_specs=[_o_plain(tm, 1024)],
             extras=[x], extra_specs=[_o_plain(tm, 1024)], epilogue=lambda r, xr: (r + xr,))[0]
    u2 = _rms_fwd("rms_mlp", x1, sp["mlp_norm"][l])

    def relu2(r):
        hv = jnp.maximum(r, 0.0)
        return hv, hv * hv

    hid, hid2 = _mm("ff1", u2, gw["w_ff1"], m=t, n=D_FF, k=D, tm=tm, tn=1024, tk=D, a_spec=_a_plain(tm, D),
                    b_spec=_w_cols(D, 1024, 2048), dims=None, out_shapes=[_sds((t, D_FF), BF16)] * 2,
                    out_specs=[_o_plain(tm, 1024)] * 2, epilogue=relu2)
    x2 = _mm("ff2", hid2, gw["w_ff2"], m=t, n=D, k=D_FF, tm=tm, tn=1024, tk=2048, a_spec=_a_plain(tm, 2048),
             b_spec=_w_rows(2048, 1024, 2048), dims=None, out_shapes=[_sds((t, D), F32)], out_specs=[_o_plain(tm, 1024)],
             extras=[x1], extra_specs=[_o_plain(tm, 1024)], epilogue=lambda r, xr: (r + xr,))[0]
    saved = dict(x=x, u=u, proj=proj, kv=kv, cqkv=cqkv, sprev=sprev, ys=ys, merged=merged, x1=x1, u2=u2, hid=hid, hid2=hid2)
    return x2, saved


def _layer_bwd(dx2, l, gw, w_al, sp, sv, emit=None):
    t = dx2.shape[0]
    tm = _tile(t, 1024)
    tk = t
    g = {}
    if emit is None:
        emit = lambda names, grads, v: v
    dpre = _mm("ff2_dx", dx2, gw["w_ff2"], m=t, n=D_FF, k=D, tm=tm, tn=1024, tk=D, a_spec=_a_plain(tm, D),
               b_spec=_w_rows_t(D, 1024, 2048), dims=NT, out_shapes=[_sds((t, D_FF), BF16)],
               out_specs=[_o_plain(tm, 1024)], extras=[sv["hid"]], extra_specs=[_o_plain(tm, 1024)],
               epilogue=lambda r, hv: (r * (2.0 * hv.astype(F32)),))[0]
    g["w_ff2"] = _mm("ff2_dw", sv["hid2"], dx2, m=D_FF, n=D, k=t, tm=1024, tn=1024, tk=tk, a_spec=_a_trans(1024, tk),
                     b_spec=_b_plain(tk, 1024), dims=TN, out_shapes=[_sds((D_FF, D), BF16)],
                     out_specs=[_o_plain(1024, 1024)])[0].reshape(N_CHIPS, D_FF // N_CHIPS, D)
    du2 = _mm("ff1_dx", dpre, gw["w_ff1"], m=t, n=D, k=D_FF, tm=tm, tn=1024, tk=2048, a_spec=_a_plain(tm, 2048),
              b_spec=_w_cols_t(2048, 1024, 2048), dims=NT, out_shapes=[_sds((t, D), F32)], out_specs=[_o_plain(tm, 1024)])[0]
    g["w_ff1"] = _mm("ff1_dw", sv["u2"], dpre, m=D, n=D_FF, k=t, tm=1024, tn=1024, tk=tk, a_spec=_a_trans(1024, tk),
                     b_spec=_b_plain(tk, 1024), dims=TN, out_shapes=[_sds((N_CHIPS, D, D_FF // N_CHIPS), BF16)],
                     out_specs=[_o_colshard(1024, 1024, D_FF // N_CHIPS)])[0]
    dx1, g["mlp_norm"] = _rms_bwd("rms_mlp_bwd", du2, sv["x1"], sp["mlp_norm"][l], dx2)
    dx1 = emit(("w_ff1", "w_ff2"), g, dx1)
    dmerged = _mm("out_dx", dx1, gw["w_out"], m=t, n=D, k=D, tm=tm, tn=512, tk=D, a_spec=_a_plain(tm, D),
                  b_spec=_w_rows_t(D, 512, 512), dims=NT, out_shapes=[_sds((t, D), BF16)], out_specs=[_o_plain(tm, 512)])[0]
    g["w_out"] = _mm("out_dw", sv["merged"], dx1, m=D, n=D, k=t, tm=1024, tn=1024, tk=tk, a_spec=_a_trans(1024, tk),
                     b_spec=_b_plain(tk, 1024), dims=TN, out_shapes=[_sds((D, D), BF16)],
                     out_specs=[_o_plain(1024, 1024)])[0].reshape(N_CHIPS, D // N_CHIPS, D)
    wups = (gw["w_pool_up"], gw["w_sb_up"], gw["w_gdn_up"])
    dg0, dg1, dg2, dm0, dm1, dm2 = _merge_bwd(sv["ys"], wups, sv["proj"], dmerged)
    dys = []
    for nm, yb, dmb, wd in zip(("w_pool_up", "w_sb_up", "w_gdn_up"), sv["ys"], (dm0, dm1, dm2), (POOL_W, SB_W, GDN_W)):
        dys.append(_mm(nm + "_dx", dmb, gw[nm], m=t, n=wd, k=D, tm=tm, tn=256, tk=512, a_spec=_a_plain(tm, 512),
                       b_spec=_w_cols_t(512, 256, 512), dims=NT, out_shapes=[_sds((t, wd), F32)],
                       out_specs=[_o_plain(tm, 256)])[0])
        g[nm] = _mm(nm + "_dw", yb, dmb, m=wd, n=D, k=t, tm=256, tn=512, tk=tk, a_spec=_a_trans(256, tk),
                    b_spec=_b_plain(tk, 512), dims=TN, out_shapes=[_sds((N_CHIPS, wd, D // N_CHIPS), BF16)],
                    out_specs=[_o_colshard(256, 512, D // N_CHIPS)])[0]
    dys[2] = emit(("w_pool_up", "w_sb_up", "w_gdn_up", "w_out"), g, dys[2])
    proj = sv["proj"]
    dp, g["pool_w"], g["pool_scale"] = _pool_bwd(proj, sp["pool_w"][l], sp["pool_scale"][l], dys[0])
    dsq, dsk, dsv = _sb_bwd(proj, sv["kv"], dys[1])
    dc, dab, dz, g["a_log"], g["dt_bias"], g["gdn_norm"] = _gdn_bwd(
        sv["cqkv"], proj, sp["a_log"][l], sp["dt_bias"][l], sp["gdn_norm"][l], sv["sprev"], dys[2])
    dgx, g["conv"] = _conv_bwd(proj, sp["conv"][l], dc)
    dproj = jnp.concatenate(
        [dsq, dsk.astype(BF16), dsv.astype(BF16), dgx, dz, dab.astype(BF16), jnp.zeros((t, AB_W - HD), BF16),
         dp, dg0, dg1, dg2], axis=1)
    du = _mm("proj_dx", dproj, w_al, m=t, n=D, k=N_AL, tm=tm, tn=1024, tk=2048, a_spec=_a_plain(tm, 2048),
             b_spec=_b_trans(2048, 1024), dims=NT, out_shapes=[_sds((t, D), F32)], out_specs=[_o_plain(tm, 1024)])[0]
    g["w_al"] = _mm("proj_dw", sv["u"], dproj, m=D, n=N_AL, k=t, tm=1024, tn=1024, tk=tk, a_spec=_a_trans(1024, tk),
                    b_spec=_b_plain(tk, 1024), dims=TN, out_shapes=[_sds((D, N_AL), BF16)], out_specs=[_o_plain(1024, 1024)])[0]
    dx, g["attn_norm"] = _rms_bwd("rms_attn_bwd", du, sv["x"], sp["attn_norm"][l], dx1)
    g["w_in"] = _w_in_to_shards(g["w_al"])
    dx = emit(("w_in",), g, dx)
    return dx, g


def _align_w_in(w):
    n_ab = ORIG_GATE - ORIG_AB
    return jnp.concatenate([w[:, ORIG_SB:ORIG_GATE], jnp.zeros((D, AB_W - n_ab), w.dtype), w[:, :ORIG_SB], w[:, ORIG_GATE:]],
                           axis=1)


def _unalign_w_in(w):
    n_ab = ORIG_GATE - ORIG_AB
    return jnp.concatenate([w[:, OFF_P:OFF_GATE], w[:, :OFF_AB + n_ab], w[:, OFF_GATE:]], axis=1)


W_IN_RUNS = ((0, ORIG_SB, OFF_P), (ORIG_SB, ORIG_GATE, OFF_SB), (ORIG_GATE, N_IN, OFF_GATE))
W_IN_SHARD = N_IN // N_CHIPS


def _w_in_from_shards(gathered):
    parts = []
    for lo, hi, al in sorted(W_IN_RUNS, key=lambda r: r[2]):
        if al == OFF_P:
            parts.append(jnp.zeros((D, OFF_P - (OFF_AB + ORIG_GATE - ORIG_AB)), gathered.dtype))
        while lo < hi:
            chip = lo // W_IN_SHARD
            end = min(hi, (chip + 1) * W_IN_SHARD)
            parts.append(gathered[chip, :, lo - chip * W_IN_SHARD:end - chip * W_IN_SHARD])
            lo = end
    return jnp.concatenate(parts, axis=1)


def _w_in_to_shards(g_al):
    shards = []
    for chip in range(N_CHIPS):
        a, b = chip * W_IN_SHARD, (chip + 1) * W_IN_SHARD
        parts = [g_al[:, al + max(a, lo) - lo:al + min(b, hi) - lo] for lo, hi, al in W_IN_RUNS if max(a, lo) < min(b, hi)]
        shards.append(jnp.concatenate(parts, axis=1))
    return jnp.stack(shards)


def _row128(v):
    return jnp.pad(v.reshape(1, -1), ((0, 0), (0, HD - v.shape[-1])))


def _local_step(x, target, weights_of, sp, emit=None, reached=None):
    saved, gw, w_in_al = [], [], []
    h = x
    for l in range(2):
        gw_l, w_al_l = weights_of(l)
        gw.append(gw_l)
        w_in_al.append(w_al_l)
        h, sv = _layer_fwd(h, l, gw_l, w_al_l, sp, None if reached is None else functools.partial(reached, l))
        saved.append(sv)
    loss, dh, g_final = _loss_head(h, sp["final_norm"], target)
    grads = [None, None]
    for l in (1, 0):
        dh, grads[l] = _layer_bwd(dh, l, gw[l], w_in_al[l], sp, saved[l],
                                  None if emit is None else functools.partial(emit, l))
    return loss, dh, grads, g_final


ANY = pl.BlockSpec(memory_space=pl.ANY)


def _me():
    return lax.axis_index("x"), lax.axis_index("y"), lax.axis_index("c")


def _other_chips(x, y):
    return [(1 - x, y), (x, 1 - y), (1 - x, 1 - y)]


def _half(ref, axis, c, rows):
    half = rows // 2
    idx = [slice(None)] * axis + [pl.ds(pl.multiple_of(c * half, 16), half)]
    return ref.at[tuple(idx)]


def _gather_steps(out, send, recv):
    n = len(out)
    x, y, c = _me()
    mine = 2 * x + y
    sibling = (x, y, 1 - c)
    chips = _other_chips(x, y)
    sends = []
    for t in range(n):
        rows = out[t].shape[1]
        for k, (px, py) in enumerate(chips):
            own_half = _half(out[t].at[mine], 0, c, rows)
            cp = pltpu.make_async_remote_copy(
                src_ref=own_half, dst_ref=own_half,
                send_sem=send.at[6 * t + k], recv_sem=recv.at[6 * t + k], device_id=(px, py, c), device_id_type=MESH)
            cp.start()
            sends.append(cp)
    for t in range(n):
        rows = out[t].shape[1]
        for k, (px, py) in enumerate(chips):
            landed = _half(out[t].at[2 * px + py], 0, c, rows)
            pltpu.make_async_remote_copy(
                src_ref=landed, dst_ref=landed, send_sem=send.at[6 * t + k], recv_sem=recv.at[6 * t + k],
                device_id=(px, py, c), device_id_type=MESH).wait_recv()
            cp = pltpu.make_async_remote_copy(
                src_ref=landed, dst_ref=landed, send_sem=send.at[6 * t + 3 + k], recv_sem=recv.at[6 * t + 3 + k],
                device_id=sibling, device_id_type=MESH)
            cp.start()
            sends.append(cp)
    for t in range(n):
        rows = out[t].shape[1]
        for k, (px, py) in enumerate(chips):
            other = _half(out[t].at[2 * px + py], 0, 1 - c, rows)
            pltpu.make_async_remote_copy(
                src_ref=other, dst_ref=other, send_sem=send.at[6 * t + 3 + k], recv_sem=recv.at[6 * t + 3 + k],
                device_id=sibling, device_id_type=MESH).wait_recv()
    for cp in sends:
        cp.wait_send()


def _gather_weights(bufs):
    n = len(bufs)

    def body(*refs):
        _gather_steps(refs[n:2 * n], *refs[2 * n:])

    return pl.pallas_call(
        body, name="gather_weights", in_specs=[ANY] * n, out_specs=[ANY] * n,
        out_shape=[_sds(s.shape, s.dtype) for s in bufs], input_output_aliases={t: t for t in range(n)},
        scratch_shapes=[pltpu.SemaphoreType.DMA((6 * n,)), pltpu.SemaphoreType.DMA((6 * n,))],
    )(*bufs)


def _gather_weights_async(bufs, tag, collective_id):
    n = len(bufs)
    refs = [jax.new_ref(b, memory_space=pltpu.MemorySpace.HBM) for b in bufs]

    @pl.kernel(mesh=plsc.ScalarSubcoreMesh(axis_name="sequencer", num_cores=1), name=f"gather_async_{tag}",
               scratch_types=(pltpu.SemaphoreType.DMA((6 * n,)), pltpu.SemaphoreType.DMA((6 * n,))),
               compiler_params=pltpu.CompilerParams(collective_id=collective_id))
    def launch(send, recv):
        x, y, c = _me()
        barrier = pltpu.get_barrier_semaphore()
        peers = [(x, y, 1 - c)] + [(px, py, c) for px, py in _other_chips(x, y)]
        for peer in peers:
            pl.semaphore_signal(barrier, inc=1, device_id=peer, device_id_type=MESH)
        pl.semaphore_wait(barrier, len(peers))
        _gather_steps(refs, send, recv)

    launch()
    return [r[...] for r in refs]


def _rs_pair(grads):
    n = len(grads)

    def body(*refs):
        g, out = refs[:n], refs[n:2 * n]
        send, recv = refs[2 * n:]
        x, y, c = _me()
        copies = []
        for t in range(n):
            cp = pltpu.make_async_remote_copy(
                src_ref=_half(g[t], 1, 1 - c, g[t].shape[1]), dst_ref=out[t], send_sem=send.at[t], recv_sem=recv.at[t],
                device_id=(x, y, 1 - c), device_id_type=MESH)
            cp.start()
            copies.append(cp)
        for cp in copies:
            cp.wait()

    return pl.pallas_call(
        body, name="rs_pair", in_specs=[ANY] * n, out_specs=[ANY] * n,
        out_shape=[_sds((N_CHIPS, s.shape[1] // 2, s.shape[2]), s.dtype) for s in grads],
        scratch_shapes=[pltpu.SemaphoreType.DMA((n,)), pltpu.SemaphoreType.DMA((n,))],
    )(*grads)


def _rs_chips_steps(p, out, send, recv):
    x, y, c = _me()
    copies = []
    for t in range(len(p)):
        for k, (px, py) in enumerate(_other_chips(x, y)):
            cp = pltpu.make_async_remote_copy(
                src_ref=p[t].at[2 * px + py], dst_ref=out[t].at[k], send_sem=send.at[3 * t + k],
                recv_sem=recv.at[3 * t + k], device_id=(px, py, c), device_id_type=MESH)
            cp.start()
            copies.append(cp)
    for cp in copies:
        cp.wait()


def _rs_chips_async(parts, tag, collective_id):
    n = len(parts)
    src = [jax.new_ref(p, memory_space=pltpu.MemorySpace.HBM) for p in parts]
    got = [jax.empty_ref(_sds((3, *p.shape[1:]), p.dtype), memory_space=pltpu.MemorySpace.HBM) for p in parts]

    @pl.kernel(mesh=plsc.ScalarSubcoreMesh(axis_name="sequencer", num_cores=1), name=f"rs_chips_async_{tag}",
               scratch_types=(pltpu.SemaphoreType.DMA((3 * n,)), pltpu.SemaphoreType.DMA((3 * n,))),
               compiler_params=pltpu.CompilerParams(collective_id=collective_id))
    def launch(send, recv):
        x, y, c = _me()
        barrier = pltpu.get_barrier_semaphore()
        peers = [(px, py, c) for px, py in _other_chips(x, y)]
        for peer in peers:
            pl.semaphore_signal(barrier, inc=1, device_id=peer, device_id_type=MESH)
        pl.semaphore_wait(barrier, len(peers))
        _rs_chips_steps(src, got, send, recv)

    launch()
    return [g[...] for g in got]


def _rs_chips(parts):
    n = len(parts)

    def body(*refs):
        _rs_chips_steps(refs[:n], refs[n:2 * n], *refs[2 * n:])

    return pl.pallas_call(
        body, name="rs_chips", in_specs=[ANY] * n, out_specs=[ANY] * n,
        out_shape=[_sds((3, *s.shape[1:]), s.dtype) for s in parts],
        scratch_shapes=[pltpu.SemaphoreType.DMA((3 * n,)), pltpu.SemaphoreType.DMA((3 * n,))],
    )(*parts)


def _pair_exchange(bufs):
    n = len(bufs)

    def body(*refs):
        out = refs[n:2 * n]
        send, recv = refs[2 * n:]
        x, y, c = _me()
        copies = []
        for t in range(n):
            cp = pltpu.make_async_remote_copy(
                src_ref=out[t].at[c], dst_ref=out[t].at[c], send_sem=send.at[t], recv_sem=recv.at[t],
                device_id=(x, y, 1 - c), device_id_type=MESH)
            cp.start()
            copies.append(cp)
        for t, cp in enumerate(copies):
            cp.wait_send()
            pltpu.make_async_remote_copy(
                src_ref=out[t].at[1 - c], dst_ref=out[t].at[1 - c], send_sem=send.at[t], recv_sem=recv.at[t],
                device_id=(x, y, 1 - c), device_id_type=MESH).wait_recv()

    return pl.pallas_call(
        body, name="pair_exchange", in_specs=[ANY] * n, out_specs=[ANY] * n,
        out_shape=[_sds(s.shape, s.dtype) for s in bufs], input_output_aliases={t: t for t in range(n)},
        scratch_shapes=[pltpu.SemaphoreType.DMA((n,)), pltpu.SemaphoreType.DMA((n,))],
    )(*bufs)


def _row_tile(rows, cols, itemsize, budget=2 * 1024 * 1024):
    tr = rows
    while tr * cols * itemsize > budget and tr % 32 == 0:
        tr //= 2
    return tr


def _sum_pair(name, g, got, where):
    nchip, rows, cols = g.shape
    half = rows // 2
    tr = _row_tile(half, cols, 4)
    per = half // tr

    def body(w_ref, g_ref, r_ref, o_ref):
        o_ref[...] = (g_ref[...].astype(F32) + r_ref[...].astype(F32)).astype(o_ref.dtype)

    blk = pl.BlockSpec((None, tr, cols), lambda j, i, w_ref: (j, i, 0))
    return pl.pallas_call(
        body, name=name,
        grid_spec=pltpu.PrefetchScalarGridSpec(
            num_scalar_prefetch=1, grid=(nchip, per),
            in_specs=[pl.BlockSpec((None, tr, cols), lambda j, i, w_ref: (j, w_ref[1] * per + i, 0)), blk], out_specs=blk),
        out_shape=_sds((nchip, half, cols), BF16), compiler_params=_cp("parallel", "parallel"),
    )(where, g, got)


def _sum_chips(name, p, got, where):
    _, rows, cols = p.shape
    tr = _row_tile(rows, cols, 4)

    def body(w_ref, p_ref, r0, r1, r2, o_ref):
        o_ref[...] = ((p_ref[...].astype(F32) + r0[...].astype(F32)) + r1[...].astype(F32)) + r2[...].astype(F32)

    def got_k(k):
        return pl.BlockSpec((None, tr, cols), lambda i, w_ref: (k, i, 0))

    return pl.pallas_call(
        body, name=name,
        grid_spec=pltpu.PrefetchScalarGridSpec(
            num_scalar_prefetch=1, grid=(rows // tr,),
            in_specs=[pl.BlockSpec((None, tr, cols), lambda i, w_ref: (w_ref[0], i, 0)), got_k(0), got_k(1), got_k(2)],
            out_specs=pl.BlockSpec((None, tr, cols), lambda i, w_ref: (w_ref[1], i, 0))),
        out_shape=_sds((2, rows, cols), F32), compiler_params=_cp("parallel"),
    )(where, p, got, got, got)


def _reduce_scatter(grads, where):
    return _rs_finish(*_rs_begin(grads, where), where)


def _rs_begin(grads, where, tag=None, collective_id=None):
    got = _rs_pair(grads)
    parts = [_sum_pair(f"sum_pair_{t}", g, r, where) for t, (g, r) in enumerate(zip(grads, got))]
    return parts, (_rs_chips(parts) if tag is None else _rs_chips_async(parts, tag, collective_id))


def _rs_finish(parts, got, where):
    halves = [_sum_chips(f"sum_chips_{t}", p, r, where) for t, (p, r) in enumerate(zip(parts, got))]
    return _pair_exchange(halves)


def _all_reduce_small(name, v):
    rows = v.shape[0]

    def body(v_ref, o_ref, land, send, recv):
        x, y, c = _me()
        mine = 4 * x + 2 * y + c
        copies = []
        for k in range(1, 8):
            kx, ky, kc = k >> 2, (k >> 1) & 1, k & 1
            peer = (x ^ kx, y ^ ky, c ^ kc)
            cp = pltpu.make_async_remote_copy(
                src_ref=v_ref, dst_ref=land.at[mine], send_sem=send.at[k - 1], recv_sem=recv.at[k - 1],
                device_id=peer, device_id_type=MESH)
            cp.start()
            copies.append(cp)
        land[mine] = v_ref[...]
        for k in range(1, 8):
            kx, ky, kc = k >> 2, (k >> 1) & 1, k & 1
            src = 4 * (x ^ kx) + 2 * (y ^ ky) + (c ^ kc)
            pltpu.make_async_remote_copy(
                src_ref=v_ref, dst_ref=land.at[src], send_sem=send.at[k - 1], recv_sem=recv.at[k - 1],
                device_id=(x ^ kx, y ^ ky, c ^ kc), device_id_type=MESH).wait_recv()
        acc = land[0]
        for d in range(1, 8):
            acc = acc + land[d]
        o_ref[...] = acc
        for cp in copies:
            cp.wait_send()

    vm = pl.BlockSpec(memory_space=pltpu.VMEM)
    return pl.pallas_call(
        body, name=name, in_specs=[vm], out_specs=vm, out_shape=_sds((rows, 128), F32),
        scratch_shapes=[pltpu.VMEM((8, rows, 128), F32), pltpu.SemaphoreType.DMA((7,)), pltpu.SemaphoreType.DMA((7,))],
    )(v)


def _adamw(name, w, g, m, v):
    rows, cols = w.shape
    tr = _row_tile(rows, cols, 4, budget=1024 * 1024)
    c1 = 1.0 / (1.0 - ADAM_B1 ** ADAM_STEP)
    c2 = 1.0 / (1.0 - ADAM_B2 ** ADAM_STEP)

    def body(w_ref, g_ref, m_ref, v_ref, d_ref, nm_ref, nv_ref):
        gv = g_ref[...]
        nm = ADAM_B1 * m_ref[...] + (1.0 - ADAM_B1) * gv
        nv = ADAM_B2 * v_ref[...] + (1.0 - ADAM_B2) * (gv * gv)
        d_ref[...] = -ADAM_LR * ((nm * c1) / (jnp.sqrt(nv * c2) + ADAM_EPS) + ADAM_WD * w_ref[...])
        nm_ref[...] = nm
        nv_ref[...] = nv

    blk = pl.BlockSpec((tr, cols), lambda i: (i, 0))
    return pl.pallas_call(
        body, name=name, grid=(rows // tr,), in_specs=[blk] * 4, out_specs=[blk] * 3,
        out_shape=[_sds((rows, cols), F32)] * 3, compiler_params=_cp("parallel"),
    )(w, g, m, v)


def _adamw_layers(name, w, g0, g1, m, v):
    _, half, cols = g0.shape
    tr = _row_tile(half, cols, 4, budget=1024 * 1024)
    per_half = half // tr
    per = 2 * per_half
    c1 = 1.0 / (1.0 - ADAM_B1 ** ADAM_STEP)
    c2 = 1.0 / (1.0 - ADAM_B2 ** ADAM_STEP)

    def body(w_ref, g0_ref, g1_ref, m_ref, v_ref, g_ref, d_ref, nm_ref, nv_ref):
        gv = jnp.where(pl.program_id(0) == 0, g0_ref[...], g1_ref[...])
        nm = ADAM_B1 * m_ref[...] + (1.0 - ADAM_B1) * gv
        nv = ADAM_B2 * v_ref[...] + (1.0 - ADAM_B2) * (gv * gv)
        g_ref[...] = gv
        d_ref[...] = -ADAM_LR * ((nm * c1) / (jnp.sqrt(nv * c2) + ADAM_EPS) + ADAM_WD * w_ref[...])
        nm_ref[...] = nm
        nv_ref[...] = nv

    both = pl.BlockSpec((None, tr, cols), lambda l, i: (l, i, 0))

    def halves(i):
        return i // per_half, i % per_half, 0

    first = pl.BlockSpec((None, tr, cols), lambda l, i: halves(i * (1 - l) + (per - 1) * l))
    second = pl.BlockSpec((None, tr, cols), lambda l, i: halves(i * l))
    return pl.pallas_call(
        body, name=name, grid=(2, per), in_specs=[both, first, second, both, both], out_specs=[both] * 4,
        out_shape=[_sds(w.shape, F32)] * 4, compiler_params=_cp("arbitrary", "arbitrary"),
    )(w, g0, g1, m, v)


def _to_bf16_slot(name, w, l, where):
    _, rows, cols = w.shape
    tr = _row_tile(rows, cols, 4)

    def body(w_ref, x_ref, o_ref):
        o_ref[...] = x_ref[...].astype(BF16)

    return pl.pallas_call(
        body, name=name,
        grid_spec=pltpu.PrefetchScalarGridSpec(
            num_scalar_prefetch=1, grid=(rows // tr,), in_specs=[pl.BlockSpec((None, tr, cols), lambda i, w_ref: (l, i, 0))],
            out_specs=pl.BlockSpec((None, tr, cols), lambda i, w_ref: (w_ref[0], i, 0))),
        out_shape=_sds((N_CHIPS, rows, cols), BF16), compiler_params=_cp("parallel"))(where, w)


BIG = ("w_in", "w_pool_up", "w_sb_up", "w_gdn_up", "w_out", "w_ff1", "w_ff2")
SMALL = (("attn_norm", (D,)), ("pool_w", (4, 128, 128)), ("pool_scale", (POOL_W,)), ("gdn_a_log", (HEADS,)),
         ("gdn_dt_bias", (HEADS,)), ("gdn_norm", (HD,)), ("mlp_norm", (D,)))


PACK_TILE = 8 * 128


def _rows128(a):
    flat = a.reshape(-1)
    pad = (-flat.shape[0]) % PACK_TILE
    return jnp.pad(flat, (0, pad)).reshape(-1, 128)


def _pack(parts):
    packed = jnp.concatenate([_rows128(p) for p in parts], axis=0)
    return jnp.pad(packed, ((0, (-packed.shape[0]) % 8), (0, 0)))


def _unpack(packed, shapes):
    out, r = [], 0
    for shp in shapes:
        size = 1
        for s in shp:
            size *= s
        nr = -(-size // PACK_TILE) * 8
        out.append(packed[r:r + nr].reshape(-1)[:size].reshape(shp))
        r += nr
    return out


def kernel(x, attn_norm, w_in, pool_w, pool_scale, gdn_conv, gdn_a_log, gdn_dt_bias, gdn_norm, w_pool_up, w_sb_up, w_gdn_up, w_out, mlp_norm, w_ff1, w_ff2, final_norm, loss_target, m_attn_norm, m_w_in, m_pool_w, m_pool_scale, m_gdn_conv, m_gdn_a_log, m_gdn_dt_bias, m_gdn_norm, m_w_pool_up, m_w_sb_up, m_w_gdn_up, m_w_out, m_mlp_norm, m_w_ff1, m_w_ff2, m_final_norm, v_attn_norm, v_w_in, v_pool_w, v_pool_scale, v_gdn_conv, v_gdn_a_log, v_gdn_dt_bias, v_gdn_norm, v_w_pool_up, v_w_sb_up, v_w_gdn_up, v_w_out, v_mlp_norm, v_w_ff1, v_w_ff2, v_final_norm):
    weights = dict(attn_norm=attn_norm, w_in=w_in, pool_w=pool_w, pool_scale=pool_scale, gdn_conv=gdn_conv,
                   gdn_a_log=gdn_a_log, gdn_dt_bias=gdn_dt_bias, gdn_norm=gdn_norm, w_pool_up=w_pool_up, w_sb_up=w_sb_up,
                   w_gdn_up=w_gdn_up, w_out=w_out, mlp_norm=mlp_norm, w_ff1=w_ff1, w_ff2=w_ff2, final_norm=final_norm)
    mom1 = dict(attn_norm=m_attn_norm, w_in=m_w_in, pool_w=m_pool_w, pool_scale=m_pool_scale, gdn_conv=m_gdn_conv,
                gdn_a_log=m_gdn_a_log, gdn_dt_bias=m_gdn_dt_bias, gdn_norm=m_gdn_norm, w_pool_up=m_w_pool_up,
                w_sb_up=m_w_sb_up, w_gdn_up=m_w_gdn_up, w_out=m_w_out, mlp_norm=m_mlp_norm, w_ff1=m_w_ff1, w_ff2=m_w_ff2,
                final_norm=m_final_norm)
    mom2 = dict(attn_norm=v_attn_norm, w_in=v_w_in, pool_w=v_pool_w, pool_scale=v_pool_scale, gdn_conv=v_gdn_conv,
                gdn_a_log=v_gdn_a_log, gdn_dt_bias=v_gdn_dt_bias, gdn_norm=v_gdn_norm, w_pool_up=v_w_pool_up,
                w_sb_up=v_w_sb_up, w_gdn_up=v_w_gdn_up, w_out=v_w_out, mlp_norm=v_mlp_norm, w_ff1=v_w_ff1, w_ff2=v_w_ff2,
                final_norm=v_final_norm)
    xi, yi, ci = lax.axis_index("x"), lax.axis_index("y"), lax.axis_index("c")
    chip = 2 * xi + yi
    where = jnp.stack([chip, ci]).astype(jnp.int32)

    bufs = [[_to_bf16_slot(f"cast_{nm}_{l}", weights[nm], l, where) for nm in BIG] for l in range(2)]
    first = _gather_weights(bufs[0][:1])
    rest, _ = lax.optimization_barrier((bufs[0][1:], first))
    gw = [dict(zip(BIG, list(first) + _gather_weights_async(rest, "0_rest", 1))), {}]

    def reached(l, stage, value):
        if l == 0 and stage == "proj":
            later, _ = lax.optimization_barrier((bufs[1][:1], value))
            gw[1]["w_in"] = _gather_weights_async(later, "1_w_in", 2)[0]
        if l == 0 and stage == "merged":
            later, _ = lax.optimization_barrier((bufs[1][1:], value))
            gw[1].update(zip(BIG[1:], _gather_weights_async(later, "1_rest", 3)))

    def weights_of(l):
        return gw[l], _w_in_from_shards(gw[l]["w_in"])

    conv_cols = gdn_conv.shape[-1]
    conv_place = lax.dynamic_update_slice(jnp.zeros((2, GDN_CONV, N_CHIPS * conv_cols), F32),
                                          jnp.where(ci == 0, gdn_conv, 0.0), (0, 0, chip * conv_cols))
    conv_full = _all_reduce_small("gather_conv", _rows128(conv_place)).reshape(2, GDN_CONV, N_CHIPS * conv_cols)
    sp = dict(attn_norm=attn_norm.reshape(2, 1, D), pool_w=pool_w, pool_scale=pool_scale.reshape(2, 1, POOL_W),
              conv=conv_full, a_log=jnp.stack([_row128(gdn_a_log[l]) for l in range(2)]),
              dt_bias=jnp.stack([_row128(gdn_dt_bias[l]) for l in range(2)]), gdn_norm=gdn_norm.reshape(2, 1, HD),
              mlp_norm=mlp_norm.reshape(2, 1, D), final_norm=final_norm.reshape(1, D))

    started = []

    def emit(l, names, g, v):
        last = l == 0 and names == ("w_in",)
        tag = None if last else f"{l}_{names[0]}"
        parts, got = _rs_begin([g[nm] for nm in names], where, tag, 4 + len(started))
        started.append((l, names, parts, got, last))
        if not last:
            v, _ = lax.optimization_barrier((v, parts))
        return v

    loss, grad_x, grads, g_final = _local_step(x[0], loss_target[0], weights_of, sp, emit, reached)
    loss = lax.psum(loss[0, 0], ("x", "y", "c"))
    big_grads = {nm: [None, None] for nm in BIG}
    for l, names, parts, got, last in started:
        if not last:
            got, _ = lax.optimization_barrier((got, grad_x))
        for nm, red in zip(names, _rs_finish(parts, got, where)):
            big_grads[nm][l] = red
    small_parts, small_shapes = [], []
    for l in range(2):
        g = grads[l]
        for nm, shp in SMALL:
            key = {"gdn_a_log": "a_log", "gdn_dt_bias": "dt_bias"}.get(nm, nm)
            val = g[key]
            small_parts.append(val[0, :HEADS] if nm in ("gdn_a_log", "gdn_dt_bias") else val)
            small_shapes.append(shp)
        small_parts.append(g["conv"])
        small_shapes.append((GDN_CONV, N_CHIPS * conv_cols))
    small_parts.append(g_final)
    small_shapes.append((D,))
    reduced = _unpack(_all_reduce_small("reduce_small", _pack(small_parts)), small_shapes)
    per = len(SMALL) + 1
    grad = {}
    for i, (nm, _) in enumerate(SMALL):
        grad[nm] = jnp.stack([reduced[i], reduced[per + i]])
    conv_g = jnp.stack([reduced[per - 1], reduced[2 * per - 1]])
    grad["gdn_conv"] = lax.dynamic_slice(conv_g, (0, 0, chip * conv_cols), (2, GDN_CONV, conv_cols))
    grad["final_norm"] = reduced[-1]

    delta, new_m, new_v = {}, {}, {}
    for nm in BIG:
        grad[nm], delta[nm], new_m[nm], new_v[nm] = _adamw_layers("adamw_" + nm, weights[nm], *big_grads[nm], mom1[nm], mom2[nm])
    small_names = [nm for nm, _ in SMALL] + ["gdn_conv", "final_norm"]
    packs = [_pack([src[nm] for nm in small_names]) for src in (weights, grad, mom1, mom2)]
    outs = _adamw("adamw_small", *packs)
    shapes = [weights[nm].shape for nm in small_names]
    for dst, packed in zip((delta, new_m, new_v), outs):
        for nm, val in zip(small_names, _unpack(packed, shapes)):
            dst[nm] = val

    order = ("attn_norm", "w_in", "pool_w", "pool_scale", "gdn_conv", "gdn_a_log", "gdn_dt_bias", "gdn_norm", "w_pool_up",
             "w_sb_up", "w_gdn_up", "w_out", "mlp_norm", "w_ff1", "w_ff2", "final_norm")
    return (loss, grad_x[None], *[grad[n] for n in order], *[delta[n] for n in order], *[new_m[n] for n in order],
            *[new_v[n] for n in order])
```

```python
import functools

import jax
import jax.numpy as jnp
from jax import lax
from jax.experimental import pallas as pl
from jax.experimental.pallas import tpu as pltpu
from jax.experimental.pallas import tpu_sc as plsc

F32, BF16 = jnp.float32, jnp.bfloat16
HIGH = lax.Precision.HIGH
MESH = pl.DeviceIdType.MESH

D = 2048
EPS = 1e-6
POOL_WINDOWS = (2, 4, 8, 16)
POOL_W, SB_W, GDN_W = 512, 768, 768
HEADS, HD = 6, 128
SB_BLOCK = 128
GDN_CHUNK = 64
D_FF = 4 * D
N_IN = 12044
N_CHIPS = 4
OFF_SB, OFF_GQKV, OFF_Z, OFF_AB, OFF_P, OFF_GATE = 0, 2304, 4608, 5376, 5632, 6144
AB_W = 256
ORIG_SB, ORIG_AB, ORIG_GATE = 512, 5888, 5900
N_AL = 12288
VMEM_LIMIT = 48 * 1024 * 1024

ADAM_LR, ADAM_B1, ADAM_B2, ADAM_EPS, ADAM_WD, ADAM_STEP = 0.001, 0.9, 0.999, 1e-08, 0.01, 10

NT = (((1,), (1,)), ((), ()))
TN = (((0,), (0,)), ((), ()))


def _cp(*sem):
    return pltpu.CompilerParams(dimension_semantics=sem, vmem_limit_bytes=VMEM_LIMIT)


def _dot(a, b, dims=None, precision=None):
    if dims is None:
        dims = (((a.ndim - 1,), (0,)), ((), ()))
    return lax.dot_general(a, b, dims, precision=precision, preferred_element_type=F32)


def _hdot(a, b, dims=None):
    return _dot(a, b, dims, precision=HIGH)


def _bdot(a, b, dims=None):
    return _dot(a.astype(BF16), b.astype(BF16), dims)


def _mm(name, a, b, *, m, n, k, tm, tn, tk, a_spec, b_spec, dims, out_shapes, out_specs,
        extras=(), extra_specs=(), epilogue=None):
    nk = k // tk
    ne, no = len(extras), len(out_shapes)

    def body(*refs):
        a_ref, b_ref = refs[0], refs[1]
        ex = refs[2:2 + ne]
        outs = refs[2 + ne:2 + ne + no]
        kk = pl.program_id(2)

        def finish(r):
            res = epilogue(r, *[e[...] for e in ex]) if epilogue is not None else (r,)
            for o, v in zip(outs, res):
                o[...] = v.astype(o.dtype)

        part = _dot(a_ref[...].astype(BF16), b_ref[...].astype(BF16), dims)
        if nk == 1:
            finish(part)
            return
        acc = refs[-1]

        @pl.when(kk == 0)
        def _():
            acc[...] = part

        @pl.when((kk > 0) & (kk < nk - 1))
        def _():
            acc[...] += part

        @pl.when(kk == nk - 1)
        def _():
            finish(acc[...] + part)

    return pl.pallas_call(
        body, name=name, grid=(m // tm, n // tn, nk),
        in_specs=[a_spec, b_spec, *extra_specs], out_specs=out_specs, out_shape=out_shapes,
        scratch_shapes=[] if nk == 1 else [pltpu.VMEM((tm, tn), F32)],
        compiler_params=_cp("parallel", "parallel", "arbitrary"),
    )(a, b, *extras)


def _a_plain(tm, tk):
    return pl.BlockSpec((tm, tk), lambda i, j, kk: (i, kk))


def _a_trans(tm, tk):
    return pl.BlockSpec((tk, tm), lambda i, j, kk: (kk, i))


def _b_plain(tk, tn):
    return pl.BlockSpec((tk, tn), lambda i, j, kk: (kk, j))


def _b_trans(tk, tn):
    return pl.BlockSpec((tn, tk), lambda i, j, kk: (j, kk))


def _o_plain(tm, tn):
    return pl.BlockSpec((tm, tn), lambda i, j, kk: (i, j))


def _o_colshard(tm, tn, ns_cols):
    per = ns_cols // tn
    return pl.BlockSpec((None, tm, tn), lambda i, j, kk: (j // per, i, j % per))


def _w_cols(tk, tn, ns):
    per = ns // tn
    return pl.BlockSpec((None, tk, tn), lambda i, j, kk: (j // per, kk, j % per))


def _w_cols_t(tk, tn, ns):
    per = ns // tk
    return pl.BlockSpec((None, tn, tk), lambda i, j, kk: (kk // per, j, kk % per))


def _w_rows(tk, tn, ks):
    per = ks // tk
    return pl.BlockSpec((None, tk, tn), lambda i, j, kk: (kk // per, kk % per, j))


def _w_rows_t(tk, tn, ks):
    per = ks // tn
    return pl.BlockSpec((None, tn, tk), lambda i, j, kk: (j // per, j % per, kk))


def _sds(shape, dtype):
    return jax.ShapeDtypeStruct(shape, dtype)


def _rms_fwd(name, x, gain):
    t = x.shape[0]
    tt = min(256, t)

    def body(x_ref, g_ref, u_ref):
        xv = x_ref[...]
        r = lax.rsqrt(jnp.mean(xv * xv, axis=-1, keepdims=True) + EPS)
        u_ref[...] = (xv * r * g_ref[...]).astype(u_ref.dtype)

    return pl.pallas_call(
        body, name=name, grid=(t // tt,),
        in_specs=[pl.BlockSpec((tt, D), lambda i: (i, 0)), pl.BlockSpec((1, D), lambda i: (0, 0))],
        out_specs=pl.BlockSpec((tt, D), lambda i: (i, 0)), out_shape=_sds((t, D), BF16),
        compiler_params=_cp("parallel"),
    )(x, gain)


def _rms_bwd(name, du, x, gain, dres):
    t = x.shape[0]
    tt = min(256, t)

    def body(du_ref, x_ref, g_ref, dres_ref, dx_ref, dg_ref):
        @pl.when(pl.program_id(0) == 0)
        def _():
            dg_ref[...] = jnp.zeros_like(dg_ref)

        xv, duv = x_ref[...], du_ref[...]
        r = lax.rsqrt(jnp.mean(xv * xv, axis=-1, keepdims=True) + EPS)
        nx = xv * r
        dn = duv * g_ref[...]
        dg_ref[...] += jnp.sum(duv * nx, axis=0, keepdims=True)
        dx_ref[...] = dres_ref[...] + r * (dn - nx * jnp.mean(dn * nx, axis=-1, keepdims=True))

    row = pl.BlockSpec((tt, D), lambda i: (i, 0))
    vec = pl.BlockSpec((1, D), lambda i: (0, 0))
    return pl.pallas_call(
        body, name=name, grid=(t // tt,), in_specs=[row, row, vec, row], out_specs=[row, vec],
        out_shape=[_sds((t, D), F32), _sds((1, D), F32)], compiler_params=_cp("arbitrary"),
    )(du, x, gain, dres)


def _loss_head(x, gain, target):
    t = x.shape[0]
    tt = min(256, t)

    def body(x_ref, g_ref, t_ref, loss_ref, dx_ref, dg_ref):
        @pl.when(pl.program_id(0) == 0)
        def _():
            dg_ref[...] = jnp.zeros_like(dg_ref)
            loss_ref[...] = jnp.zeros_like(loss_ref)

        xv = x_ref[...]
        r = lax.rsqrt(jnp.mean(xv * xv, axis=-1, keepdims=True) + EPS)
        nx = xv * r
        err = nx * g_ref[...] - t_ref[...]
        loss_ref[...] += 0.5 * jnp.sum(jnp.mean(err * err, axis=-1, keepdims=True), axis=0, keepdims=True)
        dy = err * (1.0 / D)
        dn = dy * g_ref[...]
        dg_ref[...] += jnp.sum(dy * nx, axis=0, keepdims=True)
        dx_ref[...] = r * (dn - nx * jnp.mean(dn * nx, axis=-1, keepdims=True))

    row = pl.BlockSpec((tt, D), lambda i: (i, 0))
    vec = pl.BlockSpec((1, D), lambda i: (0, 0))
    one = pl.BlockSpec((1, 1), lambda i: (0, 0))
    return pl.pallas_call(
        body, name="loss_head", grid=(t // tt,), in_specs=[row, vec, row], out_specs=[one, row, vec],
        out_shape=[_sds((1, 1), F32), _sds((t, D), F32), _sds((1, D), F32)], compiler_params=_cp("arbitrary"),
    )(x, gain, target)


def _shift_down(v, s, t_idx):
    return jnp.where(t_idx >= s, pltpu.roll(v, s, 0), 0.0)


def _shift_up(v, s, t_idx, t):
    return jnp.where(t_idx < t - s, pltpu.roll(v, t - s, 0), 0.0)


def _pool_d(p, g, t_idx):
    s = p
    for step in range(g + 1):
        s = s + _shift_down(s, 1 << step, t_idx)
    cnt = jnp.minimum(t_idx + 1, POOL_WINDOWS[g]).astype(F32)
    return s / cnt - p, cnt


def _pool_fwd(proj, pool_w, pool_scale):
    t = proj.shape[0]
    g128 = POOL_W // len(POOL_WINDOWS)

    def body(p_ref, w_ref, s_ref, y_ref):
        t_idx = lax.broadcasted_iota(jnp.int32, (t, g128), 0)
        for g in range(len(POOL_WINDOWS)):
            sl = slice(g * g128, (g + 1) * g128)
            d, _ = _pool_d(p_ref[:, sl], g, t_idx)
            y_ref[:, sl] = (_bdot(d, w_ref[g]) * s_ref[:, sl]).astype(y_ref.dtype)

    return pl.pallas_call(
        body, name="pool_fwd", grid=(1,),
        in_specs=[pl.BlockSpec((t, POOL_W), lambda i: (0, OFF_P // POOL_W)),
                  pl.BlockSpec((4, g128, g128), lambda i: (0, 0, 0)), pl.BlockSpec((1, POOL_W), lambda i: (0, 0))],
        out_specs=pl.BlockSpec((t, POOL_W), lambda i: (0, 0)), out_shape=_sds((t, POOL_W), BF16),
        compiler_params=_cp("arbitrary"),
    )(proj, pool_w, pool_scale)


def _pool_bwd(proj, pool_w, pool_scale, dy):
    t = proj.shape[0]
    g128 = POOL_W // len(POOL_WINDOWS)

    def body(p_ref, w_ref, s_ref, dy_ref, dp_ref, dw_ref, ds_ref):
        t_idx = lax.broadcasted_iota(jnp.int32, (t, g128), 0)
        for g in range(len(POOL_WINDOWS)):
            sl = slice(g * g128, (g + 1) * g128)
            d, cnt = _pool_d(p_ref[:, sl], g, t_idx)
            dyv = dy_ref[:, sl].astype(F32)
            ds_ref[:, sl] = jnp.sum(dyv * _bdot(d, w_ref[g]), axis=0, keepdims=True)
            dys = dyv * s_ref[:, sl]
            dw_ref[g] = _bdot(d, dys, TN)
            dd = _bdot(dys, w_ref[g], NT)
            s = dd / cnt
            for step in range(g + 1):
                s = s + _shift_up(s, 1 << step, t_idx, t)
            dp_ref[:, sl] = (s - dd).astype(dp_ref.dtype)

    return pl.pallas_call(
        body, name="pool_bwd", grid=(1,),
        in_specs=[pl.BlockSpec((t, POOL_W), lambda i: (0, OFF_P // POOL_W)),
                  pl.BlockSpec((4, g128, g128), lambda i: (0, 0, 0)), pl.BlockSpec((1, POOL_W), lambda i: (0, 0)),
                  pl.BlockSpec((t, POOL_W), lambda i: (0, 0))],
        out_specs=[pl.BlockSpec((t, POOL_W), lambda i: (0, 0)), pl.BlockSpec((4, g128, g128), lambda i: (0, 0, 0)),
                   pl.BlockSpec((1, POOL_W), lambda i: (0, 0))],
        out_shape=[_sds((t, POOL_W), BF16), _sds((4, g128, g128), F32), _sds((1, POOL_W), F32)],
        compiler_params=_cp("arbitrary"),
    )(proj, pool_w, pool_scale, dy)


SB_GROUP = 3
SB_GW = SB_GROUP * HD


def _sb_cast_kv(proj):
    t = proj.shape[0]
    tt = min(512, t)

    def body(x_ref, o_ref):
        o_ref[...] = x_ref[...].astype(BF16)

    return pl.pallas_call(
        body, name="sb_cast_kv", grid=(t // tt, 2),
        in_specs=[pl.BlockSpec((tt, SB_W), lambda i, j: (i, OFF_SB // SB_W + 1 + j))],
        out_specs=pl.BlockSpec((tt, SB_W), lambda i, j: (i, j)), out_shape=_sds((t, 2 * SB_W), BF16),
        compiler_params=_cp("parallel", "parallel"),
    )(proj)


def _sb_specs(t):
    q_spec = pl.BlockSpec((SB_BLOCK, SB_GW), lambda g, i: (i, OFF_SB // SB_GW + g))
    k_spec = pl.BlockSpec((t, SB_GW), lambda g, i: (0, g))
    v_spec = pl.BlockSpec((t, SB_GW), lambda g, i: (0, SB_W // SB_GW + g))
    return q_spec, k_spec, v_spec


def _head(ref, h, rows=None):
    cols = slice(h * HD, (h + 1) * HD)
    return ref[:, cols] if rows is None else ref[rows, cols]


SB_KEYS = 512


def _sub(v, b):
    return v[:, b * SB_BLOCK:(b + 1) * SB_BLOCK]


def _sb_keep(kc, limit):
    row = lax.broadcasted_iota(jnp.int32, (SB_BLOCK, kc), 0)
    col = lax.broadcasted_iota(jnp.int32, (SB_BLOCK, kc), 1)
    return col < row + limit


def _sb_chunk(q, keys, run, later, limit):
    kc = keys.shape[0]
    z = _dot(q, keys, NT)
    lsz = jax.nn.log_sigmoid(z)
    ls = lsz - z
    if limit is not None:
        keep = _sb_keep(kc, limit)
        ls = jnp.where(keep, ls, 0.0)
    parts = [None] * (kc // SB_BLOCK)
    for b in reversed(range(kc // SB_BLOCK)):
        parts[b] = _hdot(_sub(ls, b), later) + run
        run = run + jnp.sum(_sub(ls, b), axis=1, keepdims=True)
    a = jnp.exp(lsz + jnp.concatenate(parts, axis=1))
    if limit is not None:
        a = jnp.where(keep, a, 0.0)
    return z, a, run


def _sb_fwd(proj, kv):
    t = proj.shape[0]
    kc = min(SB_KEYS, t)
    scale = HD ** -0.5

    def body(q_ref, k_ref, v_ref, o_ref):
        i = pl.program_id(1)
        top = (i * SB_BLOCK) // kc
        qs = [(_head(q_ref, h) * scale).astype(BF16) for h in range(SB_GROUP)]
        row = lax.broadcasted_iota(jnp.int32, (SB_BLOCK, SB_BLOCK), 0)
        col = lax.broadcasted_iota(jnp.int32, (SB_BLOCK, SB_BLOCK), 1)
        later = (row > col).astype(F32)

        def chunk(jc, carry, masked):
            rows = pl.ds(pl.multiple_of(jc * kc, kc), kc)
            limit = i * SB_BLOCK - jc * kc if masked else None
            out = []
            for h in range(SB_GROUP):
                acc, run = carry[h]
                _, a, run = _sb_chunk(qs[h], _head(k_ref, h, rows), run, later, limit)
                out.append((acc + _dot(a.astype(BF16), _head(v_ref, h, rows)), run))
            return tuple(out)

        zero = tuple((jnp.zeros((SB_BLOCK, HD), F32), jnp.zeros((SB_BLOCK, 1), F32)) for _ in range(SB_GROUP))
        carry = chunk(top, zero, True)
        carry = lax.fori_loop(0, top, lambda jj, c: chunk(top - 1 - jj, c, False), carry)
        for h in range(SB_GROUP):
            o_ref[:, h * HD:(h + 1) * HD] = carry[h][0].astype(o_ref.dtype)

    return pl.pallas_call(
        body, name="sb_fwd", grid=(HEADS // SB_GROUP, t // SB_BLOCK), in_specs=list(_sb_specs(t)),
        out_specs=pl.BlockSpec((SB_BLOCK, SB_GW), lambda g, i: (i, g)), out_shape=_sds((t, SB_W), BF16),
        compiler_params=_cp("parallel", "arbitrary"),
    )(proj, kv, kv)


def _sb_bwd(proj, kv, dy):
    t = proj.shape[0]
    nq = t // SB_BLOCK
    kc = min(SB_KEYS, t)
    scale = HD ** -0.5

    def body(q_ref, k_ref, v_ref, do_ref, dq_ref, dk_ref, dv_ref, z_scr, e_scr):
        i = pl.program_id(1)
        top = (i * SB_BLOCK) // kc

        @pl.when(i == 0)
        def _():
            dk_ref[...] = jnp.zeros_like(dk_ref)
            dv_ref[...] = jnp.zeros_like(dv_ref)

        qs = [(_head(q_ref, h) * scale).astype(BF16) for h in range(SB_GROUP)]
        dos = [_head(do_ref, h).astype(BF16) for h in range(SB_GROUP)]
        row = lax.broadcasted_iota(jnp.int32, (SB_BLOCK, SB_BLOCK), 0)
        col = lax.broadcasted_iota(jnp.int32, (SB_BLOCK, SB_BLOCK), 1)
        later = (row > col).astype(F32)
        earlier = (row < col).astype(F32)

        def down(jc, runs, masked):
            rows = pl.ds(pl.multiple_of(jc * kc, kc), kc)
            limit = i * SB_BLOCK - jc * kc if masked else None
            out = []
            for h in range(SB_GROUP):
                z, a, run = _sb_chunk(qs[h], _head(k_ref, h, rows), runs[h], later, limit)
                z_scr[h, jc] = z
                e_scr[h, jc] = a * _dot(dos[h], _head(v_ref, h, rows), NT)
                dv_ref[rows, h * HD:(h + 1) * HD] += _dot(a.astype(BF16), dos[h], TN)
                out.append(run)
            return tuple(out)

        zero = tuple(jnp.zeros((SB_BLOCK, 1), F32) for _ in range(SB_GROUP))
        runs = down(top, zero, True)
        lax.fori_loop(0, top, lambda jj, r: down(top - 1 - jj, r, False), runs)

        def up(jc, carry, masked):
            rows = pl.ds(pl.multiple_of(jc * kc, kc), kc)
            out = []
            for h in range(SB_GROUP):
                dq, run = carry[h]
                z, e = z_scr[h, jc], e_scr[h, jc]
                parts = []
                for b in range(kc // SB_BLOCK):
                    parts.append(_hdot(_sub(e, b), earlier) + run)
                    run = run + jnp.sum(_sub(e, b), axis=1, keepdims=True)
                sz = jax.nn.sigmoid(z)
                dz = e * (1.0 - sz) - jnp.concatenate(parts, axis=1) * sz
                if masked:
                    dz = jnp.where(_sb_keep(kc, i * SB_BLOCK - jc * kc), dz, 0.0)
                dz = dz.astype(BF16)
                dk_ref[rows, h * HD:(h + 1) * HD] += _dot(dz, qs[h], TN)
                out.append((dq + _dot(dz, _head(k_ref, h, rows)), run))
            return tuple(out)

        zero = tuple((jnp.zeros((SB_BLOCK, HD), F32), jnp.zeros((SB_BLOCK, 1), F32)) for _ in range(SB_GROUP))
        carry = lax.fori_loop(0, top, lambda jc, c: up(jc, c, False), zero)
        carry = up(top, carry, True)
        for h in range(SB_GROUP):
            dq_ref[:, h * HD:(h + 1) * HD] = (carry[h][0] * scale).astype(dq_ref.dtype)

    blk = pl.BlockSpec((SB_BLOCK, SB_GW), lambda g, i: (i, g))
    seq = pl.BlockSpec((t, SB_GW), lambda g, i: (0, g))
    scratch = pltpu.VMEM((SB_GROUP, t // kc, SB_BLOCK, kc), F32)
    return pl.pallas_call(
        body, name="sb_bwd", grid=(HEADS // SB_GROUP, nq), in_specs=[*_sb_specs(t), blk], out_specs=[blk, seq, seq],
        out_shape=[_sds((t, SB_W), BF16), _sds((t, SB_W), F32), _sds((t, SB_W), F32)],
        scratch_shapes=[scratch, scratch], compiler_params=_cp("parallel", "arbitrary"),
    )(proj, kv, kv, dy)


CONV_TILE = 256
GDN_CONV = 4


def _conv_pre(x, w_ref, t_idx):
    pre = w_ref[GDN_CONV - 1:GDN_CONV, :] * x
    for s in range(1, GDN_CONV):
        pre = pre + w_ref[GDN_CONV - 1 - s:GDN_CONV - s, :] * _shift_down(x, s, t_idx)
    return pre


def _conv_fwd(proj, conv_w):
    t = proj.shape[0]
    width = conv_w.shape[1]

    def body(x_ref, w_ref, y_ref):
        t_idx = lax.broadcasted_iota(jnp.int32, (t, CONV_TILE), 0)
        pre = _conv_pre(x_ref[...], w_ref, t_idx)
        y_ref[...] = pre * jax.nn.sigmoid(pre)

    return pl.pallas_call(
        body, name="conv_fwd", grid=(width // CONV_TILE,),
        in_specs=[pl.BlockSpec((t, CONV_TILE), lambda c: (0, OFF_GQKV // CONV_TILE + c)),
                  pl.BlockSpec((GDN_CONV, CONV_TILE), lambda c: (0, c))],
        out_specs=pl.BlockSpec((t, CONV_TILE), lambda c: (0, c)), out_shape=_sds((t, width), F32),
        compiler_params=_cp("parallel"),
    )(proj, conv_w)


def _conv_bwd(proj, conv_w, dc):
    t = proj.shape[0]
    width = dc.shape[1]
    per = width // CONV_TILE
    part = 0

    def body(x_ref, w_ref, dc_ref, dx_ref, dw_ref):
        t_idx = lax.broadcasted_iota(jnp.int32, (t, CONV_TILE), 0)
        x = x_ref[...]
        pre = _conv_pre(x, w_ref, t_idx)
        sg = jax.nn.sigmoid(pre)
        dpre = dc_ref[...] * (sg * (1.0 + pre * (1.0 - sg)))
        dx = w_ref[GDN_CONV - 1:GDN_CONV, :] * dpre
        dw_ref[GDN_CONV - 1:GDN_CONV, :] = jnp.sum(dpre * x, axis=0, keepdims=True)
        for s in range(1, GDN_CONV):
            dx = dx + w_ref[GDN_CONV - 1 - s:GDN_CONV - s, :] * _shift_up(dpre, s, t_idx, t)
            dw_ref[GDN_CONV - 1 - s:GDN_CONV - s, :] = jnp.sum(dpre * _shift_down(x, s, t_idx), axis=0, keepdims=True)
        dx_ref[...] = dx.astype(dx_ref.dtype)

    return pl.pallas_call(
        body, name="conv_bwd", grid=(per,),
        in_specs=[pl.BlockSpec((t, CONV_TILE), lambda c: (0, OFF_GQKV // CONV_TILE + part * per + c)),
                  pl.BlockSpec((GDN_CONV, CONV_TILE), lambda c: (0, part * per + c)),
                  pl.BlockSpec((t, CONV_TILE), lambda c: (0, c))],
        out_specs=[pl.BlockSpec((t, CONV_TILE), lambda c: (0, c)), pl.BlockSpec((GDN_CONV, CONV_TILE), lambda c: (0, c))],
        out_shape=[_sds((t, width), BF16), _sds((GDN_CONV, width), F32)],
        compiler_params=_cp("parallel"),
    )(proj, conv_w, dc)


def _heads(x):
    return jnp.concatenate([x[:, h * HD:(h + 1) * HD][None] for h in range(HEADS)], axis=0)


def _hb(a, b, ca=2, cb=1):
    return lax.dot_general(a, b, (((ca,), (cb,)), ((0,), (0,))), precision=HIGH, preferred_element_type=F32)


@jax.custom_vjp
def _unit_lower_inverse(lower):
    c = lower.shape[-1]
    eye = lax.broadcasted_iota(jnp.int32, (c, c), 0) == lax.broadcasted_iota(jnp.int32, (c, c), 1)
    inv = jnp.where(eye, 1.0, 0.0) - lower
    pw = _hb(lower, lower)
    for step in range(5):
        inv = inv + _hb(inv, pw)
        if step < 4:
            pw = _hb(pw, pw)
    return inv


def _unit_lower_inverse_fwd(lower):
    inv = _unit_lower_inverse(lower)
    return inv, inv


def _unit_lower_inverse_bwd(inv, d_inv):
    return (-_hb(_hb(inv, d_inv, 1, 1), inv, 2, 2),)


_unit_lower_inverse.defvjp(_unit_lower_inverse_fwd, _unit_lower_inverse_bwd)


def _gdn_prep(cq, ck, cv, ab, alog_row, dtb_row):
    c = GDN_CHUNK
    row = lax.broadcasted_iota(jnp.int32, (c, c), 0)
    col = lax.broadcasted_iota(jnp.int32, (c, c), 1)
    incl, strict, eye = row >= col, row > col, row == col
    def lanes(v, first):
        return jnp.concatenate([v[:, first + h:first + h + 1][None] for h in range(HEADS)], axis=0)

    a_col, b_col = lanes(ab, 0), lanes(ab, HEADS)
    a_log, dt_bias = lanes(alog_row, 0), lanes(dtb_row, 0)
    qn = cq * lax.rsqrt(jnp.sum(cq * cq, axis=-1, keepdims=True) + EPS) * (HD ** -0.5)
    kn = ck * lax.rsqrt(jnp.sum(ck * ck, axis=-1, keepdims=True) + EPS)
    la_col = -jnp.exp(a_log) * jax.nn.softplus(a_col + dt_bias)
    beta = jax.nn.sigmoid(b_col)
    la_row = jnp.sum(jnp.where(eye, la_col, 0.0), axis=1, keepdims=True)
    g_col = jnp.sum(jnp.where(incl, la_row, 0.0), axis=2, keepdims=True)
    g_row = jnp.sum(jnp.where(row <= col, la_col, 0.0), axis=1, keepdims=True)
    g_last = jnp.sum(la_col, axis=1, keepdims=True)
    gamma = jnp.where(incl, jnp.exp(jnp.where(incl, g_col - g_row, 0.0)), 0.0)
    lower = jnp.where(strict, beta * _hb(kn, kn, 2, 2) * gamma, 0.0)
    inv = _unit_lower_inverse(lower)
    u = _hb(inv, cv * beta)
    w = _hb(inv, kn * (beta * jnp.exp(g_col)))
    qk = _hb(qn, kn, 2, 2) * gamma
    return u, w, qk, qn * jnp.exp(g_col), kn * jnp.exp(g_last - g_col), jnp.exp(g_last)


def _gdn_post(o, z, gain):
    y = o * lax.rsqrt(jnp.mean(o * o, axis=-1, keepdims=True) + EPS) * gain
    return y * (z * jax.nn.sigmoid(z))


def _gdn_specs(nc, reverse):
    c = GDN_CHUNK

    def ch(n):
        return nc - 1 - n if reverse else n

    def wide(array_off):
        return pl.BlockSpec((c, GDN_W), lambda n: (ch(n), array_off // GDN_W))

    ab = pl.BlockSpec((c, HD), lambda n: (ch(n), OFF_AB // HD))
    row = pl.BlockSpec((1, HD), lambda n: (0, 0))
    state = pl.BlockSpec((None, HEADS, HD, HD), lambda n: (ch(n), 0, 0, 0))
    return wide, ab, row, state


def _gdn_fwd(cqkv, proj, a_log, dt_bias, gain):
    t = proj.shape[0]
    nc = t // GDN_CHUNK
    wide, ab, row, state = _gdn_specs(nc, False)

    def body(cq_ref, ck_ref, cv_ref, ab_ref, z_ref, al_ref, dt_ref, g_ref, y_ref, sprev_ref, s_scr):
        @pl.when(pl.program_id(0) == 0)
        def _():
            s_scr[...] = jnp.zeros_like(s_scr)

        u, w, qk, qd, kd, dec = _gdn_prep(_heads(cq_ref[...]), _heads(ck_ref[...]), _heads(cv_ref[...]), ab_ref[...],
                                          al_ref[...], dt_ref[...])
        s = s_scr[...]
        sprev_ref[...] = s
        v_new = u - _hb(w, s)
        o = _hb(qd, s) + _hb(qk, v_new)
        s_scr[...] = s * dec + _hb(kd, v_new, 1, 1)
        y = _gdn_post(o, _heads(z_ref[...]), g_ref[...])
        for h in range(HEADS):
            y_ref[:, h * HD:(h + 1) * HD] = y[h].astype(y_ref.dtype)

    return pl.pallas_call(
        body, name="gdn_fwd", grid=(nc,),
        in_specs=[wide(0), wide(GDN_W), wide(2 * GDN_W), ab, wide(OFF_Z), row, row, row],
        out_specs=[wide(0), state], out_shape=[_sds((t, GDN_W), BF16), _sds((nc, HEADS, HD, HD), F32)],
        scratch_shapes=[pltpu.VMEM((HEADS, HD, HD), F32)], compiler_params=_cp("arbitrary"),
    )(cqkv, cqkv, cqkv, proj, proj, a_log, dt_bias, gain)


def _gdn_bwd(cqkv, proj, a_log, dt_bias, gain, sprev, dy):
    t = proj.shape[0]
    nc = t // GDN_CHUNK
    wide, ab, row, state = _gdn_specs(nc, True)

    def body(cq_ref, ck_ref, cv_ref, ab_ref, z_ref, al_ref, dt_ref, g_ref, sp_ref, dy_ref,
             dc_ref, dab_ref, dz_ref, dal_ref, ddt_ref, dg_ref, ds_scr):
        @pl.when(pl.program_id(0) == 0)
        def _():
            ds_scr[...] = jnp.zeros_like(ds_scr)
            dal_ref[...] = jnp.zeros_like(dal_ref)
            ddt_ref[...] = jnp.zeros_like(ddt_ref)
            dg_ref[...] = jnp.zeros_like(dg_ref)

        (u, w, qk, qd, kd, dec), prep_vjp = jax.vjp(
            _gdn_prep, _heads(cq_ref[...]), _heads(ck_ref[...]), _heads(cv_ref[...]), ab_ref[...], al_ref[...], dt_ref[...])
        s = sp_ref[...]
        v_new = u - _hb(w, s)
        o = _hb(qd, s) + _hb(qk, v_new)
        _, post_vjp = jax.vjp(_gdn_post, o, _heads(z_ref[...]), g_ref[...])
        do, dz, dgain = post_vjp(_heads(dy_ref[...]).astype(F32))
        ds_next = ds_scr[...]
        d_vnew = _hb(qk, do, 1, 1) + _hb(kd, ds_next)
        d_qk = _hb(do, v_new, 2, 2)
        d_qd = _hb(do, s, 2, 2)
        d_kd = _hb(v_new, ds_next, 2, 2)
        d_dec = jnp.sum(jnp.sum(s * ds_next, axis=2, keepdims=True), axis=1, keepdims=True)
        ds_scr[...] = dec * ds_next + _hb(qd, do, 1, 1) - _hb(w, d_vnew, 1, 1)
        d_w = -_hb(d_vnew, s, 2, 2)
        dcq, dck, dcv, dab, dal, ddt = prep_vjp((d_vnew, d_w, d_qk, d_qd, d_kd, d_dec))
        for h in range(HEADS):
            dc_ref[:, h * HD:(h + 1) * HD] = dcq[h]
            dc_ref[:, GDN_W + h * HD:GDN_W + (h + 1) * HD] = dck[h]
            dc_ref[:, 2 * GDN_W + h * HD:2 * GDN_W + (h + 1) * HD] = dcv[h]
            dz_ref[:, h * HD:(h + 1) * HD] = dz[h].astype(dz_ref.dtype)
        dab_ref[...] = dab
        dal_ref[...] += dal
        ddt_ref[...] += ddt
        dg_ref[...] += dgain

    c = GDN_CHUNK
    return pl.pallas_call(
        body, name="gdn_bwd", grid=(nc,),
        in_specs=[wide(0), wide(GDN_W), wide(2 * GDN_W), ab, wide(OFF_Z), row, row, row, state, wide(0)],
        out_specs=[pl.BlockSpec((c, 3 * GDN_W), lambda n: (nc - 1 - n, 0)), pl.BlockSpec((c, HD), lambda n: (nc - 1 - n, 0)),
                   wide(0), row, row, row],
        out_shape=[_sds((t, 3 * GDN_W), F32), _sds((t, HD), F32), _sds((t, GDN_W), BF16),
                   _sds((1, HD), F32), _sds((1, HD), F32), _sds((1, HD), F32)],
        scratch_shapes=[pltpu.VMEM((HEADS, HD, HD), F32)], compiler_params=_cp("arbitrary"),
    )(cqkv, cqkv, cqkv, proj, proj, a_log, dt_bias, gain, sprev, dy)


MERGE_TN = 512


def _merge_specs(t, tm):
    tn = MERGE_TN
    ys = [pl.BlockSpec((tm, wd), lambda i, j: (i, 0)) for wd in (POOL_W, SB_W, GDN_W)]
    ws = [pl.BlockSpec((None, wd, tn), lambda i, j: (j, 0, 0)) for wd in (POOL_W, SB_W, GDN_W)]
    gs = [pl.BlockSpec((tm, tn), functools.partial(lambda i, j, b: (i, OFF_GATE // tn + b * (D // tn) + j), b=b))
          for b in range(3)]
    out = pl.BlockSpec((tm, tn), lambda i, j: (i, j))
    return ys, ws, gs, out


def _merge_fwd(ys, wups, proj):
    t = proj.shape[0]
    tm = min(512, t)
    y_specs, w_specs, g_specs, out = _merge_specs(t, tm)

    def body(y0, y1, y2, w0, w1, w2, g0, g1, g2, o_ref):
        acc = jnp.zeros(o_ref.shape, F32)
        for y, w, g in ((y0, w0, g0), (y1, w1, g1), (y2, w2, g2)):
            acc = acc + jax.nn.sigmoid(g[...]) * _dot(y[...], w[...])
        o_ref[...] = acc.astype(o_ref.dtype)

    return pl.pallas_call(
        body, name="merge_fwd", grid=(t // tm, D // MERGE_TN), in_specs=[*y_specs, *w_specs, *g_specs],
        out_specs=out, out_shape=_sds((t, D), BF16), compiler_params=_cp("parallel", "parallel"),
    )(*ys, *wups, proj, proj, proj)


def _merge_bwd(ys, wups, proj, dmerged):
    t = proj.shape[0]
    tm = min(512, t)
    y_specs, w_specs, g_specs, out = _merge_specs(t, tm)

    def body(y0, y1, y2, w0, w1, w2, g0, g1, g2, dm_ref, dg0, dg1, dg2, dm0, dm1, dm2):
        dm = dm_ref[...].astype(F32)
        for y, w, g, dg, dmb in ((y0, w0, g0, dg0, dm0), (y1, w1, g1, dg1, dm1), (y2, w2, g2, dg2, dm2)):
            sg = jax.nn.sigmoid(g[...])
            dg[...] = (dm * _dot(y[...], w[...]) * sg * (1.0 - sg)).astype(dg.dtype)
            dmb[...] = (dm * sg).astype(dmb.dtype)

    return pl.pallas_call(
        body, name="merge_bwd", grid=(t // tm, D // MERGE_TN), in_specs=[*y_specs, *w_specs, *g_specs, out],
        out_specs=[out] * 6, out_shape=[_sds((t, D), BF16)] * 6, compiler_params=_cp("parallel", "parallel"),
    )(*ys, *wups, proj, proj, proj, dmerged)


def _tile(t, want):
    return min(t, want)


def _layer_fwd(x, l, gw, w_al, sp, reached=None):
    t = x.shape[0]
    tm = _tile(t, 1024)
    u = _rms_fwd("rms_attn", x, sp["attn_norm"][l])
    proj = _mm("proj", u, w_al, m=t, n=N_AL, k=D, tm=tm, tn=1024, tk=D, a_spec=_a_plain(tm, D),
               b_spec=_b_plain(D, 1024), dims=None, out_shapes=[_sds((t, N_AL), F32)], out_specs=[_o_plain(tm, 1024)])[0]
    if reached is not None:
        reached("proj", proj)
    y_pool = _pool_fwd(proj, sp["pool_w"][l], sp["pool_scale"][l])
    kv = _sb_cast_kv(proj)
    y_sb = _sb_fwd(proj, kv)
    cqkv = _conv_fwd(proj, sp["conv"][l])
    y_gdn, sprev = _gdn_fwd(cqkv, proj, sp["a_log"][l], sp["dt_bias"][l], sp["gdn_norm"][l])
    ys = (y_pool, y_sb, y_gdn)
    wups = (gw["w_pool_up"], gw["w_sb_up"], gw["w_gdn_up"])
    merged = _merge_fwd(ys, wups, proj)
    if reached is not None:
        reached("merged", merged)
    x1 = _mm("out_proj", merged, gw["w_out"], m=t, n=D, k=D, tm=tm, tn=1024, tk=512, a_spec=_a_plain(tm, 512),
             b_spec=_w_rows(512, 1024, 512), dims=None, out_shapes=[_sds((t, D), F32)], out_specs=[_o_plain(tm, 1024)],
             extras=[x], extra_specs=[_o_plain(tm, 1024)], epilogue=lambda r, xr: (r + xr,))[0]
    u2 = _rms_fwd("rms_mlp", x1, sp["mlp_norm"][l])

    def relu2(r):
        hv = jnp.maximum(r, 0.0)
        return hv, hv * hv

    hid, hid2 = _mm("ff1", u2, gw["w_ff1"], m=t, n=D_FF, k=D, tm=tm, tn=1024, tk=D, a_spec=_a_plain(tm, D),
                    b_spec=_w_cols(D, 1024, 2048), dims=None, out_shapes=[_sds((t, D_FF), BF16)] * 2,
                    out_specs=[_o_plain(tm, 1024)] * 2, epilogue=relu2)
    x2 = _mm("ff2", hid2, gw["w_ff2"], m=t, n=D, k=D_FF, tm=tm, tn=1024, tk=2048, a_spec=_a_plain(tm, 2048),
             b_spec=_w_rows(2048, 1024, 2048), dims=None, out_shapes=[_sds((t, D), F32)], out_specs=[_o_plain(tm, 1024)],
             extras=[x1], extra_specs=[_o_plain(tm, 1024)], epilogue=lambda r, xr: (r + xr,))[0]
    saved = dict(x=x, u=u, proj=proj, kv=kv, cqkv=cqkv, sprev=sprev, ys=ys, merged=merged, x1=x1, u2=u2, hid=hid, hid2=hid2)
    return x2, saved


def _layer_bwd(dx2, l, gw, w_al, sp, sv, emit=None):
    t = dx2.shape[0]
    tm = _tile(t, 1024)
    tk = t
    g = {}
    if emit is None:
        emit = lambda names, grads, v: v
    dpre = _mm("ff2_dx", dx2, gw["w_ff2"], m=t, n=D_FF, k=D, tm=tm, tn=1024, tk=D, a_spec=_a_plain(tm, D),
               b_spec=_w_rows_t(D, 1024, 2048), dims=NT, out_shapes=[_sds((t, D_FF), BF16)],
               out_specs=[_o_plain(tm, 1024)], extras=[sv["hid"]], extra_specs=[_o_plain(tm, 1024)],
               epilogue=lambda r, hv: (r * (2.0 * hv.astype(F32)),))[0]
    g["w_ff2"] = _mm("ff2_dw", sv["hid2"], dx2, m=D_FF, n=D, k=t, tm=1024, tn=1024, tk=tk, a_spec=_a_trans(1024, tk),
                     b_spec=_b_plain(tk, 1024), dims=TN, out_shapes=[_sds((D_FF, D), BF16)],
                     out_specs=[_o_plain(1024, 1024)])[0].reshape(N_CHIPS, D_FF // N_CHIPS, D)
    du2 = _mm("ff1_dx", dpre, gw["w_ff1"], m=t, n=D, k=D_FF, tm=tm, tn=1024, tk=2048, a_spec=_a_plain(tm, 2048),
              b_spec=_w_cols_t(2048, 1024, 2048), dims=NT, out_shapes=[_sds((t, D), F32)], out_specs=[_o_plain(tm, 1024)])[0]
    g["w_ff1"] = _mm("ff1_dw", sv["u2"], dpre, m=D, n=D_FF, k=t, tm=1024, tn=1024, tk=tk, a_spec=_a_trans(1024, tk),
                     b_spec=_b_plain(tk, 1024), dims=TN, out_shapes=[_sds((N_CHIPS, D, D_FF // N_CHIPS), BF16)],
                     out_specs=[_o_colshard(1024, 1024, D_FF // N_CHIPS)])[0]
    dx1, g["mlp_norm"] = _rms_bwd("rms_mlp_bwd", du2, sv["x1"], sp["mlp_norm"][l], dx2)
    dx1 = emit(("w_ff1", "w_ff2"), g, dx1)
    dmerged = _mm("out_dx", dx1, gw["w_out"], m=t, n=D, k=D, tm=tm, tn=512, tk=D, a_spec=_a_plain(tm, D),
                  b_spec=_w_rows_t(D, 512, 512), dims=NT, out_shapes=[_sds((t, D), BF16)], out_specs=[_o_plain(tm, 512)])[0]
    g["w_out"] = _mm("out_dw", sv["merged"], dx1, m=D, n=D, k=t, tm=1024, tn=1024, tk=tk, a_spec=_a_trans(1024, tk),
                     b_spec=_b_plain(tk, 1024), dims=TN, out_shapes=[_sds((D, D), BF16)],
                     out_specs=[_o_plain(1024, 1024)])[0].reshape(N_CHIPS, D // N_CHIPS, D)
    wups = (gw["w_pool_up"], gw["w_sb_up"], gw["w_gdn_up"])
    dg0, dg1, dg2, dm0, dm1, dm2 = _merge_bwd(sv["ys"], wups, sv["proj"], dmerged)
    dys = []
    for nm, yb, dmb, wd in zip(("w_pool_up", "w_sb_up", "w_gdn_up"), sv["ys"], (dm0, dm1, dm2), (POOL_W, SB_W, GDN_W)):
        dys.append(_mm(nm + "_dx", dmb, gw[nm], m=t, n=wd, k=D, tm=tm, tn=256, tk=512, a_spec=_a_plain(tm, 512),
                       b_spec=_w_cols_t(512, 256, 512), dims=NT, out_shapes=[_sds((t, wd), F32)],
                       out_specs=[_o_plain(tm, 256)])[0])
        g[nm] = _mm(nm + "_dw", yb, dmb, m=wd, n=D, k=t, tm=256, tn=512, tk=tk, a_spec=_a_trans(256, tk),
                    b_spec=_b_plain(tk, 512), dims=TN, out_shapes=[_sds((N_CHIPS, wd, D // N_CHIPS), BF16)],
                    out_specs=[_o_colshard(256, 512, D // N_CHIPS)])[0]
    dys[2] = emit(("w_pool_up", "w_sb_up", "w_gdn_up", "w_out"), g, dys[2])
    proj = sv["proj"]
    dp, g["pool_w"], g["pool_scale"] = _pool_bwd(proj, sp["pool_w"][l], sp["pool_scale"][l], dys[0])
    dsq, dsk, dsv = _sb_bwd(proj, sv["kv"], dys[1])
    dc, dab, dz, g["a_log"], g["dt_bias"], g["gdn_norm"] = _gdn_bwd(
        sv["cqkv"], proj, sp["a_log"][l], sp["dt_bias"][l], sp["gdn_norm"][l], sv["sprev"], dys[2])
    dgx, g["conv"] = _conv_bwd(proj, sp["conv"][l], dc)
    dproj = jnp.concatenate(
        [dsq, dsk.astype(BF16), dsv.astype(BF16), dgx, dz, dab.astype(BF16), jnp.zeros((t, AB_W - HD), BF16),
         dp, dg0, dg1, dg2], axis=1)
    du = _mm("proj_dx", dproj, w_al, m=t, n=D, k=N_AL, tm=tm, tn=1024, tk=2048, a_spec=_a_plain(tm, 2048),
             b_spec=_b_trans(2048, 1024), dims=NT, out_shapes=[_sds((t, D), F32)], out_specs=[_o_plain(tm, 1024)])[0]
    g["w_al"] = _mm("proj_dw", sv["u"], dproj, m=D, n=N_AL, k=t, tm=1024, tn=1024, tk=tk, a_spec=_a_trans(1024, tk),
                    b_spec=_b_plain(tk, 1024), dims=TN, out_shapes=[_sds((D, N_AL), BF16)], out_specs=[_o_plain(1024, 1024)])[0]
    dx, g["attn_norm"] = _rms_bwd("rms_attn_bwd", du, sv["x"], sp["attn_norm"][l], dx1)
    g["w_in"] = _w_in_to_shards(g["w_al"])
    dx = emit(("w_in",), g, dx)
    return dx, g


def _align_w_in(w):
    n_ab = ORIG_GATE - ORIG_AB
    return jnp.concatenate([w[:, ORIG_SB:ORIG_GATE], jnp.zeros((D, AB_W - n_ab), w.dtype), w[:, :ORIG_SB], w[:, ORIG_GATE:]],
                           axis=1)


def _unalign_w_in(w):
    n_ab = ORIG_GATE - ORIG_AB
    return jnp.concatenate([w[:, OFF_P:OFF_GATE], w[:, :OFF_AB + n_ab], w[:, OFF_GATE:]], axis=1)


W_IN_RUNS = ((0, ORIG_SB, OFF_P), (ORIG_SB, ORIG_GATE, OFF_SB), (ORIG_GATE, N_IN, OFF_GATE))
W_IN_SHARD = N_IN // N_CHIPS


def _w_in_from_shards(gathered):
    parts = []
    for lo, hi, al in sorted(W_IN_RUNS, key=lambda r: r[2]):
        if al == OFF_P:
            parts.append(jnp.zeros((D, OFF_P - (OFF_AB + ORIG_GATE - ORIG_AB)), gathered.dtype))
        while lo < hi:
            chip = lo // W_IN_SHARD
            end = min(hi, (chip + 1) * W_IN_SHARD)
            parts.append(gathered[chip, :, lo - chip * W_IN_SHARD:end - chip * W_IN_SHARD])
            lo = end
    return jnp.concatenate(parts, axis=1)


def _w_in_to_shards(g_al):
    shards = []
    for chip in range(N_CHIPS):
        a, b = chip * W_IN_SHARD, (chip + 1) * W_IN_SHARD
        parts = [g_al[:, al + max(a, lo) - lo:al + min(b, hi) - lo] for lo, hi, al in W_IN_RUNS if max(a, lo) < min(b, hi)]
        shards.append(jnp.concatenate(parts, axis=1))
    return jnp.stack(shards)


def _row128(v):
    return jnp.pad(v.reshape(1, -1), ((0, 0), (0, HD - v.shape[-1])))


def _local_step(x, target, weights_of, sp, emit=None, reached=None):
    saved, gw, w_in_al = [], [], []
    h = x
    for l in range(2):
        gw_l, w_al_l = weights_of(l)
        gw.append(gw_l)
        w_in_al.append(w_al_l)
        h, sv = _layer_fwd(h, l, gw_l, w_al_l, sp, None if reached is None else functools.partial(reached, l))
        saved.append(sv)
    loss, dh, g_final = _loss_head(h, sp["final_norm"], target)
    grads = [None, None]
    for l in (1, 0):
        dh, grads[l] = _layer_bwd(dh, l, gw[l], w_in_al[l], sp, saved[l],
                                  None if emit is None else functools.partial(emit, l))
    return loss, dh, grads, g_final


ANY = pl.BlockSpec(memory_space=pl.ANY)


def _me():
    return lax.axis_index("x"), lax.axis_index("y"), lax.axis_index("c")


def _other_chips(x, y):
    return [(1 - x, y), (x, 1 - y), (1 - x, 1 - y)]


def _half(ref, axis, c, rows):
    half = rows // 2
    idx = [slice(None)] * axis + [pl.ds(pl.multiple_of(c * half, 16), half)]
    return ref.at[tuple(idx)]


def _gather_steps(out, send, recv):
    n = len(out)
    x, y, c = _me()
    mine = 2 * x + y
    sibling = (x, y, 1 - c)
    chips = _other_chips(x, y)
    sends = []
    for t in range(n):
        rows = out[t].shape[1]
        for k, (px, py) in enumerate(chips):
            own_half = _half(out[t].at[mine], 0, c, rows)
            cp = pltpu.make_async_remote_copy(
                src_ref=own_half, dst_ref=own_half,
                send_sem=send.at[6 * t + k], recv_sem=recv.at[6 * t + k], device_id=(px, py, c), device_id_type=MESH)
            cp.start()
            sends.append(cp)
    for t in range(n):
        rows = out[t].shape[1]
        for k, (px, py) in enumerate(chips):
            landed = _half(out[t].at[2 * px + py], 0, c, rows)
            pltpu.make_async_remote_copy(
                src_ref=landed, dst_ref=landed, send_sem=send.at[6 * t + k], recv_sem=recv.at[6 * t + k],
                device_id=(px, py, c), device_id_type=MESH).wait_recv()
            cp = pltpu.make_async_remote_copy(
                src_ref=landed, dst_ref=landed, send_sem=send.at[6 * t + 3 + k], recv_sem=recv.at[6 * t + 3 + k],
                device_id=sibling, device_id_type=MESH)
            cp.start()
            sends.append(cp)
    for t in range(n):
        rows = out[t].shape[1]
        for k, (px, py) in enumerate(chips):
            other = _half(out[t].at[2 * px + py], 0, 1 - c, rows)
            pltpu.make_async_remote_copy(
                src_ref=other, dst_ref=other, send_sem=send.at[6 * t + 3 + k], recv_sem=recv.at[6 * t + 3 + k],
                device_id=sibling, device_id_type=MESH).wait_recv()
    for cp in sends:
        cp.wait_send()


def _gather_weights(bufs):
    n = len(bufs)

    def body(*refs):
        _gather_steps(refs[n:2 * n], *refs[2 * n:])

    return pl.pallas_call(
        body, name="gather_weights", in_specs=[ANY] * n, out_specs=[ANY] * n,
        out_shape=[_sds(s.shape, s.dtype) for s in bufs], input_output_aliases={t: t for t in range(n)},
        scratch_shapes=[pltpu.SemaphoreType.DMA((6 * n,)), pltpu.SemaphoreType.DMA((6 * n,))],
    )(*bufs)


def _gather_weights_async(bufs, tag, collective_id):
    n = len(bufs)
    refs = [jax.new_ref(b, memory_space=pltpu.MemorySpace.HBM) for b in bufs]

    @pl.kernel(mesh=plsc.ScalarSubcoreMesh(axis_name="sequencer", num_cores=1), name=f"gather_async_{tag}",
               scratch_types=(pltpu.SemaphoreType.DMA((6 * n,)), pltpu.SemaphoreType.DMA((6 * n,))),
               compiler_params=pltpu.CompilerParams(collective_id=collective_id))
    def launch(send, recv):
        x, y, c = _me()
        barrier = pltpu.get_barrier_semaphore()
        peers = [(x, y, 1 - c)] + [(px, py, c) for px, py in _other_chips(x, y)]
        for peer in peers:
            pl.semaphore_signal(barrier, inc=1, device_id=peer, device_id_type=MESH)
        pl.semaphore_wait(barrier, len(peers))
        _gather_steps(refs, send, recv)

    launch()
    return [r[...] for r in refs]


def _rs_pair(grads):
    n = len(grads)

    def body(*refs):
        g, out = refs[:n], refs[n:2 * n]
        send, recv = refs[2 * n:]
        x, y, c = _me()
        copies = []
        for t in range(n):
            cp = pltpu.make_async_remote_copy(
                src_ref=_half(g[t], 1, 1 - c, g[t].shape[1]), dst_ref=out[t], send_sem=send.at[t], recv_sem=recv.at[t],
                device_id=(x, y, 1 - c), device_id_type=MESH)
            cp.start()
            copies.append(cp)
        for cp in copies:
            cp.wait()

    return pl.pallas_call(
        body, name="rs_pair", in_specs=[ANY] * n, out_specs=[ANY] * n,
        out_shape=[_sds((N_CHIPS, s.shape[1] // 2, s.shape[2]), s.dtype) for s in grads],
        scratch_shapes=[pltpu.SemaphoreType.DMA((n,)), pltpu.SemaphoreType.DMA((n,))],
    )(*grads)


def _rs_chips_steps(p, out, send, recv):
    x, y, c = _me()
    copies = []
    for t in range(len(p)):
        for k, (px, py) in enumerate(_other_chips(x, y)):
            cp = pltpu.make_async_remote_copy(
                src_ref=p[t].at[2 * px + py], dst_ref=out[t].at[k], send_sem=send.at[3 * t + k],
                recv_sem=recv.at[3 * t + k], device_id=(px, py, c), device_id_type=MESH)
            cp.start()
            copies.append(cp)
    for cp in copies:
        cp.wait()


def _rs_chips_async(parts, tag, collective_id):
    n = len(parts)
    src = [jax.new_ref(p, memory_space=pltpu.MemorySpace.HBM) for p in parts]
    got = [jax.empty_ref(_sds((3, *p.shape[1:]), p.dtype), memory_space=pltpu.MemorySpace.HBM) for p in parts]

    @pl.kernel(mesh=plsc.ScalarSubcoreMesh(axis_name="sequencer", num_cores=1), name=f"rs_chips_async_{tag}",
               scratch_types=(pltpu.SemaphoreType.DMA((3 * n,)), pltpu.SemaphoreType.DMA((3 * n,))),
               compiler_params=pltpu.CompilerParams(collective_id=collective_id))
    def launch(send, recv):
        x, y, c = _me()
        barrier = pltpu.get_barrier_semaphore()
        peers = [(px, py, c) for px, py in _other_chips(x, y)]
        for peer in peers:
            pl.semaphore_signal(barrier, inc=1, device_id=peer, device_id_type=MESH)
        pl.semaphore_wait(barrier, len(peers))
        _rs_chips_steps(src, got, send, recv)

    launch()
    return [g[...] for g in got]


def _rs_chips(parts):
    n = len(parts)

    def body(*refs):
        _rs_chips_steps(refs[:n], refs[n:2 * n], *refs[2 * n:])

    return pl.pallas_call(
        body, name="rs_chips", in_specs=[ANY] * n, out_specs=[ANY] * n,
        out_shape=[_sds((3, *s.shape[1:]), s.dtype) for s in parts],
        scratch_shapes=[pltpu.SemaphoreType.DMA((3 * n,)), pltpu.SemaphoreType.DMA((3 * n,))],
    )(*parts)


def _pair_exchange(bufs):
    n = len(bufs)

    def body(*refs):
        out = refs[n:2 * n]
        send, recv = refs[2 * n:]
        x, y, c = _me()
        copies = []
        for t in range(n):
            cp = pltpu.make_async_remote_copy(
                src_ref=out[t].at[c], dst_ref=out[t].at[c], send_sem=send.at[t], recv_sem=recv.at[t],
                device_id=(x, y, 1 - c), device_id_type=MESH)
            cp.start()
            copies.append(cp)
        for t, cp in enumerate(copies):
            cp.wait_send()
            pltpu.make_async_remote_copy(
                src_ref=out[t].at[1 - c], dst_ref=out[t].at[1 - c], send_sem=send.at[t], recv_sem=recv.at[t],
                device_id=(x, y, 1 - c), device_id_type=MESH).wait_recv()

    return pl.pallas_call(
        body, name="pair_exchange", in_specs=[ANY] * n, out_specs=[ANY] * n,
        out_shape=[_sds(s.shape, s.dtype) for s in bufs], input_output_aliases={t: t for t in range(n)},
        scratch_shapes=[pltpu.SemaphoreType.DMA((n,)), pltpu.SemaphoreType.DMA((n,))],
    )(*bufs)


def _row_tile(rows, cols, itemsize, budget=2 * 1024 * 1024):
    tr = rows
    while tr * cols * itemsize > budget and tr % 32 == 0:
        tr //= 2
    return tr


def _sum_pair(name, g, got, where):
    nchip, rows, cols = g.shape
    half = rows // 2
    tr = _row_tile(half, cols, 4)
    per = half // tr

    def body(w_ref, g_ref, r_ref, o_ref):
        o_ref[...] = (g_ref[...].astype(F32) + r_ref[...].astype(F32)).astype(o_ref.dtype)

    blk = pl.BlockSpec((None, tr, cols), lambda j, i, w_ref: (j, i, 0))
    return pl.pallas_call(
        body, name=name,
        grid_spec=pltpu.PrefetchScalarGridSpec(
            num_scalar_prefetch=1, grid=(nchip, per),
            in_specs=[pl.BlockSpec((None, tr, cols), lambda j, i, w_ref: (j, w_ref[1] * per + i, 0)), blk], out_specs=blk),
        out_shape=_sds((nchip, half, cols), BF16), compiler_params=_cp("parallel", "parallel"),
    )(where, g, got)


def _sum_chips(name, p, got, where):
    _, rows, cols = p.shape
    tr = _row_tile(rows, cols, 4)

    def body(w_ref, p_ref, r0, r1, r2, o_ref):
        o_ref[...] = ((p_ref[...].astype(F32) + r0[...].astype(F32)) + r1[...].astype(F32)) + r2[...].astype(F32)

    def got_k(k):
        return pl.BlockSpec((None, tr, cols), lambda i, w_ref: (k, i, 0))

    return pl.pallas_call(
        body, name=name,
        grid_spec=pltpu.PrefetchScalarGridSpec(
            num_scalar_prefetch=1, grid=(rows // tr,),
            in_specs=[pl.BlockSpec((None, tr, cols), lambda i, w_ref: (w_ref[0], i, 0)), got_k(0), got_k(1), got_k(2)],
            out_specs=pl.BlockSpec((None, tr, cols), lambda i, w_ref: (w_ref[1], i, 0))),
        out_shape=_sds((2, rows, cols), F32), compiler_params=_cp("parallel"),
    )(where, p, got, got, got)


def _reduce_scatter(grads, where):
    return _rs_finish(*_rs_begin(grads, where), where)


def _rs_begin(grads, where, tag=None, collective_id=None):
    got = _rs_pair(grads)
    parts = [_sum_pair(f"sum_pair_{t}", g, r, where) for t, (g, r) in enumerate(zip(grads, got))]
    return parts, (_rs_chips(parts) if tag is None else _rs_chips_async(parts, tag, collective_id))


def _rs_finish(parts, got, where):
    halves = [_sum_chips(f"sum_chips_{t}", p, r, where) for t, (p, r) in enumerate(zip(parts, got))]
    return _pair_exchange(halves)


def _all_reduce_small(name, v):
    rows = v.shape[0]

    def body(v_ref, o_ref, land, send, recv):
        x, y, c = _me()
        mine = 4 * x + 2 * y + c
        copies = []
        for k in range(1, 8):
            kx, ky, kc = k >> 2, (k >> 1) & 1, k & 1
            peer = (x ^ kx, y ^ ky, c ^ kc)
            cp = pltpu.make_async_remote_copy(
                src_ref=v_ref, dst_ref=land.at[mine], send_sem=send.at[k - 1], recv_sem=recv.at[k - 1],
                device_id=peer, device_id_type=MESH)
            cp.start()
            copies.append(cp)
        land[mine] = v_ref[...]
        for k in range(1, 8):
            kx, ky, kc = k >> 2, (k >> 1) & 1, k & 1
            src = 4 * (x ^ kx) + 2 * (y ^ ky) + (c ^ kc)
            pltpu.make_async_remote_copy(
                src_ref=v_ref, dst_ref=land.at[src], send_sem=send.at[k - 1], recv_sem=recv.at[k - 1],
                device_id=(x ^ kx, y ^ ky, c ^ kc), device_id_type=MESH).wait_recv()
        acc = land[0]
        for d in range(1, 8):
            acc = acc + land[d]
        o_ref[...] = acc
        for cp in copies:
            cp.wait_send()

    vm = pl.BlockSpec(memory_space=pltpu.VMEM)
    return pl.pallas_call(
        body, name=name, in_specs=[vm], out_specs=vm, out_shape=_sds((rows, 128), F32),
        scratch_shapes=[pltpu.VMEM((8, rows, 128), F32), pltpu.SemaphoreType.DMA((7,)), pltpu.SemaphoreType.DMA((7,))],
    )(v)


def _adamw(name, w, g, m, v):
    rows, cols = w.shape
    tr = _row_tile(rows, cols, 4, budget=1024 * 1024)
    c1 = 1.0 / (1.0 - ADAM_B1 ** ADAM_STEP)
    c2 = 1.0 / (1.0 - ADAM_B2 ** ADAM_STEP)

    def body(w_ref, g_ref, m_ref, v_ref, d_ref, nm_ref, nv_ref):
        gv = g_ref[...]
        nm = ADAM_B1 * m_ref[...] + (1.0 - ADAM_B1) * gv
        nv = ADAM_B2 * v_ref[...] + (1.0 - ADAM_B2) * (gv * gv)
        d_ref[...] = -ADAM_LR * ((nm * c1) / (jnp.sqrt(nv * c2) + ADAM_EPS) + ADAM_WD * w_ref[...])
        nm_ref[...] = nm
        nv_ref[...] = nv

    blk = pl.BlockSpec((tr, cols), lambda i: (i, 0))
    return pl.pallas_call(
        body, name=name, grid=(rows // tr,), in_specs=[blk] * 4, out_specs=[blk] * 3,
        out_shape=[_sds((rows, cols), F32)] * 3, compiler_params=_cp("parallel"),
    )(w, g, m, v)


def _adamw_layers(name, w, g0, g1, m, v):
    _, half, cols = g0.shape
    tr = _row_tile(half, cols, 4, budget=1024 * 1024)
    per_half = half // tr
    per = 2 * per_half
    c1 = 1.0 / (1.0 - ADAM_B1 ** ADAM_STEP)
    c2 = 1.0 / (1.0 - ADAM_B2 ** ADAM_STEP)

    def body(w_ref, g0_ref, g1_ref, m_ref, v_ref, g_ref, d_ref, nm_ref, nv_ref):
        gv = jnp.where(pl.program_id(0) == 0, g0_ref[...], g1_ref[...])
        nm = ADAM_B1 * m_ref[...] + (1.0 - ADAM_B1) * gv
        nv = ADAM_B2 * v_ref[...] + (1.0 - ADAM_B2) * (gv * gv)
        g_ref[...] = gv
        d_ref[...] = -ADAM_LR * ((nm * c1) / (jnp.sqrt(nv * c2) + ADAM_EPS) + ADAM_WD * w_ref[...])
        nm_ref[...] = nm
        nv_ref[...] = nv

    both = pl.BlockSpec((None, tr, cols), lambda l, i: (l, i, 0))

    def halves(i):
        return i // per_half, i % per_half, 0

    first = pl.BlockSpec((None, tr, cols), lambda l, i: halves(i * (1 - l) + (per - 1) * l))
    second = pl.BlockSpec((None, tr, cols), lambda l, i: halves(i * l))
    return pl.pallas_call(
        body, name=name, grid=(2, per), in_specs=[both, first, second, both, both], out_specs=[both] * 4,
        out_shape=[_sds(w.shape, F32)] * 4, compiler_params=_cp("arbitrary", "arbitrary"),
    )(w, g0, g1, m, v)


def _to_bf16_slot(name, w, l, where):
    _, rows, cols = w.shape
    tr = _row_tile(rows, cols, 4)

    def body(w_ref, x_ref, o_ref):
        o_ref[...] = x_ref[...].astype(BF16)

    return pl.pallas_call(
        body, name=name,
        grid_spec=pltpu.PrefetchScalarGridSpec(
            num_scalar_prefetch=1, grid=(rows // tr,), in_specs=[pl.BlockSpec((None, tr, cols), lambda i, w_ref: (l, i, 0))],
            out_specs=pl.BlockSpec((None, tr, cols), lambda i, w_ref: (w_ref[0], i, 0))),
        out_shape=_sds((N_CHIPS, rows, cols), BF16), compiler_params=_cp("parallel"))(where, w)


BIG = ("w_in", "w_pool_up", "w_sb_up", "w_gdn_up", "w_out", "w_ff1", "w_ff2")
SMALL = (("attn_norm", (D,)), ("pool_w", (4, 128, 128)), ("pool_scale", (POOL_W,)), ("gdn_a_log", (HEADS,)),
         ("gdn_dt_bias", (HEADS,)), ("gdn_norm", (HD,)), ("mlp_norm", (D,)))


PACK_TILE = 8 * 128


def _rows128(a):
    flat = a.reshape(-1)
    pad = (-flat.shape[0]) % PACK_TILE
    return jnp.pad(flat, (0, pad)).reshape(-1, 128)


def _pack(parts):
    packed = jnp.concatenate([_rows128(p) for p in parts], axis=0)
    return jnp.pad(packed, ((0, (-packed.shape[0]) % 8), (0, 0)))


def _unpack(packed, shapes):
    out, r = [], 0
    for shp in shapes:
        size = 1
        for s in shp:
            size *= s
        nr = -(-size // PACK_TILE) * 8
        out.append(packed[r:r + nr].reshape(-1)[:size].reshape(shp))
        r += nr
    return out


def kernel(x, attn_norm, w_in, pool_w, pool_scale, gdn_conv, gdn_a_log, gdn_dt_bias, gdn_norm, w_pool_up, w_sb_up, w_gdn_up, w_out, mlp_norm, w_ff1, w_ff2, final_norm, loss_target, m_attn_norm, m_w_in, m_pool_w, m_pool_scale, m_gdn_conv, m_gdn_a_log, m_gdn_dt_bias, m_gdn_norm, m_w_pool_up, m_w_sb_up, m_w_gdn_up, m_w_out, m_mlp_norm, m_w_ff1, m_w_ff2, m_final_norm, v_attn_norm, v_w_in, v_pool_w, v_pool_scale, v_gdn_conv, v_gdn_a_log, v_gdn_dt_bias, v_gdn_norm, v_w_pool_up, v_w_sb_up, v_w_gdn_up, v_w_out, v_mlp_norm, v_w_ff1, v_w_ff2, v_final_norm):
    weights = dict(attn_norm=attn_norm, w_in=w_in, pool_w=pool_w, pool_scale=pool_scale, gdn_conv=gdn_conv,
                   gdn_a_log=gdn_a_log, gdn_dt_bias=gdn_dt_bias, gdn_norm=gdn_norm, w_pool_up=w_pool_up, w_sb_up=w_sb_up,
                   w_gdn_up=w_gdn_up, w_out=w_out, mlp_norm=mlp_norm, w_ff1=w_ff1, w_ff2=w_ff2, final_norm=final_norm)
    mom1 = dict(attn_norm=m_attn_norm, w_in=m_w_in, pool_w=m_pool_w, pool_scale=m_pool_scale, gdn_conv=m_gdn_conv,
                gdn_a_log=m_gdn_a_log, gdn_dt_bias=m_gdn_dt_bias, gdn_norm=m_gdn_norm, w_pool_up=m_w_pool_up,
                w_sb_up=m_w_sb_up, w_gdn_up=m_w_gdn_up, w_out=m_w_out, mlp_norm=m_mlp_norm, w_ff1=m_w_ff1, w_ff2=m_w_ff2,
                final_norm=m_final_norm)
    mom2 = dict(attn_norm=v_attn_norm, w_in=v_w_in, pool_w=v_pool_w, pool_scale=v_pool_scale, gdn_conv=v_gdn_conv,
                gdn_a_log=v_gdn_a_log, gdn_dt_bias=v_gdn_dt_bias, gdn_norm=v_gdn_norm, w_pool_up=v_w_pool_up,
                w_sb_up=v_w_sb_up, w_gdn_up=v_w_gdn_up, w_out=v_w_out, mlp_norm=v_mlp_norm, w_ff1=v_w_ff1, w_ff2=v_w_ff2,
                final_norm=v_final_norm)
    xi, yi, ci = lax.axis_index("x"), lax.axis_index("y"), lax.axis_index("c")
    chip = 2 * xi + yi
    where = jnp.stack([chip, ci]).astype(jnp.int32)

    bufs = [[_to_bf16_slot(f"cast_{nm}_{l}", weights[nm], l, where) for nm in BIG] for l in range(2)]
    first = _gather_weights_async(bufs[0][:1], "0_w_in", 4)
    first, _ = lax.optimization_barrier((first, (bufs[0][1:], bufs[1])))
    rest, _ = lax.optimization_barrier((bufs[0][1:], first))
    gw = [dict(zip(BIG, list(first) + _gather_weights_async(rest, "0_rest", 1))), {}]

    def reached(l, stage, value):
        if l == 0 and stage == "proj":
            later, _ = lax.optimization_barrier((bufs[1][:1], value))
            gw[1]["w_in"] = _gather_weights_async(later, "1_w_in", 2)[0]
        if l == 0 and stage == "merged":
            later, _ = lax.optimization_barrier((bufs[1][1:], value))
            gw[1].update(zip(BIG[1:], _gather_weights_async(later, "1_rest", 3)))

    def weights_of(l):
        return gw[l], _w_in_from_shards(gw[l]["w_in"])

    conv_cols = gdn_conv.shape[-1]
    conv_place = lax.dynamic_update_slice(jnp.zeros((2, GDN_CONV, N_CHIPS * conv_cols), F32),
                                          jnp.where(ci == 0, gdn_conv, 0.0), (0, 0, chip * conv_cols))
    conv_full = _all_reduce_small("gather_conv", _rows128(conv_place)).reshape(2, GDN_CONV, N_CHIPS * conv_cols)
    sp = dict(attn_norm=attn_norm.reshape(2, 1, D), pool_w=pool_w, pool_scale=pool_scale.reshape(2, 1, POOL_W),
              conv=conv_full, a_log=jnp.stack([_row128(gdn_a_log[l]) for l in range(2)]),
              dt_bias=jnp.stack([_row128(gdn_dt_bias[l]) for l in range(2)]), gdn_norm=gdn_norm.reshape(2, 1, HD),
              mlp_norm=mlp_norm.reshape(2, 1, D), final_norm=final_norm.reshape(1, D))

    started = []

    def emit(l, names, g, v):
        parts, got = _rs_begin([g[nm] for nm in names], where, f"{l}_{names[0]}", 5 + len(started))
        started.append((l, names, parts, got))
        v, _ = lax.optimization_barrier((v, parts))
        return v

    loss, grad_x, grads, g_final = _local_step(x[0], loss_target[0], weights_of, sp, emit, reached)
    loss = lax.psum(loss[0, 0], ("x", "y", "c"))
    big_grads = {nm: [None, None] for nm in BIG}
    for l, names, parts, got in started[:-1]:
        got, _ = lax.optimization_barrier((got, grad_x))
        for nm, red in zip(names, _rs_finish(parts, got, where)):
            big_grads[nm][l] = red
    small_parts, small_shapes = [], []
    for l in range(2):
        g = grads[l]
        for nm, shp in SMALL:
            key = {"gdn_a_log": "a_log", "gdn_dt_bias": "dt_bias"}.get(nm, nm)
            val = g[key]
            small_parts.append(val[0, :HEADS] if nm in ("gdn_a_log", "gdn_dt_bias") else val)
            small_shapes.append(shp)
        small_parts.append(g["conv"])
        small_shapes.append((GDN_CONV, N_CHIPS * conv_cols))
    small_parts.append(g_final)
    small_shapes.append((D,))
    reduced = _unpack(_all_reduce_small("reduce_small", _pack(small_parts)), small_shapes)
    per = len(SMALL) + 1
    grad = {}
    for i, (nm, _) in enumerate(SMALL):
        grad[nm] = jnp.stack([reduced[i], reduced[per + i]])
    conv_g = jnp.stack([reduced[per - 1], reduced[2 * per - 1]])
    grad["gdn_conv"] = lax.dynamic_slice(conv_g, (0, 0, chip * conv_cols), (2, GDN_CONV, conv_cols))
    grad["final_norm"] = reduced[-1]

    delta, new_m, new_v = {}, {}, {}
    for nm in BIG[1:]:
        grad[nm], delta[nm], new_m[nm], new_v[nm] = _adamw_layers("adamw_" + nm, weights[nm], *big_grads[nm], mom1[nm], mom2[nm])
    l, names, parts, got = started[-1]
    got, _ = lax.optimization_barrier((got, [new_v[nm] for nm in BIG[1:]]))
    for nm, red in zip(names, _rs_finish(parts, got, where)):
        big_grads[nm][l] = red
    nm = BIG[0]
    grad[nm], delta[nm], new_m[nm], new_v[nm] = _adamw_layers("adamw_" + nm, weights[nm], *big_grads[nm], mom1[nm], mom2[nm])
    small_names = [nm for nm, _ in SMALL] + ["gdn_conv", "final_norm"]
    packs = [_pack([src[nm] for nm in small_names]) for src in (weights, grad, mom1, mom2)]
    outs = _adamw("adamw_small", *packs)
    shapes = [weights[nm].shape for nm in small_names]
    for dst, packed in zip((delta, new_m, new_v), outs):
        for nm, val in zip(small_names, _unpack(packed, shapes)):
            dst[nm] = val

    order = ("attn_norm", "w_in", "pool_w", "pool_scale", "gdn_conv", "gdn_a_log", "gdn_dt_bias", "gdn_norm", "w_pool_up",
             "w_sb_up", "w_gdn_up", "w_out", "mlp_norm", "w_ff1", "w_ff2", "final_norm")
    return (loss, grad_x[None], *[grad[n] for n in order], *[delta[n] for n in order], *[new_m[n] for n in order],
            *[new_v[n] for n in order])
```

```python
import functools

import jax
import jax.numpy as jnp
from jax import lax
from jax.experimental import pallas as pl
from jax.experimental.pallas import tpu as pltpu
from jax.experimental.pallas import tpu_sc as plsc

F32, BF16 = jnp.float32, jnp.bfloat16
HIGH = lax.Precision.HIGH
MESH = pl.DeviceIdType.MESH

D = 2048
EPS = 1e-6
POOL_WINDOWS = (2, 4, 8, 16)
POOL_W, SB_W, GDN_W = 512, 768, 768
HEADS, HD = 6, 128
SB_BLOCK = 128
GDN_CHUNK = 64
D_FF = 4 * D
N_IN = 12044
N_CHIPS = 4
OFF_SB, OFF_GQKV, OFF_Z, OFF_AB, OFF_P, OFF_GATE = 0, 2304, 4608, 5376, 5632, 6144
AB_W = 256
ORIG_SB, ORIG_AB, ORIG_GATE = 512, 5888, 5900
N_AL = 12288
VMEM_LIMIT = 48 * 1024 * 1024

ADAM_LR, ADAM_B1, ADAM_B2, ADAM_EPS, ADAM_WD, ADAM_STEP = 0.001, 0.9, 0.999, 1e-08, 0.01, 10

NT = (((1,), (1,)), ((), ()))
TN = (((0,), (0,)), ((), ()))


def _cp(*sem):
    return pltpu.CompilerParams(dimension_semantics=sem, vmem_limit_bytes=VMEM_LIMIT)


def _dot(a, b, dims=None, precision=None):
    if dims is None:
        dims = (((a.ndim - 1,), (0,)), ((), ()))
    return lax.dot_general(a, b, dims, precision=precision, preferred_element_type=F32)


def _hdot(a, b, dims=None):
    return _dot(a, b, dims, precision=HIGH)


def _bdot(a, b, dims=None):
    return _dot(a.astype(BF16), b.astype(BF16), dims)


def _mm(name, a, b, *, m, n, k, tm, tn, tk, a_spec, b_spec, dims, out_shapes, out_specs,
        extras=(), extra_specs=(), epilogue=None):
    nk = k // tk
    ne, no = len(extras), len(out_shapes)

    def body(*refs):
        a_ref, b_ref = refs[0], refs[1]
        ex = refs[2:2 + ne]
        outs = refs[2 + ne:2 + ne + no]
        kk = pl.program_id(2)

        def finish(r):
            res = epilogue(r, *[e[...] for e in ex]) if epilogue is not None else (r,)
            for o, v in zip(outs, res):
                o[...] = v.astype(o.dtype)

        part = _dot(a_ref[...].astype(BF16), b_ref[...].astype(BF16), dims)
        if nk == 1:
            finish(part)
            return
        acc = refs[-1]

        @pl.when(kk == 0)
        def _():
            acc[...] = part

        @pl.when((kk > 0) & (kk < nk - 1))
        def _():
            acc[...] += part

        @pl.when(kk == nk - 1)
        def _():
            finish(acc[...] + part)

    return pl.pallas_call(
        body, name=name, grid=(m // tm, n // tn, nk),
        in_specs=[a_spec, b_spec, *extra_specs], out_specs=out_specs, out_shape=out_shapes,
        scratch_shapes=[] if nk == 1 else [pltpu.VMEM((tm, tn), F32)],
        compiler_params=_cp("parallel", "parallel", "arbitrary"),
    )(a, b, *extras)


def _a_plain(tm, tk):
    return pl.BlockSpec((tm, tk), lambda i, j, kk: (i, kk))


def _a_trans(tm, tk):
    return pl.BlockSpec((tk, tm), lambda i, j, kk: (kk, i))


def _b_plain(tk, tn):
    return pl.BlockSpec((tk, tn), lambda i, j, kk: (kk, j))


def _b_trans(tk, tn):
    return pl.BlockSpec((tn, tk), lambda i, j, kk: (j, kk))


def _o_plain(tm, tn):
    return pl.BlockSpec((tm, tn), lambda i, j, kk: (i, j))


def _o_colshard(tm, tn, ns_cols):
    per = ns_cols // tn
    return pl.BlockSpec((None, tm, tn), lambda i, j, kk: (j // per, i, j % per))


def _w_cols(tk, tn, ns):
    per = ns // tn
    return pl.BlockSpec((None, tk, tn), lambda i, j, kk: (j // per, kk, j % per))


def _w_cols_t(tk, tn, ns):
    per = ns // tk
    return pl.BlockSpec((None, tn, tk), lambda i, j, kk: (kk // per, j, kk % per))


def _w_rows(tk, tn, ks):
    per = ks // tk
    return pl.BlockSpec((None, tk, tn), lambda i, j, kk: (kk // per, kk % per, j))


def _w_rows_t(tk, tn, ks):
    per = ks // tn
    return pl.BlockSpec((None, tn, tk), lambda i, j, kk: (j // per, j % per, kk))


def _sds(shape, dtype):
    return jax.ShapeDtypeStruct(shape, dtype)


def _rms_fwd(name, x, gain):
    t = x.shape[0]
    tt = min(256, t)

    def body(x_ref, g_ref, u_ref):
        xv = x_ref[...]
        r = lax.rsqrt(jnp.mean(xv * xv, axis=-1, keepdims=True) + EPS)
        u_ref[...] = (xv * r * g_ref[...]).astype(u_ref.dtype)

    return pl.pallas_call(
        body, name=name, grid=(t // tt,),
        in_specs=[pl.BlockSpec((tt, D), lambda i: (i, 0)), pl.BlockSpec((1, D), lambda i: (0, 0))],
        out_specs=pl.BlockSpec((tt, D), lambda i: (i, 0)), out_shape=_sds((t, D), BF16),
        compiler_params=_cp("parallel"),
    )(x, gain)


def _rms_bwd(name, du, x, gain, dres):
    t = x.shape[0]
    tt = min(256, t)

    def body(du_ref, x_ref, g_ref, dres_ref, dx_ref, dg_ref):
        @pl.when(pl.program_id(0) == 0)
        def _():
            dg_ref[...] = jnp.zeros_like(dg_ref)

        xv, duv = x_ref[...], du_ref[...]
        r = lax.rsqrt(jnp.mean(xv * xv, axis=-1, keepdims=True) + EPS)
        nx = xv * r
        dn = duv * g_ref[...]
        dg_ref[...] += jnp.sum(duv * nx, axis=0, keepdims=True)
        dx_ref[...] = dres_ref[...] + r * (dn - nx * jnp.mean(dn * nx, axis=-1, keepdims=True))

    row = pl.BlockSpec((tt, D), lambda i: (i, 0))
    vec = pl.BlockSpec((1, D), lambda i: (0, 0))
    return pl.pallas_call(
        body, name=name, grid=(t // tt,), in_specs=[row, row, vec, row], out_specs=[row, vec],
        out_shape=[_sds((t, D), F32), _sds((1, D), F32)], compiler_params=_cp("arbitrary"),
    )(du, x, gain, dres)


def _loss_head(x, gain, target):
    t = x.shape[0]
    tt = min(256, t)

    def body(x_ref, g_ref, t_ref, loss_ref, dx_ref, dg_ref):
        @pl.when(pl.program_id(0) == 0)
        def _():
            dg_ref[...] = jnp.zeros_like(dg_ref)
            loss_ref[...] = jnp.zeros_like(loss_ref)

        xv = x_ref[...]
        r = lax.rsqrt(jnp.mean(xv * xv, axis=-1, keepdims=True) + EPS)
        nx = xv * r
        err = nx * g_ref[...] - t_ref[...]
        loss_ref[...] += 0.5 * jnp.sum(jnp.mean(err * err, axis=-1, keepdims=True), axis=0, keepdims=True)
        dy = err * (1.0 / D)
        dn = dy * g_ref[...]
        dg_ref[...] += jnp.sum(dy * nx, axis=0, keepdims=True)
        dx_ref[...] = r * (dn - nx * jnp.mean(dn * nx, axis=-1, keepdims=True))

    row = pl.BlockSpec((tt, D), lambda i: (i, 0))
    vec = pl.BlockSpec((1, D), lambda i: (0, 0))
    one = pl.BlockSpec((1, 1), lambda i: (0, 0))
    return pl.pallas_call(
        body, name="loss_head", grid=(t // tt,), in_specs=[row, vec, row], out_specs=[one, row, vec],
        out_shape=[_sds((1, 1), F32), _sds((t, D), F32), _sds((1, D), F32)], compiler_params=_cp("arbitrary"),
    )(x, gain, target)


def _shift_down(v, s, t_idx):
    return jnp.where(t_idx >= s, pltpu.roll(v, s, 0), 0.0)


def _shift_up(v, s, t_idx, t):
    return jnp.where(t_idx < t - s, pltpu.roll(v, t - s, 0), 0.0)


def _pool_d(p, g, t_idx):
    s = p
    for step in range(g + 1):
        s = s + _shift_down(s, 1 << step, t_idx)
    cnt = jnp.minimum(t_idx + 1, POOL_WINDOWS[g]).astype(F32)
    return s / cnt - p, cnt


def _pool_fwd(proj, pool_w, pool_scale):
    t = proj.shape[0]
    g128 = POOL_W // len(POOL_WINDOWS)

    def body(p_ref, w_ref, s_ref, y_ref):
        t_idx = lax.broadcasted_iota(jnp.int32, (t, g128), 0)
        for g in range(len(POOL_WINDOWS)):
            sl = slice(g * g128, (g + 1) * g128)
            d, _ = _pool_d(p_ref[:, sl], g, t_idx)
            y_ref[:, sl] = (_bdot(d, w_ref[g]) * s_ref[:, sl]).astype(y_ref.dtype)

    return pl.pallas_call(
        body, name="pool_fwd", grid=(1,),
        in_specs=[pl.BlockSpec((t, POOL_W), lambda i: (0, OFF_P // POOL_W)),
                  pl.BlockSpec((4, g128, g128), lambda i: (0, 0, 0)), pl.BlockSpec((1, POOL_W), lambda i: (0, 0))],
        out_specs=pl.BlockSpec((t, POOL_W), lambda i: (0, 0)), out_shape=_sds((t, POOL_W), BF16),
        compiler_params=_cp("arbitrary"),
    )(proj, pool_w, pool_scale)


def _pool_bwd(proj, pool_w, pool_scale, dy):
    t = proj.shape[0]
    g128 = POOL_W // len(POOL_WINDOWS)

    def body(p_ref, w_ref, s_ref, dy_ref, dp_ref, dw_ref, ds_ref):
        t_idx = lax.broadcasted_iota(jnp.int32, (t, g128), 0)
        for g in range(len(POOL_WINDOWS)):
            sl = slice(g * g128, (g + 1) * g128)
            d, cnt = _pool_d(p_ref[:, sl], g, t_idx)
            dyv = dy_ref[:, sl].astype(F32)
            ds_ref[:, sl] = jnp.sum(dyv * _bdot(d, w_ref[g]), axis=0, keepdims=True)
            dys = dyv * s_ref[:, sl]
            dw_ref[g] = _bdot(d, dys, TN)
            dd = _bdot(dys, w_ref[g], NT)
            s = dd / cnt
            for step in range(g + 1):
                s = s + _shift_up(s, 1 << step, t_idx, t)
            dp_ref[:, sl] = (s - dd).astype(dp_ref.dtype)

    return pl.pallas_call(
        body, name="pool_bwd", grid=(1,),
        in_specs=[pl.BlockSpec((t, POOL_W), lambda i: (0, OFF_P // POOL_W)),
                  pl.BlockSpec((4, g128, g128), lambda i: (0, 0, 0)), pl.BlockSpec((1, POOL_W), lambda i: (0, 0)),
                  pl.BlockSpec((t, POOL_W), lambda i: (0, 0))],
        out_specs=[pl.BlockSpec((t, POOL_W), lambda i: (0, 0)), pl.BlockSpec((4, g128, g128), lambda i: (0, 0, 0)),
                   pl.BlockSpec((1, POOL_W), lambda i: (0, 0))],
        out_shape=[_sds((t, POOL_W), BF16), _sds((4, g128, g128), F32), _sds((1, POOL_W), F32)],
        compiler_params=_cp("arbitrary"),
    )(proj, pool_w, pool_scale, dy)


SB_GROUP = 3
SB_GW = SB_GROUP * HD


def _sb_cast_kv(proj):
    t = proj.shape[0]
    tt = min(512, t)

    def body(x_ref, o_ref):
        o_ref[...] = x_ref[...].astype(BF16)

    return pl.pallas_call(
        body, name="sb_cast_kv", grid=(t // tt, 2),
        in_specs=[pl.BlockSpec((tt, SB_W), lambda i, j: (i, OFF_SB // SB_W + 1 + j))],
        out_specs=pl.BlockSpec((tt, SB_W), lambda i, j: (i, j)), out_shape=_sds((t, 2 * SB_W), BF16),
        compiler_params=_cp("parallel", "parallel"),
    )(proj)


def _sb_specs(t):
    q_spec = pl.BlockSpec((SB_BLOCK, SB_GW), lambda g, i: (i, OFF_SB // SB_GW + g))
    k_spec = pl.BlockSpec((t, SB_GW), lambda g, i: (0, g))
    v_spec = pl.BlockSpec((t, SB_GW), lambda g, i: (0, SB_W // SB_GW + g))
    return q_spec, k_spec, v_spec


def _head(ref, h, rows=None):
    cols = slice(h * HD, (h + 1) * HD)
    return ref[:, cols] if rows is None else ref[rows, cols]


SB_KEYS = 512


def _sub(v, b):
    return v[:, b * SB_BLOCK:(b + 1) * SB_BLOCK]


def _sb_keep(kc, limit):
    row = lax.broadcasted_iota(jnp.int32, (SB_BLOCK, kc), 0)
    col = lax.broadcasted_iota(jnp.int32, (SB_BLOCK, kc), 1)
    return col < row + limit


def _sb_chunk(q, keys, run, later, limit):
    kc = keys.shape[0]
    z = _dot(q, keys, NT)
    lsz = jax.nn.log_sigmoid(z)
    ls = lsz - z
    if limit is not None:
        keep = _sb_keep(kc, limit)
        ls = jnp.where(keep, ls, 0.0)
    parts = [None] * (kc // SB_BLOCK)
    for b in reversed(range(kc // SB_BLOCK)):
        parts[b] = _hdot(_sub(ls, b), later) + run
        run = run + jnp.sum(_sub(ls, b), axis=1, keepdims=True)
    a = jnp.exp(lsz + jnp.concatenate(parts, axis=1))
    if limit is not None:
        a = jnp.where(keep, a, 0.0)
    return z, a, run


def _sb_fwd(proj, kv):
    t = proj.shape[0]
    kc = min(SB_KEYS, t)
    scale = HD ** -0.5

    def body(q_ref, k_ref, v_ref, o_ref):
        i = pl.program_id(1)
        top = (i * SB_BLOCK) // kc
        qs = [(_head(q_ref, h) * scale).astype(BF16) for h in range(SB_GROUP)]
        row = lax.broadcasted_iota(jnp.int32, (SB_BLOCK, SB_BLOCK), 0)
        col = lax.broadcasted_iota(jnp.int32, (SB_BLOCK, SB_BLOCK), 1)
        later = (row > col).astype(F32)

        def chunk(jc, carry, masked):
            rows = pl.ds(pl.multiple_of(jc * kc, kc), kc)
            limit = i * SB_BLOCK - jc * kc if masked else None
            out = []
            for h in range(SB_GROUP):
                acc, run = carry[h]
                _, a, run = _sb_chunk(qs[h], _head(k_ref, h, rows), run, later, limit)
                out.append((acc + _dot(a.astype(BF16), _head(v_ref, h, rows)), run))
            return tuple(out)

        zero = tuple((jnp.zeros((SB_BLOCK, HD), F32), jnp.zeros((SB_BLOCK, 1), F32)) for _ in range(SB_GROUP))
        carry = chunk(top, zero, True)
        carry = lax.fori_loop(0, top, lambda jj, c: chunk(top - 1 - jj, c, False), carry)
        for h in range(SB_GROUP):
            o_ref[:, h * HD:(h + 1) * HD] = carry[h][0].astype(o_ref.dtype)

    return pl.pallas_call(
        body, name="sb_fwd", grid=(HEADS // SB_GROUP, t // SB_BLOCK), in_specs=list(_sb_specs(t)),
        out_specs=pl.BlockSpec((SB_BLOCK, SB_GW), lambda g, i: (i, g)), out_shape=_sds((t, SB_W), BF16),
        compiler_params=_cp("parallel", "arbitrary"),
    )(proj, kv, kv)


def _sb_bwd(proj, kv, dy):
    t = proj.shape[0]
    nq = t // SB_BLOCK
    kc = min(SB_KEYS, t)
    scale = HD ** -0.5

    def body(q_ref, k_ref, v_ref, do_ref, dq_ref, dk_ref, dv_ref, z_scr, e_scr):
        i = pl.program_id(1)
        top = (i * SB_BLOCK) // kc

        @pl.when(i == 0)
        def _():
            dk_ref[...] = jnp.zeros_like(dk_ref)
            dv_ref[...] = jnp.zeros_like(dv_ref)

        qs = [(_head(q_ref, h) * scale).astype(BF16) for h in range(SB_GROUP)]
        dos = [_head(do_ref, h).astype(BF16) for h in range(SB_GROUP)]
        row = lax.broadcasted_iota(jnp.int32, (SB_BLOCK, SB_BLOCK), 0)
        col = lax.broadcasted_iota(jnp.int32, (SB_BLOCK, SB_BLOCK), 1)
        later = (row > col).astype(F32)
        earlier = (row < col).astype(F32)

        def down(jc, runs, masked):
            rows = pl.ds(pl.multiple_of(jc * kc, kc), kc)
            limit = i * SB_BLOCK - jc * kc if masked else None
            out = []
            for h in range(SB_GROUP):
                z, a, run = _sb_chunk(qs[h], _head(k_ref, h, rows), runs[h], later, limit)
                z_scr[h, jc] = z
                e_scr[h, jc] = a * _dot(dos[h], _head(v_ref, h, rows), NT)
                dv_ref[rows, h * HD:(h + 1) * HD] += _dot(a.astype(BF16), dos[h], TN)
                out.append(run)
            return tuple(out)

        zero = tuple(jnp.zeros((SB_BLOCK, 1), F32) for _ in range(SB_GROUP))
        runs = down(top, zero, True)
        lax.fori_loop(0, top, lambda jj, r: down(top - 1 - jj, r, False), runs)

        def up(jc, carry, masked):
            rows = pl.ds(pl.multiple_of(jc * kc, kc), kc)
            out = []
            for h in range(SB_GROUP):
                dq, run = carry[h]
                z, e = z_scr[h, jc], e_scr[h, jc]
                parts = []
                for b in range(kc // SB_BLOCK):
                    parts.append(_hdot(_sub(e, b), earlier) + run)
                    run = run + jnp.sum(_sub(e, b), axis=1, keepdims=True)
                sz = jax.nn.sigmoid(z)
                dz = e * (1.0 - sz) - jnp.concatenate(parts, axis=1) * sz
                if masked:
                    dz = jnp.where(_sb_keep(kc, i * SB_BLOCK - jc * kc), dz, 0.0)
                dz = dz.astype(BF16)
                dk_ref[rows, h * HD:(h + 1) * HD] += _dot(dz, qs[h], TN)
                out.append((dq + _dot(dz, _head(k_ref, h, rows)), run))
            return tuple(out)

        zero = tuple((jnp.zeros((SB_BLOCK, HD), F32), jnp.zeros((SB_BLOCK, 1), F32)) for _ in range(SB_GROUP))
        carry = lax.fori_loop(0, top, lambda jc, c: up(jc, c, False), zero)
        carry = up(top, carry, True)
        for h in range(SB_GROUP):
            dq_ref[:, h * HD:(h + 1) * HD] = (carry[h][0] * scale).astype(dq_ref.dtype)

    blk = pl.BlockSpec((SB_BLOCK, SB_GW), lambda g, i: (i, g))
    seq = pl.BlockSpec((t, SB_GW), lambda g, i: (0, g))
    scratch = pltpu.VMEM((SB_GROUP, t // kc, SB_BLOCK, kc), F32)
    return pl.pallas_call(
        body, name="sb_bwd", grid=(HEADS // SB_GROUP, nq), in_specs=[*_sb_specs(t), blk], out_specs=[blk, seq, seq],
        out_shape=[_sds((t, SB_W), BF16), _sds((t, SB_W), F32), _sds((t, SB_W), F32)],
        scratch_shapes=[scratch, scratch], compiler_params=_cp("parallel", "arbitrary"),
    )(proj, kv, kv, dy)


CONV_TILE = 256
GDN_CONV = 4


def _conv_pre(x, w_ref, t_idx):
    pre = w_ref[GDN_CONV - 1:GDN_CONV, :] * x
    for s in range(1, GDN_CONV):
        pre = pre + w_ref[GDN_CONV - 1 - s:GDN_CONV - s, :] * _shift_down(x, s, t_idx)
    return pre


def _conv_fwd(proj, conv_w):
    t = proj.shape[0]
    width = conv_w.shape[1]

    def body(x_ref, w_ref, y_ref):
        t_idx = lax.broadcasted_iota(jnp.int32, (t, CONV_TILE), 0)
        pre = _conv_pre(x_ref[...], w_ref, t_idx)
        y_ref[...] = pre * jax.nn.sigmoid(pre)

    return pl.pallas_call(
        body, name="conv_fwd", grid=(width // CONV_TILE,),
        in_specs=[pl.BlockSpec((t, CONV_TILE), lambda c: (0, OFF_GQKV // CONV_TILE + c)),
                  pl.BlockSpec((GDN_CONV, CONV_TILE), lambda c: (0, c))],
        out_specs=pl.BlockSpec((t, CONV_TILE), lambda c: (0, c)), out_shape=_sds((t, width), F32),
        compiler_params=_cp("parallel"),
    )(proj, conv_w)


def _conv_bwd(proj, conv_w, dc):
    t = proj.shape[0]
    width = dc.shape[1]
    per = width // CONV_TILE
    part = 0

    def body(x_ref, w_ref, dc_ref, dx_ref, dw_ref):
        t_idx = lax.broadcasted_iota(jnp.int32, (t, CONV_TILE), 0)
        x = x_ref[...]
        pre = _conv_pre(x, w_ref, t_idx)
        sg = jax.nn.sigmoid(pre)
        dpre = dc_ref[...] * (sg * (1.0 + pre * (1.0 - sg)))
        dx = w_ref[GDN_CONV - 1:GDN_CONV, :] * dpre
        dw_ref[GDN_CONV - 1:GDN_CONV, :] = jnp.sum(dpre * x, axis=0, keepdims=True)
        for s in range(1, GDN_CONV):
            dx = dx + w_ref[GDN_CONV - 1 - s:GDN_CONV - s, :] * _shift_up(dpre, s, t_idx, t)
            dw_ref[GDN_CONV - 1 - s:GDN_CONV - s, :] = jnp.sum(dpre * _shift_down(x, s, t_idx), axis=0, keepdims=True)
        dx_ref[...] = dx.astype(dx_ref.dtype)

    return pl.pallas_call(
        body, name="conv_bwd", grid=(per,),
        in_specs=[pl.BlockSpec((t, CONV_TILE), lambda c: (0, OFF_GQKV // CONV_TILE + part * per + c)),
                  pl.BlockSpec((GDN_CONV, CONV_TILE), lambda c: (0, part * per + c)),
                  pl.BlockSpec((t, CONV_TILE), lambda c: (0, c))],
        out_specs=[pl.BlockSpec((t, CONV_TILE), lambda c: (0, c)), pl.BlockSpec((GDN_CONV, CONV_TILE), lambda c: (0, c))],
        out_shape=[_sds((t, width), BF16), _sds((GDN_CONV, width), F32)],
        compiler_params=_cp("parallel"),
    )(proj, conv_w, dc)


def _heads(x):
    return jnp.concatenate([x[:, h * HD:(h + 1) * HD][None] for h in range(HEADS)], axis=0)


def _hb(a, b, ca=2, cb=1):
    return lax.dot_general(a, b, (((ca,), (cb,)), ((0,), (0,))), precision=HIGH, preferred_element_type=F32)


@jax.custom_vjp
def _unit_lower_inverse(lower):
    c = lower.shape[-1]
    eye = lax.broadcasted_iota(jnp.int32, (c, c), 0) == lax.broadcasted_iota(jnp.int32, (c, c), 1)
    inv = jnp.where(eye, 1.0, 0.0) - lower
    pw = _hb(lower, lower)
    for step in range(5):
        inv = inv + _hb(inv, pw)
        if step < 4:
            pw = _hb(pw, pw)
    return inv


def _unit_lower_inverse_fwd(lower):
    inv = _unit_lower_inverse(lower)
    return inv, inv


def _unit_lower_inverse_bwd(inv, d_inv):
    return (-_hb(_hb(inv, d_inv, 1, 1), inv, 2, 2),)


_unit_lower_inverse.defvjp(_unit_lower_inverse_fwd, _unit_lower_inverse_bwd)


@jax.custom_vjp
def _known_inverse(lower, inv):
    return inv


_known_inverse.defvjp(lambda lower, inv: (inv, inv),
                      lambda inv, d_inv: (*_unit_lower_inverse_bwd(inv, d_inv), jnp.zeros_like(inv)))


def _gdn_prep(cq, ck, cv, ab, alog_row, dtb_row, inv=None, keep_inverse=False):
    c = GDN_CHUNK
    row = lax.broadcasted_iota(jnp.int32, (c, c), 0)
    col = lax.broadcasted_iota(jnp.int32, (c, c), 1)
    incl, strict, eye = row >= col, row > col, row == col
    def lanes(v, first):
        return jnp.concatenate([v[:, first + h:first + h + 1][None] for h in range(HEADS)], axis=0)

    a_col, b_col = lanes(ab, 0), lanes(ab, HEADS)
    a_log, dt_bias = lanes(alog_row, 0), lanes(dtb_row, 0)
    qn = cq * lax.rsqrt(jnp.sum(cq * cq, axis=-1, keepdims=True) + EPS) * (HD ** -0.5)
    kn = ck * lax.rsqrt(jnp.sum(ck * ck, axis=-1, keepdims=True) + EPS)
    la_col = -jnp.exp(a_log) * jax.nn.softplus(a_col + dt_bias)
    beta = jax.nn.sigmoid(b_col)
    la_row = jnp.sum(jnp.where(eye, la_col, 0.0), axis=1, keepdims=True)
    g_col = jnp.sum(jnp.where(incl, la_row, 0.0), axis=2, keepdims=True)
    g_row = jnp.sum(jnp.where(row <= col, la_col, 0.0), axis=1, keepdims=True)
    g_last = jnp.sum(la_col, axis=1, keepdims=True)
    gamma = jnp.where(incl, jnp.exp(jnp.where(incl, g_col - g_row, 0.0)), 0.0)
    lower = jnp.where(strict, beta * _hb(kn, kn, 2, 2) * gamma, 0.0)
    inv = _unit_lower_inverse(lower) if inv is None else _known_inverse(lower, inv)
    u = _hb(inv, cv * beta)
    w = _hb(inv, kn * (beta * jnp.exp(g_col)))
    qk = _hb(qn, kn, 2, 2) * gamma
    out = (u, w, qk, qn * jnp.exp(g_col), kn * jnp.exp(g_last - g_col), jnp.exp(g_last))
    return (*out, inv) if keep_inverse else out


def _gdn_post(o, z, gain):
    y = o * lax.rsqrt(jnp.mean(o * o, axis=-1, keepdims=True) + EPS) * gain
    return y * (z * jax.nn.sigmoid(z))


def _gdn_specs(nc, reverse):
    c = GDN_CHUNK

    def ch(n):
        return nc - 1 - n if reverse else n

    def wide(array_off):
        return pl.BlockSpec((c, GDN_W), lambda n: (ch(n), array_off // GDN_W))

    ab = pl.BlockSpec((c, HD), lambda n: (ch(n), OFF_AB // HD))
    row = pl.BlockSpec((1, HD), lambda n: (0, 0))
    state = pl.BlockSpec((None, HEADS, HD, HD), lambda n: (ch(n), 0, 0, 0))
    inverse = pl.BlockSpec((None, HEADS, c, c), lambda n: (ch(n), 0, 0, 0))
    return wide, ab, row, state, inverse


def _gdn_fwd(cqkv, proj, a_log, dt_bias, gain):
    t = proj.shape[0]
    nc = t // GDN_CHUNK
    wide, ab, row, state, inverse = _gdn_specs(nc, False)

    def body(cq_ref, ck_ref, cv_ref, ab_ref, z_ref, al_ref, dt_ref, g_ref, y_ref, sprev_ref, inv_ref, s_scr):
        @pl.when(pl.program_id(0) == 0)
        def _():
            s_scr[...] = jnp.zeros_like(s_scr)

        u, w, qk, qd, kd, dec, inv = _gdn_prep(_heads(cq_ref[...]), _heads(ck_ref[...]), _heads(cv_ref[...]), ab_ref[...],
                                               al_ref[...], dt_ref[...], keep_inverse=True)
        inv_ref[...] = inv
        s = s_scr[...]
        sprev_ref[...] = s
        v_new = u - _hb(w, s)
        o = _hb(qd, s) + _hb(qk, v_new)
        s_scr[...] = s * dec + _hb(kd, v_new, 1, 1)
        y = _gdn_post(o, _heads(z_ref[...]), g_ref[...])
        for h in range(HEADS):
            y_ref[:, h * HD:(h + 1) * HD] = y[h].astype(y_ref.dtype)

    return pl.pallas_call(
        body, name="gdn_fwd", grid=(nc,),
        in_specs=[wide(0), wide(GDN_W), wide(2 * GDN_W), ab, wide(OFF_Z), row, row, row],
        out_specs=[wide(0), state, inverse],
        out_shape=[_sds((t, GDN_W), BF16), _sds((nc, HEADS, HD, HD), F32), _sds((nc, HEADS, GDN_CHUNK, GDN_CHUNK), F32)],
        scratch_shapes=[pltpu.VMEM((HEADS, HD, HD), F32)], compiler_params=_cp("arbitrary"),
    )(cqkv, cqkv, cqkv, proj, proj, a_log, dt_bias, gain)


def _gdn_bwd(cqkv, proj, a_log, dt_bias, gain, sprev, inverses, dy):
    t = proj.shape[0]
    nc = t // GDN_CHUNK
    wide, ab, row, state, inverse = _gdn_specs(nc, True)

    def body(cq_ref, ck_ref, cv_ref, ab_ref, z_ref, al_ref, dt_ref, g_ref, sp_ref, inv_ref, dy_ref,
             dc_ref, dab_ref, dz_ref, dal_ref, ddt_ref, dg_ref, ds_scr):
        @pl.when(pl.program_id(0) == 0)
        def _():
            ds_scr[...] = jnp.zeros_like(ds_scr)
            dal_ref[...] = jnp.zeros_like(dal_ref)
            ddt_ref[...] = jnp.zeros_like(ddt_ref)
            dg_ref[...] = jnp.zeros_like(dg_ref)

        (u, w, qk, qd, kd, dec), prep_vjp = jax.vjp(
            functools.partial(_gdn_prep, inv=inv_ref[...]),
            _heads(cq_ref[...]), _heads(ck_ref[...]), _heads(cv_ref[...]), ab_ref[...], al_ref[...], dt_ref[...])
        s = sp_ref[...]
        v_new = u - _hb(w, s)
        o = _hb(qd, s) + _hb(qk, v_new)
        _, post_vjp = jax.vjp(_gdn_post, o, _heads(z_ref[...]), g_ref[...])
        do, dz, dgain = post_vjp(_heads(dy_ref[...]).astype(F32))
        ds_next = ds_scr[...]
        d_vnew = _hb(qk, do, 1, 1) + _hb(kd, ds_next)
        d_qk = _hb(do, v_new, 2, 2)
        d_qd = _hb(do, s, 2, 2)
        d_kd = _hb(v_new, ds_next, 2, 2)
        d_dec = jnp.sum(jnp.sum(s * ds_next, axis=2, keepdims=True), axis=1, keepdims=True)
        ds_scr[...] = dec * ds_next + _hb(qd, do, 1, 1) - _hb(w, d_vnew, 1, 1)
        d_w = -_hb(d_vnew, s, 2, 2)
        dcq, dck, dcv, dab, dal, ddt = prep_vjp((d_vnew, d_w, d_qk, d_qd, d_kd, d_dec))
        for h in range(HEADS):
            dc_ref[:, h * HD:(h + 1) * HD] = dcq[h]
            dc_ref[:, GDN_W + h * HD:GDN_W + (h + 1) * HD] = dck[h]
            dc_ref[:, 2 * GDN_W + h * HD:2 * GDN_W + (h + 1) * HD] = dcv[h]
            dz_ref[:, h * HD:(h + 1) * HD] = dz[h].astype(dz_ref.dtype)
        dab_ref[...] = dab
        dal_ref[...] += dal
        ddt_ref[...] += ddt
        dg_ref[...] += dgain

    c = GDN_CHUNK
    return pl.pallas_call(
        body, name="gdn_bwd", grid=(nc,),
        in_specs=[wide(0), wide(GDN_W), wide(2 * GDN_W), ab, wide(OFF_Z), row, row, row, state, inverse, wide(0)],
        out_specs=[pl.BlockSpec((c, 3 * GDN_W), lambda n: (nc - 1 - n, 0)), pl.BlockSpec((c, HD), lambda n: (nc - 1 - n, 0)),
                   wide(0), row, row, row],
        out_shape=[_sds((t, 3 * GDN_W), F32), _sds((t, HD), F32), _sds((t, GDN_W), BF16),
                   _sds((1, HD), F32), _sds((1, HD), F32), _sds((1, HD), F32)],
        scratch_shapes=[pltpu.VMEM((HEADS, HD, HD), F32)], compiler_params=_cp("arbitrary"),
    )(cqkv, cqkv, cqkv, proj, proj, a_log, dt_bias, gain, sprev, inverses, dy)


MERGE_TN = 512


def _merge_specs(t, tm):
    tn = MERGE_TN
    ys = [pl.BlockSpec((tm, wd), lambda i, j: (i, 0)) for wd in (POOL_W, SB_W, GDN_W)]
    ws = [pl.BlockSpec((None, wd, tn), lambda i, j: (j, 0, 0)) for wd in (POOL_W, SB_W, GDN_W)]
    gs = [pl.BlockSpec((tm, tn), functools.partial(lambda i, j, b: (i, OFF_GATE // tn + b * (D // tn) + j), b=b))
          for b in range(3)]
    out = pl.BlockSpec((tm, tn), lambda i, j: (i, j))
    return ys, ws, gs, out


def _merge_fwd(ys, wups, proj):
    t = proj.shape[0]
    tm = min(512, t)
    y_specs, w_specs, g_specs, out = _merge_specs(t, tm)

    def body(y0, y1, y2, w0, w1, w2, g0, g1, g2, o_ref):
        acc = jnp.zeros(o_ref.shape, F32)
        for y, w, g in ((y0, w0, g0), (y1, w1, g1), (y2, w2, g2)):
            acc = acc + jax.nn.sigmoid(g[...]) * _dot(y[...], w[...])
        o_ref[...] = acc.astype(o_ref.dtype)

    return pl.pallas_call(
        body, name="merge_fwd", grid=(t // tm, D // MERGE_TN), in_specs=[*y_specs, *w_specs, *g_specs],
        out_specs=out, out_shape=_sds((t, D), BF16), compiler_params=_cp("parallel", "parallel"),
    )(*ys, *wups, proj, proj, proj)


def _merge_bwd(ys, wups, proj, dmerged):
    t = proj.shape[0]
    tm = min(512, t)
    y_specs, w_specs, g_specs, out = _merge_specs(t, tm)

    def body(y0, y1, y2, w0, w1, w2, g0, g1, g2, dm_ref, dg0, dg1, dg2, dm0, dm1, dm2):
        dm = dm_ref[...].astype(F32)
        for y, w, g, dg, dmb in ((y0, w0, g0, dg0, dm0), (y1, w1, g1, dg1, dm1), (y2, w2, g2, dg2, dm2)):
            sg = jax.nn.sigmoid(g[...])
            dg[...] = (dm * _dot(y[...], w[...]) * sg * (1.0 - sg)).astype(dg.dtype)
            dmb[...] = (dm * sg).astype(dmb.dtype)

    return pl.pallas_call(
        body, name="merge_bwd", grid=(t // tm, D // MERGE_TN), in_specs=[*y_specs, *w_specs, *g_specs, out],
        out_specs=[out] * 6, out_shape=[_sds((t, D), BF16)] * 6, compiler_params=_cp("parallel", "parallel"),
    )(*ys, *wups, proj, proj, proj, dmerged)


def _tile(t, want):
    return min(t, want)


def _layer_fwd(x, l, gw, w_al, sp, reached=None):
    t = x.shape[0]
    tm = _tile(t, 1024)
    u = _rms_fwd("rms_attn", x, sp["attn_norm"][l])
    proj = _mm("proj", u, w_al, m=t, n=N_AL, k=D, tm=tm, tn=1024, tk=D, a_spec=_a_plain(tm, D),
               b_spec=_b_plain(D, 1024), dims=None, out_shapes=[_sds((t, N_AL), F32)], out_specs=[_o_plain(tm, 1024)])[0]
    if reached is not None:
        reached("proj", proj)
    y_pool = _pool_fwd(proj, sp["pool_w"][l], sp["pool_scale"][l])
    kv = _sb_cast_kv(proj)
    y_sb = _sb_fwd(proj, kv)
    cqkv = _conv_fwd(proj, sp["conv"][l])
    y_gdn, sprev, inverses = _gdn_fwd(cqkv, proj, sp["a_log"][l], sp["dt_bias"][l], sp["gdn_norm"][l])
    ys = (y_pool, y_sb, y_gdn)
    wups = (gw["w_pool_up"], gw["w_sb_up"], gw["w_gdn_up"])
    merged = _merge_fwd(ys, wups, proj)
    if reached is not None:
        reached("merged", merged)
    x1 = _mm("out_proj", merged, gw["w_out"], m=t, n=D, k=D, tm=tm, tn=1024, tk=512, a_spec=_a_plain(tm, 512),
             b_spec=_w_rows(512, 1024, 512), dims=None, out_shapes=[_sds((t, D), F32)], out_specs=[_o_plain(tm, 1024)],
             extras=[x], extra_specs=[_o_plain(tm, 1024)], epilogue=lambda r, xr: (r + xr,))[0]
    u2 = _rms_fwd("rms_mlp", x1, sp["mlp_norm"][l])

    def relu2(r):
        hv = jnp.maximum(r, 0.0)
        return hv, hv * hv

    hid, hid2 = _mm("ff1", u2, gw["w_ff1"], m=t, n=D_FF, k=D, tm=tm, tn=1024, tk=D, a_spec=_a_plain(tm, D),
                    b_spec=_w_cols(D, 1024, 2048), dims=None, out_shapes=[_sds((t, D_FF), BF16)] * 2,
                    out_specs=[_o_plain(tm, 1024)] * 2, epilogue=relu2)
    x2 = _mm("ff2", hid2, gw["w_ff2"], m=t, n=D, k=D_FF, tm=tm, tn=1024, tk=2048, a_spec=_a_plain(tm, 2048),
             b_spec=_w_rows(2048, 1024, 2048), dims=None, out_shapes=[_sds((t, D), F32)], out_specs=[_o_plain(tm, 1024)],
             extras=[x1], extra_specs=[_o_plain(tm, 1024)], epilogue=lambda r, xr: (r + xr,))[0]
    saved = dict(x=x, u=u, proj=proj, kv=kv, cqkv=cqkv, sprev=sprev, inverses=inverses, ys=ys, merged=merged, x1=x1, u2=u2, hid=hid, hid2=hid2)
    return x2, saved


def _layer_bwd(dx2, l, gw, w_al, sp, sv, emit=None):
    t = dx2.shape[0]
    tm = _tile(t, 1024)
    tk = t
    g = {}
    if emit is None:
        emit = lambda names, grads, v: v
    dpre = _mm("ff2_dx", dx2, gw["w_ff2"], m=t, n=D_FF, k=D, tm=tm, tn=1024, tk=D, a_spec=_a_plain(tm, D),
               b_spec=_w_rows_t(D, 1024, 2048), dims=NT, out_shapes=[_sds((t, D_FF), BF16)],
               out_specs=[_o_plain(tm, 1024)], extras=[sv["hid"]], extra_specs=[_o_plain(tm, 1024)],
               epilogue=lambda r, hv: (r * (2.0 * hv.astype(F32)),))[0]
    g["w_ff2"] = _mm("ff2_dw", sv["hid2"], dx2, m=D_FF, n=D, k=t, tm=1024, tn=1024, tk=tk, a_spec=_a_trans(1024, tk),
                     b_spec=_b_plain(tk, 1024), dims=TN, out_shapes=[_sds((D_FF, D), BF16)],
                     out_specs=[_o_plain(1024, 1024)])[0].reshape(N_CHIPS, D_FF // N_CHIPS, D)
    du2 = _mm("ff1_dx", dpre, gw["w_ff1"], m=t, n=D, k=D_FF, tm=tm, tn=1024, tk=2048, a_spec=_a_plain(tm, 2048),
              b_spec=_w_cols_t(2048, 1024, 2048), dims=NT, out_shapes=[_sds((t, D), F32)], out_specs=[_o_plain(tm, 1024)])[0]
    g["w_ff1"] = _mm("ff1_dw", sv["u2"], dpre, m=D, n=D_FF, k=t, tm=1024, tn=1024, tk=tk, a_spec=_a_trans(1024, tk),
                     b_spec=_b_plain(tk, 1024), dims=TN, out_shapes=[_sds((N_CHIPS, D, D_FF // N_CHIPS), BF16)],
                     out_specs=[_o_colshard(1024, 1024, D_FF // N_CHIPS)])[0]
    dx1, g["mlp_norm"] = _rms_bwd("rms_mlp_bwd", du2, sv["x1"], sp["mlp_norm"][l], dx2)
    dx1 = emit(("w_ff1", "w_ff2"), g, dx1)
    dmerged = _mm("out_dx", dx1, gw["w_out"], m=t, n=D, k=D, tm=tm, tn=512, tk=D, a_spec=_a_plain(tm, D),
                  b_spec=_w_rows_t(D, 512, 512), dims=NT, out_shapes=[_sds((t, D), BF16)], out_specs=[_o_plain(tm, 512)])[0]
    g["w_out"] = _mm("out_dw", sv["merged"], dx1, m=D, n=D, k=t, tm=1024, tn=1024, tk=tk, a_spec=_a_trans(1024, tk),
                     b_spec=_b_plain(tk, 1024), dims=TN, out_shapes=[_sds((D, D), BF16)],
                     out_specs=[_o_plain(1024, 1024)])[0].reshape(N_CHIPS, D // N_CHIPS, D)
    wups = (gw["w_pool_up"], gw["w_sb_up"], gw["w_gdn_up"])
    dg0, dg1, dg2, dm0, dm1, dm2 = _merge_bwd(sv["ys"], wups, sv["proj"], dmerged)
    dys = []
    for nm, yb, dmb, wd in zip(("w_pool_up", "w_sb_up", "w_gdn_up"), sv["ys"], (dm0, dm1, dm2), (POOL_W, SB_W, GDN_W)):
        dys.append(_mm(nm + "_dx", dmb, gw[nm], m=t, n=wd, k=D, tm=tm, tn=256, tk=512, a_spec=_a_plain(tm, 512),
                       b_spec=_w_cols_t(512, 256, 512), dims=NT, out_shapes=[_sds((t, wd), F32)],
                       out_specs=[_o_plain(tm, 256)])[0])
        g[nm] = _mm(nm + "_dw", yb, dmb, m=wd, n=D, k=t, tm=256, tn=512, tk=tk, a_spec=_a_trans(256, tk),
                    b_spec=_b_plain(tk, 512), dims=TN, out_shapes=[_sds((N_CHIPS, wd, D // N_CHIPS), BF16)],
                    out_specs=[_o_colshard(256, 512, D // N_CHIPS)])[0]
    dys[2] = emit(("w_pool_up", "w_sb_up", "w_gdn_up", "w_out"), g, dys[2])
    proj = sv["proj"]
    dp, g["pool_w"], g["pool_scale"] = _pool_bwd(proj, sp["pool_w"][l], sp["pool_scale"][l], dys[0])
    dsq, dsk, dsv = _sb_bwd(proj, sv["kv"], dys[1])
    dc, dab, dz, g["a_log"], g["dt_bias"], g["gdn_norm"] = _gdn_bwd(
        sv["cqkv"], proj, sp["a_log"][l], sp["dt_bias"][l], sp["gdn_norm"][l], sv["sprev"], sv["inverses"], dys[2])
    dgx, g["conv"] = _conv_bwd(proj, sp["conv"][l], dc)
    dproj = jnp.concatenate(
        [dsq, dsk.astype(BF16), dsv.astype(BF16), dgx, dz, dab.astype(BF16), jnp.zeros((t, AB_W - HD), BF16),
         dp, dg0, dg1, dg2], axis=1)
    du = _mm("proj_dx", dproj, w_al, m=t, n=D, k=N_AL, tm=tm, tn=1024, tk=2048, a_spec=_a_plain(tm, 2048),
             b_spec=_b_trans(2048, 1024), dims=NT, out_shapes=[_sds((t, D), F32)], out_specs=[_o_plain(tm, 1024)])[0]
    g["w_al"] = _mm("proj_dw", sv["u"], dproj, m=D, n=N_AL, k=t, tm=1024, tn=1024, tk=tk, a_spec=_a_trans(1024, tk),
                    b_spec=_b_plain(tk, 1024), dims=TN, out_shapes=[_sds((D, N_AL), BF16)], out_specs=[_o_plain(1024, 1024)])[0]
    dx, g["attn_norm"] = _rms_bwd("rms_attn_bwd", du, sv["x"], sp["attn_norm"][l], dx1)
    g["w_in"] = _w_in_to_shards(g["w_al"])
    dx = emit(("w_in",), g, dx)
    return dx, g


def _align_w_in(w):
    n_ab = ORIG_GATE - ORIG_AB
    return jnp.concatenate([w[:, ORIG_SB:ORIG_GATE], jnp.zeros((D, AB_W - n_ab), w.dtype), w[:, :ORIG_SB], w[:, ORIG_GATE:]],
                           axis=1)


def _unalign_w_in(w):
    n_ab = ORIG_GATE - ORIG_AB
    return jnp.concatenate([w[:, OFF_P:OFF_GATE], w[:, :OFF_AB + n_ab], w[:, OFF_GATE:]], axis=1)


W_IN_RUNS = ((0, ORIG_SB, OFF_P), (ORIG_SB, ORIG_GATE, OFF_SB), (ORIG_GATE, N_IN, OFF_GATE))
W_IN_SHARD = N_IN // N_CHIPS


def _w_in_from_shards(gathered):
    parts = []
    for lo, hi, al in sorted(W_IN_RUNS, key=lambda r: r[2]):
        if al == OFF_P:
            parts.append(jnp.zeros((D, OFF_P - (OFF_AB + ORIG_GATE - ORIG_AB)), gathered.dtype))
        while lo < hi:
            chip = lo // W_IN_SHARD
            end = min(hi, (chip + 1) * W_IN_SHARD)
            parts.append(gathered[chip, :, lo - chip * W_IN_SHARD:end - chip * W_IN_SHARD])
            lo = end
    return jnp.concatenate(parts, axis=1)


def _w_in_to_shards(g_al):
    shards = []
    for chip in range(N_CHIPS):
        a, b = chip * W_IN_SHARD, (chip + 1) * W_IN_SHARD
        parts = [g_al[:, al + max(a, lo) - lo:al + min(b, hi) - lo] for lo, hi, al in W_IN_RUNS if max(a, lo) < min(b, hi)]
        shards.append(jnp.concatenate(parts, axis=1))
    return jnp.stack(shards)


def _row128(v):
    return jnp.pad(v.reshape(1, -1), ((0, 0), (0, HD - v.shape[-1])))


def _local_step(x, target, weights_of, sp, emit=None, reached=None):
    saved, gw, w_in_al = [], [], []
    h = x
    for l in range(2):
        gw_l, w_al_l = weights_of(l)
        gw.append(gw_l)
        w_in_al.append(w_al_l)
        h, sv = _layer_fwd(h, l, gw_l, w_al_l, sp, None if reached is None else functools.partial(reached, l))
        saved.append(sv)
    loss, dh, g_final = _loss_head(h, sp["final_norm"], target)
    grads = [None, None]
    for l in (1, 0):
        dh, grads[l] = _layer_bwd(dh, l, gw[l], w_in_al[l], sp, saved[l],
                                  None if emit is None else functools.partial(emit, l))
    return loss, dh, grads, g_final


ANY = pl.BlockSpec(memory_space=pl.ANY)


def _me():
    return lax.axis_index("x"), lax.axis_index("y"), lax.axis_index("c")


def _other_chips(x, y):
    return [(1 - x, y), (x, 1 - y), (1 - x, 1 - y)]


def _half(ref, axis, c, rows):
    half = rows // 2
    idx = [slice(None)] * axis + [pl.ds(pl.multiple_of(c * half, 16), half)]
    return ref.at[tuple(idx)]


def _gather_steps(out, send, recv):
    n = len(out)
    x, y, c = _me()
    mine = 2 * x + y
    sibling = (x, y, 1 - c)
    chips = _other_chips(x, y)
    sends = []
    for t in range(n):
        rows = out[t].shape[1]
        for k, (px, py) in enumerate(chips):
            own_half = _half(out[t].at[mine], 0, c, rows)
            cp = pltpu.make_async_remote_copy(
                src_ref=own_half, dst_ref=own_half,
                send_sem=send.at[6 * t + k], recv_sem=recv.at[6 * t + k], device_id=(px, py, c), device_id_type=MESH)
            cp.start()
            sends.append(cp)
    for t in range(n):
        rows = out[t].shape[1]
        for k, (px, py) in enumerate(chips):
            landed = _half(out[t].at[2 * px + py], 0, c, rows)
            pltpu.make_async_remote_copy(
                src_ref=landed, dst_ref=landed, send_sem=send.at[6 * t + k], recv_sem=recv.at[6 * t + k],
                device_id=(px, py, c), device_id_type=MESH).wait_recv()
            cp = pltpu.make_async_remote_copy(
                src_ref=landed, dst_ref=landed, send_sem=send.at[6 * t + 3 + k], recv_sem=recv.at[6 * t + 3 + k],
                device_id=sibling, device_id_type=MESH)
            cp.start()
            sends.append(cp)
    for t in range(n):
        rows = out[t].shape[1]
        for k, (px, py) in enumerate(chips):
            other = _half(out[t].at[2 * px + py], 0, 1 - c, rows)
            pltpu.make_async_remote_copy(
                src_ref=other, dst_ref=other, send_sem=send.at[6 * t + 3 + k], recv_sem=recv.at[6 * t + 3 + k],
                device_id=sibling, device_id_type=MESH).wait_recv()
    for cp in sends:
        cp.wait_send()


def _gather_weights(bufs):
    n = len(bufs)

    def body(*refs):
        _gather_steps(refs[n:2 * n], *refs[2 * n:])

    return pl.pallas_call(
        body, name="gather_weights", in_specs=[ANY] * n, out_specs=[ANY] * n,
        out_shape=[_sds(s.shape, s.dtype) for s in bufs], input_output_aliases={t: t for t in range(n)},
        scratch_shapes=[pltpu.SemaphoreType.DMA((6 * n,)), pltpu.SemaphoreType.DMA((6 * n,))],
    )(*bufs)


def _gather_weights_async(bufs, tag, collective_id):
    n = len(bufs)
    refs = [jax.new_ref(b, memory_space=pltpu.MemorySpace.HBM) for b in bufs]

    @pl.kernel(mesh=plsc.ScalarSubcoreMesh(axis_name="sequencer", num_cores=1), name=f"gather_async_{tag}",
               scratch_types=(pltpu.SemaphoreType.DMA((6 * n,)), pltpu.SemaphoreType.DMA((6 * n,))),
               compiler_params=pltpu.CompilerParams(collective_id=collective_id))
    def launch(send, recv):
        x, y, c = _me()
        barrier = pltpu.get_barrier_semaphore()
        peers = [(x, y, 1 - c)] + [(px, py, c) for px, py in _other_chips(x, y)]
        for peer in peers:
            pl.semaphore_signal(barrier, inc=1, device_id=peer, device_id_type=MESH)
        pl.semaphore_wait(barrier, len(peers))
        _gather_steps(refs, send, recv)

    launch()
    return [r[...] for r in refs]


def _rs_pair(grads):
    n = len(grads)

    def body(*refs):
        g, out = refs[:n], refs[n:2 * n]
        send, recv = refs[2 * n:]
        x, y, c = _me()
        copies = []
        for t in range(n):
            cp = pltpu.make_async_remote_copy(
                src_ref=_half(g[t], 1, 1 - c, g[t].shape[1]), dst_ref=out[t], send_sem=send.at[t], recv_sem=recv.at[t],
                device_id=(x, y, 1 - c), device_id_type=MESH)
            cp.start()
            copies.append(cp)
        for cp in copies:
            cp.wait()

    return pl.pallas_call(
        body, name="rs_pair", in_specs=[ANY] * n, out_specs=[ANY] * n,
        out_shape=[_sds((N_CHIPS, s.shape[1] // 2, s.shape[2]), s.dtype) for s in grads],
        scratch_shapes=[pltpu.SemaphoreType.DMA((n,)), pltpu.SemaphoreType.DMA((n,))],
    )(*grads)


def _rs_chips_steps(p, out, send, recv):
    x, y, c = _me()
    copies = []
    for t in range(len(p)):
        for k, (px, py) in enumerate(_other_chips(x, y)):
            cp = pltpu.make_async_remote_copy(
                src_ref=p[t].at[2 * px + py], dst_ref=out[t].at[k], send_sem=send.at[3 * t + k],
                recv_sem=recv.at[3 * t + k], device_id=(px, py, c), device_id_type=MESH)
            cp.start()
            copies.append(cp)
    for cp in copies:
        cp.wait()


def _rs_chips_async(parts, tag, collective_id):
    n = len(parts)
    src = [jax.new_ref(p, memory_space=pltpu.MemorySpace.HBM) for p in parts]
    got = [jax.empty_ref(_sds((3, *p.shape[1:]), p.dtype), memory_space=pltpu.MemorySpace.HBM) for p in parts]

    @pl.kernel(mesh=plsc.ScalarSubcoreMesh(axis_name="sequencer", num_cores=1), name=f"rs_chips_async_{tag}",
               scratch_types=(pltpu.SemaphoreType.DMA((3 * n,)), pltpu.SemaphoreType.DMA((3 * n,))),
               compiler_params=pltpu.CompilerParams(collective_id=collective_id))
    def launch(send, recv):
        x, y, c = _me()
        barrier = pltpu.get_barrier_semaphore()
        peers = [(px, py, c) for px, py in _other_chips(x, y)]
        for peer in peers:
            pl.semaphore_signal(barrier, inc=1, device_id=peer, device_id_type=MESH)
        pl.semaphore_wait(barrier, len(peers))
        _rs_chips_steps(src, got, send, recv)

    launch()
    return [g[...] for g in got]


def _rs_chips(parts):
    n = len(parts)

    def body(*refs):
        _rs_chips_steps(refs[:n], refs[n:2 * n], *refs[2 * n:])

    return pl.pallas_call(
        body, name="rs_chips", in_specs=[ANY] * n, out_specs=[ANY] * n,
        out_shape=[_sds((3, *s.shape[1:]), s.dtype) for s in parts],
        scratch_shapes=[pltpu.SemaphoreType.DMA((3 * n,)), pltpu.SemaphoreType.DMA((3 * n,))],
    )(*parts)


def _pair_exchange(bufs):
    n = len(bufs)

    def body(*refs):
        out = refs[n:2 * n]
        send, recv = refs[2 * n:]
        x, y, c = _me()
        copies = []
        for t in range(n):
            cp = pltpu.make_async_remote_copy(
                src_ref=out[t].at[c], dst_ref=out[t].at[c], send_sem=send.at[t], recv_sem=recv.at[t],
                device_id=(x, y, 1 - c), device_id_type=MESH)
            cp.start()
            copies.append(cp)
        for t, cp in enumerate(copies):
            cp.wait_send()
            pltpu.make_async_remote_copy(
                src_ref=out[t].at[1 - c], dst_ref=out[t].at[1 - c], send_sem=send.at[t], recv_sem=recv.at[t],
                device_id=(x, y, 1 - c), device_id_type=MESH).wait_recv()

    return pl.pallas_call(
        body, name="pair_exchange", in_specs=[ANY] * n, out_specs=[ANY] * n,
        out_shape=[_sds(s.shape, s.dtype) for s in bufs], input_output_aliases={t: t for t in range(n)},
        scratch_shapes=[pltpu.SemaphoreType.DMA((n,)), pltpu.SemaphoreType.DMA((n,))],
    )(*bufs)


def _row_tile(rows, cols, itemsize, budget=2 * 1024 * 1024):
    tr = rows
    while tr * cols * itemsize > budget and tr % 32 == 0:
        tr //= 2
    return tr


def _sum_pair(name, g, got, where):
    nchip, rows, cols = g.shape
    half = rows // 2
    tr = _row_tile(half, cols, 4)
    per = half // tr

    def body(w_ref, g_ref, r_ref, o_ref):
        o_ref[...] = (g_ref[...].astype(F32) + r_ref[...].astype(F32)).astype(o_ref.dtype)

    blk = pl.BlockSpec((None, tr, cols), lambda j, i, w_ref: (j, i, 0))
    return pl.pallas_call(
        body, name=name,
        grid_spec=pltpu.PrefetchScalarGridSpec(
            num_scalar_prefetch=1, grid=(nchip, per),
            in_specs=[pl.BlockSpec((None, tr, cols), lambda j, i, w_ref: (j, w_ref[1] * per + i, 0)), blk], out_specs=blk),
        out_shape=_sds((nchip, half, cols), BF16), compiler_params=_cp("parallel", "parallel"),
    )(where, g, got)


def _sum_chips(name, p, got, where):
    _, rows, cols = p.shape
    tr = _row_tile(rows, cols, 4)

    def body(w_ref, p_ref, r0, r1, r2, o_ref):
        o_ref[...] = ((p_ref[...].astype(F32) + r0[...].astype(F32)) + r1[...].astype(F32)) + r2[...].astype(F32)

    def got_k(k):
        return pl.BlockSpec((None, tr, cols), lambda i, w_ref: (k, i, 0))

    return pl.pallas_call(
        body, name=name,
        grid_spec=pltpu.PrefetchScalarGridSpec(
            num_scalar_prefetch=1, grid=(rows // tr,),
            in_specs=[pl.BlockSpec((None, tr, cols), lambda i, w_ref: (w_ref[0], i, 0)), got_k(0), got_k(1), got_k(2)],
            out_specs=pl.BlockSpec((None, tr, cols), lambda i, w_ref: (w_ref[1], i, 0))),
        out_shape=_sds((2, rows, cols), F32), compiler_params=_cp("parallel"),
    )(where, p, got, got, got)


def _reduce_scatter(grads, where):
    return _rs_finish(*_rs_begin(grads, where), where)


def _rs_begin(grads, where, tag=None, collective_id=None):
    got = _rs_pair(grads)
    parts = [_sum_pair(f"sum_pair_{t}", g, r, where) for t, (g, r) in enumerate(zip(grads, got))]
    return parts, (_rs_chips(parts) if tag is None else _rs_chips_async(parts, tag, collective_id))


def _rs_finish(parts, got, where):
    halves = [_sum_chips(f"sum_chips_{t}", p, r, where) for t, (p, r) in enumerate(zip(parts, got))]
    return _pair_exchange(halves)


def _all_reduce_small(name, v):
    rows = v.shape[0]

    def body(v_ref, o_ref, land, send, recv):
        x, y, c = _me()
        mine = 4 * x + 2 * y + c
        copies = []
        for k in range(1, 8):
            kx, ky, kc = k >> 2, (k >> 1) & 1, k & 1
            peer = (x ^ kx, y ^ ky, c ^ kc)
            cp = pltpu.make_async_remote_copy(
                src_ref=v_ref, dst_ref=land.at[mine], send_sem=send.at[k - 1], recv_sem=recv.at[k - 1],
                device_id=peer, device_id_type=MESH)
            cp.start()
            copies.append(cp)
        land[mine] = v_ref[...]
        for k in range(1, 8):
            kx, ky, kc = k >> 2, (k >> 1) & 1, k & 1
            src = 4 * (x ^ kx) + 2 * (y ^ ky) + (c ^ kc)
            pltpu.make_async_remote_copy(
                src_ref=v_ref, dst_ref=land.at[src], send_sem=send.at[k - 1], recv_sem=recv.at[k - 1],
                device_id=(x ^ kx, y ^ ky, c ^ kc), device_id_type=MESH).wait_recv()
        acc = land[0]
        for d in range(1, 8):
            acc = acc + land[d]
        o_ref[...] = acc
        for cp in copies:
            cp.wait_send()

    vm = pl.BlockSpec(memory_space=pltpu.VMEM)
    return pl.pallas_call(
        body, name=name, in_specs=[vm], out_specs=vm, out_shape=_sds((rows, 128), F32),
        scratch_shapes=[pltpu.VMEM((8, rows, 128), F32), pltpu.SemaphoreType.DMA((7,)), pltpu.SemaphoreType.DMA((7,))],
    )(v)


def _adamw(name, w, g, m, v):
    rows, cols = w.shape
    tr = _row_tile(rows, cols, 4, budget=1024 * 1024)
    c1 = 1.0 / (1.0 - ADAM_B1 ** ADAM_STEP)
    c2 = 1.0 / (1.0 - ADAM_B2 ** ADAM_STEP)

    def body(w_ref, g_ref, m_ref, v_ref, d_ref, nm_ref, nv_ref):
        gv = g_ref[...]
        nm = ADAM_B1 * m_ref[...] + (1.0 - ADAM_B1) * gv
        nv = ADAM_B2 * v_ref[...] + (1.0 - ADAM_B2) * (gv * gv)
        d_ref[...] = -ADAM_LR * ((nm * c1) / (jnp.sqrt(nv * c2) + ADAM_EPS) + ADAM_WD * w_ref[...])
        nm_ref[...] = nm
        nv_ref[...] = nv

    blk = pl.BlockSpec((tr, cols), lambda i: (i, 0))
    return pl.pallas_call(
        body, name=name, grid=(rows // tr,), in_specs=[blk] * 4, out_specs=[blk] * 3,
        out_shape=[_sds((rows, cols), F32)] * 3, compiler_params=_cp("parallel"),
    )(w, g, m, v)


def _adamw_layers(name, w, g0, g1, m, v):
    _, half, cols = g0.shape
    tr = _row_tile(half, cols, 4, budget=1024 * 1024)
    per_half = half // tr
    per = 2 * per_half
    c1 = 1.0 / (1.0 - ADAM_B1 ** ADAM_STEP)
    c2 = 1.0 / (1.0 - ADAM_B2 ** ADAM_STEP)

    def body(w_ref, g0_ref, g1_ref, m_ref, v_ref, g_ref, d_ref, nm_ref, nv_ref):
        gv = jnp.where(pl.program_id(0) == 0, g0_ref[...], g1_ref[...])
        nm = ADAM_B1 * m_ref[...] + (1.0 - ADAM_B1) * gv
        nv = ADAM_B2 * v_ref[...] + (1.0 - ADAM_B2) * (gv * gv)
        g_ref[...] = gv
        d_ref[...] = -ADAM_LR * ((nm * c1) / (jnp.sqrt(nv * c2) + ADAM_EPS) + ADAM_WD * w_ref[...])
        nm_ref[...] = nm
        nv_ref[...] = nv

    both = pl.BlockSpec((None, tr, cols), lambda l, i: (l, i, 0))

    def halves(i):
        return i // per_half, i % per_half, 0

    first = pl.BlockSpec((None, tr, cols), lambda l, i: halves(i * (1 - l) + (per - 1) * l))
    second = pl.BlockSpec((None, tr, cols), lambda l, i: halves(i * l))
    return pl.pallas_call(
        body, name=name, grid=(2, per), in_specs=[both, first, second, both, both], out_specs=[both] * 4,
        out_shape=[_sds(w.shape, F32)] * 4, compiler_params=_cp("arbitrary", "arbitrary"),
    )(w, g0, g1, m, v)


def _to_bf16_slot(name, w, l, where):
    _, rows, cols = w.shape
    tr = _row_tile(rows, cols, 4)

    def body(w_ref, x_ref, o_ref):
        o_ref[...] = x_ref[...].astype(BF16)

    return pl.pallas_call(
        body, name=name,
        grid_spec=pltpu.PrefetchScalarGridSpec(
            num_scalar_prefetch=1, grid=(rows // tr,), in_specs=[pl.BlockSpec((None, tr, cols), lambda i, w_ref: (l, i, 0))],
            out_specs=pl.BlockSpec((None, tr, cols), lambda i, w_ref: (w_ref[0], i, 0))),
        out_shape=_sds((N_CHIPS, rows, cols), BF16), compiler_params=_cp("parallel"))(where, w)


BIG = ("w_in", "w_pool_up", "w_sb_up", "w_gdn_up", "w_out", "w_ff1", "w_ff2")
SMALL = (("attn_norm", (D,)), ("pool_w", (4, 128, 128)), ("pool_scale", (POOL_W,)), ("gdn_a_log", (HEADS,)),
         ("gdn_dt_bias", (HEADS,)), ("gdn_norm", (HD,)), ("mlp_norm", (D,)))


PACK_TILE = 8 * 128


def _rows128(a):
    flat = a.reshape(-1)
    pad = (-flat.shape[0]) % PACK_TILE
    return jnp.pad(flat, (0, pad)).reshape(-1, 128)


def _pack(parts):
    packed = jnp.concatenate([_rows128(p) for p in parts], axis=0)
    return jnp.pad(packed, ((0, (-packed.shape[0]) % 8), (0, 0)))


def _unpack(packed, shapes):
    out, r = [], 0
    for shp in shapes:
        size = 1
        for s in shp:
            size *= s
        nr = -(-size // PACK_TILE) * 8
        out.append(packed[r:r + nr].reshape(-1)[:size].reshape(shp))
        r += nr
    return out


def kernel(x, attn_norm, w_in, pool_w, pool_scale, gdn_conv, gdn_a_log, gdn_dt_bias, gdn_norm, w_pool_up, w_sb_up, w_gdn_up, w_out, mlp_norm, w_ff1, w_ff2, final_norm, loss_target, m_attn_norm, m_w_in, m_pool_w, m_pool_scale, m_gdn_conv, m_gdn_a_log, m_gdn_dt_bias, m_gdn_norm, m_w_pool_up, m_w_sb_up, m_w_gdn_up, m_w_out, m_mlp_norm, m_w_ff1, m_w_ff2, m_final_norm, v_attn_norm, v_w_in, v_pool_w, v_pool_scale, v_gdn_conv, v_gdn_a_log, v_gdn_dt_bias, v_gdn_norm, v_w_pool_up, v_w_sb_up, v_w_gdn_up, v_w_out, v_mlp_norm, v_w_ff1, v_w_ff2, v_final_norm):
    weights = dict(attn_norm=attn_norm, w_in=w_in, pool_w=pool_w, pool_scale=pool_scale, gdn_conv=gdn_conv,
                   gdn_a_log=gdn_a_log, gdn_dt_bias=gdn_dt_bias, gdn_norm=gdn_norm, w_pool_up=w_pool_up, w_sb_up=w_sb_up,
                   w_gdn_up=w_gdn_up, w_out=w_out, mlp_norm=mlp_norm, w_ff1=w_ff1, w_ff2=w_ff2, final_norm=final_norm)
    mom1 = dict(attn_norm=m_attn_norm, w_in=m_w_in, pool_w=m_pool_w, pool_scale=m_pool_scale, gdn_conv=m_gdn_conv,
                gdn_a_log=m_gdn_a_log, gdn_dt_bias=m_gdn_dt_bias, gdn_norm=m_gdn_norm, w_pool_up=m_w_pool_up,
                w_sb_up=m_w_sb_up, w_gdn_up=m_w_gdn_up, w_out=m_w_out, mlp_norm=m_mlp_norm, w_ff1=m_w_ff1, w_ff2=m_w_ff2,
                final_norm=m_final_norm)
    mom2 = dict(attn_norm=v_attn_norm, w_in=v_w_in, pool_w=v_pool_w, pool_scale=v_pool_scale, gdn_conv=v_gdn_conv,
                gdn_a_log=v_gdn_a_log, gdn_dt_bias=v_gdn_dt_bias, gdn_norm=v_gdn_norm, w_pool_up=v_w_pool_up,
                w_sb_up=v_w_sb_up, w_gdn_up=v_w_gdn_up, w_out=v_w_out, mlp_norm=v_mlp_norm, w_ff1=v_w_ff1, w_ff2=v_w_ff2,
                final_norm=v_final_norm)
    xi, yi, ci = lax.axis_index("x"), lax.axis_index("y"), lax.axis_index("c")
    chip = 2 * xi + yi
    where = jnp.stack([chip, ci]).astype(jnp.int32)

    bufs = [[_to_bf16_slot(f"cast_{nm}_{l}", weights[nm], l, where) for nm in BIG] for l in range(2)]
    first = _gather_weights_async(bufs[0][:1], "0_w_in", 4)
    first, _ = lax.optimization_barrier((first, (bufs[0][1:], bufs[1])))
    rest, _ = lax.optimization_barrier((bufs[0][1:], first))
    gw = [dict(zip(BIG, list(first) + _gather_weights_async(rest, "0_rest", 1))), {}]

    def reached(l, stage, value):
        if l == 0 and stage == "proj":
            later, _ = lax.optimization_barrier((bufs[1][:1], value))
            gw[1]["w_in"] = _gather_weights_async(later, "1_w_in", 2)[0]
        if l == 0 and stage == "merged":
            later, _ = lax.optimization_barrier((bufs[1][1:], value))
            gw[1].update(zip(BIG[1:], _gather_weights_async(later, "1_rest", 3)))

    def weights_of(l):
        return gw[l], _w_in_from_shards(gw[l]["w_in"])

    conv_cols = gdn_conv.shape[-1]
    conv_place = lax.dynamic_update_slice(jnp.zeros((2, GDN_CONV, N_CHIPS * conv_cols), F32),
                                          jnp.where(ci == 0, gdn_conv, 0.0), (0, 0, chip * conv_cols))
    conv_full = _all_reduce_small("gather_conv", _rows128(conv_place)).reshape(2, GDN_CONV, N_CHIPS * conv_cols)
    sp = dict(attn_norm=attn_norm.reshape(2, 1, D), pool_w=pool_w, pool_scale=pool_scale.reshape(2, 1, POOL_W),
              conv=conv_full, a_log=jnp.stack([_row128(gdn_a_log[l]) for l in range(2)]),
              dt_bias=jnp.stack([_row128(gdn_dt_bias[l]) for l in range(2)]), gdn_norm=gdn_norm.reshape(2, 1, HD),
              mlp_norm=mlp_norm.reshape(2, 1, D), final_norm=final_norm.reshape(1, D))

    started = []

    def emit(l, names, g, v):
        parts, got = _rs_begin([g[nm] for nm in names], where, f"{l}_{names[0]}", 5 + len(started))
        started.append((l, names, parts, got))
        v, _ = lax.optimization_barrier((v, parts))
        return v

    loss, grad_x, grads, g_final = _local_step(x[0], loss_target[0], weights_of, sp, emit, reached)
    loss = lax.psum(loss[0, 0], ("x", "y", "c"))
    big_grads = {nm: [None, None] for nm in BIG}
    for l, names, parts, got in started[:-1]:
        got, _ = lax.optimization_barrier((got, grad_x))
        for nm, red in zip(names, _rs_finish(parts, got, where)):
            big_grads[nm][l] = red
    small_parts, small_shapes = [], []
    for l in range(2):
        g = grads[l]
        for nm, shp in SMALL:
            key = {"gdn_a_log": "a_log", "gdn_dt_bias": "dt_bias"}.get(nm, nm)
            val = g[key]
            small_parts.append(val[0, :HEADS] if nm in ("gdn_a_log", "gdn_dt_bias") else val)
            small_shapes.append(shp)
        small_parts.append(g["conv"])
        small_shapes.append((GDN_CONV, N_CHIPS * conv_cols))
    small_parts.append(g_final)
    small_shapes.append((D,))
    small_pack = _pack(small_parts)

    grad, delta, new_m, new_v = {}, {}, {}, {}
    for nm in BIG[1:]:
        grad[nm], delta[nm], new_m[nm], new_v[nm] = _adamw_layers("adamw_" + nm, weights[nm], *big_grads[nm], mom1[nm], mom2[nm])
    l, names, parts, got = started[-1]
    got, _ = lax.optimization_barrier((got, [new_v[nm] for nm in BIG[1:]]))
    for nm, red in zip(names, _rs_finish(parts, got, where)):
        big_grads[nm][l] = red
    nm = BIG[0]
    grad[nm], delta[nm], new_m[nm], new_v[nm] = _adamw_layers("adamw_" + nm, weights[nm], *big_grads[nm], mom1[nm], mom2[nm])
    small_pack, _ = lax.optimization_barrier((small_pack, new_v[nm]))
    reduced = _unpack(_all_reduce_small("reduce_small", small_pack), small_shapes)
    per = len(SMALL) + 1
    for i, (nm, _) in enumerate(SMALL):
        grad[nm] = jnp.stack([reduced[i], reduced[per + i]])
    conv_g = jnp.stack([reduced[per - 1], reduced[2 * per - 1]])
    grad["gdn_conv"] = lax.dynamic_slice(conv_g, (0, 0, chip * conv_cols), (2, GDN_CONV, conv_cols))
    grad["final_norm"] = reduced[-1]
    small_names = [nm for nm, _ in SMALL] + ["gdn_conv", "final_norm"]
    packs = [_pack([src[nm] for nm in small_names]) for src in (weights, grad, mom1, mom2)]
    outs = _adamw("adamw_small", *packs)
    shapes = [weights[nm].shape for nm in small_names]
    for dst, packed in zip((delta, new_m, new_v), outs):
        for nm, val in zip(small_names, _unpack(packed, shapes)):
            dst[nm] = val

    order = ("attn_norm", "w_in", "pool_w", "pool_scale", "gdn_conv", "gdn_a_log", "gdn_dt_bias", "gdn_norm", "w_pool_up",
             "w_sb_up", "w_gdn_up", "w_out", "mlp_norm", "w_ff1", "w_ff2", "final_norm")
    return (loss, grad_x[None], *[grad[n] for n in order], *[delta[n] for n in order], *[new_m[n] for n in order],
            *[new_v[n] for n in order])
```

```python
import functools

import jax
import jax.numpy as jnp
from jax import lax
from jax.experimental import pallas as pl
from jax.experimental.pallas import tpu as pltpu
from jax.experimental.pallas import tpu_sc as plsc

F32, BF16 = jnp.float32, jnp.bfloat16
HIGH = lax.Precision.HIGH
MESH = pl.DeviceIdType.MESH

D = 2048
EPS = 1e-6
POOL_WINDOWS = (2, 4, 8, 16)
POOL_W, SB_W, GDN_W = 512, 768, 768
HEADS, HD = 6, 128
SB_BLOCK = 128
GDN_CHUNK = 64
D_FF = 4 * D
N_IN = 12044
N_CHIPS = 4
OFF_SB, OFF_GQKV, OFF_P, OFF_Z, OFF_AB, OFF_GATE = 0, 2304, 4608, 5120, 5888, 6144
AB_W = 256
ZAB_W = GDN_W + AB_W
ORIG_SB, ORIG_Z, ORIG_AB, ORIG_GATE = 512, 5120, 5888, 5900
N_AL = 12288
VMEM_LIMIT = 48 * 1024 * 1024

ADAM_LR, ADAM_B1, ADAM_B2, ADAM_EPS, ADAM_WD, ADAM_STEP = 0.001, 0.9, 0.999, 1e-08, 0.01, 10

NT = (((1,), (1,)), ((), ()))
TN = (((0,), (0,)), ((), ()))


def _cp(*sem):
    return pltpu.CompilerParams(dimension_semantics=sem, vmem_limit_bytes=VMEM_LIMIT)


def _dot(a, b, dims=None, precision=None):
    if dims is None:
        dims = (((a.ndim - 1,), (0,)), ((), ()))
    return lax.dot_general(a, b, dims, precision=precision, preferred_element_type=F32)


def _hdot(a, b, dims=None):
    return _dot(a, b, dims, precision=HIGH)


def _bdot(a, b, dims=None):
    return _dot(a.astype(BF16), b.astype(BF16), dims)


def _mm(name, a, b, *, m, n, k, tm, tn, tk, a_spec, b_spec, dims, out_shapes, out_specs,
        extras=(), extra_specs=(), epilogue=None):
    nk = k // tk
    ne, no = len(extras), len(out_shapes)

    def body(*refs):
        a_ref, b_ref = refs[0], refs[1]
        ex = refs[2:2 + ne]
        outs = refs[2 + ne:2 + ne + no]
        kk = pl.program_id(2)

        def finish(r):
            res = epilogue(r, *[e[...] for e in ex]) if epilogue is not None else (r,)
            for o, v in zip(outs, res):
                o[...] = v.astype(o.dtype)

        part = _dot(a_ref[...].astype(BF16), b_ref[...].astype(BF16), dims)
        if nk == 1:
            finish(part)
            return
        acc = refs[-1]

        @pl.when(kk == 0)
        def _():
            acc[...] = part

        @pl.when((kk > 0) & (kk < nk - 1))
        def _():
            acc[...] += part

        @pl.when(kk == nk - 1)
        def _():
            finish(acc[...] + part)

    return pl.pallas_call(
        body, name=name, grid=(m // tm, n // tn, nk),
        in_specs=[a_spec, b_spec, *extra_specs], out_specs=out_specs, out_shape=out_shapes,
        scratch_shapes=[] if nk == 1 else [pltpu.VMEM((tm, tn), F32)],
        compiler_params=_cp("parallel", "parallel", "arbitrary"),
    )(a, b, *extras)


def _a_plain(tm, tk):
    return pl.BlockSpec((tm, tk), lambda i, j, kk: (i, kk))


def _a_trans(tm, tk):
    return pl.BlockSpec((tk, tm), lambda i, j, kk: (kk, i))


def _b_plain(tk, tn):
    return pl.BlockSpec((tk, tn), lambda i, j, kk: (kk, j))


def _b_trans(tk, tn):
    return pl.BlockSpec((tn, tk), lambda i, j, kk: (j, kk))


def _o_plain(tm, tn):
    return pl.BlockSpec((tm, tn), lambda i, j, kk: (i, j))


def _o_colshard(tm, tn, ns_cols):
    per = ns_cols // tn
    return pl.BlockSpec((None, tm, tn), lambda i, j, kk: (j // per, i, j % per))


def _w_cols(tk, tn, ns):
    per = ns // tn
    return pl.BlockSpec((None, tk, tn), lambda i, j, kk: (j // per, kk, j % per))


def _w_cols_t(tk, tn, ns):
    per = ns // tk
    return pl.BlockSpec((None, tn, tk), lambda i, j, kk: (kk // per, j, kk % per))


def _w_rows(tk, tn, ks):
    per = ks // tk
    return pl.BlockSpec((None, tk, tn), lambda i, j, kk: (kk // per, kk % per, j))


def _w_rows_t(tk, tn, ks):
    per = ks // tn
    return pl.BlockSpec((None, tn, tk), lambda i, j, kk: (j // per, j % per, kk))


def _sds(shape, dtype):
    return jax.ShapeDtypeStruct(shape, dtype)


def _rms_fwd(name, x, gain):
    t = x.shape[0]
    tt = min(256, t)

    def body(x_ref, g_ref, u_ref):
        xv = x_ref[...]
        r = lax.rsqrt(jnp.mean(xv * xv, axis=-1, keepdims=True) + EPS)
        u_ref[...] = (xv * r * g_ref[...]).astype(u_ref.dtype)

    return pl.pallas_call(
        body, name=name, grid=(t // tt,),
        in_specs=[pl.BlockSpec((tt, D), lambda i: (i, 0)), pl.BlockSpec((1, D), lambda i: (0, 0))],
        out_specs=pl.BlockSpec((tt, D), lambda i: (i, 0)), out_shape=_sds((t, D), BF16),
        compiler_params=_cp("parallel"),
    )(x, gain)


def _rms_bwd(name, du, x, gain, dres):
    t = x.shape[0]
    tt = min(256, t)

    def body(du_ref, x_ref, g_ref, dres_ref, dx_ref, dg_ref):
        @pl.when(pl.program_id(0) == 0)
        def _():
            dg_ref[...] = jnp.zeros_like(dg_ref)

        xv, duv = x_ref[...], du_ref[...]
        r = lax.rsqrt(jnp.mean(xv * xv, axis=-1, keepdims=True) + EPS)
        nx = xv * r
        dn = duv * g_ref[...]
        dg_ref[...] += jnp.sum(duv * nx, axis=0, keepdims=True)
        dx_ref[...] = dres_ref[...] + r * (dn - nx * jnp.mean(dn * nx, axis=-1, keepdims=True))

    row = pl.BlockSpec((tt, D), lambda i: (i, 0))
    vec = pl.BlockSpec((1, D), lambda i: (0, 0))
    return pl.pallas_call(
        body, name=name, grid=(t // tt,), in_specs=[row, row, vec, row], out_specs=[row, vec],
        out_shape=[_sds((t, D), F32), _sds((1, D), F32)], compiler_params=_cp("arbitrary"),
    )(du, x, gain, dres)


def _loss_head(x, gain, target):
    t = x.shape[0]
    tt = min(256, t)

    def body(x_ref, g_ref, t_ref, loss_ref, dx_ref, dg_ref):
        @pl.when(pl.program_id(0) == 0)
        def _():
            dg_ref[...] = jnp.zeros_like(dg_ref)
            loss_ref[...] = jnp.zeros_like(loss_ref)

        xv = x_ref[...]
        r = lax.rsqrt(jnp.mean(xv * xv, axis=-1, keepdims=True) + EPS)
        nx = xv * r
        err = nx * g_ref[...] - t_ref[...]
        loss_ref[...] += 0.5 * jnp.sum(jnp.mean(err * err, axis=-1, keepdims=True), axis=0, keepdims=True)
        dy = err * (1.0 / D)
        dn = dy * g_ref[...]
        dg_ref[...] += jnp.sum(dy * nx, axis=0, keepdims=True)
        dx_ref[...] = r * (dn - nx * jnp.mean(dn * nx, axis=-1, keepdims=True))

    row = pl.BlockSpec((tt, D), lambda i: (i, 0))
    vec = pl.BlockSpec((1, D), lambda i: (0, 0))
    one = pl.BlockSpec((1, 1), lambda i: (0, 0))
    return pl.pallas_call(
        body, name="loss_head", grid=(t // tt,), in_specs=[row, vec, row], out_specs=[one, row, vec],
        out_shape=[_sds((1, 1), F32), _sds((t, D), F32), _sds((1, D), F32)], compiler_params=_cp("arbitrary"),
    )(x, gain, target)


def _shift_down(v, s, t_idx):
    return jnp.where(t_idx >= s, pltpu.roll(v, s, 0), 0.0)


def _shift_up(v, s, t_idx, t):
    return jnp.where(t_idx < t - s, pltpu.roll(v, t - s, 0), 0.0)


def _pool_d(p, g, t_idx):
    s = p
    for step in range(g + 1):
        s = s + _shift_down(s, 1 << step, t_idx)
    cnt = jnp.minimum(t_idx + 1, POOL_WINDOWS[g]).astype(F32)
    return s / cnt - p, cnt


def _pool_fwd(proj, pool_w, pool_scale):
    t = proj.shape[0]
    g128 = POOL_W // len(POOL_WINDOWS)

    def body(p_ref, w_ref, s_ref, y_ref):
        t_idx = lax.broadcasted_iota(jnp.int32, (t, g128), 0)
        for g in range(len(POOL_WINDOWS)):
            sl = slice(g * g128, (g + 1) * g128)
            d, _ = _pool_d(p_ref[:, sl], g, t_idx)
            y_ref[:, sl] = (_bdot(d, w_ref[g]) * s_ref[:, sl]).astype(y_ref.dtype)

    return pl.pallas_call(
        body, name="pool_fwd", grid=(1,),
        in_specs=[pl.BlockSpec((t, POOL_W), lambda i: (0, OFF_P // POOL_W)),
                  pl.BlockSpec((4, g128, g128), lambda i: (0, 0, 0)), pl.BlockSpec((1, POOL_W), lambda i: (0, 0))],
        out_specs=pl.BlockSpec((t, POOL_W), lambda i: (0, 0)), out_shape=_sds((t, POOL_W), BF16),
        compiler_params=_cp("arbitrary"),
    )(proj, pool_w, pool_scale)


def _pool_bwd(proj, pool_w, pool_scale, dy, dproj):
    t = proj.shape[0]
    g128 = POOL_W // len(POOL_WINDOWS)

    def body(p_ref, w_ref, s_ref, dy_ref, _, dp_ref, dw_ref, ds_ref):
        t_idx = lax.broadcasted_iota(jnp.int32, (t, g128), 0)
        for g in range(len(POOL_WINDOWS)):
            sl = slice(g * g128, (g + 1) * g128)
            d, cnt = _pool_d(p_ref[:, sl], g, t_idx)
            dyv = dy_ref[:, sl].astype(F32)
            ds_ref[:, sl] = jnp.sum(dyv * _bdot(d, w_ref[g]), axis=0, keepdims=True)
            dys = dyv * s_ref[:, sl]
            dw_ref[g] = _bdot(d, dys, TN)
            dd = _bdot(dys, w_ref[g], NT)
            s = dd / cnt
            for step in range(g + 1):
                s = s + _shift_up(s, 1 << step, t_idx, t)
            dp_ref[:, sl] = (s - dd).astype(dp_ref.dtype)

    return pl.pallas_call(
        body, name="pool_bwd", grid=(1,),
        in_specs=[pl.BlockSpec((t, POOL_W), lambda i: (0, OFF_P // POOL_W)),
                  pl.BlockSpec((4, g128, g128), lambda i: (0, 0, 0)), pl.BlockSpec((1, POOL_W), lambda i: (0, 0)),
                  pl.BlockSpec((t, POOL_W), lambda i: (0, 0)), ANY],
        out_specs=[pl.BlockSpec((t, POOL_W), lambda i: (0, OFF_P // POOL_W)), pl.BlockSpec((4, g128, g128), lambda i: (0, 0, 0)),
                   pl.BlockSpec((1, POOL_W), lambda i: (0, 0))],
        out_shape=[_sds(dproj.shape, dproj.dtype), _sds((4, g128, g128), F32), _sds((1, POOL_W), F32)],
        input_output_aliases={4: 0}, compiler_params=_cp("arbitrary"),
    )(proj, pool_w, pool_scale, dy, dproj)


SB_GROUP = 3
SB_GW = SB_GROUP * HD


def _sb_cast_kv(proj):
    t = proj.shape[0]
    tt = min(512, t)

    def body(x_ref, o_ref):
        o_ref[...] = x_ref[...].astype(BF16)

    return pl.pallas_call(
        body, name="sb_cast_kv", grid=(t // tt, 2),
        in_specs=[pl.BlockSpec((tt, SB_W), lambda i, j: (i, OFF_SB // SB_W + 1 + j))],
        out_specs=pl.BlockSpec((tt, SB_W), lambda i, j: (i, j)), out_shape=_sds((t, 2 * SB_W), BF16),
        compiler_params=_cp("parallel", "parallel"),
    )(proj)


def _sb_specs(t):
    q_spec = pl.BlockSpec((SB_BLOCK, SB_GW), lambda g, i: (i, OFF_SB // SB_GW + g))
    k_spec = pl.BlockSpec((t, SB_GW), lambda g, i: (0, g))
    v_spec = pl.BlockSpec((t, SB_GW), lambda g, i: (0, SB_W // SB_GW + g))
    return q_spec, k_spec, v_spec


def _head(ref, h, rows=None):
    cols = slice(h * HD, (h + 1) * HD)
    return ref[:, cols] if rows is None else ref[rows, cols]


SB_KEYS = 512


def _sub(v, b):
    return v[:, b * SB_BLOCK:(b + 1) * SB_BLOCK]


def _sb_keep(kc, limit):
    row = lax.broadcasted_iota(jnp.int32, (SB_BLOCK, kc), 0)
    col = lax.broadcasted_iota(jnp.int32, (SB_BLOCK, kc), 1)
    return col < row + limit


def _sb_chunk(q, keys, run, later, limit):
    kc = keys.shape[0]
    z = _dot(q, keys, NT)
    lsz = jax.nn.log_sigmoid(z)
    ls = lsz - z
    if limit is not None:
        keep = _sb_keep(kc, limit)
        ls = jnp.where(keep, ls, 0.0)
    parts = [None] * (kc // SB_BLOCK)
    for b in reversed(range(kc // SB_BLOCK)):
        parts[b] = _hdot(_sub(ls, b), later) + run
        run = run + jnp.sum(_sub(ls, b), axis=1, keepdims=True)
    a = jnp.exp(lsz + jnp.concatenate(parts, axis=1))
    if limit is not None:
        a = jnp.where(keep, a, 0.0)
    return z, a, run


def _sb_fwd(proj, kv):
    t = proj.shape[0]
    kc = min(SB_KEYS, t)
    scale = HD ** -0.5

    def body(q_ref, k_ref, v_ref, o_ref):
        i = pl.program_id(1)
        top = (i * SB_BLOCK) // kc
        qs = [(_head(q_ref, h) * scale).astype(BF16) for h in range(SB_GROUP)]
        row = lax.broadcasted_iota(jnp.int32, (SB_BLOCK, SB_BLOCK), 0)
        col = lax.broadcasted_iota(jnp.int32, (SB_BLOCK, SB_BLOCK), 1)
        later = (row > col).astype(F32)

        def chunk(jc, carry, masked):
            rows = pl.ds(pl.multiple_of(jc * kc, kc), kc)
            limit = i * SB_BLOCK - jc * kc if masked else None
            out = []
            for h in range(SB_GROUP):
                acc, run = carry[h]
                _, a, run = _sb_chunk(qs[h], _head(k_ref, h, rows), run, later, limit)
                out.append((acc + _dot(a.astype(BF16), _head(v_ref, h, rows)), run))
            return tuple(out)

        zero = tuple((jnp.zeros((SB_BLOCK, HD), F32), jnp.zeros((SB_BLOCK, 1), F32)) for _ in range(SB_GROUP))
        carry = chunk(top, zero, True)
        carry = lax.fori_loop(0, top, lambda jj, c: chunk(top - 1 - jj, c, False), carry)
        for h in range(SB_GROUP):
            o_ref[:, h * HD:(h + 1) * HD] = carry[h][0].astype(o_ref.dtype)

    return pl.pallas_call(
        body, name="sb_fwd", grid=(HEADS // SB_GROUP, t // SB_BLOCK), in_specs=list(_sb_specs(t)),
        out_specs=pl.BlockSpec((SB_BLOCK, SB_GW), lambda g, i: (i, g)), out_shape=_sds((t, SB_W), BF16),
        compiler_params=_cp("parallel", "arbitrary"),
    )(proj, kv, kv)


def _sb_bwd(proj, kv, dy):
    t = proj.shape[0]
    nq = t // SB_BLOCK
    kc = min(SB_KEYS, t)
    scale = HD ** -0.5

    def body(q_ref, k_ref, v_ref, do_ref, dq_ref, dk_ref, dv_ref, z_scr, e_scr):
        i = pl.program_id(1)
        top = (i * SB_BLOCK) // kc

        @pl.when(i == 0)
        def _():
            dk_ref[...] = jnp.zeros_like(dk_ref)
            dv_ref[...] = jnp.zeros_like(dv_ref)

        qs = [(_head(q_ref, h) * scale).astype(BF16) for h in range(SB_GROUP)]
        dos = [_head(do_ref, h).astype(BF16) for h in range(SB_GROUP)]
        row = lax.broadcasted_iota(jnp.int32, (SB_BLOCK, SB_BLOCK), 0)
        col = lax.broadcasted_iota(jnp.int32, (SB_BLOCK, SB_BLOCK), 1)
        later = (row > col).astype(F32)
        earlier = (row < col).astype(F32)

        def down(jc, runs, masked):
            rows = pl.ds(pl.multiple_of(jc * kc, kc), kc)
            limit = i * SB_BLOCK - jc * kc if masked else None
            out = []
            for h in range(SB_GROUP):
                z, a, run = _sb_chunk(qs[h], _head(k_ref, h, rows), runs[h], later, limit)
                z_scr[h, jc] = z
                e_scr[h, jc] = a * _dot(dos[h], _head(v_ref, h, rows), NT)
                dv_ref[rows, h * HD:(h + 1) * HD] += _dot(a.astype(BF16), dos[h], TN)
                out.append(run)
            return tuple(out)

        zero = tuple(jnp.zeros((SB_BLOCK, 1), F32) for _ in range(SB_GROUP))
        runs = down(top, zero, True)
        lax.fori_loop(0, top, lambda jj, r: down(top - 1 - jj, r, False), runs)

        def up(jc, carry, masked):
            rows = pl.ds(pl.multiple_of(jc * kc, kc), kc)
            out = []
            for h in range(SB_GROUP):
                dq, run = carry[h]
                z, e = z_scr[h, jc], e_scr[h, jc]
                parts = []
                for b in range(kc // SB_BLOCK):
                    parts.append(_hdot(_sub(e, b), earlier) + run)
                    run = run + jnp.sum(_sub(e, b), axis=1, keepdims=True)
                sz = jax.nn.sigmoid(z)
                dz = e * (1.0 - sz) - jnp.concatenate(parts, axis=1) * sz
                if masked:
                    dz = jnp.where(_sb_keep(kc, i * SB_BLOCK - jc * kc), dz, 0.0)
                dz = dz.astype(BF16)
                dk_ref[rows, h * HD:(h + 1) * HD] += _dot(dz, qs[h], TN)
                out.append((dq + _dot(dz, _head(k_ref, h, rows)), run))
            return tuple(out)

        zero = tuple((jnp.zeros((SB_BLOCK, HD), F32), jnp.zeros((SB_BLOCK, 1), F32)) for _ in range(SB_GROUP))
        carry = lax.fori_loop(0, top, lambda jc, c: up(jc, c, False), zero)
        carry = up(top, carry, True)
        for h in range(SB_GROUP):
            dq_ref[:, h * HD:(h + 1) * HD] = (carry[h][0] * scale).astype(dq_ref.dtype)

    blk = pl.BlockSpec((SB_BLOCK, SB_GW), lambda g, i: (i, g))
    seq = pl.BlockSpec((t, SB_GW), lambda g, i: (0, g))
    scratch = pltpu.VMEM((SB_GROUP, t // kc, SB_BLOCK, kc), F32)
    return pl.pallas_call(
        body, name="sb_bwd", grid=(HEADS // SB_GROUP, nq), in_specs=[*_sb_specs(t), blk], out_specs=[blk, seq, seq],
        out_shape=[_sds((t, SB_W), BF16), _sds((t, SB_W), F32), _sds((t, SB_W), F32)],
        scratch_shapes=[scratch, scratch], compiler_params=_cp("parallel", "arbitrary"),
    )(proj, kv, kv, dy)


CONV_TILE = 256
GDN_CONV = 4


def _conv_pre(x, w_ref, t_idx):
    pre = w_ref[GDN_CONV - 1:GDN_CONV, :] * x
    for s in range(1, GDN_CONV):
        pre = pre + w_ref[GDN_CONV - 1 - s:GDN_CONV - s, :] * _shift_down(x, s, t_idx)
    return pre


def _conv_fwd(proj, conv_w):
    t = proj.shape[0]
    width = conv_w.shape[1]

    def body(x_ref, w_ref, y_ref):
        t_idx = lax.broadcasted_iota(jnp.int32, (t, CONV_TILE), 0)
        pre = _conv_pre(x_ref[...], w_ref, t_idx)
        y_ref[...] = pre * jax.nn.sigmoid(pre)

    return pl.pallas_call(
        body, name="conv_fwd", grid=(width // CONV_TILE,),
        in_specs=[pl.BlockSpec((t, CONV_TILE), lambda c: (0, OFF_GQKV // CONV_TILE + c)),
                  pl.BlockSpec((GDN_CONV, CONV_TILE), lambda c: (0, c))],
        out_specs=pl.BlockSpec((t, CONV_TILE), lambda c: (0, c)), out_shape=_sds((t, width), F32),
        compiler_params=_cp("parallel"),
    )(proj, conv_w)


def _conv_bwd(proj, conv_w, dc, dproj):
    t = proj.shape[0]
    width = dc.shape[1]
    per = width // CONV_TILE
    first = OFF_GQKV // CONV_TILE

    def body(x_ref, w_ref, dc_ref, _, dx_ref, dw_ref):
        t_idx = lax.broadcasted_iota(jnp.int32, (t, CONV_TILE), 0)
        x = x_ref[...]
        pre = _conv_pre(x, w_ref, t_idx)
        sg = jax.nn.sigmoid(pre)
        dpre = dc_ref[...] * (sg * (1.0 + pre * (1.0 - sg)))
        dx = w_ref[GDN_CONV - 1:GDN_CONV, :] * dpre
        dw_ref[GDN_CONV - 1:GDN_CONV, :] = jnp.sum(dpre * x, axis=0, keepdims=True)
        for s in range(1, GDN_CONV):
            dx = dx + w_ref[GDN_CONV - 1 - s:GDN_CONV - s, :] * _shift_up(dpre, s, t_idx, t)
            dw_ref[GDN_CONV - 1 - s:GDN_CONV - s, :] = jnp.sum(dpre * _shift_down(x, s, t_idx), axis=0, keepdims=True)
        dx_ref[...] = dx.astype(dx_ref.dtype)

    return pl.pallas_call(
        body, name="conv_bwd", grid=(per,),
        in_specs=[pl.BlockSpec((t, CONV_TILE), lambda c: (0, first + c)),
                  pl.BlockSpec((GDN_CONV, CONV_TILE), lambda c: (0, c)),
                  pl.BlockSpec((t, CONV_TILE), lambda c: (0, c)), ANY],
        out_specs=[pl.BlockSpec((t, CONV_TILE), lambda c: (0, first + c)), pl.BlockSpec((GDN_CONV, CONV_TILE), lambda c: (0, c))],
        out_shape=[_sds(dproj.shape, dproj.dtype), _sds((GDN_CONV, width), F32)],
        input_output_aliases={3: 0}, compiler_params=_cp("parallel"),
    )(proj, conv_w, dc, dproj)


def _heads(x):
    return jnp.concatenate([x[:, h * HD:(h + 1) * HD][None] for h in range(HEADS)], axis=0)


def _hb(a, b, ca=2, cb=1):
    return lax.dot_general(a, b, (((ca,), (cb,)), ((0,), (0,))), precision=HIGH, preferred_element_type=F32)


@jax.custom_vjp
def _unit_lower_inverse(lower):
    c = lower.shape[-1]
    eye = lax.broadcasted_iota(jnp.int32, (c, c), 0) == lax.broadcasted_iota(jnp.int32, (c, c), 1)
    inv = jnp.where(eye, 1.0, 0.0) - lower
    pw = _hb(lower, lower)
    for step in range(5):
        inv = inv + _hb(inv, pw)
        if step < 4:
            pw = _hb(pw, pw)
    return inv


def _unit_lower_inverse_fwd(lower):
    inv = _unit_lower_inverse(lower)
    return inv, inv


def _unit_lower_inverse_bwd(inv, d_inv):
    return (-_hb(_hb(inv, d_inv, 1, 1), inv, 2, 2),)


_unit_lower_inverse.defvjp(_unit_lower_inverse_fwd, _unit_lower_inverse_bwd)


@jax.custom_vjp
def _known_inverse(lower, inv):
    return inv


_known_inverse.defvjp(lambda lower, inv: (inv, inv),
                      lambda inv, d_inv: (*_unit_lower_inverse_bwd(inv, d_inv), jnp.zeros_like(inv)))


def _gdn_prep(cq, ck, cv, ab, alog_row, dtb_row, inv=None, keep_inverse=False):
    c = GDN_CHUNK
    row = lax.broadcasted_iota(jnp.int32, (c, c), 0)
    col = lax.broadcasted_iota(jnp.int32, (c, c), 1)
    incl, strict, eye = row >= col, row > col, row == col
    def lanes(v, first):
        return jnp.concatenate([v[:, first + h:first + h + 1][None] for h in range(HEADS)], axis=0)

    a_col, b_col = lanes(ab, 0), lanes(ab, HEADS)
    a_log, dt_bias = lanes(alog_row, 0), lanes(dtb_row, 0)
    qn = cq * lax.rsqrt(jnp.sum(cq * cq, axis=-1, keepdims=True) + EPS) * (HD ** -0.5)
    kn = ck * lax.rsqrt(jnp.sum(ck * ck, axis=-1, keepdims=True) + EPS)
    la_col = -jnp.exp(a_log) * jax.nn.softplus(a_col + dt_bias)
    beta = jax.nn.sigmoid(b_col)
    la_row = jnp.sum(jnp.where(eye, la_col, 0.0), axis=1, keepdims=True)
    g_col = jnp.sum(jnp.where(incl, la_row, 0.0), axis=2, keepdims=True)
    g_row = jnp.sum(jnp.where(row <= col, la_col, 0.0), axis=1, keepdims=True)
    g_last = jnp.sum(la_col, axis=1, keepdims=True)
    gamma = jnp.where(incl, jnp.exp(jnp.where(incl, g_col - g_row, 0.0)), 0.0)
    lower = jnp.where(strict, beta * _hb(kn, kn, 2, 2) * gamma, 0.0)
    inv = _unit_lower_inverse(lower) if inv is None else _known_inverse(lower, inv)
    u = _hb(inv, cv * beta)
    w = _hb(inv, kn * (beta * jnp.exp(g_col)))
    qk = _hb(qn, kn, 2, 2) * gamma
    out = (u, w, qk, qn * jnp.exp(g_col), kn * jnp.exp(g_last - g_col), jnp.exp(g_last))
    return (*out, inv) if keep_inverse else out


def _gdn_post(o, z, gain):
    y = o * lax.rsqrt(jnp.mean(o * o, axis=-1, keepdims=True) + EPS) * gain
    return y * (z * jax.nn.sigmoid(z))


def _gdn_specs(nc, reverse):
    c = GDN_CHUNK

    def ch(n):
        return nc - 1 - n if reverse else n

    def wide(array_off):
        return pl.BlockSpec((c, GDN_W), lambda n: (ch(n), array_off // GDN_W))

    zab = pl.BlockSpec((c, ZAB_W), lambda n: (ch(n), OFF_Z // ZAB_W))
    row = pl.BlockSpec((1, HD), lambda n: (0, 0))
    state = pl.BlockSpec((None, HEADS, HD, HD), lambda n: (ch(n), 0, 0, 0))
    inverse = pl.BlockSpec((None, HEADS, c, c), lambda n: (ch(n), 0, 0, 0))
    return wide, zab, row, state, inverse


def _gdn_fwd(cqkv, proj, a_log, dt_bias, gain):
    t = proj.shape[0]
    nc = t // GDN_CHUNK
    wide, zab, row, state, inverse = _gdn_specs(nc, False)

    def body(cq_ref, ck_ref, cv_ref, zab_ref, al_ref, dt_ref, g_ref, y_ref, sprev_ref, inv_ref, s_scr):
        @pl.when(pl.program_id(0) == 0)
        def _():
            s_scr[...] = jnp.zeros_like(s_scr)

        z_ref, ab_ref = zab_ref.at[:, :GDN_W], zab_ref.at[:, GDN_W:GDN_W + HD]
        u, w, qk, qd, kd, dec, inv = _gdn_prep(_heads(cq_ref[...]), _heads(ck_ref[...]), _heads(cv_ref[...]), ab_ref[...],
                                               al_ref[...], dt_ref[...], keep_inverse=True)
        inv_ref[...] = inv
        s = s_scr[...]
        sprev_ref[...] = s
        v_new = u - _hb(w, s)
        o = _hb(qd, s) + _hb(qk, v_new)
        s_scr[...] = s * dec + _hb(kd, v_new, 1, 1)
        y = _gdn_post(o, _heads(z_ref[...]), g_ref[...])
        for h in range(HEADS):
            y_ref[:, h * HD:(h + 1) * HD] = y[h].astype(y_ref.dtype)

    return pl.pallas_call(
        body, name="gdn_fwd", grid=(nc,),
        in_specs=[wide(0), wide(GDN_W), wide(2 * GDN_W), zab, row, row, row],
        out_specs=[wide(0), state, inverse],
        out_shape=[_sds((t, GDN_W), BF16), _sds((nc, HEADS, HD, HD), F32), _sds((nc, HEADS, GDN_CHUNK, GDN_CHUNK), F32)],
        scratch_shapes=[pltpu.VMEM((HEADS, HD, HD), F32)], compiler_params=_cp("arbitrary"),
    )(cqkv, cqkv, cqkv, proj, a_log, dt_bias, gain)


def _gdn_bwd(cqkv, proj, a_log, dt_bias, gain, sprev, inverses, dy, dproj):
    t = proj.shape[0]
    nc = t // GDN_CHUNK
    wide, zab, row, state, inverse = _gdn_specs(nc, True)

    def body(cq_ref, ck_ref, cv_ref, zab_ref, al_ref, dt_ref, g_ref, sp_ref, inv_ref, dy_ref, _,
             dc_ref, dzab_ref, dal_ref, ddt_ref, dg_ref, ds_scr):
        @pl.when(pl.program_id(0) == 0)
        def _():
            ds_scr[...] = jnp.zeros_like(ds_scr)
            dal_ref[...] = jnp.zeros_like(dal_ref)
            ddt_ref[...] = jnp.zeros_like(ddt_ref)
            dg_ref[...] = jnp.zeros_like(dg_ref)

        z_ref, ab_ref = zab_ref.at[:, :GDN_W], zab_ref.at[:, GDN_W:GDN_W + HD]

        (u, w, qk, qd, kd, dec), prep_vjp = jax.vjp(
            functools.partial(_gdn_prep, inv=inv_ref[...]),
            _heads(cq_ref[...]), _heads(ck_ref[...]), _heads(cv_ref[...]), ab_ref[...], al_ref[...], dt_ref[...])
        s = sp_ref[...]
        v_new = u - _hb(w, s)
        o = _hb(qd, s) + _hb(qk, v_new)
        _, post_vjp = jax.vjp(_gdn_post, o, _heads(z_ref[...]), g_ref[...])
        do, dz, dgain = post_vjp(_heads(dy_ref[...]).astype(F32))
        ds_next = ds_scr[...]
        d_vnew = _hb(qk, do, 1, 1) + _hb(kd, ds_next)
        d_qk = _hb(do, v_new, 2, 2)
        d_qd = _hb(do, s, 2, 2)
        d_kd = _hb(v_new, ds_next, 2, 2)
        d_dec = jnp.sum(jnp.sum(s * ds_next, axis=2, keepdims=True), axis=1, keepdims=True)
        ds_scr[...] = dec * ds_next + _hb(qd, do, 1, 1) - _hb(w, d_vnew, 1, 1)
        d_w = -_hb(d_vnew, s, 2, 2)
        dcq, dck, dcv, dab, dal, ddt = prep_vjp((d_vnew, d_w, d_qk, d_qd, d_kd, d_dec))
        for h in range(HEADS):
            dc_ref[:, h * HD:(h + 1) * HD] = dcq[h]
            dc_ref[:, GDN_W + h * HD:GDN_W + (h + 1) * HD] = dck[h]
            dc_ref[:, 2 * GDN_W + h * HD:2 * GDN_W + (h + 1) * HD] = dcv[h]
            dzab_ref[:, h * HD:(h + 1) * HD] = dz[h].astype(dzab_ref.dtype)
        dzab_ref[:, GDN_W:GDN_W + HD] = dab.astype(dzab_ref.dtype)
        dzab_ref[:, GDN_W + HD:] = jnp.zeros((GDN_CHUNK, AB_W - HD), dzab_ref.dtype)
        dal_ref[...] += dal
        ddt_ref[...] += ddt
        dg_ref[...] += dgain

    c = GDN_CHUNK
    return pl.pallas_call(
        body, name="gdn_bwd", grid=(nc,),
        in_specs=[wide(0), wide(GDN_W), wide(2 * GDN_W), zab, row, row, row, state, inverse, wide(0), ANY],
        out_specs=[pl.BlockSpec((c, 3 * GDN_W), lambda n: (nc - 1 - n, 0)), zab, row, row, row],
        out_shape=[_sds((t, 3 * GDN_W), F32), _sds(dproj.shape, dproj.dtype),
                   _sds((1, HD), F32), _sds((1, HD), F32), _sds((1, HD), F32)],
        input_output_aliases={10: 1},
        scratch_shapes=[pltpu.VMEM((HEADS, HD, HD), F32)], compiler_params=_cp("arbitrary"),
    )(cqkv, cqkv, cqkv, proj, a_log, dt_bias, gain, sprev, inverses, dy, dproj)


MERGE_TN = 512


def _merge_specs(t, tm):
    tn = MERGE_TN
    ys = [pl.BlockSpec((tm, wd), lambda i, j: (i, 0)) for wd in (POOL_W, SB_W, GDN_W)]
    ws = [pl.BlockSpec((None, wd, tn), lambda i, j: (j, 0, 0)) for wd in (POOL_W, SB_W, GDN_W)]
    gs = [pl.BlockSpec((tm, tn), functools.partial(lambda i, j, b: (i, OFF_GATE // tn + b * (D // tn) + j), b=b))
          for b in range(3)]
    out = pl.BlockSpec((tm, tn), lambda i, j: (i, j))
    return ys, ws, gs, out


def _merge_fwd(ys, wups, proj):
    t = proj.shape[0]
    tm = min(512, t)
    y_specs, w_specs, g_specs, out = _merge_specs(t, tm)

    def body(y0, y1, y2, w0, w1, w2, g0, g1, g2, o_ref):
        acc = jnp.zeros(o_ref.shape, F32)
        for y, w, g in ((y0, w0, g0), (y1, w1, g1), (y2, w2, g2)):
            acc = acc + jax.nn.sigmoid(g[...]) * _dot(y[...], w[...])
        o_ref[...] = acc.astype(o_ref.dtype)

    return pl.pallas_call(
        body, name="merge_fwd", grid=(t // tm, D // MERGE_TN), in_specs=[*y_specs, *w_specs, *g_specs],
        out_specs=out, out_shape=_sds((t, D), BF16), compiler_params=_cp("parallel", "parallel"),
    )(*ys, *wups, proj, proj, proj)


def _merge_bwd(ys, wups, proj, dmerged):
    t = proj.shape[0]
    tm = min(512, t)
    tn = MERGE_TN
    per = D // tn
    ys_specs = [pl.BlockSpec((tm, wd), lambda i, b, j: (i, 0)) for wd in (POOL_W, SB_W, GDN_W)]
    w_specs = [pl.BlockSpec((None, wd, tn), lambda i, b, j: (j, 0, 0)) for wd in (POOL_W, SB_W, GDN_W)]
    gate = pl.BlockSpec((tm, tn), lambda i, b, j: (i, OFF_GATE // tn + b * per + j))
    branch = pl.BlockSpec((tm, tn), lambda i, b, j: (i, b * per + j))
    merged = pl.BlockSpec((tm, tn), lambda i, b, j: (i, j))

    def body(y0, y1, y2, w0, w1, w2, g_ref, dm_ref, dg_ref, dmb_ref):
        b = pl.program_id(1)
        dm = dm_ref[...].astype(F32)
        sg = jax.nn.sigmoid(g_ref[...])
        dmb_ref[...] = (dm * sg).astype(dmb_ref.dtype)
        for k, (y, w) in enumerate(((y0, w0), (y1, w1), (y2, w2))):
            @pl.when(b == k)
            def _():
                dg_ref[...] = (dm * _dot(y[...], w[...]) * sg * (1.0 - sg)).astype(dg_ref.dtype)

    return pl.pallas_call(
        body, name="merge_bwd", grid=(t // tm, 3, per), in_specs=[*ys_specs, *w_specs, gate, merged],
        out_specs=[gate, branch], out_shape=[_sds((t, N_AL), BF16), _sds((t, 3 * D), BF16)],
        compiler_params=_cp("parallel", "arbitrary", "arbitrary"),
    )(*ys, *wups, proj, dmerged)


def _place(name, dproj, src, col):
    t, w = src.shape
    tt = min(512, t)

    def body(s_ref, _, o_ref):
        o_ref[...] = s_ref[...].astype(o_ref.dtype)

    return pl.pallas_call(
        body, name=name, grid=(t // tt,), in_specs=[pl.BlockSpec((tt, w), lambda i: (i, 0)), ANY],
        out_specs=pl.BlockSpec((tt, w), lambda i: (i, col // w)), out_shape=_sds(dproj.shape, dproj.dtype),
        input_output_aliases={1: 0}, compiler_params=_cp("parallel"),
    )(src, dproj)


def _tile(t, want):
    return min(t, want)


def _layer_fwd(x, l, gw, w_al, sp, reached=None):
    t = x.shape[0]
    tm = _tile(t, 1024)
    u = _rms_fwd("rms_attn", x, sp["attn_norm"][l])
    proj = _mm("proj", u, w_al, m=t, n=N_AL, k=D, tm=tm, tn=1024, tk=D, a_spec=_a_plain(tm, D),
               b_spec=_b_plain(D, 1024), dims=None, out_shapes=[_sds((t, N_AL), F32)], out_specs=[_o_plain(tm, 1024)])[0]
    if reached is not None:
        reached("proj", proj)
    y_pool = _pool_fwd(proj, sp["pool_w"][l], sp["pool_scale"][l])
    kv = _sb_cast_kv(proj)
    y_sb = _sb_fwd(proj, kv)
    cqkv = _conv_fwd(proj, sp["conv"][l])
    y_gdn, sprev, inverses = _gdn_fwd(cqkv, proj, sp["a_log"][l], sp["dt_bias"][l], sp["gdn_norm"][l])
    ys = (y_pool, y_sb, y_gdn)
    wups = (gw["w_pool_up"], gw["w_sb_up"], gw["w_gdn_up"])
    merged = _merge_fwd(ys, wups, proj)
    if reached is not None:
        reached("merged", merged)
    x1 = _mm("out_proj", merged, gw["w_out"], m=t, n=D, k=D, tm=tm, tn=1024, tk=512, a_spec=_a_plain(tm, 512),
             b_spec=_w_rows(512, 1024, 512), dims=None, out_shapes=[_sds((t, D), F32)], out_specs=[_o_plain(tm, 1024)],
             extras=[x], extra_specs=[_o_plain(tm, 1024)], epilogue=lambda r, xr: (r + xr,))[0]
    u2 = _rms_fwd("rms_mlp", x1, sp["mlp_norm"][l])

    def relu2(r):
        hv = jnp.maximum(r, 0.0)
        return hv, hv * hv

    hid, hid2 = _mm("ff1", u2, gw["w_ff1"], m=t, n=D_FF, k=D, tm=tm, tn=1024, tk=D, a_spec=_a_plain(tm, D),
                    b_spec=_w_cols(D, 1024, 2048), dims=None, out_shapes=[_sds((t, D_FF), BF16)] * 2,
                    out_specs=[_o_plain(tm, 1024)] * 2, epilogue=relu2)
    x2 = _mm("ff2", hid2, gw["w_ff2"], m=t, n=D, k=D_FF, tm=tm, tn=1024, tk=2048, a_spec=_a_plain(tm, 2048),
             b_spec=_w_rows(2048, 1024, 2048), dims=None, out_shapes=[_sds((t, D), F32)], out_specs=[_o_plain(tm, 1024)],
             extras=[x1], extra_specs=[_o_plain(tm, 1024)], epilogue=lambda r, xr: (r + xr,))[0]
    saved = dict(x=x, u=u, proj=proj, kv=kv, cqkv=cqkv, sprev=sprev, inverses=inverses, ys=ys, merged=merged, x1=x1, u2=u2, hid=hid, hid2=hid2)
    return x2, saved


def _layer_bwd(dx2, l, gw, w_al, sp, sv, emit=None):
    t = dx2.shape[0]
    tm = _tile(t, 1024)
    tk = t
    g = {}
    if emit is None:
        emit = lambda names, grads, v: v
    dpre = _mm("ff2_dx", dx2, gw["w_ff2"], m=t, n=D_FF, k=D, tm=tm, tn=1024, tk=D, a_spec=_a_plain(tm, D),
               b_spec=_w_rows_t(D, 1024, 2048), dims=NT, out_shapes=[_sds((t, D_FF), BF16)],
               out_specs=[_o_plain(tm, 1024)], extras=[sv["hid"]], extra_specs=[_o_plain(tm, 1024)],
               epilogue=lambda r, hv: (r * (2.0 * hv.astype(F32)),))[0]
    g["w_ff2"] = _mm("ff2_dw", sv["hid2"], dx2, m=D_FF, n=D, k=t, tm=1024, tn=1024, tk=tk, a_spec=_a_trans(1024, tk),
                     b_spec=_b_plain(tk, 1024), dims=TN, out_shapes=[_sds((D_FF, D), BF16)],
                     out_specs=[_o_plain(1024, 1024)])[0].reshape(N_CHIPS, D_FF // N_CHIPS, D)
    du2 = _mm("ff1_dx", dpre, gw["w_ff1"], m=t, n=D, k=D_FF, tm=tm, tn=1024, tk=2048, a_spec=_a_plain(tm, 2048),
              b_spec=_w_cols_t(2048, 1024, 2048), dims=NT, out_shapes=[_sds((t, D), F32)], out_specs=[_o_plain(tm, 1024)])[0]
    g["w_ff1"] = _mm("ff1_dw", sv["u2"], dpre, m=D, n=D_FF, k=t, tm=1024, tn=1024, tk=tk, a_spec=_a_trans(1024, tk),
                     b_spec=_b_plain(tk, 1024), dims=TN, out_shapes=[_sds((N_CHIPS, D, D_FF // N_CHIPS), BF16)],
                     out_specs=[_o_colshard(1024, 1024, D_FF // N_CHIPS)])[0]
    dx1, g["mlp_norm"] = _rms_bwd("rms_mlp_bwd", du2, sv["x1"], sp["mlp_norm"][l], dx2)
    dx1 = emit(("w_ff1", "w_ff2"), g, dx1)
    dmerged = _mm("out_dx", dx1, gw["w_out"], m=t, n=D, k=D, tm=tm, tn=512, tk=D, a_spec=_a_plain(tm, D),
                  b_spec=_w_rows_t(D, 512, 512), dims=NT, out_shapes=[_sds((t, D), BF16)], out_specs=[_o_plain(tm, 512)])[0]
    g["w_out"] = _mm("out_dw", sv["merged"], dx1, m=D, n=D, k=t, tm=1024, tn=1024, tk=tk, a_spec=_a_trans(1024, tk),
                     b_spec=_b_plain(tk, 1024), dims=TN, out_shapes=[_sds((D, D), BF16)],
                     out_specs=[_o_plain(1024, 1024)])[0].reshape(N_CHIPS, D // N_CHIPS, D)
    wups = (gw["w_pool_up"], gw["w_sb_up"], gw["w_gdn_up"])
    dproj, dm_all = _merge_bwd(sv["ys"], wups, sv["proj"], dmerged)
    per = D // MERGE_TN
    dys = []
    for b, (nm, yb, wd) in enumerate(zip(("w_pool_up", "w_sb_up", "w_gdn_up"), sv["ys"], (POOL_W, SB_W, GDN_W))):
        dm_rows = pl.BlockSpec((tm, MERGE_TN), functools.partial(lambda i, j, kk, b: (i, b * per + kk), b=b))
        dm_cols = pl.BlockSpec((tk, MERGE_TN), functools.partial(lambda i, j, kk, b: (kk, b * per + j), b=b))
        dys.append(_mm(nm + "_dx", dm_all, gw[nm], m=t, n=wd, k=D, tm=tm, tn=256, tk=MERGE_TN, a_spec=dm_rows,
                       b_spec=_w_cols_t(MERGE_TN, 256, 512), dims=NT, out_shapes=[_sds((t, wd), F32)],
                       out_specs=[_o_plain(tm, 256)])[0])
        g[nm] = _mm(nm + "_dw", yb, dm_all, m=wd, n=D, k=t, tm=256, tn=MERGE_TN, tk=tk, a_spec=_a_trans(256, tk),
                    b_spec=dm_cols, dims=TN, out_shapes=[_sds((N_CHIPS, wd, D // N_CHIPS), BF16)],
                    out_specs=[_o_colshard(256, MERGE_TN, D // N_CHIPS)])[0]
    dys[2] = emit(("w_pool_up", "w_sb_up", "w_gdn_up", "w_out"), g, dys[2])
    proj = sv["proj"]
    dproj, g["pool_w"], g["pool_scale"] = _pool_bwd(proj, sp["pool_w"][l], sp["pool_scale"][l], dys[0], dproj)
    for k, piece in enumerate(_sb_bwd(proj, sv["kv"], dys[1])):
        dproj = _place(f"place_sb_{k}", dproj, piece, OFF_SB + k * SB_W)
    dc, dproj, g["a_log"], g["dt_bias"], g["gdn_norm"] = _gdn_bwd(
        sv["cqkv"], proj, sp["a_log"][l], sp["dt_bias"][l], sp["gdn_norm"][l], sv["sprev"], sv["inverses"], dys[2], dproj)
    dproj, g["conv"] = _conv_bwd(proj, sp["conv"][l], dc, dproj)
    du = _mm("proj_dx", dproj, w_al, m=t, n=D, k=N_AL, tm=tm, tn=1024, tk=2048, a_spec=_a_plain(tm, 2048),
             b_spec=_b_trans(2048, 1024), dims=NT, out_shapes=[_sds((t, D), F32)], out_specs=[_o_plain(tm, 1024)])[0]
    g["w_al"] = _mm("proj_dw", sv["u"], dproj, m=D, n=N_AL, k=t, tm=1024, tn=1024, tk=tk, a_spec=_a_trans(1024, tk),
                    b_spec=_b_plain(tk, 1024), dims=TN, out_shapes=[_sds((D, N_AL), BF16)], out_specs=[_o_plain(1024, 1024)])[0]
    dx, g["attn_norm"] = _rms_bwd("rms_attn_bwd", du, sv["x"], sp["attn_norm"][l], dx1)
    g["w_in"] = _w_in_to_shards(g["w_al"])
    dx = emit(("w_in",), g, dx)
    return dx, g


W_IN_RUNS = ((0, ORIG_SB, OFF_P), (ORIG_SB, ORIG_Z, OFF_SB), (ORIG_Z, ORIG_GATE, OFF_Z), (ORIG_GATE, N_IN, OFF_GATE))
W_IN_SHARD = N_IN // N_CHIPS


def _w_in_from_shards(gathered):
    parts = []
    for lo, hi, al in sorted(W_IN_RUNS, key=lambda r: r[2]):
        if al == OFF_GATE:
            parts.append(jnp.zeros((D, OFF_GATE - (OFF_AB + ORIG_GATE - ORIG_AB)), gathered.dtype))
        while lo < hi:
            chip = lo // W_IN_SHARD
            end = min(hi, (chip + 1) * W_IN_SHARD)
            parts.append(gathered[chip, :, lo - chip * W_IN_SHARD:end - chip * W_IN_SHARD])
            lo = end
    return jnp.concatenate(parts, axis=1)


def _w_in_to_shards(g_al):
    shards = []
    for chip in range(N_CHIPS):
        a, b = chip * W_IN_SHARD, (chip + 1) * W_IN_SHARD
        parts = [g_al[:, al + max(a, lo) - lo:al + min(b, hi) - lo] for lo, hi, al in W_IN_RUNS if max(a, lo) < min(b, hi)]
        shards.append(jnp.concatenate(parts, axis=1))
    return jnp.stack(shards)


def _row128(v):
    return jnp.pad(v.reshape(1, -1), ((0, 0), (0, HD - v.shape[-1])))


def _local_step(x, target, weights_of, sp, emit=None, reached=None):
    saved, gw, w_in_al = [], [], []
    h = x
    for l in range(2):
        gw_l, w_al_l = weights_of(l)
        gw.append(gw_l)
        w_in_al.append(w_al_l)
        h, sv = _layer_fwd(h, l, gw_l, w_al_l, sp, None if reached is None else functools.partial(reached, l))
        saved.append(sv)
    loss, dh, g_final = _loss_head(h, sp["final_norm"], target)
    grads = [None, None]
    for l in (1, 0):
        dh, grads[l] = _layer_bwd(dh, l, gw[l], w_in_al[l], sp, saved[l],
                                  None if emit is None else functools.partial(emit, l))
    return loss, dh, grads, g_final


ANY = pl.BlockSpec(memory_space=pl.ANY)


def _me():
    return lax.axis_index("x"), lax.axis_index("y"), lax.axis_index("c")


def _other_chips(x, y):
    return [(1 - x, y), (x, 1 - y), (1 - x, 1 - y)]


def _half(ref, axis, c, rows):
    half = rows // 2
    idx = [slice(None)] * axis + [pl.ds(pl.multiple_of(c * half, 16), half)]
    return ref.at[tuple(idx)]


def _gather_steps(out, send, recv):
    n = len(out)
    x, y, c = _me()
    mine = 2 * x + y
    sibling = (x, y, 1 - c)
    chips = _other_chips(x, y)
    sends = []
    for t in range(n):
        rows = out[t].shape[1]
        for k, (px, py) in enumerate(chips):
            own_half = _half(out[t].at[mine], 0, c, rows)
            cp = pltpu.make_async_remote_copy(
                src_ref=own_half, dst_ref=own_half,
                send_sem=send.at[6 * t + k], recv_sem=recv.at[6 * t + k], device_id=(px, py, c), device_id_type=MESH)
            cp.start()
            sends.append(cp)
    for t in range(n):
        rows = out[t].shape[1]
        for k, (px, py) in enumerate(chips):
            landed = _half(out[t].at[2 * px + py], 0, c, rows)
            pltpu.make_async_remote_copy(
                src_ref=landed, dst_ref=landed, send_sem=send.at[6 * t + k], recv_sem=recv.at[6 * t + k],
                device_id=(px, py, c), device_id_type=MESH).wait_recv()
            cp = pltpu.make_async_remote_copy(
                src_ref=landed, dst_ref=landed, send_sem=send.at[6 * t + 3 + k], recv_sem=recv.at[6 * t + 3 + k],
                device_id=sibling, device_id_type=MESH)
            cp.start()
            sends.append(cp)
    for t in range(n):
        rows = out[t].shape[1]
        for k, (px, py) in enumerate(chips):
            other = _half(out[t].at[2 * px + py], 0, 1 - c, rows)
            pltpu.make_async_remote_copy(
                src_ref=other, dst_ref=other, send_sem=send.at[6 * t + 3 + k], recv_sem=recv.at[6 * t + 3 + k],
                device_id=sibling, device_id_type=MESH).wait_recv()
    for cp in sends:
        cp.wait_send()


def _gather_weights_async(bufs, tag, collective_id):
    n = len(bufs)
    refs = [jax.new_ref(b, memory_space=pltpu.MemorySpace.HBM) for b in bufs]

    @pl.kernel(mesh=plsc.ScalarSubcoreMesh(axis_name="sequencer", num_cores=1), name=f"gather_async_{tag}",
               scratch_types=(pltpu.SemaphoreType.DMA((6 * n,)), pltpu.SemaphoreType.DMA((6 * n,))),
               compiler_params=pltpu.CompilerParams(collective_id=collective_id))
    def launch(send, recv):
        x, y, c = _me()
        barrier = pltpu.get_barrier_semaphore()
        peers = [(x, y, 1 - c)] + [(px, py, c) for px, py in _other_chips(x, y)]
        for peer in peers:
            pl.semaphore_signal(barrier, inc=1, device_id=peer, device_id_type=MESH)
        pl.semaphore_wait(barrier, len(peers))
        _gather_steps(refs, send, recv)

    launch()
    return [r[...] for r in refs]


def _rs_pair(grads):
    n = len(grads)

    def body(*refs):
        g, out = refs[:n], refs[n:2 * n]
        send, recv = refs[2 * n:]
        x, y, c = _me()
        copies = []
        for t in range(n):
            cp = pltpu.make_async_remote_copy(
                src_ref=_half(g[t], 1, 1 - c, g[t].shape[1]), dst_ref=out[t], send_sem=send.at[t], recv_sem=recv.at[t],
                device_id=(x, y, 1 - c), device_id_type=MESH)
            cp.start()
            copies.append(cp)
        for cp in copies:
            cp.wait()

    return pl.pallas_call(
        body, name="rs_pair", in_specs=[ANY] * n, out_specs=[ANY] * n,
        out_shape=[_sds((N_CHIPS, s.shape[1] // 2, s.shape[2]), s.dtype) for s in grads],
        scratch_shapes=[pltpu.SemaphoreType.DMA((n,)), pltpu.SemaphoreType.DMA((n,))],
    )(*grads)


def _rs_chips_steps(p, out, send, recv):
    x, y, c = _me()
    copies = []
    for t in range(len(p)):
        for k, (px, py) in enumerate(_other_chips(x, y)):
            cp = pltpu.make_async_remote_copy(
                src_ref=p[t].at[2 * px + py], dst_ref=out[t].at[k], send_sem=send.at[3 * t + k],
                recv_sem=recv.at[3 * t + k], device_id=(px, py, c), device_id_type=MESH)
            cp.start()
            copies.append(cp)
    for cp in copies:
        cp.wait()


def _rs_chips_async(parts, tag, collective_id):
    n = len(parts)
    src = [jax.new_ref(p, memory_space=pltpu.MemorySpace.HBM) for p in parts]
    got = [jax.empty_ref(_sds((3, *p.shape[1:]), p.dtype), memory_space=pltpu.MemorySpace.HBM) for p in parts]

    @pl.kernel(mesh=plsc.ScalarSubcoreMesh(axis_name="sequencer", num_cores=1), name=f"rs_chips_async_{tag}",
               scratch_types=(pltpu.SemaphoreType.DMA((3 * n,)), pltpu.SemaphoreType.DMA((3 * n,))),
               compiler_params=pltpu.CompilerParams(collective_id=collective_id))
    def launch(send, recv):
        x, y, c = _me()
        barrier = pltpu.get_barrier_semaphore()
        peers = [(px, py, c) for px, py in _other_chips(x, y)]
        for peer in peers:
            pl.semaphore_signal(barrier, inc=1, device_id=peer, device_id_type=MESH)
        pl.semaphore_wait(barrier, len(peers))
        _rs_chips_steps(src, got, send, recv)

    launch()
    return [g[...] for g in got]


def _pair_exchange(bufs):
    n = len(bufs)

    def body(*refs):
        out = refs[n:2 * n]
        send, recv = refs[2 * n:]
        x, y, c = _me()
        copies = []
        for t in range(n):
            cp = pltpu.make_async_remote_copy(
                src_ref=out[t].at[c], dst_ref=out[t].at[c], send_sem=send.at[t], recv_sem=recv.at[t],
                device_id=(x, y, 1 - c), device_id_type=MESH)
            cp.start()
            copies.append(cp)
        for t, cp in enumerate(copies):
            cp.wait_send()
            pltpu.make_async_remote_copy(
                src_ref=out[t].at[1 - c], dst_ref=out[t].at[1 - c], send_sem=send.at[t], recv_sem=recv.at[t],
                device_id=(x, y, 1 - c), device_id_type=MESH).wait_recv()

    return pl.pallas_call(
        body, name="pair_exchange", in_specs=[ANY] * n, out_specs=[ANY] * n,
        out_shape=[_sds(s.shape, s.dtype) for s in bufs], input_output_aliases={t: t for t in range(n)},
        scratch_shapes=[pltpu.SemaphoreType.DMA((n,)), pltpu.SemaphoreType.DMA((n,))],
    )(*bufs)


def _row_tile(rows, cols, itemsize, budget=2 * 1024 * 1024):
    tr = rows
    while tr * cols * itemsize > budget and tr % 32 == 0:
        tr //= 2
    return tr


def _sum_pair(name, g, got, where):
    nchip, rows, cols = g.shape
    half = rows // 2
    tr = _row_tile(half, cols, 4)
    per = half // tr

    def body(w_ref, g_ref, r_ref, o_ref):
        o_ref[...] = (g_ref[...].astype(F32) + r_ref[...].astype(F32)).astype(o_ref.dtype)

    blk = pl.BlockSpec((None, tr, cols), lambda j, i, w_ref: (j, i, 0))
    return pl.pallas_call(
        body, name=name,
        grid_spec=pltpu.PrefetchScalarGridSpec(
            num_scalar_prefetch=1, grid=(nchip, per),
            in_specs=[pl.BlockSpec((None, tr, cols), lambda j, i, w_ref: (j, w_ref[1] * per + i, 0)), blk], out_specs=blk),
        out_shape=_sds((nchip, half, cols), BF16), compiler_params=_cp("parallel", "parallel"),
    )(where, g, got)


def _sum_chips(name, p, got, where):
    _, rows, cols = p.shape
    tr = _row_tile(rows, cols, 4)

    def body(w_ref, p_ref, r0, r1, r2, o_ref):
        o_ref[...] = ((p_ref[...].astype(F32) + r0[...].astype(F32)) + r1[...].astype(F32)) + r2[...].astype(F32)

    def got_k(k):
        return pl.BlockSpec((None, tr, cols), lambda i, w_ref: (k, i, 0))

    return pl.pallas_call(
        body, name=name,
        grid_spec=pltpu.PrefetchScalarGridSpec(
            num_scalar_prefetch=1, grid=(rows // tr,),
            in_specs=[pl.BlockSpec((None, tr, cols), lambda i, w_ref: (w_ref[0], i, 0)), got_k(0), got_k(1), got_k(2)],
            out_specs=pl.BlockSpec((None, tr, cols), lambda i, w_ref: (w_ref[1], i, 0))),
        out_shape=_sds((2, rows, cols), F32), compiler_params=_cp("parallel"),
    )(where, p, got, got, got)


def _rs_begin(grads, where, tag, collective_id):
    got = _rs_pair(grads)
    parts = [_sum_pair(f"sum_pair_{t}", g, r, where) for t, (g, r) in enumerate(zip(grads, got))]
    return parts, _rs_chips_async(parts, tag, collective_id)


def _rs_finish(parts, got, where):
    halves = [_sum_chips(f"sum_chips_{t}", p, r, where) for t, (p, r) in enumerate(zip(parts, got))]
    return _pair_exchange(halves)


def _all_reduce_small(name, v):
    rows = v.shape[0]

    def body(v_ref, o_ref, land, send, recv):
        x, y, c = _me()
        mine = 4 * x + 2 * y + c
        copies = []
        for k in range(1, 8):
            kx, ky, kc = k >> 2, (k >> 1) & 1, k & 1
            peer = (x ^ kx, y ^ ky, c ^ kc)
            cp = pltpu.make_async_remote_copy(
                src_ref=v_ref, dst_ref=land.at[mine], send_sem=send.at[k - 1], recv_sem=recv.at[k - 1],
                device_id=peer, device_id_type=MESH)
            cp.start()
            copies.append(cp)
        land[mine] = v_ref[...]
        for k in range(1, 8):
            kx, ky, kc = k >> 2, (k >> 1) & 1, k & 1
            src = 4 * (x ^ kx) + 2 * (y ^ ky) + (c ^ kc)
            pltpu.make_async_remote_copy(
                src_ref=v_ref, dst_ref=land.at[src], send_sem=send.at[k - 1], recv_sem=recv.at[k - 1],
                device_id=(x ^ kx, y ^ ky, c ^ kc), device_id_type=MESH).wait_recv()
        acc = land[0]
        for d in range(1, 8):
            acc = acc + land[d]
        o_ref[...] = acc
        for cp in copies:
            cp.wait_send()

    vm = pl.BlockSpec(memory_space=pltpu.VMEM)
    return pl.pallas_call(
        body, name=name, in_specs=[vm], out_specs=vm, out_shape=_sds((rows, 128), F32),
        scratch_shapes=[pltpu.VMEM((8, rows, 128), F32), pltpu.SemaphoreType.DMA((7,)), pltpu.SemaphoreType.DMA((7,))],
    )(v)


def _adamw(name, w, g, m, v):
    rows, cols = w.shape
    tr = _row_tile(rows, cols, 4, budget=1024 * 1024)
    c1 = 1.0 / (1.0 - ADAM_B1 ** ADAM_STEP)
    c2 = 1.0 / (1.0 - ADAM_B2 ** ADAM_STEP)

    def body(w_ref, g_ref, m_ref, v_ref, d_ref, nm_ref, nv_ref):
        gv = g_ref[...]
        nm = ADAM_B1 * m_ref[...] + (1.0 - ADAM_B1) * gv
        nv = ADAM_B2 * v_ref[...] + (1.0 - ADAM_B2) * (gv * gv)
        d_ref[...] = -ADAM_LR * ((nm * c1) / (jnp.sqrt(nv * c2) + ADAM_EPS) + ADAM_WD * w_ref[...])
        nm_ref[...] = nm
        nv_ref[...] = nv

    blk = pl.BlockSpec((tr, cols), lambda i: (i, 0))
    return pl.pallas_call(
        body, name=name, grid=(rows // tr,), in_specs=[blk] * 4, out_specs=[blk] * 3,
        out_shape=[_sds((rows, cols), F32)] * 3, compiler_params=_cp("parallel"),
    )(w, g, m, v)


def _adamw_layers(name, w, g0, g1, m, v):
    _, half, cols = g0.shape
    tr = _row_tile(half, cols, 4, budget=1024 * 1024)
    per_half = half // tr
    per = 2 * per_half
    c1 = 1.0 / (1.0 - ADAM_B1 ** ADAM_STEP)
    c2 = 1.0 / (1.0 - ADAM_B2 ** ADAM_STEP)

    def body(w_ref, g0_ref, g1_ref, m_ref, v_ref, g_ref, d_ref, nm_ref, nv_ref):
        gv = jnp.where(pl.program_id(0) == 0, g0_ref[...], g1_ref[...])
        nm = ADAM_B1 * m_ref[...] + (1.0 - ADAM_B1) * gv
        nv = ADAM_B2 * v_ref[...] + (1.0 - ADAM_B2) * (gv * gv)
        g_ref[...] = gv
        d_ref[...] = -ADAM_LR * ((nm * c1) / (jnp.sqrt(nv * c2) + ADAM_EPS) + ADAM_WD * w_ref[...])
        nm_ref[...] = nm
        nv_ref[...] = nv

    both = pl.BlockSpec((None, tr, cols), lambda l, i: (l, i, 0))

    def halves(i):
        return i // per_half, i % per_half, 0

    first = pl.BlockSpec((None, tr, cols), lambda l, i: halves(i * (1 - l) + (per - 1) * l))
    second = pl.BlockSpec((None, tr, cols), lambda l, i: halves(i * l))
    return pl.pallas_call(
        body, name=name, grid=(2, per), in_specs=[both, first, second, both, both], out_specs=[both] * 4,
        out_shape=[_sds(w.shape, F32)] * 4, compiler_params=_cp("arbitrary", "arbitrary"),
    )(w, g0, g1, m, v)


def _to_bf16_slot(name, w, l, where):
    _, rows, cols = w.shape
    tr = _row_tile(rows, cols, 4)

    def body(w_ref, x_ref, o_ref):
        o_ref[...] = x_ref[...].astype(BF16)

    return pl.pallas_call(
        body, name=name,
        grid_spec=pltpu.PrefetchScalarGridSpec(
            num_scalar_prefetch=1, grid=(rows // tr,), in_specs=[pl.BlockSpec((None, tr, cols), lambda i, w_ref: (l, i, 0))],
            out_specs=pl.BlockSpec((None, tr, cols), lambda i, w_ref: (w_ref[0], i, 0))),
        out_shape=_sds((N_CHIPS, rows, cols), BF16), compiler_params=_cp("parallel"))(where, w)


BIG = ("w_in", "w_pool_up", "w_sb_up", "w_gdn_up", "w_out", "w_ff1", "w_ff2")
SMALL = (("attn_norm", (D,)), ("pool_w", (4, 128, 128)), ("pool_scale", (POOL_W,)), ("gdn_a_log", (HEADS,)),
         ("gdn_dt_bias", (HEADS,)), ("gdn_norm", (HD,)), ("mlp_norm", (D,)))


PACK_TILE = 8 * 128


def _rows128(a):
    flat = a.reshape(-1)
    pad = (-flat.shape[0]) % PACK_TILE
    return jnp.pad(flat, (0, pad)).reshape(-1, 128)


def _pack(parts):
    packed = jnp.concatenate([_rows128(p) for p in parts], axis=0)
    return jnp.pad(packed, ((0, (-packed.shape[0]) % 8), (0, 0)))


def _unpack(packed, shapes):
    out, r = [], 0
    for shp in shapes:
        size = 1
        for s in shp:
            size *= s
        nr = -(-size // PACK_TILE) * 8
        out.append(packed[r:r + nr].reshape(-1)[:size].reshape(shp))
        r += nr
    return out


def kernel(x, attn_norm, w_in, pool_w, pool_scale, gdn_conv, gdn_a_log, gdn_dt_bias, gdn_norm, w_pool_up, w_sb_up, w_gdn_up, w_out, mlp_norm, w_ff1, w_ff2, final_norm, loss_target, m_attn_norm, m_w_in, m_pool_w, m_pool_scale, m_gdn_conv, m_gdn_a_log, m_gdn_dt_bias, m_gdn_norm, m_w_pool_up, m_w_sb_up, m_w_gdn_up, m_w_out, m_mlp_norm, m_w_ff1, m_w_ff2, m_final_norm, v_attn_norm, v_w_in, v_pool_w, v_pool_scale, v_gdn_conv, v_gdn_a_log, v_gdn_dt_bias, v_gdn_norm, v_w_pool_up, v_w_sb_up, v_w_gdn_up, v_w_out, v_mlp_norm, v_w_ff1, v_w_ff2, v_final_norm):
    weights = dict(attn_norm=attn_norm, w_in=w_in, pool_w=pool_w, pool_scale=pool_scale, gdn_conv=gdn_conv,
                   gdn_a_log=gdn_a_log, gdn_dt_bias=gdn_dt_bias, gdn_norm=gdn_norm, w_pool_up=w_pool_up, w_sb_up=w_sb_up,
                   w_gdn_up=w_gdn_up, w_out=w_out, mlp_norm=mlp_norm, w_ff1=w_ff1, w_ff2=w_ff2, final_norm=final_norm)
    mom1 = dict(attn_norm=m_attn_norm, w_in=m_w_in, pool_w=m_pool_w, pool_scale=m_pool_scale, gdn_conv=m_gdn_conv,
                gdn_a_log=m_gdn_a_log, gdn_dt_bias=m_gdn_dt_bias, gdn_norm=m_gdn_norm, w_pool_up=m_w_pool_up,
                w_sb_up=m_w_sb_up, w_gdn_up=m_w_gdn_up, w_out=m_w_out, mlp_norm=m_mlp_norm, w_ff1=m_w_ff1, w_ff2=m_w_ff2,
                final_norm=m_final_norm)
    mom2 = dict(attn_norm=v_attn_norm, w_in=v_w_in, pool_w=v_pool_w, pool_scale=v_pool_scale, gdn_conv=v_gdn_conv,
                gdn_a_log=v_gdn_a_log, gdn_dt_bias=v_gdn_dt_bias, gdn_norm=v_gdn_norm, w_pool_up=v_w_pool_up,
                w_sb_up=v_w_sb_up, w_gdn_up=v_w_gdn_up, w_out=v_w_out, mlp_norm=v_mlp_norm, w_ff1=v_w_ff1, w_ff2=v_w_ff2,
                final_norm=v_final_norm)
    xi, yi, ci = lax.axis_index("x"), lax.axis_index("y"), lax.axis_index("c")
    chip = 2 * xi + yi
    where = jnp.stack([chip, ci]).astype(jnp.int32)

    bufs = [[_to_bf16_slot(f"cast_{nm}_{l}", weights[nm], l, where) for nm in BIG] for l in range(2)]
    first = _gather_weights_async(bufs[0][:1], "0_w_in", 4)
    first, _ = lax.optimization_barrier((first, (bufs[0][1:], bufs[1])))
    rest, _ = lax.optimization_barrier((bufs[0][1:], first))
    gw = [dict(zip(BIG, list(first) + _gather_weights_async(rest, "0_rest", 1))), {}]

    def reached(l, stage, value):
        if l == 0 and stage == "proj":
            later, _ = lax.optimization_barrier((bufs[1][:1], value))
            gw[1]["w_in"] = _gather_weights_async(later, "1_w_in", 2)[0]
        if l == 0 and stage == "merged":
            later, _ = lax.optimization_barrier((bufs[1][1:], value))
            gw[1].update(zip(BIG[1:], _gather_weights_async(later, "1_rest", 3)))

    def weights_of(l):
        return gw[l], _w_in_from_shards(gw[l]["w_in"])

    conv_cols = gdn_conv.shape[-1]
    conv_place = lax.dynamic_update_slice(jnp.zeros((2, GDN_CONV, N_CHIPS * conv_cols), F32),
                                          jnp.where(ci == 0, gdn_conv, 0.0), (0, 0, chip * conv_cols))
    conv_full = _all_reduce_small("gather_conv", _rows128(conv_place)).reshape(2, GDN_CONV, N_CHIPS * conv_cols)
    sp = dict(attn_norm=attn_norm.reshape(2, 1, D), pool_w=pool_w, pool_scale=pool_scale.reshape(2, 1, POOL_W),
              conv=conv_full, a_log=jnp.stack([_row128(gdn_a_log[l]) for l in range(2)]),
              dt_bias=jnp.stack([_row128(gdn_dt_bias[l]) for l in range(2)]), gdn_norm=gdn_norm.reshape(2, 1, HD),
              mlp_norm=mlp_norm.reshape(2, 1, D), final_norm=final_norm.reshape(1, D))

    started = []

    def emit(l, names, g, v):
        parts, got = _rs_begin([g[nm] for nm in names], where, f"{l}_{names[0]}", 5 + len(started))
        started.append((l, names, parts, got))
        v, _ = lax.optimization_barrier((v, parts))
        return v

    loss, grad_x, grads, g_final = _local_step(x[0], loss_target[0], weights_of, sp, emit, reached)
    big_grads = {nm: [None, None] for nm in BIG}
    for l, names, parts, got in started[:-1]:
        got, _ = lax.optimization_barrier((got, grad_x))
        for nm, red in zip(names, _rs_finish(parts, got, where)):
            big_grads[nm][l] = red
    small_parts, small_shapes = [], []
    for l in range(2):
        g = grads[l]
        for nm, shp in SMALL:
            key = {"gdn_a_log": "a_log", "gdn_dt_bias": "dt_bias"}.get(nm, nm)
            val = g[key]
            small_parts.append(val[0, :HEADS] if nm in ("gdn_a_log", "gdn_dt_bias") else val)
            small_shapes.append(shp)
        small_parts.append(g["conv"])
        small_shapes.append((GDN_CONV, N_CHIPS * conv_cols))
    small_parts += [g_final, loss]
    small_shapes += [(D,), (1, 1)]
    small_pack = _pack(small_parts)

    grad, delta, new_m, new_v = {}, {}, {}, {}
    for nm in BIG[1:]:
        grad[nm], delta[nm], new_m[nm], new_v[nm] = _adamw_layers("adamw_" + nm, weights[nm], *big_grads[nm], mom1[nm], mom2[nm])
    l, names, parts, got = started[-1]
    got, _ = lax.optimization_barrier((got, [new_v[nm] for nm in BIG[1:]]))
    for nm, red in zip(names, _rs_finish(parts, got, where)):
        big_grads[nm][l] = red
    nm = BIG[0]
    grad[nm], delta[nm], new_m[nm], new_v[nm] = _adamw_layers("adamw_" + nm, weights[nm], *big_grads[nm], mom1[nm], mom2[nm])
    small_pack, _ = lax.optimization_barrier((small_pack, new_v[nm]))
    reduced = _unpack(_all_reduce_small("reduce_small", small_pack), small_shapes)
    per = len(SMALL) + 1
    for i, (nm, _) in enumerate(SMALL):
        grad[nm] = jnp.stack([reduced[i], reduced[per + i]])
    conv_g = jnp.stack([reduced[per - 1], reduced[2 * per - 1]])
    grad["gdn_conv"] = lax.dynamic_slice(conv_g, (0, 0, chip * conv_cols), (2, GDN_CONV, conv_cols))
    grad["final_norm"] = reduced[-2]
    loss = reduced[-1][0, 0]
    small_names = [nm for nm, _ in SMALL] + ["gdn_conv", "final_norm"]
    packs = [_pack([src[nm] for nm in small_names]) for src in (weights, grad, mom1, mom2)]
    outs = _adamw("adamw_small", *packs)
    shapes = [weights[nm].shape for nm in small_names]
    for dst, packed in zip((delta, new_m, new_v), outs):
        for nm, val in zip(small_names, _unpack(packed, shapes)):
            dst[nm] = val

    order = ("attn_norm", "w_in", "pool_w", "pool_scale", "gdn_conv", "gdn_a_log", "gdn_dt_bias", "gdn_norm", "w_pool_up",
             "w_sb_up", "w_gdn_up", "w_out", "mlp_norm", "w_ff1", "w_ff2", "final_norm")
    return (loss, grad_x[None], *[grad[n] for n in order], *[delta[n] for n in order], *[new_m[n] for n in order],
            *[new_v[n] for n in order])
```

```python
import functools

import jax
import jax.numpy as jnp
from jax import lax
from jax.experimental import pallas as pl
from jax.experimental.pallas import tpu as pltpu
from jax.experimental.pallas import tpu_sc as plsc

F32, BF16 = jnp.float32, jnp.bfloat16
HIGH = lax.Precision.HIGH
MESH = pl.DeviceIdType.MESH

D = 2048
EPS = 1e-6
POOL_WINDOWS = (2, 4, 8, 16)
POOL_W, SB_W, GDN_W = 512, 768, 768
HEADS, HD = 6, 128
SB_BLOCK = 128
GDN_CHUNK = 64
D_FF = 4 * D
N_IN = 12044
N_CHIPS = 4
OFF_SB, OFF_GQKV, OFF_P, OFF_Z, OFF_AB, OFF_GATE = 0, 2304, 4608, 5120, 5888, 6144
AB_W = 256
ZAB_W = GDN_W + AB_W
ORIG_SB, ORIG_Z, ORIG_AB, ORIG_GATE = 512, 5120, 5888, 5900
N_AL = 12288
VMEM_LIMIT = 48 * 1024 * 1024

ADAM_LR, ADAM_B1, ADAM_B2, ADAM_EPS, ADAM_WD, ADAM_STEP = 0.001, 0.9, 0.999, 1e-08, 0.01, 10

NT = (((1,), (1,)), ((), ()))
TN = (((0,), (0,)), ((), ()))


def _cp(*sem):
    return pltpu.CompilerParams(dimension_semantics=sem, vmem_limit_bytes=VMEM_LIMIT)


def _dot(a, b, dims=None, precision=None):
    if dims is None:
        dims = (((a.ndim - 1,), (0,)), ((), ()))
    return lax.dot_general(a, b, dims, precision=precision, preferred_element_type=F32)


def _hdot(a, b, dims=None):
    return _dot(a, b, dims, precision=HIGH)


def _bdot(a, b, dims=None):
    return _dot(a.astype(BF16), b.astype(BF16), dims)


def _mm(name, a, b, *, m, n, k, tm, tn, tk, a_spec, b_spec, dims, out_shapes, out_specs,
        extras=(), extra_specs=(), epilogue=None):
    nk = k // tk
    ne, no = len(extras), len(out_shapes)

    def body(*refs):
        a_ref, b_ref = refs[0], refs[1]
        ex = refs[2:2 + ne]
        outs = refs[2 + ne:2 + ne + no]
        kk = pl.program_id(2)

        def finish(r):
            res = epilogue(r, *[e[...] for e in ex]) if epilogue is not None else (r,)
            for o, v in zip(outs, res):
                o[...] = v.astype(o.dtype)

        part = _dot(a_ref[...].astype(BF16), b_ref[...].astype(BF16), dims)
        if nk == 1:
            finish(part)
            return
        acc = refs[-1]

        @pl.when(kk == 0)
        def _():
            acc[...] = part

        @pl.when((kk > 0) & (kk < nk - 1))
        def _():
            acc[...] += part

        @pl.when(kk == nk - 1)
        def _():
            finish(acc[...] + part)

    return pl.pallas_call(
        body, name=name, grid=(m // tm, n // tn, nk),
        in_specs=[a_spec, b_spec, *extra_specs], out_specs=out_specs, out_shape=out_shapes,
        scratch_shapes=[] if nk == 1 else [pltpu.VMEM((tm, tn), F32)],
        compiler_params=_cp("parallel", "parallel", "arbitrary"),
    )(a, b, *extras)


def _a_plain(tm, tk):
    return pl.BlockSpec((tm, tk), lambda i, j, kk: (i, kk))


def _a_trans(tm, tk):
    return pl.BlockSpec((tk, tm), lambda i, j, kk: (kk, i))


def _b_plain(tk, tn):
    return pl.BlockSpec((tk, tn), lambda i, j, kk: (kk, j))


def _b_trans(tk, tn):
    return pl.BlockSpec((tn, tk), lambda i, j, kk: (j, kk))


def _o_plain(tm, tn):
    return pl.BlockSpec((tm, tn), lambda i, j, kk: (i, j))


def _o_colshard(tm, tn, ns_cols):
    per = ns_cols // tn
    return pl.BlockSpec((None, tm, tn), lambda i, j, kk: (j // per, i, j % per))


def _w_cols(tk, tn, ns):
    per = ns // tn
    return pl.BlockSpec((None, tk, tn), lambda i, j, kk: (j // per, kk, j % per))


def _w_cols_t(tk, tn, ns):
    per = ns // tk
    return pl.BlockSpec((None, tn, tk), lambda i, j, kk: (kk // per, j, kk % per))


def _w_rows(tk, tn, ks):
    per = ks // tk
    return pl.BlockSpec((None, tk, tn), lambda i, j, kk: (kk // per, kk % per, j))


def _w_rows_t(tk, tn, ks):
    per = ks // tn
    return pl.BlockSpec((None, tn, tk), lambda i, j, kk: (j // per, j % per, kk))


def _sds(shape, dtype):
    return jax.ShapeDtypeStruct(shape, dtype)


def _rms_fwd(name, x, gain):
    t = x.shape[0]
    tt = min(256, t)

    def body(x_ref, g_ref, u_ref):
        xv = x_ref[...]
        r = lax.rsqrt(jnp.mean(xv * xv, axis=-1, keepdims=True) + EPS)
        u_ref[...] = (xv * r * g_ref[...]).astype(u_ref.dtype)

    return pl.pallas_call(
        body, name=name, grid=(t // tt,),
        in_specs=[pl.BlockSpec((tt, D), lambda i: (i, 0)), pl.BlockSpec((1, D), lambda i: (0, 0))],
        out_specs=pl.BlockSpec((tt, D), lambda i: (i, 0)), out_shape=_sds((t, D), BF16),
        compiler_params=_cp("parallel"),
    )(x, gain)


def _rms_bwd(name, du, x, gain, dres):
    t = x.shape[0]
    tt = min(256, t)

    def body(du_ref, x_ref, g_ref, dres_ref, dx_ref, dg_ref):
        @pl.when(pl.program_id(0) == 0)
        def _():
            dg_ref[...] = jnp.zeros_like(dg_ref)

        xv, duv = x_ref[...], du_ref[...]
        r = lax.rsqrt(jnp.mean(xv * xv, axis=-1, keepdims=True) + EPS)
        nx = xv * r
        dn = duv * g_ref[...]
        dg_ref[...] += jnp.sum(duv * nx, axis=0, keepdims=True)
        dx_ref[...] = dres_ref[...] + r * (dn - nx * jnp.mean(dn * nx, axis=-1, keepdims=True))

    row = pl.BlockSpec((tt, D), lambda i: (i, 0))
    vec = pl.BlockSpec((1, D), lambda i: (0, 0))
    return pl.pallas_call(
        body, name=name, grid=(t // tt,), in_specs=[row, row, vec, row], out_specs=[row, vec],
        out_shape=[_sds((t, D), F32), _sds((1, D), F32)], compiler_params=_cp("arbitrary"),
    )(du, x, gain, dres)


def _loss_head(x, gain, target):
    t = x.shape[0]
    tt = min(256, t)

    def body(x_ref, g_ref, t_ref, loss_ref, dx_ref, dg_ref):
        @pl.when(pl.program_id(0) == 0)
        def _():
            dg_ref[...] = jnp.zeros_like(dg_ref)
            loss_ref[...] = jnp.zeros_like(loss_ref)

        xv = x_ref[...]
        r = lax.rsqrt(jnp.mean(xv * xv, axis=-1, keepdims=True) + EPS)
        nx = xv * r
        err = nx * g_ref[...] - t_ref[...]
        loss_ref[...] += 0.5 * jnp.sum(jnp.mean(err * err, axis=-1, keepdims=True), axis=0, keepdims=True)
        dy = err * (1.0 / D)
        dn = dy * g_ref[...]
        dg_ref[...] += jnp.sum(dy * nx, axis=0, keepdims=True)
        dx_ref[...] = r * (dn - nx * jnp.mean(dn * nx, axis=-1, keepdims=True))

    row = pl.BlockSpec((tt, D), lambda i: (i, 0))
    vec = pl.BlockSpec((1, D), lambda i: (0, 0))
    one = pl.BlockSpec((1, 1), lambda i: (0, 0))
    return pl.pallas_call(
        body, name="loss_head", grid=(t // tt,), in_specs=[row, vec, row], out_specs=[one, row, vec],
        out_shape=[_sds((1, 1), F32), _sds((t, D), F32), _sds((1, D), F32)], compiler_params=_cp("arbitrary"),
    )(x, gain, target)


def _shift_down(v, s, t_idx):
    return jnp.where(t_idx >= s, pltpu.roll(v, s, 0), 0.0)


def _shift_up(v, s, t_idx, t):
    return jnp.where(t_idx < t - s, pltpu.roll(v, t - s, 0), 0.0)


def _pool_d(p, g, t_idx):
    s = p
    for step in range(g + 1):
        s = s + _shift_down(s, 1 << step, t_idx)
    cnt = jnp.minimum(t_idx + 1, POOL_WINDOWS[g]).astype(F32)
    return s / cnt - p, cnt


def _pool_fwd(proj, pool_w, pool_scale):
    t = proj.shape[0]
    g128 = POOL_W // len(POOL_WINDOWS)

    def body(p_ref, w_ref, s_ref, y_ref):
        t_idx = lax.broadcasted_iota(jnp.int32, (t, g128), 0)
        for g in range(len(POOL_WINDOWS)):
            sl = slice(g * g128, (g + 1) * g128)
            d, _ = _pool_d(p_ref[:, sl], g, t_idx)
            y_ref[:, sl] = (_bdot(d, w_ref[g]) * s_ref[:, sl]).astype(y_ref.dtype)

    return pl.pallas_call(
        body, name="pool_fwd", grid=(1,),
        in_specs=[pl.BlockSpec((t, POOL_W), lambda i: (0, OFF_P // POOL_W)),
                  pl.BlockSpec((4, g128, g128), lambda i: (0, 0, 0)), pl.BlockSpec((1, POOL_W), lambda i: (0, 0))],
        out_specs=pl.BlockSpec((t, POOL_W), lambda i: (0, 0)), out_shape=_sds((t, POOL_W), BF16),
        compiler_params=_cp("arbitrary"),
    )(proj, pool_w, pool_scale)


def _pool_bwd(proj, pool_w, pool_scale, dy, dproj):
    t = proj.shape[0]
    g128 = POOL_W // len(POOL_WINDOWS)

    def body(p_ref, w_ref, s_ref, dy_ref, _, dp_ref, dw_ref, ds_ref):
        t_idx = lax.broadcasted_iota(jnp.int32, (t, g128), 0)
        for g in range(len(POOL_WINDOWS)):
            sl = slice(g * g128, (g + 1) * g128)
            d, cnt = _pool_d(p_ref[:, sl], g, t_idx)
            dyv = dy_ref[:, sl].astype(F32)
            ds_ref[:, sl] = jnp.sum(dyv * _bdot(d, w_ref[g]), axis=0, keepdims=True)
            dys = dyv * s_ref[:, sl]
            dw_ref[g] = _bdot(d, dys, TN)
            dd = _bdot(dys, w_ref[g], NT)
            s = dd / cnt
            for step in range(g + 1):
                s = s + _shift_up(s, 1 << step, t_idx, t)
            dp_ref[:, sl] = (s - dd).astype(dp_ref.dtype)

    return pl.pallas_call(
        body, name="pool_bwd", grid=(1,),
        in_specs=[pl.BlockSpec((t, POOL_W), lambda i: (0, OFF_P // POOL_W)),
                  pl.BlockSpec((4, g128, g128), lambda i: (0, 0, 0)), pl.BlockSpec((1, POOL_W), lambda i: (0, 0)),
                  pl.BlockSpec((t, POOL_W), lambda i: (0, 0)), ANY],
        out_specs=[pl.BlockSpec((t, POOL_W), lambda i: (0, OFF_P // POOL_W)), pl.BlockSpec((4, g128, g128), lambda i: (0, 0, 0)),
                   pl.BlockSpec((1, POOL_W), lambda i: (0, 0))],
        out_shape=[_sds(dproj.shape, dproj.dtype), _sds((4, g128, g128), F32), _sds((1, POOL_W), F32)],
        input_output_aliases={4: 0}, compiler_params=_cp("arbitrary"),
    )(proj, pool_w, pool_scale, dy, dproj)


SB_GROUP = 3
SB_GW = SB_GROUP * HD


def _sb_cast_kv(proj):
    t = proj.shape[0]
    tt = min(512, t)

    def body(x_ref, o_ref):
        o_ref[...] = x_ref[...].astype(BF16)

    return pl.pallas_call(
        body, name="sb_cast_kv", grid=(t // tt, 2),
        in_specs=[pl.BlockSpec((tt, SB_W), lambda i, j: (i, OFF_SB // SB_W + 1 + j))],
        out_specs=pl.BlockSpec((tt, SB_W), lambda i, j: (i, j)), out_shape=_sds((t, 2 * SB_W), BF16),
        compiler_params=_cp("parallel", "parallel"),
    )(proj)


def _sb_specs(t):
    q_spec = pl.BlockSpec((SB_BLOCK, SB_GW), lambda g, i: (i, OFF_SB // SB_GW + g))
    k_spec = pl.BlockSpec((t, SB_GW), lambda g, i: (0, g))
    v_spec = pl.BlockSpec((t, SB_GW), lambda g, i: (0, SB_W // SB_GW + g))
    return q_spec, k_spec, v_spec


def _head(ref, h, rows=None):
    cols = slice(h * HD, (h + 1) * HD)
    return ref[:, cols] if rows is None else ref[rows, cols]


SB_KEYS = 512


def _sub(v, b):
    return v[:, b * SB_BLOCK:(b + 1) * SB_BLOCK]


def _sb_keep(kc, limit):
    row = lax.broadcasted_iota(jnp.int32, (SB_BLOCK, kc), 0)
    col = lax.broadcasted_iota(jnp.int32, (SB_BLOCK, kc), 1)
    return col < row + limit


def _sb_chunk(q, keys, run, later, limit):
    kc = keys.shape[0]
    z = _dot(q, keys, NT)
    lsz = jax.nn.log_sigmoid(z)
    ls = lsz - z
    if limit is not None:
        keep = _sb_keep(kc, limit)
        ls = jnp.where(keep, ls, 0.0)
    parts = [None] * (kc // SB_BLOCK)
    for b in reversed(range(kc // SB_BLOCK)):
        parts[b] = _hdot(_sub(ls, b), later) + run
        run = run + jnp.sum(_sub(ls, b), axis=1, keepdims=True)
    a = jnp.exp(lsz + jnp.concatenate(parts, axis=1))
    if limit is not None:
        a = jnp.where(keep, a, 0.0)
    return z, a, run


def _sb_fwd(proj, kv):
    t = proj.shape[0]
    kc = min(SB_KEYS, t)
    scale = HD ** -0.5

    def body(q_ref, k_ref, v_ref, o_ref):
        i = pl.program_id(1)
        top = (i * SB_BLOCK) // kc
        qs = [(_head(q_ref, h) * scale).astype(BF16) for h in range(SB_GROUP)]
        row = lax.broadcasted_iota(jnp.int32, (SB_BLOCK, SB_BLOCK), 0)
        col = lax.broadcasted_iota(jnp.int32, (SB_BLOCK, SB_BLOCK), 1)
        later = (row > col).astype(F32)

        def chunk(jc, carry, masked):
            rows = pl.ds(pl.multiple_of(jc * kc, kc), kc)
            limit = i * SB_BLOCK - jc * kc if masked else None
            out = []
            for h in range(SB_GROUP):
                acc, run = carry[h]
                _, a, run = _sb_chunk(qs[h], _head(k_ref, h, rows), run, later, limit)
                out.append((acc + _dot(a.astype(BF16), _head(v_ref, h, rows)), run))
            return tuple(out)

        zero = tuple((jnp.zeros((SB_BLOCK, HD), F32), jnp.zeros((SB_BLOCK, 1), F32)) for _ in range(SB_GROUP))
        carry = chunk(top, zero, True)
        carry = lax.fori_loop(0, top, lambda jj, c: chunk(top - 1 - jj, c, False), carry)
        for h in range(SB_GROUP):
            o_ref[:, h * HD:(h + 1) * HD] = carry[h][0].astype(o_ref.dtype)

    return pl.pallas_call(
        body, name="sb_fwd", grid=(HEADS // SB_GROUP, t // SB_BLOCK), in_specs=list(_sb_specs(t)),
        out_specs=pl.BlockSpec((SB_BLOCK, SB_GW), lambda g, i: (i, g)), out_shape=_sds((t, SB_W), BF16),
        compiler_params=_cp("parallel", "arbitrary"),
    )(proj, kv, kv)


def _sb_bwd(proj, kv, dy):
    t = proj.shape[0]
    nq = t // SB_BLOCK
    kc = min(SB_KEYS, t)
    scale = HD ** -0.5

    def body(q_ref, k_ref, v_ref, do_ref, dq_ref, dk_ref, dv_ref, z_scr, e_scr):
        i = pl.program_id(1)
        top = (i * SB_BLOCK) // kc

        @pl.when(i == 0)
        def _():
            dk_ref[...] = jnp.zeros_like(dk_ref)
            dv_ref[...] = jnp.zeros_like(dv_ref)

        qs = [(_head(q_ref, h) * scale).astype(BF16) for h in range(SB_GROUP)]
        dos = [_head(do_ref, h).astype(BF16) for h in range(SB_GROUP)]
        row = lax.broadcasted_iota(jnp.int32, (SB_BLOCK, SB_BLOCK), 0)
        col = lax.broadcasted_iota(jnp.int32, (SB_BLOCK, SB_BLOCK), 1)
        later = (row > col).astype(F32)
        earlier = (row < col).astype(F32)

        def down(jc, runs, masked):
            rows = pl.ds(pl.multiple_of(jc * kc, kc), kc)
            limit = i * SB_BLOCK - jc * kc if masked else None
            out = []
            for h in range(SB_GROUP):
                z, a, run = _sb_chunk(qs[h], _head(k_ref, h, rows), runs[h], later, limit)
                z_scr[h, jc] = z
                e_scr[h, jc] = a * _dot(dos[h], _head(v_ref, h, rows), NT)
                dv_ref[rows, h * HD:(h + 1) * HD] += _dot(a.astype(BF16), dos[h], TN)
                out.append(run)
            return tuple(out)

        zero = tuple(jnp.zeros((SB_BLOCK, 1), F32) for _ in range(SB_GROUP))
        runs = down(top, zero, True)
        lax.fori_loop(0, top, lambda jj, r: down(top - 1 - jj, r, False), runs)

        def up(jc, carry, masked):
            rows = pl.ds(pl.multiple_of(jc * kc, kc), kc)
            out = []
            for h in range(SB_GROUP):
                dq, run = carry[h]
                z, e = z_scr[h, jc], e_scr[h, jc]
                parts = []
                for b in range(kc // SB_BLOCK):
                    parts.append(_hdot(_sub(e, b), earlier) + run)
                    run = run + jnp.sum(_sub(e, b), axis=1, keepdims=True)
                sz = jax.nn.sigmoid(z)
                dz = e * (1.0 - sz) - jnp.concatenate(parts, axis=1) * sz
                if masked:
                    dz = jnp.where(_sb_keep(kc, i * SB_BLOCK - jc * kc), dz, 0.0)
                dz = dz.astype(BF16)
                dk_ref[rows, h * HD:(h + 1) * HD] += _dot(dz, qs[h], TN)
                out.append((dq + _dot(dz, _head(k_ref, h, rows)), run))
            return tuple(out)

        zero = tuple((jnp.zeros((SB_BLOCK, HD), F32), jnp.zeros((SB_BLOCK, 1), F32)) for _ in range(SB_GROUP))
        carry = lax.fori_loop(0, top, lambda jc, c: up(jc, c, False), zero)
        carry = up(top, carry, True)
        for h in range(SB_GROUP):
            dq_ref[:, h * HD:(h + 1) * HD] = (carry[h][0] * scale).astype(dq_ref.dtype)

    blk = pl.BlockSpec((SB_BLOCK, SB_GW), lambda g, i: (i, g))
    seq = pl.BlockSpec((t, SB_GW), lambda g, i: (0, g))
    scratch = pltpu.VMEM((SB_GROUP, t // kc, SB_BLOCK, kc), F32)
    return pl.pallas_call(
        body, name="sb_bwd", grid=(HEADS // SB_GROUP, nq), in_specs=[*_sb_specs(t), blk], out_specs=[blk, seq, seq],
        out_shape=[_sds((t, SB_W), BF16), _sds((t, SB_W), F32), _sds((t, SB_W), F32)],
        scratch_shapes=[scratch, scratch], compiler_params=_cp("parallel", "arbitrary"),
    )(proj, kv, kv, dy)


CONV_TILE = 256
GDN_CONV = 4


def _conv_pre(x, w_ref, t_idx):
    pre = w_ref[GDN_CONV - 1:GDN_CONV, :] * x
    for s in range(1, GDN_CONV):
        pre = pre + w_ref[GDN_CONV - 1 - s:GDN_CONV - s, :] * _shift_down(x, s, t_idx)
    return pre


def _conv_fwd(proj, conv_w):
    t = proj.shape[0]
    width = conv_w.shape[1]

    def body(x_ref, w_ref, y_ref):
        t_idx = lax.broadcasted_iota(jnp.int32, (t, CONV_TILE), 0)
        pre = _conv_pre(x_ref[...], w_ref, t_idx)
        y_ref[...] = pre * jax.nn.sigmoid(pre)

    return pl.pallas_call(
        body, name="conv_fwd", grid=(width // CONV_TILE,),
        in_specs=[pl.BlockSpec((t, CONV_TILE), lambda c: (0, OFF_GQKV // CONV_TILE + c)),
                  pl.BlockSpec((GDN_CONV, CONV_TILE), lambda c: (0, c))],
        out_specs=pl.BlockSpec((t, CONV_TILE), lambda c: (0, c)), out_shape=_sds((t, width), F32),
        compiler_params=_cp("parallel"),
    )(proj, conv_w)


def _conv_bwd(proj, conv_w, dc, dproj):
    t = proj.shape[0]
    width = dc.shape[1]
    per = width // CONV_TILE
    first = OFF_GQKV // CONV_TILE

    def body(x_ref, w_ref, dc_ref, _, dx_ref, dw_ref):
        t_idx = lax.broadcasted_iota(jnp.int32, (t, CONV_TILE), 0)
        x = x_ref[...]
        pre = _conv_pre(x, w_ref, t_idx)
        sg = jax.nn.sigmoid(pre)
        dpre = dc_ref[...] * (sg * (1.0 + pre * (1.0 - sg)))
        dx = w_ref[GDN_CONV - 1:GDN_CONV, :] * dpre
        dw_ref[GDN_CONV - 1:GDN_CONV, :] = jnp.sum(dpre * x, axis=0, keepdims=True)
        for s in range(1, GDN_CONV):
            dx = dx + w_ref[GDN_CONV - 1 - s:GDN_CONV - s, :] * _shift_up(dpre, s, t_idx, t)
            dw_ref[GDN_CONV - 1 - s:GDN_CONV - s, :] = jnp.sum(dpre * _shift_down(x, s, t_idx), axis=0, keepdims=True)
        dx_ref[...] = dx.astype(dx_ref.dtype)

    return pl.pallas_call(
        body, name="conv_bwd", grid=(per,),
        in_specs=[pl.BlockSpec((t, CONV_TILE), lambda c: (0, first + c)),
                  pl.BlockSpec((GDN_CONV, CONV_TILE), lambda c: (0, c)),
                  pl.BlockSpec((t, CONV_TILE), lambda c: (0, c)), ANY],
        out_specs=[pl.BlockSpec((t, CONV_TILE), lambda c: (0, first + c)), pl.BlockSpec((GDN_CONV, CONV_TILE), lambda c: (0, c))],
        out_shape=[_sds(dproj.shape, dproj.dtype), _sds((GDN_CONV, width), F32)],
        input_output_aliases={3: 0}, compiler_params=_cp("parallel"),
    )(proj, conv_w, dc, dproj)


def _heads(x):
    return jnp.concatenate([x[:, h * HD:(h + 1) * HD][None] for h in range(HEADS)], axis=0)


def _hb(a, b, ca=2, cb=1):
    return lax.dot_general(a, b, (((ca,), (cb,)), ((0,), (0,))), precision=HIGH, preferred_element_type=F32)


@jax.custom_vjp
def _unit_lower_inverse(lower):
    c = lower.shape[-1]
    eye = lax.broadcasted_iota(jnp.int32, (c, c), 0) == lax.broadcasted_iota(jnp.int32, (c, c), 1)
    inv = jnp.where(eye, 1.0, 0.0) - lower
    pw = _hb(lower, lower)
    for step in range(5):
        inv = inv + _hb(inv, pw)
        if step < 4:
            pw = _hb(pw, pw)
    return inv


def _unit_lower_inverse_fwd(lower):
    inv = _unit_lower_inverse(lower)
    return inv, inv


def _unit_lower_inverse_bwd(inv, d_inv):
    return (-_hb(_hb(inv, d_inv, 1, 1), inv, 2, 2),)


_unit_lower_inverse.defvjp(_unit_lower_inverse_fwd, _unit_lower_inverse_bwd)


@jax.custom_vjp
def _known_inverse(lower, inv):
    return inv


_known_inverse.defvjp(lambda lower, inv: (inv, inv),
                      lambda inv, d_inv: (*_unit_lower_inverse_bwd(inv, d_inv), jnp.zeros_like(inv)))


def _gdn_prep(cq, ck, cv, ab, alog_row, dtb_row, inv=None, keep_inverse=False):
    c = GDN_CHUNK
    row = lax.broadcasted_iota(jnp.int32, (c, c), 0)
    col = lax.broadcasted_iota(jnp.int32, (c, c), 1)
    incl, strict, eye = row >= col, row > col, row == col
    def lanes(v, first):
        return jnp.concatenate([v[:, first + h:first + h + 1][None] for h in range(HEADS)], axis=0)

    a_col, b_col = lanes(ab, 0), lanes(ab, HEADS)
    a_log, dt_bias = lanes(alog_row, 0), lanes(dtb_row, 0)
    qn = cq * lax.rsqrt(jnp.sum(cq * cq, axis=-1, keepdims=True) + EPS) * (HD ** -0.5)
    kn = ck * lax.rsqrt(jnp.sum(ck * ck, axis=-1, keepdims=True) + EPS)
    la_col = -jnp.exp(a_log) * jax.nn.softplus(a_col + dt_bias)
    beta = jax.nn.sigmoid(b_col)
    la_row = jnp.sum(jnp.where(eye, la_col, 0.0), axis=1, keepdims=True)
    g_col = jnp.sum(jnp.where(incl, la_row, 0.0), axis=2, keepdims=True)
    g_row = jnp.sum(jnp.where(row <= col, la_col, 0.0), axis=1, keepdims=True)
    g_last = jnp.sum(la_col, axis=1, keepdims=True)
    gamma = jnp.where(incl, jnp.exp(jnp.where(incl, g_col - g_row, 0.0)), 0.0)
    lower = jnp.where(strict, beta * _hb(kn, kn, 2, 2) * gamma, 0.0)
    inv = _unit_lower_inverse(lower) if inv is None else _known_inverse(lower, inv)
    u = _hb(inv, cv * beta)
    w = _hb(inv, kn * (beta * jnp.exp(g_col)))
    qk = _hb(qn, kn, 2, 2) * gamma
    out = (u, w, qk, qn * jnp.exp(g_col), kn * jnp.exp(g_last - g_col), jnp.exp(g_last))
    return (*out, inv) if keep_inverse else out


def _gdn_post(o, z, gain):
    y = o * lax.rsqrt(jnp.mean(o * o, axis=-1, keepdims=True) + EPS) * gain
    return y * (z * jax.nn.sigmoid(z))


def _gdn_specs(nc, reverse):
    c = GDN_CHUNK

    def ch(n):
        return nc - 1 - n if reverse else n

    def wide(array_off):
        return pl.BlockSpec((c, GDN_W), lambda n: (ch(n), array_off // GDN_W))

    zab = pl.BlockSpec((c, ZAB_W), lambda n: (ch(n), OFF_Z // ZAB_W))
    row = pl.BlockSpec((1, HD), lambda n: (0, 0))
    state = pl.BlockSpec((None, HEADS, HD, HD), lambda n: (ch(n), 0, 0, 0))
    inverse = pl.BlockSpec((None, HEADS, c, c), lambda n: (ch(n), 0, 0, 0))
    return wide, zab, row, state, inverse


def _gdn_fwd(cqkv, proj, a_log, dt_bias, gain):
    t = proj.shape[0]
    nc = t // GDN_CHUNK
    wide, zab, row, state, inverse = _gdn_specs(nc, False)

    def body(cq_ref, ck_ref, cv_ref, zab_ref, al_ref, dt_ref, g_ref, y_ref, sprev_ref, inv_ref, s_scr):
        @pl.when(pl.program_id(0) == 0)
        def _():
            s_scr[...] = jnp.zeros_like(s_scr)

        z_ref, ab_ref = zab_ref.at[:, :GDN_W], zab_ref.at[:, GDN_W:GDN_W + HD]
        u, w, qk, qd, kd, dec, inv = _gdn_prep(_heads(cq_ref[...]), _heads(ck_ref[...]), _heads(cv_ref[...]), ab_ref[...],
                                               al_ref[...], dt_ref[...], keep_inverse=True)
        inv_ref[...] = inv
        s = s_scr[...]
        sprev_ref[...] = s
        v_new = u - _hb(w, s)
        o = _hb(qd, s) + _hb(qk, v_new)
        s_scr[...] = s * dec + _hb(kd, v_new, 1, 1)
        y = _gdn_post(o, _heads(z_ref[...]), g_ref[...])
        for h in range(HEADS):
            y_ref[:, h * HD:(h + 1) * HD] = y[h].astype(y_ref.dtype)

    return pl.pallas_call(
        body, name="gdn_fwd", grid=(nc,),
        in_specs=[wide(0), wide(GDN_W), wide(2 * GDN_W), zab, row, row, row],
        out_specs=[wide(0), state, inverse],
        out_shape=[_sds((t, GDN_W), BF16), _sds((nc, HEADS, HD, HD), F32), _sds((nc, HEADS, GDN_CHUNK, GDN_CHUNK), F32)],
        scratch_shapes=[pltpu.VMEM((HEADS, HD, HD), F32)], compiler_params=_cp("arbitrary"),
    )(cqkv, cqkv, cqkv, proj, a_log, dt_bias, gain)


def _gdn_bwd(cqkv, proj, a_log, dt_bias, gain, sprev, inverses, dy, dproj):
    t = proj.shape[0]
    nc = t // GDN_CHUNK
    wide, zab, row, state, inverse = _gdn_specs(nc, True)

    def body(cq_ref, ck_ref, cv_ref, zab_ref, al_ref, dt_ref, g_ref, sp_ref, inv_ref, dy_ref, _,
             dc_ref, dzab_ref, dal_ref, ddt_ref, dg_ref, ds_scr):
        @pl.when(pl.program_id(0) == 0)
        def _():
            ds_scr[...] = jnp.zeros_like(ds_scr)
            dal_ref[...] = jnp.zeros_like(dal_ref)
            ddt_ref[...] = jnp.zeros_like(ddt_ref)
            dg_ref[...] = jnp.zeros_like(dg_ref)

        z_ref, ab_ref = zab_ref.at[:, :GDN_W], zab_ref.at[:, GDN_W:GDN_W + HD]

        (u, w, qk, qd, kd, dec), prep_vjp = jax.vjp(
            functools.partial(_gdn_prep, inv=inv_ref[...]),
            _heads(cq_ref[...]), _heads(ck_ref[...]), _heads(cv_ref[...]), ab_ref[...], al_ref[...], dt_ref[...])
        s = sp_ref[...]
        v_new = u - _hb(w, s)
        o = _hb(qd, s) + _hb(qk, v_new)
        _, post_vjp = jax.vjp(_gdn_post, o, _heads(z_ref[...]), g_ref[...])
        do, dz, dgain = post_vjp(_heads(dy_ref[...]).astype(F32))
        ds_next = ds_scr[...]
        d_vnew = _hb(qk, do, 1, 1) + _hb(kd, ds_next)
        d_qk = _hb(do, v_new, 2, 2)
        d_qd = _hb(do, s, 2, 2)
        d_kd = _hb(v_new, ds_next, 2, 2)
        d_dec = jnp.sum(jnp.sum(s * ds_next, axis=2, keepdims=True), axis=1, keepdims=True)
        ds_scr[...] = dec * ds_next + _hb(qd, do, 1, 1) - _hb(w, d_vnew, 1, 1)
        d_w = -_hb(d_vnew, s, 2, 2)
        dcq, dck, dcv, dab, dal, ddt = prep_vjp((d_vnew, d_w, d_qk, d_qd, d_kd, d_dec))
        for h in range(HEADS):
            dc_ref[:, h * HD:(h + 1) * HD] = dcq[h]
            dc_ref[:, GDN_W + h * HD:GDN_W + (h + 1) * HD] = dck[h]
            dc_ref[:, 2 * GDN_W + h * HD:2 * GDN_W + (h + 1) * HD] = dcv[h]
            dzab_ref[:, h * HD:(h + 1) * HD] = dz[h].astype(dzab_ref.dtype)
        dzab_ref[:, GDN_W:GDN_W + HD] = dab.astype(dzab_ref.dtype)
        dzab_ref[:, GDN_W + HD:] = jnp.zeros((GDN_CHUNK, AB_W - HD), dzab_ref.dtype)
        dal_ref[...] += dal
        ddt_ref[...] += ddt
        dg_ref[...] += dgain

    c = GDN_CHUNK
    return pl.pallas_call(
        body, name="gdn_bwd", grid=(nc,),
        in_specs=[wide(0), wide(GDN_W), wide(2 * GDN_W), zab, row, row, row, state, inverse, wide(0), ANY],
        out_specs=[pl.BlockSpec((c, 3 * GDN_W), lambda n: (nc - 1 - n, 0)), zab, row, row, row],
        out_shape=[_sds((t, 3 * GDN_W), F32), _sds(dproj.shape, dproj.dtype),
                   _sds((1, HD), F32), _sds((1, HD), F32), _sds((1, HD), F32)],
        input_output_aliases={10: 1},
        scratch_shapes=[pltpu.VMEM((HEADS, HD, HD), F32)], compiler_params=_cp("arbitrary"),
    )(cqkv, cqkv, cqkv, proj, a_log, dt_bias, gain, sprev, inverses, dy, dproj)


MERGE_TN = 512


def _merge_specs(t, tm):
    tn = MERGE_TN
    ys = [pl.BlockSpec((tm, wd), lambda i, j: (i, 0)) for wd in (POOL_W, SB_W, GDN_W)]
    ws = [pl.BlockSpec((None, wd, tn), lambda i, j: (j, 0, 0)) for wd in (POOL_W, SB_W, GDN_W)]
    gs = [pl.BlockSpec((tm, tn), functools.partial(lambda i, j, b: (i, OFF_GATE // tn + b * (D // tn) + j), b=b))
          for b in range(3)]
    out = pl.BlockSpec((tm, tn), lambda i, j: (i, j))
    return ys, ws, gs, out


def _merge_fwd(ys, wups, proj):
    t = proj.shape[0]
    tm = min(512, t)
    y_specs, w_specs, g_specs, out = _merge_specs(t, tm)

    def body(y0, y1, y2, w0, w1, w2, g0, g1, g2, o_ref):
        acc = jnp.zeros(o_ref.shape, F32)
        for y, w, g in ((y0, w0, g0), (y1, w1, g1), (y2, w2, g2)):
            acc = acc + jax.nn.sigmoid(g[...]) * _dot(y[...], w[...])
        o_ref[...] = acc.astype(o_ref.dtype)

    return pl.pallas_call(
        body, name="merge_fwd", grid=(t // tm, D // MERGE_TN), in_specs=[*y_specs, *w_specs, *g_specs],
        out_specs=out, out_shape=_sds((t, D), BF16), compiler_params=_cp("parallel", "parallel"),
    )(*ys, *wups, proj, proj, proj)


def _merge_bwd(ys, wups, proj, dmerged):
    t = proj.shape[0]
    tm = min(512, t)
    tn = MERGE_TN
    per = D // tn
    ys_specs = [pl.BlockSpec((tm, wd), lambda i, b, j: (i, 0)) for wd in (POOL_W, SB_W, GDN_W)]

    def w_spec(k, wd):
        return pl.BlockSpec((None, wd, tn), lambda i, b, j: (jnp.where(b == k, j, jnp.where(b < k, 0, per - 1)), 0, 0))

    w_specs = [w_spec(k, wd) for k, wd in enumerate((POOL_W, SB_W, GDN_W))]
    gate = pl.BlockSpec((tm, tn), lambda i, b, j: (i, OFF_GATE // tn + b * per + j))
    branch = pl.BlockSpec((tm, tn), lambda i, b, j: (i, b * per + j))
    merged = pl.BlockSpec((tm, tn), lambda i, b, j: (i, j))

    def body(y0, y1, y2, w0, w1, w2, g_ref, dm_ref, dg_ref, dmb_ref):
        b = pl.program_id(1)
        dm = dm_ref[...].astype(F32)
        sg = jax.nn.sigmoid(g_ref[...])
        dmb_ref[...] = (dm * sg).astype(dmb_ref.dtype)
        for k, (y, w) in enumerate(((y0, w0), (y1, w1), (y2, w2))):
            @pl.when(b == k)
            def _():
                dg_ref[...] = (dm * _dot(y[...], w[...]) * sg * (1.0 - sg)).astype(dg_ref.dtype)

    return pl.pallas_call(
        body, name="merge_bwd", grid=(t // tm, 3, per), in_specs=[*ys_specs, *w_specs, gate, merged],
        out_specs=[gate, branch], out_shape=[_sds((t, N_AL), BF16), _sds((t, 3 * D), BF16)],
        compiler_params=_cp("parallel", "arbitrary", "arbitrary"),
    )(*ys, *wups, proj, dmerged)


def _place(name, dproj, src, col):
    t, w = src.shape
    tt = min(512, t)

    def body(s_ref, _, o_ref):
        o_ref[...] = s_ref[...].astype(o_ref.dtype)

    return pl.pallas_call(
        body, name=name, grid=(t // tt,), in_specs=[pl.BlockSpec((tt, w), lambda i: (i, 0)), ANY],
        out_specs=pl.BlockSpec((tt, w), lambda i: (i, col // w)), out_shape=_sds(dproj.shape, dproj.dtype),
        input_output_aliases={1: 0}, compiler_params=_cp("parallel"),
    )(src, dproj)


def _tile(t, want):
    return min(t, want)


def _layer_fwd(x, l, gw, w_al, sp, reached=None):
    t = x.shape[0]
    tm = _tile(t, 1024)
    u = _rms_fwd("rms_attn", x, sp["attn_norm"][l])
    proj = _mm("proj", u, w_al, m=t, n=N_AL, k=D, tm=tm, tn=1024, tk=D, a_spec=_a_plain(tm, D),
               b_spec=_b_plain(D, 1024), dims=None, out_shapes=[_sds((t, N_AL), F32)], out_specs=[_o_plain(tm, 1024)])[0]
    if reached is not None:
        reached("proj", proj)
    y_pool = _pool_fwd(proj, sp["pool_w"][l], sp["pool_scale"][l])
    kv = _sb_cast_kv(proj)
    y_sb = _sb_fwd(proj, kv)
    cqkv = _conv_fwd(proj, sp["conv"][l])
    y_gdn, sprev, inverses = _gdn_fwd(cqkv, proj, sp["a_log"][l], sp["dt_bias"][l], sp["gdn_norm"][l])
    ys = (y_pool, y_sb, y_gdn)
    wups = (gw["w_pool_up"], gw["w_sb_up"], gw["w_gdn_up"])
    merged = _merge_fwd(ys, wups, proj)
    if reached is not None:
        reached("merged", merged)
    x1 = _mm("out_proj", merged, gw["w_out"], m=t, n=D, k=D, tm=tm, tn=1024, tk=512, a_spec=_a_plain(tm, 512),
             b_spec=_w_rows(512, 1024, 512), dims=None, out_shapes=[_sds((t, D), F32)], out_specs=[_o_plain(tm, 1024)],
             extras=[x], extra_specs=[_o_plain(tm, 1024)], epilogue=lambda r, xr: (r + xr,))[0]
    u2 = _rms_fwd("rms_mlp", x1, sp["mlp_norm"][l])

    def relu2(r):
        hv = jnp.maximum(r, 0.0)
        return hv, hv * hv

    hid, hid2 = _mm("ff1", u2, gw["w_ff1"], m=t, n=D_FF, k=D, tm=tm, tn=1024, tk=D, a_spec=_a_plain(tm, D),
                    b_spec=_w_cols(D, 1024, 2048), dims=None, out_shapes=[_sds((t, D_FF), BF16)] * 2,
                    out_specs=[_o_plain(tm, 1024)] * 2, epilogue=relu2)
    x2 = _mm("ff2", hid2, gw["w_ff2"], m=t, n=D, k=D_FF, tm=tm, tn=1024, tk=2048, a_spec=_a_plain(tm, 2048),
             b_spec=_w_rows(2048, 1024, 2048), dims=None, out_shapes=[_sds((t, D), F32)], out_specs=[_o_plain(tm, 1024)],
             extras=[x1], extra_specs=[_o_plain(tm, 1024)], epilogue=lambda r, xr: (r + xr,))[0]
    saved = dict(x=x, u=u, proj=proj, kv=kv, cqkv=cqkv, sprev=sprev, inverses=inverses, ys=ys, merged=merged, x1=x1, u2=u2, hid=hid, hid2=hid2)
    return x2, saved


def _layer_bwd(dx2, l, gw, w_al, sp, sv, emit=None):
    t = dx2.shape[0]
    tm = _tile(t, 1024)
    tk = t
    g = {}
    if emit is None:
        emit = lambda names, grads, v: v
    dpre = _mm("ff2_dx", dx2, gw["w_ff2"], m=t, n=D_FF, k=D, tm=tm, tn=1024, tk=D, a_spec=_a_plain(tm, D),
               b_spec=_w_rows_t(D, 1024, 2048), dims=NT, out_shapes=[_sds((t, D_FF), BF16)],
               out_specs=[_o_plain(tm, 1024)], extras=[sv["hid"]], extra_specs=[_o_plain(tm, 1024)],
               epilogue=lambda r, hv: (r * (2.0 * hv.astype(F32)),))[0]
    g["w_ff2"] = _mm("ff2_dw", sv["hid2"], dx2, m=D_FF, n=D, k=t, tm=1024, tn=1024, tk=tk, a_spec=_a_trans(1024, tk),
                     b_spec=_b_plain(tk, 1024), dims=TN, out_shapes=[_sds((D_FF, D), BF16)],
                     out_specs=[_o_plain(1024, 1024)])[0].reshape(N_CHIPS, D_FF // N_CHIPS, D)
    du2 = _mm("ff1_dx", dpre, gw["w_ff1"], m=t, n=D, k=D_FF, tm=tm, tn=1024, tk=2048, a_spec=_a_plain(tm, 2048),
              b_spec=_w_cols_t(2048, 1024, 2048), dims=NT, out_shapes=[_sds((t, D), F32)], out_specs=[_o_plain(tm, 1024)])[0]
    g["w_ff1"] = _mm("ff1_dw", sv["u2"], dpre, m=D, n=D_FF, k=t, tm=1024, tn=1024, tk=tk, a_spec=_a_trans(1024, tk),
                     b_spec=_b_plain(tk, 1024), dims=TN, out_shapes=[_sds((N_CHIPS, D, D_FF // N_CHIPS), BF16)],
                     out_specs=[_o_colshard(1024, 1024, D_FF // N_CHIPS)])[0]
    dx1, g["mlp_norm"] = _rms_bwd("rms_mlp_bwd", du2, sv["x1"], sp["mlp_norm"][l], dx2)
    dx1 = emit(("w_ff1", "w_ff2"), g, dx1)
    dmerged = _mm("out_dx", dx1, gw["w_out"], m=t, n=D, k=D, tm=tm, tn=512, tk=D, a_spec=_a_plain(tm, D),
                  b_spec=_w_rows_t(D, 512, 512), dims=NT, out_shapes=[_sds((t, D), BF16)], out_specs=[_o_plain(tm, 512)])[0]
    g["w_out"] = _mm("out_dw", sv["merged"], dx1, m=D, n=D, k=t, tm=1024, tn=1024, tk=tk, a_spec=_a_trans(1024, tk),
                     b_spec=_b_plain(tk, 1024), dims=TN, out_shapes=[_sds((D, D), BF16)],
                     out_specs=[_o_plain(1024, 1024)])[0].reshape(N_CHIPS, D // N_CHIPS, D)
    wups = (gw["w_pool_up"], gw["w_sb_up"], gw["w_gdn_up"])
    dproj, dm_all = _merge_bwd(sv["ys"], wups, sv["proj"], dmerged)
    per = D // MERGE_TN
    dys = []
    for b, (nm, yb, wd) in enumerate(zip(("w_pool_up", "w_sb_up", "w_gdn_up"), sv["ys"], (POOL_W, SB_W, GDN_W))):
        dm_rows = pl.BlockSpec((tm, MERGE_TN), functools.partial(lambda i, j, kk, b: (i, b * per + kk), b=b))
        dm_cols = pl.BlockSpec((tk, MERGE_TN), functools.partial(lambda i, j, kk, b: (kk, b * per + j), b=b))
        dys.append(_mm(nm + "_dx", dm_all, gw[nm], m=t, n=wd, k=D, tm=tm, tn=256, tk=MERGE_TN, a_spec=dm_rows,
                       b_spec=_w_cols_t(MERGE_TN, 256, 512), dims=NT, out_shapes=[_sds((t, wd), F32)],
                       out_specs=[_o_plain(tm, 256)])[0])
        g[nm] = _mm(nm + "_dw", yb, dm_all, m=wd, n=D, k=t, tm=256, tn=MERGE_TN, tk=tk, a_spec=_a_trans(256, tk),
                    b_spec=dm_cols, dims=TN, out_shapes=[_sds((N_CHIPS, wd, D // N_CHIPS), BF16)],
                    out_specs=[_o_colshard(256, MERGE_TN, D // N_CHIPS)])[0]
    dys[2] = emit(("w_pool_up", "w_sb_up", "w_gdn_up", "w_out"), g, dys[2])
    proj = sv["proj"]
    dproj, g["pool_w"], g["pool_scale"] = _pool_bwd(proj, sp["pool_w"][l], sp["pool_scale"][l], dys[0], dproj)
    for k, piece in enumerate(_sb_bwd(proj, sv["kv"], dys[1])):
        dproj = _place(f"place_sb_{k}", dproj, piece, OFF_SB + k * SB_W)
    dc, dproj, g["a_log"], g["dt_bias"], g["gdn_norm"] = _gdn_bwd(
        sv["cqkv"], proj, sp["a_log"][l], sp["dt_bias"][l], sp["gdn_norm"][l], sv["sprev"], sv["inverses"], dys[2], dproj)
    dproj, g["conv"] = _conv_bwd(proj, sp["conv"][l], dc, dproj)
    du = _mm("proj_dx", dproj, w_al, m=t, n=D, k=N_AL, tm=tm, tn=1024, tk=2048, a_spec=_a_plain(tm, 2048),
             b_spec=_b_trans(2048, 1024), dims=NT, out_shapes=[_sds((t, D), F32)], out_specs=[_o_plain(tm, 1024)])[0]
    g["w_al"] = _mm("proj_dw", sv["u"], dproj, m=D, n=N_AL, k=t, tm=1024, tn=1024, tk=tk, a_spec=_a_trans(1024, tk),
                    b_spec=_b_plain(tk, 1024), dims=TN, out_shapes=[_sds((D, N_AL), BF16)], out_specs=[_o_plain(1024, 1024)])[0]
    dx, g["attn_norm"] = _rms_bwd("rms_attn_bwd", du, sv["x"], sp["attn_norm"][l], dx1)
    g["w_in"] = _w_in_to_shards(g["w_al"])
    dx = emit(("w_in",), g, dx)
    return dx, g


W_IN_RUNS = ((0, ORIG_SB, OFF_P), (ORIG_SB, ORIG_Z, OFF_SB), (ORIG_Z, ORIG_GATE, OFF_Z), (ORIG_GATE, N_IN, OFF_GATE))
W_IN_SHARD = N_IN // N_CHIPS


def _w_in_from_shards(gathered):
    parts = []
    for lo, hi, al in sorted(W_IN_RUNS, key=lambda r: r[2]):
        if al == OFF_GATE:
            parts.append(jnp.zeros((D, OFF_GATE - (OFF_AB + ORIG_GATE - ORIG_AB)), gathered.dtype))
        while lo < hi:
            chip = lo // W_IN_SHARD
            end = min(hi, (chip + 1) * W_IN_SHARD)
            parts.append(gathered[chip, :, lo - chip * W_IN_SHARD:end - chip * W_IN_SHARD])
            lo = end
    return jnp.concatenate(parts, axis=1)


def _w_in_to_shards(g_al):
    shards = []
    for chip in range(N_CHIPS):
        a, b = chip * W_IN_SHARD, (chip + 1) * W_IN_SHARD
        parts = [g_al[:, al + max(a, lo) - lo:al + min(b, hi) - lo] for lo, hi, al in W_IN_RUNS if max(a, lo) < min(b, hi)]
        shards.append(jnp.concatenate(parts, axis=1))
    return jnp.stack(shards)


def _row128(v):
    return jnp.pad(v.reshape(1, -1), ((0, 0), (0, HD - v.shape[-1])))


def _local_step(x, target, weights_of, sp, emit=None, reached=None):
    saved, gw, w_in_al = [], [], []
    h = x
    for l in range(2):
        gw_l, w_al_l = weights_of(l)
        gw.append(gw_l)
        w_in_al.append(w_al_l)
        h, sv = _layer_fwd(h, l, gw_l, w_al_l, sp, None if reached is None else functools.partial(reached, l))
        saved.append(sv)
    loss, dh, g_final = _loss_head(h, sp["final_norm"], target)
    grads = [None, None]
    for l in (1, 0):
        dh, grads[l] = _layer_bwd(dh, l, gw[l], w_in_al[l], sp, saved[l],
                                  None if emit is None else functools.partial(emit, l))
    return loss, dh, grads, g_final


ANY = pl.BlockSpec(memory_space=pl.ANY)


def _me():
    return lax.axis_index("x"), lax.axis_index("y"), lax.axis_index("c")


def _other_chips(x, y):
    return [(1 - x, y), (x, 1 - y), (1 - x, 1 - y)]


def _half(ref, axis, c, rows):
    half = rows // 2
    idx = [slice(None)] * axis + [pl.ds(pl.multiple_of(c * half, 16), half)]
    return ref.at[tuple(idx)]


def _gather_steps(out, send, recv):
    n = len(out)
    x, y, c = _me()
    mine = 2 * x + y
    sibling = (x, y, 1 - c)
    chips = _other_chips(x, y)
    sends = []
    for t in range(n):
        rows = out[t].shape[1]
        for k, (px, py) in enumerate(chips):
            own_half = _half(out[t].at[mine], 0, c, rows)
            cp = pltpu.make_async_remote_copy(
                src_ref=own_half, dst_ref=own_half,
                send_sem=send.at[6 * t + k], recv_sem=recv.at[6 * t + k], device_id=(px, py, c), device_id_type=MESH)
            cp.start()
            sends.append(cp)
    for t in range(n):
        rows = out[t].shape[1]
        for k, (px, py) in enumerate(chips):
            landed = _half(out[t].at[2 * px + py], 0, c, rows)
            pltpu.make_async_remote_copy(
                src_ref=landed, dst_ref=landed, send_sem=send.at[6 * t + k], recv_sem=recv.at[6 * t + k],
                device_id=(px, py, c), device_id_type=MESH).wait_recv()
            cp = pltpu.make_async_remote_copy(
                src_ref=landed, dst_ref=landed, send_sem=send.at[6 * t + 3 + k], recv_sem=recv.at[6 * t + 3 + k],
                device_id=sibling, device_id_type=MESH)
            cp.start()
            sends.append(cp)
    for t in range(n):
        rows = out[t].shape[1]
        for k, (px, py) in enumerate(chips):
            other = _half(out[t].at[2 * px + py], 0, 1 - c, rows)
            pltpu.make_async_remote_copy(
                src_ref=other, dst_ref=other, send_sem=send.at[6 * t + 3 + k], recv_sem=recv.at[6 * t + 3 + k],
                device_id=sibling, device_id_type=MESH).wait_recv()
    for cp in sends:
        cp.wait_send()


def _gather_weights_async(bufs, tag, collective_id):
    n = len(bufs)
    refs = [jax.new_ref(b, memory_space=pltpu.MemorySpace.HBM) for b in bufs]

    @pl.kernel(mesh=plsc.ScalarSubcoreMesh(axis_name="sequencer", num_cores=1), name=f"gather_async_{tag}",
               scratch_types=(pltpu.SemaphoreType.DMA((6 * n,)), pltpu.SemaphoreType.DMA((6 * n,))),
               compiler_params=pltpu.CompilerParams(collective_id=collective_id))
    def launch(send, recv):
        x, y, c = _me()
        barrier = pltpu.get_barrier_semaphore()
        peers = [(x, y, 1 - c)] + [(px, py, c) for px, py in _other_chips(x, y)]
        for peer in peers:
            pl.semaphore_signal(barrier, inc=1, device_id=peer, device_id_type=MESH)
        pl.semaphore_wait(barrier, len(peers))
        _gather_steps(refs, send, recv)

    launch()
    return [r[...] for r in refs]


def _rs_pair(grads):
    n = len(grads)

    def body(*refs):
        g, out = refs[:n], refs[n:2 * n]
        send, recv = refs[2 * n:]
        x, y, c = _me()
        copies = []
        for t in range(n):
            cp = pltpu.make_async_remote_copy(
                src_ref=_half(g[t], 1, 1 - c, g[t].shape[1]), dst_ref=out[t], send_sem=send.at[t], recv_sem=recv.at[t],
                device_id=(x, y, 1 - c), device_id_type=MESH)
            cp.start()
            copies.append(cp)
        for cp in copies:
            cp.wait()

    return pl.pallas_call(
        body, name="rs_pair", in_specs=[ANY] * n, out_specs=[ANY] * n,
        out_shape=[_sds((N_CHIPS, s.shape[1] // 2, s.shape[2]), s.dtype) for s in grads],
        scratch_shapes=[pltpu.SemaphoreType.DMA((n,)), pltpu.SemaphoreType.DMA((n,))],
    )(*grads)


def _rs_chips_steps(p, out, send, recv):
    x, y, c = _me()
    copies = []
    for t in range(len(p)):
        for k, (px, py) in enumerate(_other_chips(x, y)):
            cp = pltpu.make_async_remote_copy(
                src_ref=p[t].at[2 * px + py], dst_ref=out[t].at[k], send_sem=send.at[3 * t + k],
                recv_sem=recv.at[3 * t + k], device_id=(px, py, c), device_id_type=MESH)
            cp.start()
            copies.append(cp)
    for cp in copies:
        cp.wait()


def _rs_chips_async(parts, tag, collective_id):
    n = len(parts)
    src = [jax.new_ref(p, memory_space=pltpu.MemorySpace.HBM) for p in parts]
    got = [jax.empty_ref(_sds((3, *p.shape[1:]), p.dtype), memory_space=pltpu.MemorySpace.HBM) for p in parts]

    @pl.kernel(mesh=plsc.ScalarSubcoreMesh(axis_name="sequencer", num_cores=1), name=f"rs_chips_async_{tag}",
               scratch_types=(pltpu.SemaphoreType.DMA((3 * n,)), pltpu.SemaphoreType.DMA((3 * n,))),
               compiler_params=pltpu.CompilerParams(collective_id=collective_id))
    def launch(send, recv):
        x, y, c = _me()
        barrier = pltpu.get_barrier_semaphore()
        peers = [(px, py, c) for px, py in _other_chips(x, y)]
        for peer in peers:
            pl.semaphore_signal(barrier, inc=1, device_id=peer, device_id_type=MESH)
        pl.semaphore_wait(barrier, len(peers))
        _rs_chips_steps(src, got, send, recv)

    launch()
    return [g[...] for g in got]


def _pair_exchange(bufs):
    n = len(bufs)

    def body(*refs):
        out = refs[n:2 * n]
        send, recv = refs[2 * n:]
        x, y, c = _me()
        copies = []
        for t in range(n):
            cp = pltpu.make_async_remote_copy(
                src_ref=out[t].at[c], dst_ref=out[t].at[c], send_sem=send.at[t], recv_sem=recv.at[t],
                device_id=(x, y, 1 - c), device_id_type=MESH)
            cp.start()
            copies.append(cp)
        for t, cp in enumerate(copies):
            cp.wait_send()
            pltpu.make_async_remote_copy(
                src_ref=out[t].at[1 - c], dst_ref=out[t].at[1 - c], send_sem=send.at[t], recv_sem=recv.at[t],
                device_id=(x, y, 1 - c), device_id_type=MESH).wait_recv()

    return pl.pallas_call(
        body, name="pair_exchange", in_specs=[ANY] * n, out_specs=[ANY] * n,
        out_shape=[_sds(s.shape, s.dtype) for s in bufs], input_output_aliases={t: t for t in range(n)},
        scratch_shapes=[pltpu.SemaphoreType.DMA((n,)), pltpu.SemaphoreType.DMA((n,))],
    )(*bufs)


def _row_tile(rows, cols, itemsize, budget=2 * 1024 * 1024):
    tr = rows
    while tr * cols * itemsize > budget and tr % 32 == 0:
        tr //= 2
    return tr


def _sum_pair(name, g, got, where):
    nchip, rows, cols = g.shape
    half = rows // 2
    tr = _row_tile(half, cols, 4)
    per = half // tr

    def body(w_ref, g_ref, r_ref, o_ref):
        o_ref[...] = (g_ref[...].astype(F32) + r_ref[...].astype(F32)).astype(o_ref.dtype)

    blk = pl.BlockSpec((None, tr, cols), lambda j, i, w_ref: (j, i, 0))
    return pl.pallas_call(
        body, name=name,
        grid_spec=pltpu.PrefetchScalarGridSpec(
            num_scalar_prefetch=1, grid=(nchip, per),
            in_specs=[pl.BlockSpec((None, tr, cols), lambda j, i, w_ref: (j, w_ref[1] * per + i, 0)), blk], out_specs=blk),
        out_shape=_sds((nchip, half, cols), BF16), compiler_params=_cp("parallel", "parallel"),
    )(where, g, got)


def _sum_chips(name, p, got, where):
    _, rows, cols = p.shape
    tr = _row_tile(rows, cols, 4)

    def body(w_ref, p_ref, r0, r1, r2, o_ref):
        o_ref[...] = ((p_ref[...].astype(F32) + r0[...].astype(F32)) + r1[...].astype(F32)) + r2[...].astype(F32)

    def got_k(k):
        return pl.BlockSpec((None, tr, cols), lambda i, w_ref: (k, i, 0))

    return pl.pallas_call(
        body, name=name,
        grid_spec=pltpu.PrefetchScalarGridSpec(
            num_scalar_prefetch=1, grid=(rows // tr,),
            in_specs=[pl.BlockSpec((None, tr, cols), lambda i, w_ref: (w_ref[0], i, 0)), got_k(0), got_k(1), got_k(2)],
            out_specs=pl.BlockSpec((None, tr, cols), lambda i, w_ref: (w_ref[1], i, 0))),
        out_shape=_sds((2, rows, cols), F32), compiler_params=_cp("parallel"),
    )(where, p, got, got, got)


def _rs_begin(grads, where, tag, collective_id):
    got = _rs_pair(grads)
    parts = [_sum_pair(f"sum_pair_{t}", g, r, where) for t, (g, r) in enumerate(zip(grads, got))]
    return parts, _rs_chips_async(parts, tag, collective_id)


def _rs_finish(parts, got, where):
    halves = [_sum_chips(f"sum_chips_{t}", p, r, where) for t, (p, r) in enumerate(zip(parts, got))]
    return _pair_exchange(halves)


def _all_reduce_small(name, v):
    rows = v.shape[0]

    def body(v_ref, o_ref, land, send, recv):
        x, y, c = _me()
        mine = 4 * x + 2 * y + c
        copies = []
        for k in range(1, 8):
            kx, ky, kc = k >> 2, (k >> 1) & 1, k & 1
            peer = (x ^ kx, y ^ ky, c ^ kc)
            cp = pltpu.make_async_remote_copy(
                src_ref=v_ref, dst_ref=land.at[mine], send_sem=send.at[k - 1], recv_sem=recv.at[k - 1],
                device_id=peer, device_id_type=MESH)
            cp.start()
            copies.append(cp)
        land[mine] = v_ref[...]
        for k in range(1, 8):
            kx, ky, kc = k >> 2, (k >> 1) & 1, k & 1
            src = 4 * (x ^ kx) + 2 * (y ^ ky) + (c ^ kc)
            pltpu.make_async_remote_copy(
                src_ref=v_ref, dst_ref=land.at[src], send_sem=send.at[k - 1], recv_sem=recv.at[k - 1],
                device_id=(x ^ kx, y ^ ky, c ^ kc), device_id_type=MESH).wait_recv()
        acc = land[0]
        for d in range(1, 8):
            acc = acc + land[d]
        o_ref[...] = acc
        for cp in copies:
            cp.wait_send()

    vm = pl.BlockSpec(memory_space=pltpu.VMEM)
    return pl.pallas_call(
        body, name=name, in_specs=[vm], out_specs=vm, out_shape=_sds((rows, 128), F32),
        scratch_shapes=[pltpu.VMEM((8, rows, 128), F32), pltpu.SemaphoreType.DMA((7,)), pltpu.SemaphoreType.DMA((7,))],
    )(v)


def _adamw(name, w, g, m, v):
    rows, cols = w.shape
    tr = _row_tile(rows, cols, 4, budget=1024 * 1024)
    c1 = 1.0 / (1.0 - ADAM_B1 ** ADAM_STEP)
    c2 = 1.0 / (1.0 - ADAM_B2 ** ADAM_STEP)

    def body(w_ref, g_ref, m_ref, v_ref, d_ref, nm_ref, nv_ref):
        gv = g_ref[...]
        nm = ADAM_B1 * m_ref[...] + (1.0 - ADAM_B1) * gv
        nv = ADAM_B2 * v_ref[...] + (1.0 - ADAM_B2) * (gv * gv)
        d_ref[...] = -ADAM_LR * ((nm * c1) / (jnp.sqrt(nv * c2) + ADAM_EPS) + ADAM_WD * w_ref[...])
        nm_ref[...] = nm
        nv_ref[...] = nv

    blk = pl.BlockSpec((tr, cols), lambda i: (i, 0))
    return pl.pallas_call(
        body, name=name, grid=(rows // tr,), in_specs=[blk] * 4, out_specs=[blk] * 3,
        out_shape=[_sds((rows, cols), F32)] * 3, compiler_params=_cp("parallel"),
    )(w, g, m, v)


def _adamw_layers(name, w, g0, g1, m, v):
    _, half, cols = g0.shape
    tr = _row_tile(half, cols, 4, budget=1024 * 1024)
    per_half = half // tr
    per = 2 * per_half
    c1 = 1.0 / (1.0 - ADAM_B1 ** ADAM_STEP)
    c2 = 1.0 / (1.0 - ADAM_B2 ** ADAM_STEP)

    def body(w_ref, g0_ref, g1_ref, m_ref, v_ref, g_ref, d_ref, nm_ref, nv_ref):
        gv = jnp.where(pl.program_id(0) == 0, g0_ref[...], g1_ref[...])
        nm = ADAM_B1 * m_ref[...] + (1.0 - ADAM_B1) * gv
        nv = ADAM_B2 * v_ref[...] + (1.0 - ADAM_B2) * (gv * gv)
        g_ref[...] = gv
        d_ref[...] = -ADAM_LR * ((nm * c1) / (jnp.sqrt(nv * c2) + ADAM_EPS) + ADAM_WD * w_ref[...])
        nm_ref[...] = nm
        nv_ref[...] = nv

    both = pl.BlockSpec((None, tr, cols), lambda l, i: (l, i, 0))

    def halves(i):
        return i // per_half, i % per_half, 0

    first = pl.BlockSpec((None, tr, cols), lambda l, i: halves(i * (1 - l) + (per - 1) * l))
    second = pl.BlockSpec((None, tr, cols), lambda l, i: halves(i * l))
    return pl.pallas_call(
        body, name=name, grid=(2, per), in_specs=[both, first, second, both, both], out_specs=[both] * 4,
        out_shape=[_sds(w.shape, F32)] * 4, compiler_params=_cp("arbitrary", "arbitrary"),
    )(w, g0, g1, m, v)


def _to_bf16_slot(name, w, l, where):
    _, rows, cols = w.shape
    tr = _row_tile(rows, cols, 4)

    def body(w_ref, x_ref, o_ref):
        o_ref[...] = x_ref[...].astype(BF16)

    return pl.pallas_call(
        body, name=name,
        grid_spec=pltpu.PrefetchScalarGridSpec(
            num_scalar_prefetch=1, grid=(rows // tr,), in_specs=[pl.BlockSpec((None, tr, cols), lambda i, w_ref: (l, i, 0))],
            out_specs=pl.BlockSpec((None, tr, cols), lambda i, w_ref: (w_ref[0], i, 0))),
        out_shape=_sds((N_CHIPS, rows, cols), BF16), compiler_params=_cp("parallel"))(where, w)


BIG = ("w_in", "w_pool_up", "w_sb_up", "w_gdn_up", "w_out", "w_ff1", "w_ff2")
SMALL = (("attn_norm", (D,)), ("pool_w", (4, 128, 128)), ("pool_scale", (POOL_W,)), ("gdn_a_log", (HEADS,)),
         ("gdn_dt_bias", (HEADS,)), ("gdn_norm", (HD,)), ("mlp_norm", (D,)))


PACK_TILE = 8 * 128


def _rows128(a):
    flat = a.reshape(-1)
    pad = (-flat.shape[0]) % PACK_TILE
    return jnp.pad(flat, (0, pad)).reshape(-1, 128)


def _pack(parts):
    packed = jnp.concatenate([_rows128(p) for p in parts], axis=0)
    return jnp.pad(packed, ((0, (-packed.shape[0]) % 8), (0, 0)))


def _unpack(packed, shapes):
    out, r = [], 0
    for shp in shapes:
        size = 1
        for s in shp:
            size *= s
        nr = -(-size // PACK_TILE) * 8
        out.append(packed[r:r + nr].reshape(-1)[:size].reshape(shp))
        r += nr
    return out


def kernel(x, attn_norm, w_in, pool_w, pool_scale, gdn_conv, gdn_a_log, gdn_dt_bias, gdn_norm, w_pool_up, w_sb_up, w_gdn_up, w_out, mlp_norm, w_ff1, w_ff2, final_norm, loss_target, m_attn_norm, m_w_in, m_pool_w, m_pool_scale, m_gdn_conv, m_gdn_a_log, m_gdn_dt_bias, m_gdn_norm, m_w_pool_up, m_w_sb_up, m_w_gdn_up, m_w_out, m_mlp_norm, m_w_ff1, m_w_ff2, m_final_norm, v_attn_norm, v_w_in, v_pool_w, v_pool_scale, v_gdn_conv, v_gdn_a_log, v_gdn_dt_bias, v_gdn_norm, v_w_pool_up, v_w_sb_up, v_w_gdn_up, v_w_out, v_mlp_norm, v_w_ff1, v_w_ff2, v_final_norm):
    weights = dict(attn_norm=attn_norm, w_in=w_in, pool_w=pool_w, pool_scale=pool_scale, gdn_conv=gdn_conv,
                   gdn_a_log=gdn_a_log, gdn_dt_bias=gdn_dt_bias, gdn_norm=gdn_norm, w_pool_up=w_pool_up, w_sb_up=w_sb_up,
                   w_gdn_up=w_gdn_up, w_out=w_out, mlp_norm=mlp_norm, w_ff1=w_ff1, w_ff2=w_ff2, final_norm=final_norm)
    mom1 = dict(attn_norm=m_attn_norm, w_in=m_w_in, pool_w=m_pool_w, pool_scale=m_pool_scale, gdn_conv=m_gdn_conv,
                gdn_a_log=m_gdn_a_log, gdn_dt_bias=m_gdn_dt_bias, gdn_norm=m_gdn_norm, w_pool_up=m_w_pool_up,
                w_sb_up=m_w_sb_up, w_gdn_up=m_w_gdn_up, w_out=m_w_out, mlp_norm=m_mlp_norm, w_ff1=m_w_ff1, w_ff2=m_w_ff2,
                final_norm=m_final_norm)
    mom2 = dict(attn_norm=v_attn_norm, w_in=v_w_in, pool_w=v_pool_w, pool_scale=v_pool_scale, gdn_conv=v_gdn_conv,
                gdn_a_log=v_gdn_a_log, gdn_dt_bias=v_gdn_dt_bias, gdn_norm=v_gdn_norm, w_pool_up=v_w_pool_up,
                w_sb_up=v_w_sb_up, w_gdn_up=v_w_gdn_up, w_out=v_w_out, mlp_norm=v_mlp_norm, w_ff1=v_w_ff1, w_ff2=v_w_ff2,
                final_norm=v_final_norm)
    xi, yi, ci = lax.axis_index("x"), lax.axis_index("y"), lax.axis_index("c")
    chip = 2 * xi + yi
    where = jnp.stack([chip, ci]).astype(jnp.int32)

    bufs = [[_to_bf16_slot(f"cast_{nm}_{l}", weights[nm], l, where) for nm in BIG] for l in range(2)]
    first = _gather_weights_async(bufs[0][:1], "0_w_in", 4)
    first, _ = lax.optimization_barrier((first, (bufs[0][1:], bufs[1])))
    rest, _ = lax.optimization_barrier((bufs[0][1:], first))
    gw = [dict(zip(BIG, list(first) + _gather_weights_async(rest, "0_rest", 1))), {}]

    def reached(l, stage, value):
        if l == 0 and stage == "proj":
            later, _ = lax.optimization_barrier((bufs[1][:1], value))
            gw[1]["w_in"] = _gather_weights_async(later, "1_w_in", 2)[0]
        if l == 0 and stage == "merged":
            later, _ = lax.optimization_barrier((bufs[1][1:], value))
            gw[1].update(zip(BIG[1:], _gather_weights_async(later, "1_rest", 3)))

    def weights_of(l):
        return gw[l], _w_in_from_shards(gw[l]["w_in"])

    conv_cols = gdn_conv.shape[-1]
    conv_place = lax.dynamic_update_slice(jnp.zeros((2, GDN_CONV, N_CHIPS * conv_cols), F32),
                                          jnp.where(ci == 0, gdn_conv, 0.0), (0, 0, chip * conv_cols))
    conv_full = _all_reduce_small("gather_conv", _rows128(conv_place)).reshape(2, GDN_CONV, N_CHIPS * conv_cols)
    sp = dict(attn_norm=attn_norm.reshape(2, 1, D), pool_w=pool_w, pool_scale=pool_scale.reshape(2, 1, POOL_W),
              conv=conv_full, a_log=jnp.stack([_row128(gdn_a_log[l]) for l in range(2)]),
              dt_bias=jnp.stack([_row128(gdn_dt_bias[l]) for l in range(2)]), gdn_norm=gdn_norm.reshape(2, 1, HD),
              mlp_norm=mlp_norm.reshape(2, 1, D), final_norm=final_norm.reshape(1, D))

    started = []

    def emit(l, names, g, v):
        parts, got = _rs_begin([g[nm] for nm in names], where, f"{l}_{names[0]}", 5 + len(started))
        started.append((l, names, parts, got))
        v, _ = lax.optimization_barrier((v, parts))
        return v

    loss, grad_x, grads, g_final = _local_step(x[0], loss_target[0], weights_of, sp, emit, reached)
    big_grads = {nm: [None, None] for nm in BIG}
    for l, names, parts, got in started[:-1]:
        got, _ = lax.optimization_barrier((got, grad_x))
        for nm, red in zip(names, _rs_finish(parts, got, where)):
            big_grads[nm][l] = red
    small_parts, small_shapes = [], []
    for l in range(2):
        g = grads[l]
        for nm, shp in SMALL:
            key = {"gdn_a_log": "a_log", "gdn_dt_bias": "dt_bias"}.get(nm, nm)
            val = g[key]
            small_parts.append(val[0, :HEADS] if nm in ("gdn_a_log", "gdn_dt_bias") else val)
            small_shapes.append(shp)
        small_parts.append(g["conv"])
        small_shapes.append((GDN_CONV, N_CHIPS * conv_cols))
    small_parts += [g_final, loss]
    small_shapes += [(D,), (1, 1)]
    small_pack = _pack(small_parts)

    grad, delta, new_m, new_v = {}, {}, {}, {}
    for nm in BIG[1:]:
        grad[nm], delta[nm], new_m[nm], new_v[nm] = _adamw_layers("adamw_" + nm, weights[nm], *big_grads[nm], mom1[nm], mom2[nm])
    l, names, parts, got = started[-1]
    got, _ = lax.optimization_barrier((got, [new_v[nm] for nm in BIG[1:]]))
    for nm, red in zip(names, _rs_finish(parts, got, where)):
        big_grads[nm][l] = red
    nm = BIG[0]
    grad[nm], delta[nm], new_m[nm], new_v[nm] = _adamw_layers("adamw_" + nm, weights[nm], *big_grads[nm], mom1[nm], mom2[nm])
    small_pack, _ = lax.optimization_barrier((small_pack, new_v[nm]))
    reduced = _unpack(_all_reduce_small("reduce_small", small_pack), small_shapes)
    per = len(SMALL) + 1
    for i, (nm, _) in enumerate(SMALL):
        grad[nm] = jnp.stack([reduced[i], reduced[per + i]])
    conv_g = jnp.stack([reduced[per - 1], reduced[2 * per - 1]])
    grad["gdn_conv"] = lax.dynamic_slice(conv_g, (0, 0, chip * conv_cols), (2, GDN_CONV, conv_cols))
    grad["final_norm"] = reduced[-2]
    loss = reduced[-1][0, 0]
    small_names = [nm for nm, _ in SMALL] + ["gdn_conv", "final_norm"]
    packs = [_pack([src[nm] for nm in small_names]) for src in (weights, grad, mom1, mom2)]
    outs = _adamw("adamw_small", *packs)
    shapes = [weights[nm].shape for nm in small_names]
    for dst, packed in zip((delta, new_m, new_v), outs):
        for nm, val in zip(small_names, _unpack(packed, shapes)):
            dst[nm] = val

    order = ("attn_norm", "w_in", "pool_w", "pool_scale", "gdn_conv", "gdn_a_log", "gdn_dt_bias", "gdn_norm", "w_pool_up",
             "w_sb_up", "w_gdn_up", "w_out", "mlp_norm", "w_ff1", "w_ff2", "final_norm")
    return (loss, grad_x[None], *[grad[n] for n in order], *[delta[n] for n in order], *[new_m[n] for n in order],
            *[new_v[n] for n in order])
```

```python
import functools

import jax
import jax.numpy as jnp
from jax import lax
from jax.experimental import pallas as pl
from jax.experimental.pallas import tpu as pltpu
from jax.experimental.pallas import tpu_sc as plsc

F32, BF16 = jnp.float32, jnp.bfloat16
HIGH = lax.Precision.HIGH
MESH = pl.DeviceIdType.MESH

D = 2048
EPS = 1e-6
POOL_WINDOWS = (2, 4, 8, 16)
POOL_W, SB_W, GDN_W = 512, 768, 768
HEADS, HD = 6, 128
SB_BLOCK = 128
GDN_CHUNK = 64
D_FF = 4 * D
N_IN = 12044
N_CHIPS = 4
OFF_SB, OFF_GQKV, OFF_P, OFF_Z, OFF_AB, OFF_GATE = 0, 2304, 4608, 5120, 5888, 6144
AB_W = 256
ZAB_W = GDN_W + AB_W
ORIG_SB, ORIG_Z, ORIG_AB, ORIG_GATE = 512, 5120, 5888, 5900
N_AL = 12288
VMEM_LIMIT = 48 * 1024 * 1024

ADAM_LR, ADAM_B1, ADAM_B2, ADAM_EPS, ADAM_WD, ADAM_STEP = 0.001, 0.9, 0.999, 1e-08, 0.01, 10

NT = (((1,), (1,)), ((), ()))
TN = (((0,), (0,)), ((), ()))


def _cp(*sem):
    return pltpu.CompilerParams(dimension_semantics=sem, vmem_limit_bytes=VMEM_LIMIT)


def _dot(a, b, dims=None, precision=None):
    if dims is None:
        dims = (((a.ndim - 1,), (0,)), ((), ()))
    return lax.dot_general(a, b, dims, precision=precision, preferred_element_type=F32)


def _hdot(a, b, dims=None):
    return _dot(a, b, dims, precision=HIGH)


def _bdot(a, b, dims=None):
    return _dot(a.astype(BF16), b.astype(BF16), dims)


def _mm(name, a, b, *, m, n, k, tm, tn, tk, a_spec, b_spec, dims, out_shapes, out_specs,
        extras=(), extra_specs=(), epilogue=None):
    nk = k // tk
    ne, no = len(extras), len(out_shapes)

    def body(*refs):
        a_ref, b_ref = refs[0], refs[1]
        ex = refs[2:2 + ne]
        outs = refs[2 + ne:2 + ne + no]
        kk = pl.program_id(2)

        def finish(r):
            res = epilogue(r, *[e[...] for e in ex]) if epilogue is not None else (r,)
            for o, v in zip(outs, res):
                o[...] = v.astype(o.dtype)

        part = _dot(a_ref[...].astype(BF16), b_ref[...].astype(BF16), dims)
        if nk == 1:
            finish(part)
            return
        acc = refs[-1]

        @pl.when(kk == 0)
        def _():
            acc[...] = part

        @pl.when((kk > 0) & (kk < nk - 1))
        def _():
            acc[...] += part

        @pl.when(kk == nk - 1)
        def _():
            finish(acc[...] + part)

    return pl.pallas_call(
        body, name=name, grid=(m // tm, n // tn, nk),
        in_specs=[a_spec, b_spec, *extra_specs], out_specs=out_specs, out_shape=out_shapes,
        scratch_shapes=[] if nk == 1 else [pltpu.VMEM((tm, tn), F32)],
        compiler_params=_cp("parallel", "parallel", "arbitrary"),
    )(a, b, *extras)


def _a_plain(tm, tk):
    return pl.BlockSpec((tm, tk), lambda i, j, kk: (i, kk))


def _a_trans(tm, tk):
    return pl.BlockSpec((tk, tm), lambda i, j, kk: (kk, i))


def _b_plain(tk, tn):
    return pl.BlockSpec((tk, tn), lambda i, j, kk: (kk, j))


def _b_trans(tk, tn):
    return pl.BlockSpec((tn, tk), lambda i, j, kk: (j, kk))


def _o_plain(tm, tn):
    return pl.BlockSpec((tm, tn), lambda i, j, kk: (i, j))


def _o_colshard(tm, tn, ns_cols):
    per = ns_cols // tn
    return pl.BlockSpec((None, tm, tn), lambda i, j, kk: (j // per, i, j % per))


def _w_cols(tk, tn, ns):
    per = ns // tn
    return pl.BlockSpec((None, tk, tn), lambda i, j, kk: (j // per, kk, j % per))


def _w_cols_t(tk, tn, ns):
    per = ns // tk
    return pl.BlockSpec((None, tn, tk), lambda i, j, kk: (kk // per, j, kk % per))


def _w_rows(tk, tn, ks):
    per = ks // tk
    return pl.BlockSpec((None, tk, tn), lambda i, j, kk: (kk // per, kk % per, j))


def _w_rows_t(tk, tn, ks):
    per = ks // tn
    return pl.BlockSpec((None, tn, tk), lambda i, j, kk: (j // per, j % per, kk))


def _sds(shape, dtype):
    return jax.ShapeDtypeStruct(shape, dtype)


def _rms_fwd(name, x, gain):
    t = x.shape[0]
    tt = min(256, t)

    def body(x_ref, g_ref, u_ref):
        xv = x_ref[...]
        r = lax.rsqrt(jnp.mean(xv * xv, axis=-1, keepdims=True) + EPS)
        u_ref[...] = (xv * r * g_ref[...]).astype(u_ref.dtype)

    return pl.pallas_call(
        body, name=name, grid=(t // tt,),
        in_specs=[pl.BlockSpec((tt, D), lambda i: (i, 0)), pl.BlockSpec((1, D), lambda i: (0, 0))],
        out_specs=pl.BlockSpec((tt, D), lambda i: (i, 0)), out_shape=_sds((t, D), BF16),
        compiler_params=_cp("parallel"),
    )(x, gain)


def _rms_bwd(name, du, x, gain, dres):
    t = x.shape[0]
    tt = min(256, t)

    def body(du_ref, x_ref, g_ref, dres_ref, dx_ref, dg_ref):
        @pl.when(pl.program_id(0) == 0)
        def _():
            dg_ref[...] = jnp.zeros_like(dg_ref)

        xv, duv = x_ref[...], du_ref[...]
        r = lax.rsqrt(jnp.mean(xv * xv, axis=-1, keepdims=True) + EPS)
        nx = xv * r
        dn = duv * g_ref[...]
        dg_ref[...] += jnp.sum(duv * nx, axis=0, keepdims=True)
        dx_ref[...] = dres_ref[...] + r * (dn - nx * jnp.mean(dn * nx, axis=-1, keepdims=True))

    row = pl.BlockSpec((tt, D), lambda i: (i, 0))
    vec = pl.BlockSpec((1, D), lambda i: (0, 0))
    return pl.pallas_call(
        body, name=name, grid=(t // tt,), in_specs=[row, row, vec, row], out_specs=[row, vec],
        out_shape=[_sds((t, D), F32), _sds((1, D), F32)], compiler_params=_cp("arbitrary"),
    )(du, x, gain, dres)


def _loss_head(x, gain, target):
    t = x.shape[0]
    tt = min(256, t)

    def body(x_ref, g_ref, t_ref, loss_ref, dx_ref, dg_ref):
        @pl.when(pl.program_id(0) == 0)
        def _():
            dg_ref[...] = jnp.zeros_like(dg_ref)
            loss_ref[...] = jnp.zeros_like(loss_ref)

        xv = x_ref[...]
        r = lax.rsqrt(jnp.mean(xv * xv, axis=-1, keepdims=True) + EPS)
        nx = xv * r
        err = nx * g_ref[...] - t_ref[...]
        loss_ref[...] += 0.5 * jnp.sum(jnp.mean(err * err, axis=-1, keepdims=True), axis=0, keepdims=True)
        dy = err * (1.0 / D)
        dn = dy * g_ref[...]
        dg_ref[...] += jnp.sum(dy * nx, axis=0, keepdims=True)
        dx_ref[...] = r * (dn - nx * jnp.mean(dn * nx, axis=-1, keepdims=True))

    row = pl.BlockSpec((tt, D), lambda i: (i, 0))
    vec = pl.BlockSpec((1, D), lambda i: (0, 0))
    one = pl.BlockSpec((1, 1), lambda i: (0, 0))
    return pl.pallas_call(
        body, name="loss_head", grid=(t // tt,), in_specs=[row, vec, row], out_specs=[one, row, vec],
        out_shape=[_sds((1, 1), F32), _sds((t, D), F32), _sds((1, D), F32)], compiler_params=_cp("arbitrary"),
    )(x, gain, target)


def _shift_down(v, s, t_idx):
    return jnp.where(t_idx >= s, pltpu.roll(v, s, 0), 0.0)


def _shift_up(v, s, t_idx, t):
    return jnp.where(t_idx < t - s, pltpu.roll(v, t - s, 0), 0.0)


def _pool_d(p, g, t_idx):
    s = p
    for step in range(g + 1):
        s = s + _shift_down(s, 1 << step, t_idx)
    cnt = jnp.minimum(t_idx + 1, POOL_WINDOWS[g]).astype(F32)
    return s / cnt - p, cnt


def _pool_fwd(proj, pool_w, pool_scale):
    t = proj.shape[0]
    g128 = POOL_W // len(POOL_WINDOWS)

    def body(p_ref, w_ref, s_ref, y_ref):
        t_idx = lax.broadcasted_iota(jnp.int32, (t, g128), 0)
        for g in range(len(POOL_WINDOWS)):
            sl = slice(g * g128, (g + 1) * g128)
            d, _ = _pool_d(p_ref[:, sl], g, t_idx)
            y_ref[:, sl] = (_bdot(d, w_ref[g]) * s_ref[:, sl]).astype(y_ref.dtype)

    return pl.pallas_call(
        body, name="pool_fwd", grid=(1,),
        in_specs=[pl.BlockSpec((t, POOL_W), lambda i: (0, OFF_P // POOL_W)),
                  pl.BlockSpec((4, g128, g128), lambda i: (0, 0, 0)), pl.BlockSpec((1, POOL_W), lambda i: (0, 0))],
        out_specs=pl.BlockSpec((t, POOL_W), lambda i: (0, 0)), out_shape=_sds((t, POOL_W), BF16),
        compiler_params=_cp("arbitrary"),
    )(proj, pool_w, pool_scale)


def _pool_bwd(proj, pool_w, pool_scale, dy, dproj):
    t = proj.shape[0]
    g128 = POOL_W // len(POOL_WINDOWS)

    def body(p_ref, w_ref, s_ref, dy_ref, _, dp_ref, dw_ref, ds_ref):
        t_idx = lax.broadcasted_iota(jnp.int32, (t, g128), 0)
        for g in range(len(POOL_WINDOWS)):
            sl = slice(g * g128, (g + 1) * g128)
            d, cnt = _pool_d(p_ref[:, sl], g, t_idx)
            dyv = dy_ref[:, sl].astype(F32)
            ds_ref[:, sl] = jnp.sum(dyv * _bdot(d, w_ref[g]), axis=0, keepdims=True)
            dys = dyv * s_ref[:, sl]
            dw_ref[g] = _bdot(d, dys, TN)
            dd = _bdot(dys, w_ref[g], NT)
            s = dd / cnt
            for step in range(g + 1):
                s = s + _shift_up(s, 1 << step, t_idx, t)
            dp_ref[:, sl] = (s - dd).astype(dp_ref.dtype)

    return pl.pallas_call(
        body, name="pool_bwd", grid=(1,),
        in_specs=[pl.BlockSpec((t, POOL_W), lambda i: (0, OFF_P // POOL_W)),
                  pl.BlockSpec((4, g128, g128), lambda i: (0, 0, 0)), pl.BlockSpec((1, POOL_W), lambda i: (0, 0)),
                  pl.BlockSpec((t, POOL_W), lambda i: (0, 0)), ANY],
        out_specs=[pl.BlockSpec((t, POOL_W), lambda i: (0, OFF_P // POOL_W)), pl.BlockSpec((4, g128, g128), lambda i: (0, 0, 0)),
                   pl.BlockSpec((1, POOL_W), lambda i: (0, 0))],
        out_shape=[_sds(dproj.shape, dproj.dtype), _sds((4, g128, g128), F32), _sds((1, POOL_W), F32)],
        input_output_aliases={4: 0}, compiler_params=_cp("arbitrary"),
    )(proj, pool_w, pool_scale, dy, dproj)


SB_GROUP = 3
SB_GW = SB_GROUP * HD


def _sb_cast_kv(proj):
    t = proj.shape[0]
    tt = min(512, t)

    def body(x_ref, o_ref):
        o_ref[...] = x_ref[...].astype(BF16)

    return pl.pallas_call(
        body, name="sb_cast_kv", grid=(t // tt, 2),
        in_specs=[pl.BlockSpec((tt, SB_W), lambda i, j: (i, OFF_SB // SB_W + 1 + j))],
        out_specs=pl.BlockSpec((tt, SB_W), lambda i, j: (i, j)), out_shape=_sds((t, 2 * SB_W), BF16),
        compiler_params=_cp("parallel", "parallel"),
    )(proj)


def _sb_specs(t):
    q_spec = pl.BlockSpec((SB_BLOCK, SB_GW), lambda g, i: (i, OFF_SB // SB_GW + g))
    k_spec = pl.BlockSpec((t, SB_GW), lambda g, i: (0, g))
    v_spec = pl.BlockSpec((t, SB_GW), lambda g, i: (0, SB_W // SB_GW + g))
    return q_spec, k_spec, v_spec


def _head(ref, h, rows=None):
    cols = slice(h * HD, (h + 1) * HD)
    return ref[:, cols] if rows is None else ref[rows, cols]


SB_KEYS = 512


def _sub(v, b):
    return v[:, b * SB_BLOCK:(b + 1) * SB_BLOCK]


def _sb_keep(kc, limit):
    row = lax.broadcasted_iota(jnp.int32, (SB_BLOCK, kc), 0)
    col = lax.broadcasted_iota(jnp.int32, (SB_BLOCK, kc), 1)
    return col < row + limit


def _sb_chunk(q, keys, run, later, limit):
    kc = keys.shape[0]
    z = _dot(q, keys, NT)
    lsz = jax.nn.log_sigmoid(z)
    ls = lsz - z
    if limit is not None:
        keep = _sb_keep(kc, limit)
        ls = jnp.where(keep, ls, 0.0)
    parts = [None] * (kc // SB_BLOCK)
    for b in reversed(range(kc // SB_BLOCK)):
        parts[b] = _hdot(_sub(ls, b), later) + run
        run = run + jnp.sum(_sub(ls, b), axis=1, keepdims=True)
    a = jnp.exp(lsz + jnp.concatenate(parts, axis=1))
    if limit is not None:
        a = jnp.where(keep, a, 0.0)
    return z, a, run


def _sb_fwd(proj, kv):
    t = proj.shape[0]
    kc = min(SB_KEYS, t)
    scale = HD ** -0.5

    def body(q_ref, k_ref, v_ref, o_ref):
        i = pl.program_id(1)
        top = (i * SB_BLOCK) // kc
        qs = [(_head(q_ref, h) * scale).astype(BF16) for h in range(SB_GROUP)]
        row = lax.broadcasted_iota(jnp.int32, (SB_BLOCK, SB_BLOCK), 0)
        col = lax.broadcasted_iota(jnp.int32, (SB_BLOCK, SB_BLOCK), 1)
        later = (row > col).astype(F32)

        def chunk(jc, carry, masked):
            rows = pl.ds(pl.multiple_of(jc * kc, kc), kc)
            limit = i * SB_BLOCK - jc * kc if masked else None
            out = []
            for h in range(SB_GROUP):
                acc, run = carry[h]
                _, a, run = _sb_chunk(qs[h], _head(k_ref, h, rows), run, later, limit)
                out.append((acc + _dot(a.astype(BF16), _head(v_ref, h, rows)), run))
            return tuple(out)

        zero = tuple((jnp.zeros((SB_BLOCK, HD), F32), jnp.zeros((SB_BLOCK, 1), F32)) for _ in range(SB_GROUP))
        carry = chunk(top, zero, True)
        carry = lax.fori_loop(0, top, lambda jj, c: chunk(top - 1 - jj, c, False), carry)
        for h in range(SB_GROUP):
            o_ref[:, h * HD:(h + 1) * HD] = carry[h][0].astype(o_ref.dtype)

    return pl.pallas_call(
        body, name="sb_fwd", grid=(HEADS // SB_GROUP, t // SB_BLOCK), in_specs=list(_sb_specs(t)),
        out_specs=pl.BlockSpec((SB_BLOCK, SB_GW), lambda g, i: (i, g)), out_shape=_sds((t, SB_W), BF16),
        compiler_params=_cp("parallel", "arbitrary"),
    )(proj, kv, kv)


def _sb_bwd(proj, kv, dy):
    t = proj.shape[0]
    nq = t // SB_BLOCK
    kc = min(SB_KEYS, t)
    scale = HD ** -0.5

    def body(q_ref, k_ref, v_ref, do_ref, dq_ref, dk_ref, dv_ref, z_scr, e_scr):
        i = pl.program_id(1)
        top = (i * SB_BLOCK) // kc

        @pl.when(i == 0)
        def _():
            dk_ref[...] = jnp.zeros_like(dk_ref)
            dv_ref[...] = jnp.zeros_like(dv_ref)

        qs = [(_head(q_ref, h) * scale).astype(BF16) for h in range(SB_GROUP)]
        dos = [_head(do_ref, h).astype(BF16) for h in range(SB_GROUP)]
        row = lax.broadcasted_iota(jnp.int32, (SB_BLOCK, SB_BLOCK), 0)
        col = lax.broadcasted_iota(jnp.int32, (SB_BLOCK, SB_BLOCK), 1)
        later = (row > col).astype(F32)
        earlier = (row < col).astype(F32)

        def down(jc, runs, masked):
            rows = pl.ds(pl.multiple_of(jc * kc, kc), kc)
            limit = i * SB_BLOCK - jc * kc if masked else None
            out = []
            for h in range(SB_GROUP):
                z, a, run = _sb_chunk(qs[h], _head(k_ref, h, rows), runs[h], later, limit)
                z_scr[h, jc] = z
                e_scr[h, jc] = a * _dot(dos[h], _head(v_ref, h, rows), NT)
                dv_ref[rows, h * HD:(h + 1) * HD] += _dot(a.astype(BF16), dos[h], TN)
                out.append(run)
            return tuple(out)

        zero = tuple(jnp.zeros((SB_BLOCK, 1), F32) for _ in range(SB_GROUP))
        runs = down(top, zero, True)
        lax.fori_loop(0, top, lambda jj, r: down(top - 1 - jj, r, False), runs)

        def up(jc, carry, masked):
            rows = pl.ds(pl.multiple_of(jc * kc, kc), kc)
            out = []
            for h in range(SB_GROUP):
                dq, run = carry[h]
                z, e = z_scr[h, jc], e_scr[h, jc]
                parts = []
                for b in range(kc // SB_BLOCK):
                    parts.append(_hdot(_sub(e, b), earlier) + run)
                    run = run + jnp.sum(_sub(e, b), axis=1, keepdims=True)
                sz = jax.nn.sigmoid(z)
                dz = e * (1.0 - sz) - jnp.concatenate(parts, axis=1) * sz
                if masked:
                    dz = jnp.where(_sb_keep(kc, i * SB_BLOCK - jc * kc), dz, 0.0)
                dz = dz.astype(BF16)
                dk_ref[rows, h * HD:(h + 1) * HD] += _dot(dz, qs[h], TN)
                out.append((dq + _dot(dz, _head(k_ref, h, rows)), run))
            return tuple(out)

        zero = tuple((jnp.zeros((SB_BLOCK, HD), F32), jnp.zeros((SB_BLOCK, 1), F32)) for _ in range(SB_GROUP))
        carry = lax.fori_loop(0, top, lambda jc, c: up(jc, c, False), zero)
        carry = up(top, carry, True)
        for h in range(SB_GROUP):
            dq_ref[:, h * HD:(h + 1) * HD] = (carry[h][0] * scale).astype(dq_ref.dtype)

    blk = pl.BlockSpec((SB_BLOCK, SB_GW), lambda g, i: (i, g))
    seq = pl.BlockSpec((t, SB_GW), lambda g, i: (0, g))
    scratch = pltpu.VMEM((SB_GROUP, t // kc, SB_BLOCK, kc), F32)
    return pl.pallas_call(
        body, name="sb_bwd", grid=(HEADS // SB_GROUP, nq), in_specs=[*_sb_specs(t), blk], out_specs=[blk, seq, seq],
        out_shape=[_sds((t, SB_W), BF16), _sds((t, SB_W), F32), _sds((t, SB_W), F32)],
        scratch_shapes=[scratch, scratch], compiler_params=_cp("parallel", "arbitrary"),
    )(proj, kv, kv, dy)


CONV_TILE = 256
GDN_CONV = 4


def _conv_pre(x, w_ref, t_idx):
    pre = w_ref[GDN_CONV - 1:GDN_CONV, :] * x
    for s in range(1, GDN_CONV):
        pre = pre + w_ref[GDN_CONV - 1 - s:GDN_CONV - s, :] * _shift_down(x, s, t_idx)
    return pre


def _conv_fwd(proj, conv_w):
    t = proj.shape[0]
    width = conv_w.shape[1]

    def body(x_ref, w_ref, y_ref):
        t_idx = lax.broadcasted_iota(jnp.int32, (t, CONV_TILE), 0)
        pre = _conv_pre(x_ref[...], w_ref, t_idx)
        y_ref[...] = pre * jax.nn.sigmoid(pre)

    return pl.pallas_call(
        body, name="conv_fwd", grid=(width // CONV_TILE,),
        in_specs=[pl.BlockSpec((t, CONV_TILE), lambda c: (0, OFF_GQKV // CONV_TILE + c)),
                  pl.BlockSpec((GDN_CONV, CONV_TILE), lambda c: (0, c))],
        out_specs=pl.BlockSpec((t, CONV_TILE), lambda c: (0, c)), out_shape=_sds((t, width), F32),
        compiler_params=_cp("parallel"),
    )(proj, conv_w)


def _conv_bwd(proj, conv_w, dc, dproj):
    t = proj.shape[0]
    width = dc.shape[1]
    per = width // CONV_TILE
    first = OFF_GQKV // CONV_TILE

    def body(x_ref, w_ref, dc_ref, _, dx_ref, dw_ref):
        t_idx = lax.broadcasted_iota(jnp.int32, (t, CONV_TILE), 0)
        x = x_ref[...]
        pre = _conv_pre(x, w_ref, t_idx)
        sg = jax.nn.sigmoid(pre)
        dpre = dc_ref[...] * (sg * (1.0 + pre * (1.0 - sg)))
        dx = w_ref[GDN_CONV - 1:GDN_CONV, :] * dpre
        dw_ref[GDN_CONV - 1:GDN_CONV, :] = jnp.sum(dpre * x, axis=0, keepdims=True)
        for s in range(1, GDN_CONV):
            dx = dx + w_ref[GDN_CONV - 1 - s:GDN_CONV - s, :] * _shift_up(dpre, s, t_idx, t)
            dw_ref[GDN_CONV - 1 - s:GDN_CONV - s, :] = jnp.sum(dpre * _shift_down(x, s, t_idx), axis=0, keepdims=True)
        dx_ref[...] = dx.astype(dx_ref.dtype)

    return pl.pallas_call(
        body, name="conv_bwd", grid=(per,),
        in_specs=[pl.BlockSpec((t, CONV_TILE), lambda c: (0, first + c)),
                  pl.BlockSpec((GDN_CONV, CONV_TILE), lambda c: (0, c)),
                  pl.BlockSpec((t, CONV_TILE), lambda c: (0, c)), ANY],
        out_specs=[pl.BlockSpec((t, CONV_TILE), lambda c: (0, first + c)), pl.BlockSpec((GDN_CONV, CONV_TILE), lambda c: (0, c))],
        out_shape=[_sds(dproj.shape, dproj.dtype), _sds((GDN_CONV, width), F32)],
        input_output_aliases={3: 0}, compiler_params=_cp("parallel"),
    )(proj, conv_w, dc, dproj)


def _heads(x):
    return jnp.concatenate([x[:, h * HD:(h + 1) * HD][None] for h in range(HEADS)], axis=0)


def _hb(a, b, ca=2, cb=1):
    return lax.dot_general(a, b, (((ca,), (cb,)), ((0,), (0,))), precision=HIGH, preferred_element_type=F32)


@jax.custom_vjp
def _unit_lower_inverse(lower):
    c = lower.shape[-1]
    eye = lax.broadcasted_iota(jnp.int32, (c, c), 0) == lax.broadcasted_iota(jnp.int32, (c, c), 1)
    inv = jnp.where(eye, 1.0, 0.0) - lower
    pw = _hb(lower, lower)
    for step in range(5):
        inv = inv + _hb(inv, pw)
        if step < 4:
            pw = _hb(pw, pw)
    return inv


def _unit_lower_inverse_fwd(lower):
    inv = _unit_lower_inverse(lower)
    return inv, inv


def _unit_lower_inverse_bwd(inv, d_inv):
    return (-_hb(_hb(inv, d_inv, 1, 1), inv, 2, 2),)


_unit_lower_inverse.defvjp(_unit_lower_inverse_fwd, _unit_lower_inverse_bwd)


@jax.custom_vjp
def _known_inverse(lower, inv):
    return inv


_known_inverse.defvjp(lambda lower, inv: (inv, inv),
                      lambda inv, d_inv: (*_unit_lower_inverse_bwd(inv, d_inv), jnp.zeros_like(inv)))


def _gdn_prep(cq, ck, cv, ab, alog_row, dtb_row, inv=None, keep_inverse=False):
    c = GDN_CHUNK
    row = lax.broadcasted_iota(jnp.int32, (c, c), 0)
    col = lax.broadcasted_iota(jnp.int32, (c, c), 1)
    incl, strict, eye = row >= col, row > col, row == col
    def lanes(v, first):
        return jnp.concatenate([v[:, first + h:first + h + 1][None] for h in range(HEADS)], axis=0)

    a_col, b_col = lanes(ab, 0), lanes(ab, HEADS)
    a_log, dt_bias = lanes(alog_row, 0), lanes(dtb_row, 0)
    qn = cq * lax.rsqrt(jnp.sum(cq * cq, axis=-1, keepdims=True) + EPS) * (HD ** -0.5)
    kn = ck * lax.rsqrt(jnp.sum(ck * ck, axis=-1, keepdims=True) + EPS)
    la_col = -jnp.exp(a_log) * jax.nn.softplus(a_col + dt_bias)
    beta = jax.nn.sigmoid(b_col)
    la_row = jnp.sum(jnp.where(eye, la_col, 0.0), axis=1, keepdims=True)
    g_col = jnp.sum(jnp.where(incl, la_row, 0.0), axis=2, keepdims=True)
    g_row = jnp.sum(jnp.where(row <= col, la_col, 0.0), axis=1, keepdims=True)
    g_last = jnp.sum(la_col, axis=1, keepdims=True)
    gamma = jnp.where(incl, jnp.exp(jnp.where(incl, g_col - g_row, 0.0)), 0.0)
    lower = jnp.where(strict, beta * _hb(kn, kn, 2, 2) * gamma, 0.0)
    inv = _unit_lower_inverse(lower) if inv is None else _known_inverse(lower, inv)
    u = _hb(inv, cv * beta)
    w = _hb(inv, kn * (beta * jnp.exp(g_col)))
    qk = _hb(qn, kn, 2, 2) * gamma
    out = (u, w, qk, qn * jnp.exp(g_col), kn * jnp.exp(g_last - g_col), jnp.exp(g_last))
    return (*out, inv) if keep_inverse else out


def _gdn_post(o, z, gain):
    y = o * lax.rsqrt(jnp.mean(o * o, axis=-1, keepdims=True) + EPS) * gain
    return y * (z * jax.nn.sigmoid(z))


def _gdn_specs(nc, reverse):
    c = GDN_CHUNK

    def ch(n):
        return nc - 1 - n if reverse else n

    def wide(array_off):
        return pl.BlockSpec((c, GDN_W), lambda n: (ch(n), array_off // GDN_W))

    zab = pl.BlockSpec((c, ZAB_W), lambda n: (ch(n), OFF_Z // ZAB_W))
    row = pl.BlockSpec((1, HD), lambda n: (0, 0))
    state = pl.BlockSpec((None, HEADS, HD, HD), lambda n: (ch(n), 0, 0, 0))
    inverse = pl.BlockSpec((None, HEADS, c, c), lambda n: (ch(n), 0, 0, 0))
    return wide, zab, row, state, inverse


def _gdn_fwd(cqkv, proj, a_log, dt_bias, gain):
    t = proj.shape[0]
    nc = t // GDN_CHUNK
    wide, zab, row, state, inverse = _gdn_specs(nc, False)

    def body(cq_ref, ck_ref, cv_ref, zab_ref, al_ref, dt_ref, g_ref, y_ref, sprev_ref, inv_ref, s_scr):
        @pl.when(pl.program_id(0) == 0)
        def _():
            s_scr[...] = jnp.zeros_like(s_scr)

        z_ref, ab_ref = zab_ref.at[:, :GDN_W], zab_ref.at[:, GDN_W:GDN_W + HD]
        u, w, qk, qd, kd, dec, inv = _gdn_prep(_heads(cq_ref[...]), _heads(ck_ref[...]), _heads(cv_ref[...]), ab_ref[...],
                                               al_ref[...], dt_ref[...], keep_inverse=True)
        inv_ref[...] = inv
        s = s_scr[...]
        sprev_ref[...] = s
        v_new = u - _hb(w, s)
        o = _hb(qd, s) + _hb(qk, v_new)
        s_scr[...] = s * dec + _hb(kd, v_new, 1, 1)
        y = _gdn_post(o, _heads(z_ref[...]), g_ref[...])
        for h in range(HEADS):
            y_ref[:, h * HD:(h + 1) * HD] = y[h].astype(y_ref.dtype)

    return pl.pallas_call(
        body, name="gdn_fwd", grid=(nc,),
        in_specs=[wide(0), wide(GDN_W), wide(2 * GDN_W), zab, row, row, row],
        out_specs=[wide(0), state, inverse],
        out_shape=[_sds((t, GDN_W), BF16), _sds((nc, HEADS, HD, HD), F32), _sds((nc, HEADS, GDN_CHUNK, GDN_CHUNK), F32)],
        scratch_shapes=[pltpu.VMEM((HEADS, HD, HD), F32)], compiler_params=_cp("arbitrary"),
    )(cqkv, cqkv, cqkv, proj, a_log, dt_bias, gain)


def _gdn_bwd(cqkv, proj, a_log, dt_bias, gain, sprev, inverses, dy, dproj):
    t = proj.shape[0]
    nc = t // GDN_CHUNK
    wide, zab, row, state, inverse = _gdn_specs(nc, True)

    def body(cq_ref, ck_ref, cv_ref, zab_ref, al_ref, dt_ref, g_ref, sp_ref, inv_ref, dy_ref, _,
             dc_ref, dzab_ref, dal_ref, ddt_ref, dg_ref, ds_scr):
        @pl.when(pl.program_id(0) == 0)
        def _():
            ds_scr[...] = jnp.zeros_like(ds_scr)
            dal_ref[...] = jnp.zeros_like(dal_ref)
            ddt_ref[...] = jnp.zeros_like(ddt_ref)
            dg_ref[...] = jnp.zeros_like(dg_ref)

        z_ref, ab_ref = zab_ref.at[:, :GDN_W], zab_ref.at[:, GDN_W:GDN_W + HD]

        (u, w, qk, qd, kd, dec), prep_vjp = jax.vjp(
            functools.partial(_gdn_prep, inv=inv_ref[...]),
            _heads(cq_ref[...]), _heads(ck_ref[...]), _heads(cv_ref[...]), ab_ref[...], al_ref[...], dt_ref[...])
        s = sp_ref[...]
        v_new = u - _hb(w, s)
        o = _hb(qd, s) + _hb(qk, v_new)
        _, post_vjp = jax.vjp(_gdn_post, o, _heads(z_ref[...]), g_ref[...])
        do, dz, dgain = post_vjp(_heads(dy_ref[...]).astype(F32))
        ds_next = ds_scr[...]
        d_vnew = _hb(qk, do, 1, 1) + _hb(kd, ds_next)
        d_qk = _hb(do, v_new, 2, 2)
        d_qd = _hb(do, s, 2, 2)
        d_kd = _hb(v_new, ds_next, 2, 2)
        d_dec = jnp.sum(jnp.sum(s * ds_next, axis=2, keepdims=True), axis=1, keepdims=True)
        ds_scr[...] = dec * ds_next + _hb(qd, do, 1, 1) - _hb(w, d_vnew, 1, 1)
        d_w = -_hb(d_vnew, s, 2, 2)
        dcq, dck, dcv, dab, dal, ddt = prep_vjp((d_vnew, d_w, d_qk, d_qd, d_kd, d_dec))
        for h in range(HEADS):
            dc_ref[:, h * HD:(h + 1) * HD] = dcq[h]
            dc_ref[:, GDN_W + h * HD:GDN_W + (h + 1) * HD] = dck[h]
            dc_ref[:, 2 * GDN_W + h * HD:2 * GDN_W + (h + 1) * HD] = dcv[h]
            dzab_ref[:, h * HD:(h + 1) * HD] = dz[h].astype(dzab_ref.dtype)
        dzab_ref[:, GDN_W:GDN_W + HD] = dab.astype(dzab_ref.dtype)
        dzab_ref[:, GDN_W + HD:] = jnp.zeros((GDN_CHUNK, AB_W - HD), dzab_ref.dtype)
        dal_ref[...] += dal
        ddt_ref[...] += ddt
        dg_ref[...] += dgain

    c = GDN_CHUNK
    return pl.pallas_call(
        body, name="gdn_bwd", grid=(nc,),
        in_specs=[wide(0), wide(GDN_W), wide(2 * GDN_W), zab, row, row, row, state, inverse, wide(0), ANY],
        out_specs=[pl.BlockSpec((c, 3 * GDN_W), lambda n: (nc - 1 - n, 0)), zab, row, row, row],
        out_shape=[_sds((t, 3 * GDN_W), F32), _sds(dproj.shape, dproj.dtype),
                   _sds((1, HD), F32), _sds((1, HD), F32), _sds((1, HD), F32)],
        input_output_aliases={10: 1},
        scratch_shapes=[pltpu.VMEM((HEADS, HD, HD), F32)], compiler_params=_cp("arbitrary"),
    )(cqkv, cqkv, cqkv, proj, a_log, dt_bias, gain, sprev, inverses, dy, dproj)


MERGE_TN = 512


def _merge_specs(t, tm):
    tn = MERGE_TN
    ys = [pl.BlockSpec((tm, wd), lambda i, j: (i, 0)) for wd in (POOL_W, SB_W, GDN_W)]
    ws = [pl.BlockSpec((None, wd, tn), lambda i, j: (j, 0, 0)) for wd in (POOL_W, SB_W, GDN_W)]
    gs = [pl.BlockSpec((tm, tn), functools.partial(lambda i, j, b: (i, OFF_GATE // tn + b * (D // tn) + j), b=b))
          for b in range(3)]
    out = pl.BlockSpec((tm, tn), lambda i, j: (i, j))
    return ys, ws, gs, out


def _merge_fwd(ys, wups, proj):
    t = proj.shape[0]
    tm = min(512, t)
    y_specs, w_specs, g_specs, out = _merge_specs(t, tm)

    def body(y0, y1, y2, w0, w1, w2, g0, g1, g2, o_ref):
        acc = jnp.zeros(o_ref.shape, F32)
        for y, w, g in ((y0, w0, g0), (y1, w1, g1), (y2, w2, g2)):
            acc = acc + jax.nn.sigmoid(g[...]) * _dot(y[...], w[...])
        o_ref[...] = acc.astype(o_ref.dtype)

    return pl.pallas_call(
        body, name="merge_fwd", grid=(t // tm, D // MERGE_TN), in_specs=[*y_specs, *w_specs, *g_specs],
        out_specs=out, out_shape=_sds((t, D), BF16), compiler_params=_cp("parallel", "parallel"),
    )(*ys, *wups, proj, proj, proj)


def _merge_bwd(ys, wups, proj, dmerged):
    t = proj.shape[0]
    tm = min(1024, t)
    tn = MERGE_TN
    per = D // tn
    ys_specs = [pl.BlockSpec((tm, wd), lambda i, b, j: (i, 0)) for wd in (POOL_W, SB_W, GDN_W)]

    def w_spec(k, wd):
        return pl.BlockSpec((None, wd, tn), lambda i, b, j: (jnp.where(b == k, j, jnp.where(b < k, 0, per - 1)), 0, 0))

    w_specs = [w_spec(k, wd) for k, wd in enumerate((POOL_W, SB_W, GDN_W))]
    gate = pl.BlockSpec((tm, tn), lambda i, b, j: (i, OFF_GATE // tn + b * per + j))
    branch = pl.BlockSpec((tm, tn), lambda i, b, j: (i, b * per + j))
    merged = pl.BlockSpec((tm, tn), lambda i, b, j: (i, j))

    def body(y0, y1, y2, w0, w1, w2, g_ref, dm_ref, dg_ref, dmb_ref):
        b = pl.program_id(1)
        dm = dm_ref[...].astype(F32)
        sg = jax.nn.sigmoid(g_ref[...])
        dmb_ref[...] = (dm * sg).astype(dmb_ref.dtype)
        for k, (y, w) in enumerate(((y0, w0), (y1, w1), (y2, w2))):
            @pl.when(b == k)
            def _():
                dg_ref[...] = (dm * _dot(y[...], w[...]) * sg * (1.0 - sg)).astype(dg_ref.dtype)

    return pl.pallas_call(
        body, name="merge_bwd", grid=(t // tm, 3, per), in_specs=[*ys_specs, *w_specs, gate, merged],
        out_specs=[gate, branch], out_shape=[_sds((t, N_AL), BF16), _sds((t, 3 * D), BF16)],
        compiler_params=_cp("parallel", "arbitrary", "arbitrary"),
    )(*ys, *wups, proj, dmerged)


def _place(name, dproj, src, col):
    t, w = src.shape
    tt = min(512, t)

    def body(s_ref, _, o_ref):
        o_ref[...] = s_ref[...].astype(o_ref.dtype)

    return pl.pallas_call(
        body, name=name, grid=(t // tt,), in_specs=[pl.BlockSpec((tt, w), lambda i: (i, 0)), ANY],
        out_specs=pl.BlockSpec((tt, w), lambda i: (i, col // w)), out_shape=_sds(dproj.shape, dproj.dtype),
        input_output_aliases={1: 0}, compiler_params=_cp("parallel"),
    )(src, dproj)


def _tile(t, want):
    return min(t, want)


def _layer_fwd(x, l, gw, w_al, sp, reached=None):
    t = x.shape[0]
    tm = _tile(t, 1024)
    u = _rms_fwd("rms_attn", x, sp["attn_norm"][l])
    proj = _mm("proj", u, w_al, m=t, n=N_AL, k=D, tm=tm, tn=1024, tk=D, a_spec=_a_plain(tm, D),
               b_spec=_b_plain(D, 1024), dims=None, out_shapes=[_sds((t, N_AL), F32)], out_specs=[_o_plain(tm, 1024)])[0]
    if reached is not None:
        reached("proj", proj)
    y_pool = _pool_fwd(proj, sp["pool_w"][l], sp["pool_scale"][l])
    kv = _sb_cast_kv(proj)
    y_sb = _sb_fwd(proj, kv)
    cqkv = _conv_fwd(proj, sp["conv"][l])
    y_gdn, sprev, inverses = _gdn_fwd(cqkv, proj, sp["a_log"][l], sp["dt_bias"][l], sp["gdn_norm"][l])
    ys = (y_pool, y_sb, y_gdn)
    wups = (gw["w_pool_up"], gw["w_sb_up"], gw["w_gdn_up"])
    merged = _merge_fwd(ys, wups, proj)
    if reached is not None:
        reached("merged", merged)
    x1 = _mm("out_proj", merged, gw["w_out"], m=t, n=D, k=D, tm=tm, tn=1024, tk=512, a_spec=_a_plain(tm, 512),
             b_spec=_w_rows(512, 1024, 512), dims=None, out_shapes=[_sds((t, D), F32)], out_specs=[_o_plain(tm, 1024)],
             extras=[x], extra_specs=[_o_plain(tm, 1024)], epilogue=lambda r, xr: (r + xr,))[0]
    u2 = _rms_fwd("rms_mlp", x1, sp["mlp_norm"][l])

    def relu2(r):
        hv = jnp.maximum(r, 0.0)
        return hv, hv * hv

    hid, hid2 = _mm("ff1", u2, gw["w_ff1"], m=t, n=D_FF, k=D, tm=tm, tn=1024, tk=D, a_spec=_a_plain(tm, D),
                    b_spec=_w_cols(D, 1024, 2048), dims=None, out_shapes=[_sds((t, D_FF), BF16)] * 2,
                    out_specs=[_o_plain(tm, 1024)] * 2, epilogue=relu2)
    x2 = _mm("ff2", hid2, gw["w_ff2"], m=t, n=D, k=D_FF, tm=tm, tn=1024, tk=2048, a_spec=_a_plain(tm, 2048),
             b_spec=_w_rows(2048, 1024, 2048), dims=None, out_shapes=[_sds((t, D), F32)], out_specs=[_o_plain(tm, 1024)],
             extras=[x1], extra_specs=[_o_plain(tm, 1024)], epilogue=lambda r, xr: (r + xr,))[0]
    saved = dict(x=x, u=u, proj=proj, kv=kv, cqkv=cqkv, sprev=sprev, inverses=inverses, ys=ys, merged=merged, x1=x1, u2=u2, hid=hid, hid2=hid2)
    return x2, saved


def _layer_bwd(dx2, l, gw, w_al, sp, sv, emit=None):
    t = dx2.shape[0]
    tm = _tile(t, 1024)
    tk = t
    g = {}
    if emit is None:
        emit = lambda names, grads, v: v
    dpre = _mm("ff2_dx", dx2, gw["w_ff2"], m=t, n=D_FF, k=D, tm=tm, tn=1024, tk=D, a_spec=_a_plain(tm, D),
               b_spec=_w_rows_t(D, 1024, 2048), dims=NT, out_shapes=[_sds((t, D_FF), BF16)],
               out_specs=[_o_plain(tm, 1024)], extras=[sv["hid"]], extra_specs=[_o_plain(tm, 1024)],
               epilogue=lambda r, hv: (r * (2.0 * hv.astype(F32)),))[0]
    g["w_ff2"] = _mm("ff2_dw", sv["hid2"], dx2, m=D_FF, n=D, k=t, tm=1024, tn=1024, tk=tk, a_spec=_a_trans(1024, tk),
                     b_spec=_b_plain(tk, 1024), dims=TN, out_shapes=[_sds((D_FF, D), BF16)],
                     out_specs=[_o_plain(1024, 1024)])[0].reshape(N_CHIPS, D_FF // N_CHIPS, D)
    du2 = _mm("ff1_dx", dpre, gw["w_ff1"], m=t, n=D, k=D_FF, tm=tm, tn=1024, tk=2048, a_spec=_a_plain(tm, 2048),
              b_spec=_w_cols_t(2048, 1024, 2048), dims=NT, out_shapes=[_sds((t, D), F32)], out_specs=[_o_plain(tm, 1024)])[0]
    g["w_ff1"] = _mm("ff1_dw", sv["u2"], dpre, m=D, n=D_FF, k=t, tm=1024, tn=1024, tk=tk, a_spec=_a_trans(1024, tk),
                     b_spec=_b_plain(tk, 1024), dims=TN, out_shapes=[_sds((N_CHIPS, D, D_FF // N_CHIPS), BF16)],
                     out_specs=[_o_colshard(1024, 1024, D_FF // N_CHIPS)])[0]
    dx1, g["mlp_norm"] = _rms_bwd("rms_mlp_bwd", du2, sv["x1"], sp["mlp_norm"][l], dx2)
    dx1 = emit(("w_ff1", "w_ff2"), g, dx1)
    dmerged = _mm("out_dx", dx1, gw["w_out"], m=t, n=D, k=D, tm=tm, tn=512, tk=D, a_spec=_a_plain(tm, D),
                  b_spec=_w_rows_t(D, 512, 512), dims=NT, out_shapes=[_sds((t, D), BF16)], out_specs=[_o_plain(tm, 512)])[0]
    g["w_out"] = _mm("out_dw", sv["merged"], dx1, m=D, n=D, k=t, tm=1024, tn=1024, tk=tk, a_spec=_a_trans(1024, tk),
                     b_spec=_b_plain(tk, 1024), dims=TN, out_shapes=[_sds((D, D), BF16)],
                     out_specs=[_o_plain(1024, 1024)])[0].reshape(N_CHIPS, D // N_CHIPS, D)
    wups = (gw["w_pool_up"], gw["w_sb_up"], gw["w_gdn_up"])
    dproj, dm_all = _merge_bwd(sv["ys"], wups, sv["proj"], dmerged)
    per = D // MERGE_TN
    dys = []
    for b, (nm, yb, wd) in enumerate(zip(("w_pool_up", "w_sb_up", "w_gdn_up"), sv["ys"], (POOL_W, SB_W, GDN_W))):
        dm_rows = pl.BlockSpec((tm, MERGE_TN), functools.partial(lambda i, j, kk, b: (i, b * per + kk), b=b))
        dm_cols = pl.BlockSpec((tk, MERGE_TN), functools.partial(lambda i, j, kk, b: (kk, b * per + j), b=b))
        dys.append(_mm(nm + "_dx", dm_all, gw[nm], m=t, n=wd, k=D, tm=tm, tn=256, tk=MERGE_TN, a_spec=dm_rows,
                       b_spec=_w_cols_t(MERGE_TN, 256, 512), dims=NT, out_shapes=[_sds((t, wd), F32)],
                       out_specs=[_o_plain(tm, 256)])[0])
        g[nm] = _mm(nm + "_dw", yb, dm_all, m=wd, n=D, k=t, tm=256, tn=MERGE_TN, tk=tk, a_spec=_a_trans(256, tk),
                    b_spec=dm_cols, dims=TN, out_shapes=[_sds((N_CHIPS, wd, D // N_CHIPS), BF16)],
                    out_specs=[_o_colshard(256, MERGE_TN, D // N_CHIPS)])[0]
    dys[2] = emit(("w_pool_up", "w_sb_up", "w_gdn_up", "w_out"), g, dys[2])
    proj = sv["proj"]
    dproj, g["pool_w"], g["pool_scale"] = _pool_bwd(proj, sp["pool_w"][l], sp["pool_scale"][l], dys[0], dproj)
    for k, piece in enumerate(_sb_bwd(proj, sv["kv"], dys[1])):
        dproj = _place(f"place_sb_{k}", dproj, piece, OFF_SB + k * SB_W)
    dc, dproj, g["a_log"], g["dt_bias"], g["gdn_norm"] = _gdn_bwd(
        sv["cqkv"], proj, sp["a_log"][l], sp["dt_bias"][l], sp["gdn_norm"][l], sv["sprev"], sv["inverses"], dys[2], dproj)
    dproj, g["conv"] = _conv_bwd(proj, sp["conv"][l], dc, dproj)
    du = _mm("proj_dx", dproj, w_al, m=t, n=D, k=N_AL, tm=tm, tn=1024, tk=2048, a_spec=_a_plain(tm, 2048),
             b_spec=_b_trans(2048, 1024), dims=NT, out_shapes=[_sds((t, D), F32)], out_specs=[_o_plain(tm, 1024)])[0]
    g["w_al"] = _mm("proj_dw", sv["u"], dproj, m=D, n=N_AL, k=t, tm=1024, tn=1024, tk=tk, a_spec=_a_trans(1024, tk),
                    b_spec=_b_plain(tk, 1024), dims=TN, out_shapes=[_sds((D, N_AL), BF16)], out_specs=[_o_plain(1024, 1024)])[0]
    dx, g["attn_norm"] = _rms_bwd("rms_attn_bwd", du, sv["x"], sp["attn_norm"][l], dx1)
    g["w_in"] = _w_in_to_shards(g["w_al"])
    dx = emit(("w_in",), g, dx)
    return dx, g


W_IN_RUNS = ((0, ORIG_SB, OFF_P), (ORIG_SB, ORIG_Z, OFF_SB), (ORIG_Z, ORIG_GATE, OFF_Z), (ORIG_GATE, N_IN, OFF_GATE))
W_IN_SHARD = N_IN // N_CHIPS


def _w_in_from_shards(gathered):
    parts = []
    for lo, hi, al in sorted(W_IN_RUNS, key=lambda r: r[2]):
        if al == OFF_GATE:
            parts.append(jnp.zeros((D, OFF_GATE - (OFF_AB + ORIG_GATE - ORIG_AB)), gathered.dtype))
        while lo < hi:
            chip = lo // W_IN_SHARD
            end = min(hi, (chip + 1) * W_IN_SHARD)
            parts.append(gathered[chip, :, lo - chip * W_IN_SHARD:end - chip * W_IN_SHARD])
            lo = end
    return jnp.concatenate(parts, axis=1)


def _w_in_to_shards(g_al):
    shards = []
    for chip in range(N_CHIPS):
        a, b = chip * W_IN_SHARD, (chip + 1) * W_IN_SHARD
        parts = [g_al[:, al + max(a, lo) - lo:al + min(b, hi) - lo] for lo, hi, al in W_IN_RUNS if max(a, lo) < min(b, hi)]
        shards.append(jnp.concatenate(parts, axis=1))
    return jnp.stack(shards)


def _row128(v):
    return jnp.pad(v.reshape(1, -1), ((0, 0), (0, HD - v.shape[-1])))


def _local_step(x, target, weights_of, sp, emit=None, reached=None):
    saved, gw, w_in_al = [], [], []
    h = x
    for l in range(2):
        gw_l, w_al_l = weights_of(l)
        gw.append(gw_l)
        w_in_al.append(w_al_l)
        h, sv = _layer_fwd(h, l, gw_l, w_al_l, sp, None if reached is None else functools.partial(reached, l))
        saved.append(sv)
    loss, dh, g_final = _loss_head(h, sp["final_norm"], target)
    grads = [None, None]
    for l in (1, 0):
        dh, grads[l] = _layer_bwd(dh, l, gw[l], w_in_al[l], sp, saved[l],
                                  None if emit is None else functools.partial(emit, l))
    return loss, dh, grads, g_final


ANY = pl.BlockSpec(memory_space=pl.ANY)


def _me():
    return lax.axis_index("x"), lax.axis_index("y"), lax.axis_index("c")


def _other_chips(x, y):
    return [(1 - x, y), (x, 1 - y), (1 - x, 1 - y)]


def _half(ref, axis, c, rows):
    half = rows // 2
    idx = [slice(None)] * axis + [pl.ds(pl.multiple_of(c * half, 16), half)]
    return ref.at[tuple(idx)]


def _gather_steps(out, send, recv):
    n = len(out)
    x, y, c = _me()
    mine = 2 * x + y
    sibling = (x, y, 1 - c)
    chips = _other_chips(x, y)
    sends = []
    for t in range(n):
        rows = out[t].shape[1]
        for k, (px, py) in enumerate(chips):
            own_half = _half(out[t].at[mine], 0, c, rows)
            cp = pltpu.make_async_remote_copy(
                src_ref=own_half, dst_ref=own_half,
                send_sem=send.at[6 * t + k], recv_sem=recv.at[6 * t + k], device_id=(px, py, c), device_id_type=MESH)
            cp.start()
            sends.append(cp)
    for t in range(n):
        rows = out[t].shape[1]
        for k, (px, py) in enumerate(chips):
            landed = _half(out[t].at[2 * px + py], 0, c, rows)
            pltpu.make_async_remote_copy(
                src_ref=landed, dst_ref=landed, send_sem=send.at[6 * t + k], recv_sem=recv.at[6 * t + k],
                device_id=(px, py, c), device_id_type=MESH).wait_recv()
            cp = pltpu.make_async_remote_copy(
                src_ref=landed, dst_ref=landed, send_sem=send.at[6 * t + 3 + k], recv_sem=recv.at[6 * t + 3 + k],
                device_id=sibling, device_id_type=MESH)
            cp.start()
            sends.append(cp)
    for t in range(n):
        rows = out[t].shape[1]
        for k, (px, py) in enumerate(chips):
            other = _half(out[t].at[2 * px + py], 0, 1 - c, rows)
            pltpu.make_async_remote_copy(
                src_ref=other, dst_ref=other, send_sem=send.at[6 * t + 3 + k], recv_sem=recv.at[6 * t + 3 + k],
                device_id=sibling, device_id_type=MESH).wait_recv()
    for cp in sends:
        cp.wait_send()


def _gather_weights_async(bufs, tag, collective_id):
    n = len(bufs)
    refs = [jax.new_ref(b, memory_space=pltpu.MemorySpace.HBM) for b in bufs]

    @pl.kernel(mesh=plsc.ScalarSubcoreMesh(axis_name="sequencer", num_cores=1), name=f"gather_async_{tag}",
               scratch_types=(pltpu.SemaphoreType.DMA((6 * n,)), pltpu.SemaphoreType.DMA((6 * n,))),
               compiler_params=pltpu.CompilerParams(collective_id=collective_id))
    def launch(send, recv):
        x, y, c = _me()
        barrier = pltpu.get_barrier_semaphore()
        peers = [(x, y, 1 - c)] + [(px, py, c) for px, py in _other_chips(x, y)]
        for peer in peers:
            pl.semaphore_signal(barrier, inc=1, device_id=peer, device_id_type=MESH)
        pl.semaphore_wait(barrier, len(peers))
        _gather_steps(refs, send, recv)

    launch()
    return [r[...] for r in refs]


def _rs_pair(grads):
    n = len(grads)

    def body(*refs):
        g, out = refs[:n], refs[n:2 * n]
        send, recv = refs[2 * n:]
        x, y, c = _me()
        copies = []
        for t in range(n):
            cp = pltpu.make_async_remote_copy(
                src_ref=_half(g[t], 1, 1 - c, g[t].shape[1]), dst_ref=out[t], send_sem=send.at[t], recv_sem=recv.at[t],
                device_id=(x, y, 1 - c), device_id_type=MESH)
            cp.start()
            copies.append(cp)
        for cp in copies:
            cp.wait()

    return pl.pallas_call(
        body, name="rs_pair", in_specs=[ANY] * n, out_specs=[ANY] * n,
        out_shape=[_sds((N_CHIPS, s.shape[1] // 2, s.shape[2]), s.dtype) for s in grads],
        scratch_shapes=[pltpu.SemaphoreType.DMA((n,)), pltpu.SemaphoreType.DMA((n,))],
    )(*grads)


def _rs_chips_steps(p, out, send, recv):
    x, y, c = _me()
    copies = []
    for t in range(len(p)):
        for k, (px, py) in enumerate(_other_chips(x, y)):
            cp = pltpu.make_async_remote_copy(
                src_ref=p[t].at[2 * px + py], dst_ref=out[t].at[k], send_sem=send.at[3 * t + k],
                recv_sem=recv.at[3 * t + k], device_id=(px, py, c), device_id_type=MESH)
            cp.start()
            copies.append(cp)
    for cp in copies:
        cp.wait()


def _rs_chips_async(parts, tag, collective_id):
    n = len(parts)
    src = [jax.new_ref(p, memory_space=pltpu.MemorySpace.HBM) for p in parts]
    got = [jax.empty_ref(_sds((3, *p.shape[1:]), p.dtype), memory_space=pltpu.MemorySpace.HBM) for p in parts]

    @pl.kernel(mesh=plsc.ScalarSubcoreMesh(axis_name="sequencer", num_cores=1), name=f"rs_chips_async_{tag}",
               scratch_types=(pltpu.SemaphoreType.DMA((3 * n,)), pltpu.SemaphoreType.DMA((3 * n,))),
               compiler_params=pltpu.CompilerParams(collective_id=collective_id))
    def launch(send, recv):
        x, y, c = _me()
        barrier = pltpu.get_barrier_semaphore()
        peers = [(px, py, c) for px, py in _other_chips(x, y)]
        for peer in peers:
            pl.semaphore_signal(barrier, inc=1, device_id=peer, device_id_type=MESH)
        pl.semaphore_wait(barrier, len(peers))
        _rs_chips_steps(src, got, send, recv)

    launch()
    return [g[...] for g in got]


def _pair_exchange_async(bufs, tag, collective_id):
    n = len(bufs)
    refs = [jax.new_ref(b, memory_space=pltpu.MemorySpace.HBM) for b in bufs]

    @pl.kernel(mesh=plsc.ScalarSubcoreMesh(axis_name="sequencer", num_cores=1), name=f"pair_exchange_async_{tag}",
               scratch_types=(pltpu.SemaphoreType.DMA((n,)), pltpu.SemaphoreType.DMA((n,))),
               compiler_params=pltpu.CompilerParams(collective_id=collective_id))
    def launch(send, recv):
        x, y, c = _me()
        sibling = (x, y, 1 - c)
        barrier = pltpu.get_barrier_semaphore()
        pl.semaphore_signal(barrier, inc=1, device_id=sibling, device_id_type=MESH)
        pl.semaphore_wait(barrier, 1)
        copies = []
        for t in range(n):
            cp = pltpu.make_async_remote_copy(
                src_ref=refs[t].at[c], dst_ref=refs[t].at[c], send_sem=send.at[t], recv_sem=recv.at[t],
                device_id=sibling, device_id_type=MESH)
            cp.start()
            copies.append(cp)
        for t, cp in enumerate(copies):
            cp.wait_send()
            pltpu.make_async_remote_copy(
                src_ref=refs[t].at[1 - c], dst_ref=refs[t].at[1 - c], send_sem=send.at[t], recv_sem=recv.at[t],
                device_id=sibling, device_id_type=MESH).wait_recv()

    launch()
    return [r[...] for r in refs]


def _row_tile(rows, cols, itemsize, budget=2 * 1024 * 1024):
    tr = rows
    while tr * cols * itemsize > budget and tr % 32 == 0:
        tr //= 2
    return tr


def _sum_pair(name, g, got, where):
    nchip, rows, cols = g.shape
    half = rows // 2
    tr = _row_tile(half, cols, 4)
    per = half // tr

    def body(w_ref, g_ref, r_ref, o_ref):
        o_ref[...] = (g_ref[...].astype(F32) + r_ref[...].astype(F32)).astype(o_ref.dtype)

    blk = pl.BlockSpec((None, tr, cols), lambda j, i, w_ref: (j, i, 0))
    return pl.pallas_call(
        body, name=name,
        grid_spec=pltpu.PrefetchScalarGridSpec(
            num_scalar_prefetch=1, grid=(nchip, per),
            in_specs=[pl.BlockSpec((None, tr, cols), lambda j, i, w_ref: (j, w_ref[1] * per + i, 0)), blk], out_specs=blk),
        out_shape=_sds((nchip, half, cols), BF16), compiler_params=_cp("parallel", "parallel"),
    )(where, g, got)


def _sum_chips(name, p, got, where):
    _, rows, cols = p.shape
    tr = _row_tile(rows, cols, 4)

    def body(w_ref, p_ref, r0, r1, r2, o_ref):
        o_ref[...] = ((p_ref[...].astype(F32) + r0[...].astype(F32)) + r1[...].astype(F32)) + r2[...].astype(F32)

    def got_k(k):
        return pl.BlockSpec((None, tr, cols), lambda i, w_ref: (k, i, 0))

    return pl.pallas_call(
        body, name=name,
        grid_spec=pltpu.PrefetchScalarGridSpec(
            num_scalar_prefetch=1, grid=(rows // tr,),
            in_specs=[pl.BlockSpec((None, tr, cols), lambda i, w_ref: (w_ref[0], i, 0)), got_k(0), got_k(1), got_k(2)],
            out_specs=pl.BlockSpec((None, tr, cols), lambda i, w_ref: (w_ref[1], i, 0))),
        out_shape=_sds((2, rows, cols), F32), compiler_params=_cp("parallel"),
    )(where, p, got, got, got)


def _rs_begin(grads, where, tag, collective_id):
    got = _rs_pair(grads)
    parts = [_sum_pair(f"sum_pair_{t}", g, r, where) for t, (g, r) in enumerate(zip(grads, got))]
    return parts, _rs_chips_async(parts, tag, collective_id)


def _rs_finish(parts, got, where, tag, collective_id):
    halves = [_sum_chips(f"sum_chips_{t}", p, r, where) for t, (p, r) in enumerate(zip(parts, got))]
    return _pair_exchange_async(halves, tag, collective_id)


def _all_reduce_small(name, v):
    rows = v.shape[0]

    def body(v_ref, o_ref, land, send, recv):
        x, y, c = _me()
        mine = 4 * x + 2 * y + c
        copies = []
        for k in range(1, 8):
            kx, ky, kc = k >> 2, (k >> 1) & 1, k & 1
            peer = (x ^ kx, y ^ ky, c ^ kc)
            cp = pltpu.make_async_remote_copy(
                src_ref=v_ref, dst_ref=land.at[mine], send_sem=send.at[k - 1], recv_sem=recv.at[k - 1],
                device_id=peer, device_id_type=MESH)
            cp.start()
            copies.append(cp)
        land[mine] = v_ref[...]
        for k in range(1, 8):
            kx, ky, kc = k >> 2, (k >> 1) & 1, k & 1
            src = 4 * (x ^ kx) + 2 * (y ^ ky) + (c ^ kc)
            pltpu.make_async_remote_copy(
                src_ref=v_ref, dst_ref=land.at[src], send_sem=send.at[k - 1], recv_sem=recv.at[k - 1],
                device_id=(x ^ kx, y ^ ky, c ^ kc), device_id_type=MESH).wait_recv()
        acc = land[0]
        for d in range(1, 8):
            acc = acc + land[d]
        o_ref[...] = acc
        for cp in copies:
            cp.wait_send()

    vm = pl.BlockSpec(memory_space=pltpu.VMEM)
    return pl.pallas_call(
        body, name=name, in_specs=[vm], out_specs=vm, out_shape=_sds((rows, 128), F32),
        scratch_shapes=[pltpu.VMEM((8, rows, 128), F32), pltpu.SemaphoreType.DMA((7,)), pltpu.SemaphoreType.DMA((7,))],
    )(v)


def _adamw(name, w, g, m, v):
    rows, cols = w.shape
    tr = _row_tile(rows, cols, 4, budget=1024 * 1024)
    c1 = 1.0 / (1.0 - ADAM_B1 ** ADAM_STEP)
    c2 = 1.0 / (1.0 - ADAM_B2 ** ADAM_STEP)

    def body(w_ref, g_ref, m_ref, v_ref, d_ref, nm_ref, nv_ref):
        gv = g_ref[...]
        nm = ADAM_B1 * m_ref[...] + (1.0 - ADAM_B1) * gv
        nv = ADAM_B2 * v_ref[...] + (1.0 - ADAM_B2) * (gv * gv)
        d_ref[...] = -ADAM_LR * ((nm * c1) / (jnp.sqrt(nv * c2) + ADAM_EPS) + ADAM_WD * w_ref[...])
        nm_ref[...] = nm
        nv_ref[...] = nv

    blk = pl.BlockSpec((tr, cols), lambda i: (i, 0))
    return pl.pallas_call(
        body, name=name, grid=(rows // tr,), in_specs=[blk] * 4, out_specs=[blk] * 3,
        out_shape=[_sds((rows, cols), F32)] * 3, compiler_params=_cp("parallel"),
    )(w, g, m, v)


def _adamw_layers(name, w, g0, g1, m, v):
    _, half, cols = g0.shape
    tr = _row_tile(half, cols, 4, budget=1024 * 1024)
    per_half = half // tr
    per = 2 * per_half
    c1 = 1.0 / (1.0 - ADAM_B1 ** ADAM_STEP)
    c2 = 1.0 / (1.0 - ADAM_B2 ** ADAM_STEP)

    def body(w_ref, g0_ref, g1_ref, m_ref, v_ref, g_ref, d_ref, nm_ref, nv_ref):
        gv = jnp.where(pl.program_id(0) == 0, g0_ref[...], g1_ref[...])
        nm = ADAM_B1 * m_ref[...] + (1.0 - ADAM_B1) * gv
        nv = ADAM_B2 * v_ref[...] + (1.0 - ADAM_B2) * (gv * gv)
        g_ref[...] = gv
        d_ref[...] = -ADAM_LR * ((nm * c1) / (jnp.sqrt(nv * c2) + ADAM_EPS) + ADAM_WD * w_ref[...])
        nm_ref[...] = nm
        nv_ref[...] = nv

    both = pl.BlockSpec((None, tr, cols), lambda l, i: (l, i, 0))

    def halves(i):
        return i // per_half, i % per_half, 0

    first = pl.BlockSpec((None, tr, cols), lambda l, i: halves(i * (1 - l) + (per - 1) * l))
    second = pl.BlockSpec((None, tr, cols), lambda l, i: halves(i * l))
    return pl.pallas_call(
        body, name=name, grid=(2, per), in_specs=[both, first, second, both, both], out_specs=[both] * 4,
        out_shape=[_sds(w.shape, F32)] * 4, compiler_params=_cp("arbitrary", "arbitrary"),
    )(w, g0, g1, m, v)


def _to_bf16_slot(name, w, l, where):
    _, rows, cols = w.shape
    tr = _row_tile(rows, cols, 4)

    def body(w_ref, x_ref, o_ref):
        o_ref[...] = x_ref[...].astype(BF16)

    return pl.pallas_call(
        body, name=name,
        grid_spec=pltpu.PrefetchScalarGridSpec(
            num_scalar_prefetch=1, grid=(rows // tr,), in_specs=[pl.BlockSpec((None, tr, cols), lambda i, w_ref: (l, i, 0))],
            out_specs=pl.BlockSpec((None, tr, cols), lambda i, w_ref: (w_ref[0], i, 0))),
        out_shape=_sds((N_CHIPS, rows, cols), BF16), compiler_params=_cp("parallel"))(where, w)


BIG = ("w_in", "w_pool_up", "w_sb_up", "w_gdn_up", "w_out", "w_ff1", "w_ff2")
SMALL = (("attn_norm", (D,)), ("pool_w", (4, 128, 128)), ("pool_scale", (POOL_W,)), ("gdn_a_log", (HEADS,)),
         ("gdn_dt_bias", (HEADS,)), ("gdn_norm", (HD,)), ("mlp_norm", (D,)))


PACK_TILE = 8 * 128


def _rows128(a):
    flat = a.reshape(-1)
    pad = (-flat.shape[0]) % PACK_TILE
    return jnp.pad(flat, (0, pad)).reshape(-1, 128)


def _pack(parts):
    packed = jnp.concatenate([_rows128(p) for p in parts], axis=0)
    return jnp.pad(packed, ((0, (-packed.shape[0]) % 8), (0, 0)))


def _unpack(packed, shapes):
    out, r = [], 0
    for shp in shapes:
        size = 1
        for s in shp:
            size *= s
        nr = -(-size // PACK_TILE) * 8
        out.append(packed[r:r + nr].reshape(-1)[:size].reshape(shp))
        r += nr
    return out


def kernel(x, attn_norm, w_in, pool_w, pool_scale, gdn_conv, gdn_a_log, gdn_dt_bias, gdn_norm, w_pool_up, w_sb_up, w_gdn_up, w_out, mlp_norm, w_ff1, w_ff2, final_norm, loss_target, m_attn_norm, m_w_in, m_pool_w, m_pool_scale, m_gdn_conv, m_gdn_a_log, m_gdn_dt_bias, m_gdn_norm, m_w_pool_up, m_w_sb_up, m_w_gdn_up, m_w_out, m_mlp_norm, m_w_ff1, m_w_ff2, m_final_norm, v_attn_norm, v_w_in, v_pool_w, v_pool_scale, v_gdn_conv, v_gdn_a_log, v_gdn_dt_bias, v_gdn_norm, v_w_pool_up, v_w_sb_up, v_w_gdn_up, v_w_out, v_mlp_norm, v_w_ff1, v_w_ff2, v_final_norm):
    weights = dict(attn_norm=attn_norm, w_in=w_in, pool_w=pool_w, pool_scale=pool_scale, gdn_conv=gdn_conv,
                   gdn_a_log=gdn_a_log, gdn_dt_bias=gdn_dt_bias, gdn_norm=gdn_norm, w_pool_up=w_pool_up, w_sb_up=w_sb_up,
                   w_gdn_up=w_gdn_up, w_out=w_out, mlp_norm=mlp_norm, w_ff1=w_ff1, w_ff2=w_ff2, final_norm=final_norm)
    mom1 = dict(attn_norm=m_attn_norm, w_in=m_w_in, pool_w=m_pool_w, pool_scale=m_pool_scale, gdn_conv=m_gdn_conv,
                gdn_a_log=m_gdn_a_log, gdn_dt_bias=m_gdn_dt_bias, gdn_norm=m_gdn_norm, w_pool_up=m_w_pool_up,
                w_sb_up=m_w_sb_up, w_gdn_up=m_w_gdn_up, w_out=m_w_out, mlp_norm=m_mlp_norm, w_ff1=m_w_ff1, w_ff2=m_w_ff2,
                final_norm=m_final_norm)
    mom2 = dict(attn_norm=v_attn_norm, w_in=v_w_in, pool_w=v_pool_w, pool_scale=v_pool_scale, gdn_conv=v_gdn_conv,
                gdn_a_log=v_gdn_a_log, gdn_dt_bias=v_gdn_dt_bias, gdn_norm=v_gdn_norm, w_pool_up=v_w_pool_up,
                w_sb_up=v_w_sb_up, w_gdn_up=v_w_gdn_up, w_out=v_w_out, mlp_norm=v_mlp_norm, w_ff1=v_w_ff1, w_ff2=v_w_ff2,
                final_norm=v_final_norm)
    xi, yi, ci = lax.axis_index("x"), lax.axis_index("y"), lax.axis_index("c")
    chip = 2 * xi + yi
    where = jnp.stack([chip, ci]).astype(jnp.int32)

    bufs = [[_to_bf16_slot(f"cast_{nm}_{l}", weights[nm], l, where) for nm in BIG] for l in range(2)]
    first = _gather_weights_async(bufs[0][:1], "0_w_in", 4)
    first, _ = lax.optimization_barrier((first, (bufs[0][1:], bufs[1])))
    rest, _ = lax.optimization_barrier((bufs[0][1:], first))
    gw = [dict(zip(BIG, list(first) + _gather_weights_async(rest, "0_rest", 1))), {}]

    def reached(l, stage, value):
        if l == 0 and stage == "proj":
            later, _ = lax.optimization_barrier((bufs[1][:1], value))
            gw[1]["w_in"] = _gather_weights_async(later, "1_w_in", 2)[0]
        if l == 0 and stage == "merged":
            later, _ = lax.optimization_barrier((bufs[1][1:], value))
            gw[1].update(zip(BIG[1:], _gather_weights_async(later, "1_rest", 3)))

    def weights_of(l):
        return gw[l], _w_in_from_shards(gw[l]["w_in"])

    conv_cols = gdn_conv.shape[-1]
    conv_place = lax.dynamic_update_slice(jnp.zeros((2, GDN_CONV, N_CHIPS * conv_cols), F32),
                                          jnp.where(ci == 0, gdn_conv, 0.0), (0, 0, chip * conv_cols))
    conv_full = _all_reduce_small("gather_conv", _rows128(conv_place)).reshape(2, GDN_CONV, N_CHIPS * conv_cols)
    sp = dict(attn_norm=attn_norm.reshape(2, 1, D), pool_w=pool_w, pool_scale=pool_scale.reshape(2, 1, POOL_W),
              conv=conv_full, a_log=jnp.stack([_row128(gdn_a_log[l]) for l in range(2)]),
              dt_bias=jnp.stack([_row128(gdn_dt_bias[l]) for l in range(2)]), gdn_norm=gdn_norm.reshape(2, 1, HD),
              mlp_norm=mlp_norm.reshape(2, 1, D), final_norm=final_norm.reshape(1, D))

    started = []

    def emit(l, names, g, v):
        parts, got = _rs_begin([g[nm] for nm in names], where, f"{l}_{names[0]}", 5 + len(started))
        started.append((l, names, parts, got))
        v, _ = lax.optimization_barrier((v, parts))
        return v

    loss, grad_x, grads, g_final = _local_step(x[0], loss_target[0], weights_of, sp, emit, reached)
    big_grads = {nm: [None, None] for nm in BIG}
    for k, (l, names, parts, got) in enumerate(started[:-1]):
        got, _ = lax.optimization_barrier((got, grad_x))
        for nm, red in zip(names, _rs_finish(parts, got, where, f"{l}_{names[0]}", 11 + k)):
            big_grads[nm][l] = red
    small_parts, small_shapes = [], []
    for l in range(2):
        g = grads[l]
        for nm, shp in SMALL:
            key = {"gdn_a_log": "a_log", "gdn_dt_bias": "dt_bias"}.get(nm, nm)
            val = g[key]
            small_parts.append(val[0, :HEADS] if nm in ("gdn_a_log", "gdn_dt_bias") else val)
            small_shapes.append(shp)
        small_parts.append(g["conv"])
        small_shapes.append((GDN_CONV, N_CHIPS * conv_cols))
    small_parts += [g_final, loss]
    small_shapes += [(D,), (1, 1)]
    small_pack = _pack(small_parts)

    grad, delta, new_m, new_v = {}, {}, {}, {}
    for nm in BIG[1:]:
        grad[nm], delta[nm], new_m[nm], new_v[nm] = _adamw_layers("adamw_" + nm, weights[nm], *big_grads[nm], mom1[nm], mom2[nm])
    l, names, parts, got = started[-1]
    got, _ = lax.optimization_barrier((got, [new_v[nm] for nm in BIG[1:]]))
    for nm, red in zip(names, _rs_finish(parts, got, where, f"{l}_{names[0]}", 10 + len(started))):
        big_grads[nm][l] = red
    nm = BIG[0]
    grad[nm], delta[nm], new_m[nm], new_v[nm] = _adamw_layers("adamw_" + nm, weights[nm], *big_grads[nm], mom1[nm], mom2[nm])
    small_pack, _ = lax.optimization_barrier((small_pack, new_v[nm]))
    reduced = _unpack(_all_reduce_small("reduce_small", small_pack), small_shapes)
    per = len(SMALL) + 1
    for i, (nm, _) in enumerate(SMALL):
        grad[nm] = jnp.stack([reduced[i], reduced[per + i]])
    conv_g = jnp.stack([reduced[per - 1], reduced[2 * per - 1]])
    grad["gdn_conv"] = lax.dynamic_slice(conv_g, (0, 0, chip * conv_cols), (2, GDN_CONV, conv_cols))
    grad["final_norm"] = reduced[-2]
    loss = reduced[-1][0, 0]
    small_names = [nm for nm, _ in SMALL] + ["gdn_conv", "final_norm"]
    packs = [_pack([src[nm] for nm in small_names]) for src in (weights, grad, mom1, mom2)]
    outs = _adamw("adamw_small", *packs)
    shapes = [weights[nm].shape for nm in small_names]
    for dst, packed in zip((delta, new_m, new_v), outs):
        for nm, val in zip(small_names, _unpack(packed, shapes)):
            dst[nm] = val

    order = ("attn_norm", "w_in", "pool_w", "pool_scale", "gdn_conv", "gdn_a_log", "gdn_dt_bias", "gdn_norm", "w_pool_up",
             "w_sb_up", "w_gdn_up", "w_out", "mlp_norm", "w_ff1", "w_ff2", "final_norm")
    return (loss, grad_x[None], *[grad[n] for n in order], *[delta[n] for n in order], *[new_m[n] for n in order],
            *[new_v[n] for n in order])
```

```python
import functools

import jax
import jax.numpy as jnp
from jax import lax
from jax.experimental import pallas as pl
from jax.experimental.pallas import tpu as pltpu
from jax.experimental.pallas import tpu_sc as plsc

F32, BF16 = jnp.float32, jnp.bfloat16
HIGH = lax.Precision.HIGH
MESH = pl.DeviceIdType.MESH

D = 2048
EPS = 1e-6
POOL_WINDOWS = (2, 4, 8, 16)
POOL_W, SB_W, GDN_W = 512, 768, 768
HEADS, HD = 6, 128
SB_BLOCK = 128
GDN_CHUNK = 64
D_FF = 4 * D
N_IN = 12044
N_CHIPS = 4
OFF_SB, OFF_GQKV, OFF_P, OFF_Z, OFF_AB, OFF_GATE = 0, 2304, 4608, 5120, 5888, 6144
AB_W = 256
ZAB_W = GDN_W + AB_W
ORIG_SB, ORIG_Z, ORIG_AB, ORIG_GATE = 512, 5120, 5888, 5900
N_AL = 12288
VMEM_LIMIT = 48 * 1024 * 1024

ADAM_LR, ADAM_B1, ADAM_B2, ADAM_EPS, ADAM_WD, ADAM_STEP = 0.001, 0.9, 0.999, 1e-08, 0.01, 10

NT = (((1,), (1,)), ((), ()))
TN = (((0,), (0,)), ((), ()))


def _cp(*sem):
    return pltpu.CompilerParams(dimension_semantics=sem, vmem_limit_bytes=VMEM_LIMIT)


def _dot(a, b, dims=None, precision=None):
    if dims is None:
        dims = (((a.ndim - 1,), (0,)), ((), ()))
    return lax.dot_general(a, b, dims, precision=precision, preferred_element_type=F32)


def _hdot(a, b, dims=None):
    return _dot(a, b, dims, precision=HIGH)


def _bdot(a, b, dims=None):
    return _dot(a.astype(BF16), b.astype(BF16), dims)


def _mm(name, a, b, *, m, n, k, tm, tn, tk, a_spec, b_spec, dims, out_shapes, out_specs,
        extras=(), extra_specs=(), epilogue=None):
    nk = k // tk
    ne, no = len(extras), len(out_shapes)

    def body(*refs):
        a_ref, b_ref = refs[0], refs[1]
        ex = refs[2:2 + ne]
        outs = refs[2 + ne:2 + ne + no]
        kk = pl.program_id(2)

        def finish(r):
            res = epilogue(r, *[e[...] for e in ex]) if epilogue is not None else (r,)
            for o, v in zip(outs, res):
                o[...] = v.astype(o.dtype)

        part = _dot(a_ref[...].astype(BF16), b_ref[...].astype(BF16), dims)
        if nk == 1:
            finish(part)
            return
        acc = refs[-1]

        @pl.when(kk == 0)
        def _():
            acc[...] = part

        @pl.when((kk > 0) & (kk < nk - 1))
        def _():
            acc[...] += part

        @pl.when(kk == nk - 1)
        def _():
            finish(acc[...] + part)

    return pl.pallas_call(
        body, name=name, grid=(m // tm, n // tn, nk),
        in_specs=[a_spec, b_spec, *extra_specs], out_specs=out_specs, out_shape=out_shapes,
        scratch_shapes=[] if nk == 1 else [pltpu.VMEM((tm, tn), F32)],
        compiler_params=_cp("parallel", "parallel", "arbitrary"),
    )(a, b, *extras)


def _a_plain(tm, tk):
    return pl.BlockSpec((tm, tk), lambda i, j, kk: (i, kk))


def _a_trans(tm, tk):
    return pl.BlockSpec((tk, tm), lambda i, j, kk: (kk, i))


def _b_plain(tk, tn):
    return pl.BlockSpec((tk, tn), lambda i, j, kk: (kk, j))


def _b_trans(tk, tn):
    return pl.BlockSpec((tn, tk), lambda i, j, kk: (j, kk))


def _o_plain(tm, tn):
    return pl.BlockSpec((tm, tn), lambda i, j, kk: (i, j))


def _o_colshard(tm, tn, ns_cols):
    per = ns_cols // tn
    return pl.BlockSpec((None, tm, tn), lambda i, j, kk: (j // per, i, j % per))


def _w_cols(tk, tn, ns):
    per = ns // tn
    return pl.BlockSpec((None, tk, tn), lambda i, j, kk: (j // per, kk, j % per))


def _w_cols_t(tk, tn, ns):
    per = ns // tk
    return pl.BlockSpec((None, tn, tk), lambda i, j, kk: (kk // per, j, kk % per))


def _w_rows(tk, tn, ks):
    per = ks // tk
    return pl.BlockSpec((None, tk, tn), lambda i, j, kk: (kk // per, kk % per, j))


def _w_rows_t(tk, tn, ks):
    per = ks // tn
    return pl.BlockSpec((None, tn, tk), lambda i, j, kk: (j // per, j % per, kk))


def _sds(shape, dtype):
    return jax.ShapeDtypeStruct(shape, dtype)


def _rms_fwd(name, x, gain):
    t = x.shape[0]
    tt = min(256, t)

    def body(x_ref, g_ref, u_ref):
        xv = x_ref[...]
        r = lax.rsqrt(jnp.mean(xv * xv, axis=-1, keepdims=True) + EPS)
        u_ref[...] = (xv * r * g_ref[...]).astype(u_ref.dtype)

    return pl.pallas_call(
        body, name=name, grid=(t // tt,),
        in_specs=[pl.BlockSpec((tt, D), lambda i: (i, 0)), pl.BlockSpec((1, D), lambda i: (0, 0))],
        out_specs=pl.BlockSpec((tt, D), lambda i: (i, 0)), out_shape=_sds((t, D), BF16),
        compiler_params=_cp("parallel"),
    )(x, gain)


def _rms_bwd(name, du, x, gain, dres):
    t = x.shape[0]
    tt = min(256, t)

    def body(du_ref, x_ref, g_ref, dres_ref, dx_ref, dg_ref):
        @pl.when(pl.program_id(0) == 0)
        def _():
            dg_ref[...] = jnp.zeros_like(dg_ref)

        xv, duv = x_ref[...], du_ref[...]
        r = lax.rsqrt(jnp.mean(xv * xv, axis=-1, keepdims=True) + EPS)
        nx = xv * r
        dn = duv * g_ref[...]
        dg_ref[...] += jnp.sum(duv * nx, axis=0, keepdims=True)
        dx_ref[...] = dres_ref[...] + r * (dn - nx * jnp.mean(dn * nx, axis=-1, keepdims=True))

    row = pl.BlockSpec((tt, D), lambda i: (i, 0))
    vec = pl.BlockSpec((1, D), lambda i: (0, 0))
    return pl.pallas_call(
        body, name=name, grid=(t // tt,), in_specs=[row, row, vec, row], out_specs=[row, vec],
        out_shape=[_sds((t, D), F32), _sds((1, D), F32)], compiler_params=_cp("arbitrary"),
    )(du, x, gain, dres)


def _loss_head(x, gain, target):
    t = x.shape[0]
    tt = min(256, t)

    def body(x_ref, g_ref, t_ref, loss_ref, dx_ref, dg_ref):
        @pl.when(pl.program_id(0) == 0)
        def _():
            dg_ref[...] = jnp.zeros_like(dg_ref)
            loss_ref[...] = jnp.zeros_like(loss_ref)

        xv = x_ref[...]
        r = lax.rsqrt(jnp.mean(xv * xv, axis=-1, keepdims=True) + EPS)
        nx = xv * r
        err = nx * g_ref[...] - t_ref[...]
        loss_ref[...] += 0.5 * jnp.sum(jnp.mean(err * err, axis=-1, keepdims=True), axis=0, keepdims=True)
        dy = err * (1.0 / D)
        dn = dy * g_ref[...]
        dg_ref[...] += jnp.sum(dy * nx, axis=0, keepdims=True)
        dx_ref[...] = r * (dn - nx * jnp.mean(dn * nx, axis=-1, keepdims=True))

    row = pl.BlockSpec((tt, D), lambda i: (i, 0))
    vec = pl.BlockSpec((1, D), lambda i: (0, 0))
    one = pl.BlockSpec((1, 1), lambda i: (0, 0))
    return pl.pallas_call(
        body, name="loss_head", grid=(t // tt,), in_specs=[row, vec, row], out_specs=[one, row, vec],
        out_shape=[_sds((1, 1), F32), _sds((t, D), F32), _sds((1, D), F32)], compiler_params=_cp("arbitrary"),
    )(x, gain, target)


def _shift_down(v, s, t_idx):
    return jnp.where(t_idx >= s, pltpu.roll(v, s, 0), 0.0)


def _shift_up(v, s, t_idx, t):
    return jnp.where(t_idx < t - s, pltpu.roll(v, t - s, 0), 0.0)


def _pool_d(p, g, t_idx):
    s = p
    for step in range(g + 1):
        s = s + _shift_down(s, 1 << step, t_idx)
    cnt = jnp.minimum(t_idx + 1, POOL_WINDOWS[g]).astype(F32)
    return s / cnt - p, cnt


def _pool_fwd(proj, pool_w, pool_scale):
    t = proj.shape[0]
    g128 = POOL_W // len(POOL_WINDOWS)

    def body(p_ref, w_ref, s_ref, y_ref):
        t_idx = lax.broadcasted_iota(jnp.int32, (t, g128), 0)
        for g in range(len(POOL_WINDOWS)):
            sl = slice(g * g128, (g + 1) * g128)
            d, _ = _pool_d(p_ref[:, sl], g, t_idx)
            y_ref[:, sl] = (_bdot(d, w_ref[g]) * s_ref[:, sl]).astype(y_ref.dtype)

    return pl.pallas_call(
        body, name="pool_fwd", grid=(1,),
        in_specs=[pl.BlockSpec((t, POOL_W), lambda i: (0, OFF_P // POOL_W)),
                  pl.BlockSpec((4, g128, g128), lambda i: (0, 0, 0)), pl.BlockSpec((1, POOL_W), lambda i: (0, 0))],
        out_specs=pl.BlockSpec((t, POOL_W), lambda i: (0, 0)), out_shape=_sds((t, POOL_W), BF16),
        compiler_params=_cp("arbitrary"),
    )(proj, pool_w, pool_scale)


def _pool_bwd(proj, pool_w, pool_scale, dy, dproj):
    t = proj.shape[0]
    g128 = POOL_W // len(POOL_WINDOWS)

    def body(p_ref, w_ref, s_ref, dy_ref, _, dp_ref, dw_ref, ds_ref):
        t_idx = lax.broadcasted_iota(jnp.int32, (t, g128), 0)
        for g in range(len(POOL_WINDOWS)):
            sl = slice(g * g128, (g + 1) * g128)
            d, cnt = _pool_d(p_ref[:, sl], g, t_idx)
            dyv = dy_ref[:, sl].astype(F32)
            ds_ref[:, sl] = jnp.sum(dyv * _bdot(d, w_ref[g]), axis=0, keepdims=True)
            dys = dyv * s_ref[:, sl]
            dw_ref[g] = _bdot(d, dys, TN)
            dd = _bdot(dys, w_ref[g], NT)
            s = dd / cnt
            for step in range(g + 1):
                s = s + _shift_up(s, 1 << step, t_idx, t)
            dp_ref[:, sl] = (s - dd).astype(dp_ref.dtype)

    return pl.pallas_call(
        body, name="pool_bwd", grid=(1,),
        in_specs=[pl.BlockSpec((t, POOL_W), lambda i: (0, OFF_P // POOL_W)),
                  pl.BlockSpec((4, g128, g128), lambda i: (0, 0, 0)), pl.BlockSpec((1, POOL_W), lambda i: (0, 0)),
                  pl.BlockSpec((t, POOL_W), lambda i: (0, 0)), ANY],
        out_specs=[pl.BlockSpec((t, POOL_W), lambda i: (0, OFF_P // POOL_W)), pl.BlockSpec((4, g128, g128), lambda i: (0, 0, 0)),
                   pl.BlockSpec((1, POOL_W), lambda i: (0, 0))],
        out_shape=[_sds(dproj.shape, dproj.dtype), _sds((4, g128, g128), F32), _sds((1, POOL_W), F32)],
        input_output_aliases={4: 0}, compiler_params=_cp("arbitrary"),
    )(proj, pool_w, pool_scale, dy, dproj)


SB_GROUP = 3
SB_GW = SB_GROUP * HD


def _sb_cast_kv(proj):
    t = proj.shape[0]
    tt = min(512, t)

    def body(x_ref, o_ref):
        o_ref[...] = x_ref[...].astype(BF16)

    return pl.pallas_call(
        body, name="sb_cast_kv", grid=(t // tt, 2),
        in_specs=[pl.BlockSpec((tt, SB_W), lambda i, j: (i, OFF_SB // SB_W + 1 + j))],
        out_specs=pl.BlockSpec((tt, SB_W), lambda i, j: (i, j)), out_shape=_sds((t, 2 * SB_W), BF16),
        compiler_params=_cp("parallel", "parallel"),
    )(proj)


def _sb_specs(t):
    q_spec = pl.BlockSpec((SB_BLOCK, SB_GW), lambda g, i: (i, OFF_SB // SB_GW + g))
    k_spec = pl.BlockSpec((t, SB_GW), lambda g, i: (0, g))
    v_spec = pl.BlockSpec((t, SB_GW), lambda g, i: (0, SB_W // SB_GW + g))
    return q_spec, k_spec, v_spec


def _head(ref, h, rows=None):
    cols = slice(h * HD, (h + 1) * HD)
    return ref[:, cols] if rows is None else ref[rows, cols]


SB_KEYS = 512


def _sub(v, b):
    return v[:, b * SB_BLOCK:(b + 1) * SB_BLOCK]


def _sb_keep(kc, limit):
    row = lax.broadcasted_iota(jnp.int32, (SB_BLOCK, kc), 0)
    col = lax.broadcasted_iota(jnp.int32, (SB_BLOCK, kc), 1)
    return col < row + limit


def _sb_chunk(q, keys, run, later, limit):
    kc = keys.shape[0]
    z = _dot(q, keys, NT)
    lsz = jax.nn.log_sigmoid(z)
    ls = lsz - z
    if limit is not None:
        keep = _sb_keep(kc, limit)
        ls = jnp.where(keep, ls, 0.0)
    parts = [None] * (kc // SB_BLOCK)
    for b in reversed(range(kc // SB_BLOCK)):
        parts[b] = _hdot(_sub(ls, b), later) + run
        run = run + jnp.sum(_sub(ls, b), axis=1, keepdims=True)
    a = jnp.exp(lsz + jnp.concatenate(parts, axis=1))
    if limit is not None:
        a = jnp.where(keep, a, 0.0)
    return z, a, run


def _sb_fwd(proj, kv):
    t = proj.shape[0]
    kc = min(SB_KEYS, t)
    scale = HD ** -0.5

    def body(q_ref, k_ref, v_ref, o_ref):
        i = pl.program_id(1)
        top = (i * SB_BLOCK) // kc
        qs = [(_head(q_ref, h) * scale).astype(BF16) for h in range(SB_GROUP)]
        row = lax.broadcasted_iota(jnp.int32, (SB_BLOCK, SB_BLOCK), 0)
        col = lax.broadcasted_iota(jnp.int32, (SB_BLOCK, SB_BLOCK), 1)
        later = (row > col).astype(F32)

        def chunk(jc, carry, masked):
            rows = pl.ds(pl.multiple_of(jc * kc, kc), kc)
            limit = i * SB_BLOCK - jc * kc if masked else None
            out = []
            for h in range(SB_GROUP):
                acc, run = carry[h]
                _, a, run = _sb_chunk(qs[h], _head(k_ref, h, rows), run, later, limit)
                out.append((acc + _dot(a.astype(BF16), _head(v_ref, h, rows)), run))
            return tuple(out)

        zero = tuple((jnp.zeros((SB_BLOCK, HD), F32), jnp.zeros((SB_BLOCK, 1), F32)) for _ in range(SB_GROUP))
        carry = chunk(top, zero, True)
        carry = lax.fori_loop(0, top, lambda jj, c: chunk(top - 1 - jj, c, False), carry)
        for h in range(SB_GROUP):
            o_ref[:, h * HD:(h + 1) * HD] = carry[h][0].astype(o_ref.dtype)

    return pl.pallas_call(
        body, name="sb_fwd", grid=(HEADS // SB_GROUP, t // SB_BLOCK), in_specs=list(_sb_specs(t)),
        out_specs=pl.BlockSpec((SB_BLOCK, SB_GW), lambda g, i: (i, g)), out_shape=_sds((t, SB_W), BF16),
        compiler_params=_cp("parallel", "arbitrary"),
    )(proj, kv, kv)


def _sb_bwd(proj, kv, dy):
    t = proj.shape[0]
    nq = t // SB_BLOCK
    kc = min(SB_KEYS, t)
    scale = HD ** -0.5

    def body(q_ref, k_ref, v_ref, do_ref, dq_ref, dk_ref, dv_ref, z_scr, e_scr):
        i = pl.program_id(1)
        top = (i * SB_BLOCK) // kc

        @pl.when(i == 0)
        def _():
            dk_ref[...] = jnp.zeros_like(dk_ref)
            dv_ref[...] = jnp.zeros_like(dv_ref)

        qs = [(_head(q_ref, h) * scale).astype(BF16) for h in range(SB_GROUP)]
        dos = [_head(do_ref, h).astype(BF16) for h in range(SB_GROUP)]
        row = lax.broadcasted_iota(jnp.int32, (SB_BLOCK, SB_BLOCK), 0)
        col = lax.broadcasted_iota(jnp.int32, (SB_BLOCK, SB_BLOCK), 1)
        later = (row > col).astype(F32)
        earlier = (row < col).astype(F32)

        def down(jc, runs, masked):
            rows = pl.ds(pl.multiple_of(jc * kc, kc), kc)
            limit = i * SB_BLOCK - jc * kc if masked else None
            out = []
            for h in range(SB_GROUP):
                z, a, run = _sb_chunk(qs[h], _head(k_ref, h, rows), runs[h], later, limit)
                z_scr[h, jc] = z
                e_scr[h, jc] = a * _dot(dos[h], _head(v_ref, h, rows), NT)
                dv_ref[rows, h * HD:(h + 1) * HD] += _dot(a.astype(BF16), dos[h], TN)
                out.append(run)
            return tuple(out)

        zero = tuple(jnp.zeros((SB_BLOCK, 1), F32) for _ in range(SB_GROUP))
        runs = down(top, zero, True)
        lax.fori_loop(0, top, lambda jj, r: down(top - 1 - jj, r, False), runs)

        def up(jc, carry, masked):
            rows = pl.ds(pl.multiple_of(jc * kc, kc), kc)
            out = []
            for h in range(SB_GROUP):
                dq, run = carry[h]
                z, e = z_scr[h, jc], e_scr[h, jc]
                parts = []
                for b in range(kc // SB_BLOCK):
                    parts.append(_hdot(_sub(e, b), earlier) + run)
                    run = run + jnp.sum(_sub(e, b), axis=1, keepdims=True)
                sz = jax.nn.sigmoid(z)
                dz = e * (1.0 - sz) - jnp.concatenate(parts, axis=1) * sz
                if masked:
                    dz = jnp.where(_sb_keep(kc, i * SB_BLOCK - jc * kc), dz, 0.0)
                dz = dz.astype(BF16)
                dk_ref[rows, h * HD:(h + 1) * HD] += _dot(dz, qs[h], TN)
                out.append((dq + _dot(dz, _head(k_ref, h, rows)), run))
            return tuple(out)

        zero = tuple((jnp.zeros((SB_BLOCK, HD), F32), jnp.zeros((SB_BLOCK, 1), F32)) for _ in range(SB_GROUP))
        carry = lax.fori_loop(0, top, lambda jc, c: up(jc, c, False), zero)
        carry = up(top, carry, True)
        for h in range(SB_GROUP):
            dq_ref[:, h * HD:(h + 1) * HD] = (carry[h][0] * scale).astype(dq_ref.dtype)

    blk = pl.BlockSpec((SB_BLOCK, SB_GW), lambda g, i: (i, g))
    seq = pl.BlockSpec((t, SB_GW), lambda g, i: (0, g))
    scratch = pltpu.VMEM((SB_GROUP, t // kc, SB_BLOCK, kc), F32)
    return pl.pallas_call(
        body, name="sb_bwd", grid=(HEADS // SB_GROUP, nq), in_specs=[*_sb_specs(t), blk], out_specs=[blk, seq, seq],
        out_shape=[_sds((t, SB_W), BF16), _sds((t, SB_W), F32), _sds((t, SB_W), F32)],
        scratch_shapes=[scratch, scratch], compiler_params=_cp("parallel", "arbitrary"),
    )(proj, kv, kv, dy)


CONV_TILE = 256
GDN_CONV = 4


def _conv_pre(x, w_ref, t_idx):
    pre = w_ref[GDN_CONV - 1:GDN_CONV, :] * x
    for s in range(1, GDN_CONV):
        pre = pre + w_ref[GDN_CONV - 1 - s:GDN_CONV - s, :] * _shift_down(x, s, t_idx)
    return pre


def _conv_fwd(proj, conv_w):
    t = proj.shape[0]
    width = conv_w.shape[1]

    def body(x_ref, w_ref, y_ref):
        t_idx = lax.broadcasted_iota(jnp.int32, (t, CONV_TILE), 0)
        pre = _conv_pre(x_ref[...], w_ref, t_idx)
        y_ref[...] = pre * jax.nn.sigmoid(pre)

    return pl.pallas_call(
        body, name="conv_fwd", grid=(width // CONV_TILE,),
        in_specs=[pl.BlockSpec((t, CONV_TILE), lambda c: (0, OFF_GQKV // CONV_TILE + c)),
                  pl.BlockSpec((GDN_CONV, CONV_TILE), lambda c: (0, c))],
        out_specs=pl.BlockSpec((t, CONV_TILE), lambda c: (0, c)), out_shape=_sds((t, width), F32),
        compiler_params=_cp("parallel"),
    )(proj, conv_w)


def _conv_bwd(proj, conv_w, dc, dproj):
    t = proj.shape[0]
    width = dc.shape[1]
    per = width // CONV_TILE
    first = OFF_GQKV // CONV_TILE

    def body(x_ref, w_ref, dc_ref, _, dx_ref, dw_ref):
        t_idx = lax.broadcasted_iota(jnp.int32, (t, CONV_TILE), 0)
        x = x_ref[...]
        pre = _conv_pre(x, w_ref, t_idx)
        sg = jax.nn.sigmoid(pre)
        dpre = dc_ref[...] * (sg * (1.0 + pre * (1.0 - sg)))
        dx = w_ref[GDN_CONV - 1:GDN_CONV, :] * dpre
        dw_ref[GDN_CONV - 1:GDN_CONV, :] = jnp.sum(dpre * x, axis=0, keepdims=True)
        for s in range(1, GDN_CONV):
            dx = dx + w_ref[GDN_CONV - 1 - s:GDN_CONV - s, :] * _shift_up(dpre, s, t_idx, t)
            dw_ref[GDN_CONV - 1 - s:GDN_CONV - s, :] = jnp.sum(dpre * _shift_down(x, s, t_idx), axis=0, keepdims=True)
        dx_ref[...] = dx.astype(dx_ref.dtype)

    return pl.pallas_call(
        body, name="conv_bwd", grid=(per,),
        in_specs=[pl.BlockSpec((t, CONV_TILE), lambda c: (0, first + c)),
                  pl.BlockSpec((GDN_CONV, CONV_TILE), lambda c: (0, c)),
                  pl.BlockSpec((t, CONV_TILE), lambda c: (0, c)), ANY],
        out_specs=[pl.BlockSpec((t, CONV_TILE), lambda c: (0, first + c)), pl.BlockSpec((GDN_CONV, CONV_TILE), lambda c: (0, c))],
        out_shape=[_sds(dproj.shape, dproj.dtype), _sds((GDN_CONV, width), F32)],
        input_output_aliases={3: 0}, compiler_params=_cp("parallel"),
    )(proj, conv_w, dc, dproj)


def _heads(x):
    return jnp.concatenate([x[:, h * HD:(h + 1) * HD][None] for h in range(HEADS)], axis=0)


def _hb(a, b, ca=2, cb=1):
    return lax.dot_general(a, b, (((ca,), (cb,)), ((0,), (0,))), precision=HIGH, preferred_element_type=F32)


@jax.custom_vjp
def _unit_lower_inverse(lower):
    c = lower.shape[-1]
    eye = lax.broadcasted_iota(jnp.int32, (c, c), 0) == lax.broadcasted_iota(jnp.int32, (c, c), 1)
    inv = jnp.where(eye, 1.0, 0.0) - lower
    pw = _hb(lower, lower)
    for step in range(5):
        inv = inv + _hb(inv, pw)
        if step < 4:
            pw = _hb(pw, pw)
    return inv


def _unit_lower_inverse_fwd(lower):
    inv = _unit_lower_inverse(lower)
    return inv, inv


def _unit_lower_inverse_bwd(inv, d_inv):
    return (-_hb(_hb(inv, d_inv, 1, 1), inv, 2, 2),)


_unit_lower_inverse.defvjp(_unit_lower_inverse_fwd, _unit_lower_inverse_bwd)


@jax.custom_vjp
def _known_inverse(lower, inv):
    return inv


_known_inverse.defvjp(lambda lower, inv: (inv, inv),
                      lambda inv, d_inv: (*_unit_lower_inverse_bwd(inv, d_inv), jnp.zeros_like(inv)))


def _gdn_prep(cq, ck, cv, ab, alog_row, dtb_row, inv=None, keep_inverse=False):
    c = GDN_CHUNK
    row = lax.broadcasted_iota(jnp.int32, (c, c), 0)
    col = lax.broadcasted_iota(jnp.int32, (c, c), 1)
    incl, strict, eye = row >= col, row > col, row == col
    def lanes(v, first):
        return jnp.concatenate([v[:, first + h:first + h + 1][None] for h in range(HEADS)], axis=0)

    a_col, b_col = lanes(ab, 0), lanes(ab, HEADS)
    a_log, dt_bias = lanes(alog_row, 0), lanes(dtb_row, 0)
    qn = cq * lax.rsqrt(jnp.sum(cq * cq, axis=-1, keepdims=True) + EPS) * (HD ** -0.5)
    kn = ck * lax.rsqrt(jnp.sum(ck * ck, axis=-1, keepdims=True) + EPS)
    la_col = -jnp.exp(a_log) * jax.nn.softplus(a_col + dt_bias)
    beta = jax.nn.sigmoid(b_col)
    la_row = jnp.sum(jnp.where(eye, la_col, 0.0), axis=1, keepdims=True)
    g_col = jnp.sum(jnp.where(incl, la_row, 0.0), axis=2, keepdims=True)
    g_row = jnp.sum(jnp.where(row <= col, la_col, 0.0), axis=1, keepdims=True)
    g_last = jnp.sum(la_col, axis=1, keepdims=True)
    gamma = jnp.where(incl, jnp.exp(jnp.where(incl, g_col - g_row, 0.0)), 0.0)
    lower = jnp.where(strict, beta * _hb(kn, kn, 2, 2) * gamma, 0.0)
    inv = _unit_lower_inverse(lower) if inv is None else _known_inverse(lower, inv)
    u = _hb(inv, cv * beta)
    w = _hb(inv, kn * (beta * jnp.exp(g_col)))
    qk = _hb(qn, kn, 2, 2) * gamma
    out = (u, w, qk, qn * jnp.exp(g_col), kn * jnp.exp(g_last - g_col), jnp.exp(g_last))
    return (*out, inv) if keep_inverse else out


def _gdn_post(o, z, gain):
    y = o * lax.rsqrt(jnp.mean(o * o, axis=-1, keepdims=True) + EPS) * gain
    return y * (z * jax.nn.sigmoid(z))


def _gdn_specs(nc, reverse):
    c = GDN_CHUNK

    def ch(n):
        return nc - 1 - n if reverse else n

    def wide(array_off):
        return pl.BlockSpec((c, GDN_W), lambda n: (ch(n), array_off // GDN_W))

    zab = pl.BlockSpec((c, ZAB_W), lambda n: (ch(n), OFF_Z // ZAB_W))
    row = pl.BlockSpec((1, HD), lambda n: (0, 0))
    state = pl.BlockSpec((None, HEADS, HD, HD), lambda n: (ch(n), 0, 0, 0))
    inverse = pl.BlockSpec((None, HEADS, c, c), lambda n: (ch(n), 0, 0, 0))
    return wide, zab, row, state, inverse


def _gdn_fwd(cqkv, proj, a_log, dt_bias, gain):
    t = proj.shape[0]
    nc = t // GDN_CHUNK
    wide, zab, row, state, inverse = _gdn_specs(nc, False)

    def body(cq_ref, ck_ref, cv_ref, zab_ref, al_ref, dt_ref, g_ref, y_ref, sprev_ref, inv_ref, s_scr):
        @pl.when(pl.program_id(0) == 0)
        def _():
            s_scr[...] = jnp.zeros_like(s_scr)

        z_ref, ab_ref = zab_ref.at[:, :GDN_W], zab_ref.at[:, GDN_W:GDN_W + HD]
        u, w, qk, qd, kd, dec, inv = _gdn_prep(_heads(cq_ref[...]), _heads(ck_ref[...]), _heads(cv_ref[...]), ab_ref[...],
                                               al_ref[...], dt_ref[...], keep_inverse=True)
        inv_ref[...] = inv
        s = s_scr[...]
        sprev_ref[...] = s
        v_new = u - _hb(w, s)
        o = _hb(qd, s) + _hb(qk, v_new)
        s_scr[...] = s * dec + _hb(kd, v_new, 1, 1)
        y = _gdn_post(o, _heads(z_ref[...]), g_ref[...])
        for h in range(HEADS):
            y_ref[:, h * HD:(h + 1) * HD] = y[h].astype(y_ref.dtype)

    return pl.pallas_call(
        body, name="gdn_fwd", grid=(nc,),
        in_specs=[wide(0), wide(GDN_W), wide(2 * GDN_W), zab, row, row, row],
        out_specs=[wide(0), state, inverse],
        out_shape=[_sds((t, GDN_W), BF16), _sds((nc, HEADS, HD, HD), F32), _sds((nc, HEADS, GDN_CHUNK, GDN_CHUNK), F32)],
        scratch_shapes=[pltpu.VMEM((HEADS, HD, HD), F32)], compiler_params=_cp("arbitrary"),
    )(cqkv, cqkv, cqkv, proj, a_log, dt_bias, gain)


def _gdn_bwd(cqkv, proj, a_log, dt_bias, gain, sprev, inverses, dy, dproj):
    t = proj.shape[0]
    nc = t // GDN_CHUNK
    wide, zab, row, state, inverse = _gdn_specs(nc, True)

    def body(cq_ref, ck_ref, cv_ref, zab_ref, al_ref, dt_ref, g_ref, sp_ref, inv_ref, dy_ref, _,
             dc_ref, dzab_ref, dal_ref, ddt_ref, dg_ref, ds_scr):
        @pl.when(pl.program_id(0) == 0)
        def _():
            ds_scr[...] = jnp.zeros_like(ds_scr)
            dal_ref[...] = jnp.zeros_like(dal_ref)
            ddt_ref[...] = jnp.zeros_like(ddt_ref)
            dg_ref[...] = jnp.zeros_like(dg_ref)

        z_ref, ab_ref = zab_ref.at[:, :GDN_W], zab_ref.at[:, GDN_W:GDN_W + HD]

        (u, w, qk, qd, kd, dec), prep_vjp = jax.vjp(
            functools.partial(_gdn_prep, inv=inv_ref[...]),
            _heads(cq_ref[...]), _heads(ck_ref[...]), _heads(cv_ref[...]), ab_ref[...], al_ref[...], dt_ref[...])
        s = sp_ref[...]
        v_new = u - _hb(w, s)
        o = _hb(qd, s) + _hb(qk, v_new)
        _, post_vjp = jax.vjp(_gdn_post, o, _heads(z_ref[...]), g_ref[...])
        do, dz, dgain = post_vjp(_heads(dy_ref[...]).astype(F32))
        ds_next = ds_scr[...]
        d_vnew = _hb(qk, do, 1, 1) + _hb(kd, ds_next)
        d_qk = _hb(do, v_new, 2, 2)
        d_qd = _hb(do, s, 2, 2)
        d_kd = _hb(v_new, ds_next, 2, 2)
        d_dec = jnp.sum(jnp.sum(s * ds_next, axis=2, keepdims=True), axis=1, keepdims=True)
        ds_scr[...] = dec * ds_next + _hb(qd, do, 1, 1) - _hb(w, d_vnew, 1, 1)
        d_w = -_hb(d_vnew, s, 2, 2)
        dcq, dck, dcv, dab, dal, ddt = prep_vjp((d_vnew, d_w, d_qk, d_qd, d_kd, d_dec))
        for h in range(HEADS):
            dc_ref[:, h * HD:(h + 1) * HD] = dcq[h]
            dc_ref[:, GDN_W + h * HD:GDN_W + (h + 1) * HD] = dck[h]
            dc_ref[:, 2 * GDN_W + h * HD:2 * GDN_W + (h + 1) * HD] = dcv[h]
            dzab_ref[:, h * HD:(h + 1) * HD] = dz[h].astype(dzab_ref.dtype)
        dzab_ref[:, GDN_W:GDN_W + HD] = dab.astype(dzab_ref.dtype)
        dzab_ref[:, GDN_W + HD:] = jnp.zeros((GDN_CHUNK, AB_W - HD), dzab_ref.dtype)
        dal_ref[...] += dal
        ddt_ref[...] += ddt
        dg_ref[...] += dgain

    c = GDN_CHUNK
    return pl.pallas_call(
        body, name="gdn_bwd", grid=(nc,),
        in_specs=[wide(0), wide(GDN_W), wide(2 * GDN_W), zab, row, row, row, state, inverse, wide(0), ANY],
        out_specs=[pl.BlockSpec((c, 3 * GDN_W), lambda n: (nc - 1 - n, 0)), zab, row, row, row],
        out_shape=[_sds((t, 3 * GDN_W), F32), _sds(dproj.shape, dproj.dtype),
                   _sds((1, HD), F32), _sds((1, HD), F32), _sds((1, HD), F32)],
        input_output_aliases={10: 1},
        scratch_shapes=[pltpu.VMEM((HEADS, HD, HD), F32)], compiler_params=_cp("arbitrary"),
    )(cqkv, cqkv, cqkv, proj, a_log, dt_bias, gain, sprev, inverses, dy, dproj)


MERGE_TN = 512


def _merge_specs(t, tm):
    tn = MERGE_TN
    ys = [pl.BlockSpec((tm, wd), lambda i, j: (i, 0)) for wd in (POOL_W, SB_W, GDN_W)]
    ws = [pl.BlockSpec((None, wd, tn), lambda i, j: (j, 0, 0)) for wd in (POOL_W, SB_W, GDN_W)]
    gs = [pl.BlockSpec((tm, tn), functools.partial(lambda i, j, b: (i, OFF_GATE // tn + b * (D // tn) + j), b=b))
          for b in range(3)]
    out = pl.BlockSpec((tm, tn), lambda i, j: (i, j))
    return ys, ws, gs, out


def _merge_fwd(ys, wups, proj):
    t = proj.shape[0]
    tm = min(512, t)
    y_specs, w_specs, g_specs, out = _merge_specs(t, tm)

    def body(y0, y1, y2, w0, w1, w2, g0, g1, g2, o_ref):
        acc = jnp.zeros(o_ref.shape, F32)
        for y, w, g in ((y0, w0, g0), (y1, w1, g1), (y2, w2, g2)):
            acc = acc + jax.nn.sigmoid(g[...]) * _dot(y[...], w[...])
        o_ref[...] = acc.astype(o_ref.dtype)

    return pl.pallas_call(
        body, name="merge_fwd", grid=(t // tm, D // MERGE_TN), in_specs=[*y_specs, *w_specs, *g_specs],
        out_specs=out, out_shape=_sds((t, D), BF16), compiler_params=_cp("parallel", "parallel"),
    )(*ys, *wups, proj, proj, proj)


def _merge_bwd(ys, wups, proj, dmerged):
    t = proj.shape[0]
    tm = min(1024, t)
    tn = MERGE_TN
    per = D // tn
    ys_specs = [pl.BlockSpec((tm, wd), lambda i, b, j: (i, 0)) for wd in (POOL_W, SB_W, GDN_W)]

    def w_spec(k, wd):
        return pl.BlockSpec((None, wd, tn), lambda i, b, j: (jnp.where(b == k, j, jnp.where(b < k, 0, per - 1)), 0, 0))

    w_specs = [w_spec(k, wd) for k, wd in enumerate((POOL_W, SB_W, GDN_W))]
    gate = pl.BlockSpec((tm, tn), lambda i, b, j: (i, OFF_GATE // tn + b * per + j))
    branch = pl.BlockSpec((tm, tn), lambda i, b, j: (i, b * per + j))
    merged = pl.BlockSpec((tm, tn), lambda i, b, j: (i, j))

    def body(y0, y1, y2, w0, w1, w2, g_ref, dm_ref, dg_ref, dmb_ref):
        b = pl.program_id(1)
        dm = dm_ref[...].astype(F32)
        sg = jax.nn.sigmoid(g_ref[...])
        dmb_ref[...] = (dm * sg).astype(dmb_ref.dtype)
        for k, (y, w) in enumerate(((y0, w0), (y1, w1), (y2, w2))):
            @pl.when(b == k)
            def _():
                dg_ref[...] = (dm * _dot(y[...], w[...]) * sg * (1.0 - sg)).astype(dg_ref.dtype)

    return pl.pallas_call(
        body, name="merge_bwd", grid=(t // tm, 3, per), in_specs=[*ys_specs, *w_specs, gate, merged],
        out_specs=[gate, branch], out_shape=[_sds((t, N_AL), BF16), _sds((t, 3 * D), BF16)],
        compiler_params=_cp("parallel", "arbitrary", "arbitrary"),
    )(*ys, *wups, proj, dmerged)


def _place(name, dproj, src, col):
    t, w = src.shape
    tt = min(512, t)

    def body(s_ref, _, o_ref):
        o_ref[...] = s_ref[...].astype(o_ref.dtype)

    return pl.pallas_call(
        body, name=name, grid=(t // tt,), in_specs=[pl.BlockSpec((tt, w), lambda i: (i, 0)), ANY],
        out_specs=pl.BlockSpec((tt, w), lambda i: (i, col // w)), out_shape=_sds(dproj.shape, dproj.dtype),
        input_output_aliases={1: 0}, compiler_params=_cp("parallel"),
    )(src, dproj)


def _tile(t, want):
    return min(t, want)


def _layer_fwd(x, l, gw, w_al, sp, reached=None):
    t = x.shape[0]
    tm = _tile(t, 1024)
    u = _rms_fwd("rms_attn", x, sp["attn_norm"][l])
    proj = _mm("proj", u, w_al, m=t, n=N_AL, k=D, tm=tm, tn=1024, tk=D, a_spec=_a_plain(tm, D),
               b_spec=_b_plain(D, 1024), dims=None, out_shapes=[_sds((t, N_AL), F32)], out_specs=[_o_plain(tm, 1024)])[0]
    if reached is not None:
        reached("proj", proj)
    y_pool = _pool_fwd(proj, sp["pool_w"][l], sp["pool_scale"][l])
    kv = _sb_cast_kv(proj)
    y_sb = _sb_fwd(proj, kv)
    cqkv = _conv_fwd(proj, sp["conv"][l])
    y_gdn, sprev, inverses = _gdn_fwd(cqkv, proj, sp["a_log"][l], sp["dt_bias"][l], sp["gdn_norm"][l])
    ys = (y_pool, y_sb, y_gdn)
    wups = (gw["w_pool_up"], gw["w_sb_up"], gw["w_gdn_up"])
    merged = _merge_fwd(ys, wups, proj)
    if reached is not None:
        reached("merged", merged)
    x1 = _mm("out_proj", merged, gw["w_out"], m=t, n=D, k=D, tm=tm, tn=1024, tk=512, a_spec=_a_plain(tm, 512),
             b_spec=_w_rows(512, 1024, 512), dims=None, out_shapes=[_sds((t, D), F32)], out_specs=[_o_plain(tm, 1024)],
             extras=[x], extra_specs=[_o_plain(tm, 1024)], epilogue=lambda r, xr: (r + xr,))[0]
    u2 = _rms_fwd("rms_mlp", x1, sp["mlp_norm"][l])

    def relu2(r):
        hv = jnp.maximum(r, 0.0)
        return hv, hv * hv

    hid, hid2 = _mm("ff1", u2, gw["w_ff1"], m=t, n=D_FF, k=D, tm=tm, tn=1024, tk=D, a_spec=_a_plain(tm, D),
                    b_spec=_w_cols(D, 1024, 2048), dims=None, out_shapes=[_sds((t, D_FF), BF16)] * 2,
                    out_specs=[_o_plain(tm, 1024)] * 2, epilogue=relu2)
    x2 = _mm("ff2", hid2, gw["w_ff2"], m=t, n=D, k=D_FF, tm=tm, tn=1024, tk=2048, a_spec=_a_plain(tm, 2048),
             b_spec=_w_rows(2048, 1024, 2048), dims=None, out_shapes=[_sds((t, D), F32)], out_specs=[_o_plain(tm, 1024)],
             extras=[x1], extra_specs=[_o_plain(tm, 1024)], epilogue=lambda r, xr: (r + xr,))[0]
    saved = dict(x=x, u=u, proj=proj, kv=kv, cqkv=cqkv, sprev=sprev, inverses=inverses, ys=ys, merged=merged, x1=x1, u2=u2, hid=hid, hid2=hid2)
    return x2, saved


def _layer_bwd(dx2, l, gw, w_al, sp, sv, emit=None):
    t = dx2.shape[0]
    tm = _tile(t, 1024)
    tk = t
    g = {}
    if emit is None:
        emit = lambda names, grads, v: v
    dpre = _mm("ff2_dx", dx2, gw["w_ff2"], m=t, n=D_FF, k=D, tm=tm, tn=1024, tk=D, a_spec=_a_plain(tm, D),
               b_spec=_w_rows_t(D, 1024, 2048), dims=NT, out_shapes=[_sds((t, D_FF), BF16)],
               out_specs=[_o_plain(tm, 1024)], extras=[sv["hid"]], extra_specs=[_o_plain(tm, 1024)],
               epilogue=lambda r, hv: (r * (2.0 * hv.astype(F32)),))[0]
    g["w_ff2"] = _mm("ff2_dw", sv["hid2"], dx2, m=D_FF, n=D, k=t, tm=1024, tn=1024, tk=tk, a_spec=_a_trans(1024, tk),
                     b_spec=_b_plain(tk, 1024), dims=TN, out_shapes=[_sds((D_FF, D), BF16)],
                     out_specs=[_o_plain(1024, 1024)])[0].reshape(N_CHIPS, D_FF // N_CHIPS, D)
    du2 = _mm("ff1_dx", dpre, gw["w_ff1"], m=t, n=D, k=D_FF, tm=tm, tn=1024, tk=2048, a_spec=_a_plain(tm, 2048),
              b_spec=_w_cols_t(2048, 1024, 2048), dims=NT, out_shapes=[_sds((t, D), F32)], out_specs=[_o_plain(tm, 1024)])[0]
    g["w_ff1"] = _mm("ff1_dw", sv["u2"], dpre, m=D, n=D_FF, k=t, tm=1024, tn=1024, tk=tk, a_spec=_a_trans(1024, tk),
                     b_spec=_b_plain(tk, 1024), dims=TN, out_shapes=[_sds((N_CHIPS, D, D_FF // N_CHIPS), BF16)],
                     out_specs=[_o_colshard(1024, 1024, D_FF // N_CHIPS)])[0]
    dx1, g["mlp_norm"] = _rms_bwd("rms_mlp_bwd", du2, sv["x1"], sp["mlp_norm"][l], dx2)
    dx1 = emit(("w_ff1", "w_ff2"), g, dx1)
    dmerged = _mm("out_dx", dx1, gw["w_out"], m=t, n=D, k=D, tm=tm, tn=512, tk=D, a_spec=_a_plain(tm, D),
                  b_spec=_w_rows_t(D, 512, 512), dims=NT, out_shapes=[_sds((t, D), BF16)], out_specs=[_o_plain(tm, 512)])[0]
    g["w_out"] = _mm("out_dw", sv["merged"], dx1, m=D, n=D, k=t, tm=1024, tn=1024, tk=tk, a_spec=_a_trans(1024, tk),
                     b_spec=_b_plain(tk, 1024), dims=TN, out_shapes=[_sds((D, D), BF16)],
                     out_specs=[_o_plain(1024, 1024)])[0].reshape(N_CHIPS, D // N_CHIPS, D)
    wups = (gw["w_pool_up"], gw["w_sb_up"], gw["w_gdn_up"])
    dproj, dm_all = _merge_bwd(sv["ys"], wups, sv["proj"], dmerged)
    per = D // MERGE_TN
    dys = []
    for b, (nm, yb, wd) in enumerate(zip(("w_pool_up", "w_sb_up", "w_gdn_up"), sv["ys"], (POOL_W, SB_W, GDN_W))):
        dm_rows = pl.BlockSpec((tm, MERGE_TN), functools.partial(lambda i, j, kk, b: (i, b * per + kk), b=b))
        dm_cols = pl.BlockSpec((tk, MERGE_TN), functools.partial(lambda i, j, kk, b: (kk, b * per + j), b=b))
        dys.append(_mm(nm + "_dx", dm_all, gw[nm], m=t, n=wd, k=D, tm=tm, tn=256, tk=MERGE_TN, a_spec=dm_rows,
                       b_spec=_w_cols_t(MERGE_TN, 256, 512), dims=NT, out_shapes=[_sds((t, wd), F32)],
                       out_specs=[_o_plain(tm, 256)])[0])
        g[nm] = _mm(nm + "_dw", yb, dm_all, m=wd, n=D, k=t, tm=256, tn=MERGE_TN, tk=tk, a_spec=_a_trans(256, tk),
                    b_spec=dm_cols, dims=TN, out_shapes=[_sds((N_CHIPS, wd, D // N_CHIPS), BF16)],
                    out_specs=[_o_colshard(256, MERGE_TN, D // N_CHIPS)])[0]
    dys[2] = emit(("w_pool_up", "w_sb_up", "w_gdn_up", "w_out"), g, dys[2])
    proj = sv["proj"]
    dproj, g["pool_w"], g["pool_scale"] = _pool_bwd(proj, sp["pool_w"][l], sp["pool_scale"][l], dys[0], dproj)
    for k, piece in enumerate(_sb_bwd(proj, sv["kv"], dys[1])):
        dproj = _place(f"place_sb_{k}", dproj, piece, OFF_SB + k * SB_W)
    dc, dproj, g["a_log"], g["dt_bias"], g["gdn_norm"] = _gdn_bwd(
        sv["cqkv"], proj, sp["a_log"][l], sp["dt_bias"][l], sp["gdn_norm"][l], sv["sprev"], sv["inverses"], dys[2], dproj)
    dproj, g["conv"] = _conv_bwd(proj, sp["conv"][l], dc, dproj)
    du = _mm("proj_dx", dproj, w_al, m=t, n=D, k=N_AL, tm=tm, tn=1024, tk=2048, a_spec=_a_plain(tm, 2048),
             b_spec=_b_trans(2048, 1024), dims=NT, out_shapes=[_sds((t, D), F32)], out_specs=[_o_plain(tm, 1024)])[0]
    g["w_al"] = _mm("proj_dw", sv["u"], dproj, m=D, n=N_AL, k=t, tm=1024, tn=1024, tk=tk, a_spec=_a_trans(1024, tk),
                    b_spec=_b_plain(tk, 1024), dims=TN, out_shapes=[_sds((D, N_AL), BF16)], out_specs=[_o_plain(1024, 1024)])[0]
    dx, g["attn_norm"] = _rms_bwd("rms_attn_bwd", du, sv["x"], sp["attn_norm"][l], dx1)
    g["w_in"] = _w_in_to_shards(g["w_al"])
    dx = emit(("w_in",), g, dx)
    return dx, g


W_IN_RUNS = ((0, ORIG_SB, OFF_P), (ORIG_SB, ORIG_Z, OFF_SB), (ORIG_Z, ORIG_GATE, OFF_Z), (ORIG_GATE, N_IN, OFF_GATE))
W_IN_SHARD = N_IN // N_CHIPS


def _w_in_from_shards(gathered):
    parts = []
    for lo, hi, al in sorted(W_IN_RUNS, key=lambda r: r[2]):
        if al == OFF_GATE:
            parts.append(jnp.zeros((D, OFF_GATE - (OFF_AB + ORIG_GATE - ORIG_AB)), gathered.dtype))
        while lo < hi:
            chip = lo // W_IN_SHARD
            end = min(hi, (chip + 1) * W_IN_SHARD)
            parts.append(gathered[chip, :, lo - chip * W_IN_SHARD:end - chip * W_IN_SHARD])
            lo = end
    return jnp.concatenate(parts, axis=1)


def _w_in_to_shards(g_al):
    shards = []
    for chip in range(N_CHIPS):
        a, b = chip * W_IN_SHARD, (chip + 1) * W_IN_SHARD
        parts = [g_al[:, al + max(a, lo) - lo:al + min(b, hi) - lo] for lo, hi, al in W_IN_RUNS if max(a, lo) < min(b, hi)]
        shards.append(jnp.concatenate(parts, axis=1))
    return jnp.stack(shards)


def _row128(v):
    return jnp.pad(v.reshape(1, -1), ((0, 0), (0, HD - v.shape[-1])))


def _local_step(x, target, weights_of, sp, emit=None, reached=None):
    saved, gw, w_in_al = [], [], []
    h = x
    for l in range(2):
        gw_l, w_al_l = weights_of(l)
        gw.append(gw_l)
        w_in_al.append(w_al_l)
        h, sv = _layer_fwd(h, l, gw_l, w_al_l, sp, None if reached is None else functools.partial(reached, l))
        saved.append(sv)
    loss, dh, g_final = _loss_head(h, sp["final_norm"], target)
    grads = [None, None]
    for l in (1, 0):
        dh, grads[l] = _layer_bwd(dh, l, gw[l], w_in_al[l], sp, saved[l],
                                  None if emit is None else functools.partial(emit, l))
    return loss, dh, grads, g_final


ANY = pl.BlockSpec(memory_space=pl.ANY)


def _me():
    return lax.axis_index("x"), lax.axis_index("y"), lax.axis_index("c")


def _other_chips(x, y):
    return [(1 - x, y), (x, 1 - y), (1 - x, 1 - y)]


def _half(ref, axis, c, rows):
    half = rows // 2
    idx = [slice(None)] * axis + [pl.ds(pl.multiple_of(c * half, 16), half)]
    return ref.at[tuple(idx)]


def _gather_steps(out, send, recv):
    n = len(out)
    x, y, c = _me()
    mine = 2 * x + y
    sibling = (x, y, 1 - c)
    chips = _other_chips(x, y)
    sends = []
    for t in range(n):
        rows = out[t].shape[1]
        for k, (px, py) in enumerate(chips):
            own_half = _half(out[t].at[mine], 0, c, rows)
            cp = pltpu.make_async_remote_copy(
                src_ref=own_half, dst_ref=own_half,
                send_sem=send.at[6 * t + k], recv_sem=recv.at[6 * t + k], device_id=(px, py, c), device_id_type=MESH)
            cp.start()
            sends.append(cp)
    for t in range(n):
        rows = out[t].shape[1]
        for k, (px, py) in enumerate(chips):
            landed = _half(out[t].at[2 * px + py], 0, c, rows)
            pltpu.make_async_remote_copy(
                src_ref=landed, dst_ref=landed, send_sem=send.at[6 * t + k], recv_sem=recv.at[6 * t + k],
                device_id=(px, py, c), device_id_type=MESH).wait_recv()
            cp = pltpu.make_async_remote_copy(
                src_ref=landed, dst_ref=landed, send_sem=send.at[6 * t + 3 + k], recv_sem=recv.at[6 * t + 3 + k],
                device_id=sibling, device_id_type=MESH)
            cp.start()
            sends.append(cp)
    for t in range(n):
        rows = out[t].shape[1]
        for k, (px, py) in enumerate(chips):
            other = _half(out[t].at[2 * px + py], 0, 1 - c, rows)
            pltpu.make_async_remote_copy(
                src_ref=other, dst_ref=other, send_sem=send.at[6 * t + 3 + k], recv_sem=recv.at[6 * t + 3 + k],
                device_id=sibling, device_id_type=MESH).wait_recv()
    for cp in sends:
        cp.wait_send()


def _gather_weights_async(bufs, tag, collective_id):
    n = len(bufs)
    refs = [jax.new_ref(b, memory_space=pltpu.MemorySpace.HBM) for b in bufs]

    @pl.kernel(mesh=plsc.ScalarSubcoreMesh(axis_name="sequencer", num_cores=1), name=f"gather_async_{tag}",
               scratch_types=(pltpu.SemaphoreType.DMA((6 * n,)), pltpu.SemaphoreType.DMA((6 * n,))),
               compiler_params=pltpu.CompilerParams(collective_id=collective_id))
    def launch(send, recv):
        x, y, c = _me()
        barrier = pltpu.get_barrier_semaphore()
        peers = [(x, y, 1 - c)] + [(px, py, c) for px, py in _other_chips(x, y)]
        for peer in peers:
            pl.semaphore_signal(barrier, inc=1, device_id=peer, device_id_type=MESH)
        pl.semaphore_wait(barrier, len(peers))
        _gather_steps(refs, send, recv)

    launch()
    return [r[...] for r in refs]


def _rs_pair(grads):
    n = len(grads)

    def body(*refs):
        g, out = refs[:n], refs[n:2 * n]
        send, recv = refs[2 * n:]
        x, y, c = _me()
        copies = []
        for t in range(n):
            cp = pltpu.make_async_remote_copy(
                src_ref=_half(g[t], 1, 1 - c, g[t].shape[1]), dst_ref=out[t], send_sem=send.at[t], recv_sem=recv.at[t],
                device_id=(x, y, 1 - c), device_id_type=MESH)
            cp.start()
            copies.append(cp)
        for cp in copies:
            cp.wait()

    return pl.pallas_call(
        body, name="rs_pair", in_specs=[ANY] * n, out_specs=[ANY] * n,
        out_shape=[_sds((N_CHIPS, s.shape[1] // 2, s.shape[2]), s.dtype) for s in grads],
        scratch_shapes=[pltpu.SemaphoreType.DMA((n,)), pltpu.SemaphoreType.DMA((n,))],
    )(*grads)


def _rs_chips_steps(p, out, send, recv):
    x, y, c = _me()
    copies = []
    for t in range(len(p)):
        for k, (px, py) in enumerate(_other_chips(x, y)):
            cp = pltpu.make_async_remote_copy(
                src_ref=p[t].at[2 * px + py], dst_ref=out[t].at[k], send_sem=send.at[3 * t + k],
                recv_sem=recv.at[3 * t + k], device_id=(px, py, c), device_id_type=MESH)
            cp.start()
            copies.append(cp)
    for cp in copies:
        cp.wait()


def _rs_chips_async(parts, tag, collective_id):
    n = len(parts)
    src = [jax.new_ref(p, memory_space=pltpu.MemorySpace.HBM) for p in parts]
    got = [jax.empty_ref(_sds((3, *p.shape[1:]), p.dtype), memory_space=pltpu.MemorySpace.HBM) for p in parts]

    @pl.kernel(mesh=plsc.ScalarSubcoreMesh(axis_name="sequencer", num_cores=1), name=f"rs_chips_async_{tag}",
               scratch_types=(pltpu.SemaphoreType.DMA((3 * n,)), pltpu.SemaphoreType.DMA((3 * n,))),
               compiler_params=pltpu.CompilerParams(collective_id=collective_id))
    def launch(send, recv):
        x, y, c = _me()
        barrier = pltpu.get_barrier_semaphore()
        peers = [(px, py, c) for px, py in _other_chips(x, y)]
        for peer in peers:
            pl.semaphore_signal(barrier, inc=1, device_id=peer, device_id_type=MESH)
        pl.semaphore_wait(barrier, len(peers))
        _rs_chips_steps(src, got, send, recv)

    launch()
    return [g[...] for g in got]


def _pair_exchange(bufs):
    n = len(bufs)

    def body(*refs):
        out = refs[n:2 * n]
        send, recv = refs[2 * n:]
        x, y, c = _me()
        copies = []
        for t in range(n):
            cp = pltpu.make_async_remote_copy(
                src_ref=out[t].at[c], dst_ref=out[t].at[c], send_sem=send.at[t], recv_sem=recv.at[t],
                device_id=(x, y, 1 - c), device_id_type=MESH)
            cp.start()
            copies.append(cp)
        for t, cp in enumerate(copies):
            cp.wait_send()
            pltpu.make_async_remote_copy(
                src_ref=out[t].at[1 - c], dst_ref=out[t].at[1 - c], send_sem=send.at[t], recv_sem=recv.at[t],
                device_id=(x, y, 1 - c), device_id_type=MESH).wait_recv()

    return pl.pallas_call(
        body, name="pair_exchange", in_specs=[ANY] * n, out_specs=[ANY] * n,
        out_shape=[_sds(s.shape, s.dtype) for s in bufs], input_output_aliases={t: t for t in range(n)},
        scratch_shapes=[pltpu.SemaphoreType.DMA((n,)), pltpu.SemaphoreType.DMA((n,))],
    )(*bufs)


def _row_tile(rows, cols, itemsize, budget=2 * 1024 * 1024):
    tr = rows
    while tr * cols * itemsize > budget and tr % 32 == 0:
        tr //= 2
    return tr


def _sum_pair(name, g, got, where):
    nchip, rows, cols = g.shape
    half = rows // 2
    tr = _row_tile(half, cols, 4)
    per = half // tr

    def body(w_ref, g_ref, r_ref, o_ref):
        o_ref[...] = (g_ref[...].astype(F32) + r_ref[...].astype(F32)).astype(o_ref.dtype)

    blk = pl.BlockSpec((None, tr, cols), lambda j, i, w_ref: (j, i, 0))
    return pl.pallas_call(
        body, name=name,
        grid_spec=pltpu.PrefetchScalarGridSpec(
            num_scalar_prefetch=1, grid=(nchip, per),
            in_specs=[pl.BlockSpec((None, tr, cols), lambda j, i, w_ref: (j, w_ref[1] * per + i, 0)), blk], out_specs=blk),
        out_shape=_sds((nchip, half, cols), BF16), compiler_params=_cp("parallel", "parallel"),
    )(where, g, got)


def _sum_chips(name, p, got, where):
    _, rows, cols = p.shape
    tr = _row_tile(rows, cols, 4)

    def body(w_ref, p_ref, r0, r1, r2, o_ref):
        o_ref[...] = ((p_ref[...].astype(F32) + r0[...].astype(F32)) + r1[...].astype(F32)) + r2[...].astype(F32)

    def got_k(k):
        return pl.BlockSpec((None, tr, cols), lambda i, w_ref: (k, i, 0))

    return pl.pallas_call(
        body, name=name,
        grid_spec=pltpu.PrefetchScalarGridSpec(
            num_scalar_prefetch=1, grid=(rows // tr,),
            in_specs=[pl.BlockSpec((None, tr, cols), lambda i, w_ref: (w_ref[0], i, 0)), got_k(0), got_k(1), got_k(2)],
            out_specs=pl.BlockSpec((None, tr, cols), lambda i, w_ref: (w_ref[1], i, 0))),
        out_shape=_sds((2, rows, cols), F32), compiler_params=_cp("parallel"),
    )(where, p, got, got, got)


def _rs_begin(grads, where, tag, collective_id):
    got = _rs_pair(grads)
    parts = [_sum_pair(f"sum_pair_{t}", g, r, where) for t, (g, r) in enumerate(zip(grads, got))]
    return parts, _rs_chips_async(parts, tag, collective_id)


def _rs_finish(parts, got, where):
    halves = [_sum_chips(f"sum_chips_{t}", p, r, where) for t, (p, r) in enumerate(zip(parts, got))]
    return _pair_exchange(halves)


def _all_reduce_small(name, v):
    rows = v.shape[0]

    def body(v_ref, o_ref, land, send, recv):
        x, y, c = _me()
        mine = 4 * x + 2 * y + c
        copies = []
        for k in range(1, 8):
            kx, ky, kc = k >> 2, (k >> 1) & 1, k & 1
            peer = (x ^ kx, y ^ ky, c ^ kc)
            cp = pltpu.make_async_remote_copy(
                src_ref=v_ref, dst_ref=land.at[mine], send_sem=send.at[k - 1], recv_sem=recv.at[k - 1],
                device_id=peer, device_id_type=MESH)
            cp.start()
            copies.append(cp)
        land[mine] = v_ref[...]
        for k in range(1, 8):
            kx, ky, kc = k >> 2, (k >> 1) & 1, k & 1
            src = 4 * (x ^ kx) + 2 * (y ^ ky) + (c ^ kc)
            pltpu.make_async_remote_copy(
                src_ref=v_ref, dst_ref=land.at[src], send_sem=send.at[k - 1], recv_sem=recv.at[k - 1],
                device_id=(x ^ kx, y ^ ky, c ^ kc), device_id_type=MESH).wait_recv()
        acc = land[0]
        for d in range(1, 8):
            acc = acc + land[d]
        o_ref[...] = acc
        for cp in copies:
            cp.wait_send()

    vm = pl.BlockSpec(memory_space=pltpu.VMEM)
    return pl.pallas_call(
        body, name=name, in_specs=[vm], out_specs=vm, out_shape=_sds((rows, 128), F32),
        scratch_shapes=[pltpu.VMEM((8, rows, 128), F32), pltpu.SemaphoreType.DMA((7,)), pltpu.SemaphoreType.DMA((7,))],
    )(v)


def _adamw(name, w, g, m, v):
    rows, cols = w.shape
    tr = _row_tile(rows, cols, 4, budget=1024 * 1024)
    c1 = 1.0 / (1.0 - ADAM_B1 ** ADAM_STEP)
    c2 = 1.0 / (1.0 - ADAM_B2 ** ADAM_STEP)

    def body(w_ref, g_ref, m_ref, v_ref, d_ref, nm_ref, nv_ref):
        gv = g_ref[...]
        nm = ADAM_B1 * m_ref[...] + (1.0 - ADAM_B1) * gv
        nv = ADAM_B2 * v_ref[...] + (1.0 - ADAM_B2) * (gv * gv)
        d_ref[...] = -ADAM_LR * ((nm * c1) / (jnp.sqrt(nv * c2) + ADAM_EPS) + ADAM_WD * w_ref[...])
        nm_ref[...] = nm
        nv_ref[...] = nv

    blk = pl.BlockSpec((tr, cols), lambda i: (i, 0))
    return pl.pallas_call(
        body, name=name, grid=(rows // tr,), in_specs=[blk] * 4, out_specs=[blk] * 3,
        out_shape=[_sds((rows, cols), F32)] * 3, compiler_params=_cp("parallel"),
    )(w, g, m, v)


def _adamw_layers(name, w, g0, g1, m, v):
    _, half, cols = g0.shape
    tr = _row_tile(half, cols, 4)
    per_half = half // tr
    per = 2 * per_half
    c1 = 1.0 / (1.0 - ADAM_B1 ** ADAM_STEP)
    c2 = 1.0 / (1.0 - ADAM_B2 ** ADAM_STEP)

    def body(w_ref, g0_ref, g1_ref, m_ref, v_ref, g_ref, d_ref, nm_ref, nv_ref):
        gv = jnp.where(pl.program_id(0) == 0, g0_ref[...], g1_ref[...])
        nm = ADAM_B1 * m_ref[...] + (1.0 - ADAM_B1) * gv
        nv = ADAM_B2 * v_ref[...] + (1.0 - ADAM_B2) * (gv * gv)
        g_ref[...] = gv
        d_ref[...] = -ADAM_LR * ((nm * c1) / (jnp.sqrt(nv * c2) + ADAM_EPS) + ADAM_WD * w_ref[...])
        nm_ref[...] = nm
        nv_ref[...] = nv

    both = pl.BlockSpec((None, tr, cols), lambda l, i: (l, i, 0))

    def halves(i):
        return i // per_half, i % per_half, 0

    first = pl.BlockSpec((None, tr, cols), lambda l, i: halves(i * (1 - l) + (per - 1) * l))
    second = pl.BlockSpec((None, tr, cols), lambda l, i: halves(i * l))
    return pl.pallas_call(
        body, name=name, grid=(2, per), in_specs=[both, first, second, both, both], out_specs=[both] * 4,
        out_shape=[_sds(w.shape, F32)] * 4, compiler_params=_cp("arbitrary", "arbitrary"),
    )(w, g0, g1, m, v)


def _to_bf16_slot(name, w, l, where):
    _, rows, cols = w.shape
    tr = _row_tile(rows, cols, 4)

    def body(w_ref, x_ref, o_ref):
        o_ref[...] = x_ref[...].astype(BF16)

    return pl.pallas_call(
        body, name=name,
        grid_spec=pltpu.PrefetchScalarGridSpec(
            num_scalar_prefetch=1, grid=(rows // tr,), in_specs=[pl.BlockSpec((None, tr, cols), lambda i, w_ref: (l, i, 0))],
            out_specs=pl.BlockSpec((None, tr, cols), lambda i, w_ref: (w_ref[0], i, 0))),
        out_shape=_sds((N_CHIPS, rows, cols), BF16), compiler_params=_cp("parallel"))(where, w)


BIG = ("w_in", "w_pool_up", "w_sb_up", "w_gdn_up", "w_out", "w_ff1", "w_ff2")
SMALL = (("attn_norm", (D,)), ("pool_w", (4, 128, 128)), ("pool_scale", (POOL_W,)), ("gdn_a_log", (HEADS,)),
         ("gdn_dt_bias", (HEADS,)), ("gdn_norm", (HD,)), ("mlp_norm", (D,)))


PACK_TILE = 8 * 128


def _rows128(a):
    flat = a.reshape(-1)
    pad = (-flat.shape[0]) % PACK_TILE
    return jnp.pad(flat, (0, pad)).reshape(-1, 128)


def _pack(parts):
    packed = jnp.concatenate([_rows128(p) for p in parts], axis=0)
    return jnp.pad(packed, ((0, (-packed.shape[0]) % 8), (0, 0)))


def _unpack(packed, shapes):
    out, r = [], 0
    for shp in shapes:
        size = 1
        for s in shp:
            size *= s
        nr = -(-size // PACK_TILE) * 8
        out.append(packed[r:r + nr].reshape(-1)[:size].reshape(shp))
        r += nr
    return out


def kernel(x, attn_norm, w_in, pool_w, pool_scale, gdn_conv, gdn_a_log, gdn_dt_bias, gdn_norm, w_pool_up, w_sb_up, w_gdn_up, w_out, mlp_norm, w_ff1, w_ff2, final_norm, loss_target, m_attn_norm, m_w_in, m_pool_w, m_pool_scale, m_gdn_conv, m_gdn_a_log, m_gdn_dt_bias, m_gdn_norm, m_w_pool_up, m_w_sb_up, m_w_gdn_up, m_w_out, m_mlp_norm, m_w_ff1, m_w_ff2, m_final_norm, v_attn_norm, v_w_in, v_pool_w, v_pool_scale, v_gdn_conv, v_gdn_a_log, v_gdn_dt_bias, v_gdn_norm, v_w_pool_up, v_w_sb_up, v_w_gdn_up, v_w_out, v_mlp_norm, v_w_ff1, v_w_ff2, v_final_norm):
    weights = dict(attn_norm=attn_norm, w_in=w_in, pool_w=pool_w, pool_scale=pool_scale, gdn_conv=gdn_conv,
                   gdn_a_log=gdn_a_log, gdn_dt_bias=gdn_dt_bias, gdn_norm=gdn_norm, w_pool_up=w_pool_up, w_sb_up=w_sb_up,
                   w_gdn_up=w_gdn_up, w_out=w_out, mlp_norm=mlp_norm, w_ff1=w_ff1, w_ff2=w_ff2, final_norm=final_norm)
    mom1 = dict(attn_norm=m_attn_norm, w_in=m_w_in, pool_w=m_pool_w, pool_scale=m_pool_scale, gdn_conv=m_gdn_conv,
                gdn_a_log=m_gdn_a_log, gdn_dt_bias=m_gdn_dt_bias, gdn_norm=m_gdn_norm, w_pool_up=m_w_pool_up,
                w_sb_up=m_w_sb_up, w_gdn_up=m_w_gdn_up, w_out=m_w_out, mlp_norm=m_mlp_norm, w_ff1=m_w_ff1, w_ff2=m_w_ff2,
                final_norm=m_final_norm)
    mom2 = dict(attn_norm=v_attn_norm, w_in=v_w_in, pool_w=v_pool_w, pool_scale=v_pool_scale, gdn_conv=v_gdn_conv,
                gdn_a_log=v_gdn_a_log, gdn_dt_bias=v_gdn_dt_bias, gdn_norm=v_gdn_norm, w_pool_up=v_w_pool_up,
                w_sb_up=v_w_sb_up, w_gdn_up=v_w_gdn_up, w_out=v_w_out, mlp_norm=v_mlp_norm, w_ff1=v_w_ff1, w_ff2=v_w_ff2,
                final_norm=v_final_norm)
    xi, yi, ci = lax.axis_index("x"), lax.axis_index("y"), lax.axis_index("c")
    chip = 2 * xi + yi
    where = jnp.stack([chip, ci]).astype(jnp.int32)

    bufs = [[_to_bf16_slot(f"cast_{nm}_{l}", weights[nm], l, where) for nm in BIG] for l in range(2)]
    first = _gather_weights_async(bufs[0][:1], "0_w_in", 4)
    first, _ = lax.optimization_barrier((first, (bufs[0][1:], bufs[1])))
    rest, _ = lax.optimization_barrier((bufs[0][1:], first))
    gw = [dict(zip(BIG, list(first) + _gather_weights_async(rest, "0_rest", 1))), {}]

    def reached(l, stage, value):
        if l == 0 and stage == "proj":
            later, _ = lax.optimization_barrier((bufs[1][:1], value))
            gw[1]["w_in"] = _gather_weights_async(later, "1_w_in", 2)[0]
        if l == 0 and stage == "merged":
            later, _ = lax.optimization_barrier((bufs[1][1:], value))
            gw[1].update(zip(BIG[1:], _gather_weights_async(later, "1_rest", 3)))

    def weights_of(l):
        return gw[l], _w_in_from_shards(gw[l]["w_in"])

    conv_cols = gdn_conv.shape[-1]
    conv_place = lax.dynamic_update_slice(jnp.zeros((2, GDN_CONV, N_CHIPS * conv_cols), F32),
                                          jnp.where(ci == 0, gdn_conv, 0.0), (0, 0, chip * conv_cols))
    conv_full = _all_reduce_small("gather_conv", _rows128(conv_place)).reshape(2, GDN_CONV, N_CHIPS * conv_cols)
    sp = dict(attn_norm=attn_norm.reshape(2, 1, D), pool_w=pool_w, pool_scale=pool_scale.reshape(2, 1, POOL_W),
              conv=conv_full, a_log=jnp.stack([_row128(gdn_a_log[l]) for l in range(2)]),
              dt_bias=jnp.stack([_row128(gdn_dt_bias[l]) for l in range(2)]), gdn_norm=gdn_norm.reshape(2, 1, HD),
              mlp_norm=mlp_norm.reshape(2, 1, D), final_norm=final_norm.reshape(1, D))

    started = []

    def emit(l, names, g, v):
        parts, got = _rs_begin([g[nm] for nm in names], where, f"{l}_{names[0]}", 5 + len(started))
        started.append((l, names, parts, got))
        v, _ = lax.optimization_barrier((v, parts))
        return v

    loss, grad_x, grads, g_final = _local_step(x[0], loss_target[0], weights_of, sp, emit, reached)
    big_grads = {nm: [None, None] for nm in BIG}
    for l, names, parts, got in started[:-1]:
        got, _ = lax.optimization_barrier((got, grad_x))
        for nm, red in zip(names, _rs_finish(parts, got, where)):
            big_grads[nm][l] = red
    small_parts, small_shapes = [], []
    for l in range(2):
        g = grads[l]
        for nm, shp in SMALL:
            key = {"gdn_a_log": "a_log", "gdn_dt_bias": "dt_bias"}.get(nm, nm)
            val = g[key]
            small_parts.append(val[0, :HEADS] if nm in ("gdn_a_log", "gdn_dt_bias") else val)
            small_shapes.append(shp)
        small_parts.append(g["conv"])
        small_shapes.append((GDN_CONV, N_CHIPS * conv_cols))
    small_parts += [g_final, loss]
    small_shapes += [(D,), (1, 1)]
    small_pack = _pack(small_parts)

    grad, delta, new_m, new_v = {}, {}, {}, {}
    for nm in BIG[1:]:
        grad[nm], delta[nm], new_m[nm], new_v[nm] = _adamw_layers("adamw_" + nm, weights[nm], *big_grads[nm], mom1[nm], mom2[nm])
    l, names, parts, got = started[-1]
    got, _ = lax.optimization_barrier((got, [new_v[nm] for nm in BIG[1:]]))
    for nm, red in zip(names, _rs_finish(parts, got, where)):
        big_grads[nm][l] = red
    nm = BIG[0]
    grad[nm], delta[nm], new_m[nm], new_v[nm] = _adamw_layers("adamw_" + nm, weights[nm], *big_grads[nm], mom1[nm], mom2[nm])
    small_pack, _ = lax.optimization_barrier((small_pack, new_v[nm]))
    reduced = _unpack(_all_reduce_small("reduce_small", small_pack), small_shapes)
    per = len(SMALL) + 1
    for i, (nm, _) in enumerate(SMALL):
        grad[nm] = jnp.stack([reduced[i], reduced[per + i]])
    conv_g = jnp.stack([reduced[per - 1], reduced[2 * per - 1]])
    grad["gdn_conv"] = lax.dynamic_slice(conv_g, (0, 0, chip * conv_cols), (2, GDN_CONV, conv_cols))
    grad["final_norm"] = reduced[-2]
    loss = reduced[-1][0, 0]
    small_names = [nm for nm, _ in SMALL] + ["gdn_conv", "final_norm"]
    packs = [_pack([src[nm] for nm in small_names]) for src in (weights, grad, mom1, mom2)]
    outs = _adamw("adamw_small", *packs)
    shapes = [weights[nm].shape for nm in small_names]
    for dst, packed in zip((delta, new_m, new_v), outs):
        for nm, val in zip(small_names, _unpack(packed, shapes)):
            dst[nm] = val

    order = ("attn_norm", "w_in", "pool_w", "pool_scale", "gdn_conv", "gdn_a_log", "gdn_dt_bias", "gdn_norm", "w_pool_up",
             "w_sb_up", "w_gdn_up", "w_out", "mlp_norm", "w_ff1", "w_ff2", "final_norm")
    return (loss, grad_x[None], *[grad[n] for n in order], *[delta[n] for n in order], *[new_m[n] for n in order],
            *[new_v[n] for n in order])
```
